```python
import jax, jax.numpy as jnp
from jax import lax
import numpy as np

D_MODEL = 1024
BATCH = 8
SEQ = 4096
DEPTH = 1

GRID_W = 64
CTX_LEN = 256
N_HEADS = 8
QK_NOPE_DIM = 64
QK_ROPE_DIM = 32
V_HEAD_DIM = 64
QK_DIM = QK_NOPE_DIM + QK_ROPE_DIM
Q_LORA_RANK = 384
KV_LORA_RANK = 256
ROPE_BASE = 10000.0
Q_BLOCK = 128
LRU_WIDTH = 1280
LRU_BLOCKS = 10
LRU_BLOCK_W = LRU_WIDTH // LRU_BLOCKS
LRU_CONV_W = 4
LRU_C = 8.0
FFN_DIM = 2816
FFN_CONV_W = 3
EPS = 1e-6
OFF_KV = Q_LORA_RANK
OFF_KR = OFF_KV + KV_LORA_RANK
OFF_XB = OFF_KR + QK_ROPE_DIM
OFF_YB = OFF_XB + LRU_WIDTH
OFF_G = OFF_YB + LRU_WIDTH
IN_DIM = OFF_G + 2 * D_MODEL

kernel_name = "hybrid_mla_rglru_convffn_dit_block"


def rms_norm(x, g):
    xf = x.astype(jnp.float32)
    y = xf * lax.rsqrt(jnp.mean(xf * xf, axis=-1, keepdims=True) + EPS)
    return (y * g.astype(jnp.float32)).astype(x.dtype)


def modulate(h, shift, scale):
    return h * (1 + scale) + shift


def dwconv(x, w, b, left, right):
    t = x.shape[1]
    xp = jnp.pad(x, ((0, 0), (left, right), (0, 0)))
    out = b
    for k in range(w.shape[0]):
        out = out + xp[:, k:k + t] * w[k]
    return out


def axial_rope_tables(n):
    rows = n // GRID_W
    row_ids = jnp.repeat(jnp.arange(rows), GRID_W).astype(jnp.float32)
    col_ids = jnp.tile(jnp.arange(GRID_W), rows).astype(jnp.float32)
    axis_dim = QK_ROPE_DIM // 2
    inv = 1.0 / (ROPE_BASE ** (jnp.arange(0, axis_dim, 2, dtype=jnp.float32) / axis_dim))
    ang = jnp.concatenate([row_ids[:, None] * inv, col_ids[:, None] * inv], axis=-1)
    return jnp.cos(ang), jnp.sin(ang)


def apply_rope(x, cos, sin):
    half = QK_ROPE_DIM // 2
    cos = cos.astype(x.dtype)
    sin = sin.astype(x.dtype)
    x1, x2 = x[..., :half], x[..., half:]
    return jnp.concatenate([x1 * cos - x2 * sin, x2 * cos + x1 * sin], axis=-1)


def sdpa(q, k, v):
    s = jnp.einsum('bqhd,bkhd->bhqk', q, k).astype(jnp.float32) * (QK_DIM ** -0.5)
    p = jax.nn.softmax(s, axis=-1).astype(v.dtype)
    return jnp.einsum('bhqk,bkhd->bqhd', p, v)


def attend_blocks(q, k, v):
    b, s, h, dq = q.shape
    nb = s // Q_BLOCK
    qb = q.reshape(b, nb, Q_BLOCK, h, dq).transpose(1, 0, 2, 3, 4)
    ob = lax.map(lambda qq: sdpa(qq, k, v), qb)
    return ob.transpose(1, 0, 2, 3, 4).reshape(b, s, h * V_HEAD_DIM)


def rglru(x, w_a, b_a, w_x, b_x, lam, h0, reverse):
    b, t, w = x.shape
    xb = x.reshape(b, t, LRU_BLOCKS, LRU_BLOCK_W)
    r = jax.nn.sigmoid(jnp.einsum('btnd,nde->btne', xb, w_a).reshape(b, t, w) + b_a)
    i = jax.nn.sigmoid(jnp.einsum('btnd,nde->btne', xb, w_x).reshape(b, t, w) + b_x)
    log_a = -LRU_C * r.astype(jnp.float32) * jax.nn.softplus(-lam.astype(jnp.float32))
    a = jnp.exp(log_a)
    mult = jnp.sqrt(-jnp.expm1(2.0 * log_a))
    u = mult * (i * x).astype(jnp.float32)
    if reverse:
        u = u.at[:, -1].add(a[:, -1] * h0)
    else:
        u = u.at[:, 0].add(a[:, 0] * h0)

    def combine(e1, e2):
        a1, b1 = e1
        a2, b2 = e2
        return a1 * a2, a2 * b1 + b2

    _, h = lax.associative_scan(combine, (a, u), reverse=reverse, axis=1)
    return h


def lru_bidir(xc, lp, h0f, h0b):
    hf = rglru(xc, lp['lru_w_a'][0], lp['lru_b_a'][0], lp['lru_w_x'][0], lp['lru_b_x'][0],
               lp['lru_lambda'][0], h0f, reverse=False)
    hb = rglru(xc, lp['lru_w_a'][1], lp['lru_b_a'][1], lp['lru_w_x'][1], lp['lru_b_x'][1],
               lp['lru_lambda'][1], h0b, reverse=True)
    return hf, hb


def mixer_inputs(h, lp, cos, sin):
    b, t, _ = h.shape
    z = h @ lp['w_in']
    q_lat, kv_lat, k_rope, xb, yb, gl = jnp.split(z, (OFF_KV, OFF_KR, OFF_XB, OFF_YB, OFF_G), axis=-1)
    q = (rms_norm(q_lat, lp['q_norm_g']) @ lp['w_uq']).reshape(b, t, N_HEADS, QK_DIM)
    kv = (rms_norm(kv_lat, lp['kv_norm_g']) @ lp['w_ukv']).reshape(b, t, N_HEADS, QK_NOPE_DIM + V_HEAD_DIM)
    k_nope, v = kv[..., :QK_NOPE_DIM], kv[..., QK_NOPE_DIM:]
    if cos is not None:
        q = jnp.concatenate([q[..., :QK_NOPE_DIM], apply_rope(q[..., QK_NOPE_DIM:], cos[:, None, :], sin[:, None, :])], axis=-1)
        k_rope = apply_rope(k_rope, cos, sin)
    k = jnp.concatenate([k_nope, jnp.broadcast_to(k_rope[:, :, None, :], (b, t, N_HEADS, QK_ROPE_DIM))], axis=-1)
    xc = dwconv(xb, lp['lru_conv_w'], lp['lru_conv_b'], LRU_CONV_W // 2, LRU_CONV_W - 1 - LRU_CONV_W // 2)
    return q, k, v, xc, yb, gl


def merge_out(attn, hf, hb, yb, gl, lp):
    y_a = attn @ lp['w_o_attn']
    y_b = (((hf + hb).astype(yb.dtype)) * jax.nn.gelu(yb)) @ lp['w_o_lru']
    g_a, g_b = jnp.split(jax.nn.sigmoid(gl + lp['b_gate']), 2, axis=-1)
    return (g_a * y_a + g_b * y_b) @ lp['w_out']


def conv_ffn(h, lp):
    u = h @ lp['w_up']
    a, g = jnp.split(u, 2, axis=-1)
    a = dwconv(a, lp['ffn_conv_w'], lp['ffn_conv_b'], FFN_CONV_W // 2, FFN_CONV_W // 2)
    return (jax.nn.silu(a) * g) @ lp['w_down']


def _fwd_setup_inputs(seed: int = 0) -> dict:
    key = jax.random.key(seed)
    ks = jax.random.split(key, 32)
    f32 = jnp.float32

    def nrm(k, shape, fan_in):
        return jax.random.normal(k, shape, f32) * (fan_in ** -0.5)

    def gain(k, shape):
        return 1.0 + 0.05 * jax.random.normal(k, shape, f32)

    def bias(k, shape):
        return 0.02 * jax.random.normal(k, shape, f32)

    a0 = jax.random.uniform(ks[20], (DEPTH, 2, LRU_WIDTH), f32, 0.9, 0.999)
    return {
        'x': jax.random.normal(ks[0], (BATCH, SEQ, D_MODEL), f32),
        'c': jax.random.normal(ks[1], (BATCH, D_MODEL), f32),
        'ctx': jax.random.normal(ks[2], (BATCH, CTX_LEN, D_MODEL), f32),
        'c_ctx': jax.random.normal(ks[3], (D_MODEL,), f32),
        'w_mod': nrm(ks[4], (DEPTH, D_MODEL, 6 * D_MODEL), D_MODEL),
        'b_mod': bias(ks[5], (DEPTH, 6 * D_MODEL)),
        'norm1_g': gain(ks[6], (DEPTH, D_MODEL)),
        'w_in': nrm(ks[7], (DEPTH, D_MODEL, IN_DIM), D_MODEL),
        'b_gate': bias(ks[8], (DEPTH, 2 * D_MODEL)),
        'q_norm_g': gain(ks[9], (DEPTH, Q_LORA_RANK)),
        'kv_norm_g': gain(ks[10], (DEPTH, KV_LORA_RANK)),
        'w_uq': nrm(ks[11], (DEPTH, Q_LORA_RANK, N_HEADS * QK_DIM), Q_LORA_RANK),
        'w_ukv': nrm(ks[12], (DEPTH, KV_LORA_RANK, N_HEADS * (QK_NOPE_DIM + V_HEAD_DIM)), KV_LORA_RANK),
        'w_o_attn': nrm(ks[13], (DEPTH, N_HEADS * V_HEAD_DIM, D_MODEL), N_HEADS * V_HEAD_DIM),
        'lru_conv_w': nrm(ks[14], (DEPTH, LRU_CONV_W, LRU_WIDTH), LRU_CONV_W),
        'lru_conv_b': bias(ks[15], (DEPTH, LRU_WIDTH)),
        'lru_w_a': nrm(ks[16], (DEPTH, 2, LRU_BLOCKS, LRU_BLOCK_W, LRU_BLOCK_W), LRU_BLOCK_W),
        'lru_b_a': bias(ks[17], (DEPTH, 2, LRU_WIDTH)),
        'lru_w_x': nrm(ks[18], (DEPTH, 2, LRU_BLOCKS, LRU_BLOCK_W, LRU_BLOCK_W), LRU_BLOCK_W),
        'lru_b_x': bias(ks[19], (DEPTH, 2, LRU_WIDTH)),
        'lru_lambda': jnp.log(a0 / (1.0 - a0)),
        'w_o_lru': nrm(ks[21], (DEPTH, LRU_WIDTH, D_MODEL), LRU_WIDTH),
        'w_out': nrm(ks[22], (DEPTH, D_MODEL, D_MODEL), D_MODEL),
        'norm2_g': gain(ks[23], (DEPTH, D_MODEL)),
        'w_up': nrm(ks[24], (DEPTH, D_MODEL, 2 * FFN_DIM), D_MODEL),
        'ffn_conv_w': nrm(ks[25], (DEPTH, FFN_CONV_W, FFN_DIM), FFN_CONV_W),
        'ffn_conv_b': bias(ks[26], (DEPTH, FFN_DIM)),
        'w_down': nrm(ks[27], (DEPTH, FFN_DIM, D_MODEL), FFN_DIM),
        'final_g': gain(ks[28], (D_MODEL,)),
    }


def _fwd_reference(x, c, ctx, c_ctx, w_mod, b_mod, norm1_g, w_in, b_gate, q_norm_g, kv_norm_g,
              w_uq, w_ukv, w_o_attn, lru_conv_w, lru_conv_b, lru_w_a, lru_b_a, lru_w_x,
              lru_b_x, lru_lambda, w_o_lru, w_out, norm2_g, w_up, ffn_conv_w, ffn_conv_b,
              w_down, final_g):
    b, s, _ = x.shape
    cos, sin = axial_rope_tables(s)
    h_zero = jnp.zeros((b, LRU_WIDTH), jnp.float32)
    for i in range(DEPTH):
        lp = {
            'w_in': w_in[i], 'b_gate': b_gate[i], 'q_norm_g': q_norm_g[i], 'kv_norm_g': kv_norm_g[i],
            'w_uq': w_uq[i], 'w_ukv': w_ukv[i], 'w_o_attn': w_o_attn[i],
            'lru_conv_w': lru_conv_w[i], 'lru_conv_b': lru_conv_b[i],
            'lru_w_a': lru_w_a[i], 'lru_b_a': lru_b_a[i], 'lru_w_x': lru_w_x[i], 'lru_b_x': lru_b_x[i],
            'lru_lambda': lru_lambda[i], 'w_o_lru': w_o_lru[i], 'w_out': w_out[i],
            'w_up': w_up[i], 'ffn_conv_w': ffn_conv_w[i], 'ffn_conv_b': ffn_conv_b[i], 'w_down': w_down[i],
        }
        mod_l = (jax.nn.silu(c) @ w_mod[i] + b_mod[i])[:, None, :]
        mod_c = jax.nn.silu(c_ctx) @ w_mod[i] + b_mod[i]
        sh1_l, sc1_l, g1_l, sh2_l, sc2_l, g2_l = jnp.split(mod_l, 6, axis=-1)
        sh1_c, sc1_c, g1_c, sh2_c, sc2_c, g2_c = jnp.split(mod_c, 6, axis=-1)

        hc = modulate(rms_norm(ctx, norm1_g[i]), sh1_c, sc1_c)
        q_c, k_c, v_c, xc_c, yb_c, gl_c = mixer_inputs(hc, lp, None, None)
        hf_c, hb_c = lru_bidir(xc_c, lp, h_zero, h_zero)
        state_f, state_b = hf_c[:, -1], hb_c[:, 0]

        hl = modulate(rms_norm(x, norm1_g[i]), sh1_l, sc1_l)
        q_l, k_l, v_l, xc_l, yb_l, gl_l = mixer_inputs(hl, lp, cos, sin)
        attn_l = attend_blocks(q_l, jnp.concatenate([k_l, k_c], axis=1), jnp.concatenate([v_l, v_c], axis=1))
        hf_l, hb_l = lru_bidir(xc_l, lp, state_f, state_b)
        x = x + g1_l * merge_out(attn_l, hf_l, hb_l, yb_l, gl_l, lp)
        x = x + g2_l * conv_ffn(modulate(rms_norm(x, norm2_g[i]), sh2_l, sc2_l), lp)

        if i < DEPTH - 1:
            attn_c = sdpa(q_c, k_c, v_c).reshape(b, ctx.shape[1], N_HEADS * V_HEAD_DIM)
            ctx = ctx + g1_c * merge_out(attn_c, hf_c, hb_c, yb_c, gl_c, lp)
            ctx = ctx + g2_c * conv_ffn(modulate(rms_norm(ctx, norm2_g[i]), sh2_c, sc2_c), lp)
    return rms_norm(x, final_g)


import jax as _jax
import jax.numpy as _jnp

TWIN_FORMAT = 'train_step'
FWD_PARAMS = ['x', 'c', 'ctx', 'c_ctx', 'w_mod', 'b_mod', 'norm1_g', 'w_in', 'b_gate', 'q_norm_g', 'kv_norm_g', 'w_uq', 'w_ukv', 'w_o_attn', 'lru_conv_w', 'lru_conv_b', 'lru_w_a', 'lru_b_a', 'lru_w_x', 'lru_b_x', 'lru_lambda', 'w_o_lru', 'w_out', 'norm2_g', 'w_up', 'ffn_conv_w', 'ffn_conv_b', 'w_down', 'final_g']
TWIN_WEIGHTS = ['c_ctx', 'w_mod', 'b_mod', 'norm1_g', 'w_in', 'b_gate', 'q_norm_g', 'kv_norm_g', 'w_uq', 'w_ukv', 'w_o_attn', 'lru_conv_w', 'lru_conv_b', 'lru_w_a', 'lru_b_a', 'lru_w_x', 'lru_b_x', 'lru_lambda', 'w_o_lru', 'w_out', 'norm2_g', 'w_up', 'ffn_conv_w', 'ffn_conv_b', 'w_down', 'final_g']
TWIN_DIFF_INPUT = 'x'
TWIN_INPUTS = ['x', 'c', 'ctx', 'c_ctx', 'w_mod', 'b_mod', 'norm1_g', 'w_in', 'b_gate', 'q_norm_g', 'kv_norm_g', 'w_uq', 'w_ukv', 'w_o_attn', 'lru_conv_w', 'lru_conv_b', 'lru_w_a', 'lru_b_a', 'lru_w_x', 'lru_b_x', 'lru_lambda', 'w_o_lru', 'w_out', 'norm2_g', 'w_up', 'ffn_conv_w', 'ffn_conv_b', 'w_down', 'final_g', 'loss_target', 'm_c_ctx', 'm_w_mod', 'm_b_mod', 'm_norm1_g', 'm_w_in', 'm_b_gate', 'm_q_norm_g', 'm_kv_norm_g', 'm_w_uq', 'm_w_ukv', 'm_w_o_attn', 'm_lru_conv_w', 'm_lru_conv_b', 'm_lru_w_a', 'm_lru_b_a', 'm_lru_w_x', 'm_lru_b_x', 'm_lru_lambda', 'm_w_o_lru', 'm_w_out', 'm_norm2_g', 'm_w_up', 'm_ffn_conv_w', 'm_ffn_conv_b', 'm_w_down', 'm_final_g', 'v_c_ctx', 'v_w_mod', 'v_b_mod', 'v_norm1_g', 'v_w_in', 'v_b_gate', 'v_q_norm_g', 'v_kv_norm_g', 'v_w_uq', 'v_w_ukv', 'v_w_o_attn', 'v_lru_conv_w', 'v_lru_conv_b', 'v_lru_w_a', 'v_lru_b_a', 'v_lru_w_x', 'v_lru_b_x', 'v_lru_lambda', 'v_w_o_lru', 'v_w_out', 'v_norm2_g', 'v_w_up', 'v_ffn_conv_w', 'v_ffn_conv_b', 'v_w_down', 'v_final_g']
TWIN_OUTPUTS = ['loss', 'grad_x', 'grad_c_ctx', 'grad_w_mod', 'grad_b_mod', 'grad_norm1_g', 'grad_w_in', 'grad_b_gate', 'grad_q_norm_g', 'grad_kv_norm_g', 'grad_w_uq', 'grad_w_ukv', 'grad_w_o_attn', 'grad_lru_conv_w', 'grad_lru_conv_b', 'grad_lru_w_a', 'grad_lru_b_a', 'grad_lru_w_x', 'grad_lru_b_x', 'grad_lru_lambda', 'grad_w_o_lru', 'grad_w_out', 'grad_norm2_g', 'grad_w_up', 'grad_ffn_conv_w', 'grad_ffn_conv_b', 'grad_w_down', 'grad_final_g', 'delta_c_ctx', 'delta_w_mod', 'delta_b_mod', 'delta_norm1_g', 'delta_w_in', 'delta_b_gate', 'delta_q_norm_g', 'delta_kv_norm_g', 'delta_w_uq', 'delta_w_ukv', 'delta_w_o_attn', 'delta_lru_conv_w', 'delta_lru_conv_b', 'delta_lru_w_a', 'delta_lru_b_a', 'delta_lru_w_x', 'delta_lru_b_x', 'delta_lru_lambda', 'delta_w_o_lru', 'delta_w_out', 'delta_norm2_g', 'delta_w_up', 'delta_ffn_conv_w', 'delta_ffn_conv_b', 'delta_w_down', 'delta_final_g', 'new_m_c_ctx', 'new_m_w_mod', 'new_m_b_mod', 'new_m_norm1_g', 'new_m_w_in', 'new_m_b_gate', 'new_m_q_norm_g', 'new_m_kv_norm_g', 'new_m_w_uq', 'new_m_w_ukv', 'new_m_w_o_attn', 'new_m_lru_conv_w', 'new_m_lru_conv_b', 'new_m_lru_w_a', 'new_m_lru_b_a', 'new_m_lru_w_x', 'new_m_lru_b_x', 'new_m_lru_lambda', 'new_m_w_o_lru', 'new_m_w_out', 'new_m_norm2_g', 'new_m_w_up', 'new_m_ffn_conv_w', 'new_m_ffn_conv_b', 'new_m_w_down', 'new_m_final_g', 'new_v_c_ctx', 'new_v_w_mod', 'new_v_b_mod', 'new_v_norm1_g', 'new_v_w_in', 'new_v_b_gate', 'new_v_q_norm_g', 'new_v_kv_norm_g', 'new_v_w_uq', 'new_v_w_ukv', 'new_v_w_o_attn', 'new_v_lru_conv_w', 'new_v_lru_conv_b', 'new_v_lru_w_a', 'new_v_lru_b_a', 'new_v_lru_w_x', 'new_v_lru_b_x', 'new_v_lru_lambda', 'new_v_w_o_lru', 'new_v_w_out', 'new_v_norm2_g', 'new_v_w_up', 'new_v_ffn_conv_w', 'new_v_ffn_conv_b', 'new_v_w_down', 'new_v_final_g']
TWIN_LEAF_KINDS = {'loss': 'loss', 'grad_x': 'grad_x', 'grad_c_ctx': 'grad_w', 'grad_w_mod': 'grad_w', 'grad_b_mod': 'grad_w', 'grad_norm1_g': 'grad_w', 'grad_w_in': 'grad_w', 'grad_b_gate': 'grad_w', 'grad_q_norm_g': 'grad_w', 'grad_kv_norm_g': 'grad_w', 'grad_w_uq': 'grad_w', 'grad_w_ukv': 'grad_w', 'grad_w_o_attn': 'grad_w', 'grad_lru_conv_w': 'grad_w', 'grad_lru_conv_b': 'grad_w', 'grad_lru_w_a': 'grad_w', 'grad_lru_b_a': 'grad_w', 'grad_lru_w_x': 'grad_w', 'grad_lru_b_x': 'grad_w', 'grad_lru_lambda': 'grad_w', 'grad_w_o_lru': 'grad_w', 'grad_w_out': 'grad_w', 'grad_norm2_g': 'grad_w', 'grad_w_up': 'grad_w', 'grad_ffn_conv_w': 'grad_w', 'grad_ffn_conv_b': 'grad_w', 'grad_w_down': 'grad_w', 'grad_final_g': 'grad_w', 'delta_c_ctx': 'delta_w', 'delta_w_mod': 'delta_w', 'delta_b_mod': 'delta_w', 'delta_norm1_g': 'delta_w', 'delta_w_in': 'delta_w', 'delta_b_gate': 'delta_w', 'delta_q_norm_g': 'delta_w', 'delta_kv_norm_g': 'delta_w', 'delta_w_uq': 'delta_w', 'delta_w_ukv': 'delta_w', 'delta_w_o_attn': 'delta_w', 'delta_lru_conv_w': 'delta_w', 'delta_lru_conv_b': 'delta_w', 'delta_lru_w_a': 'delta_w', 'delta_lru_b_a': 'delta_w', 'delta_lru_w_x': 'delta_w', 'delta_lru_b_x': 'delta_w', 'delta_lru_lambda': 'delta_w', 'delta_w_o_lru': 'delta_w', 'delta_w_out': 'delta_w', 'delta_norm2_g': 'delta_w', 'delta_w_up': 'delta_w', 'delta_ffn_conv_w': 'delta_w', 'delta_ffn_conv_b': 'delta_w', 'delta_w_down': 'delta_w', 'delta_final_g': 'delta_w', 'new_m_c_ctx': 'new_m', 'new_m_w_mod': 'new_m', 'new_m_b_mod': 'new_m', 'new_m_norm1_g': 'new_m', 'new_m_w_in': 'new_m', 'new_m_b_gate': 'new_m', 'new_m_q_norm_g': 'new_m', 'new_m_kv_norm_g': 'new_m', 'new_m_w_uq': 'new_m', 'new_m_w_ukv': 'new_m', 'new_m_w_o_attn': 'new_m', 'new_m_lru_conv_w': 'new_m', 'new_m_lru_conv_b': 'new_m', 'new_m_lru_w_a': 'new_m', 'new_m_lru_b_a': 'new_m', 'new_m_lru_w_x': 'new_m', 'new_m_lru_b_x': 'new_m', 'new_m_lru_lambda': 'new_m', 'new_m_w_o_lru': 'new_m', 'new_m_w_out': 'new_m', 'new_m_norm2_g': 'new_m', 'new_m_w_up': 'new_m', 'new_m_ffn_conv_w': 'new_m', 'new_m_ffn_conv_b': 'new_m', 'new_m_w_down': 'new_m', 'new_m_final_g': 'new_m', 'new_v_c_ctx': 'new_v', 'new_v_w_mod': 'new_v', 'new_v_b_mod': 'new_v', 'new_v_norm1_g': 'new_v', 'new_v_w_in': 'new_v', 'new_v_b_gate': 'new_v', 'new_v_q_norm_g': 'new_v', 'new_v_kv_norm_g': 'new_v', 'new_v_w_uq': 'new_v', 'new_v_w_ukv': 'new_v', 'new_v_w_o_attn': 'new_v', 'new_v_lru_conv_w': 'new_v', 'new_v_lru_conv_b': 'new_v', 'new_v_lru_w_a': 'new_v', 'new_v_lru_b_a': 'new_v', 'new_v_lru_w_x': 'new_v', 'new_v_lru_b_x': 'new_v', 'new_v_lru_lambda': 'new_v', 'new_v_w_o_lru': 'new_v', 'new_v_w_out': 'new_v', 'new_v_norm2_g': 'new_v', 'new_v_w_up': 'new_v', 'new_v_ffn_conv_w': 'new_v', 'new_v_ffn_conv_b': 'new_v', 'new_v_w_down': 'new_v', 'new_v_final_g': 'new_v'}


def _forward(args):
    return _fwd_reference(*[args[k] for k in FWD_PARAMS])


def _output_shape():
    def fwd():
        inp = _fwd_setup_inputs(0)
        return _fwd_reference(*[inp[k] for k in FWD_PARAMS])
    out = _jax.eval_shape(fwd)
    return out.shape, out.dtype

N_MICROBATCH = 1
ADAM_LR = 0.001
ADAM_B1 = 0.9
ADAM_B2 = 0.999
ADAM_EPS = 1e-08
ADAM_WD = 0.01
ADAM_STEP = 10
PER_EXAMPLE_BATCH_AXIS = {'x': 0, 'c': 0, 'ctx': 0, 'loss_target': 0}
SHARED_INPUTS = []
_WEIGHT_DTYPES = {'c_ctx': _jnp.float32, 'w_mod': _jnp.float32, 'b_mod': _jnp.float32, 'norm1_g': _jnp.float32, 'w_in': _jnp.float32, 'b_gate': _jnp.float32, 'q_norm_g': _jnp.float32, 'kv_norm_g': _jnp.float32, 'w_uq': _jnp.float32, 'w_ukv': _jnp.float32, 'w_o_attn': _jnp.float32, 'lru_conv_w': _jnp.float32, 'lru_conv_b': _jnp.float32, 'lru_w_a': _jnp.float32, 'lru_b_a': _jnp.float32, 'lru_w_x': _jnp.float32, 'lru_b_x': _jnp.float32, 'lru_lambda': _jnp.float32, 'w_o_lru': _jnp.float32, 'w_out': _jnp.float32, 'norm2_g': _jnp.float32, 'w_up': _jnp.float32, 'ffn_conv_w': _jnp.float32, 'ffn_conv_b': _jnp.float32, 'w_down': _jnp.float32, 'final_g': _jnp.float32}
MOMENT_SCALE = {'c_ctx': 3.710140e-02, 'w_mod': 2.229158e-01, 'b_mod': 4.235592e-01, 'norm1_g': 1.384824e-01, 'w_in': 1.189378e-01, 'b_gate': 6.327343e-02, 'q_norm_g': 9.228913e-03, 'kv_norm_g': 1.103938e-01, 'w_uq': 6.745108e-03, 'w_ukv': 5.442343e-02, 'w_o_attn': 5.405419e-02, 'lru_conv_w': 1.675228e-01, 'lru_conv_b': 3.079948e-01, 'lru_w_a': 1.390763e-02, 'lru_b_a': 1.553928e-02, 'lru_w_x': 2.309626e-02, 'lru_b_x': 2.658727e-02, 'lru_lambda': 3.698783e-02, 'w_o_lru': 2.381104e-01, 'w_out': 2.513834e-01, 'norm2_g': 1.073100e-01, 'w_up': 5.464444e-02, 'ffn_conv_w': 5.586695e-02, 'ffn_conv_b': 4.583156e-02, 'w_down': 9.410815e-02, 'final_g': 3.302073e+01}


def _to_microbatches(a, axis):
    t = _jnp.moveaxis(a, axis, 0)
    t = t.reshape((N_MICROBATCH, t.shape[0] // N_MICROBATCH) + t.shape[1:])
    return _jnp.moveaxis(t, 1, axis + 1)


def setup_inputs(seed: int = 0) -> dict:
    inp = _fwd_setup_inputs(seed)
    key = _jax.random.fold_in(_jax.random.key(seed), 7919)
    shape, _ = _output_shape()
    out = dict(inp)
    out["loss_target"] = _jax.random.normal(_jax.random.fold_in(key, 0), shape, _jnp.float32)
    for i, name in enumerate(TWIN_WEIGHTS):
        w = inp[name].astype(_jnp.float32)
        if MOMENT_SCALE is None:
            s = _jnp.sqrt(_jnp.mean(_jnp.square(w)) + 1e-30)
        else:
            s = MOMENT_SCALE[name]
        km, kv = _jax.random.split(_jax.random.fold_in(key, i + 1))
        out[name] = w
        out["m_" + name] = s * _jax.random.normal(km, w.shape, _jnp.float32)
        out["v_" + name] = (s * s) * _jax.random.uniform(kv, w.shape, _jnp.float32, 0.5, 1.5)
    if N_MICROBATCH > 1:
        for name, axis in PER_EXAMPLE_BATCH_AXIS.items():
            out[name] = _to_microbatches(out[name], axis)
    return {'x': out['x'], 'c': out['c'], 'ctx': out['ctx'], 'c_ctx': out['c_ctx'], 'w_mod': out['w_mod'], 'b_mod': out['b_mod'], 'norm1_g': out['norm1_g'], 'w_in': out['w_in'], 'b_gate': out['b_gate'], 'q_norm_g': out['q_norm_g'], 'kv_norm_g': out['kv_norm_g'], 'w_uq': out['w_uq'], 'w_ukv': out['w_ukv'], 'w_o_attn': out['w_o_attn'], 'lru_conv_w': out['lru_conv_w'], 'lru_conv_b': out['lru_conv_b'], 'lru_w_a': out['lru_w_a'], 'lru_b_a': out['lru_b_a'], 'lru_w_x': out['lru_w_x'], 'lru_b_x': out['lru_b_x'], 'lru_lambda': out['lru_lambda'], 'w_o_lru': out['w_o_lru'], 'w_out': out['w_out'], 'norm2_g': out['norm2_g'], 'w_up': out['w_up'], 'ffn_conv_w': out['ffn_conv_w'], 'ffn_conv_b': out['ffn_conv_b'], 'w_down': out['w_down'], 'final_g': out['final_g'], 'loss_target': out['loss_target'], 'm_c_ctx': out['m_c_ctx'], 'm_w_mod': out['m_w_mod'], 'm_b_mod': out['m_b_mod'], 'm_norm1_g': out['m_norm1_g'], 'm_w_in': out['m_w_in'], 'm_b_gate': out['m_b_gate'], 'm_q_norm_g': out['m_q_norm_g'], 'm_kv_norm_g': out['m_kv_norm_g'], 'm_w_uq': out['m_w_uq'], 'm_w_ukv': out['m_w_ukv'], 'm_w_o_attn': out['m_w_o_attn'], 'm_lru_conv_w': out['m_lru_conv_w'], 'm_lru_conv_b': out['m_lru_conv_b'], 'm_lru_w_a': out['m_lru_w_a'], 'm_lru_b_a': out['m_lru_b_a'], 'm_lru_w_x': out['m_lru_w_x'], 'm_lru_b_x': out['m_lru_b_x'], 'm_lru_lambda': out['m_lru_lambda'], 'm_w_o_lru': out['m_w_o_lru'], 'm_w_out': out['m_w_out'], 'm_norm2_g': out['m_norm2_g'], 'm_w_up': out['m_w_up'], 'm_ffn_conv_w': out['m_ffn_conv_w'], 'm_ffn_conv_b': out['m_ffn_conv_b'], 'm_w_down': out['m_w_down'], 'm_final_g': out['m_final_g'], 'v_c_ctx': out['v_c_ctx'], 'v_w_mod': out['v_w_mod'], 'v_b_mod': out['v_b_mod'], 'v_norm1_g': out['v_norm1_g'], 'v_w_in': out['v_w_in'], 'v_b_gate': out['v_b_gate'], 'v_q_norm_g': out['v_q_norm_g'], 'v_kv_norm_g': out['v_kv_norm_g'], 'v_w_uq': out['v_w_uq'], 'v_w_ukv': out['v_w_ukv'], 'v_w_o_attn': out['v_w_o_attn'], 'v_lru_conv_w': out['v_lru_conv_w'], 'v_lru_conv_b': out['v_lru_conv_b'], 'v_lru_w_a': out['v_lru_w_a'], 'v_lru_b_a': out['v_lru_b_a'], 'v_lru_w_x': out['v_lru_w_x'], 'v_lru_b_x': out['v_lru_b_x'], 'v_lru_lambda': out['v_lru_lambda'], 'v_w_o_lru': out['v_w_o_lru'], 'v_w_out': out['v_w_out'], 'v_norm2_g': out['v_norm2_g'], 'v_w_up': out['v_w_up'], 'v_ffn_conv_w': out['v_ffn_conv_w'], 'v_ffn_conv_b': out['v_ffn_conv_b'], 'v_w_down': out['v_w_down'], 'v_final_g': out['v_final_g']}


def _loss(weights, diff, rest, loss_target):
    with _jax.named_scope("forward"):
        args = {**rest, TWIN_DIFF_INPUT: diff, **{k: w.astype(_WEIGHT_DTYPES[k]) for k, w in weights.items()}}
        y = _forward(args)
    with _jax.named_scope("loss_head"):
        err = _jnp.square(y.astype(_jnp.float32) - loss_target)
        return 0.5 * _jnp.sum(_jnp.mean(err, axis=-1)) if err.ndim else 0.5 * err


def _adamw(w, g, m, v):
    m = ADAM_B1 * m + (1.0 - ADAM_B1) * g
    v = ADAM_B2 * v + (1.0 - ADAM_B2) * _jnp.square(g)
    m_hat = m / (1.0 - ADAM_B1 ** ADAM_STEP)
    v_hat = v / (1.0 - ADAM_B2 ** ADAM_STEP)
    delta = -ADAM_LR * (m_hat / (_jnp.sqrt(v_hat) + ADAM_EPS) + ADAM_WD * w)
    return delta, m, v


def reference(x, c, ctx, c_ctx, w_mod, b_mod, norm1_g, w_in, b_gate, q_norm_g, kv_norm_g, w_uq, w_ukv, w_o_attn, lru_conv_w, lru_conv_b, lru_w_a, lru_b_a, lru_w_x, lru_b_x, lru_lambda, w_o_lru, w_out, norm2_g, w_up, ffn_conv_w, ffn_conv_b, w_down, final_g, loss_target, m_c_ctx, m_w_mod, m_b_mod, m_norm1_g, m_w_in, m_b_gate, m_q_norm_g, m_kv_norm_g, m_w_uq, m_w_ukv, m_w_o_attn, m_lru_conv_w, m_lru_conv_b, m_lru_w_a, m_lru_b_a, m_lru_w_x, m_lru_b_x, m_lru_lambda, m_w_o_lru, m_w_out, m_norm2_g, m_w_up, m_ffn_conv_w, m_ffn_conv_b, m_w_down, m_final_g, v_c_ctx, v_w_mod, v_b_mod, v_norm1_g, v_w_in, v_b_gate, v_q_norm_g, v_kv_norm_g, v_w_uq, v_w_ukv, v_w_o_attn, v_lru_conv_w, v_lru_conv_b, v_lru_w_a, v_lru_b_a, v_lru_w_x, v_lru_b_x, v_lru_lambda, v_w_o_lru, v_w_out, v_norm2_g, v_w_up, v_ffn_conv_w, v_ffn_conv_b, v_w_down, v_final_g):
    given = dict(x=x, c=c, ctx=ctx, c_ctx=c_ctx, w_mod=w_mod, b_mod=b_mod, norm1_g=norm1_g, w_in=w_in, b_gate=b_gate, q_norm_g=q_norm_g, kv_norm_g=kv_norm_g, w_uq=w_uq, w_ukv=w_ukv, w_o_attn=w_o_attn, lru_conv_w=lru_conv_w, lru_conv_b=lru_conv_b, lru_w_a=lru_w_a, lru_b_a=lru_b_a, lru_w_x=lru_w_x, lru_b_x=lru_b_x, lru_lambda=lru_lambda, w_o_lru=w_o_lru, w_out=w_out, norm2_g=norm2_g, w_up=w_up, ffn_conv_w=ffn_conv_w, ffn_conv_b=ffn_conv_b, w_down=w_down, final_g=final_g, loss_target=loss_target, m_c_ctx=m_c_ctx, m_w_mod=m_w_mod, m_b_mod=m_b_mod, m_norm1_g=m_norm1_g, m_w_in=m_w_in, m_b_gate=m_b_gate, m_q_norm_g=m_q_norm_g, m_kv_norm_g=m_kv_norm_g, m_w_uq=m_w_uq, m_w_ukv=m_w_ukv, m_w_o_attn=m_w_o_attn, m_lru_conv_w=m_lru_conv_w, m_lru_conv_b=m_lru_conv_b, m_lru_w_a=m_lru_w_a, m_lru_b_a=m_lru_b_a, m_lru_w_x=m_lru_w_x, m_lru_b_x=m_lru_b_x, m_lru_lambda=m_lru_lambda, m_w_o_lru=m_w_o_lru, m_w_out=m_w_out, m_norm2_g=m_norm2_g, m_w_up=m_w_up, m_ffn_conv_w=m_ffn_conv_w, m_ffn_conv_b=m_ffn_conv_b, m_w_down=m_w_down, m_final_g=m_final_g, v_c_ctx=v_c_ctx, v_w_mod=v_w_mod, v_b_mod=v_b_mod, v_norm1_g=v_norm1_g, v_w_in=v_w_in, v_b_gate=v_b_gate, v_q_norm_g=v_q_norm_g, v_kv_norm_g=v_kv_norm_g, v_w_uq=v_w_uq, v_w_ukv=v_w_ukv, v_w_o_attn=v_w_o_attn, v_lru_conv_w=v_lru_conv_w, v_lru_conv_b=v_lru_conv_b, v_lru_w_a=v_lru_w_a, v_lru_b_a=v_lru_b_a, v_lru_w_x=v_lru_w_x, v_lru_b_x=v_lru_b_x, v_lru_lambda=v_lru_lambda, v_w_o_lru=v_w_o_lru, v_w_out=v_w_out, v_norm2_g=v_norm2_g, v_w_up=v_w_up, v_ffn_conv_w=v_ffn_conv_w, v_ffn_conv_b=v_ffn_conv_b, v_w_down=v_w_down, v_final_g=v_final_g)
    weights = {n: given[n] for n in TWIN_WEIGHTS}
    shared = {n: given[n] for n in SHARED_INPUTS}
    per_example = {n: given[n] for n in ['x', 'c', 'ctx']}
    grad_fn = _jax.value_and_grad(_loss, argnums=(0, 1))

    def one_microbatch(ex, loss_target):
        ex = dict(ex)
        diff = ex.pop(TWIN_DIFF_INPUT)
        return grad_fn(weights, diff, {**shared, **ex}, loss_target)

    if N_MICROBATCH == 1:
        loss, (grad_w, grad_x) = one_microbatch(per_example, given["loss_target"])
    else:
        def body(carry, xs):
            loss_sum, grad_sum = carry
            l_k, (gw_k, gx_k) = one_microbatch(xs[0], xs[1])
            with _jax.named_scope("update"):
                return (loss_sum + l_k, _jax.tree.map(_jnp.add, grad_sum, gw_k)), gx_k

        init = (_jnp.zeros((), _jnp.float32), _jax.tree.map(_jnp.zeros_like, weights))
        (loss, grad_w), grad_x = _jax.lax.scan(body, init, (per_example, given["loss_target"]))
    with _jax.named_scope("update"):
        delta_w, new_m, new_v = {}, {}, {}
        for n in TWIN_WEIGHTS:
            delta_w[n], new_m[n], new_v[n] = _adamw(weights[n], grad_w[n], given["m_" + n], given["v_" + n])
    return (loss, grad_x, *[grad_w[n] for n in TWIN_WEIGHTS], *[delta_w[n] for n in TWIN_WEIGHTS],
            *[new_m[n] for n in TWIN_WEIGHTS], *[new_v[n] for n in TWIN_WEIGHTS])
```

```python
import functools
import math

import jax
import jax.numpy as jnp
from jax import lax
from jax.experimental import pallas as pl
from jax.experimental.pallas import tpu as pltpu

F32 = jnp.float32
BF16 = jnp.bfloat16
MESH = pl.DeviceIdType.MESH

N_DEV = 8
D = 1024
N_HEADS = 8
HEAD_PAD = 128
QK_NOPE, QK_ROPE, V_HEAD = 64, 32, 64
QK_DIM = QK_NOPE + QK_ROPE
Q_RANK, KV_RANK = 384, 256
LRU_W, LRU_BLOCKS, LRU_BW = 1280, 10, 128
FFN = 2816
GRID_W = 64
ROPE_BASE = 10000.0
LRU_C = 8.0
EPS = 1e-6
Z_Q, Z_KV, Z_KR, Z_XB, Z_YB, Z_GL, Z_END = 0, 384, 640, 768, 2048, 3328, 5376
ADAM_LR, ADAM_B1, ADAM_B2, ADAM_EPS, ADAM_WD, ADAM_STEP = 0.001, 0.9, 0.999, 1e-08, 0.01, 10

VMEM_LIMIT = 52 * 1024 * 1024
FLAT_C = 512
BIG_ROWS = 256

WEIGHTS = ['c_ctx', 'w_mod', 'b_mod', 'norm1_g', 'w_in', 'b_gate', 'q_norm_g', 'kv_norm_g', 'w_uq', 'w_ukv',
           'w_o_attn', 'lru_conv_w', 'lru_conv_b', 'lru_w_a', 'lru_b_a', 'lru_w_x', 'lru_b_x', 'lru_lambda',
           'w_o_lru', 'w_out', 'norm2_g', 'w_up', 'ffn_conv_w', 'ffn_conv_b', 'w_down', 'final_g']
COL_SHARDED = ['w_in', 'w_uq', 'w_ukv', 'w_o_attn', 'lru_conv_w', 'lru_b_a', 'lru_b_x', 'lru_lambda', 'w_up',
               'ffn_conv_w']
ROW_SHARDED = ['w_o_lru', 'w_out', 'w_down']
BIG_BF16 = ['w_in', 'w_uq', 'w_ukv', 'w_o_attn', 'w_o_lru', 'w_out', 'w_up', 'w_down']
SMALL_F32 = ['lru_conv_w', 'lru_b_a', 'lru_b_x', 'lru_lambda', 'ffn_conv_w']
SHARDED = BIG_BF16 + SMALL_F32
REPLICATED = ['c_ctx', 'b_mod', 'norm1_g', 'b_gate', 'q_norm_g', 'kv_norm_g', 'lru_conv_b', 'lru_w_a', 'lru_w_x',
              'norm2_g', 'ffn_conv_b', 'final_g']


def _cparams(sem=None):
    return pltpu.CompilerParams(dimension_semantics=sem, vmem_limit_bytes=VMEM_LIMIT)


def _pick(n, cands):
    for c in cands:
        if c <= n and n % c == 0:
            return c
    return n


def _best_div(n, mult, cap):
    best = mult
    for d in range(mult, min(n, cap) + 1, mult):
        if n % d == 0:
            best = d
    return best


ROW_TILES = (1088, 1024, 544, 512, 256, 128, 64, 32, 16, 8)
LANE_TILES = (1408, 1024, 896, 768, 640, 512, 384, 256, 128)


def _my_pos():
    return lax.axis_index("x"), lax.axis_index("y"), lax.axis_index("c")


def _my_index():
    x, y, c = _my_pos()
    return 4 * x + 2 * y + c


def all_gather(name, shard):
    r, ccols = shard.shape

    def body(x_ref, out_ref, send_sems, recv_sems, local_sem):
        x, y, c = _my_pos()
        me, sibling = (x, y, c), (x, y, 1 - c)
        chips = [(1 - x, y), (x, 1 - y), (1 - x, 1 - y)]

        def slot(px, py, pc):
            return out_ref.at[4 * px + 2 * py + pc]

        def copy(k, block, to, src=None):
            return pltpu.make_async_remote_copy(
                src_ref=slot(*block) if src is None else src, dst_ref=slot(*block),
                send_sem=send_sems.at[k], recv_sem=recv_sems.at[k], device_id=to, device_id_type=MESH)

        mine = pltpu.make_async_copy(x_ref, slot(*me), local_sem)
        mine.start()
        first = [copy(0, me, sibling, src=x_ref)]
        first += [copy(1 + j, me, (*chip, c), src=x_ref) for j, chip in enumerate(chips)]
        for cp in first:
            cp.start()
        passed = [copy(4 + j, (*chip, c), sibling) for j, chip in enumerate(chips)]
        for j, chip in enumerate(chips):
            copy(1 + j, (*chip, c), me).wait_recv()
            passed[j].start()
        copy(0, sibling, me).wait_recv()
        for j, chip in enumerate(chips):
            copy(4 + j, (*chip, 1 - c), me).wait_recv()
        for cp in first + passed:
            cp.wait_send()
        mine.wait()

    return pl.pallas_call(
        body, name=name,
        out_shape=jax.ShapeDtypeStruct((N_DEV, r, ccols), shard.dtype),
        in_specs=[pl.BlockSpec(memory_space=pl.ANY)],
        out_specs=pl.BlockSpec(memory_space=pl.ANY),
        scratch_shapes=[pltpu.SemaphoreType.DMA((7,)), pltpu.SemaphoreType.DMA((7,)), pltpu.SemaphoreType.DMA],
    )(shard)


def all_to_all(name, chunks):
    _, r, ccols = chunks.shape

    def body(x_ref, out_ref, send_sems, recv_sems, local_sem):
        x, y, c = _my_pos()
        me = 4 * x + 2 * y + c
        mine = pltpu.make_async_copy(x_ref.at[me], out_ref.at[me], local_sem)
        mine.start()
        copies = []
        for rel in range(1, N_DEV):
            dx, dy, dc = (rel >> 2) & 1, (rel >> 1) & 1, rel & 1
            px, py, pc = x ^ dx, y ^ dy, c ^ dc
            peer = 4 * px + 2 * py + pc
            cp = pltpu.make_async_remote_copy(
                src_ref=x_ref.at[peer], dst_ref=out_ref.at[me],
                send_sem=send_sems.at[rel - 1], recv_sem=recv_sems.at[rel - 1],
                device_id=(px, py, pc), device_id_type=MESH)
            cp.start()
            copies.append((cp, peer))
        for rel, (cp, peer) in enumerate(copies):
            pltpu.make_async_remote_copy(
                src_ref=x_ref.at[peer], dst_ref=out_ref.at[peer],
                send_sem=send_sems.at[rel], recv_sem=recv_sems.at[rel],
                device_id=(x, y, c), device_id_type=MESH).wait_recv()
        for cp, _ in copies:
            cp.wait_send()
        mine.wait()

    return pl.pallas_call(
        body, name=name,
        out_shape=jax.ShapeDtypeStruct(chunks.shape, chunks.dtype),
        in_specs=[pl.BlockSpec(memory_space=pl.ANY)],
        out_specs=pl.BlockSpec(memory_space=pl.ANY),
        scratch_shapes=[pltpu.SemaphoreType.DMA((7,)), pltpu.SemaphoreType.DMA((7,)), pltpu.SemaphoreType.DMA],
    )(chunks)


def sum_slots(name, slots):
    _, r, ccols = slots.shape
    tr = _best_div(r, 16, 1024)

    def body(s_ref, o_ref):
        acc = s_ref[0].astype(F32)
        for p in range(1, N_DEV):
            acc = acc + s_ref[p].astype(F32)
        o_ref[...] = acc

    return pl.pallas_call(
        body, name=name, grid=(r // tr,),
        out_shape=jax.ShapeDtypeStruct((r, ccols), F32),
        in_specs=[pl.BlockSpec((N_DEV, tr, ccols), lambda i: (0, i, 0))],
        out_specs=pl.BlockSpec((tr, ccols), lambda i: (i, 0)),
        compiler_params=_cparams(("parallel",)),
    )(slots)


def matmul(name, a, b, mode, out_dtype, tm=None, tn=None, tk=None):
    if mode == 'nn':
        (m, k), (k2, n) = a.shape, b.shape
    elif mode == 'nt':
        (m, k), (n, k2) = a.shape, b.shape
    else:
        (k, m), (k2, n) = a.shape, b.shape
    assert k == k2, (name, a.shape, b.shape, mode)
    if mode == 'tn':
        tm = tm or _pick(m, LANE_TILES)
        tk = tk or _pick(k, ROW_TILES)
    else:
        tm = tm or _pick(m, ROW_TILES)
        tk = tk or _pick(k, LANE_TILES)
    tn = tn or _pick(n, LANE_TILES)
    nk = k // tk
    if mode == 'nn':
        a_spec = pl.BlockSpec((tm, tk), lambda i, j, kk: (i, kk))
        b_spec = pl.BlockSpec((tk, tn), lambda i, j, kk: (kk, j))
        dn = (((1,), (0,)), ((), ()))
    elif mode == 'nt':
        a_spec = pl.BlockSpec((tm, tk), lambda i, j, kk: (i, kk))
        b_spec = pl.BlockSpec((tn, tk), lambda i, j, kk: (j, kk))
        dn = (((1,), (1,)), ((), ()))
    else:
        a_spec = pl.BlockSpec((tk, tm), lambda i, j, kk: (kk, i))
        b_spec = pl.BlockSpec((tk, tn), lambda i, j, kk: (kk, j))
        dn = (((0,), (0,)), ((), ()))

    def body(a_ref, b_ref, o_ref, acc_ref):
        kk = pl.program_id(2)

        @pl.when(kk == 0)
        def _():
            acc_ref[...] = jnp.zeros_like(acc_ref)

        acc_ref[...] += lax.dot_general(a_ref[...].astype(BF16), b_ref[...].astype(BF16), dn,
                                        preferred_element_type=F32)

        @pl.when(kk == nk - 1)
        def _():
            o_ref[...] = acc_ref[...].astype(o_ref.dtype)

    return pl.pallas_call(
        body, name=name, grid=(m // tm, n // tn, nk),
        out_shape=jax.ShapeDtypeStruct((m, n), out_dtype),
        in_specs=[a_spec, b_spec],
        out_specs=pl.BlockSpec((tm, tn), lambda i, j, kk: (i, j)),
        scratch_shapes=[pltpu.VMEM((tm, tn), F32)],
        compiler_params=_cparams(("parallel", "parallel", "arbitrary")),
    )(a, b)


def rowwise(name, fn, rows, params, out_rows, out_accs, n_rows, t_lat, tm):
    nb = n_rows // tm
    in_specs, piece_counts = [], []
    operands = []
    for arr, off, width in rows:
        g = math.gcd(off, width) if off else width
        assert g % 128 == 0 or (off == 0 and width == arr.shape[1]), (name, off, width)
        cnt = width // g
        last = arr.shape[0] // tm - 1
        clamp = arr.shape[0] < n_rows
        for p in range(cnt):
            cb = off // g + p
            if clamp:
                in_specs.append(pl.BlockSpec((tm, g), lambda i, cb=cb, last=last: (jnp.minimum(i, last), cb)))
            else:
                in_specs.append(pl.BlockSpec((tm, g), lambda i, cb=cb: (i, cb)))
            operands.append(arr)
        piece_counts.append(cnt)
    for p in params:
        in_specs.append(pl.BlockSpec(p.shape, lambda i, nd=p.ndim: (0,) * nd))
        operands.append(p)
    n_in = sum(piece_counts)
    n_par = len(params)
    n_or = len(out_rows)
    out_shape = [jax.ShapeDtypeStruct((n_rows, w), dt) for w, dt in out_rows]
    out_shape += [jax.ShapeDtypeStruct(s, F32) for s in out_accs]
    out_specs = [pl.BlockSpec((tm, w), lambda i: (i, 0)) for w, _ in out_rows]
    out_specs += [pl.BlockSpec(s, lambda i, nd=len(s): (0,) * nd) for s in out_accs]

    def body(*refs):
        in_refs, par_refs = refs[:n_in], refs[n_in:n_in + n_par]
        orow_refs = refs[n_in + n_par:n_in + n_par + n_or]
        oacc_refs = refs[n_in + n_par + n_or:]
        i = pl.program_id(0)
        tiles, at = [], 0
        for cnt in piece_counts:
            parts = [in_refs[at + p][...] for p in range(cnt)]
            tiles.append(parts[0] if cnt == 1 else jnp.concatenate(parts, axis=1))
            at += cnt
        is_ctx = i * tm >= t_lat
        outs, accs = fn(is_ctx, tiles, [p[...] for p in par_refs])
        for o_ref, o in zip(orow_refs, outs):
            o_ref[...] = o.astype(o_ref.dtype)
        if oacc_refs:
            @pl.when(i == 0)
            def _():
                for a_ref in oacc_refs:
                    a_ref[...] = jnp.zeros_like(a_ref)
            for a_ref, a in zip(oacc_refs, accs):
                a_ref[...] += a.astype(F32)

    res = pl.pallas_call(
        body, name=name, grid=(nb,),
        out_shape=out_shape, in_specs=in_specs, out_specs=out_specs,
        compiler_params=_cparams(("arbitrary",)),
    )(*operands)
    return res[:n_or], res[n_or:]


def _rms(x, g):
    return x * lax.rsqrt(jnp.mean(x * x, axis=-1, keepdims=True) + EPS) * g


def _norm_mod(x, g, sc, sh):
    return _rms(x, g) * (1.0 + sc) + sh


def _sigmoid(x):
    return 1.0 / (1.0 + jnp.exp(-x))


def _silu(x):
    return x * _sigmoid(x)


def _gelu(x):
    return 0.5 * x * (1.0 + jnp.tanh(math.sqrt(2.0 / math.pi) * (x + 0.044715 * (x * x * x))))


def _sel(is_ctx, p):
    return jnp.where(is_ctx, p[1:2], p[0:1])


def _seg_acc(is_ctx, v):
    rows = lax.broadcasted_iota(jnp.int32, (2, v.shape[1]), 0)
    return jnp.where(rows == is_ctx.astype(jnp.int32), jnp.broadcast_to(v, (2, v.shape[1])), 0.0)


def _rsum(v):
    return jnp.sum(v, axis=0, keepdims=True)


def _shift_rows(x, o, t_lat, n):
    if o == 0:
        return x
    y = pltpu.roll(x, (-o) % n, 0)
    t = lax.broadcasted_iota(jnp.int32, x.shape, 0)
    src = t + o
    ok = (src >= 0) & (src < n) & ((src >= t_lat) == (t >= t_lat))
    return jnp.where(ok, y, 0.0)


def conv_fwd(name, xarr, col_off, width, w, b, left, n_rows, t_lat, cb=128):
    taps = w.shape[0]
    assert col_off % cb == 0 and width % cb == 0

    def body(x_ref, w_ref, b_ref, o_ref):
        x = x_ref[...]
        acc = jnp.broadcast_to(b_ref[...], x.shape)
        for k in range(taps):
            acc = acc + _shift_rows(x, k - left, t_lat, n_rows) * w_ref[k:k + 1, :]
        o_ref[...] = acc

    return pl.pallas_call(
        body, name=name, grid=(width // cb,),
        out_shape=jax.ShapeDtypeStruct((n_rows, width), F32),
        in_specs=[pl.BlockSpec((n_rows, cb), lambda j: (0, col_off // cb + j)),
                  pl.BlockSpec((taps, cb), lambda j: (0, j)),
                  pl.BlockSpec((1, cb), lambda j: (0, j))],
        out_specs=pl.BlockSpec((n_rows, cb), lambda j: (0, j)),
        compiler_params=_cparams(("parallel",)),
    )(xarr, w, b)


def conv_bwd(name, dout, xarr, col_off, width, w, left, n_rows, t_lat, cb=128):
    taps = w.shape[0]

    def body(d_ref, x_ref, w_ref, dx_ref, dw_ref, db_ref):
        d = d_ref[...]
        x = x_ref[...]
        dx = jnp.zeros_like(d)
        dws = []
        for k in range(taps):
            dx = dx + _shift_rows(d, left - k, t_lat, n_rows) * w_ref[k:k + 1, :]
            dws.append(_rsum(d * _shift_rows(x, k - left, t_lat, n_rows)))
        dx_ref[...] = dx.astype(dx_ref.dtype)
        dw_ref[...] = jnp.concatenate(dws, axis=0)
        db_ref[...] = _rsum(d)

    return pl.pallas_call(
        body, name=name, grid=(width // cb,),
        out_shape=[jax.ShapeDtypeStruct((n_rows, width), BF16), jax.ShapeDtypeStruct((taps, width), F32),
                   jax.ShapeDtypeStruct((1, width), F32)],
        in_specs=[pl.BlockSpec((n_rows, cb), lambda j: (0, j)),
                  pl.BlockSpec((n_rows, cb), lambda j: (0, col_off // cb + j)),
                  pl.BlockSpec((taps, cb), lambda j: (0, j))],
        out_specs=[pl.BlockSpec((n_rows, cb), lambda j: (0, j)), pl.BlockSpec((taps, cb), lambda j: (0, j)),
                   pl.BlockSpec((1, cb), lambda j: (0, j))],
        compiler_params=_cparams(("parallel",)),
    )(dout, xarr, w)


def _chunk_order(direction, nb, nbl):
    if direction == 'f':
        return lambda s: ((s + nbl) % nb, 0)
    return lambda s: (nb - 1 - s, 0)


def _adjoint_order(direction, nb, nbl):
    if direction == 'f':
        return lambda s: ((nb - 1 - s + nbl) % nb, 0)
    return lambda s: (s, 0)


def scan_fwd(name, a, u, direction, n_rows, t_lat, tc=128):
    w = a.shape[1]
    nb, nbl = n_rows // tc, t_lat // tc
    order = _chunk_order(direction, nb, nbl)
    rev = direction == 'b'
    nt = tc // 8

    def body(a_ref, u_ref, h_ref, hp_ref, carry):
        @pl.when(pl.program_id(0) == 0)
        def _():
            carry[...] = jnp.zeros_like(carry)

        def tile(kt, h):
            k = (nt - 1 - kt) if rev else kt
            r0 = pl.multiple_of(k * 8, 8)
            at = a_ref[pl.ds(r0, 8), :]
            ut = u_ref[pl.ds(r0, 8), :]
            hs, hps = [None] * 8, [None] * 8
            for j in (reversed(range(8)) if rev else range(8)):
                hps[j] = h
                h = at[j:j + 1, :] * h + ut[j:j + 1, :]
                hs[j] = h
            h_ref[pl.ds(r0, 8), :] = jnp.concatenate(hs, axis=0)
            hp_ref[pl.ds(r0, 8), :] = jnp.concatenate(hps, axis=0)
            return h

        carry[...] = lax.fori_loop(0, nt, tile, carry[...])

    spec = pl.BlockSpec((tc, w), order)
    return pl.pallas_call(
        body, name=name, grid=(nb,),
        out_shape=[jax.ShapeDtypeStruct((n_rows, w), F32)] * 2,
        in_specs=[spec, spec], out_specs=[spec, spec],
        scratch_shapes=[pltpu.VMEM((1, w), F32)],
        compiler_params=_cparams(("arbitrary",)),
    )(a, u)


def scan_adj(name, a, dh, hprev, direction, n_rows, t_lat, tc=128):
    w = a.shape[1]
    nb, nbl = n_rows // tc, t_lat // tc
    order = _adjoint_order(direction, nb, nbl)
    rev = direction == 'f'
    nt = tc // 8

    def dh_order(s):
        c, _ = order(s)
        return (jnp.minimum(c, nbl - 1), 0)

    def body(a_ref, dh_ref, hp_ref, du_ref, da_ref, carry):
        s = pl.program_id(0)

        @pl.when(s == 0)
        def _():
            carry[...] = jnp.zeros_like(carry)

        chunk, _ = order(s)
        live = (chunk < nbl).astype(F32)

        def tile(kt, c):
            k = (nt - 1 - kt) if rev else kt
            r0 = pl.multiple_of(k * 8, 8)
            at = a_ref[pl.ds(r0, 8), :]
            dt = dh_ref[pl.ds(r0, 8), :] * live
            lams = [None] * 8
            for j in (reversed(range(8)) if rev else range(8)):
                lam = dt[j:j + 1, :] + c
                lams[j] = lam
                c = at[j:j + 1, :] * lam
            lam8 = jnp.concatenate(lams, axis=0)
            du_ref[pl.ds(r0, 8), :] = lam8
            da_ref[pl.ds(r0, 8), :] = lam8 * hp_ref[pl.ds(r0, 8), :]
            return c

        carry[...] = lax.fori_loop(0, nt, tile, carry[...])

    spec = pl.BlockSpec((tc, w), order)
    return pl.pallas_call(
        body, name=name, grid=(nb,),
        out_shape=[jax.ShapeDtypeStruct((n_rows, w), F32)] * 2,
        in_specs=[spec, pl.BlockSpec((tc, w), dh_order), spec], out_specs=[spec, spec],
        scratch_shapes=[pltpu.VMEM((1, w), F32)],
        compiler_params=_cparams(("arbitrary",)),
    )(a, dh, hprev)


def _neg_expm1(y):
    series = -(y * (1.0 + y * (0.5 + y * (1.0 / 6.0 + y * (1.0 / 24.0)))))
    return jnp.where(y > -0.03, series, 1.0 - jnp.exp(y))


def _gate_elem(pre_r, pre_i, xc, b_a, b_x, sp):
    r = _sigmoid(pre_r + b_a)
    i = _sigmoid(pre_i + b_x)
    log_a = (-LRU_C) * r * sp
    a = jnp.exp(log_a)
    mult = jnp.sqrt(_neg_expm1(2.0 * log_a))
    return a, mult * (i * xc)


def _blockdiag(xb16, w_ref_val, d):
    outs = []
    for n in range(LRU_BLOCKS):
        outs.append(jnp.dot(xb16[:, n * LRU_BW:(n + 1) * LRU_BW], w_ref_val[d * LRU_BLOCKS + n],
                            preferred_element_type=F32))
    return jnp.concatenate(outs, axis=1)


def gates_fwd(xc, w_a, w_x, b_a, b_x, sp, n_rows, t_lat, tm):
    def fn(is_ctx, rows, params):
        (x,), (wa, wx, ba, bx, spv) = rows, params
        xb16 = x.astype(BF16)
        outs = []
        for d in range(2):
            a, u = _gate_elem(_blockdiag(xb16, wa, d), _blockdiag(xb16, wx, d), x,
                              ba[d:d + 1], bx[d:d + 1], spv[d:d + 1])
            outs += [a, u]
        return outs, []

    (a_f, u_f, a_b, u_b), _ = rowwise("gates_fwd", fn, [(xc, 0, LRU_W)], [w_a, w_x, b_a, b_x, sp],
                                      [(LRU_W, F32)] * 4, [], n_rows, t_lat, tm)
    return a_f, u_f, a_b, u_b


def gates_bwd(xc, da_f, du_f, da_b, du_b, w_a, w_x, b_a, b_x, sp, n_rows, t_lat, tm):
    def fn(is_ctx, rows, params):
        (x, daf, duf, dab, dub), (wa, wx, ba, bx, spv) = rows, params
        xb16 = x.astype(BF16)
        dxc = jnp.zeros_like(x)
        dwa, dwx, dba, dbx, dsp = [], [], [], [], []
        for d, (da, du) in enumerate(((daf, duf), (dab, dub))):
            pre_r, pre_i = _blockdiag(xb16, wa, d), _blockdiag(xb16, wx, d)
            _, vjp = jax.vjp(_gate_elem, pre_r, pre_i, x, ba[d:d + 1], bx[d:d + 1], spv[d:d + 1])
            dpr, dpi, dx_e, dba_d, dbx_d, dsp_d = vjp((da, du))
            dxc = dxc + dx_e
            dpr16, dpi16 = dpr.astype(BF16), dpi.astype(BF16)
            back = []
            for n in range(LRU_BLOCKS):
                sl = slice(n * LRU_BW, (n + 1) * LRU_BW)
                nt_dims = (((1,), (1,)), ((), ()))
                back.append(lax.dot_general(dpr16[:, sl], wa[d * LRU_BLOCKS + n], nt_dims, preferred_element_type=F32)
                            + lax.dot_general(dpi16[:, sl], wx[d * LRU_BLOCKS + n], nt_dims,
                                              preferred_element_type=F32))
                tn_dims = (((0,), (0,)), ((), ()))
                dwa.append(lax.dot_general(xb16[:, sl], dpr16[:, sl], tn_dims, preferred_element_type=F32)[None])
                dwx.append(lax.dot_general(xb16[:, sl], dpi16[:, sl], tn_dims, preferred_element_type=F32)[None])
            dxc = dxc + jnp.concatenate(back, axis=1)
            dba.append(dba_d)
            dbx.append(dbx_d)
            dsp.append(dsp_d)
        cat0 = lambda xs: jnp.concatenate(xs, axis=0)
        return [dxc], [cat0(dwa), cat0(dwx), cat0(dba), cat0(dbx), cat0(dsp)]

    (dxc,), accs = rowwise("gates_bwd", fn,
                           [(xc, 0, LRU_W), (da_f, 0, LRU_W), (du_f, 0, LRU_W), (da_b, 0, LRU_W), (du_b, 0, LRU_W)],
                           [w_a, w_x, b_a, b_x, sp], [(LRU_W, F32)],
                           [(2 * LRU_BLOCKS, LRU_BW, LRU_BW)] * 2 + [(2, LRU_W)] * 3, n_rows, t_lat, tm)
    return dxc, accs


def _rope_tables(t_lat, n_rows):
    rows = t_lat // GRID_W
    row_ids = jnp.repeat(jnp.arange(rows), GRID_W).astype(F32)
    col_ids = jnp.tile(jnp.arange(GRID_W), rows).astype(F32)
    axis_dim = QK_ROPE // 2
    inv = 1.0 / (ROPE_BASE ** (jnp.arange(0, axis_dim, 2, dtype=F32) / axis_dim))
    ang = jnp.concatenate([row_ids[:, None] * inv, col_ids[:, None] * inv], axis=-1)
    cos, sin = jnp.cos(ang), jnp.sin(ang)
    half = QK_ROPE // 2
    ones, zeros = jnp.ones((t_lat, QK_NOPE), F32), jnp.zeros((t_lat, QK_NOPE), F32)
    pad1, pad0 = jnp.ones((t_lat, HEAD_PAD - QK_DIM), F32), jnp.zeros((t_lat, HEAD_PAD - QK_DIM), F32)
    zh = jnp.zeros((t_lat, half), F32)
    c_tab = jnp.concatenate([ones, cos, cos, pad1], axis=1)
    s1 = jnp.concatenate([zeros, -sin, zh, pad0], axis=1)
    s2 = jnp.concatenate([zeros, zh, sin, pad0], axis=1)
    n_ctx = n_rows - t_lat
    c_tab = jnp.concatenate([c_tab, jnp.ones((n_ctx, HEAD_PAD), F32)], axis=0)
    s1 = jnp.concatenate([s1, jnp.zeros((n_ctx, HEAD_PAD), F32)], axis=0)
    s2 = jnp.concatenate([s2, jnp.zeros((n_ctx, HEAD_PAD), F32)], axis=0)
    return c_tab, s1, s2


def _rope(x, c, s1, s2):
    half = QK_ROPE // 2
    return x * c + pltpu.roll(x, HEAD_PAD - half, 1) * s1 + pltpu.roll(x, half, 1) * s2


def _rope_t(dy, c, s1, s2):
    half = QK_ROPE // 2
    return dy * c + pltpu.roll(dy * s1, half, 1) + pltpu.roll(dy * s2, HEAD_PAD - half, 1)


def _heads(x):
    return [x[:, h * HEAD_PAD:(h + 1) * HEAD_PAD] for h in range(N_HEADS)]


def attn_fwd(q, k, v, t_lat, n_rows, tq):
    scale = QK_DIM ** -0.5

    def body(q_ref, k_ref, v_ref, o_ref, lse_ref):
        s = lax.dot_general(q_ref[...], k_ref[...], (((1,), (1,)), ((), ())), preferred_element_type=F32) * scale
        m = jnp.max(s, axis=-1, keepdims=True)
        p = jnp.exp(s - m)
        l = jnp.sum(p, axis=-1, keepdims=True)
        o = jnp.dot(p.astype(BF16), v_ref[...], preferred_element_type=F32) / l
        o_ref[...] = o.astype(o_ref.dtype)
        lse_ref[...] = jnp.broadcast_to(m + jnp.log(l), lse_ref.shape)

    qspec = pl.BlockSpec((tq, HEAD_PAD), lambda h, i: (i, h))
    kspec = pl.BlockSpec((n_rows, HEAD_PAD), lambda h, i: (0, h))
    return pl.pallas_call(
        body, name="attn_fwd", grid=(N_HEADS, t_lat // tq),
        out_shape=[jax.ShapeDtypeStruct((t_lat, N_HEADS * HEAD_PAD), BF16),
                   jax.ShapeDtypeStruct((t_lat, N_HEADS * HEAD_PAD), F32)],
        in_specs=[qspec, kspec, kspec], out_specs=[qspec, qspec],
        compiler_params=_cparams(("parallel", "arbitrary")),
    )(q, k, v)


def attn_bwd(q, k, v, o, do, lse, t_lat, n_rows, tq):
    scale = QK_DIM ** -0.5
    nt = (((1,), (1,)), ((), ()))
    tn = (((0,), (0,)), ((), ()))

    def body(q_ref, k_ref, v_ref, o_ref, do_ref, lse_ref, dq_ref, dk_ref, dv_ref):
        @pl.when(pl.program_id(1) == 0)
        def _():
            dk_ref[...] = jnp.zeros_like(dk_ref)
            dv_ref[...] = jnp.zeros_like(dv_ref)

        qv, kv, vv, dov = q_ref[...], k_ref[...], v_ref[...], do_ref[...]
        s = lax.dot_general(qv, kv, nt, preferred_element_type=F32) * scale
        p = jnp.exp(s - lse_ref[:, 0:1])
        dv_ref[...] += lax.dot_general(p.astype(BF16), dov, tn, preferred_element_type=F32)
        dp = lax.dot_general(dov, vv, nt, preferred_element_type=F32)
        delta = jnp.sum(dov.astype(F32) * o_ref[...].astype(F32), axis=-1, keepdims=True)
        ds = (p * (dp - delta) * scale).astype(BF16)
        dq_ref[...] = jnp.dot(ds, kv, preferred_element_type=F32)
        dk_ref[...] += lax.dot_general(ds, qv, tn, preferred_element_type=F32)

    qspec = pl.BlockSpec((tq, HEAD_PAD), lambda h, i: (i, h))
    kspec = pl.BlockSpec((n_rows, HEAD_PAD), lambda h, i: (0, h))
    return pl.pallas_call(
        body, name="attn_bwd", grid=(N_HEADS, t_lat // tq),
        out_shape=[jax.ShapeDtypeStruct((t_lat, N_HEADS * HEAD_PAD), F32),
                   jax.ShapeDtypeStruct((n_rows, N_HEADS * HEAD_PAD), F32),
                   jax.ShapeDtypeStruct((n_rows, N_HEADS * HEAD_PAD), F32)],
        in_specs=[qspec, kspec, kspec, qspec, qspec, qspec], out_specs=[qspec, kspec, kspec],
        compiler_params=_cparams(("parallel", "arbitrary")),
    )(q, k, v, o, do, lse)


def adamw(w, g, m, v):
    r, ccols = w.shape
    tr = _best_div(r, 8, 1024)
    c1 = 1.0 - ADAM_B1 ** ADAM_STEP
    c2 = 1.0 - ADAM_B2 ** ADAM_STEP

    def body(w_ref, g_ref, m_ref, v_ref, d_ref, nm_ref, nv_ref):
        gv = g_ref[...]
        nm = ADAM_B1 * m_ref[...] + (1.0 - ADAM_B1) * gv
        nv = ADAM_B2 * v_ref[...] + (1.0 - ADAM_B2) * (gv * gv)
        d_ref[...] = -ADAM_LR * ((nm / c1) / (jnp.sqrt(nv / c2) + ADAM_EPS) + ADAM_WD * w_ref[...])
        nm_ref[...] = nm
        nv_ref[...] = nv

    spec = pl.BlockSpec((tr, ccols), lambda i: (i, 0))
    return pl.pallas_call(
        body, name="adamw", grid=(r // tr,),
        out_shape=[jax.ShapeDtypeStruct((r, ccols), F32)] * 3,
        in_specs=[spec] * 4, out_specs=[spec] * 3,
        compiler_params=_cparams(("parallel",)),
    )(w, g, m, v)


def _flat(parts, dtype, row_mult):
    v = jnp.concatenate([p.reshape(-1).astype(dtype) for p in parts])
    quantum = row_mult * FLAT_C
    total = -(-v.shape[0] // quantum) * quantum
    return jnp.pad(v, (0, total - v.shape[0])).reshape(total // FLAT_C, FLAT_C)


def _unflat(flat, shapes):
    v = flat.reshape(-1)
    out, at = [], 0
    for s in shapes:
        n = math.prod(s)
        out.append(v[at:at + n].reshape(s))
        at += n
    return out


def _gathered_to_full(name, g):
    if name in ROW_SHARDED:
        return g.reshape((g.shape[0] * g.shape[1],) + g.shape[2:])
    k = g.shape[1]
    return jnp.transpose(g, (1, 0, 2)).reshape(k, N_DEV * g.shape[2])


def _full_to_chunks(name, full):
    if name in ROW_SHARDED:
        return full.reshape(N_DEV, -1)
    k, n = full.shape
    return jnp.transpose(full.reshape(k, N_DEV, n // N_DEV), (1, 0, 2)).reshape(N_DEV, -1)


def _pad_w_in(w_in):
    k = w_in.shape[0]
    z = lambda n: jnp.zeros((k, n), w_in.dtype)
    return jnp.concatenate([w_in[:, :640], z(QK_NOPE), w_in[:, 640:672], z(HEAD_PAD - QK_DIM), w_in[:, 672:]], axis=1)


def _unpad_w_in(g):
    return jnp.concatenate([g[:, :640], g[:, 640 + QK_NOPE:640 + QK_DIM], g[:, Z_XB:]], axis=1)


def _pad_w_uq(w):
    return jnp.pad(w.reshape(Q_RANK, N_HEADS, QK_DIM), ((0, 0), (0, 0), (0, HEAD_PAD - QK_DIM))).reshape(Q_RANK, -1)


def _unpad_w_uq(g):
    return g.reshape(Q_RANK, N_HEADS, HEAD_PAD)[:, :, :QK_DIM].reshape(Q_RANK, N_HEADS * QK_DIM)


def _pad_w_ukv(w):
    w3 = w.reshape(KV_RANK, N_HEADS, QK_NOPE + V_HEAD)
    pad = lambda t: jnp.pad(t, ((0, 0), (0, 0), (0, HEAD_PAD - t.shape[2]))).reshape(KV_RANK, -1)
    return jnp.concatenate([pad(w3[:, :, :QK_NOPE]), pad(w3[:, :, QK_NOPE:])], axis=1)


def _unpad_w_ukv(g):
    half = N_HEADS * HEAD_PAD
    gk = g[:, :half].reshape(KV_RANK, N_HEADS, HEAD_PAD)[:, :, :QK_NOPE]
    gv = g[:, half:].reshape(KV_RANK, N_HEADS, HEAD_PAD)[:, :, :V_HEAD]
    return jnp.concatenate([gk, gv], axis=2).reshape(KV_RANK, -1)


def _pad_w_o_attn(w):
    return jnp.pad(w.reshape(N_HEADS, V_HEAD, D), ((0, 0), (0, HEAD_PAD - V_HEAD), (0, 0))).reshape(-1, D)


def _unpad_w_o_attn(g):
    return g.reshape(N_HEADS, HEAD_PAD, D)[:, :V_HEAD].reshape(N_HEADS * V_HEAD, D)


def local_step(x, ctx, target, mod_l, mod_c, wt):
    t_lat, n_ctx = x.shape[0], ctx.shape[0]
    n = t_lat + n_ctx
    tm = _pick(math.gcd(t_lat, n), (256, 128))
    tq = _pick(t_lat, (256, 128))
    row = lambda v: v.reshape(1, -1).astype(F32)
    two = lambda a, b: jnp.stack([a, b]).astype(F32)
    sh1_l, sc1_l, g1_l, sh2_l, sc2_l, g2_l = jnp.split(mod_l, 6)
    sh1_c, sc1_c = jnp.split(mod_c, 6)[:2]
    sc1, sh1 = two(sc1_l, sc1_c), two(sh1_l, sh1_c)
    g1, g2, sc2, sh2 = row(g1_l), row(g2_l), row(sc2_l), row(sh2_l)
    norm1_g, norm2_g, final_g = row(wt['norm1_g']), row(wt['norm2_g']), row(wt['final_g'])
    q_g, kv_g, b_gate = row(wt['q_norm_g']), row(wt['kv_norm_g']), row(wt['b_gate'])
    w_in, w_uq, w_ukv = _pad_w_in(wt['w_in']), _pad_w_uq(wt['w_uq']), _pad_w_ukv(wt['w_ukv'])
    w_o_attn = _pad_w_o_attn(wt['w_o_attn'])
    w_o_lru, w_out, w_up, w_down = wt['w_o_lru'], wt['w_out'], wt['w_up'], wt['w_down']
    lru_w_a = wt['lru_w_a'].reshape(2 * LRU_BLOCKS, LRU_BW, LRU_BW).astype(BF16)
    lru_w_x = wt['lru_w_x'].reshape(2 * LRU_BLOCKS, LRU_BW, LRU_BW).astype(BF16)
    b_a, b_x, lam = wt['lru_b_a'], wt['lru_b_x'], wt['lru_lambda']
    sp = jnp.logaddexp(-lam, 0.0)
    c_tab, s1_tab, s2_tab = _rope_tables(t_lat, n)
    rw = functools.partial(rowwise, n_rows=n, t_lat=t_lat, tm=tm)
    rw_lat = functools.partial(rowwise, n_rows=t_lat, t_lat=t_lat, tm=tm)

    xs = jnp.concatenate([x, ctx], axis=0)

    def f_norm1(is_ctx, rows, params):
        (xv,), (g, sc, sh) = rows, params
        return [_norm_mod(xv, g, _sel(is_ctx, sc), _sel(is_ctx, sh))], []

    (h,), _ = rw("norm1", f_norm1, [(xs, 0, D)], [norm1_g, sc1, sh1], [(D, BF16)], [])
    z = matmul("w_in", h, w_in, 'nn', F32)

    def f_qkv_norm(is_ctx, rows, params):
        (ql, kvl), (gq, gkv) = rows, params
        return [_rms(ql, gq), _rms(kvl, gkv)], []

    (qn, kvn), _ = rw("qkv_norm", f_qkv_norm, [(z, Z_Q, Q_RANK), (z, Z_KV, KV_RANK)], [q_g, kv_g],
                      [(Q_RANK, BF16), (KV_RANK, BF16)], [])
    qp = matmul("w_uq", qn, w_uq, 'nn', F32)
    kvp = matmul("w_ukv", kvn, w_ukv, 'nn', F32)

    def f_rope(is_ctx, rows, params):
        qv, kk, vv, kr, c, s1, s2 = rows
        krr = _rope(kr, c, s1, s2)
        qo = jnp.concatenate([_rope(qh, c, s1, s2) for qh in _heads(qv)], axis=1)
        ko = jnp.concatenate([kh + krr for kh in _heads(kk)], axis=1)
        return [qo, ko, vv], []

    hp = N_HEADS * HEAD_PAD
    (qr, kr_, vr), _ = rw("rope", f_rope,
                          [(qp, 0, hp), (kvp, 0, hp), (kvp, hp, hp), (z, Z_KR, HEAD_PAD), (c_tab, 0, HEAD_PAD),
                           (s1_tab, 0, HEAD_PAD), (s2_tab, 0, HEAD_PAD)], [], [(hp, BF16)] * 3, [])
    attn, lse = attn_fwd(qr, kr_, vr, t_lat, n, tq)

    xc = conv_fwd("lru_conv", z, Z_XB, LRU_W, wt['lru_conv_w'], row(wt['lru_conv_b']), 2, n, t_lat)
    a_f, u_f, a_b, u_b = gates_fwd(xc, lru_w_a, lru_w_x, b_a, b_x, sp, n, t_lat, tm)
    h_f, hp_f = scan_fwd("scan_f", a_f, u_f, 'f', n, t_lat)
    h_b, hp_b = scan_fwd("scan_b", a_b, u_b, 'b', n, t_lat)

    def f_lru_out(is_ctx, rows, params):
        hf, hb, yb = rows
        return [(hf + hb) * _gelu(yb)], []

    (ybin,), _ = rw_lat("lru_out", f_lru_out, [(h_f, 0, LRU_W), (h_b, 0, LRU_W), (z, Z_YB, LRU_W)], [],
                        [(LRU_W, BF16)], [])
    y_a = matmul("w_o_attn", attn, w_o_attn, 'nn', F32)
    y_b = matmul("w_o_lru", ybin, w_o_lru, 'nn', F32)

    def _merge(ya, yb, gl, bg):
        gates = _sigmoid(gl + bg)
        return gates[:, :D] * ya + gates[:, D:] * yb

    def f_merge(is_ctx, rows, params):
        (ya, yb, gl), (bg,) = rows, params
        return [_merge(ya, yb, gl, bg)], []

    (mrg,), _ = rw_lat("merge", f_merge, [(y_a, 0, D), (y_b, 0, D), (z, Z_GL, 2 * D)], [b_gate], [(D, BF16)], [])
    o = matmul("w_out", mrg, w_out, 'nn', F32)

    def _res_norm2(xv, ov, g1v, g, sc, sh):
        x1 = xv + g1v * ov
        return x1, _norm_mod(x1, g, sc, sh)

    def f_norm2(is_ctx, rows, params):
        (xv, ov), (g1v, g, sc, sh) = rows, params
        x1, h2v = _res_norm2(xv, ov, g1v, g, sc, sh)
        return [x1, h2v], []

    (x1, h2), _ = rw_lat("norm2", f_norm2, [(x, 0, D), (o, 0, D)], [g1, norm2_g, sc2, sh2], [(D, F32), (D, BF16)], [])
    u = matmul("w_up", h2, w_up, 'nn', F32)
    ac = conv_fwd("ffn_conv", u, 0, FFN, wt['ffn_conv_w'], row(wt['ffn_conv_b']), 1, t_lat, t_lat)

    def f_ffn_act(is_ctx, rows, params):
        acv, gv = rows
        return [_silu(acv) * gv], []

    (f,), _ = rw_lat("ffn_act", f_ffn_act, [(ac, 0, FFN), (u, FFN, FFN)], [], [(FFN, BF16)], [])
    dn = matmul("w_down", f, w_down, 'nn', F32)

    def _tile_loss(x1v, dv, g2v, fg, tgt):
        y = _rms(x1v + g2v * dv, fg)
        e = y - tgt
        return 0.5 * jnp.sum(jnp.mean(e * e, axis=-1, keepdims=True), axis=0, keepdims=True)

    def f_final(is_ctx, rows, params):
        (x1v, dv, tgt), (g2v, fg) = rows, params
        lv, vjp = jax.vjp(lambda a, b, c, d: _tile_loss(a, b, c, d, tgt), x1v, dv, g2v, fg)
        dx2, dd, dg2, dfg = vjp(jnp.ones((1, 1), F32))
        return [dx2, dd], [dg2, dfg, jnp.broadcast_to(lv, (1, 128))]

    (dx2, dd), (dg2, dfinal_g, loss_v) = rw_lat("final", f_final, [(x1, 0, D), (dn, 0, D), (target, 0, D)],
                                                [g2, final_g], [(D, F32), (D, BF16)], [(1, D), (1, D), (1, 128)])
    loss = loss_v[0, 0]

    grads = {'final_g': dfinal_g}
    df = matmul("d_f", dd, w_down, 'nt', F32)
    grads['w_down'] = matmul("g_w_down", f, dd, 'tn', F32)

    def b_ffn_act(is_ctx, rows, params):
        acv, gv, dfv = rows
        _, vjp = jax.vjp(lambda a, g: _silu(a) * g, acv, gv)
        dac, dg = vjp(dfv)
        return [dac, dg], []

    (dac, dgate), _ = rw_lat("ffn_act_bwd", b_ffn_act, [(ac, 0, FFN), (u, FFN, FFN), (df, 0, FFN)], [],
                             [(FFN, F32), (FFN, BF16)], [])
    da, grads['ffn_conv_w'], grads['ffn_conv_b'] = conv_bwd("ffn_conv_bwd", dac, u, 0, FFN, wt['ffn_conv_w'], 1,
                                                            t_lat, t_lat)
    du = jnp.concatenate([da, dgate], axis=1)
    dh2 = matmul("d_h2", du, w_up, 'nt', F32)
    grads['w_up'] = matmul("g_w_up", h2, du, 'tn', F32)

    def b_norm2(is_ctx, rows, params):
        (xv, ov, dh2v, dx2v), (g1v, g, sc, sh) = rows, params
        _, vjp = jax.vjp(_res_norm2, xv, ov, g1v, g, sc, sh)
        dx, do, dg1v, dg, dsc, dsh = vjp((dx2v, dh2v))
        return [dx, do], [dg1v, dg, dsc, dsh]

    (dx_res, do), (dg1, dnorm2_g, dsc2, dsh2) = rw_lat(
        "norm2_bwd", b_norm2, [(x, 0, D), (o, 0, D), (dh2, 0, D), (dx2, 0, D)], [g1, norm2_g, sc2, sh2],
        [(D, F32), (D, BF16)], [(1, D)] * 4)
    grads['norm2_g'] = dnorm2_g
    dmrg = matmul("d_merge", do, w_out, 'nt', F32)
    grads['w_out'] = matmul("g_w_out", mrg, do, 'tn', F32)

    def b_merge(is_ctx, rows, params):
        (ya, yb, gl, dm), (bg,) = rows, params
        _, vjp = jax.vjp(_merge, ya, yb, gl, bg)
        dya, dyb, dgl, dbg = vjp(dm)
        return [dya, dyb, dgl], [dbg]

    (dy_a, dy_b, dgl), (grads['b_gate'],) = rw_lat(
        "merge_bwd", b_merge, [(y_a, 0, D), (y_b, 0, D), (z, Z_GL, 2 * D), (dmrg, 0, D)], [b_gate],
        [(D, BF16), (D, BF16), (2 * D, BF16)], [(1, 2 * D)])
    dattn = matmul("d_attn", dy_a, w_o_attn, 'nt', BF16)
    grads['w_o_attn'] = _unpad_w_o_attn(matmul("g_w_o_attn", attn, dy_a, 'tn', F32))
    dybin = matmul("d_lru_out", dy_b, w_o_lru, 'nt', F32)
    grads['w_o_lru'] = matmul("g_w_o_lru", ybin, dy_b, 'tn', F32)

    def b_lru_out(is_ctx, rows, params):
        hf, hb, yb, dyv = rows
        _, vjp = jax.vjp(lambda s, y: s * _gelu(y), hf + hb, yb)
        dh, dyb = vjp(dyv)
        return [dh, dyb], []

    (dh_lru, dyb), _ = rw_lat("lru_out_bwd", b_lru_out,
                              [(h_f, 0, LRU_W), (h_b, 0, LRU_W), (z, Z_YB, LRU_W), (dybin, 0, LRU_W)], [],
                              [(LRU_W, F32), (LRU_W, BF16)], [])
    du_f, da_f = scan_adj("scan_f_adj", a_f, dh_lru, hp_f, 'f', n, t_lat)
    du_b, da_b = scan_adj("scan_b_adj", a_b, dh_lru, hp_b, 'b', n, t_lat)
    dxc, (dw_a, dw_x, db_a, db_x, dsp) = gates_bwd(xc, da_f, du_f, da_b, du_b, lru_w_a, lru_w_x, b_a, b_x, sp,
                                                   n, t_lat, tm)
    grads['lru_w_a'] = dw_a.reshape(2, LRU_BLOCKS, LRU_BW, LRU_BW)
    grads['lru_w_x'] = dw_x.reshape(2, LRU_BLOCKS, LRU_BW, LRU_BW)
    grads['lru_b_a'], grads['lru_b_x'] = db_a, db_x
    grads['lru_lambda'] = -dsp * _sigmoid(-lam)
    dxb, grads['lru_conv_w'], grads['lru_conv_b'] = conv_bwd("lru_conv_bwd", dxc, z, Z_XB, LRU_W, wt['lru_conv_w'],
                                                             2, n, t_lat)

    dq, dk, dv = attn_bwd(qr, kr_, vr, attn, dattn, lse, t_lat, n, tq)

    def b_rope(is_ctx, rows, params):
        dqv, dkv, dvv, c, s1, s2 = rows
        live = jnp.where(is_ctx, 0.0, 1.0)
        dqo = jnp.concatenate([_rope_t(dqh, c, s1, s2) for dqh in _heads(dqv)], axis=1) * live
        dkh = _heads(dkv)
        dkr = dkh[0]
        for t in dkh[1:]:
            dkr = dkr + t
        lanes = lax.broadcasted_iota(jnp.int32, dkr.shape, 1)
        dkr = jnp.where((lanes >= QK_NOPE) & (lanes < QK_DIM), _rope_t(dkr, c, s1, s2), 0.0)
        return [dqo, jnp.concatenate([dkv, dvv], axis=1), dkr], []

    (dqp, dkvp, dkr), _ = rw("rope_bwd", b_rope,
                             [(dq, 0, hp), (dk, 0, hp), (dv, 0, hp), (c_tab, 0, HEAD_PAD), (s1_tab, 0, HEAD_PAD),
                              (s2_tab, 0, HEAD_PAD)], [], [(hp, BF16), (2 * hp, BF16), (HEAD_PAD, BF16)], [])
    dqn = matmul("d_qn", dqp, w_uq, 'nt', F32)
    grads['w_uq'] = _unpad_w_uq(matmul("g_w_uq", qn, dqp, 'tn', F32))
    dkvn = matmul("d_kvn", dkvp, w_ukv, 'nt', F32)
    grads['w_ukv'] = _unpad_w_ukv(matmul("g_w_ukv", kvn, dkvp, 'tn', F32))

    def b_qkv_norm(is_ctx, rows, params):
        (ql, kvl, dqv, dkvv), (gq, gkv) = rows, params
        _, vjp_q = jax.vjp(_rms, ql, gq)
        _, vjp_kv = jax.vjp(_rms, kvl, gkv)
        dql, dgq = vjp_q(dqv)
        dkvl, dgkv = vjp_kv(dkvv)
        return [dql, dkvl], [dgq, dgkv]

    (dq_lat, dkv_lat), (grads['q_norm_g'], grads['kv_norm_g']) = rw(
        "qkv_norm_bwd", b_qkv_norm, [(z, Z_Q, Q_RANK), (z, Z_KV, KV_RANK), (dqn, 0, Q_RANK), (dkvn, 0, KV_RANK)],
        [q_g, kv_g], [(Q_RANK, BF16), (KV_RANK, BF16)], [(1, Q_RANK), (1, KV_RANK)])
    pad_ctx = lambda t: jnp.pad(t, ((0, n_ctx), (0, 0)))
    dz = jnp.concatenate([dq_lat, dkv_lat, dkr, dxb, pad_ctx(dyb), pad_ctx(dgl)], axis=1)
    dh = matmul("d_h", dz, w_in, 'nt', F32)
    grads['w_in'] = _unpad_w_in(matmul("g_w_in", h, dz, 'tn', F32))

    def b_norm1(is_ctx, rows, params):
        (xv, dhv, dxr), (g, sc, sh) = rows, params
        scv, shv = _sel(is_ctx, sc), _sel(is_ctx, sh)
        _, vjp = jax.vjp(_norm_mod, xv, g, scv, shv)
        dx, dg, dsc, dsh = vjp(dhv)
        return [dx + dxr], [dg, _seg_acc(is_ctx, dsc), _seg_acc(is_ctx, dsh)]

    (dxs,), (grads['norm1_g'], dsc1, dsh1) = rw("norm1_bwd", b_norm1, [(xs, 0, D), (dh, 0, D), (dx_res, 0, D)],
                                                [norm1_g, sc1, sh1], [(D, F32)], [(1, D), (2, D), (2, D)])
    grad_x = dxs[:t_lat]
    zero = jnp.zeros((D,), F32)
    dmod_l = jnp.concatenate([dsh1[0], dsc1[0], dg1[0], dsh2[0], dsc2[0], dg2[0]])
    dmod_c = jnp.concatenate([dsh1[1], dsc1[1], zero, zero, zero, zero])
    return loss, grad_x, grads, dmod_l, dmod_c


def kernel(x, c, ctx, c_ctx, w_mod, b_mod, norm1_g, w_in, b_gate, q_norm_g, kv_norm_g, w_uq, w_ukv, w_o_attn, lru_conv_w, lru_conv_b, lru_w_a, lru_b_a, lru_w_x, lru_b_x, lru_lambda, w_o_lru, w_out, norm2_g, w_up, ffn_conv_w, ffn_conv_b, w_down, final_g, loss_target, m_c_ctx, m_w_mod, m_b_mod, m_norm1_g, m_w_in, m_b_gate, m_q_norm_g, m_kv_norm_g, m_w_uq, m_w_ukv, m_w_o_attn, m_lru_conv_w, m_lru_conv_b, m_lru_w_a, m_lru_b_a, m_lru_w_x, m_lru_b_x, m_lru_lambda, m_w_o_lru, m_w_out, m_norm2_g, m_w_up, m_ffn_conv_w, m_ffn_conv_b, m_w_down, m_final_g, v_c_ctx, v_w_mod, v_b_mod, v_norm1_g, v_w_in, v_b_gate, v_q_norm_g, v_kv_norm_g, v_w_uq, v_w_ukv, v_w_o_attn, v_lru_conv_w, v_lru_conv_b, v_lru_w_a, v_lru_b_a, v_lru_w_x, v_lru_b_x, v_lru_lambda, v_w_o_lru, v_w_out, v_norm2_g, v_w_up, v_ffn_conv_w, v_ffn_conv_b, v_w_down, v_final_g):
    given = dict(locals())
    strip = lambda name, a: a if name in ('c_ctx', 'final_g') else a[0]
    wsh = {n: strip(n, given[n]) for n in WEIGHTS}
    msh = {n: strip(n, given['m_' + n]) for n in WEIGHTS}
    vsh = {n: strip(n, given['v_' + n]) for n in WEIGHTS}
    me = _my_index()

    small = _flat([c[0]] + [wsh[n] for n in SMALL_F32], F32, 8)
    small_all = all_gather("gather_small", small)
    shapes = [(D,)] + [wsh[n].shape for n in SMALL_F32]
    per_dev = [_unflat(small_all[p], shapes) for p in range(N_DEV)]
    c_all = jnp.stack([pd[0] for pd in per_dev])
    full = {}
    for i, n in enumerate(SMALL_F32):
        full[n] = _gathered_to_full(n, jnp.stack([pd[1 + i] for pd in per_dev]))

    cond = jnp.concatenate([c_all, c_ctx[None], jnp.zeros((7, D), F32)], axis=0)
    sil = cond * jax.nn.sigmoid(cond)
    mod_cols = matmul("mod_proj", sil, wsh['w_mod'], 'nn', F32)
    mod_all = all_gather("gather_mod", mod_cols)
    mod_all = jnp.transpose(mod_all, (1, 0, 2)).reshape(16, 6 * D) + b_mod[0][None]
    mod_l = lax.dynamic_index_in_dim(mod_all, me, axis=0, keepdims=False)
    mod_c = mod_all[N_DEV]

    big = _flat([wsh[n] for n in BIG_BF16], BF16, BIG_ROWS)
    big_all = all_gather("gather_weights", big)
    big_shapes = [wsh[n].shape for n in BIG_BF16]
    per_dev = [_unflat(big_all[p], big_shapes) for p in range(N_DEV)]
    for i, n in enumerate(BIG_BF16):
        full[n] = _gathered_to_full(n, jnp.stack([pd[i] for pd in per_dev]))
    for n in REPLICATED:
        if n not in ('c_ctx', 'b_mod'):
            full[n] = wsh[n]

    loss, grad_x, grads, dmod_l, dmod_c = local_step(x[0], ctx[0], loss_target[0], mod_l, mod_c, full)
    loss = lax.psum(loss, ("x", "y", "c"))

    dmod = _flat([dmod_l, dmod_c], F32, 8)
    dmod_all = all_gather("gather_dmod", dmod)
    dm = jnp.stack([jnp.stack(_unflat(dmod_all[p], [(6 * D,), (6 * D,)])) for p in range(N_DEV)])
    dmod_c_tot = dm[0, 1]
    for p in range(1, N_DEV):
        dmod_c_tot = dmod_c_tot + dm[p, 1]
    dm16 = jnp.concatenate([dm[:, 0], dmod_c_tot[None], jnp.zeros((7, 6 * D), F32)], axis=0)
    ncol = 6 * D // N_DEV
    dm16_cols = lax.dynamic_slice_in_dim(dm16.reshape(16, N_DEV, ncol), me, 1, axis=1)[:, 0]
    grad_w_mod = matmul("g_w_mod", sil, dm16_cols, 'tn', F32)
    dsil = matmul("d_cond", dm16_cols, wsh['w_mod'], 'nt', F32)
    sg = jax.nn.sigmoid(c_ctx)
    grads['c_ctx'] = dsil[N_DEV] * (sg * (1.0 + c_ctx * (1.0 - sg)))
    grads['b_mod'] = dmod_l + dmod_c

    rep_sizes = [math.prod(wsh[n].shape) for n in REPLICATED]
    chunks = [_full_to_chunks(n, grads[n].reshape(full[n].shape)) for n in SHARDED]
    chunks += [grads[n].reshape(N_DEV, -1) for n in REPLICATED]
    chunk_len = sum(ch.shape[1] for ch in chunks)
    quantum = BIG_ROWS * FLAT_C
    padded = -(-chunk_len // quantum) * quantum
    send = jnp.concatenate(chunks, axis=1)
    send = jnp.pad(send, ((0, 0), (0, padded - chunk_len))).astype(BF16).reshape(N_DEV, padded // FLAT_C, FLAT_C)
    got = all_to_all("grad_exchange", send)
    gsum = sum_slots("grad_sum", got).reshape(-1)
    n_sharded = sum(math.prod(wsh[n].shape) for n in SHARDED)
    rep_part = _flat([gsum[n_sharded:chunk_len]], F32, 8)
    rep_all = all_gather("gather_replicated_grads", rep_part).reshape(N_DEV, -1)
    g_final = {}
    at = 0
    for n in SHARDED:
        cnt = math.prod(wsh[n].shape)
        g_final[n] = gsum[at:at + cnt].reshape(wsh[n].shape)
        at += cnt
    at = 0
    for n, cnt in zip(REPLICATED, rep_sizes):
        g_final[n] = rep_all[:, at:at + cnt // N_DEV].reshape(wsh[n].shape)
        at += cnt // N_DEV
    g_final['w_mod'] = grad_w_mod

    w_flat = _flat([wsh[n] for n in WEIGHTS], F32, BIG_ROWS)
    g_flat = _flat([g_final[n] for n in WEIGHTS], F32, BIG_ROWS)
    m_flat = _flat([msh[n] for n in WEIGHTS], F32, BIG_ROWS)
    v_flat = _flat([vsh[n] for n in WEIGHTS], F32, BIG_ROWS)
    d_flat, nm_flat, nv_flat = adamw(w_flat, g_flat, m_flat, v_flat)
    out_shapes = [given[n].shape for n in WEIGHTS]
    return (loss, grad_x[None],
            *[g_final[n].reshape(s) for n, s in zip(WEIGHTS, out_shapes)],
            *_unflat(d_flat, out_shapes), *_unflat(nm_flat, out_shapes), *_unflat(nv_flat, out_shapes))
```

```python
import functools
import math

import jax
import jax.numpy as jnp
from jax import lax
from jax.experimental import pallas as pl
from jax.experimental.pallas import tpu as pltpu

F32 = jnp.float32
BF16 = jnp.bfloat16
MESH = pl.DeviceIdType.MESH

N_DEV = 8
D = 1024
N_HEADS = 8
HEAD_PAD = 128
QK_NOPE, QK_ROPE, V_HEAD = 64, 32, 64
QK_DIM = QK_NOPE + QK_ROPE
Q_RANK, KV_RANK = 384, 256
LRU_W, LRU_BLOCKS, LRU_BW = 1280, 10, 128
FFN = 2816
GRID_W = 64
ROPE_BASE = 10000.0
LRU_C = 8.0
EPS = 1e-6
Z_Q, Z_KV, Z_KR, Z_XB, Z_YB, Z_GL, Z_END = 0, 384, 640, 768, 2048, 3328, 5376
ADAM_LR, ADAM_B1, ADAM_B2, ADAM_EPS, ADAM_WD, ADAM_STEP = 0.001, 0.9, 0.999, 1e-08, 0.01, 10

VMEM_LIMIT = 52 * 1024 * 1024
FLAT_C = 512
BIG_ROWS = 256

WEIGHTS = ['c_ctx', 'w_mod', 'b_mod', 'norm1_g', 'w_in', 'b_gate', 'q_norm_g', 'kv_norm_g', 'w_uq', 'w_ukv',
           'w_o_attn', 'lru_conv_w', 'lru_conv_b', 'lru_w_a', 'lru_b_a', 'lru_w_x', 'lru_b_x', 'lru_lambda',
           'w_o_lru', 'w_out', 'norm2_g', 'w_up', 'ffn_conv_w', 'ffn_conv_b', 'w_down', 'final_g']
COL_SHARDED = ['w_in', 'w_uq', 'w_ukv', 'w_o_attn', 'lru_conv_w', 'lru_b_a', 'lru_b_x', 'lru_lambda', 'w_up',
               'ffn_conv_w']
ROW_SHARDED = ['w_o_lru', 'w_out', 'w_down']
BIG_BF16 = ['w_in', 'w_uq', 'w_ukv', 'w_o_attn', 'w_o_lru', 'w_out', 'w_up', 'w_down']
SMALL_F32 = ['lru_conv_w', 'lru_b_a', 'lru_b_x', 'lru_lambda', 'ffn_conv_w']
SHARDED = BIG_BF16 + SMALL_F32
REPLICATED = ['c_ctx', 'b_mod', 'norm1_g', 'b_gate', 'q_norm_g', 'kv_norm_g', 'lru_conv_b', 'lru_w_a', 'lru_w_x',
              'norm2_g', 'ffn_conv_b', 'final_g']


def _cparams(sem=None):
    return pltpu.CompilerParams(dimension_semantics=sem, vmem_limit_bytes=VMEM_LIMIT)


def _pick(n, cands):
    for c in cands:
        if c <= n and n % c == 0:
            return c
    return n


def _best_div(n, mult, cap):
    best = mult
    for d in range(mult, min(n, cap) + 1, mult):
        if n % d == 0:
            best = d
    return best


ROW_TILES = (1088, 1024, 544, 512, 256, 128, 64, 32, 16, 8)
LANE_TILES = (1408, 1024, 896, 768, 640, 512, 384, 256, 128)


def _my_pos():
    return lax.axis_index("x"), lax.axis_index("y"), lax.axis_index("c")


def _my_index():
    x, y, c = _my_pos()
    return 4 * x + 2 * y + c


def all_gather_multi(name, shards):
    n_arr = len(shards)
    arrays = range(n_arr)

    def body(*refs):
        x_refs, out_refs = refs[:n_arr], refs[n_arr:2 * n_arr]
        send_sems, recv_sems, local_sems = refs[2 * n_arr:]
        x, y, c = _my_pos()
        me, sibling = (x, y, c), (x, y, 1 - c)
        chips = [(1 - x, y), (x, 1 - y), (1 - x, 1 - y)]

        def slot(a, px, py, pc):
            return out_refs[a].at[4 * px + 2 * py + pc]

        def copy(a, k, block, to, src=None):
            return pltpu.make_async_remote_copy(
                src_ref=slot(a, *block) if src is None else src, dst_ref=slot(a, *block),
                send_sem=send_sems.at[7 * a + k], recv_sem=recv_sems.at[7 * a + k], device_id=to,
                device_id_type=MESH)

        mine = [pltpu.make_async_copy(x_refs[a], slot(a, *me), local_sems.at[a]) for a in arrays]
        first = [copy(a, 1 + j, me, (*chip, c), src=x_refs[a]) for j, chip in enumerate(chips) for a in arrays]
        first += [copy(a, 0, me, sibling, src=x_refs[a]) for a in arrays]
        for cp in first + mine:
            cp.start()
        passed = []
        for j, chip in enumerate(chips):
            for a in arrays:
                copy(a, 1 + j, (*chip, c), me).wait_recv()
                passed.append(copy(a, 4 + j, (*chip, c), sibling))
                passed[-1].start()
        for a in arrays:
            copy(a, 0, sibling, me).wait_recv()
            for j, chip in enumerate(chips):
                copy(a, 4 + j, (*chip, 1 - c), me).wait_recv()
        for cp in first + passed:
            cp.wait_send()
        for cp in mine:
            cp.wait()

    hbm = pl.BlockSpec(memory_space=pl.ANY)
    return pl.pallas_call(
        body, name=name,
        out_shape=[jax.ShapeDtypeStruct((N_DEV,) + s.shape, s.dtype) for s in shards],
        in_specs=[hbm] * n_arr, out_specs=[hbm] * n_arr,
        scratch_shapes=[pltpu.SemaphoreType.DMA((7 * n_arr,)), pltpu.SemaphoreType.DMA((7 * n_arr,)),
                        pltpu.SemaphoreType.DMA((n_arr,))],
    )(*shards)


def all_gather(name, shard):
    return all_gather_multi(name, [shard])[0]


def all_to_all_multi(name, chunk_arrays):
    n_arr = len(chunk_arrays)
    arrays = range(n_arr)

    def body(*refs):
        x_refs, out_refs = refs[:n_arr], refs[n_arr:2 * n_arr]
        send_sems, recv_sems, local_sems = refs[2 * n_arr:]
        x, y, c = _my_pos()
        me = 4 * x + 2 * y + c
        mine = [pltpu.make_async_copy(x_refs[a].at[me], out_refs[a].at[me], local_sems.at[a]) for a in arrays]
        sends, arrivals = [], []
        for rel in (6, 4, 2, 7, 5, 3, 1):
            dx, dy, dc = (rel >> 2) & 1, (rel >> 1) & 1, rel & 1
            px, py, pc = x ^ dx, y ^ dy, c ^ dc
            peer = 4 * px + 2 * py + pc
            for a in arrays:
                k = 7 * a + rel - 1
                sends.append(pltpu.make_async_remote_copy(
                    src_ref=x_refs[a].at[peer], dst_ref=out_refs[a].at[me],
                    send_sem=send_sems.at[k], recv_sem=recv_sems.at[k],
                    device_id=(px, py, pc), device_id_type=MESH))
                arrivals.append(pltpu.make_async_remote_copy(
                    src_ref=x_refs[a].at[peer], dst_ref=out_refs[a].at[peer],
                    send_sem=send_sems.at[k], recv_sem=recv_sems.at[k],
                    device_id=(x, y, c), device_id_type=MESH))
        for cp in sends + mine:
            cp.start()
        for cp in arrivals:
            cp.wait_recv()
        for cp in sends:
            cp.wait_send()
        for cp in mine:
            cp.wait()

    hbm = pl.BlockSpec(memory_space=pl.ANY)
    return pl.pallas_call(
        body, name=name,
        out_shape=[jax.ShapeDtypeStruct(s.shape, s.dtype) for s in chunk_arrays],
        in_specs=[hbm] * n_arr, out_specs=[hbm] * n_arr,
        scratch_shapes=[pltpu.SemaphoreType.DMA((7 * n_arr,)), pltpu.SemaphoreType.DMA((7 * n_arr,)),
                        pltpu.SemaphoreType.DMA((n_arr,))],
    )(*chunk_arrays)


def sum_slots(name, slots):
    _, r, ccols = slots.shape
    tc = _pick(ccols, (256, 128))

    def body(s_ref, o_ref):
        acc = s_ref[0].astype(F32)
        for p in range(1, N_DEV):
            acc = acc + s_ref[p].astype(F32)
        o_ref[...] = acc

    return pl.pallas_call(
        body, name=name, grid=(ccols // tc,),
        out_shape=jax.ShapeDtypeStruct((r, ccols), F32),
        in_specs=[pl.BlockSpec((N_DEV, r, tc), lambda j: (0, 0, j))],
        out_specs=pl.BlockSpec((r, tc), lambda j: (0, j)),
        compiler_params=_cparams(("parallel",)),
    )(slots)


def matmul(name, a, b, mode, out_dtype, tm=None, tn=None, tk=None):
    if mode == 'nn':
        (m, k), (k2, n) = a.shape, b.shape
    elif mode == 'nt':
        (m, k), (n, k2) = a.shape, b.shape
    else:
        (k, m), (k2, n) = a.shape, b.shape
    assert k == k2, (name, a.shape, b.shape, mode)
    if mode == 'tn':
        tm = tm or _pick(m, LANE_TILES)
        tk = tk or _pick(k, ROW_TILES)
    else:
        tm = tm or _pick(m, ROW_TILES)
        tk = tk or _pick(k, LANE_TILES)
    tn = tn or _pick(n, LANE_TILES)
    nk = k // tk
    if mode == 'nn':
        a_spec = pl.BlockSpec((tm, tk), lambda i, j, kk: (i, kk))
        b_spec = pl.BlockSpec((tk, tn), lambda i, j, kk: (kk, j))
        dn = (((1,), (0,)), ((), ()))
    elif mode == 'nt':
        a_spec = pl.BlockSpec((tm, tk), lambda i, j, kk: (i, kk))
        b_spec = pl.BlockSpec((tn, tk), lambda i, j, kk: (j, kk))
        dn = (((1,), (1,)), ((), ()))
    else:
        a_spec = pl.BlockSpec((tk, tm), lambda i, j, kk: (kk, i))
        b_spec = pl.BlockSpec((tk, tn), lambda i, j, kk: (kk, j))
        dn = (((0,), (0,)), ((), ()))

    def body(a_ref, b_ref, o_ref, acc_ref):
        kk = pl.program_id(2)

        @pl.when(kk == 0)
        def _():
            acc_ref[...] = jnp.zeros_like(acc_ref)

        acc_ref[...] += lax.dot_general(a_ref[...].astype(BF16), b_ref[...].astype(BF16), dn,
                                        preferred_element_type=F32)

        @pl.when(kk == nk - 1)
        def _():
            o_ref[...] = acc_ref[...].astype(o_ref.dtype)

    return pl.pallas_call(
        body, name=name, grid=(m // tm, n // tn, nk),
        out_shape=jax.ShapeDtypeStruct((m, n), out_dtype),
        in_specs=[a_spec, b_spec],
        out_specs=pl.BlockSpec((tm, tn), lambda i, j, kk: (i, j)),
        scratch_shapes=[pltpu.VMEM((tm, tn), F32)],
        compiler_params=_cparams(("parallel", "parallel", "arbitrary")),
    )(a, b)


def rowwise(name, fn, rows, params, out_rows, out_accs, n_rows, t_lat, tm):
    nb = n_rows // tm
    in_specs, piece_counts = [], []
    operands = []
    for arr, off, width in rows:
        g = math.gcd(off, width) if off else width
        assert g % 128 == 0 or (off == 0 and width == arr.shape[1]), (name, off, width)
        cnt = width // g
        last = arr.shape[0] // tm - 1
        clamp = arr.shape[0] < n_rows
        for p in range(cnt):
            cb = off // g + p
            if clamp:
                in_specs.append(pl.BlockSpec((tm, g), lambda i, cb=cb, last=last: (jnp.minimum(i, last), cb)))
            else:
                in_specs.append(pl.BlockSpec((tm, g), lambda i, cb=cb: (i, cb)))
            operands.append(arr)
        piece_counts.append(cnt)
    for p in params:
        in_specs.append(pl.BlockSpec(p.shape, lambda i, nd=p.ndim: (0,) * nd))
        operands.append(p)
    n_in = sum(piece_counts)
    n_par = len(params)
    n_or = len(out_rows)
    out_shape = [jax.ShapeDtypeStruct((n_rows, w), dt) for w, dt in out_rows]
    out_shape += [jax.ShapeDtypeStruct(s, F32) for s in out_accs]
    out_specs = [pl.BlockSpec((tm, w), lambda i: (i, 0)) for w, _ in out_rows]
    out_specs += [pl.BlockSpec(s, lambda i, nd=len(s): (0,) * nd) for s in out_accs]

    def body(*refs):
        in_refs, par_refs = refs[:n_in], refs[n_in:n_in + n_par]
        orow_refs = refs[n_in + n_par:n_in + n_par + n_or]
        oacc_refs = refs[n_in + n_par + n_or:]
        i = pl.program_id(0)
        tiles, at = [], 0
        for cnt in piece_counts:
            parts = [in_refs[at + p][...] for p in range(cnt)]
            tiles.append(parts[0] if cnt == 1 else jnp.concatenate(parts, axis=1))
            at += cnt
        is_ctx = i * tm >= t_lat
        outs, accs = fn(is_ctx, tiles, [p[...] for p in par_refs])
        for o_ref, o in zip(orow_refs, outs):
            o_ref[...] = o.astype(o_ref.dtype)
        if oacc_refs:
            @pl.when(i == 0)
            def _():
                for a_ref in oacc_refs:
                    a_ref[...] = jnp.zeros_like(a_ref)
            for a_ref, a in zip(oacc_refs, accs):
                a_ref[...] += a.astype(F32)

    res = pl.pallas_call(
        body, name=name, grid=(nb,),
        out_shape=out_shape, in_specs=in_specs, out_specs=out_specs,
        compiler_params=_cparams(("arbitrary",)),
    )(*operands)
    return res[:n_or], res[n_or:]


def _rms(x, g):
    return x * lax.rsqrt(jnp.mean(x * x, axis=-1, keepdims=True) + EPS) * g


def _norm_mod(x, g, sc, sh):
    return _rms(x, g) * (1.0 + sc) + sh


def _sigmoid(x):
    return 1.0 / (1.0 + jnp.exp(-x))


def _silu(x):
    return x * _sigmoid(x)


def _gelu(x):
    return 0.5 * x * (1.0 + jnp.tanh(math.sqrt(2.0 / math.pi) * (x + 0.044715 * (x * x * x))))


def _sel(is_ctx, p):
    return jnp.where(is_ctx, p[1:2], p[0:1])


def _seg_acc(is_ctx, v):
    rows = lax.broadcasted_iota(jnp.int32, (2, v.shape[1]), 0)
    return jnp.where(rows == is_ctx.astype(jnp.int32), jnp.broadcast_to(v, (2, v.shape[1])), 0.0)


def _rsum(v):
    return jnp.sum(v, axis=0, keepdims=True)


def _shift_rows(x, o, t_lat, n):
    if o == 0:
        return x
    y = pltpu.roll(x, (-o) % n, 0)
    t = lax.broadcasted_iota(jnp.int32, x.shape, 0)
    src = t + o
    ok = (src >= 0) & (src < n) & ((src >= t_lat) == (t >= t_lat))
    return jnp.where(ok, y, 0.0)


def conv_fwd(name, xarr, col_off, width, w, b, left, n_rows, t_lat, cb=128):
    taps = w.shape[0]
    assert col_off % cb == 0 and width % cb == 0

    def body(x_ref, w_ref, b_ref, o_ref):
        x = x_ref[...]
        acc = jnp.broadcast_to(b_ref[...], x.shape)
        for k in range(taps):
            acc = acc + _shift_rows(x, k - left, t_lat, n_rows) * w_ref[k:k + 1, :]
        o_ref[...] = acc

    return pl.pallas_call(
        body, name=name, grid=(width // cb,),
        out_shape=jax.ShapeDtypeStruct((n_rows, width), F32),
        in_specs=[pl.BlockSpec((n_rows, cb), lambda j: (0, col_off // cb + j)),
                  pl.BlockSpec((taps, cb), lambda j: (0, j)),
                  pl.BlockSpec((1, cb), lambda j: (0, j))],
        out_specs=pl.BlockSpec((n_rows, cb), lambda j: (0, j)),
        compiler_params=_cparams(("parallel",)),
    )(xarr, w, b)


def conv_bwd(name, dout, xarr, col_off, width, w, left, n_rows, t_lat, cb=128):
    taps = w.shape[0]

    def body(d_ref, x_ref, w_ref, dx_ref, dw_ref, db_ref):
        d = d_ref[...]
        x = x_ref[...]
        dx = jnp.zeros_like(d)
        dws = []
        for k in range(taps):
            dx = dx + _shift_rows(d, left - k, t_lat, n_rows) * w_ref[k:k + 1, :]
            dws.append(_rsum(d * _shift_rows(x, k - left, t_lat, n_rows)))
        dx_ref[...] = dx.astype(dx_ref.dtype)
        dw_ref[...] = jnp.concatenate(dws, axis=0)
        db_ref[...] = _rsum(d)

    return pl.pallas_call(
        body, name=name, grid=(width // cb,),
        out_shape=[jax.ShapeDtypeStruct((n_rows, width), BF16), jax.ShapeDtypeStruct((taps, width), F32),
                   jax.ShapeDtypeStruct((1, width), F32)],
        in_specs=[pl.BlockSpec((n_rows, cb), lambda j: (0, j)),
                  pl.BlockSpec((n_rows, cb), lambda j: (0, col_off // cb + j)),
                  pl.BlockSpec((taps, cb), lambda j: (0, j))],
        out_specs=[pl.BlockSpec((n_rows, cb), lambda j: (0, j)), pl.BlockSpec((taps, cb), lambda j: (0, j)),
                   pl.BlockSpec((1, cb), lambda j: (0, j))],
        compiler_params=_cparams(("parallel",)),
    )(dout, xarr, w)


def _chunk_order(direction, nb, nbl):
    if direction == 'f':
        return lambda s: ((s + nbl) % nb, 0)
    return lambda s: (nb - 1 - s, 0)


def _adjoint_order(direction, nb, nbl):
    if direction == 'f':
        return lambda s: ((nb - 1 - s + nbl) % nb, 0)
    return lambda s: (s, 0)


def scan_fwd(name, a, u, direction, n_rows, t_lat, tc=128):
    w = a.shape[1]
    nb, nbl = n_rows // tc, t_lat // tc
    order = _chunk_order(direction, nb, nbl)
    rev = direction == 'b'
    nt = tc // 8

    def body(a_ref, u_ref, h_ref, hp_ref, carry):
        @pl.when(pl.program_id(0) == 0)
        def _():
            carry[...] = jnp.zeros_like(carry)

        def tile(kt, h):
            k = (nt - 1 - kt) if rev else kt
            r0 = pl.multiple_of(k * 8, 8)
            at = a_ref[pl.ds(r0, 8), :]
            ut = u_ref[pl.ds(r0, 8), :]
            hs, hps = [None] * 8, [None] * 8
            for j in (reversed(range(8)) if rev else range(8)):
                hps[j] = h
                h = at[j:j + 1, :] * h + ut[j:j + 1, :]
                hs[j] = h
            h_ref[pl.ds(r0, 8), :] = jnp.concatenate(hs, axis=0)
            hp_ref[pl.ds(r0, 8), :] = jnp.concatenate(hps, axis=0)
            return h

        carry[...] = lax.fori_loop(0, nt, tile, carry[...])

    spec = pl.BlockSpec((tc, w), order)
    return pl.pallas_call(
        body, name=name, grid=(nb,),
        out_shape=[jax.ShapeDtypeStruct((n_rows, w), F32)] * 2,
        in_specs=[spec, spec], out_specs=[spec, spec],
        scratch_shapes=[pltpu.VMEM((1, w), F32)],
        compiler_params=_cparams(("arbitrary",)),
    )(a, u)


def scan_adj(name, a, dh, hprev, direction, n_rows, t_lat, tc=128):
    w = a.shape[1]
    nb, nbl = n_rows // tc, t_lat // tc
    order = _adjoint_order(direction, nb, nbl)
    rev = direction == 'f'
    nt = tc // 8

    def dh_order(s):
        c, _ = order(s)
        return (jnp.minimum(c, nbl - 1), 0)

    def body(a_ref, dh_ref, hp_ref, du_ref, da_ref, carry):
        s = pl.program_id(0)

        @pl.when(s == 0)
        def _():
            carry[...] = jnp.zeros_like(carry)

        chunk, _ = order(s)
        live = (chunk < nbl).astype(F32)

        def tile(kt, c):
            k = (nt - 1 - kt) if rev else kt
            r0 = pl.multiple_of(k * 8, 8)
            at = a_ref[pl.ds(r0, 8), :]
            dt = dh_ref[pl.ds(r0, 8), :] * live
            lams = [None] * 8
            for j in (reversed(range(8)) if rev else range(8)):
                lam = dt[j:j + 1, :] + c
                lams[j] = lam
                c = at[j:j + 1, :] * lam
            lam8 = jnp.concatenate(lams, axis=0)
            du_ref[pl.ds(r0, 8), :] = lam8
            da_ref[pl.ds(r0, 8), :] = lam8 * hp_ref[pl.ds(r0, 8), :]
            return c

        carry[...] = lax.fori_loop(0, nt, tile, carry[...])

    spec = pl.BlockSpec((tc, w), order)
    return pl.pallas_call(
        body, name=name, grid=(nb,),
        out_shape=[jax.ShapeDtypeStruct((n_rows, w), F32)] * 2,
        in_specs=[spec, pl.BlockSpec((tc, w), dh_order), spec], out_specs=[spec, spec],
        scratch_shapes=[pltpu.VMEM((1, w), F32)],
        compiler_params=_cparams(("arbitrary",)),
    )(a, dh, hprev)


def _neg_expm1(y):
    series = -(y * (1.0 + y * (0.5 + y * (1.0 / 6.0 + y * (1.0 / 24.0)))))
    return jnp.where(y > -0.03, series, 1.0 - jnp.exp(y))


def _gate_elem(pre_r, pre_i, xc, b_a, b_x, sp):
    r = _sigmoid(pre_r + b_a)
    i = _sigmoid(pre_i + b_x)
    log_a = (-LRU_C) * r * sp
    a = jnp.exp(log_a)
    mult = jnp.sqrt(_neg_expm1(2.0 * log_a))
    return a, mult * (i * xc)


def _blockdiag(xb16, w_ref_val, d):
    outs = []
    for n in range(LRU_BLOCKS):
        outs.append(jnp.dot(xb16[:, n * LRU_BW:(n + 1) * LRU_BW], w_ref_val[d * LRU_BLOCKS + n],
                            preferred_element_type=F32))
    return jnp.concatenate(outs, axis=1)


def gates_fwd(xc, w_a, w_x, b_a, b_x, sp, n_rows, t_lat, tm):
    def fn(is_ctx, rows, params):
        (x,), (wa, wx, ba, bx, spv) = rows, params
        xb16 = x.astype(BF16)
        outs = []
        for d in range(2):
            a, u = _gate_elem(_blockdiag(xb16, wa, d), _blockdiag(xb16, wx, d), x,
                              ba[d:d + 1], bx[d:d + 1], spv[d:d + 1])
            outs += [a, u]
        return outs, []

    (a_f, u_f, a_b, u_b), _ = rowwise("gates_fwd", fn, [(xc, 0, LRU_W)], [w_a, w_x, b_a, b_x, sp],
                                      [(LRU_W, F32)] * 4, [], n_rows, t_lat, tm)
    return a_f, u_f, a_b, u_b


def gates_bwd(xc, da_f, du_f, da_b, du_b, w_a, w_x, b_a, b_x, sp, n_rows, t_lat, tm):
    def fn(is_ctx, rows, params):
        (x, daf, duf, dab, dub), (wa, wx, ba, bx, spv) = rows, params
        xb16 = x.astype(BF16)
        dxc = jnp.zeros_like(x)
        dwa, dwx, dba, dbx, dsp = [], [], [], [], []
        for d, (da, du) in enumerate(((daf, duf), (dab, dub))):
            pre_r, pre_i = _blockdiag(xb16, wa, d), _blockdiag(xb16, wx, d)
            _, vjp = jax.vjp(_gate_elem, pre_r, pre_i, x, ba[d:d + 1], bx[d:d + 1], spv[d:d + 1])
            dpr, dpi, dx_e, dba_d, dbx_d, dsp_d = vjp((da, du))
            dxc = dxc + dx_e
            dpr16, dpi16 = dpr.astype(BF16), dpi.astype(BF16)
            back = []
            for n in range(LRU_BLOCKS):
                sl = slice(n * LRU_BW, (n + 1) * LRU_BW)
                nt_dims = (((1,), (1,)), ((), ()))
                back.append(lax.dot_general(dpr16[:, sl], wa[d * LRU_BLOCKS + n], nt_dims, preferred_element_type=F32)
                            + lax.dot_general(dpi16[:, sl], wx[d * LRU_BLOCKS + n], nt_dims,
                                              preferred_element_type=F32))
                tn_dims = (((0,), (0,)), ((), ()))
                dwa.append(lax.dot_general(xb16[:, sl], dpr16[:, sl], tn_dims, preferred_element_type=F32)[None])
                dwx.append(lax.dot_general(xb16[:, sl], dpi16[:, sl], tn_dims, preferred_element_type=F32)[None])
            dxc = dxc + jnp.concatenate(back, axis=1)
            dba.append(dba_d)
            dbx.append(dbx_d)
            dsp.append(dsp_d)
        cat0 = lambda xs: jnp.concatenate(xs, axis=0)
        return [dxc], [cat0(dwa), cat0(dwx), cat0(dba), cat0(dbx), cat0(dsp)]

    (dxc,), accs = rowwise("gates_bwd", fn,
                           [(xc, 0, LRU_W), (da_f, 0, LRU_W), (du_f, 0, LRU_W), (da_b, 0, LRU_W), (du_b, 0, LRU_W)],
                           [w_a, w_x, b_a, b_x, sp], [(LRU_W, F32)],
                           [(2 * LRU_BLOCKS, LRU_BW, LRU_BW)] * 2 + [(2, LRU_W)] * 3, n_rows, t_lat, tm)
    return dxc, accs


def _rope_tables(t_lat, n_rows):
    rows = t_lat // GRID_W
    row_ids = jnp.repeat(jnp.arange(rows), GRID_W).astype(F32)
    col_ids = jnp.tile(jnp.arange(GRID_W), rows).astype(F32)
    axis_dim = QK_ROPE // 2
    inv = 1.0 / (ROPE_BASE ** (jnp.arange(0, axis_dim, 2, dtype=F32) / axis_dim))
    ang = jnp.concatenate([row_ids[:, None] * inv, col_ids[:, None] * inv], axis=-1)
    cos, sin = jnp.cos(ang), jnp.sin(ang)
    half = QK_ROPE // 2
    ones, zeros = jnp.ones((t_lat, QK_NOPE), F32), jnp.zeros((t_lat, QK_NOPE), F32)
    pad1, pad0 = jnp.ones((t_lat, HEAD_PAD - QK_DIM), F32), jnp.zeros((t_lat, HEAD_PAD - QK_DIM), F32)
    zh = jnp.zeros((t_lat, half), F32)
    c_tab = jnp.concatenate([ones, cos, cos, pad1], axis=1)
    s1 = jnp.concatenate([zeros, -sin, zh, pad0], axis=1)
    s2 = jnp.concatenate([zeros, zh, sin, pad0], axis=1)
    n_ctx = n_rows - t_lat
    c_tab = jnp.concatenate([c_tab, jnp.ones((n_ctx, HEAD_PAD), F32)], axis=0)
    s1 = jnp.concatenate([s1, jnp.zeros((n_ctx, HEAD_PAD), F32)], axis=0)
    s2 = jnp.concatenate([s2, jnp.zeros((n_ctx, HEAD_PAD), F32)], axis=0)
    return c_tab, s1, s2


def _rope(x, c, s1, s2):
    half = QK_ROPE // 2
    return x * c + pltpu.roll(x, HEAD_PAD - half, 1) * s1 + pltpu.roll(x, half, 1) * s2


def _rope_t(dy, c, s1, s2):
    half = QK_ROPE // 2
    return dy * c + pltpu.roll(dy * s1, half, 1) + pltpu.roll(dy * s2, HEAD_PAD - half, 1)


def _heads(x):
    return [x[:, h * HEAD_PAD:(h + 1) * HEAD_PAD] for h in range(N_HEADS)]


def attn_fwd(q, k, v, t_lat, n_rows, tq):
    scale = QK_DIM ** -0.5

    def body(q_ref, k_ref, v_ref, o_ref, lse_ref):
        s = lax.dot_general(q_ref[...], k_ref[...], (((1,), (1,)), ((), ())), preferred_element_type=F32) * scale
        m = jnp.max(s, axis=-1, keepdims=True)
        p = jnp.exp(s - m)
        l = jnp.sum(p, axis=-1, keepdims=True)
        o = jnp.dot(p.astype(BF16), v_ref[...], preferred_element_type=F32) / l
        o_ref[...] = o.astype(o_ref.dtype)
        lse_ref[...] = jnp.broadcast_to(m + jnp.log(l), lse_ref.shape)

    qspec = pl.BlockSpec((tq, HEAD_PAD), lambda h, i: (i, h))
    kspec = pl.BlockSpec((n_rows, HEAD_PAD), lambda h, i: (0, h))
    return pl.pallas_call(
        body, name="attn_fwd", grid=(N_HEADS, t_lat // tq),
        out_shape=[jax.ShapeDtypeStruct((t_lat, N_HEADS * HEAD_PAD), BF16),
                   jax.ShapeDtypeStruct((t_lat, N_HEADS * HEAD_PAD), F32)],
        in_specs=[qspec, kspec, kspec], out_specs=[qspec, qspec],
        compiler_params=_cparams(("parallel", "arbitrary")),
    )(q, k, v)


def attn_bwd(q, k, v, o, do, lse, t_lat, n_rows, tq):
    scale = QK_DIM ** -0.5
    nt = (((1,), (1,)), ((), ()))
    tn = (((0,), (0,)), ((), ()))

    def body(q_ref, k_ref, v_ref, o_ref, do_ref, lse_ref, dq_ref, dk_ref, dv_ref):
        @pl.when(pl.program_id(1) == 0)
        def _():
            dk_ref[...] = jnp.zeros_like(dk_ref)
            dv_ref[...] = jnp.zeros_like(dv_ref)

        qv, kv, vv, dov = q_ref[...], k_ref[...], v_ref[...], do_ref[...]
        s = lax.dot_general(qv, kv, nt, preferred_element_type=F32) * scale
        p = jnp.exp(s - lse_ref[:, 0:1])
        dv_ref[...] += lax.dot_general(p.astype(BF16), dov, tn, preferred_element_type=F32)
        dp = lax.dot_general(dov, vv, nt, preferred_element_type=F32)
        delta = jnp.sum(dov.astype(F32) * o_ref[...].astype(F32), axis=-1, keepdims=True)
        ds = (p * (dp - delta) * scale).astype(BF16)
        dq_ref[...] = jnp.dot(ds, kv, preferred_element_type=F32)
        dk_ref[...] += lax.dot_general(ds, qv, tn, preferred_element_type=F32)

    qspec = pl.BlockSpec((tq, HEAD_PAD), lambda h, i: (i, h))
    kspec = pl.BlockSpec((n_rows, HEAD_PAD), lambda h, i: (0, h))
    return pl.pallas_call(
        body, name="attn_bwd", grid=(N_HEADS, t_lat // tq),
        out_shape=[jax.ShapeDtypeStruct((t_lat, N_HEADS * HEAD_PAD), F32),
                   jax.ShapeDtypeStruct((n_rows, N_HEADS * HEAD_PAD), F32),
                   jax.ShapeDtypeStruct((n_rows, N_HEADS * HEAD_PAD), F32)],
        in_specs=[qspec, kspec, kspec, qspec, qspec, qspec], out_specs=[qspec, kspec, kspec],
        compiler_params=_cparams(("parallel", "arbitrary")),
    )(q, k, v, o, do, lse)


def adamw(name, w, g, m, v):
    r, ccols = w.shape
    tr = _best_div(r, 8, max(8, 262144 // ccols)) if r % 8 == 0 else r
    c1 = 1.0 - ADAM_B1 ** ADAM_STEP
    c2 = 1.0 - ADAM_B2 ** ADAM_STEP

    def body(w_ref, g_ref, m_ref, v_ref, d_ref, nm_ref, nv_ref):
        gv = g_ref[...]
        nm = ADAM_B1 * m_ref[...] + (1.0 - ADAM_B1) * gv
        nv = ADAM_B2 * v_ref[...] + (1.0 - ADAM_B2) * (gv * gv)
        d_ref[...] = -ADAM_LR * ((nm / c1) / (jnp.sqrt(nv / c2) + ADAM_EPS) + ADAM_WD * w_ref[...])
        nm_ref[...] = nm
        nv_ref[...] = nv

    spec = pl.BlockSpec((tr, ccols), lambda i: (i, 0))
    return pl.pallas_call(
        body, name=name, grid=(r // tr,),
        out_shape=[jax.ShapeDtypeStruct((r, ccols), F32)] * 3,
        in_specs=[spec] * 4, out_specs=[spec] * 3,
        compiler_params=_cparams(("parallel",)),
    )(w, g, m, v)


def _flat(parts, dtype, row_mult):
    v = jnp.concatenate([p.reshape(-1).astype(dtype) for p in parts])
    quantum = row_mult * FLAT_C
    total = -(-v.shape[0] // quantum) * quantum
    return jnp.pad(v, (0, total - v.shape[0])).reshape(total // FLAT_C, FLAT_C)


def _unflat(flat, shapes):
    v = flat.reshape(-1)
    out, at = [], 0
    for s in shapes:
        n = math.prod(s)
        out.append(v[at:at + n].reshape(s))
        at += n
    return out


def _gathered_to_full(name, g):
    k = g.shape[1]
    return jnp.transpose(g, (1, 0, 2)).reshape(k, N_DEV * g.shape[2])


def _full_to_chunks(name, full):
    k, n = full.shape
    return jnp.transpose(full.reshape(k, N_DEV, n // N_DEV), (1, 0, 2)).reshape(N_DEV, -1)


def _shard_to_rb(name, w):
    return w if name in ROW_SHARDED else w.T


def _rb_to_shard(name, g):
    return g if name in ROW_SHARDED else g.T


def _global_rows(g, lo, hi):
    n = g.shape[1]
    out = []
    for p in range(N_DEV):
        a, b = max(lo, p * n), min(hi, (p + 1) * n)
        if a < b:
            out.append(g[p, a - p * n:b - p * n])
    return out


def _rb_from_gathered(name, g):
    cols = g.shape[2]
    if name == 'w_in':
        z = lambda k: jnp.zeros((k, cols), g.dtype)
        parts = (_global_rows(g, 0, Z_KR) + [z(QK_NOPE)] + _global_rows(g, Z_KR, Z_KR + QK_ROPE)
                 + [z(HEAD_PAD - QK_DIM)] + _global_rows(g, Z_KR + QK_ROPE, N_DEV * g.shape[1]))
        return jnp.concatenate(parts, axis=0)
    if name == 'w_uq':
        return jnp.pad(g, ((0, 0), (0, HEAD_PAD - QK_DIM), (0, 0))).reshape(N_HEADS * HEAD_PAD, cols)
    if name == 'w_ukv':
        pad = lambda t: jnp.pad(t, ((0, 0), (0, HEAD_PAD - t.shape[1]), (0, 0))).reshape(N_HEADS * HEAD_PAD, cols)
        return jnp.concatenate([pad(g[:, :QK_NOPE]), pad(g[:, QK_NOPE:])], axis=0)
    if name == 'w_o_attn':
        full = g.reshape(D, N_HEADS, V_HEAD)
        return jnp.pad(full, ((0, 0), (0, 0), (0, HEAD_PAD - V_HEAD))).reshape(D, N_HEADS * HEAD_PAD)
    return g.reshape(N_DEV * g.shape[1], cols)


def _chunks_from_rb_grad(name, g):
    cols = g.shape[1]
    if name == 'w_in':
        full = jnp.concatenate([g[:Z_KR], g[Z_KR + QK_NOPE:Z_KR + QK_DIM], g[Z_XB:]], axis=0)
        return full.reshape(N_DEV, -1, cols)
    if name == 'w_uq':
        return g.reshape(N_HEADS, HEAD_PAD, cols)[:, :QK_DIM]
    if name == 'w_ukv':
        half = N_HEADS * HEAD_PAD
        gk = g[:half].reshape(N_HEADS, HEAD_PAD, cols)[:, :QK_NOPE]
        gv = g[half:].reshape(N_HEADS, HEAD_PAD, cols)[:, :V_HEAD]
        return jnp.concatenate([gk, gv], axis=1)
    if name == 'w_o_attn':
        full = g.reshape(D, N_HEADS, HEAD_PAD)[:, :, :V_HEAD].reshape(D, N_HEADS * V_HEAD)
        return full.reshape(N_DEV, D // N_DEV, N_HEADS * V_HEAD)
    return g.reshape(N_DEV, -1, cols)


def local_step(x, ctx, target, mod_l, mod_c, wt):
    t_lat, n_ctx = x.shape[0], ctx.shape[0]
    n = t_lat + n_ctx
    tm = _pick(math.gcd(t_lat, n), (256, 128))
    tq = _pick(t_lat, (256, 128))
    row = lambda v: v.reshape(1, -1).astype(F32)
    two = lambda a, b: jnp.stack([a, b]).astype(F32)
    sh1_l, sc1_l, g1_l, sh2_l, sc2_l, g2_l = jnp.split(mod_l, 6)
    sh1_c, sc1_c = jnp.split(mod_c, 6)[:2]
    sc1, sh1 = two(sc1_l, sc1_c), two(sh1_l, sh1_c)
    g1, g2, sc2, sh2 = row(g1_l), row(g2_l), row(sc2_l), row(sh2_l)
    norm1_g, norm2_g, final_g = row(wt['norm1_g']), row(wt['norm2_g']), row(wt['final_g'])
    q_g, kv_g, b_gate = row(wt['q_norm_g']), row(wt['kv_norm_g']), row(wt['b_gate'])
    w_in_t, w_uq_t, w_ukv_t, w_o_attn_t, w_up_t = wt['w_in'], wt['w_uq'], wt['w_ukv'], wt['w_o_attn'], wt['w_up']
    w_o_lru, w_out, w_down = wt['w_o_lru'], wt['w_out'], wt['w_down']
    lru_w_a = wt['lru_w_a'].reshape(2 * LRU_BLOCKS, LRU_BW, LRU_BW).astype(BF16)
    lru_w_x = wt['lru_w_x'].reshape(2 * LRU_BLOCKS, LRU_BW, LRU_BW).astype(BF16)
    b_a, b_x, lam = wt['lru_b_a'], wt['lru_b_x'], wt['lru_lambda']
    sp = jnp.logaddexp(-lam, 0.0)
    c_tab, s1_tab, s2_tab = _rope_tables(t_lat, n)
    rw = functools.partial(rowwise, n_rows=n, t_lat=t_lat, tm=tm)
    rw_lat = functools.partial(rowwise, n_rows=t_lat, t_lat=t_lat, tm=tm)

    xs = jnp.concatenate([x, ctx], axis=0)

    def f_norm1(is_ctx, rows, params):
        (xv,), (g, sc, sh) = rows, params
        return [_norm_mod(xv, g, _sel(is_ctx, sc), _sel(is_ctx, sh))], []

    (h,), _ = rw("norm1", f_norm1, [(xs, 0, D)], [norm1_g, sc1, sh1], [(D, BF16)], [])
    z = matmul("w_in", h, w_in_t, 'nt', F32)

    def f_qkv_norm(is_ctx, rows, params):
        (ql, kvl), (gq, gkv) = rows, params
        return [_rms(ql, gq), _rms(kvl, gkv)], []

    (qn, kvn), _ = rw("qkv_norm", f_qkv_norm, [(z, Z_Q, Q_RANK), (z, Z_KV, KV_RANK)], [q_g, kv_g],
                      [(Q_RANK, BF16), (KV_RANK, BF16)], [])
    qp = matmul("w_uq", qn, w_uq_t, 'nt', F32)
    kvp = matmul("w_ukv", kvn, w_ukv_t, 'nt', F32)

    def f_rope(is_ctx, rows, params):
        qv, kk, vv, kr, c, s1, s2 = rows
        krr = _rope(kr, c, s1, s2)
        qo = jnp.concatenate([_rope(qh, c, s1, s2) for qh in _heads(qv)], axis=1)
        ko = jnp.concatenate([kh + krr for kh in _heads(kk)], axis=1)
        return [qo, ko, vv], []

    hp = N_HEADS * HEAD_PAD
    (qr, kr_, vr), _ = rw("rope", f_rope,
                          [(qp, 0, hp), (kvp, 0, hp), (kvp, hp, hp), (z, Z_KR, HEAD_PAD), (c_tab, 0, HEAD_PAD),
                           (s1_tab, 0, HEAD_PAD), (s2_tab, 0, HEAD_PAD)], [], [(hp, BF16)] * 3, [])
    attn, lse = attn_fwd(qr, kr_, vr, t_lat, n, tq)

    xc = conv_fwd("lru_conv", z, Z_XB, LRU_W, wt['lru_conv_w'], row(wt['lru_conv_b']), 2, n, t_lat)
    a_f, u_f, a_b, u_b = gates_fwd(xc, lru_w_a, lru_w_x, b_a, b_x, sp, n, t_lat, tm)
    h_f, hp_f = scan_fwd("scan_f", a_f, u_f, 'f', n, t_lat)
    h_b, hp_b = scan_fwd("scan_b", a_b, u_b, 'b', n, t_lat)

    def f_lru_out(is_ctx, rows, params):
        hf, hb, yb = rows
        return [(hf + hb) * _gelu(yb)], []

    (ybin,), _ = rw_lat("lru_out", f_lru_out, [(h_f, 0, LRU_W), (h_b, 0, LRU_W), (z, Z_YB, LRU_W)], [],
                        [(LRU_W, BF16)], [])
    y_a = matmul("w_o_attn", attn, w_o_attn_t, 'nt', F32)
    y_b = matmul("w_o_lru", ybin, w_o_lru, 'nn', F32)

    def _merge(ya, yb, gl, bg):
        gates = _sigmoid(gl + bg)
        return gates[:, :D] * ya + gates[:, D:] * yb

    def f_merge(is_ctx, rows, params):
        (ya, yb, gl), (bg,) = rows, params
        return [_merge(ya, yb, gl, bg)], []

    (mrg,), _ = rw_lat("merge", f_merge, [(y_a, 0, D), (y_b, 0, D), (z, Z_GL, 2 * D)], [b_gate], [(D, BF16)], [])
    o = matmul("w_out", mrg, w_out, 'nn', F32)

    def _res_norm2(xv, ov, g1v, g, sc, sh):
        x1 = xv + g1v * ov
        return x1, _norm_mod(x1, g, sc, sh)

    def f_norm2(is_ctx, rows, params):
        (xv, ov), (g1v, g, sc, sh) = rows, params
        x1, h2v = _res_norm2(xv, ov, g1v, g, sc, sh)
        return [x1, h2v], []

    (x1, h2), _ = rw_lat("norm2", f_norm2, [(x, 0, D), (o, 0, D)], [g1, norm2_g, sc2, sh2], [(D, F32), (D, BF16)], [])
    u = matmul("w_up", h2, w_up_t, 'nt', F32)
    ac = conv_fwd("ffn_conv", u, 0, FFN, wt['ffn_conv_w'], row(wt['ffn_conv_b']), 1, t_lat, t_lat)

    def f_ffn_act(is_ctx, rows, params):
        acv, gv = rows
        return [_silu(acv) * gv], []

    (f,), _ = rw_lat("ffn_act", f_ffn_act, [(ac, 0, FFN), (u, FFN, FFN)], [], [(FFN, BF16)], [])
    dn = matmul("w_down", f, w_down, 'nn', F32)

    def _tile_loss(x1v, dv, g2v, fg, tgt):
        y = _rms(x1v + g2v * dv, fg)
        e = y - tgt
        return 0.5 * jnp.sum(jnp.mean(e * e, axis=-1, keepdims=True), axis=0, keepdims=True)

    def f_final(is_ctx, rows, params):
        (x1v, dv, tgt), (g2v, fg) = rows, params
        lv, vjp = jax.vjp(lambda a, b, c, d: _tile_loss(a, b, c, d, tgt), x1v, dv, g2v, fg)
        dx2, dd, dg2, dfg = vjp(jnp.ones((1, 1), F32))
        return [dx2, dd], [dg2, dfg, jnp.broadcast_to(lv, (1, 128))]

    (dx2, dd), (dg2, dfinal_g, loss_v) = rw_lat("final", f_final, [(x1, 0, D), (dn, 0, D), (target, 0, D)],
                                                [g2, final_g], [(D, F32), (D, BF16)], [(1, D), (1, D), (1, 128)])
    loss = loss_v[0, 0]

    grads = {'final_g': dfinal_g}
    df = matmul("d_f", dd, w_down, 'nt', F32)
    grads['w_down'] = matmul("g_w_down", f, dd, 'tn', BF16)

    def b_ffn_act(is_ctx, rows, params):
        acv, gv, dfv = rows
        _, vjp = jax.vjp(lambda a, g: _silu(a) * g, acv, gv)
        dac, dg = vjp(dfv)
        return [dac, dg], []

    (dac, dgate), _ = rw_lat("ffn_act_bwd", b_ffn_act, [(ac, 0, FFN), (u, FFN, FFN), (df, 0, FFN)], [],
                             [(FFN, F32), (FFN, BF16)], [])
    da, grads['ffn_conv_w'], grads['ffn_conv_b'] = conv_bwd("ffn_conv_bwd", dac, u, 0, FFN, wt['ffn_conv_w'], 1,
                                                            t_lat, t_lat)
    du = jnp.concatenate([da, dgate], axis=1)
    dh2 = matmul("d_h2", du, w_up_t, 'nn', F32)
    grads['w_up'] = matmul("g_w_up", du, h2, 'tn', BF16)

    def b_norm2(is_ctx, rows, params):
        (xv, ov, dh2v, dx2v), (g1v, g, sc, sh) = rows, params
        _, vjp = jax.vjp(_res_norm2, xv, ov, g1v, g, sc, sh)
        dx, do, dg1v, dg, dsc, dsh = vjp((dx2v, dh2v))
        return [dx, do], [dg1v, dg, dsc, dsh]

    (dx_res, do), (dg1, dnorm2_g, dsc2, dsh2) = rw_lat(
        "norm2_bwd", b_norm2, [(x, 0, D), (o, 0, D), (dh2, 0, D), (dx2, 0, D)], [g1, norm2_g, sc2, sh2],
        [(D, F32), (D, BF16)], [(1, D)] * 4)
    grads['norm2_g'] = dnorm2_g
    dmrg = matmul("d_merge", do, w_out, 'nt', F32)
    grads['w_out'] = matmul("g_w_out", mrg, do, 'tn', BF16)

    def b_merge(is_ctx, rows, params):
        (ya, yb, gl, dm), (bg,) = rows, params
        _, vjp = jax.vjp(_merge, ya, yb, gl, bg)
        dya, dyb, dgl, dbg = vjp(dm)
        return [dya, dyb, dgl], [dbg]

    (dy_a, dy_b, dgl), (grads['b_gate'],) = rw_lat(
        "merge_bwd", b_merge, [(y_a, 0, D), (y_b, 0, D), (z, Z_GL, 2 * D), (dmrg, 0, D)], [b_gate],
        [(D, BF16), (D, BF16), (2 * D, BF16)], [(1, 2 * D)])
    dattn = matmul("d_attn", dy_a, w_o_attn_t, 'nn', BF16)
    grads['w_o_attn'] = matmul("g_w_o_attn", dy_a, attn, 'tn', BF16)
    dybin = matmul("d_lru_out", dy_b, w_o_lru, 'nt', F32)
    grads['w_o_lru'] = matmul("g_w_o_lru", ybin, dy_b, 'tn', BF16)

    def b_lru_out(is_ctx, rows, params):
        hf, hb, yb, dyv = rows
        _, vjp = jax.vjp(lambda s, y: s * _gelu(y), hf + hb, yb)
        dh, dyb = vjp(dyv)
        return [dh, dyb], []

    (dh_lru, dyb), _ = rw_lat("lru_out_bwd", b_lru_out,
                              [(h_f, 0, LRU_W), (h_b, 0, LRU_W), (z, Z_YB, LRU_W), (dybin, 0, LRU_W)], [],
                              [(LRU_W, F32), (LRU_W, BF16)], [])
    du_f, da_f = scan_adj("scan_f_adj", a_f, dh_lru, hp_f, 'f', n, t_lat)
    du_b, da_b = scan_adj("scan_b_adj", a_b, dh_lru, hp_b, 'b', n, t_lat)
    dxc, (dw_a, dw_x, db_a, db_x, dsp) = gates_bwd(xc, da_f, du_f, da_b, du_b, lru_w_a, lru_w_x, b_a, b_x, sp,
                                                   n, t_lat, tm)
    grads['lru_w_a'] = dw_a.reshape(2, LRU_BLOCKS, LRU_BW, LRU_BW)
    grads['lru_w_x'] = dw_x.reshape(2, LRU_BLOCKS, LRU_BW, LRU_BW)
    grads['lru_b_a'], grads['lru_b_x'] = db_a, db_x
    grads['lru_lambda'] = -dsp * _sigmoid(-lam)
    dxb, grads['lru_conv_w'], grads['lru_conv_b'] = conv_bwd("lru_conv_bwd", dxc, z, Z_XB, LRU_W, wt['lru_conv_w'],
                                                             2, n, t_lat)

    dq, dk, dv = attn_bwd(qr, kr_, vr, attn, dattn, lse, t_lat, n, tq)

    def b_rope(is_ctx, rows, params):
        dqv, dkv, dvv, c, s1, s2 = rows
        live = jnp.where(is_ctx, 0.0, 1.0)
        dqo = jnp.concatenate([_rope_t(dqh, c, s1, s2) for dqh in _heads(dqv)], axis=1) * live
        dkh = _heads(dkv)
        dkr = dkh[0]
        for t in dkh[1:]:
            dkr = dkr + t
        lanes = lax.broadcasted_iota(jnp.int32, dkr.shape, 1)
        dkr = jnp.where((lanes >= QK_NOPE) & (lanes < QK_DIM), _rope_t(dkr, c, s1, s2), 0.0)
        return [dqo, jnp.concatenate([dkv, dvv], axis=1), dkr], []

    (dqp, dkvp, dkr), _ = rw("rope_bwd", b_rope,
                             [(dq, 0, hp), (dk, 0, hp), (dv, 0, hp), (c_tab, 0, HEAD_PAD), (s1_tab, 0, HEAD_PAD),
                              (s2_tab, 0, HEAD_PAD)], [], [(hp, BF16), (2 * hp, BF16), (HEAD_PAD, BF16)], [])
    dqn = matmul("d_qn", dqp, w_uq_t, 'nn', F32)
    grads['w_uq'] = matmul("g_w_uq", dqp, qn, 'tn', BF16)
    dkvn = matmul("d_kvn", dkvp, w_ukv_t, 'nn', F32)
    grads['w_ukv'] = matmul("g_w_ukv", dkvp, kvn, 'tn', BF16)

    def b_qkv_norm(is_ctx, rows, params):
        (ql, kvl, dqv, dkvv), (gq, gkv) = rows, params
        _, vjp_q = jax.vjp(_rms, ql, gq)
        _, vjp_kv = jax.vjp(_rms, kvl, gkv)
        dql, dgq = vjp_q(dqv)
        dkvl, dgkv = vjp_kv(dkvv)
        return [dql, dkvl], [dgq, dgkv]

    (dq_lat, dkv_lat), (grads['q_norm_g'], grads['kv_norm_g']) = rw(
        "qkv_norm_bwd", b_qkv_norm, [(z, Z_Q, Q_RANK), (z, Z_KV, KV_RANK), (dqn, 0, Q_RANK), (dkvn, 0, KV_RANK)],
        [q_g, kv_g], [(Q_RANK, BF16), (KV_RANK, BF16)], [(1, Q_RANK), (1, KV_RANK)])
    pad_ctx = lambda t: jnp.pad(t, ((0, n_ctx), (0, 0)))
    dz = jnp.concatenate([dq_lat, dkv_lat, dkr, dxb, pad_ctx(dyb), pad_ctx(dgl)], axis=1)
    dh = matmul("d_h", dz, w_in_t, 'nn', F32)
    grads['w_in'] = matmul("g_w_in", dz, h, 'tn', BF16)

    def b_norm1(is_ctx, rows, params):
        (xv, dhv, dxr), (g, sc, sh) = rows, params
        scv, shv = _sel(is_ctx, sc), _sel(is_ctx, sh)
        _, vjp = jax.vjp(_norm_mod, xv, g, scv, shv)
        dx, dg, dsc, dsh = vjp(dhv)
        return [dx + dxr], [dg, _seg_acc(is_ctx, dsc), _seg_acc(is_ctx, dsh)]

    (dxs,), (grads['norm1_g'], dsc1, dsh1) = rw("norm1_bwd", b_norm1, [(xs, 0, D), (dh, 0, D), (dx_res, 0, D)],
                                                [norm1_g, sc1, sh1], [(D, F32)], [(1, D), (2, D), (2, D)])
    grad_x = dxs[:t_lat]
    zero = jnp.zeros((D,), F32)
    dmod_l = jnp.concatenate([dsh1[0], dsc1[0], dg1[0], dsh2[0], dsc2[0], dg2[0]])
    dmod_c = jnp.concatenate([dsh1[1], dsc1[1], zero, zero, zero, zero])
    return loss, grad_x, grads, dmod_l, dmod_c


def kernel(x, c, ctx, c_ctx, w_mod, b_mod, norm1_g, w_in, b_gate, q_norm_g, kv_norm_g, w_uq, w_ukv, w_o_attn, lru_conv_w, lru_conv_b, lru_w_a, lru_b_a, lru_w_x, lru_b_x, lru_lambda, w_o_lru, w_out, norm2_g, w_up, ffn_conv_w, ffn_conv_b, w_down, final_g, loss_target, m_c_ctx, m_w_mod, m_b_mod, m_norm1_g, m_w_in, m_b_gate, m_q_norm_g, m_kv_norm_g, m_w_uq, m_w_ukv, m_w_o_attn, m_lru_conv_w, m_lru_conv_b, m_lru_w_a, m_lru_b_a, m_lru_w_x, m_lru_b_x, m_lru_lambda, m_w_o_lru, m_w_out, m_norm2_g, m_w_up, m_ffn_conv_w, m_ffn_conv_b, m_w_down, m_final_g, v_c_ctx, v_w_mod, v_b_mod, v_norm1_g, v_w_in, v_b_gate, v_q_norm_g, v_kv_norm_g, v_w_uq, v_w_ukv, v_w_o_attn, v_lru_conv_w, v_lru_conv_b, v_lru_w_a, v_lru_b_a, v_lru_w_x, v_lru_b_x, v_lru_lambda, v_w_o_lru, v_w_out, v_norm2_g, v_w_up, v_ffn_conv_w, v_ffn_conv_b, v_w_down, v_final_g):
    given = dict(locals())
    strip = lambda name, a: a if name in ('c_ctx', 'final_g') else a[0]
    wsh = {n: strip(n, given[n]) for n in WEIGHTS}
    msh = {n: strip(n, given['m_' + n]) for n in WEIGHTS}
    vsh = {n: strip(n, given['v_' + n]) for n in WEIGHTS}
    me = _my_index()

    small = _flat([c[0]] + [wsh[n] for n in SMALL_F32], F32, 8)
    small_all = all_gather("gather_small", small)
    shapes = [(D,)] + [wsh[n].shape for n in SMALL_F32]
    per_dev = [_unflat(small_all[p], shapes) for p in range(N_DEV)]
    c_all = jnp.stack([pd[0] for pd in per_dev])
    full = {}
    for i, n in enumerate(SMALL_F32):
        full[n] = _gathered_to_full(n, jnp.stack([pd[1 + i] for pd in per_dev]))

    cond = jnp.concatenate([c_all, c_ctx[None], jnp.zeros((7, D), F32)], axis=0)
    sil = cond * jax.nn.sigmoid(cond)
    mod_cols = matmul("mod_proj", sil, wsh['w_mod'], 'nn', F32)
    mod_all = all_gather("gather_mod", mod_cols)
    mod_all = jnp.transpose(mod_all, (1, 0, 2)).reshape(16, 6 * D) + b_mod[0][None]
    mod_l = lax.dynamic_index_in_dim(mod_all, me, axis=0, keepdims=False)
    mod_c = mod_all[N_DEV]

    gathered = all_gather_multi("gather_weights", [_shard_to_rb(n, wsh[n]).astype(BF16) for n in BIG_BF16])
    for n, g in zip(BIG_BF16, gathered):
        full[n] = _rb_from_gathered(n, g)
    for n in REPLICATED:
        if n not in ('c_ctx', 'b_mod'):
            full[n] = wsh[n]

    loss, grad_x, grads, dmod_l, dmod_c = local_step(x[0], ctx[0], loss_target[0], mod_l, mod_c, full)
    loss = lax.psum(loss, ("x", "y", "c"))

    dmod = _flat([dmod_l, dmod_c], F32, 8)
    dmod_all = all_gather("gather_dmod", dmod)
    dm = jnp.stack([jnp.stack(_unflat(dmod_all[p], [(6 * D,), (6 * D,)])) for p in range(N_DEV)])
    dmod_c_tot = dm[0, 1]
    for p in range(1, N_DEV):
        dmod_c_tot = dmod_c_tot + dm[p, 1]
    dm16 = jnp.concatenate([dm[:, 0], dmod_c_tot[None], jnp.zeros((7, 6 * D), F32)], axis=0)
    ncol = 6 * D // N_DEV
    dm16_cols = lax.dynamic_slice_in_dim(dm16.reshape(16, N_DEV, ncol), me, 1, axis=1)[:, 0]
    grad_w_mod = matmul("g_w_mod", sil, dm16_cols, 'tn', F32)
    dsil = matmul("d_cond", dm16_cols, wsh['w_mod'], 'nt', F32)
    sg = jax.nn.sigmoid(c_ctx)
    grads['c_ctx'] = dsil[N_DEV] * (sg * (1.0 + c_ctx * (1.0 - sg)))
    grads['b_mod'] = dmod_l + dmod_c

    rep_sizes = [math.prod(wsh[n].shape) for n in REPLICATED]
    big_chunks = [_chunks_from_rb_grad(n, grads[n]) for n in BIG_BF16]
    chunks = [_full_to_chunks(n, grads[n].reshape(full[n].shape)) for n in SMALL_F32]
    chunks += [grads[n].reshape(N_DEV, -1) for n in REPLICATED]
    chunk_len = sum(ch.shape[1] for ch in chunks)
    quantum = 8 * FLAT_C
    padded = -(-chunk_len // quantum) * quantum
    send = jnp.pad(jnp.concatenate(chunks, axis=1), ((0, 0), (0, padded - chunk_len)))
    got = all_to_all_multi("grad_exchange", big_chunks + [send.reshape(N_DEV, padded // FLAT_C, FLAT_C)])
    g_final = {'w_mod': grad_w_mod}
    for n, slots in zip(BIG_BF16, got[:-1]):
        g_final[n] = _rb_to_shard(n, sum_slots("sum_" + n, slots))
    gsum = sum_slots("sum_small", got[-1]).reshape(-1)
    at = 0
    for n in SMALL_F32:
        cnt = math.prod(wsh[n].shape)
        g_final[n] = gsum[at:at + cnt].reshape(wsh[n].shape)
        at += cnt
    rep_part = _flat([gsum[at:chunk_len]], F32, 8)
    rep_all = all_gather("gather_replicated_grads", rep_part).reshape(N_DEV, -1)
    at = 0
    for n, cnt in zip(REPLICATED, rep_sizes):
        g_final[n] = rep_all[:, at:at + cnt // N_DEV].reshape(wsh[n].shape)
        at += cnt // N_DEV

    stepped = {n: adamw("adamw_" + n, wsh[n], g_final[n], msh[n], vsh[n]) for n in ['w_mod'] + BIG_BF16}
    rest = [n for n in WEIGHTS if n not in stepped]
    flat = lambda d: _flat([d[n] for n in rest], F32, 8)
    rest_shapes = [wsh[n].shape for n in rest]
    rest_out = [_unflat(f, rest_shapes) for f in adamw("adamw_small", flat(wsh), flat(g_final), flat(msh), flat(vsh))]
    for i, n in enumerate(rest):
        stepped[n] = tuple(r[i] for r in rest_out)
    shaped = lambda n, a: a.reshape(given[n].shape)
    return (loss, grad_x[None],
            *[shaped(n, g_final[n]) for n in WEIGHTS],
            *[shaped(n, stepped[n][k]) for k in range(3) for n in WEIGHTS])
```

```python
import functools
import math

import jax
import jax.numpy as jnp
from jax import lax
from jax.experimental import pallas as pl
from jax.experimental.pallas import tpu as pltpu

F32 = jnp.float32
BF16 = jnp.bfloat16
MESH = pl.DeviceIdType.MESH

N_DEV = 8
D = 1024
N_HEADS = 8
HEAD_PAD = 128
QK_NOPE, QK_ROPE, V_HEAD = 64, 32, 64
QK_DIM = QK_NOPE + QK_ROPE
Q_RANK, KV_RANK = 384, 256
LRU_W, LRU_BLOCKS, LRU_BW = 1280, 10, 128
FFN = 2816
GRID_W = 64
ROPE_BASE = 10000.0
LRU_C = 8.0
EPS = 1e-6
Z_Q, Z_KV, Z_KR, Z_XB, Z_YB, Z_GL, Z_END = 0, 384, 640, 768, 2048, 3328, 5376
ADAM_LR, ADAM_B1, ADAM_B2, ADAM_EPS, ADAM_WD, ADAM_STEP = 0.001, 0.9, 0.999, 1e-08, 0.01, 10

VMEM_LIMIT = 52 * 1024 * 1024
FLAT_C = 512
BIG_ROWS = 256

WEIGHTS = ['c_ctx', 'w_mod', 'b_mod', 'norm1_g', 'w_in', 'b_gate', 'q_norm_g', 'kv_norm_g', 'w_uq', 'w_ukv',
           'w_o_attn', 'lru_conv_w', 'lru_conv_b', 'lru_w_a', 'lru_b_a', 'lru_w_x', 'lru_b_x', 'lru_lambda',
           'w_o_lru', 'w_out', 'norm2_g', 'w_up', 'ffn_conv_w', 'ffn_conv_b', 'w_down', 'final_g']
COL_SHARDED = ['w_in', 'w_uq', 'w_ukv', 'w_o_attn', 'lru_conv_w', 'lru_b_a', 'lru_b_x', 'lru_lambda', 'w_up',
               'ffn_conv_w']
ROW_SHARDED = ['w_o_lru', 'w_out', 'w_down']
BIG_BF16 = ['w_in', 'w_uq', 'w_ukv', 'w_o_attn', 'w_o_lru', 'w_out', 'w_up', 'w_down']
SMALL_F32 = ['lru_conv_w', 'lru_b_a', 'lru_b_x', 'lru_lambda', 'ffn_conv_w']
SHARDED = BIG_BF16 + SMALL_F32
REPLICATED = ['c_ctx', 'b_mod', 'norm1_g', 'b_gate', 'q_norm_g', 'kv_norm_g', 'lru_conv_b', 'lru_w_a', 'lru_w_x',
              'norm2_g', 'ffn_conv_b', 'final_g']


def _cparams(sem=None):
    return pltpu.CompilerParams(dimension_semantics=sem, vmem_limit_bytes=VMEM_LIMIT)


def _pick(n, cands):
    for c in cands:
        if c <= n and n % c == 0:
            return c
    return n


def _best_div(n, mult, cap):
    best = mult
    for d in range(mult, min(n, cap) + 1, mult):
        if n % d == 0:
            best = d
    return best


ROW_TILES = (1088, 1024, 544, 512, 256, 128, 64, 32, 16, 8)
LANE_TILES = (1408, 1024, 896, 768, 640, 512, 384, 256, 128)


def _my_pos():
    return lax.axis_index("x"), lax.axis_index("y"), lax.axis_index("c")


def _my_index():
    x, y, c = _my_pos()
    return 4 * x + 2 * y + c


def all_gather_multi(name, shards):
    n_arr = len(shards)
    arrays = range(n_arr)

    def body(*refs):
        x_refs, out_refs = refs[:n_arr], refs[n_arr:2 * n_arr]
        send_sems, recv_sems, local_sems = refs[2 * n_arr:]
        x, y, c = _my_pos()
        me, sibling = (x, y, c), (x, y, 1 - c)
        chips = [(1 - x, y), (x, 1 - y), (1 - x, 1 - y)]

        def slot(a, px, py, pc):
            return out_refs[a].at[4 * px + 2 * py + pc]

        def copy(a, k, block, to, src=None):
            return pltpu.make_async_remote_copy(
                src_ref=slot(a, *block) if src is None else src, dst_ref=slot(a, *block),
                send_sem=send_sems.at[7 * a + k], recv_sem=recv_sems.at[7 * a + k], device_id=to,
                device_id_type=MESH)

        mine = [pltpu.make_async_copy(x_refs[a], slot(a, *me), local_sems.at[a]) for a in arrays]
        first = [copy(a, 1 + j, me, (*chip, c), src=x_refs[a]) for j, chip in enumerate(chips) for a in arrays]
        first += [copy(a, 0, me, sibling, src=x_refs[a]) for a in arrays]
        for cp in first + mine:
            cp.start()
        passed = []
        for j, chip in enumerate(chips):
            for a in arrays:
                copy(a, 1 + j, (*chip, c), me).wait_recv()
                passed.append(copy(a, 4 + j, (*chip, c), sibling))
                passed[-1].start()
        for a in arrays:
            copy(a, 0, sibling, me).wait_recv()
            for j, chip in enumerate(chips):
                copy(a, 4 + j, (*chip, 1 - c), me).wait_recv()
        for cp in first + passed:
            cp.wait_send()
        for cp in mine:
            cp.wait()

    hbm = pl.BlockSpec(memory_space=pl.ANY)
    return pl.pallas_call(
        body, name=name,
        out_shape=[jax.ShapeDtypeStruct((N_DEV,) + s.shape, s.dtype) for s in shards],
        in_specs=[hbm] * n_arr, out_specs=[hbm] * n_arr,
        scratch_shapes=[pltpu.SemaphoreType.DMA((7 * n_arr,)), pltpu.SemaphoreType.DMA((7 * n_arr,)),
                        pltpu.SemaphoreType.DMA((n_arr,))],
    )(*shards)


def all_gather(name, shard):
    return all_gather_multi(name, [shard])[0]


def all_to_all_multi(name, chunk_arrays):
    n_arr = len(chunk_arrays)
    arrays = range(n_arr)

    def body(*refs):
        x_refs, out_refs = refs[:n_arr], refs[n_arr:2 * n_arr]
        send_sems, recv_sems, local_sems = refs[2 * n_arr:]
        x, y, c = _my_pos()
        me = 4 * x + 2 * y + c
        mine = [pltpu.make_async_copy(x_refs[a].at[me], out_refs[a].at[me], local_sems.at[a]) for a in arrays]
        sends, arrivals = [], []
        for rel in (6, 4, 2, 7, 5, 3, 1):
            dx, dy, dc = (rel >> 2) & 1, (rel >> 1) & 1, rel & 1
            px, py, pc = x ^ dx, y ^ dy, c ^ dc
            peer = 4 * px + 2 * py + pc
            for a in arrays:
                k = 7 * a + rel - 1
                sends.append(pltpu.make_async_remote_copy(
                    src_ref=x_refs[a].at[peer], dst_ref=out_refs[a].at[me],
                    send_sem=send_sems.at[k], recv_sem=recv_sems.at[k],
                    device_id=(px, py, pc), device_id_type=MESH))
                arrivals.append(pltpu.make_async_remote_copy(
                    src_ref=x_refs[a].at[peer], dst_ref=out_refs[a].at[peer],
                    send_sem=send_sems.at[k], recv_sem=recv_sems.at[k],
                    device_id=(x, y, c), device_id_type=MESH))
        for cp in sends + mine:
            cp.start()
        for cp in arrivals:
            cp.wait_recv()
        for cp in sends:
            cp.wait_send()
        for cp in mine:
            cp.wait()

    hbm = pl.BlockSpec(memory_space=pl.ANY)
    return pl.pallas_call(
        body, name=name,
        out_shape=[jax.ShapeDtypeStruct(s.shape, s.dtype) for s in chunk_arrays],
        in_specs=[hbm] * n_arr, out_specs=[hbm] * n_arr,
        scratch_shapes=[pltpu.SemaphoreType.DMA((7 * n_arr,)), pltpu.SemaphoreType.DMA((7 * n_arr,)),
                        pltpu.SemaphoreType.DMA((n_arr,))],
    )(*chunk_arrays)


def sum_slots(name, slots):
    _, r, ccols = slots.shape
    tc = _pick(ccols, (256, 128))

    def body(s_ref, o_ref):
        acc = s_ref[0].astype(F32)
        for p in range(1, N_DEV):
            acc = acc + s_ref[p].astype(F32)
        o_ref[...] = acc

    return pl.pallas_call(
        body, name=name, grid=(ccols // tc,),
        out_shape=jax.ShapeDtypeStruct((r, ccols), F32),
        in_specs=[pl.BlockSpec((N_DEV, r, tc), lambda j: (0, 0, j))],
        out_specs=pl.BlockSpec((r, tc), lambda j: (0, j)),
        compiler_params=_cparams(("parallel",)),
    )(slots)


def matmul(name, a, b, mode, out_dtype, tm=None, tn=None, tk=None):
    if mode == 'nn':
        (m, k), (k2, n) = a.shape, b.shape
    elif mode == 'nt':
        (m, k), (n, k2) = a.shape, b.shape
    else:
        (k, m), (k2, n) = a.shape, b.shape
    assert k == k2, (name, a.shape, b.shape, mode)
    if mode == 'tn':
        tm = tm or _pick(m, LANE_TILES)
        tk = tk or _pick(k, ROW_TILES)
    else:
        tm = tm or _pick(m, ROW_TILES)
        tk = tk or _pick(k, LANE_TILES)
    tn = tn or _pick(n, LANE_TILES)
    nk = k // tk
    if mode == 'nn':
        a_spec = pl.BlockSpec((tm, tk), lambda i, j, kk: (i, kk))
        b_spec = pl.BlockSpec((tk, tn), lambda i, j, kk: (kk, j))
        dn = (((1,), (0,)), ((), ()))
    elif mode == 'nt':
        a_spec = pl.BlockSpec((tm, tk), lambda i, j, kk: (i, kk))
        b_spec = pl.BlockSpec((tn, tk), lambda i, j, kk: (j, kk))
        dn = (((1,), (1,)), ((), ()))
    else:
        a_spec = pl.BlockSpec((tk, tm), lambda i, j, kk: (kk, i))
        b_spec = pl.BlockSpec((tk, tn), lambda i, j, kk: (kk, j))
        dn = (((0,), (0,)), ((), ()))

    def product(a_ref, b_ref):
        return lax.dot_general(a_ref[...].astype(BF16), b_ref[...].astype(BF16), dn, preferred_element_type=F32)

    def body_one(a_ref, b_ref, o_ref):
        o_ref[...] = product(a_ref, b_ref).astype(o_ref.dtype)

    def body(a_ref, b_ref, o_ref, acc_ref):
        kk = pl.program_id(2)

        @pl.when(kk == 0)
        def _():
            acc_ref[...] = jnp.zeros_like(acc_ref)

        acc_ref[...] += product(a_ref, b_ref)

        @pl.when(kk == nk - 1)
        def _():
            o_ref[...] = acc_ref[...].astype(o_ref.dtype)

    return pl.pallas_call(
        body_one if nk == 1 else body, name=name, grid=(m // tm, n // tn, nk),
        out_shape=jax.ShapeDtypeStruct((m, n), out_dtype),
        in_specs=[a_spec, b_spec],
        out_specs=pl.BlockSpec((tm, tn), lambda i, j, kk: (i, j)),
        scratch_shapes=[] if nk == 1 else [pltpu.VMEM((tm, tn), F32)],
        compiler_params=_cparams(("parallel", "parallel", "arbitrary")),
    )(a, b)


def rowwise(name, fn, rows, params, out_rows, out_accs, n_rows, t_lat, tm):
    nb = n_rows // tm
    in_specs, piece_counts = [], []
    operands = []
    for arr, off, width in rows:
        g = math.gcd(off, width) if off else width
        assert g % 128 == 0 or (off == 0 and width == arr.shape[1]), (name, off, width)
        cnt = width // g
        last = arr.shape[0] // tm - 1
        clamp = arr.shape[0] < n_rows
        for p in range(cnt):
            cb = off // g + p
            if clamp:
                in_specs.append(pl.BlockSpec((tm, g), lambda i, cb=cb, last=last: (jnp.minimum(i, last), cb)))
            else:
                in_specs.append(pl.BlockSpec((tm, g), lambda i, cb=cb: (i, cb)))
            operands.append(arr)
        piece_counts.append(cnt)
    for p in params:
        in_specs.append(pl.BlockSpec(p.shape, lambda i, nd=p.ndim: (0,) * nd))
        operands.append(p)
    n_in = sum(piece_counts)
    n_par = len(params)
    n_or = len(out_rows)
    out_shape = [jax.ShapeDtypeStruct((n_rows, w), dt) for w, dt in out_rows]
    out_shape += [jax.ShapeDtypeStruct(s, F32) for s in out_accs]
    out_specs = [pl.BlockSpec((tm, w), lambda i: (i, 0)) for w, _ in out_rows]
    out_specs += [pl.BlockSpec(s, lambda i, nd=len(s): (0,) * nd) for s in out_accs]

    def body(*refs):
        in_refs, par_refs = refs[:n_in], refs[n_in:n_in + n_par]
        orow_refs = refs[n_in + n_par:n_in + n_par + n_or]
        oacc_refs = refs[n_in + n_par + n_or:]
        i = pl.program_id(0)
        tiles, at = [], 0
        for cnt in piece_counts:
            parts = [in_refs[at + p][...].astype(F32) for p in range(cnt)]
            tiles.append(parts[0] if cnt == 1 else jnp.concatenate(parts, axis=1))
            at += cnt
        is_ctx = i * tm >= t_lat
        outs, accs = fn(is_ctx, tiles, [p[...] for p in par_refs])
        for o_ref, o in zip(orow_refs, outs):
            o_ref[...] = o.astype(o_ref.dtype)
        if oacc_refs:
            @pl.when(i == 0)
            def _():
                for a_ref in oacc_refs:
                    a_ref[...] = jnp.zeros_like(a_ref)
            for a_ref, a in zip(oacc_refs, accs):
                a_ref[...] += a.astype(F32)

    res = pl.pallas_call(
        body, name=name, grid=(nb,),
        out_shape=out_shape, in_specs=in_specs, out_specs=out_specs,
        compiler_params=_cparams(("arbitrary",)),
    )(*operands)
    return res[:n_or], res[n_or:]


def _rms(x, g):
    return x * lax.rsqrt(jnp.mean(x * x, axis=-1, keepdims=True) + EPS) * g


def _norm_mod(x, g, sc, sh):
    return _rms(x, g) * (1.0 + sc) + sh


def _sigmoid(x):
    return 1.0 / (1.0 + jnp.exp(-x))


def _silu(x):
    return x * _sigmoid(x)


def _gelu(x):
    return 0.5 * x * (1.0 + jnp.tanh(math.sqrt(2.0 / math.pi) * (x + 0.044715 * (x * x * x))))


def _sel(is_ctx, p):
    return jnp.where(is_ctx, p[1:2], p[0:1])


def _seg_acc(is_ctx, v):
    rows = lax.broadcasted_iota(jnp.int32, (2, v.shape[1]), 0)
    return jnp.where(rows == is_ctx.astype(jnp.int32), jnp.broadcast_to(v, (2, v.shape[1])), 0.0)


def _rsum(v):
    return jnp.sum(v, axis=0, keepdims=True)


def _shift_rows(x, o, t_lat, n):
    if o == 0:
        return x
    y = pltpu.roll(x, (-o) % n, 0)
    t = lax.broadcasted_iota(jnp.int32, x.shape, 0)
    src = t + o
    ok = (src >= 0) & (src < n) & ((src >= t_lat) == (t >= t_lat))
    return jnp.where(ok, y, 0.0)


def conv_fwd(name, xarr, col_off, width, w, b, left, n_rows, t_lat, out_dtype, cb=128):
    taps = w.shape[0]
    assert col_off % cb == 0 and width % cb == 0

    def body(x_ref, w_ref, b_ref, o_ref):
        x = x_ref[...].astype(F32)
        acc = jnp.broadcast_to(b_ref[...], x.shape)
        for k in range(taps):
            acc = acc + _shift_rows(x, k - left, t_lat, n_rows) * w_ref[k:k + 1, :]
        o_ref[...] = acc.astype(o_ref.dtype)

    return pl.pallas_call(
        body, name=name, grid=(width // cb,),
        out_shape=jax.ShapeDtypeStruct((n_rows, width), out_dtype),
        in_specs=[pl.BlockSpec((n_rows, cb), lambda j: (0, col_off // cb + j)),
                  pl.BlockSpec((taps, cb), lambda j: (0, j)),
                  pl.BlockSpec((1, cb), lambda j: (0, j))],
        out_specs=pl.BlockSpec((n_rows, cb), lambda j: (0, j)),
        compiler_params=_cparams(("parallel",)),
    )(xarr, w, b)


def conv_bwd(name, dout, xarr, col_off, width, w, left, n_rows, t_lat, cb=128):
    taps = w.shape[0]

    def body(d_ref, x_ref, w_ref, dx_ref, dw_ref, db_ref):
        d = d_ref[...].astype(F32)
        x = x_ref[...].astype(F32)
        dx = jnp.zeros_like(d)
        dws = []
        for k in range(taps):
            dx = dx + _shift_rows(d, left - k, t_lat, n_rows) * w_ref[k:k + 1, :]
            dws.append(_rsum(d * _shift_rows(x, k - left, t_lat, n_rows)))
        dx_ref[...] = dx.astype(dx_ref.dtype)
        dw_ref[...] = jnp.concatenate(dws, axis=0)
        db_ref[...] = _rsum(d)

    return pl.pallas_call(
        body, name=name, grid=(width // cb,),
        out_shape=[jax.ShapeDtypeStruct((n_rows, width), BF16), jax.ShapeDtypeStruct((taps, width), F32),
                   jax.ShapeDtypeStruct((1, width), F32)],
        in_specs=[pl.BlockSpec((n_rows, cb), lambda j: (0, j)),
                  pl.BlockSpec((n_rows, cb), lambda j: (0, col_off // cb + j)),
                  pl.BlockSpec((taps, cb), lambda j: (0, j))],
        out_specs=[pl.BlockSpec((n_rows, cb), lambda j: (0, j)), pl.BlockSpec((taps, cb), lambda j: (0, j)),
                   pl.BlockSpec((1, cb), lambda j: (0, j))],
        compiler_params=_cparams(("parallel",)),
    )(dout, xarr, w)


def _chunk_order(direction, nb, nbl):
    if direction == 'f':
        return lambda s: ((s + nbl) % nb, 0)
    return lambda s: (nb - 1 - s, 0)


def _adjoint_order(direction, nb, nbl):
    if direction == 'f':
        return lambda s: ((nb - 1 - s + nbl) % nb, 0)
    return lambda s: (s, 0)


SUBLANES = 8


def _chunk_scan(a, b, carry, rev):
    tc = a.shape[0]
    row = lax.broadcasted_iota(jnp.int32, a.shape, 0)
    in_tile = jnp.bitwise_and(row, SUBLANES - 1)
    for k in (1, 2, 4):
        shift = tc - k if rev else k
        edge = in_tile >= SUBLANES - k if rev else in_tile < k
        b = jnp.where(edge, b, a * pltpu.roll(b, shift, 0) + b)
        a = jnp.where(edge, a, a * pltpu.roll(a, shift, 0))
    nt = tc // SUBLANES
    hs = [None] * nt
    c = carry
    for kt in range(nt):
        k = nt - 1 - kt if rev else kt
        h = b[k * SUBLANES:(k + 1) * SUBLANES] + a[k * SUBLANES:(k + 1) * SUBLANES] * c
        hs[k] = h
        c = h[0:1] if rev else h[SUBLANES - 1:SUBLANES]
    h = jnp.concatenate(hs, axis=0)
    if rev:
        return h, jnp.where(row == tc - 1, carry, pltpu.roll(h, tc - 1, 0)), c
    return h, jnp.where(row == 0, carry, pltpu.roll(h, 1, 0)), c


def scan_fwd(name, a, u, direction, n_rows, t_lat, tc=128):
    w = a.shape[1]
    nb, nbl = n_rows // tc, t_lat // tc
    order = _chunk_order(direction, nb, nbl)
    rev = direction == 'b'

    def body(a_ref, u_ref, h_ref, hp_ref, carry):
        @pl.when(pl.program_id(0) == 0)
        def _():
            carry[...] = jnp.zeros_like(carry)

        h_ref[...], hp_ref[...], carry[...] = _chunk_scan(a_ref[...], u_ref[...], carry[...], rev)

    spec = pl.BlockSpec((tc, w), order)
    return pl.pallas_call(
        body, name=name, grid=(nb,),
        out_shape=[jax.ShapeDtypeStruct((n_rows, w), F32)] * 2,
        in_specs=[spec, spec], out_specs=[spec, spec],
        scratch_shapes=[pltpu.VMEM((1, w), F32)],
        compiler_params=_cparams(("arbitrary",)),
    )(a, u)


def scan_adj(name, a, dh, hprev, direction, n_rows, t_lat, tc=128):
    w = a.shape[1]
    nb, nbl = n_rows // tc, t_lat // tc
    order = _adjoint_order(direction, nb, nbl)
    rev = direction == 'f'

    def dh_order(s):
        c, _ = order(s)
        return (jnp.minimum(c, nbl - 1), 0)

    def body(a_ref, dh_ref, hp_ref, du_ref, da_ref, carry):
        s = pl.program_id(0)

        @pl.when(s == 0)
        def _():
            carry[...] = jnp.zeros_like(carry)

        chunk, _ = order(s)
        live = (chunk < nbl).astype(F32)

        av = a_ref[...]
        dv = dh_ref[...] * live
        _, c_next, carry[...] = _chunk_scan(av, av * dv, carry[...], rev)
        lam = dv + c_next
        du_ref[...] = lam
        da_ref[...] = lam * hp_ref[...]

    spec = pl.BlockSpec((tc, w), order)
    return pl.pallas_call(
        body, name=name, grid=(nb,),
        out_shape=[jax.ShapeDtypeStruct((n_rows, w), F32)] * 2,
        in_specs=[spec, pl.BlockSpec((tc, w), dh_order), spec], out_specs=[spec, spec],
        scratch_shapes=[pltpu.VMEM((1, w), F32)],
        compiler_params=_cparams(("arbitrary",)),
    )(a, dh, hprev)


def _neg_expm1(y):
    series = -(y * (1.0 + y * (0.5 + y * (1.0 / 6.0 + y * (1.0 / 24.0)))))
    return jnp.where(y > -0.03, series, 1.0 - jnp.exp(y))


def _gate_elem(pre_r, pre_i, xc, b_a, b_x, sp):
    r = _sigmoid(pre_r + b_a)
    i = _sigmoid(pre_i + b_x)
    log_a = (-LRU_C) * r * sp
    a = jnp.exp(log_a)
    mult = jnp.sqrt(_neg_expm1(2.0 * log_a))
    return a, mult * (i * xc)


def _gate_elem_bwd(pre_r, pre_i, xc, b_a, b_x, sp, da, du):
    r = _sigmoid(pre_r + b_a)
    i = _sigmoid(pre_i + b_x)
    log_a = (-LRU_C) * r * sp
    a = jnp.exp(log_a)
    m2 = _neg_expm1(2.0 * log_a)
    inv_mult = lax.rsqrt(m2)
    g = du * (m2 * inv_mult)
    d_mult = du * (i * xc)
    d_log_a = (da - d_mult * a * inv_mult) * a
    d_pre_r = d_log_a * ((-LRU_C) * sp) * (r * (1.0 - r))
    d_pre_i = g * xc * (i * (1.0 - i))
    return d_pre_r, d_pre_i, g * i, _rsum(d_log_a * ((-LRU_C) * r))


def _blockdiag(xb16, w_ref_val, d):
    outs = []
    for n in range(LRU_BLOCKS):
        outs.append(jnp.dot(xb16[:, n * LRU_BW:(n + 1) * LRU_BW], w_ref_val[d * LRU_BLOCKS + n],
                            preferred_element_type=F32))
    return jnp.concatenate(outs, axis=1)


def gates_fwd(xc, w_a, w_x, b_a, b_x, sp, n_rows, t_lat, tm):
    def fn(is_ctx, rows, params):
        (x,), (wa, wx, ba, bx, spv) = rows, params
        xb16 = x.astype(BF16)
        outs = []
        for d in range(2):
            a, u = _gate_elem(_blockdiag(xb16, wa, d), _blockdiag(xb16, wx, d), x,
                              ba[d:d + 1], bx[d:d + 1], spv[d:d + 1])
            outs += [a, u]
        return outs, []

    (a_f, u_f, a_b, u_b), _ = rowwise("gates_fwd", fn, [(xc, 0, LRU_W)], [w_a, w_x, b_a, b_x, sp],
                                      [(LRU_W, F32)] * 4, [], n_rows, t_lat, tm)
    return a_f, u_f, a_b, u_b


def gates_bwd(xc, da_f, du_f, da_b, du_b, w_a, w_x, b_a, b_x, sp, n_rows, t_lat, tm):
    def fn(is_ctx, rows, params):
        (x, daf, duf, dab, dub), (wa, wx, ba, bx, spv) = rows, params
        xb16 = x.astype(BF16)
        dxc = jnp.zeros_like(x)
        dwa, dwx, dba, dbx, dsp = [], [], [], [], []
        for d, (da, du) in enumerate(((daf, duf), (dab, dub))):
            dpr, dpi, dx_e, dsp_d = _gate_elem_bwd(_blockdiag(xb16, wa, d), _blockdiag(xb16, wx, d), x,
                                                   ba[d:d + 1], bx[d:d + 1], spv[d:d + 1], da, du)
            dba_d, dbx_d = _rsum(dpr), _rsum(dpi)
            dxc = dxc + dx_e
            dpr16, dpi16 = dpr.astype(BF16), dpi.astype(BF16)
            back = []
            for n in range(LRU_BLOCKS):
                sl = slice(n * LRU_BW, (n + 1) * LRU_BW)
                nt_dims = (((1,), (1,)), ((), ()))
                back.append(lax.dot_general(dpr16[:, sl], wa[d * LRU_BLOCKS + n], nt_dims, preferred_element_type=F32)
                            + lax.dot_general(dpi16[:, sl], wx[d * LRU_BLOCKS + n], nt_dims,
                                              preferred_element_type=F32))
                tn_dims = (((0,), (0,)), ((), ()))
                dwa.append(lax.dot_general(xb16[:, sl], dpr16[:, sl], tn_dims, preferred_element_type=F32)[None])
                dwx.append(lax.dot_general(xb16[:, sl], dpi16[:, sl], tn_dims, preferred_element_type=F32)[None])
            dxc = dxc + jnp.concatenate(back, axis=1)
            dba.append(dba_d)
            dbx.append(dbx_d)
            dsp.append(dsp_d)
        cat0 = lambda xs: jnp.concatenate(xs, axis=0)
        return [dxc], [cat0(dwa), cat0(dwx), cat0(dba), cat0(dbx), cat0(dsp)]

    (dxc,), accs = rowwise("gates_bwd", fn,
                           [(xc, 0, LRU_W), (da_f, 0, LRU_W), (du_f, 0, LRU_W), (da_b, 0, LRU_W), (du_b, 0, LRU_W)],
                           [w_a, w_x, b_a, b_x, sp], [(LRU_W, F32)],
                           [(2 * LRU_BLOCKS, LRU_BW, LRU_BW)] * 2 + [(2, LRU_W)] * 3, n_rows, t_lat, tm)
    return dxc, accs


def _rope_tables(t_lat, n_rows):
    rows = t_lat // GRID_W
    row_ids = jnp.repeat(jnp.arange(rows), GRID_W).astype(F32)
    col_ids = jnp.tile(jnp.arange(GRID_W), rows).astype(F32)
    axis_dim = QK_ROPE // 2
    inv = 1.0 / (ROPE_BASE ** (jnp.arange(0, axis_dim, 2, dtype=F32) / axis_dim))
    ang = jnp.concatenate([row_ids[:, None] * inv, col_ids[:, None] * inv], axis=-1)
    cos, sin = jnp.cos(ang), jnp.sin(ang)
    half = QK_ROPE // 2
    ones, zeros = jnp.ones((t_lat, QK_NOPE), F32), jnp.zeros((t_lat, QK_NOPE), F32)
    pad1, pad0 = jnp.ones((t_lat, HEAD_PAD - QK_DIM), F32), jnp.zeros((t_lat, HEAD_PAD - QK_DIM), F32)
    zh = jnp.zeros((t_lat, half), F32)
    c_tab = jnp.concatenate([ones, cos, cos, pad1], axis=1)
    s1 = jnp.concatenate([zeros, -sin, zh, pad0], axis=1)
    s2 = jnp.concatenate([zeros, zh, sin, pad0], axis=1)
    n_ctx = n_rows - t_lat
    c_tab = jnp.concatenate([c_tab, jnp.ones((n_ctx, HEAD_PAD), F32)], axis=0)
    s1 = jnp.concatenate([s1, jnp.zeros((n_ctx, HEAD_PAD), F32)], axis=0)
    s2 = jnp.concatenate([s2, jnp.zeros((n_ctx, HEAD_PAD), F32)], axis=0)
    return c_tab, s1, s2


def _rope(x, c, s1, s2):
    half = QK_ROPE // 2
    return x * c + pltpu.roll(x, HEAD_PAD - half, 1) * s1 + pltpu.roll(x, half, 1) * s2


def _rope_t(dy, c, s1, s2):
    half = QK_ROPE // 2
    return dy * c + pltpu.roll(dy * s1, half, 1) + pltpu.roll(dy * s2, HEAD_PAD - half, 1)


def _heads(x):
    return [x[:, h * HEAD_PAD:(h + 1) * HEAD_PAD] for h in range(N_HEADS)]


Q_SCALE = QK_DIM ** -0.5 * math.log2(math.e)


def attn_fwd(q, k, v, t_lat, n_rows, tq):
    def body(q_ref, k_ref, v_ref, o_ref, lse_ref):
        s = lax.dot_general(q_ref[...], k_ref[...], (((1,), (1,)), ((), ())), preferred_element_type=F32)
        m = jnp.max(s, axis=-1, keepdims=True)
        p = jnp.exp2(s - m)
        l = jnp.sum(p, axis=-1, keepdims=True)
        o = jnp.dot(p.astype(BF16), v_ref[...], preferred_element_type=F32) / l
        o_ref[...] = o.astype(o_ref.dtype)
        lse_ref[...] = jnp.broadcast_to(m + jnp.log2(l), lse_ref.shape)

    qspec = pl.BlockSpec((tq, HEAD_PAD), lambda h, i: (i, h))
    kspec = pl.BlockSpec((n_rows, HEAD_PAD), lambda h, i: (0, h))
    return pl.pallas_call(
        body, name="attn_fwd", grid=(N_HEADS, t_lat // tq),
        out_shape=[jax.ShapeDtypeStruct((t_lat, N_HEADS * HEAD_PAD), BF16),
                   jax.ShapeDtypeStruct((t_lat, N_HEADS * HEAD_PAD), F32)],
        in_specs=[qspec, kspec, kspec], out_specs=[qspec, qspec],
        compiler_params=_cparams(("parallel", "arbitrary")),
    )(q, k, v)


def attn_bwd(q, k, v, o, do, lse, t_lat, n_rows, tq):
    scale = QK_DIM ** -0.5
    nq = t_lat // tq
    nt = (((1,), (1,)), ((), ()))
    tn = (((0,), (0,)), ((), ()))

    def body(q_ref, k_ref, v_ref, o_ref, do_ref, lse_ref, dq_ref, dk_ref, dv_ref):
        @pl.when(pl.program_id(1) == 0)
        def _():
            dk_ref[...] = jnp.zeros_like(dk_ref)
            dv_ref[...] = jnp.zeros_like(dv_ref)

        qv, kv, vv, dov = q_ref[...], k_ref[...], v_ref[...], do_ref[...]
        s = lax.dot_general(qv, kv, nt, preferred_element_type=F32)
        p = jnp.exp2(s - lse_ref[:, 0:1])
        dv_ref[...] += lax.dot_general(p.astype(BF16), dov, tn, preferred_element_type=F32)
        dp = lax.dot_general(dov, vv, nt, preferred_element_type=F32)
        delta = jnp.sum(dov.astype(F32) * o_ref[...].astype(F32), axis=-1, keepdims=True)
        ds = (p * (dp - delta)).astype(BF16)
        dq_ref[...] = jnp.dot(ds, kv, preferred_element_type=F32) * scale
        dk_ref[...] += lax.dot_general(ds, qv, tn, preferred_element_type=F32)

        @pl.when(pl.program_id(1) == nq - 1)
        def _():
            dk_ref[...] = dk_ref[...] * (scale / Q_SCALE)

    qspec = pl.BlockSpec((tq, HEAD_PAD), lambda h, i: (i, h))
    kspec = pl.BlockSpec((n_rows, HEAD_PAD), lambda h, i: (0, h))
    return pl.pallas_call(
        body, name="attn_bwd", grid=(N_HEADS, t_lat // tq),
        out_shape=[jax.ShapeDtypeStruct((t_lat, N_HEADS * HEAD_PAD), F32),
                   jax.ShapeDtypeStruct((n_rows, N_HEADS * HEAD_PAD), F32),
                   jax.ShapeDtypeStruct((n_rows, N_HEADS * HEAD_PAD), F32)],
        in_specs=[qspec, kspec, kspec, qspec, qspec, qspec], out_specs=[qspec, kspec, kspec],
        compiler_params=_cparams(("parallel", "arbitrary")),
    )(q, k, v, o, do, lse)


def adamw(name, w, g, m, v):
    r, ccols = w.shape
    tr = _best_div(r, 8, max(8, 262144 // ccols)) if r % 8 == 0 else r
    c1 = 1.0 - ADAM_B1 ** ADAM_STEP
    c2 = 1.0 - ADAM_B2 ** ADAM_STEP

    def body(w_ref, g_ref, m_ref, v_ref, d_ref, nm_ref, nv_ref):
        gv = g_ref[...]
        nm = ADAM_B1 * m_ref[...] + (1.0 - ADAM_B1) * gv
        nv = ADAM_B2 * v_ref[...] + (1.0 - ADAM_B2) * (gv * gv)
        d_ref[...] = -ADAM_LR * ((nm / c1) / (jnp.sqrt(nv / c2) + ADAM_EPS) + ADAM_WD * w_ref[...])
        nm_ref[...] = nm
        nv_ref[...] = nv

    spec = pl.BlockSpec((tr, ccols), lambda i: (i, 0))
    return pl.pallas_call(
        body, name=name, grid=(r // tr,),
        out_shape=[jax.ShapeDtypeStruct((r, ccols), F32)] * 3,
        in_specs=[spec] * 4, out_specs=[spec] * 3,
        compiler_params=_cparams(("parallel",)),
    )(w, g, m, v)


def _flat(parts, dtype, row_mult):
    v = jnp.concatenate([p.reshape(-1).astype(dtype) for p in parts])
    quantum = row_mult * FLAT_C
    total = -(-v.shape[0] // quantum) * quantum
    return jnp.pad(v, (0, total - v.shape[0])).reshape(total // FLAT_C, FLAT_C)


def _unflat(flat, shapes):
    v = flat.reshape(-1)
    out, at = [], 0
    for s in shapes:
        n = math.prod(s)
        out.append(v[at:at + n].reshape(s))
        at += n
    return out


def _gathered_to_full(name, g):
    k = g.shape[1]
    return jnp.transpose(g, (1, 0, 2)).reshape(k, N_DEV * g.shape[2])


def _full_to_chunks(name, full):
    k, n = full.shape
    return jnp.transpose(full.reshape(k, N_DEV, n // N_DEV), (1, 0, 2)).reshape(N_DEV, -1)


def _shard_to_rb(name, w):
    return w if name in ROW_SHARDED else w.T


def _rb_to_shard(name, g):
    return g if name in ROW_SHARDED else g.T


def _rb_from_gathered(name, g):
    cols = g.shape[2]
    if name == 'w_in':
        z = lambda k: jnp.zeros((k, cols), g.dtype)
        full = g.reshape(N_DEV * g.shape[1], cols)
        return jnp.concatenate([full[:Z_KR], z(QK_NOPE), full[Z_KR:Z_KR + QK_ROPE], z(HEAD_PAD - QK_DIM),
                                full[Z_KR + QK_ROPE:]], axis=0)
    if name == 'w_uq':
        return jnp.pad(g, ((0, 0), (0, HEAD_PAD - QK_DIM), (0, 0))).reshape(N_HEADS * HEAD_PAD, cols)
    if name == 'w_ukv':
        pad = lambda t: jnp.pad(t, ((0, 0), (0, HEAD_PAD - t.shape[1]), (0, 0))).reshape(N_HEADS * HEAD_PAD, cols)
        return jnp.concatenate([pad(g[:, :QK_NOPE]), pad(g[:, QK_NOPE:])], axis=0)
    if name == 'w_o_attn':
        full = g.reshape(D, N_HEADS, V_HEAD)
        return jnp.pad(full, ((0, 0), (0, 0), (0, HEAD_PAD - V_HEAD))).reshape(D, N_HEADS * HEAD_PAD)
    return g.reshape(N_DEV * g.shape[1], cols)


def _chunks_from_rb_grad(name, g):
    cols = g.shape[1]
    if name == 'w_in':
        full = jnp.concatenate([g[:Z_KR], g[Z_KR + QK_NOPE:Z_KR + QK_DIM], g[Z_XB:]], axis=0)
        return full.reshape(N_DEV, -1, cols)
    if name == 'w_uq':
        return g.reshape(N_HEADS, HEAD_PAD, cols)[:, :QK_DIM]
    if name == 'w_ukv':
        half = N_HEADS * HEAD_PAD
        gk = g[:half].reshape(N_HEADS, HEAD_PAD, cols)[:, :QK_NOPE]
        gv = g[half:].reshape(N_HEADS, HEAD_PAD, cols)[:, :V_HEAD]
        return jnp.concatenate([gk, gv], axis=1)
    if name == 'w_o_attn':
        full = g.reshape(D, N_HEADS, HEAD_PAD)[:, :, :V_HEAD].reshape(D, N_HEADS * V_HEAD)
        return full.reshape(N_DEV, D // N_DEV, N_HEADS * V_HEAD)
    return g.reshape(N_DEV, -1, cols)


def local_step(x, ctx, target, mod_l, mod_c, wt):
    t_lat, n_ctx = x.shape[0], ctx.shape[0]
    n = t_lat + n_ctx
    tm = _pick(math.gcd(t_lat, n), (256, 128))
    tq = _pick(t_lat, (256, 128))
    row = lambda v: v.reshape(1, -1).astype(F32)
    two = lambda a, b: jnp.stack([a, b]).astype(F32)
    sh1_l, sc1_l, g1_l, sh2_l, sc2_l, g2_l = jnp.split(mod_l, 6)
    sh1_c, sc1_c = jnp.split(mod_c, 6)[:2]
    sc1, sh1 = two(sc1_l, sc1_c), two(sh1_l, sh1_c)
    g1, g2, sc2, sh2 = row(g1_l), row(g2_l), row(sc2_l), row(sh2_l)
    norm1_g, norm2_g, final_g = row(wt['norm1_g']), row(wt['norm2_g']), row(wt['final_g'])
    q_g, kv_g, b_gate = row(wt['q_norm_g']), row(wt['kv_norm_g']), row(wt['b_gate'])
    w_in_t, w_uq_t, w_ukv_t, w_o_attn_t, w_up_t = wt['w_in'], wt['w_uq'], wt['w_ukv'], wt['w_o_attn'], wt['w_up']
    w_o_lru, w_out, w_down = wt['w_o_lru'], wt['w_out'], wt['w_down']
    lru_w_a = wt['lru_w_a'].reshape(2 * LRU_BLOCKS, LRU_BW, LRU_BW).astype(BF16)
    lru_w_x = wt['lru_w_x'].reshape(2 * LRU_BLOCKS, LRU_BW, LRU_BW).astype(BF16)
    b_a, b_x, lam = wt['lru_b_a'], wt['lru_b_x'], wt['lru_lambda']
    sp = jnp.logaddexp(-lam, 0.0)
    c_tab, s1_tab, s2_tab = _rope_tables(t_lat, n)
    rw = functools.partial(rowwise, n_rows=n, t_lat=t_lat, tm=tm)
    rw_lat = functools.partial(rowwise, n_rows=t_lat, t_lat=t_lat, tm=tm)

    xs = jnp.concatenate([x, ctx], axis=0)

    def f_norm1(is_ctx, rows, params):
        (xv,), (g, sc, sh) = rows, params
        return [_norm_mod(xv, g, _sel(is_ctx, sc), _sel(is_ctx, sh))], []

    (h,), _ = rw("norm1", f_norm1, [(xs, 0, D)], [norm1_g, sc1, sh1], [(D, BF16)], [])
    z = matmul("w_in", h, w_in_t, 'nt', BF16)

    def f_qkv_norm(is_ctx, rows, params):
        (ql, kvl), (gq, gkv) = rows, params
        return [_rms(ql, gq), _rms(kvl, gkv)], []

    (qn, kvn), _ = rw("qkv_norm", f_qkv_norm, [(z, Z_Q, Q_RANK), (z, Z_KV, KV_RANK)], [q_g, kv_g],
                      [(Q_RANK, BF16), (KV_RANK, BF16)], [])
    qp = matmul("w_uq", qn, w_uq_t, 'nt', F32)
    kvp = matmul("w_ukv", kvn, w_ukv_t, 'nt', F32)

    def f_rope(is_ctx, rows, params):
        qv, kk, vv, kr, c, s1, s2 = rows
        krr = _rope(kr, c, s1, s2)
        qo = jnp.concatenate([_rope(qh, c, s1, s2) for qh in _heads(qv)], axis=1) * Q_SCALE
        ko = jnp.concatenate([kh + krr for kh in _heads(kk)], axis=1)
        return [qo, ko, vv], []

    hp = N_HEADS * HEAD_PAD
    (qr, kr_, vr), _ = rw("rope", f_rope,
                          [(qp, 0, hp), (kvp, 0, hp), (kvp, hp, hp), (z, Z_KR, HEAD_PAD), (c_tab, 0, HEAD_PAD),
                           (s1_tab, 0, HEAD_PAD), (s2_tab, 0, HEAD_PAD)], [], [(hp, BF16)] * 3, [])
    attn, lse = attn_fwd(qr, kr_, vr, t_lat, n, tq)

    xc = conv_fwd("lru_conv", z, Z_XB, LRU_W, wt['lru_conv_w'], row(wt['lru_conv_b']), 2, n, t_lat, F32)
    a_f, u_f, a_b, u_b = gates_fwd(xc, lru_w_a, lru_w_x, b_a, b_x, sp, n, t_lat, tm)
    h_f, hp_f = scan_fwd("scan_f", a_f, u_f, 'f', n, t_lat)
    h_b, hp_b = scan_fwd("scan_b", a_b, u_b, 'b', n, t_lat)

    def f_lru_out(is_ctx, rows, params):
        hf, hb, yb = rows
        return [(hf + hb) * _gelu(yb)], []

    (ybin,), _ = rw_lat("lru_out", f_lru_out, [(h_f, 0, LRU_W), (h_b, 0, LRU_W), (z, Z_YB, LRU_W)], [],
                        [(LRU_W, BF16)], [])
    y_a = matmul("w_o_attn", attn, w_o_attn_t, 'nt', F32)
    y_b = matmul("w_o_lru", ybin, w_o_lru, 'nn', F32)

    def _merge(ya, yb, gl, bg):
        gates = _sigmoid(gl + bg)
        return gates[:, :D] * ya + gates[:, D:] * yb

    def f_merge(is_ctx, rows, params):
        (ya, yb, gl), (bg,) = rows, params
        return [_merge(ya, yb, gl, bg)], []

    (mrg,), _ = rw_lat("merge", f_merge, [(y_a, 0, D), (y_b, 0, D), (z, Z_GL, 2 * D)], [b_gate], [(D, BF16)], [])
    o = matmul("w_out", mrg, w_out, 'nn', F32)

    def _res_norm2(xv, ov, g1v, g, sc, sh):
        x1 = xv + g1v * ov
        return x1, _norm_mod(x1, g, sc, sh)

    def f_norm2(is_ctx, rows, params):
        (xv, ov), (g1v, g, sc, sh) = rows, params
        x1, h2v = _res_norm2(xv, ov, g1v, g, sc, sh)
        return [x1, h2v], []

    (x1, h2), _ = rw_lat("norm2", f_norm2, [(x, 0, D), (o, 0, D)], [g1, norm2_g, sc2, sh2], [(D, F32), (D, BF16)], [])
    u = matmul("w_up", h2, w_up_t, 'nt', BF16)
    ac = conv_fwd("ffn_conv", u, 0, FFN, wt['ffn_conv_w'], row(wt['ffn_conv_b']), 1, t_lat, t_lat, BF16)

    def f_ffn_act(is_ctx, rows, params):
        acv, gv = rows
        return [_silu(acv) * gv], []

    (f,), _ = rw_lat("ffn_act", f_ffn_act, [(ac, 0, FFN), (u, FFN, FFN)], [], [(FFN, BF16)], [])
    dn = matmul("w_down", f, w_down, 'nn', F32)

    def _tile_loss(x1v, dv, g2v, fg, tgt):
        y = _rms(x1v + g2v * dv, fg)
        e = y - tgt
        return 0.5 * jnp.sum(jnp.mean(e * e, axis=-1, keepdims=True), axis=0, keepdims=True)

    def f_final(is_ctx, rows, params):
        (x1v, dv, tgt), (g2v, fg) = rows, params
        lv, vjp = jax.vjp(lambda a, b, c, d: _tile_loss(a, b, c, d, tgt), x1v, dv, g2v, fg)
        dx2, dd, dg2, dfg = vjp(jnp.ones((1, 1), F32))
        return [dx2, dd], [dg2, dfg, jnp.broadcast_to(lv, (1, 128))]

    (dx2, dd), (dg2, dfinal_g, loss_v) = rw_lat("final", f_final, [(x1, 0, D), (dn, 0, D), (target, 0, D)],
                                                [g2, final_g], [(D, F32), (D, BF16)], [(1, D), (1, D), (1, 128)])
    loss = loss_v[0, 0]

    grads = {'final_g': dfinal_g}
    df = matmul("d_f", dd, w_down, 'nt', BF16)
    grads['w_down'] = matmul("g_w_down", f, dd, 'tn', BF16)

    def b_ffn_act(is_ctx, rows, params):
        acv, gv, dfv = rows
        _, vjp = jax.vjp(lambda a, g: _silu(a) * g, acv, gv)
        dac, dg = vjp(dfv)
        return [dac, dg], []

    (dac, dgate), _ = rw_lat("ffn_act_bwd", b_ffn_act, [(ac, 0, FFN), (u, FFN, FFN), (df, 0, FFN)], [],
                             [(FFN, BF16), (FFN, BF16)], [])
    da, grads['ffn_conv_w'], grads['ffn_conv_b'] = conv_bwd("ffn_conv_bwd", dac, u, 0, FFN, wt['ffn_conv_w'], 1,
                                                            t_lat, t_lat)
    du = jnp.concatenate([da, dgate], axis=1)
    dh2 = matmul("d_h2", du, w_up_t, 'nn', F32)
    grads['w_up'] = matmul("g_w_up", du, h2, 'tn', BF16)

    def b_norm2(is_ctx, rows, params):
        (xv, ov, dh2v, dx2v), (g1v, g, sc, sh) = rows, params
        _, vjp = jax.vjp(_res_norm2, xv, ov, g1v, g, sc, sh)
        dx, do, dg1v, dg, dsc, dsh = vjp((dx2v, dh2v))
        return [dx, do], [dg1v, dg, dsc, dsh]

    (dx_res, do), (dg1, dnorm2_g, dsc2, dsh2) = rw_lat(
        "norm2_bwd", b_norm2, [(x, 0, D), (o, 0, D), (dh2, 0, D), (dx2, 0, D)], [g1, norm2_g, sc2, sh2],
        [(D, F32), (D, BF16)], [(1, D)] * 4)
    grads['norm2_g'] = dnorm2_g
    dmrg = matmul("d_merge", do, w_out, 'nt', F32)
    grads['w_out'] = matmul("g_w_out", mrg, do, 'tn', BF16)

    def b_merge(is_ctx, rows, params):
        (ya, yb, gl, dm), (bg,) = rows, params
        _, vjp = jax.vjp(_merge, ya, yb, gl, bg)
        dya, dyb, dgl, dbg = vjp(dm)
        return [dya, dyb, dgl], [dbg]

    (dy_a, dy_b, dgl), (grads['b_gate'],) = rw_lat(
        "merge_bwd", b_merge, [(y_a, 0, D), (y_b, 0, D), (z, Z_GL, 2 * D), (dmrg, 0, D)], [b_gate],
        [(D, BF16), (D, BF16), (2 * D, BF16)], [(1, 2 * D)])
    dattn = matmul("d_attn", dy_a, w_o_attn_t, 'nn', BF16)
    grads['w_o_attn'] = matmul("g_w_o_attn", dy_a, attn, 'tn', BF16)
    dybin = matmul("d_lru_out", dy_b, w_o_lru, 'nt', F32)
    grads['w_o_lru'] = matmul("g_w_o_lru", ybin, dy_b, 'tn', BF16)

    def b_lru_out(is_ctx, rows, params):
        hf, hb, yb, dyv = rows
        _, vjp = jax.vjp(lambda s, y: s * _gelu(y), hf + hb, yb)
        dh, dyb = vjp(dyv)
        return [dh, dyb], []

    (dh_lru, dyb), _ = rw_lat("lru_out_bwd", b_lru_out,
                              [(h_f, 0, LRU_W), (h_b, 0, LRU_W), (z, Z_YB, LRU_W), (dybin, 0, LRU_W)], [],
                              [(LRU_W, F32), (LRU_W, BF16)], [])
    du_f, da_f = scan_adj("scan_f_adj", a_f, dh_lru, hp_f, 'f', n, t_lat)
    du_b, da_b = scan_adj("scan_b_adj", a_b, dh_lru, hp_b, 'b', n, t_lat)
    dxc, (dw_a, dw_x, db_a, db_x, dsp) = gates_bwd(xc, da_f, du_f, da_b, du_b, lru_w_a, lru_w_x, b_a, b_x, sp,
                                                   n, t_lat, tm)
    grads['lru_w_a'] = dw_a.reshape(2, LRU_BLOCKS, LRU_BW, LRU_BW)
    grads['lru_w_x'] = dw_x.reshape(2, LRU_BLOCKS, LRU_BW, LRU_BW)
    grads['lru_b_a'], grads['lru_b_x'] = db_a, db_x
    grads['lru_lambda'] = -dsp * _sigmoid(-lam)
    dxb, grads['lru_conv_w'], grads['lru_conv_b'] = conv_bwd("lru_conv_bwd", dxc, z, Z_XB, LRU_W, wt['lru_conv_w'],
                                                             2, n, t_lat)

    dq, dk, dv = attn_bwd(qr, kr_, vr, attn, dattn, lse, t_lat, n, tq)

    def b_rope(is_ctx, rows, params):
        dqv, dkv, dvv, c, s1, s2 = rows
        live = jnp.where(is_ctx, 0.0, 1.0)
        dqo = jnp.concatenate([_rope_t(dqh, c, s1, s2) for dqh in _heads(dqv)], axis=1) * live
        dkh = _heads(dkv)
        dkr = dkh[0]
        for t in dkh[1:]:
            dkr = dkr + t
        lanes = lax.broadcasted_iota(jnp.int32, dkr.shape, 1)
        dkr = jnp.where((lanes >= QK_NOPE) & (lanes < QK_DIM), _rope_t(dkr, c, s1, s2), 0.0)
        return [dqo, jnp.concatenate([dkv, dvv], axis=1), dkr], []

    (dqp, dkvp, dkr), _ = rw("rope_bwd", b_rope,
                             [(dq, 0, hp), (dk, 0, hp), (dv, 0, hp), (c_tab, 0, HEAD_PAD), (s1_tab, 0, HEAD_PAD),
                              (s2_tab, 0, HEAD_PAD)], [], [(hp, BF16), (2 * hp, BF16), (HEAD_PAD, BF16)], [])
    dqn = matmul("d_qn", dqp, w_uq_t, 'nn', F32)
    grads['w_uq'] = matmul("g_w_uq", dqp, qn, 'tn', BF16)
    dkvn = matmul("d_kvn", dkvp, w_ukv_t, 'nn', F32)
    grads['w_ukv'] = matmul("g_w_ukv", dkvp, kvn, 'tn', BF16)

    def b_qkv_norm(is_ctx, rows, params):
        (ql, kvl, dqv, dkvv), (gq, gkv) = rows, params
        _, vjp_q = jax.vjp(_rms, ql, gq)
        _, vjp_kv = jax.vjp(_rms, kvl, gkv)
        dql, dgq = vjp_q(dqv)
        dkvl, dgkv = vjp_kv(dkvv)
        return [dql, dkvl], [dgq, dgkv]

    (dq_lat, dkv_lat), (grads['q_norm_g'], grads['kv_norm_g']) = rw(
        "qkv_norm_bwd", b_qkv_norm, [(z, Z_Q, Q_RANK), (z, Z_KV, KV_RANK), (dqn, 0, Q_RANK), (dkvn, 0, KV_RANK)],
        [q_g, kv_g], [(Q_RANK, BF16), (KV_RANK, BF16)], [(1, Q_RANK), (1, KV_RANK)])
    pad_ctx = lambda t: jnp.pad(t, ((0, n_ctx), (0, 0)))
    dz = jnp.concatenate([dq_lat, dkv_lat, dkr, dxb, pad_ctx(dyb), pad_ctx(dgl)], axis=1)
    dh = matmul("d_h", dz, w_in_t, 'nn', F32)
    grads['w_in'] = matmul("g_w_in", dz, h, 'tn', BF16)

    def b_norm1(is_ctx, rows, params):
        (xv, dhv, dxr), (g, sc, sh) = rows, params
        scv, shv = _sel(is_ctx, sc), _sel(is_ctx, sh)
        _, vjp = jax.vjp(_norm_mod, xv, g, scv, shv)
        dx, dg, dsc, dsh = vjp(dhv)
        return [dx + dxr], [dg, _seg_acc(is_ctx, dsc), _seg_acc(is_ctx, dsh)]

    (dxs,), (grads['norm1_g'], dsc1, dsh1) = rw("norm1_bwd", b_norm1, [(xs, 0, D), (dh, 0, D), (dx_res, 0, D)],
                                                [norm1_g, sc1, sh1], [(D, F32)], [(1, D), (2, D), (2, D)])
    grad_x = dxs[:t_lat]
    zero = jnp.zeros((D,), F32)
    dmod_l = jnp.concatenate([dsh1[0], dsc1[0], dg1[0], dsh2[0], dsc2[0], dg2[0]])
    dmod_c = jnp.concatenate([dsh1[1], dsc1[1], zero, zero, zero, zero])
    return loss, grad_x, grads, dmod_l, dmod_c


def kernel(x, c, ctx, c_ctx, w_mod, b_mod, norm1_g, w_in, b_gate, q_norm_g, kv_norm_g, w_uq, w_ukv, w_o_attn, lru_conv_w, lru_conv_b, lru_w_a, lru_b_a, lru_w_x, lru_b_x, lru_lambda, w_o_lru, w_out, norm2_g, w_up, ffn_conv_w, ffn_conv_b, w_down, final_g, loss_target, m_c_ctx, m_w_mod, m_b_mod, m_norm1_g, m_w_in, m_b_gate, m_q_norm_g, m_kv_norm_g, m_w_uq, m_w_ukv, m_w_o_attn, m_lru_conv_w, m_lru_conv_b, m_lru_w_a, m_lru_b_a, m_lru_w_x, m_lru_b_x, m_lru_lambda, m_w_o_lru, m_w_out, m_norm2_g, m_w_up, m_ffn_conv_w, m_ffn_conv_b, m_w_down, m_final_g, v_c_ctx, v_w_mod, v_b_mod, v_norm1_g, v_w_in, v_b_gate, v_q_norm_g, v_kv_norm_g, v_w_uq, v_w_ukv, v_w_o_attn, v_lru_conv_w, v_lru_conv_b, v_lru_w_a, v_lru_b_a, v_lru_w_x, v_lru_b_x, v_lru_lambda, v_w_o_lru, v_w_out, v_norm2_g, v_w_up, v_ffn_conv_w, v_ffn_conv_b, v_w_down, v_final_g):
    given = dict(locals())
    strip = lambda name, a: a if name in ('c_ctx', 'final_g') else a[0]
    wsh = {n: strip(n, given[n]) for n in WEIGHTS}
    msh = {n: strip(n, given['m_' + n]) for n in WEIGHTS}
    vsh = {n: strip(n, given['v_' + n]) for n in WEIGHTS}
    me = _my_index()

    small = _flat([c[0]] + [wsh[n] for n in SMALL_F32], F32, 8)
    small_all = all_gather("gather_small", small)
    shapes = [(D,)] + [wsh[n].shape for n in SMALL_F32]
    per_dev = [_unflat(small_all[p], shapes) for p in range(N_DEV)]
    c_all = jnp.stack([pd[0] for pd in per_dev])
    full = {}
    for i, n in enumerate(SMALL_F32):
        full[n] = _gathered_to_full(n, jnp.stack([pd[1 + i] for pd in per_dev]))

    cond = jnp.concatenate([c_all, c_ctx[None], jnp.zeros((7, D), F32)], axis=0)
    sil = cond * jax.nn.sigmoid(cond)
    mod_cols = matmul("mod_proj", sil, wsh['w_mod'], 'nn', F32)
    mod_all = all_gather("gather_mod", mod_cols)
    mod_all = jnp.transpose(mod_all, (1, 0, 2)).reshape(16, 6 * D) + b_mod[0][None]
    mod_l = lax.dynamic_index_in_dim(mod_all, me, axis=0, keepdims=False)
    mod_c = mod_all[N_DEV]

    gathered = all_gather_multi("gather_weights", [_shard_to_rb(n, wsh[n]).astype(BF16) for n in BIG_BF16])
    for n, g in zip(BIG_BF16, gathered):
        full[n] = _rb_from_gathered(n, g)
    for n in REPLICATED:
        if n not in ('c_ctx', 'b_mod'):
            full[n] = wsh[n]

    loss, grad_x, grads, dmod_l, dmod_c = local_step(x[0], ctx[0], loss_target[0], mod_l, mod_c, full)
    loss = lax.psum(loss, ("x", "y", "c"))

    dmod = _flat([dmod_l, dmod_c], F32, 8)
    dmod_all = all_gather("gather_dmod", dmod)
    dm = jnp.stack([jnp.stack(_unflat(dmod_all[p], [(6 * D,), (6 * D,)])) for p in range(N_DEV)])
    dmod_c_tot = dm[0, 1]
    for p in range(1, N_DEV):
        dmod_c_tot = dmod_c_tot + dm[p, 1]
    dm16 = jnp.concatenate([dm[:, 0], dmod_c_tot[None], jnp.zeros((7, 6 * D), F32)], axis=0)
    ncol = 6 * D // N_DEV
    dm16_cols = lax.dynamic_slice_in_dim(dm16.reshape(16, N_DEV, ncol), me, 1, axis=1)[:, 0]
    grad_w_mod = matmul("g_w_mod", sil, dm16_cols, 'tn', F32)
    dsil = matmul("d_cond", dm16_cols, wsh['w_mod'], 'nt', F32)
    sg = jax.nn.sigmoid(c_ctx)
    grads['c_ctx'] = dsil[N_DEV] * (sg * (1.0 + c_ctx * (1.0 - sg)))
    grads['b_mod'] = dmod_l + dmod_c

    rep_sizes = [math.prod(wsh[n].shape) for n in REPLICATED]
    big_chunks = [_chunks_from_rb_grad(n, grads[n]) for n in BIG_BF16]
    chunks = [_full_to_chunks(n, grads[n].reshape(full[n].shape)) for n in SMALL_F32]
    chunks += [grads[n].reshape(N_DEV, -1) for n in REPLICATED]
    chunk_len = sum(ch.shape[1] for ch in chunks)
    quantum = 8 * FLAT_C
    padded = -(-chunk_len // quantum) * quantum
    send = jnp.pad(jnp.concatenate(chunks, axis=1), ((0, 0), (0, padded - chunk_len)))
    got = all_to_all_multi("grad_exchange", big_chunks + [send.reshape(N_DEV, padded // FLAT_C, FLAT_C)])
    g_final = {'w_mod': grad_w_mod}
    for n, slots in zip(BIG_BF16, got[:-1]):
        g_final[n] = _rb_to_shard(n, sum_slots("sum_" + n, slots))
    gsum = sum_slots("sum_small", got[-1]).reshape(-1)
    at = 0
    for n in SMALL_F32:
        cnt = math.prod(wsh[n].shape)
        g_final[n] = gsum[at:at + cnt].reshape(wsh[n].shape)
        at += cnt
    rep_part = _flat([gsum[at:chunk_len]], F32, 8)
    rep_all = all_gather("gather_replicated_grads", rep_part).reshape(N_DEV, -1)
    at = 0
    for n, cnt in zip(REPLICATED, rep_sizes):
        g_final[n] = rep_all[:, at:at + cnt // N_DEV].reshape(wsh[n].shape)
        at += cnt // N_DEV

    stepped = {n: adamw("adamw_" + n, wsh[n], g_final[n], msh[n], vsh[n]) for n in ['w_mod'] + BIG_BF16}
    rest = [n for n in WEIGHTS if n not in stepped]
    flat = lambda d: _flat([d[n] for n in rest], F32, 8)
    rest_shapes = [wsh[n].shape for n in rest]
    rest_out = [_unflat(f, rest_shapes) for f in adamw("adamw_small", flat(wsh), flat(g_final), flat(msh), flat(vsh))]
    for i, n in enumerate(rest):
        stepped[n] = tuple(r[i] for r in rest_out)
    shaped = lambda n, a: a.reshape(given[n].shape)
    return (loss, grad_x[None],
            *[shaped(n, g_final[n]) for n in WEIGHTS],
            *[shaped(n, stepped[n][k]) for k in range(3) for n in WEIGHTS])
```

```python
import functools
import math

import jax
import jax.numpy as jnp
from jax import lax
from jax.experimental import pallas as pl
from jax.experimental.pallas import tpu as pltpu

F32 = jnp.float32
BF16 = jnp.bfloat16
MESH = pl.DeviceIdType.MESH

N_DEV = 8
D = 1024
N_HEADS = 8
HEAD_PAD = 128
QK_NOPE, QK_ROPE, V_HEAD = 64, 32, 64
QK_DIM = QK_NOPE + QK_ROPE
Q_RANK, KV_RANK = 384, 256
LRU_W, LRU_BLOCKS, LRU_BW = 1280, 10, 128
FFN = 2816
GRID_W = 64
ROPE_BASE = 10000.0
LRU_C = 8.0
EPS = 1e-6
Z_Q, Z_KV, Z_KR, Z_XB, Z_YB, Z_GL, Z_END = 0, 384, 640, 768, 2048, 3328, 5376
ADAM_LR, ADAM_B1, ADAM_B2, ADAM_EPS, ADAM_WD, ADAM_STEP = 0.001, 0.9, 0.999, 1e-08, 0.01, 10

VMEM_LIMIT = 52 * 1024 * 1024
FLAT_C = 512
BIG_ROWS = 256

WEIGHTS = ['c_ctx', 'w_mod', 'b_mod', 'norm1_g', 'w_in', 'b_gate', 'q_norm_g', 'kv_norm_g', 'w_uq', 'w_ukv',
           'w_o_attn', 'lru_conv_w', 'lru_conv_b', 'lru_w_a', 'lru_b_a', 'lru_w_x', 'lru_b_x', 'lru_lambda',
           'w_o_lru', 'w_out', 'norm2_g', 'w_up', 'ffn_conv_w', 'ffn_conv_b', 'w_down', 'final_g']
COL_SHARDED = ['w_in', 'w_uq', 'w_ukv', 'w_o_attn', 'lru_conv_w', 'lru_b_a', 'lru_b_x', 'lru_lambda', 'w_up',
               'ffn_conv_w']
ROW_SHARDED = ['w_o_lru', 'w_out', 'w_down']
BIG_BF16 = ['w_in', 'w_uq', 'w_ukv', 'w_o_attn', 'w_o_lru', 'w_out', 'w_up', 'w_down']
SMALL_F32 = ['lru_conv_w', 'lru_b_a', 'lru_b_x', 'lru_lambda', 'ffn_conv_w']
SHARDED = BIG_BF16 + SMALL_F32
REPLICATED = ['c_ctx', 'b_mod', 'norm1_g', 'b_gate', 'q_norm_g', 'kv_norm_g', 'lru_conv_b', 'lru_w_a', 'lru_w_x',
              'norm2_g', 'ffn_conv_b', 'final_g']


def _cparams(sem=None):
    return pltpu.CompilerParams(dimension_semantics=sem, vmem_limit_bytes=VMEM_LIMIT)


def _pick(n, cands):
    for c in cands:
        if c <= n and n % c == 0:
            return c
    return n


def _best_div(n, mult, cap):
    best = mult
    for d in range(mult, min(n, cap) + 1, mult):
        if n % d == 0:
            best = d
    return best


ROW_TILES = (1088, 1024, 544, 512, 256, 128, 64, 32, 16, 8)
LANE_TILES = (1408, 1024, 896, 768, 640, 512, 384, 256, 128)


def _my_pos():
    return lax.axis_index("x"), lax.axis_index("y"), lax.axis_index("c")


def _my_index():
    x, y, c = _my_pos()
    return 4 * x + 2 * y + c


def all_gather_multi(name, shards):
    n_arr = len(shards)
    arrays = range(n_arr)

    def body(*refs):
        x_refs, out_refs = refs[:n_arr], refs[n_arr:2 * n_arr]
        send_sems, recv_sems, local_sems = refs[2 * n_arr:]
        x, y, c = _my_pos()
        me, sibling = (x, y, c), (x, y, 1 - c)
        chips = [(1 - x, y), (x, 1 - y), (1 - x, 1 - y)]

        def slot(a, px, py, pc):
            return out_refs[a].at[4 * px + 2 * py + pc]

        def copy(a, k, block, to, src=None):
            return pltpu.make_async_remote_copy(
                src_ref=slot(a, *block) if src is None else src, dst_ref=slot(a, *block),
                send_sem=send_sems.at[7 * a + k], recv_sem=recv_sems.at[7 * a + k], device_id=to,
                device_id_type=MESH)

        mine = [pltpu.make_async_copy(x_refs[a], slot(a, *me), local_sems.at[a]) for a in arrays]
        first = [copy(a, 1 + j, me, (*chip, c), src=x_refs[a]) for j, chip in enumerate(chips) for a in arrays]
        first += [copy(a, 0, me, sibling, src=x_refs[a]) for a in arrays]
        for cp in first + mine:
            cp.start()
        passed = []
        for j, chip in enumerate(chips):
            for a in arrays:
                copy(a, 1 + j, (*chip, c), me).wait_recv()
                passed.append(copy(a, 4 + j, (*chip, c), sibling))
                passed[-1].start()
        for a in arrays:
            copy(a, 0, sibling, me).wait_recv()
            for j, chip in enumerate(chips):
                copy(a, 4 + j, (*chip, 1 - c), me).wait_recv()
        for cp in first + passed:
            cp.wait_send()
        for cp in mine:
            cp.wait()

    hbm = pl.BlockSpec(memory_space=pl.ANY)
    return pl.pallas_call(
        body, name=name,
        out_shape=[jax.ShapeDtypeStruct((N_DEV,) + s.shape, s.dtype) for s in shards],
        in_specs=[hbm] * n_arr, out_specs=[hbm] * n_arr,
        scratch_shapes=[pltpu.SemaphoreType.DMA((7 * n_arr,)), pltpu.SemaphoreType.DMA((7 * n_arr,)),
                        pltpu.SemaphoreType.DMA((n_arr,))],
    )(*shards)


def all_gather(name, shard):
    return all_gather_multi(name, [shard])[0]


def all_to_all_multi(name, chunk_arrays):
    n_arr = len(chunk_arrays)
    arrays = range(n_arr)

    def body(*refs):
        x_refs, out_refs = refs[:n_arr], refs[n_arr:2 * n_arr]
        send_sems, recv_sems, local_sems = refs[2 * n_arr:]
        x, y, c = _my_pos()
        me = 4 * x + 2 * y + c
        mine = [pltpu.make_async_copy(x_refs[a].at[me], out_refs[a].at[me], local_sems.at[a]) for a in arrays]
        sends, arrivals = [], []
        for rel in (6, 4, 2, 7, 5, 3, 1):
            dx, dy, dc = (rel >> 2) & 1, (rel >> 1) & 1, rel & 1
            px, py, pc = x ^ dx, y ^ dy, c ^ dc
            peer = 4 * px + 2 * py + pc
            for a in arrays:
                k = 7 * a + rel - 1
                sends.append(pltpu.make_async_remote_copy(
                    src_ref=x_refs[a].at[peer], dst_ref=out_refs[a].at[me],
                    send_sem=send_sems.at[k], recv_sem=recv_sems.at[k],
                    device_id=(px, py, pc), device_id_type=MESH))
                arrivals.append(pltpu.make_async_remote_copy(
                    src_ref=x_refs[a].at[peer], dst_ref=out_refs[a].at[peer],
                    send_sem=send_sems.at[k], recv_sem=recv_sems.at[k],
                    device_id=(x, y, c), device_id_type=MESH))
        for cp in sends + mine:
            cp.start()
        for cp in arrivals:
            cp.wait_recv()
        for cp in sends:
            cp.wait_send()
        for cp in mine:
            cp.wait()

    hbm = pl.BlockSpec(memory_space=pl.ANY)
    return pl.pallas_call(
        body, name=name,
        out_shape=[jax.ShapeDtypeStruct(s.shape, s.dtype) for s in chunk_arrays],
        in_specs=[hbm] * n_arr, out_specs=[hbm] * n_arr,
        scratch_shapes=[pltpu.SemaphoreType.DMA((7 * n_arr,)), pltpu.SemaphoreType.DMA((7 * n_arr,)),
                        pltpu.SemaphoreType.DMA((n_arr,))],
    )(*chunk_arrays)


def _peers():
    x, y, c = _my_pos()
    out = []
    for rel in (6, 4, 2, 7, 5, 3, 1):
        px, py, pc = x ^ ((rel >> 2) & 1), y ^ ((rel >> 1) & 1), c ^ (rel & 1)
        out.append((rel - 1, (px, py, pc), 4 * px + 2 * py + pc))
    return out


def _exchange_copies(mode, src_refs, land_refs, send_sems, recv_sems):
    x, y, c = _my_pos()
    me = 4 * x + 2 * y + c
    sends, arrivals = [], []
    for k, peer_pos, peer in _peers():
        for a, (src, land) in enumerate(zip(src_refs, land_refs)):
            piece = src.at[peer] if mode == 'scatter' else src
            sems = dict(send_sem=send_sems[a].at[k], recv_sem=recv_sems[a].at[k], device_id_type=MESH)
            sends.append(pltpu.make_async_remote_copy(src_ref=piece, dst_ref=land.at[me], device_id=peer_pos, **sems))
            arrivals.append(pltpu.make_async_remote_copy(src_ref=piece, dst_ref=land.at[peer], device_id=(x, y, c), **sems))
    return sends, arrivals


_HBM = pl.BlockSpec(memory_space=pltpu.HBM)
_SEM = pl.BlockSpec(memory_space=pltpu.SEMAPHORE)


def exchange_start(name, mode, arrays):
    n_arr = len(arrays)
    land_shapes = [a.shape if mode == 'scatter' else (N_DEV,) + a.shape for a in arrays]

    def body(*refs):
        src_refs, land_refs = refs[:n_arr], refs[n_arr:2 * n_arr]
        send_sems, recv_sems = refs[2 * n_arr:3 * n_arr], refs[3 * n_arr:4 * n_arr]
        sends, _ = _exchange_copies(mode, src_refs, land_refs, send_sems, recv_sems)
        for cp in sends:
            cp.start()

    sem = pltpu.SemaphoreType.DMA((N_DEV - 1,))
    res = pl.pallas_call(
        body, name=name,
        out_shape=[sem] * (2 * n_arr) + [pltpu.HBM(a.shape, a.dtype) for a in arrays]
        + [pltpu.HBM(s, a.dtype) for s, a in zip(land_shapes, arrays)],
        in_specs=[_HBM] * (2 * n_arr), out_specs=[_SEM] * (2 * n_arr) + [_HBM] * (2 * n_arr),
        input_output_aliases={i: 2 * n_arr + i for i in range(2 * n_arr)},
        compiler_params=pltpu.CompilerParams(has_side_effects=pltpu.SideEffectType.DATAFLOW_SIDE_EFFECTING),
    )(*[pltpu.with_memory_space_constraint(a, pltpu.HBM) for a in arrays],
      *[pltpu.with_memory_space_constraint(lax.empty(s, a.dtype), pltpu.HBM) for s, a in zip(land_shapes, arrays)])
    return res[:n_arr], res[n_arr:2 * n_arr], res[2 * n_arr:3 * n_arr], res[3 * n_arr:]


def exchange_wait(name, mode, started, after):
    send_sems, recv_sems, thru, land = started
    n_arr = len(thru)

    def body(*refs):
        src_refs, land_refs = refs[:n_arr], refs[n_arr:2 * n_arr]
        s_sems, r_sems = refs[2 * n_arr:3 * n_arr], refs[3 * n_arr:4 * n_arr]
        sends, arrivals = _exchange_copies(mode, src_refs, land_refs, s_sems, r_sems)
        for cp in sends:
            cp.wait_send()
        for cp in arrivals:
            cp.wait_recv()

    res = pl.pallas_call(
        body, name=name,
        out_shape=[pltpu.HBM(a.shape, a.dtype) for a in thru] + [pltpu.HBM(a.shape, a.dtype) for a in land],
        in_specs=[_HBM] * (2 * n_arr) + [_SEM] * (2 * n_arr) + [pl.BlockSpec(memory_space=pl.ANY)],
        out_specs=[_HBM] * (2 * n_arr),
        input_output_aliases={i: i for i in range(2 * n_arr)},
        compiler_params=pltpu.CompilerParams(has_side_effects=pltpu.SideEffectType.DATAFLOW_SIDE_EFFECTING),
    )(*thru, *land, *send_sems, *recv_sems, after)
    return res[n_arr:]


def sum_slots(name, slots):
    _, r, ccols = slots.shape
    tc = _pick(ccols, (256, 128))

    def body(s_ref, o_ref):
        acc = s_ref[0].astype(F32)
        for p in range(1, N_DEV):
            acc = acc + s_ref[p].astype(F32)
        o_ref[...] = acc

    return pl.pallas_call(
        body, name=name, grid=(ccols // tc,),
        out_shape=jax.ShapeDtypeStruct((r, ccols), F32),
        in_specs=[pl.BlockSpec((N_DEV, r, tc), lambda j: (0, 0, j))],
        out_specs=pl.BlockSpec((r, tc), lambda j: (0, j)),
        compiler_params=_cparams(("parallel",)),
    )(slots)


def matmul(name, a, b, mode, out_dtype, tm=None, tn=None, tk=None):
    if mode == 'nn':
        (m, k), (k2, n) = a.shape, b.shape
    elif mode == 'nt':
        (m, k), (n, k2) = a.shape, b.shape
    else:
        (k, m), (k2, n) = a.shape, b.shape
    assert k == k2, (name, a.shape, b.shape, mode)
    if mode == 'tn':
        tm = tm or _pick(m, LANE_TILES)
        tk = tk or _pick(k, ROW_TILES)
    else:
        tm = tm or _pick(m, ROW_TILES)
        tk = tk or _pick(k, LANE_TILES)
    tn = tn or _pick(n, LANE_TILES)
    nk = k // tk
    if mode == 'nn':
        a_spec = pl.BlockSpec((tm, tk), lambda i, j, kk: (i, kk))
        b_spec = pl.BlockSpec((tk, tn), lambda i, j, kk: (kk, j))
        dn = (((1,), (0,)), ((), ()))
    elif mode == 'nt':
        a_spec = pl.BlockSpec((tm, tk), lambda i, j, kk: (i, kk))
        b_spec = pl.BlockSpec((tn, tk), lambda i, j, kk: (j, kk))
        dn = (((1,), (1,)), ((), ()))
    else:
        a_spec = pl.BlockSpec((tk, tm), lambda i, j, kk: (kk, i))
        b_spec = pl.BlockSpec((tk, tn), lambda i, j, kk: (kk, j))
        dn = (((0,), (0,)), ((), ()))

    def product(a_ref, b_ref):
        return lax.dot_general(a_ref[...].astype(BF16), b_ref[...].astype(BF16), dn, preferred_element_type=F32)

    def body_one(a_ref, b_ref, o_ref):
        o_ref[...] = product(a_ref, b_ref).astype(o_ref.dtype)

    def body(a_ref, b_ref, o_ref, acc_ref):
        kk = pl.program_id(2)

        @pl.when(kk == 0)
        def _():
            acc_ref[...] = jnp.zeros_like(acc_ref)

        acc_ref[...] += product(a_ref, b_ref)

        @pl.when(kk == nk - 1)
        def _():
            o_ref[...] = acc_ref[...].astype(o_ref.dtype)

    return pl.pallas_call(
        body_one if nk == 1 else body, name=name, grid=(m // tm, n // tn, nk),
        out_shape=jax.ShapeDtypeStruct((m, n), out_dtype),
        in_specs=[a_spec, b_spec],
        out_specs=pl.BlockSpec((tm, tn), lambda i, j, kk: (i, j)),
        scratch_shapes=[] if nk == 1 else [pltpu.VMEM((tm, tn), F32)],
        compiler_params=_cparams(("parallel", "parallel", "arbitrary")),
    )(a, b)


def rowwise(name, fn, rows, params, out_rows, out_accs, n_rows, t_lat, tm):
    nb = n_rows // tm
    in_specs, piece_counts = [], []
    operands = []
    for arr, off, width in rows:
        g = math.gcd(off, width) if off else width
        assert g % 128 == 0 or (off == 0 and width == arr.shape[1]), (name, off, width)
        cnt = width // g
        last = arr.shape[0] // tm - 1
        clamp = arr.shape[0] < n_rows
        for p in range(cnt):
            cb = off // g + p
            if clamp:
                in_specs.append(pl.BlockSpec((tm, g), lambda i, cb=cb, last=last: (jnp.minimum(i, last), cb)))
            else:
                in_specs.append(pl.BlockSpec((tm, g), lambda i, cb=cb: (i, cb)))
            operands.append(arr)
        piece_counts.append(cnt)
    for p in params:
        in_specs.append(pl.BlockSpec(p.shape, lambda i, nd=p.ndim: (0,) * nd))
        operands.append(p)
    n_in = sum(piece_counts)
    n_par = len(params)
    n_or = len(out_rows)
    out_shape = [jax.ShapeDtypeStruct((n_rows, w), dt) for w, dt in out_rows]
    out_shape += [jax.ShapeDtypeStruct(s, F32) for s in out_accs]
    out_specs = [pl.BlockSpec((tm, w), lambda i: (i, 0)) for w, _ in out_rows]
    out_specs += [pl.BlockSpec(s, lambda i, nd=len(s): (0,) * nd) for s in out_accs]

    def body(*refs):
        in_refs, par_refs = refs[:n_in], refs[n_in:n_in + n_par]
        orow_refs = refs[n_in + n_par:n_in + n_par + n_or]
        oacc_refs = refs[n_in + n_par + n_or:]
        i = pl.program_id(0)
        tiles, at = [], 0
        for cnt in piece_counts:
            parts = [in_refs[at + p][...].astype(F32) for p in range(cnt)]
            tiles.append(parts[0] if cnt == 1 else jnp.concatenate(parts, axis=1))
            at += cnt
        is_ctx = i * tm >= t_lat
        outs, accs = fn(is_ctx, tiles, [p[...] for p in par_refs])
        for o_ref, o in zip(orow_refs, outs):
            o_ref[...] = o.astype(o_ref.dtype)
        if oacc_refs:
            @pl.when(i == 0)
            def _():
                for a_ref in oacc_refs:
                    a_ref[...] = jnp.zeros_like(a_ref)
            for a_ref, a in zip(oacc_refs, accs):
                a_ref[...] += a.astype(F32)

    res = pl.pallas_call(
        body, name=name, grid=(nb,),
        out_shape=out_shape, in_specs=in_specs, out_specs=out_specs,
        compiler_params=_cparams(("arbitrary",)),
    )(*operands)
    return res[:n_or], res[n_or:]


def _rms(x, g):
    return x * lax.rsqrt(jnp.mean(x * x, axis=-1, keepdims=True) + EPS) * g


def _norm_mod(x, g, sc, sh):
    return _rms(x, g) * (1.0 + sc) + sh


def _sigmoid(x):
    return 1.0 / (1.0 + jnp.exp(-x))


def _silu(x):
    return x * _sigmoid(x)


def _gelu(x):
    return 0.5 * x * (1.0 + jnp.tanh(math.sqrt(2.0 / math.pi) * (x + 0.044715 * (x * x * x))))


def _sel(is_ctx, p):
    return jnp.where(is_ctx, p[1:2], p[0:1])


def _seg_acc(is_ctx, v):
    rows = lax.broadcasted_iota(jnp.int32, (2, v.shape[1]), 0)
    return jnp.where(rows == is_ctx.astype(jnp.int32), jnp.broadcast_to(v, (2, v.shape[1])), 0.0)


def _rsum(v):
    return jnp.sum(v, axis=0, keepdims=True)


def _shift_rows(x, o, t_lat, n):
    if o == 0:
        return x
    y = pltpu.roll(x, (-o) % n, 0)
    t = lax.broadcasted_iota(jnp.int32, x.shape, 0)
    src = t + o
    ok = (src >= 0) & (src < n) & ((src >= t_lat) == (t >= t_lat))
    return jnp.where(ok, y, 0.0)


def conv_fwd(name, xarr, col_off, width, w, b, left, n_rows, t_lat, out_dtype, cb=128):
    taps = w.shape[0]
    assert col_off % cb == 0 and width % cb == 0

    def body(x_ref, w_ref, b_ref, o_ref):
        x = x_ref[...].astype(F32)
        acc = jnp.broadcast_to(b_ref[...], x.shape)
        for k in range(taps):
            acc = acc + _shift_rows(x, k - left, t_lat, n_rows) * w_ref[k:k + 1, :]
        o_ref[...] = acc.astype(o_ref.dtype)

    return pl.pallas_call(
        body, name=name, grid=(width // cb,),
        out_shape=jax.ShapeDtypeStruct((n_rows, width), out_dtype),
        in_specs=[pl.BlockSpec((n_rows, cb), lambda j: (0, col_off // cb + j)),
                  pl.BlockSpec((taps, cb), lambda j: (0, j)),
                  pl.BlockSpec((1, cb), lambda j: (0, j))],
        out_specs=pl.BlockSpec((n_rows, cb), lambda j: (0, j)),
        compiler_params=_cparams(("parallel",)),
    )(xarr, w, b)


def conv_bwd(name, dout, xarr, col_off, width, w, left, n_rows, t_lat, cb=128):
    taps = w.shape[0]

    def body(d_ref, x_ref, w_ref, dx_ref, dw_ref, db_ref):
        d = d_ref[...].astype(F32)
        x = x_ref[...].astype(F32)
        dx = jnp.zeros_like(d)
        dws = []
        for k in range(taps):
            dx = dx + _shift_rows(d, left - k, t_lat, n_rows) * w_ref[k:k + 1, :]
            dws.append(_rsum(d * _shift_rows(x, k - left, t_lat, n_rows)))
        dx_ref[...] = dx.astype(dx_ref.dtype)
        dw_ref[...] = jnp.concatenate(dws, axis=0)
        db_ref[...] = _rsum(d)

    return pl.pallas_call(
        body, name=name, grid=(width // cb,),
        out_shape=[jax.ShapeDtypeStruct((n_rows, width), BF16), jax.ShapeDtypeStruct((taps, width), F32),
                   jax.ShapeDtypeStruct((1, width), F32)],
        in_specs=[pl.BlockSpec((n_rows, cb), lambda j: (0, j)),
                  pl.BlockSpec((n_rows, cb), lambda j: (0, col_off // cb + j)),
                  pl.BlockSpec((taps, cb), lambda j: (0, j))],
        out_specs=[pl.BlockSpec((n_rows, cb), lambda j: (0, j)), pl.BlockSpec((taps, cb), lambda j: (0, j)),
                   pl.BlockSpec((1, cb), lambda j: (0, j))],
        compiler_params=_cparams(("parallel",)),
    )(dout, xarr, w)


def _chunk_order(direction, nb, nbl):
    if direction == 'f':
        return lambda s: ((s + nbl) % nb, 0)
    return lambda s: (nb - 1 - s, 0)


def _adjoint_order(direction, nb, nbl):
    if direction == 'f':
        return lambda s: ((nb - 1 - s + nbl) % nb, 0)
    return lambda s: (s, 0)


SUBLANES = 8


def _chunk_scan(a, b, carry, rev):
    tc = a.shape[0]
    row = lax.broadcasted_iota(jnp.int32, a.shape, 0)
    in_tile = jnp.bitwise_and(row, SUBLANES - 1)
    for k in (1, 2, 4):
        shift = tc - k if rev else k
        edge = in_tile >= SUBLANES - k if rev else in_tile < k
        b = jnp.where(edge, b, a * pltpu.roll(b, shift, 0) + b)
        a = jnp.where(edge, a, a * pltpu.roll(a, shift, 0))
    nt = tc // SUBLANES
    hs = [None] * nt
    c = carry
    for kt in range(nt):
        k = nt - 1 - kt if rev else kt
        h = b[k * SUBLANES:(k + 1) * SUBLANES] + a[k * SUBLANES:(k + 1) * SUBLANES] * c
        hs[k] = h
        c = h[0:1] if rev else h[SUBLANES - 1:SUBLANES]
    h = jnp.concatenate(hs, axis=0)
    if rev:
        return h, jnp.where(row == tc - 1, carry, pltpu.roll(h, tc - 1, 0)), c
    return h, jnp.where(row == 0, carry, pltpu.roll(h, 1, 0)), c


def scan_fwd(name, a, u, direction, n_rows, t_lat, tc=128):
    w = a.shape[1]
    nb, nbl = n_rows // tc, t_lat // tc
    order = _chunk_order(direction, nb, nbl)
    rev = direction == 'b'

    def body(a_ref, u_ref, h_ref, hp_ref, carry):
        @pl.when(pl.program_id(0) == 0)
        def _():
            carry[...] = jnp.zeros_like(carry)

        h_ref[...], hp_ref[...], carry[...] = _chunk_scan(a_ref[...], u_ref[...], carry[...], rev)

    spec = pl.BlockSpec((tc, w), order)
    return pl.pallas_call(
        body, name=name, grid=(nb,),
        out_shape=[jax.ShapeDtypeStruct((n_rows, w), F32)] * 2,
        in_specs=[spec, spec], out_specs=[spec, spec],
        scratch_shapes=[pltpu.VMEM((1, w), F32)],
        compiler_params=_cparams(("arbitrary",)),
    )(a, u)


def scan_adj(name, a, dh, hprev, direction, n_rows, t_lat, tc=128):
    w = a.shape[1]
    nb, nbl = n_rows // tc, t_lat // tc
    order = _adjoint_order(direction, nb, nbl)
    rev = direction == 'f'

    def dh_order(s):
        c, _ = order(s)
        return (jnp.minimum(c, nbl - 1), 0)

    def body(a_ref, dh_ref, hp_ref, du_ref, da_ref, carry):
        s = pl.program_id(0)

        @pl.when(s == 0)
        def _():
            carry[...] = jnp.zeros_like(carry)

        chunk, _ = order(s)
        live = (chunk < nbl).astype(F32)

        av = a_ref[...]
        dv = dh_ref[...] * live
        _, c_next, carry[...] = _chunk_scan(av, av * dv, carry[...], rev)
        lam = dv + c_next
        du_ref[...] = lam
        da_ref[...] = lam * hp_ref[...]

    spec = pl.BlockSpec((tc, w), order)
    return pl.pallas_call(
        body, name=name, grid=(nb,),
        out_shape=[jax.ShapeDtypeStruct((n_rows, w), F32)] * 2,
        in_specs=[spec, pl.BlockSpec((tc, w), dh_order), spec], out_specs=[spec, spec],
        scratch_shapes=[pltpu.VMEM((1, w), F32)],
        compiler_params=_cparams(("arbitrary",)),
    )(a, dh, hprev)


def _neg_expm1(y):
    series = -(y * (1.0 + y * (0.5 + y * (1.0 / 6.0 + y * (1.0 / 24.0)))))
    return jnp.where(y > -0.03, series, 1.0 - jnp.exp(y))


def _gate_elem(pre_r, pre_i, xc, b_a, b_x, sp):
    r = _sigmoid(pre_r + b_a)
    i = _sigmoid(pre_i + b_x)
    log_a = (-LRU_C) * r * sp
    a = jnp.exp(log_a)
    mult = jnp.sqrt(_neg_expm1(2.0 * log_a))
    return a, mult * (i * xc)


def _gate_elem_bwd(pre_r, pre_i, xc, b_a, b_x, sp, da, du):
    r = _sigmoid(pre_r + b_a)
    i = _sigmoid(pre_i + b_x)
    log_a = (-LRU_C) * r * sp
    a = jnp.exp(log_a)
    m2 = _neg_expm1(2.0 * log_a)
    inv_mult = lax.rsqrt(m2)
    g = du * (m2 * inv_mult)
    d_mult = du * (i * xc)
    d_log_a = (da - d_mult * a * inv_mult) * a
    d_pre_r = d_log_a * ((-LRU_C) * sp) * (r * (1.0 - r))
    d_pre_i = g * xc * (i * (1.0 - i))
    return d_pre_r, d_pre_i, g * i, _rsum(d_log_a * ((-LRU_C) * r))


def _blockdiag(xb16, w_ref_val, d):
    outs = []
    for n in range(LRU_BLOCKS):
        outs.append(jnp.dot(xb16[:, n * LRU_BW:(n + 1) * LRU_BW], w_ref_val[d * LRU_BLOCKS + n],
                            preferred_element_type=F32))
    return jnp.concatenate(outs, axis=1)


def gates_fwd(xc, w_a, w_x, b_a, b_x, sp, n_rows, t_lat, tm):
    def fn(is_ctx, rows, params):
        (x,), (wa, wx, ba, bx, spv) = rows, params
        xb16 = x.astype(BF16)
        outs = []
        for d in range(2):
            a, u = _gate_elem(_blockdiag(xb16, wa, d), _blockdiag(xb16, wx, d), x,
                              ba[d:d + 1], bx[d:d + 1], spv[d:d + 1])
            outs += [a, u]
        return outs, []

    (a_f, u_f, a_b, u_b), _ = rowwise("gates_fwd", fn, [(xc, 0, LRU_W)], [w_a, w_x, b_a, b_x, sp],
                                      [(LRU_W, F32)] * 4, [], n_rows, t_lat, tm)
    return a_f, u_f, a_b, u_b


def gates_bwd(xc, da_f, du_f, da_b, du_b, w_a, w_x, b_a, b_x, sp, n_rows, t_lat, tm):
    def fn(is_ctx, rows, params):
        (x, daf, duf, dab, dub), (wa, wx, ba, bx, spv) = rows, params
        xb16 = x.astype(BF16)
        dxc = jnp.zeros_like(x)
        dwa, dwx, dba, dbx, dsp = [], [], [], [], []
        for d, (da, du) in enumerate(((daf, duf), (dab, dub))):
            dpr, dpi, dx_e, dsp_d = _gate_elem_bwd(_blockdiag(xb16, wa, d), _blockdiag(xb16, wx, d), x,
                                                   ba[d:d + 1], bx[d:d + 1], spv[d:d + 1], da, du)
            dba_d, dbx_d = _rsum(dpr), _rsum(dpi)
            dxc = dxc + dx_e
            dpr16, dpi16 = dpr.astype(BF16), dpi.astype(BF16)
            back = []
            for n in range(LRU_BLOCKS):
                sl = slice(n * LRU_BW, (n + 1) * LRU_BW)
                nt_dims = (((1,), (1,)), ((), ()))
                back.append(lax.dot_general(dpr16[:, sl], wa[d * LRU_BLOCKS + n], nt_dims, preferred_element_type=F32)
                            + lax.dot_general(dpi16[:, sl], wx[d * LRU_BLOCKS + n], nt_dims,
                                              preferred_element_type=F32))
                tn_dims = (((0,), (0,)), ((), ()))
                dwa.append(lax.dot_general(xb16[:, sl], dpr16[:, sl], tn_dims, preferred_element_type=F32)[None])
                dwx.append(lax.dot_general(xb16[:, sl], dpi16[:, sl], tn_dims, preferred_element_type=F32)[None])
            dxc = dxc + jnp.concatenate(back, axis=1)
            dba.append(dba_d)
            dbx.append(dbx_d)
            dsp.append(dsp_d)
        cat0 = lambda xs: jnp.concatenate(xs, axis=0)
        return [dxc], [cat0(dwa), cat0(dwx), cat0(dba), cat0(dbx), cat0(dsp)]

    (dxc,), accs = rowwise("gates_bwd", fn,
                           [(xc, 0, LRU_W), (da_f, 0, LRU_W), (du_f, 0, LRU_W), (da_b, 0, LRU_W), (du_b, 0, LRU_W)],
                           [w_a, w_x, b_a, b_x, sp], [(LRU_W, F32)],
                           [(2 * LRU_BLOCKS, LRU_BW, LRU_BW)] * 2 + [(2, LRU_W)] * 3, n_rows, t_lat, tm)
    return dxc, accs


def _rope_tables(t_lat, n_rows):
    rows = t_lat // GRID_W
    row_ids = jnp.repeat(jnp.arange(rows), GRID_W).astype(F32)
    col_ids = jnp.tile(jnp.arange(GRID_W), rows).astype(F32)
    axis_dim = QK_ROPE // 2
    inv = 1.0 / (ROPE_BASE ** (jnp.arange(0, axis_dim, 2, dtype=F32) / axis_dim))
    ang = jnp.concatenate([row_ids[:, None] * inv, col_ids[:, None] * inv], axis=-1)
    cos, sin = jnp.cos(ang), jnp.sin(ang)
    half = QK_ROPE // 2
    ones, zeros = jnp.ones((t_lat, QK_NOPE), F32), jnp.zeros((t_lat, QK_NOPE), F32)
    pad1, pad0 = jnp.ones((t_lat, HEAD_PAD - QK_DIM), F32), jnp.zeros((t_lat, HEAD_PAD - QK_DIM), F32)
    zh = jnp.zeros((t_lat, half), F32)
    c_tab = jnp.concatenate([ones, cos, cos, pad1], axis=1)
    s1 = jnp.concatenate([zeros, -sin, zh, pad0], axis=1)
    s2 = jnp.concatenate([zeros, zh, sin, pad0], axis=1)
    n_ctx = n_rows - t_lat
    c_tab = jnp.concatenate([c_tab, jnp.ones((n_ctx, HEAD_PAD), F32)], axis=0)
    s1 = jnp.concatenate([s1, jnp.zeros((n_ctx, HEAD_PAD), F32)], axis=0)
    s2 = jnp.concatenate([s2, jnp.zeros((n_ctx, HEAD_PAD), F32)], axis=0)
    return c_tab, s1, s2


def _rope(x, c, s1, s2):
    half = QK_ROPE // 2
    return x * c + pltpu.roll(x, HEAD_PAD - half, 1) * s1 + pltpu.roll(x, half, 1) * s2


def _rope_t(dy, c, s1, s2):
    half = QK_ROPE // 2
    return dy * c + pltpu.roll(dy * s1, half, 1) + pltpu.roll(dy * s2, HEAD_PAD - half, 1)


def _heads(x):
    return [x[:, h * HEAD_PAD:(h + 1) * HEAD_PAD] for h in range(N_HEADS)]


Q_SCALE = QK_DIM ** -0.5 * math.log2(math.e)


def attn_fwd(q, k, v, t_lat, n_rows, tq):
    def body(q_ref, k_ref, v_ref, o_ref, lse_ref):
        s = lax.dot_general(q_ref[...], k_ref[...], (((1,), (1,)), ((), ())), preferred_element_type=F32)
        m = jnp.max(s, axis=-1, keepdims=True)
        p = jnp.exp2(s - m)
        l = jnp.sum(p, axis=-1, keepdims=True)
        o = jnp.dot(p.astype(BF16), v_ref[...], preferred_element_type=F32) / l
        o_ref[...] = o.astype(o_ref.dtype)
        lse_ref[...] = jnp.broadcast_to(m + jnp.log2(l), lse_ref.shape)

    qspec = pl.BlockSpec((tq, HEAD_PAD), lambda h, i: (i, h))
    kspec = pl.BlockSpec((n_rows, HEAD_PAD), lambda h, i: (0, h))
    return pl.pallas_call(
        body, name="attn_fwd", grid=(N_HEADS, t_lat // tq),
        out_shape=[jax.ShapeDtypeStruct((t_lat, N_HEADS * HEAD_PAD), BF16),
                   jax.ShapeDtypeStruct((t_lat, N_HEADS * HEAD_PAD), F32)],
        in_specs=[qspec, kspec, kspec], out_specs=[qspec, qspec],
        compiler_params=_cparams(("parallel", "arbitrary")),
    )(q, k, v)


def attn_bwd(q, k, v, o, do, lse, t_lat, n_rows, tq):
    scale = QK_DIM ** -0.5
    nq = t_lat // tq
    nt = (((1,), (1,)), ((), ()))
    tn = (((0,), (0,)), ((), ()))

    def body(q_ref, k_ref, v_ref, o_ref, do_ref, lse_ref, dq_ref, dk_ref, dv_ref):
        @pl.when(pl.program_id(1) == 0)
        def _():
            dk_ref[...] = jnp.zeros_like(dk_ref)
            dv_ref[...] = jnp.zeros_like(dv_ref)

        qv, kv, vv, dov = q_ref[...], k_ref[...], v_ref[...], do_ref[...]
        s = lax.dot_general(qv, kv, nt, preferred_element_type=F32)
        p = jnp.exp2(s - lse_ref[:, 0:1])
        dv_ref[...] += lax.dot_general(p.astype(BF16), dov, tn, preferred_element_type=F32)
        dp = lax.dot_general(dov, vv, nt, preferred_element_type=F32)
        delta = jnp.sum(dov.astype(F32) * o_ref[...].astype(F32), axis=-1, keepdims=True)
        ds = (p * (dp - delta)).astype(BF16)
        dq_ref[...] = jnp.dot(ds, kv, preferred_element_type=F32) * scale
        dk_ref[...] += lax.dot_general(ds, qv, tn, preferred_element_type=F32)

        @pl.when(pl.program_id(1) == nq - 1)
        def _():
            dk_ref[...] = dk_ref[...] * (scale / Q_SCALE)

    qspec = pl.BlockSpec((tq, HEAD_PAD), lambda h, i: (i, h))
    kspec = pl.BlockSpec((n_rows, HEAD_PAD), lambda h, i: (0, h))
    return pl.pallas_call(
        body, name="attn_bwd", grid=(N_HEADS, t_lat // tq),
        out_shape=[jax.ShapeDtypeStruct((t_lat, N_HEADS * HEAD_PAD), F32),
                   jax.ShapeDtypeStruct((n_rows, N_HEADS * HEAD_PAD), F32),
                   jax.ShapeDtypeStruct((n_rows, N_HEADS * HEAD_PAD), F32)],
        in_specs=[qspec, kspec, kspec, qspec, qspec, qspec], out_specs=[qspec, kspec, kspec],
        compiler_params=_cparams(("parallel", "arbitrary")),
    )(q, k, v, o, do, lse)


def adamw(name, w, g, m, v):
    r, ccols = w.shape
    tr = _best_div(r, 8, max(8, 262144 // ccols)) if r % 8 == 0 else r
    c1 = 1.0 - ADAM_B1 ** ADAM_STEP
    c2 = 1.0 - ADAM_B2 ** ADAM_STEP

    def body(w_ref, g_ref, m_ref, v_ref, d_ref, nm_ref, nv_ref):
        gv = g_ref[...]
        nm = ADAM_B1 * m_ref[...] + (1.0 - ADAM_B1) * gv
        nv = ADAM_B2 * v_ref[...] + (1.0 - ADAM_B2) * (gv * gv)
        d_ref[...] = -ADAM_LR * ((nm / c1) / (jnp.sqrt(nv / c2) + ADAM_EPS) + ADAM_WD * w_ref[...])
        nm_ref[...] = nm
        nv_ref[...] = nv

    spec = pl.BlockSpec((tr, ccols), lambda i: (i, 0))
    return pl.pallas_call(
        body, name=name, grid=(r // tr,),
        out_shape=[jax.ShapeDtypeStruct((r, ccols), F32)] * 3,
        in_specs=[spec] * 4, out_specs=[spec] * 3,
        compiler_params=_cparams(("parallel",)),
    )(w, g, m, v)


def _flat(parts, dtype, row_mult):
    v = jnp.concatenate([p.reshape(-1).astype(dtype) for p in parts])
    quantum = row_mult * FLAT_C
    total = -(-v.shape[0] // quantum) * quantum
    return jnp.pad(v, (0, total - v.shape[0])).reshape(total // FLAT_C, FLAT_C)


def _unflat(flat, shapes):
    v = flat.reshape(-1)
    out, at = [], 0
    for s in shapes:
        n = math.prod(s)
        out.append(v[at:at + n].reshape(s))
        at += n
    return out


def _gathered_to_full(name, g):
    k = g.shape[1]
    return jnp.transpose(g, (1, 0, 2)).reshape(k, N_DEV * g.shape[2])


def _full_to_chunks(name, full):
    k, n = full.shape
    return jnp.transpose(full.reshape(k, N_DEV, n // N_DEV), (1, 0, 2)).reshape(N_DEV, -1)


def _shard_to_rb(name, w):
    return w if name in ROW_SHARDED else w.T


def _rb_to_shard(name, g):
    return g if name in ROW_SHARDED else g.T


def _rb_from_gathered(name, g):
    cols = g.shape[2]
    if name == 'w_in':
        z = lambda k: jnp.zeros((k, cols), g.dtype)
        full = g.reshape(N_DEV * g.shape[1], cols)
        return jnp.concatenate([full[:Z_KR], z(QK_NOPE), full[Z_KR:Z_KR + QK_ROPE], z(HEAD_PAD - QK_DIM),
                                full[Z_KR + QK_ROPE:]], axis=0)
    if name == 'w_uq':
        return jnp.pad(g, ((0, 0), (0, HEAD_PAD - QK_DIM), (0, 0))).reshape(N_HEADS * HEAD_PAD, cols)
    if name == 'w_ukv':
        pad = lambda t: jnp.pad(t, ((0, 0), (0, HEAD_PAD - t.shape[1]), (0, 0))).reshape(N_HEADS * HEAD_PAD, cols)
        return jnp.concatenate([pad(g[:, :QK_NOPE]), pad(g[:, QK_NOPE:])], axis=0)
    if name == 'w_o_attn':
        full = g.reshape(D, N_HEADS, V_HEAD)
        return jnp.pad(full, ((0, 0), (0, 0), (0, HEAD_PAD - V_HEAD))).reshape(D, N_HEADS * HEAD_PAD)
    return g.reshape(N_DEV * g.shape[1], cols)


def _chunks_from_rb_grad(name, g):
    cols = g.shape[1]
    if name == 'w_in':
        full = jnp.concatenate([g[:Z_KR], g[Z_KR + QK_NOPE:Z_KR + QK_DIM], g[Z_XB:]], axis=0)
        return full.reshape(N_DEV, -1, cols)
    if name == 'w_uq':
        return g.reshape(N_HEADS, HEAD_PAD, cols)[:, :QK_DIM]
    if name == 'w_ukv':
        half = N_HEADS * HEAD_PAD
        gk = g[:half].reshape(N_HEADS, HEAD_PAD, cols)[:, :QK_NOPE]
        gv = g[half:].reshape(N_HEADS, HEAD_PAD, cols)[:, :V_HEAD]
        return jnp.concatenate([gk, gv], axis=1)
    if name == 'w_o_attn':
        full = g.reshape(D, N_HEADS, HEAD_PAD)[:, :, :V_HEAD].reshape(D, N_HEADS * V_HEAD)
        return full.reshape(N_DEV, D // N_DEV, N_HEADS * V_HEAD)
    return g.reshape(N_DEV, -1, cols)


def local_step(x, ctx, target, mod_l, mod_c, wt, on_grad=None):
    t_lat, n_ctx = x.shape[0], ctx.shape[0]
    n = t_lat + n_ctx
    tm = _pick(math.gcd(t_lat, n), (256, 128))
    tq = _pick(t_lat, (256, 128))
    row = lambda v: v.reshape(1, -1).astype(F32)
    two = lambda a, b: jnp.stack([a, b]).astype(F32)
    sh1_l, sc1_l, g1_l, sh2_l, sc2_l, g2_l = jnp.split(mod_l, 6)
    sh1_c, sc1_c = jnp.split(mod_c, 6)[:2]
    sc1, sh1 = two(sc1_l, sc1_c), two(sh1_l, sh1_c)
    g1, g2, sc2, sh2 = row(g1_l), row(g2_l), row(sc2_l), row(sh2_l)
    norm1_g, norm2_g, final_g = row(wt['norm1_g']), row(wt['norm2_g']), row(wt['final_g'])
    q_g, kv_g, b_gate = row(wt['q_norm_g']), row(wt['kv_norm_g']), row(wt['b_gate'])
    w_in_t, w_uq_t, w_ukv_t, w_o_attn_t, w_up_t = wt['w_in'], wt['w_uq'], wt['w_ukv'], wt['w_o_attn'], wt['w_up']
    w_o_lru, w_out, w_down = wt['w_o_lru'], wt['w_out'], wt['w_down']
    lru_w_a = wt['lru_w_a'].reshape(2 * LRU_BLOCKS, LRU_BW, LRU_BW).astype(BF16)
    lru_w_x = wt['lru_w_x'].reshape(2 * LRU_BLOCKS, LRU_BW, LRU_BW).astype(BF16)
    b_a, b_x, lam = wt['lru_b_a'], wt['lru_b_x'], wt['lru_lambda']
    sp = jnp.logaddexp(-lam, 0.0)
    c_tab, s1_tab, s2_tab = _rope_tables(t_lat, n)
    rw = functools.partial(rowwise, n_rows=n, t_lat=t_lat, tm=tm)
    rw_lat = functools.partial(rowwise, n_rows=t_lat, t_lat=t_lat, tm=tm)

    xs = jnp.concatenate([x, ctx], axis=0)

    def f_norm1(is_ctx, rows, params):
        (xv,), (g, sc, sh) = rows, params
        return [_norm_mod(xv, g, _sel(is_ctx, sc), _sel(is_ctx, sh))], []

    (h,), _ = rw("norm1", f_norm1, [(xs, 0, D)], [norm1_g, sc1, sh1], [(D, BF16)], [])
    z = matmul("w_in", h, w_in_t, 'nt', BF16)

    def f_qkv_norm(is_ctx, rows, params):
        (ql, kvl), (gq, gkv) = rows, params
        return [_rms(ql, gq), _rms(kvl, gkv)], []

    (qn, kvn), _ = rw("qkv_norm", f_qkv_norm, [(z, Z_Q, Q_RANK), (z, Z_KV, KV_RANK)], [q_g, kv_g],
                      [(Q_RANK, BF16), (KV_RANK, BF16)], [])
    qp = matmul("w_uq", qn, w_uq_t, 'nt', F32)
    kvp = matmul("w_ukv", kvn, w_ukv_t, 'nt', F32)

    def f_rope(is_ctx, rows, params):
        qv, kk, vv, kr, c, s1, s2 = rows
        krr = _rope(kr, c, s1, s2)
        qo = jnp.concatenate([_rope(qh, c, s1, s2) for qh in _heads(qv)], axis=1) * Q_SCALE
        ko = jnp.concatenate([kh + krr for kh in _heads(kk)], axis=1)
        return [qo, ko, vv], []

    hp = N_HEADS * HEAD_PAD
    (qr, kr_, vr), _ = rw("rope", f_rope,
                          [(qp, 0, hp), (kvp, 0, hp), (kvp, hp, hp), (z, Z_KR, HEAD_PAD), (c_tab, 0, HEAD_PAD),
                           (s1_tab, 0, HEAD_PAD), (s2_tab, 0, HEAD_PAD)], [], [(hp, BF16)] * 3, [])
    attn, lse = attn_fwd(qr, kr_, vr, t_lat, n, tq)

    xc = conv_fwd("lru_conv", z, Z_XB, LRU_W, wt['lru_conv_w'], row(wt['lru_conv_b']), 2, n, t_lat, F32)
    a_f, u_f, a_b, u_b = gates_fwd(xc, lru_w_a, lru_w_x, b_a, b_x, sp, n, t_lat, tm)
    h_f, hp_f = scan_fwd("scan_f", a_f, u_f, 'f', n, t_lat)
    h_b, hp_b = scan_fwd("scan_b", a_b, u_b, 'b', n, t_lat)

    def f_lru_out(is_ctx, rows, params):
        hf, hb, yb = rows
        return [(hf + hb) * _gelu(yb)], []

    (ybin,), _ = rw_lat("lru_out", f_lru_out, [(h_f, 0, LRU_W), (h_b, 0, LRU_W), (z, Z_YB, LRU_W)], [],
                        [(LRU_W, BF16)], [])
    y_a = matmul("w_o_attn", attn, w_o_attn_t, 'nt', F32)
    y_b = matmul("w_o_lru", ybin, w_o_lru, 'nn', F32)

    def _merge(ya, yb, gl, bg):
        gates = _sigmoid(gl + bg)
        return gates[:, :D] * ya + gates[:, D:] * yb

    def f_merge(is_ctx, rows, params):
        (ya, yb, gl), (bg,) = rows, params
        return [_merge(ya, yb, gl, bg)], []

    (mrg,), _ = rw_lat("merge", f_merge, [(y_a, 0, D), (y_b, 0, D), (z, Z_GL, 2 * D)], [b_gate], [(D, BF16)], [])
    o = matmul("w_out", mrg, w_out, 'nn', F32)

    def _res_norm2(xv, ov, g1v, g, sc, sh):
        x1 = xv + g1v * ov
        return x1, _norm_mod(x1, g, sc, sh)

    def f_norm2(is_ctx, rows, params):
        (xv, ov), (g1v, g, sc, sh) = rows, params
        x1, h2v = _res_norm2(xv, ov, g1v, g, sc, sh)
        return [x1, h2v], []

    (x1, h2), _ = rw_lat("norm2", f_norm2, [(x, 0, D), (o, 0, D)], [g1, norm2_g, sc2, sh2], [(D, F32), (D, BF16)], [])
    u = matmul("w_up", h2, w_up_t, 'nt', BF16)
    ac = conv_fwd("ffn_conv", u, 0, FFN, wt['ffn_conv_w'], row(wt['ffn_conv_b']), 1, t_lat, t_lat, BF16)

    def f_ffn_act(is_ctx, rows, params):
        acv, gv = rows
        return [_silu(acv) * gv], []

    (f,), _ = rw_lat("ffn_act", f_ffn_act, [(ac, 0, FFN), (u, FFN, FFN)], [], [(FFN, BF16)], [])
    dn = matmul("w_down", f, w_down, 'nn', F32)

    def _tile_loss(x1v, dv, g2v, fg, tgt):
        y = _rms(x1v + g2v * dv, fg)
        e = y - tgt
        return 0.5 * jnp.sum(jnp.mean(e * e, axis=-1, keepdims=True), axis=0, keepdims=True)

    def f_final(is_ctx, rows, params):
        (x1v, dv, tgt), (g2v, fg) = rows, params
        lv, vjp = jax.vjp(lambda a, b, c, d: _tile_loss(a, b, c, d, tgt), x1v, dv, g2v, fg)
        dx2, dd, dg2, dfg = vjp(jnp.ones((1, 1), F32))
        return [dx2, dd], [dg2, dfg, jnp.broadcast_to(lv, (1, 128))]

    (dx2, dd), (dg2, dfinal_g, loss_v) = rw_lat("final", f_final, [(x1, 0, D), (dn, 0, D), (target, 0, D)],
                                                [g2, final_g], [(D, F32), (D, BF16)], [(1, D), (1, D), (1, 128)])
    loss = loss_v[0, 0]

    grads = {'final_g': dfinal_g}

    def put(name, g):
        grads[name] = g
        if on_grad is not None:
            on_grad(name, g)
    df = matmul("d_f", dd, w_down, 'nt', BF16)
    put('w_down', matmul("g_w_down", f, dd, 'tn', BF16))

    def b_ffn_act(is_ctx, rows, params):
        acv, gv, dfv = rows
        _, vjp = jax.vjp(lambda a, g: _silu(a) * g, acv, gv)
        dac, dg = vjp(dfv)
        return [dac, dg], []

    (dac, dgate), _ = rw_lat("ffn_act_bwd", b_ffn_act, [(ac, 0, FFN), (u, FFN, FFN), (df, 0, FFN)], [],
                             [(FFN, BF16), (FFN, BF16)], [])
    da, grads['ffn_conv_w'], grads['ffn_conv_b'] = conv_bwd("ffn_conv_bwd", dac, u, 0, FFN, wt['ffn_conv_w'], 1,
                                                            t_lat, t_lat)
    du = jnp.concatenate([da, dgate], axis=1)
    dh2 = matmul("d_h2", du, w_up_t, 'nn', F32)
    put('w_up', matmul("g_w_up", du, h2, 'tn', BF16))

    def b_norm2(is_ctx, rows, params):
        (xv, ov, dh2v, dx2v), (g1v, g, sc, sh) = rows, params
        _, vjp = jax.vjp(_res_norm2, xv, ov, g1v, g, sc, sh)
        dx, do, dg1v, dg, dsc, dsh = vjp((dx2v, dh2v))
        return [dx, do], [dg1v, dg, dsc, dsh]

    (dx_res, do), (dg1, dnorm2_g, dsc2, dsh2) = rw_lat(
        "norm2_bwd", b_norm2, [(x, 0, D), (o, 0, D), (dh2, 0, D), (dx2, 0, D)], [g1, norm2_g, sc2, sh2],
        [(D, F32), (D, BF16)], [(1, D)] * 4)
    grads['norm2_g'] = dnorm2_g
    dmrg = matmul("d_merge", do, w_out, 'nt', F32)
    put('w_out', matmul("g_w_out", mrg, do, 'tn', BF16))

    def b_merge(is_ctx, rows, params):
        (ya, yb, gl, dm), (bg,) = rows, params
        _, vjp = jax.vjp(_merge, ya, yb, gl, bg)
        dya, dyb, dgl, dbg = vjp(dm)
        return [dya, dyb, dgl], [dbg]

    (dy_a, dy_b, dgl), (grads['b_gate'],) = rw_lat(
        "merge_bwd", b_merge, [(y_a, 0, D), (y_b, 0, D), (z, Z_GL, 2 * D), (dmrg, 0, D)], [b_gate],
        [(D, BF16), (D, BF16), (2 * D, BF16)], [(1, 2 * D)])
    dattn = matmul("d_attn", dy_a, w_o_attn_t, 'nn', BF16)
    put('w_o_attn', matmul("g_w_o_attn", dy_a, attn, 'tn', BF16))
    dybin = matmul("d_lru_out", dy_b, w_o_lru, 'nt', F32)
    put('w_o_lru', matmul("g_w_o_lru", ybin, dy_b, 'tn', BF16))

    def b_lru_out(is_ctx, rows, params):
        hf, hb, yb, dyv = rows
        _, vjp = jax.vjp(lambda s, y: s * _gelu(y), hf + hb, yb)
        dh, dyb = vjp(dyv)
        return [dh, dyb], []

    (dh_lru, dyb), _ = rw_lat("lru_out_bwd", b_lru_out,
                              [(h_f, 0, LRU_W), (h_b, 0, LRU_W), (z, Z_YB, LRU_W), (dybin, 0, LRU_W)], [],
                              [(LRU_W, F32), (LRU_W, BF16)], [])
    du_f, da_f = scan_adj("scan_f_adj", a_f, dh_lru, hp_f, 'f', n, t_lat)
    du_b, da_b = scan_adj("scan_b_adj", a_b, dh_lru, hp_b, 'b', n, t_lat)
    dxc, (dw_a, dw_x, db_a, db_x, dsp) = gates_bwd(xc, da_f, du_f, da_b, du_b, lru_w_a, lru_w_x, b_a, b_x, sp,
                                                   n, t_lat, tm)
    grads['lru_w_a'] = dw_a.reshape(2, LRU_BLOCKS, LRU_BW, LRU_BW)
    grads['lru_w_x'] = dw_x.reshape(2, LRU_BLOCKS, LRU_BW, LRU_BW)
    grads['lru_b_a'], grads['lru_b_x'] = db_a, db_x
    grads['lru_lambda'] = -dsp * _sigmoid(-lam)
    dxb, grads['lru_conv_w'], grads['lru_conv_b'] = conv_bwd("lru_conv_bwd", dxc, z, Z_XB, LRU_W, wt['lru_conv_w'],
                                                             2, n, t_lat)

    dq, dk, dv = attn_bwd(qr, kr_, vr, attn, dattn, lse, t_lat, n, tq)

    def b_rope(is_ctx, rows, params):
        dqv, dkv, dvv, c, s1, s2 = rows
        live = jnp.where(is_ctx, 0.0, 1.0)
        dqo = jnp.concatenate([_rope_t(dqh, c, s1, s2) for dqh in _heads(dqv)], axis=1) * live
        dkh = _heads(dkv)
        dkr = dkh[0]
        for t in dkh[1:]:
            dkr = dkr + t
        lanes = lax.broadcasted_iota(jnp.int32, dkr.shape, 1)
        dkr = jnp.where((lanes >= QK_NOPE) & (lanes < QK_DIM), _rope_t(dkr, c, s1, s2), 0.0)
        return [dqo, jnp.concatenate([dkv, dvv], axis=1), dkr], []

    (dqp, dkvp, dkr), _ = rw("rope_bwd", b_rope,
                             [(dq, 0, hp), (dk, 0, hp), (dv, 0, hp), (c_tab, 0, HEAD_PAD), (s1_tab, 0, HEAD_PAD),
                              (s2_tab, 0, HEAD_PAD)], [], [(hp, BF16), (2 * hp, BF16), (HEAD_PAD, BF16)], [])
    dqn = matmul("d_qn", dqp, w_uq_t, 'nn', F32)
    put('w_uq', matmul("g_w_uq", dqp, qn, 'tn', BF16))
    dkvn = matmul("d_kvn", dkvp, w_ukv_t, 'nn', F32)
    put('w_ukv', matmul("g_w_ukv", dkvp, kvn, 'tn', BF16))

    def b_qkv_norm(is_ctx, rows, params):
        (ql, kvl, dqv, dkvv), (gq, gkv) = rows, params
        _, vjp_q = jax.vjp(_rms, ql, gq)
        _, vjp_kv = jax.vjp(_rms, kvl, gkv)
        dql, dgq = vjp_q(dqv)
        dkvl, dgkv = vjp_kv(dkvv)
        return [dql, dkvl], [dgq, dgkv]

    (dq_lat, dkv_lat), (grads['q_norm_g'], grads['kv_norm_g']) = rw(
        "qkv_norm_bwd", b_qkv_norm, [(z, Z_Q, Q_RANK), (z, Z_KV, KV_RANK), (dqn, 0, Q_RANK), (dkvn, 0, KV_RANK)],
        [q_g, kv_g], [(Q_RANK, BF16), (KV_RANK, BF16)], [(1, Q_RANK), (1, KV_RANK)])
    pad_ctx = lambda t: jnp.pad(t, ((0, n_ctx), (0, 0)))
    dz = jnp.concatenate([dq_lat, dkv_lat, dkr, dxb, pad_ctx(dyb), pad_ctx(dgl)], axis=1)
    put('w_in', matmul("g_w_in", dz, h, 'tn', BF16))
    dh = matmul("d_h", dz, w_in_t, 'nn', F32)

    def b_norm1(is_ctx, rows, params):
        (xv, dhv, dxr), (g, sc, sh) = rows, params
        scv, shv = _sel(is_ctx, sc), _sel(is_ctx, sh)
        _, vjp = jax.vjp(_norm_mod, xv, g, scv, shv)
        dx, dg, dsc, dsh = vjp(dhv)
        return [dx + dxr], [dg, _seg_acc(is_ctx, dsc), _seg_acc(is_ctx, dsh)]

    (dxs,), (grads['norm1_g'], dsc1, dsh1) = rw("norm1_bwd", b_norm1, [(xs, 0, D), (dh, 0, D), (dx_res, 0, D)],
                                                [norm1_g, sc1, sh1], [(D, F32)], [(1, D), (2, D), (2, D)])
    grad_x = dxs[:t_lat]
    zero = jnp.zeros((D,), F32)
    dmod_l = jnp.concatenate([dsh1[0], dsc1[0], dg1[0], dsh2[0], dsc2[0], dg2[0]])
    dmod_c = jnp.concatenate([dsh1[1], dsc1[1], zero, zero, zero, zero])
    return loss, grad_x, grads, dmod_l, dmod_c


def kernel(x, c, ctx, c_ctx, w_mod, b_mod, norm1_g, w_in, b_gate, q_norm_g, kv_norm_g, w_uq, w_ukv, w_o_attn, lru_conv_w, lru_conv_b, lru_w_a, lru_b_a, lru_w_x, lru_b_x, lru_lambda, w_o_lru, w_out, norm2_g, w_up, ffn_conv_w, ffn_conv_b, w_down, final_g, loss_target, m_c_ctx, m_w_mod, m_b_mod, m_norm1_g, m_w_in, m_b_gate, m_q_norm_g, m_kv_norm_g, m_w_uq, m_w_ukv, m_w_o_attn, m_lru_conv_w, m_lru_conv_b, m_lru_w_a, m_lru_b_a, m_lru_w_x, m_lru_b_x, m_lru_lambda, m_w_o_lru, m_w_out, m_norm2_g, m_w_up, m_ffn_conv_w, m_ffn_conv_b, m_w_down, m_final_g, v_c_ctx, v_w_mod, v_b_mod, v_norm1_g, v_w_in, v_b_gate, v_q_norm_g, v_kv_norm_g, v_w_uq, v_w_ukv, v_w_o_attn, v_lru_conv_w, v_lru_conv_b, v_lru_w_a, v_lru_b_a, v_lru_w_x, v_lru_b_x, v_lru_lambda, v_w_o_lru, v_w_out, v_norm2_g, v_w_up, v_ffn_conv_w, v_ffn_conv_b, v_w_down, v_final_g):
    given = dict(locals())
    strip = lambda name, a: a if name in ('c_ctx', 'final_g') else a[0]
    wsh = {n: strip(n, given[n]) for n in WEIGHTS}
    msh = {n: strip(n, given['m_' + n]) for n in WEIGHTS}
    vsh = {n: strip(n, given['v_' + n]) for n in WEIGHTS}
    me = _my_index()

    small = _flat([c[0]] + [wsh[n] for n in SMALL_F32], F32, 8)
    small_all = all_gather("gather_small", small)
    shapes = [(D,)] + [wsh[n].shape for n in SMALL_F32]
    per_dev = [_unflat(small_all[p], shapes) for p in range(N_DEV)]
    c_all = jnp.stack([pd[0] for pd in per_dev])
    full = {}
    for i, n in enumerate(SMALL_F32):
        full[n] = _gathered_to_full(n, jnp.stack([pd[1 + i] for pd in per_dev]))

    cond = jnp.concatenate([c_all, c_ctx[None], jnp.zeros((7, D), F32)], axis=0)
    sil = cond * jax.nn.sigmoid(cond)
    mod_cols = matmul("mod_proj", sil, wsh['w_mod'], 'nn', F32)
    mod_all = all_gather("gather_mod", mod_cols)
    mod_all = jnp.transpose(mod_all, (1, 0, 2)).reshape(16, 6 * D) + b_mod[0][None]
    mod_l = lax.dynamic_index_in_dim(mod_all, me, axis=0, keepdims=False)
    mod_c = mod_all[N_DEV]

    gathered = all_gather_multi("gather_weights", [_shard_to_rb(n, wsh[n]).astype(BF16) for n in BIG_BF16])
    for n, g in zip(BIG_BF16, gathered):
        full[n] = _rb_from_gathered(n, g)
    for n in REPLICATED:
        if n not in ('c_ctx', 'b_mod'):
            full[n] = wsh[n]

    in_flight = {}

    def on_grad(n, g):
        chunks = _chunks_from_rb_grad(n, g)
        own = lax.dynamic_index_in_dim(chunks, me, axis=0, keepdims=True)
        in_flight[n] = (own, exchange_start("grad_send_" + n, 'scatter', [chunks]))

    loss, grad_x, grads, dmod_l, dmod_c = local_step(x[0], ctx[0], loss_target[0], mod_l, mod_c, full, on_grad)
    loss = lax.psum(loss, ("x", "y", "c"))

    dmod = _flat([dmod_l, dmod_c], F32, 8)
    dmod_all = all_gather("gather_dmod", dmod)
    dm = jnp.stack([jnp.stack(_unflat(dmod_all[p], [(6 * D,), (6 * D,)])) for p in range(N_DEV)])
    dmod_c_tot = dm[0, 1]
    for p in range(1, N_DEV):
        dmod_c_tot = dmod_c_tot + dm[p, 1]
    dm16 = jnp.concatenate([dm[:, 0], dmod_c_tot[None], jnp.zeros((7, 6 * D), F32)], axis=0)
    ncol = 6 * D // N_DEV
    dm16_cols = lax.dynamic_slice_in_dim(dm16.reshape(16, N_DEV, ncol), me, 1, axis=1)[:, 0]
    grad_w_mod = matmul("g_w_mod", sil, dm16_cols, 'tn', F32)
    dsil = matmul("d_cond", dm16_cols, wsh['w_mod'], 'nt', F32)
    sg = jax.nn.sigmoid(c_ctx)
    grads['c_ctx'] = dsil[N_DEV] * (sg * (1.0 + c_ctx * (1.0 - sg)))
    grads['b_mod'] = dmod_l + dmod_c

    rep_sizes = [math.prod(wsh[n].shape) for n in REPLICATED]
    chunks = [_full_to_chunks(n, grads[n].reshape(full[n].shape)) for n in SMALL_F32]
    chunks += [grads[n].reshape(N_DEV, -1) for n in REPLICATED]
    chunk_len = sum(ch.shape[1] for ch in chunks)
    quantum = 8 * FLAT_C
    padded = -(-chunk_len // quantum) * quantum
    send = jnp.pad(jnp.concatenate(chunks, axis=1), ((0, 0), (0, padded - chunk_len)))
    got = all_to_all_multi("grad_exchange", [send.reshape(N_DEV, padded // FLAT_C, FLAT_C)])
    g_final = {'w_mod': grad_w_mod}
    gsum = sum_slots("sum_small", got[0]).reshape(-1)
    for n in BIG_BF16:
        own, started = in_flight[n]
        (land,) = exchange_wait("grad_wait_" + n, 'scatter', started, gsum)
        slots = lax.dynamic_update_slice_in_dim(land, own, me, axis=0)
        g_final[n] = _rb_to_shard(n, sum_slots("sum_" + n, slots))
    at = 0
    for n in SMALL_F32:
        cnt = math.prod(wsh[n].shape)
        g_final[n] = gsum[at:at + cnt].reshape(wsh[n].shape)
        at += cnt
    rep_part = _flat([gsum[at:chunk_len]], F32, 8)
    rep_all = all_gather("gather_replicated_grads", rep_part).reshape(N_DEV, -1)
    at = 0
    for n, cnt in zip(REPLICATED, rep_sizes):
        g_final[n] = rep_all[:, at:at + cnt // N_DEV].reshape(wsh[n].shape)
        at += cnt // N_DEV

    stepped = {n: adamw("adamw_" + n, wsh[n], g_final[n], msh[n], vsh[n]) for n in ['w_mod'] + BIG_BF16}
    rest = [n for n in WEIGHTS if n not in stepped]
    flat = lambda d: _flat([d[n] for n in rest], F32, 8)
    rest_shapes = [wsh[n].shape for n in rest]
    rest_out = [_unflat(f, rest_shapes) for f in adamw("adamw_small", flat(wsh), flat(g_final), flat(msh), flat(vsh))]
    for i, n in enumerate(rest):
        stepped[n] = tuple(r[i] for r in rest_out)
    shaped = lambda n, a: a.reshape(given[n].shape)
    return (loss, grad_x[None],
            *[shaped(n, g_final[n]) for n in WEIGHTS],
            *[shaped(n, stepped[n][k]) for k in range(3) for n in WEIGHTS])
```

```python
import functools
import math

import jax
import jax.numpy as jnp
from jax import lax
from jax.experimental import pallas as pl
from jax.experimental.pallas import tpu as pltpu

F32 = jnp.float32
BF16 = jnp.bfloat16
MESH = pl.DeviceIdType.MESH

N_DEV = 8
D = 1024
N_HEADS = 8
HEAD_PAD = 128
QK_NOPE, QK_ROPE, V_HEAD = 64, 32, 64
QK_DIM = QK_NOPE + QK_ROPE
Q_RANK, KV_RANK = 384, 256
LRU_W, LRU_BLOCKS, LRU_BW = 1280, 10, 128
FFN = 2816
GRID_W = 64
ROPE_BASE = 10000.0
LRU_C = 8.0
EPS = 1e-6
Z_Q, Z_KV, Z_KR, Z_XB, Z_YB, Z_GL, Z_END = 0, 384, 640, 768, 2048, 3328, 5376
ADAM_LR, ADAM_B1, ADAM_B2, ADAM_EPS, ADAM_WD, ADAM_STEP = 0.001, 0.9, 0.999, 1e-08, 0.01, 10

VMEM_LIMIT = 52 * 1024 * 1024
FLAT_C = 512
BIG_ROWS = 256

WEIGHTS = ['c_ctx', 'w_mod', 'b_mod', 'norm1_g', 'w_in', 'b_gate', 'q_norm_g', 'kv_norm_g', 'w_uq', 'w_ukv',
           'w_o_attn', 'lru_conv_w', 'lru_conv_b', 'lru_w_a', 'lru_b_a', 'lru_w_x', 'lru_b_x', 'lru_lambda',
           'w_o_lru', 'w_out', 'norm2_g', 'w_up', 'ffn_conv_w', 'ffn_conv_b', 'w_down', 'final_g']
COL_SHARDED = ['w_in', 'w_uq', 'w_ukv', 'w_o_attn', 'lru_conv_w', 'lru_b_a', 'lru_b_x', 'lru_lambda', 'w_up',
               'ffn_conv_w']
ROW_SHARDED = ['w_o_lru', 'w_out', 'w_down']
BIG_BF16 = ['w_in', 'w_uq', 'w_ukv', 'w_o_attn', 'w_o_lru', 'w_out', 'w_up', 'w_down']
SMALL_F32 = ['lru_conv_w', 'lru_b_a', 'lru_b_x', 'lru_lambda', 'ffn_conv_w']
SHARDED = BIG_BF16 + SMALL_F32
REPLICATED = ['c_ctx', 'b_mod', 'norm1_g', 'b_gate', 'q_norm_g', 'kv_norm_g', 'lru_conv_b', 'lru_w_a', 'lru_w_x',
              'norm2_g', 'ffn_conv_b', 'final_g']


def _cparams(sem=None):
    return pltpu.CompilerParams(dimension_semantics=sem, vmem_limit_bytes=VMEM_LIMIT)


def _pick(n, cands):
    for c in cands:
        if c <= n and n % c == 0:
            return c
    return n


def _best_div(n, mult, cap):
    best = mult
    for d in range(mult, min(n, cap) + 1, mult):
        if n % d == 0:
            best = d
    return best


ROW_TILES = (1088, 1024, 544, 512, 256, 128, 64, 32, 16, 8)
LANE_TILES = (1408, 1024, 896, 768, 640, 512, 384, 256, 128)


def _my_pos():
    return lax.axis_index("x"), lax.axis_index("y"), lax.axis_index("c")


def _my_index():
    x, y, c = _my_pos()
    return 4 * x + 2 * y + c


def all_gather_multi(name, shards):
    n_arr = len(shards)
    arrays = range(n_arr)

    def body(*refs):
        x_refs, out_refs = refs[:n_arr], refs[n_arr:2 * n_arr]
        send_sems, recv_sems, local_sems = refs[2 * n_arr:]
        x, y, c = _my_pos()
        me, sibling = (x, y, c), (x, y, 1 - c)
        chips = [(1 - x, y), (x, 1 - y), (1 - x, 1 - y)]

        def slot(a, px, py, pc):
            return out_refs[a].at[4 * px + 2 * py + pc]

        def copy(a, k, block, to, src=None):
            return pltpu.make_async_remote_copy(
                src_ref=slot(a, *block) if src is None else src, dst_ref=slot(a, *block),
                send_sem=send_sems.at[7 * a + k], recv_sem=recv_sems.at[7 * a + k], device_id=to,
                device_id_type=MESH)

        mine = [pltpu.make_async_copy(x_refs[a], slot(a, *me), local_sems.at[a]) for a in arrays]
        first = [copy(a, 1 + j, me, (*chip, c), src=x_refs[a]) for j, chip in enumerate(chips) for a in arrays]
        first += [copy(a, 0, me, sibling, src=x_refs[a]) for a in arrays]
        for cp in first + mine:
            cp.start()
        passed = []
        for j, chip in enumerate(chips):
            for a in arrays:
                copy(a, 1 + j, (*chip, c), me).wait_recv()
                passed.append(copy(a, 4 + j, (*chip, c), sibling))
                passed[-1].start()
        for a in arrays:
            copy(a, 0, sibling, me).wait_recv()
            for j, chip in enumerate(chips):
                copy(a, 4 + j, (*chip, 1 - c), me).wait_recv()
        for cp in first + passed:
            cp.wait_send()
        for cp in mine:
            cp.wait()

    hbm = pl.BlockSpec(memory_space=pl.ANY)
    return pl.pallas_call(
        body, name=name,
        out_shape=[jax.ShapeDtypeStruct((N_DEV,) + s.shape, s.dtype) for s in shards],
        in_specs=[hbm] * n_arr, out_specs=[hbm] * n_arr,
        scratch_shapes=[pltpu.SemaphoreType.DMA((7 * n_arr,)), pltpu.SemaphoreType.DMA((7 * n_arr,)),
                        pltpu.SemaphoreType.DMA((n_arr,))],
    )(*shards)


def all_gather(name, shard):
    return all_gather_multi(name, [shard])[0]


def all_to_all_multi(name, chunk_arrays):
    n_arr = len(chunk_arrays)
    arrays = range(n_arr)

    def body(*refs):
        x_refs, out_refs = refs[:n_arr], refs[n_arr:2 * n_arr]
        send_sems, recv_sems, local_sems = refs[2 * n_arr:]
        x, y, c = _my_pos()
        me = 4 * x + 2 * y + c
        mine = [pltpu.make_async_copy(x_refs[a].at[me], out_refs[a].at[me], local_sems.at[a]) for a in arrays]
        sends, arrivals = [], []
        for rel in (6, 4, 2, 7, 5, 3, 1):
            dx, dy, dc = (rel >> 2) & 1, (rel >> 1) & 1, rel & 1
            px, py, pc = x ^ dx, y ^ dy, c ^ dc
            peer = 4 * px + 2 * py + pc
            for a in arrays:
                k = 7 * a + rel - 1
                sends.append(pltpu.make_async_remote_copy(
                    src_ref=x_refs[a].at[peer], dst_ref=out_refs[a].at[me],
                    send_sem=send_sems.at[k], recv_sem=recv_sems.at[k],
                    device_id=(px, py, pc), device_id_type=MESH))
                arrivals.append(pltpu.make_async_remote_copy(
                    src_ref=x_refs[a].at[peer], dst_ref=out_refs[a].at[peer],
                    send_sem=send_sems.at[k], recv_sem=recv_sems.at[k],
                    device_id=(x, y, c), device_id_type=MESH))
        for cp in sends + mine:
            cp.start()
        for cp in arrivals:
            cp.wait_recv()
        for cp in sends:
            cp.wait_send()
        for cp in mine:
            cp.wait()

    hbm = pl.BlockSpec(memory_space=pl.ANY)
    return pl.pallas_call(
        body, name=name,
        out_shape=[jax.ShapeDtypeStruct(s.shape, s.dtype) for s in chunk_arrays],
        in_specs=[hbm] * n_arr, out_specs=[hbm] * n_arr,
        scratch_shapes=[pltpu.SemaphoreType.DMA((7 * n_arr,)), pltpu.SemaphoreType.DMA((7 * n_arr,)),
                        pltpu.SemaphoreType.DMA((n_arr,))],
    )(*chunk_arrays)


def _peers():
    x, y, c = _my_pos()
    out = []
    for rel in (6, 4, 2, 7, 5, 3, 1):
        px, py, pc = x ^ ((rel >> 2) & 1), y ^ ((rel >> 1) & 1), c ^ (rel & 1)
        out.append((rel - 1, (px, py, pc), 4 * px + 2 * py + pc))
    return out


def _exchange_copies(mode, src_refs, land_refs, send_sems, recv_sems):
    x, y, c = _my_pos()
    me = 4 * x + 2 * y + c
    sends, arrivals = [], []
    for k, peer_pos, peer in _peers():
        for a, (src, land) in enumerate(zip(src_refs, land_refs)):
            piece = src.at[peer] if mode == 'scatter' else src
            sems = dict(send_sem=send_sems[a].at[k], recv_sem=recv_sems[a].at[k], device_id_type=MESH)
            sends.append(pltpu.make_async_remote_copy(src_ref=piece, dst_ref=land.at[me], device_id=peer_pos, **sems))
            arrivals.append(pltpu.make_async_remote_copy(src_ref=piece, dst_ref=land.at[peer], device_id=(x, y, c), **sems))
    return sends, arrivals


_HBM = pl.BlockSpec(memory_space=pltpu.HBM)
_SEM = pl.BlockSpec(memory_space=pltpu.SEMAPHORE)


def exchange_start(name, mode, arrays):
    n_arr = len(arrays)
    land_shapes = [a.shape if mode == 'scatter' else (N_DEV,) + a.shape for a in arrays]

    def body(*refs):
        src_refs, land_refs = refs[:n_arr], refs[n_arr:2 * n_arr]
        send_sems, recv_sems = refs[2 * n_arr:3 * n_arr], refs[3 * n_arr:4 * n_arr]
        sends, _ = _exchange_copies(mode, src_refs, land_refs, send_sems, recv_sems)
        for cp in sends:
            cp.start()
        token = refs[-1]
        token[...] = jnp.zeros_like(token)

    sem = pltpu.SemaphoreType.DMA((N_DEV - 1,))
    res = pl.pallas_call(
        body, name=name,
        out_shape=[sem] * (2 * n_arr) + [pltpu.HBM(a.shape, a.dtype) for a in arrays]
        + [pltpu.HBM(s, a.dtype) for s, a in zip(land_shapes, arrays)] + [jax.ShapeDtypeStruct((8, 128), F32)],
        in_specs=[_HBM] * (2 * n_arr),
        out_specs=[_SEM] * (2 * n_arr) + [_HBM] * (2 * n_arr) + [pl.BlockSpec(memory_space=pltpu.VMEM)],
        input_output_aliases={i: 2 * n_arr + i for i in range(2 * n_arr)},
        compiler_params=pltpu.CompilerParams(has_side_effects=pltpu.SideEffectType.DATAFLOW_SIDE_EFFECTING),
    )(*[pltpu.with_memory_space_constraint(a, pltpu.HBM) for a in arrays],
      *[pltpu.with_memory_space_constraint(lax.empty(s, a.dtype), pltpu.HBM) for s, a in zip(land_shapes, arrays)])
    return (res[:n_arr], res[n_arr:2 * n_arr], res[2 * n_arr:3 * n_arr], res[3 * n_arr:4 * n_arr]), res[-1]


def exchange_wait(name, mode, started, after):
    send_sems, recv_sems, thru, land = started
    n_arr = len(thru)

    def body(*refs):
        src_refs, land_refs = refs[:n_arr], refs[n_arr:2 * n_arr]
        s_sems, r_sems = refs[2 * n_arr:3 * n_arr], refs[3 * n_arr:4 * n_arr]
        sends, arrivals = _exchange_copies(mode, src_refs, land_refs, s_sems, r_sems)
        for cp in sends:
            cp.wait_send()
        for cp in arrivals:
            cp.wait_recv()

    res = pl.pallas_call(
        body, name=name,
        out_shape=[pltpu.HBM(a.shape, a.dtype) for a in thru] + [pltpu.HBM(a.shape, a.dtype) for a in land],
        in_specs=[_HBM] * (2 * n_arr) + [_SEM] * (2 * n_arr) + [pl.BlockSpec(memory_space=pl.ANY)],
        out_specs=[_HBM] * (2 * n_arr),
        input_output_aliases={i: i for i in range(2 * n_arr)},
        compiler_params=pltpu.CompilerParams(has_side_effects=pltpu.SideEffectType.DATAFLOW_SIDE_EFFECTING),
    )(*thru, *land, *send_sems, *recv_sems, after)
    return res[n_arr:]


def sum_slots(name, slots):
    _, r, ccols = slots.shape
    tc = _pick(ccols, (256, 128))

    def body(s_ref, o_ref):
        acc = s_ref[0].astype(F32)
        for p in range(1, N_DEV):
            acc = acc + s_ref[p].astype(F32)
        o_ref[...] = acc

    return pl.pallas_call(
        body, name=name, grid=(ccols // tc,),
        out_shape=jax.ShapeDtypeStruct((r, ccols), F32),
        in_specs=[pl.BlockSpec((N_DEV, r, tc), lambda j: (0, 0, j))],
        out_specs=pl.BlockSpec((r, tc), lambda j: (0, j)),
        compiler_params=_cparams(("parallel",)),
    )(slots)


def matmul(name, a, b, mode, out_dtype, tm=None, tn=None, tk=None, after=()):
    after = [t for t in after if t is not None]
    if mode == 'nn':
        (m, k), (k2, n) = a.shape, b.shape
    elif mode == 'nt':
        (m, k), (n, k2) = a.shape, b.shape
    else:
        (k, m), (k2, n) = a.shape, b.shape
    assert k == k2, (name, a.shape, b.shape, mode)
    if mode == 'tn':
        tm = tm or _pick(m, LANE_TILES)
        tk = tk or _pick(k, ROW_TILES)
    else:
        tm = tm or _pick(m, ROW_TILES)
        tk = tk or _pick(k, LANE_TILES)
    tn = tn or _pick(n, LANE_TILES)
    nk = k // tk
    if mode == 'nn':
        a_spec = pl.BlockSpec((tm, tk), lambda i, j, kk: (i, kk))
        b_spec = pl.BlockSpec((tk, tn), lambda i, j, kk: (kk, j))
        dn = (((1,), (0,)), ((), ()))
    elif mode == 'nt':
        a_spec = pl.BlockSpec((tm, tk), lambda i, j, kk: (i, kk))
        b_spec = pl.BlockSpec((tn, tk), lambda i, j, kk: (j, kk))
        dn = (((1,), (1,)), ((), ()))
    else:
        a_spec = pl.BlockSpec((tk, tm), lambda i, j, kk: (kk, i))
        b_spec = pl.BlockSpec((tk, tn), lambda i, j, kk: (kk, j))
        dn = (((0,), (0,)), ((), ()))

    def product(a_ref, b_ref):
        return lax.dot_general(a_ref[...].astype(BF16), b_ref[...].astype(BF16), dn, preferred_element_type=F32)

    n_after = len(after)

    def body_one(a_ref, b_ref, *rest):
        o_ref = rest[n_after]
        o_ref[...] = product(a_ref, b_ref).astype(o_ref.dtype)

    def body(a_ref, b_ref, *rest):
        o_ref, acc_ref = rest[n_after:]
        kk = pl.program_id(2)

        @pl.when(kk == 0)
        def _():
            acc_ref[...] = jnp.zeros_like(acc_ref)

        acc_ref[...] += product(a_ref, b_ref)

        @pl.when(kk == nk - 1)
        def _():
            o_ref[...] = acc_ref[...].astype(o_ref.dtype)

    return pl.pallas_call(
        body_one if nk == 1 else body, name=name, grid=(m // tm, n // tn, nk),
        out_shape=jax.ShapeDtypeStruct((m, n), out_dtype),
        in_specs=[a_spec, b_spec] + [pl.BlockSpec(memory_space=pl.ANY)] * n_after,
        out_specs=pl.BlockSpec((tm, tn), lambda i, j, kk: (i, j)),
        scratch_shapes=[] if nk == 1 else [pltpu.VMEM((tm, tn), F32)],
        compiler_params=_cparams(("parallel", "parallel", "arbitrary")),
    )(a, b, *after)


def rowwise(name, fn, rows, params, out_rows, out_accs, n_rows, t_lat, tm):
    nb = n_rows // tm
    in_specs, piece_counts = [], []
    operands = []
    for arr, off, width in rows:
        g = math.gcd(off, width) if off else width
        assert g % 128 == 0 or (off == 0 and width == arr.shape[1]), (name, off, width)
        cnt = width // g
        last = arr.shape[0] // tm - 1
        clamp = arr.shape[0] < n_rows
        for p in range(cnt):
            cb = off // g + p
            if clamp:
                in_specs.append(pl.BlockSpec((tm, g), lambda i, cb=cb, last=last: (jnp.minimum(i, last), cb)))
            else:
                in_specs.append(pl.BlockSpec((tm, g), lambda i, cb=cb: (i, cb)))
            operands.append(arr)
        piece_counts.append(cnt)
    for p in params:
        in_specs.append(pl.BlockSpec(p.shape, lambda i, nd=p.ndim: (0,) * nd))
        operands.append(p)
    n_in = sum(piece_counts)
    n_par = len(params)
    n_or = len(out_rows)
    out_shape = [jax.ShapeDtypeStruct((n_rows, w), dt) for w, dt in out_rows]
    out_shape += [jax.ShapeDtypeStruct(s, F32) for s in out_accs]
    out_specs = [pl.BlockSpec((tm, w), lambda i: (i, 0)) for w, _ in out_rows]
    out_specs += [pl.BlockSpec(s, lambda i, nd=len(s): (0,) * nd) for s in out_accs]

    def body(*refs):
        in_refs, par_refs = refs[:n_in], refs[n_in:n_in + n_par]
        orow_refs = refs[n_in + n_par:n_in + n_par + n_or]
        oacc_refs = refs[n_in + n_par + n_or:]
        i = pl.program_id(0)
        tiles, at = [], 0
        for cnt in piece_counts:
            parts = [in_refs[at + p][...].astype(F32) for p in range(cnt)]
            tiles.append(parts[0] if cnt == 1 else jnp.concatenate(parts, axis=1))
            at += cnt
        is_ctx = i * tm >= t_lat
        outs, accs = fn(is_ctx, tiles, [p[...] for p in par_refs])
        for o_ref, o in zip(orow_refs, outs):
            o_ref[...] = o.astype(o_ref.dtype)
        if oacc_refs:
            @pl.when(i == 0)
            def _():
                for a_ref in oacc_refs:
                    a_ref[...] = jnp.zeros_like(a_ref)
            for a_ref, a in zip(oacc_refs, accs):
                a_ref[...] += a.astype(F32)

    res = pl.pallas_call(
        body, name=name, grid=(nb,),
        out_shape=out_shape, in_specs=in_specs, out_specs=out_specs,
        compiler_params=_cparams(("arbitrary",)),
    )(*operands)
    return res[:n_or], res[n_or:]


def _rms(x, g):
    return x * lax.rsqrt(jnp.mean(x * x, axis=-1, keepdims=True) + EPS) * g


def _norm_mod(x, g, sc, sh):
    return _rms(x, g) * (1.0 + sc) + sh


def _sigmoid(x):
    return 1.0 / (1.0 + jnp.exp(-x))


def _silu(x):
    return x * _sigmoid(x)


def _gelu(x):
    return 0.5 * x * (1.0 + jnp.tanh(math.sqrt(2.0 / math.pi) * (x + 0.044715 * (x * x * x))))


def _sel(is_ctx, p):
    return jnp.where(is_ctx, p[1:2], p[0:1])


def _seg_acc(is_ctx, v):
    rows = lax.broadcasted_iota(jnp.int32, (2, v.shape[1]), 0)
    return jnp.where(rows == is_ctx.astype(jnp.int32), jnp.broadcast_to(v, (2, v.shape[1])), 0.0)


def _rsum(v):
    return jnp.sum(v, axis=0, keepdims=True)


def _shift_rows(x, o, t_lat, n):
    if o == 0:
        return x
    y = pltpu.roll(x, (-o) % n, 0)
    t = lax.broadcasted_iota(jnp.int32, x.shape, 0)
    src = t + o
    ok = (src >= 0) & (src < n) & ((src >= t_lat) == (t >= t_lat))
    return jnp.where(ok, y, 0.0)


def conv_fwd(name, xarr, col_off, width, w, b, left, n_rows, t_lat, out_dtype, cb=128):
    taps = w.shape[0]
    assert col_off % cb == 0 and width % cb == 0

    def body(x_ref, w_ref, b_ref, o_ref):
        x = x_ref[...].astype(F32)
        acc = jnp.broadcast_to(b_ref[...], x.shape)
        for k in range(taps):
            acc = acc + _shift_rows(x, k - left, t_lat, n_rows) * w_ref[k:k + 1, :]
        o_ref[...] = acc.astype(o_ref.dtype)

    return pl.pallas_call(
        body, name=name, grid=(width // cb,),
        out_shape=jax.ShapeDtypeStruct((n_rows, width), out_dtype),
        in_specs=[pl.BlockSpec((n_rows, cb), lambda j: (0, col_off // cb + j)),
                  pl.BlockSpec((taps, cb), lambda j: (0, j)),
                  pl.BlockSpec((1, cb), lambda j: (0, j))],
        out_specs=pl.BlockSpec((n_rows, cb), lambda j: (0, j)),
        compiler_params=_cparams(("parallel",)),
    )(xarr, w, b)


def conv_bwd(name, dout, xarr, col_off, width, w, left, n_rows, t_lat, cb=128):
    taps = w.shape[0]

    def body(d_ref, x_ref, w_ref, dx_ref, dw_ref, db_ref):
        d = d_ref[...].astype(F32)
        x = x_ref[...].astype(F32)
        dx = jnp.zeros_like(d)
        dws = []
        for k in range(taps):
            dx = dx + _shift_rows(d, left - k, t_lat, n_rows) * w_ref[k:k + 1, :]
            dws.append(_rsum(d * _shift_rows(x, k - left, t_lat, n_rows)))
        dx_ref[...] = dx.astype(dx_ref.dtype)
        dw_ref[...] = jnp.concatenate(dws, axis=0)
        db_ref[...] = _rsum(d)

    return pl.pallas_call(
        body, name=name, grid=(width // cb,),
        out_shape=[jax.ShapeDtypeStruct((n_rows, width), BF16), jax.ShapeDtypeStruct((taps, width), F32),
                   jax.ShapeDtypeStruct((1, width), F32)],
        in_specs=[pl.BlockSpec((n_rows, cb), lambda j: (0, j)),
                  pl.BlockSpec((n_rows, cb), lambda j: (0, col_off // cb + j)),
                  pl.BlockSpec((taps, cb), lambda j: (0, j))],
        out_specs=[pl.BlockSpec((n_rows, cb), lambda j: (0, j)), pl.BlockSpec((taps, cb), lambda j: (0, j)),
                   pl.BlockSpec((1, cb), lambda j: (0, j))],
        compiler_params=_cparams(("parallel",)),
    )(dout, xarr, w)


def _chunk_order(direction, nb, nbl):
    if direction == 'f':
        return lambda s: ((s + nbl) % nb, 0)
    return lambda s: (nb - 1 - s, 0)


def _adjoint_order(direction, nb, nbl):
    if direction == 'f':
        return lambda s: ((nb - 1 - s + nbl) % nb, 0)
    return lambda s: (s, 0)


SUBLANES = 8


def _chunk_scan(a, b, carry, rev):
    tc = a.shape[0]
    row = lax.broadcasted_iota(jnp.int32, a.shape, 0)
    in_tile = jnp.bitwise_and(row, SUBLANES - 1)
    for k in (1, 2, 4):
        shift = tc - k if rev else k
        edge = in_tile >= SUBLANES - k if rev else in_tile < k
        b = jnp.where(edge, b, a * pltpu.roll(b, shift, 0) + b)
        a = jnp.where(edge, a, a * pltpu.roll(a, shift, 0))
    nt = tc // SUBLANES
    hs = [None] * nt
    c = carry
    for kt in range(nt):
        k = nt - 1 - kt if rev else kt
        h = b[k * SUBLANES:(k + 1) * SUBLANES] + a[k * SUBLANES:(k + 1) * SUBLANES] * c
        hs[k] = h
        c = h[0:1] if rev else h[SUBLANES - 1:SUBLANES]
    h = jnp.concatenate(hs, axis=0)
    if rev:
        return h, jnp.where(row == tc - 1, carry, pltpu.roll(h, tc - 1, 0)), c
    return h, jnp.where(row == 0, carry, pltpu.roll(h, 1, 0)), c


def scan_fwd(name, a, u, direction, n_rows, t_lat, tc=128):
    w = a.shape[1]
    nb, nbl = n_rows // tc, t_lat // tc
    order = _chunk_order(direction, nb, nbl)
    rev = direction == 'b'

    def body(a_ref, u_ref, h_ref, hp_ref, carry):
        @pl.when(pl.program_id(0) == 0)
        def _():
            carry[...] = jnp.zeros_like(carry)

        h_ref[...], hp_ref[...], carry[...] = _chunk_scan(a_ref[...], u_ref[...], carry[...], rev)

    spec = pl.BlockSpec((tc, w), order)
    return pl.pallas_call(
        body, name=name, grid=(nb,),
        out_shape=[jax.ShapeDtypeStruct((n_rows, w), F32)] * 2,
        in_specs=[spec, spec], out_specs=[spec, spec],
        scratch_shapes=[pltpu.VMEM((1, w), F32)],
        compiler_params=_cparams(("arbitrary",)),
    )(a, u)


def scan_adj(name, a, dh, hprev, direction, n_rows, t_lat, tc=128):
    w = a.shape[1]
    nb, nbl = n_rows // tc, t_lat // tc
    order = _adjoint_order(direction, nb, nbl)
    rev = direction == 'f'

    def dh_order(s):
        c, _ = order(s)
        return (jnp.minimum(c, nbl - 1), 0)

    def body(a_ref, dh_ref, hp_ref, du_ref, da_ref, carry):
        s = pl.program_id(0)

        @pl.when(s == 0)
        def _():
            carry[...] = jnp.zeros_like(carry)

        chunk, _ = order(s)
        live = (chunk < nbl).astype(F32)

        av = a_ref[...]
        dv = dh_ref[...] * live
        _, c_next, carry[...] = _chunk_scan(av, av * dv, carry[...], rev)
        lam = dv + c_next
        du_ref[...] = lam
        da_ref[...] = lam * hp_ref[...]

    spec = pl.BlockSpec((tc, w), order)
    return pl.pallas_call(
        body, name=name, grid=(nb,),
        out_shape=[jax.ShapeDtypeStruct((n_rows, w), F32)] * 2,
        in_specs=[spec, pl.BlockSpec((tc, w), dh_order), spec], out_specs=[spec, spec],
        scratch_shapes=[pltpu.VMEM((1, w), F32)],
        compiler_params=_cparams(("arbitrary",)),
    )(a, dh, hprev)


def _neg_expm1(y):
    series = -(y * (1.0 + y * (0.5 + y * (1.0 / 6.0 + y * (1.0 / 24.0)))))
    return jnp.where(y > -0.03, series, 1.0 - jnp.exp(y))


def _gate_elem(pre_r, pre_i, xc, b_a, b_x, sp):
    r = _sigmoid(pre_r + b_a)
    i = _sigmoid(pre_i + b_x)
    log_a = (-LRU_C) * r * sp
    a = jnp.exp(log_a)
    mult = jnp.sqrt(_neg_expm1(2.0 * log_a))
    return a, mult * (i * xc)


def _gate_elem_bwd(pre_r, pre_i, xc, b_a, b_x, sp, da, du):
    r = _sigmoid(pre_r + b_a)
    i = _sigmoid(pre_i + b_x)
    log_a = (-LRU_C) * r * sp
    a = jnp.exp(log_a)
    m2 = _neg_expm1(2.0 * log_a)
    inv_mult = lax.rsqrt(m2)
    g = du * (m2 * inv_mult)
    d_mult = du * (i * xc)
    d_log_a = (da - d_mult * a * inv_mult) * a
    d_pre_r = d_log_a * ((-LRU_C) * sp) * (r * (1.0 - r))
    d_pre_i = g * xc * (i * (1.0 - i))
    return d_pre_r, d_pre_i, g * i, _rsum(d_log_a * ((-LRU_C) * r))


def _blockdiag(xb16, w_ref_val, d):
    outs = []
    for n in range(LRU_BLOCKS):
        outs.append(jnp.dot(xb16[:, n * LRU_BW:(n + 1) * LRU_BW], w_ref_val[d * LRU_BLOCKS + n],
                            preferred_element_type=F32))
    return jnp.concatenate(outs, axis=1)


def gates_fwd(xc, w_a, w_x, b_a, b_x, sp, n_rows, t_lat, tm):
    def fn(is_ctx, rows, params):
        (x,), (wa, wx, ba, bx, spv) = rows, params
        xb16 = x.astype(BF16)
        outs = []
        for d in range(2):
            a, u = _gate_elem(_blockdiag(xb16, wa, d), _blockdiag(xb16, wx, d), x,
                              ba[d:d + 1], bx[d:d + 1], spv[d:d + 1])
            outs += [a, u]
        return outs, []

    (a_f, u_f, a_b, u_b), _ = rowwise("gates_fwd", fn, [(xc, 0, LRU_W)], [w_a, w_x, b_a, b_x, sp],
                                      [(LRU_W, F32)] * 4, [], n_rows, t_lat, tm)
    return a_f, u_f, a_b, u_b


def gates_bwd(xc, da_f, du_f, da_b, du_b, w_a, w_x, b_a, b_x, sp, n_rows, t_lat, tm):
    def fn(is_ctx, rows, params):
        (x, daf, duf, dab, dub), (wa, wx, ba, bx, spv) = rows, params
        xb16 = x.astype(BF16)
        dxc = jnp.zeros_like(x)
        dwa, dwx, dba, dbx, dsp = [], [], [], [], []
        for d, (da, du) in enumerate(((daf, duf), (dab, dub))):
            dpr, dpi, dx_e, dsp_d = _gate_elem_bwd(_blockdiag(xb16, wa, d), _blockdiag(xb16, wx, d), x,
                                                   ba[d:d + 1], bx[d:d + 1], spv[d:d + 1], da, du)
            dba_d, dbx_d = _rsum(dpr), _rsum(dpi)
            dxc = dxc + dx_e
            dpr16, dpi16 = dpr.astype(BF16), dpi.astype(BF16)
            back = []
            for n in range(LRU_BLOCKS):
                sl = slice(n * LRU_BW, (n + 1) * LRU_BW)
                nt_dims = (((1,), (1,)), ((), ()))
                back.append(lax.dot_general(dpr16[:, sl], wa[d * LRU_BLOCKS + n], nt_dims, preferred_element_type=F32)
                            + lax.dot_general(dpi16[:, sl], wx[d * LRU_BLOCKS + n], nt_dims,
                                              preferred_element_type=F32))
                tn_dims = (((0,), (0,)), ((), ()))
                dwa.append(lax.dot_general(xb16[:, sl], dpr16[:, sl], tn_dims, preferred_element_type=F32)[None])
                dwx.append(lax.dot_general(xb16[:, sl], dpi16[:, sl], tn_dims, preferred_element_type=F32)[None])
            dxc = dxc + jnp.concatenate(back, axis=1)
            dba.append(dba_d)
            dbx.append(dbx_d)
            dsp.append(dsp_d)
        cat0 = lambda xs: jnp.concatenate(xs, axis=0)
        return [dxc], [cat0(dwa), cat0(dwx), cat0(dba), cat0(dbx), cat0(dsp)]

    (dxc,), accs = rowwise("gates_bwd", fn,
                           [(xc, 0, LRU_W), (da_f, 0, LRU_W), (du_f, 0, LRU_W), (da_b, 0, LRU_W), (du_b, 0, LRU_W)],
                           [w_a, w_x, b_a, b_x, sp], [(LRU_W, F32)],
                           [(2 * LRU_BLOCKS, LRU_BW, LRU_BW)] * 2 + [(2, LRU_W)] * 3, n_rows, t_lat, tm)
    return dxc, accs


def _rope_tables(t_lat, n_rows):
    rows = t_lat // GRID_W
    row_ids = jnp.repeat(jnp.arange(rows), GRID_W).astype(F32)
    col_ids = jnp.tile(jnp.arange(GRID_W), rows).astype(F32)
    axis_dim = QK_ROPE // 2
    inv = 1.0 / (ROPE_BASE ** (jnp.arange(0, axis_dim, 2, dtype=F32) / axis_dim))
    ang = jnp.concatenate([row_ids[:, None] * inv, col_ids[:, None] * inv], axis=-1)
    cos, sin = jnp.cos(ang), jnp.sin(ang)
    half = QK_ROPE // 2
    ones, zeros = jnp.ones((t_lat, QK_NOPE), F32), jnp.zeros((t_lat, QK_NOPE), F32)
    pad1, pad0 = jnp.ones((t_lat, HEAD_PAD - QK_DIM), F32), jnp.zeros((t_lat, HEAD_PAD - QK_DIM), F32)
    zh = jnp.zeros((t_lat, half), F32)
    c_tab = jnp.concatenate([ones, cos, cos, pad1], axis=1)
    s1 = jnp.concatenate([zeros, -sin, zh, pad0], axis=1)
    s2 = jnp.concatenate([zeros, zh, sin, pad0], axis=1)
    n_ctx = n_rows - t_lat
    c_tab = jnp.concatenate([c_tab, jnp.ones((n_ctx, HEAD_PAD), F32)], axis=0)
    s1 = jnp.concatenate([s1, jnp.zeros((n_ctx, HEAD_PAD), F32)], axis=0)
    s2 = jnp.concatenate([s2, jnp.zeros((n_ctx, HEAD_PAD), F32)], axis=0)
    return c_tab, s1, s2


def _rope(x, c, s1, s2):
    half = QK_ROPE // 2
    return x * c + pltpu.roll(x, HEAD_PAD - half, 1) * s1 + pltpu.roll(x, half, 1) * s2


def _rope_t(dy, c, s1, s2):
    half = QK_ROPE // 2
    return dy * c + pltpu.roll(dy * s1, half, 1) + pltpu.roll(dy * s2, HEAD_PAD - half, 1)


def _heads(x):
    return [x[:, h * HEAD_PAD:(h + 1) * HEAD_PAD] for h in range(N_HEADS)]


Q_SCALE = QK_DIM ** -0.5 * math.log2(math.e)


def attn_fwd(q, k, v, t_lat, n_rows, tq):
    def body(q_ref, k_ref, v_ref, o_ref, lse_ref):
        s = lax.dot_general(q_ref[...], k_ref[...], (((1,), (1,)), ((), ())), preferred_element_type=F32)
        m = jnp.max(s, axis=-1, keepdims=True)
        p = jnp.exp2(s - m)
        l = jnp.sum(p, axis=-1, keepdims=True)
        o = jnp.dot(p.astype(BF16), v_ref[...], preferred_element_type=F32) / l
        o_ref[...] = o.astype(o_ref.dtype)
        lse_ref[...] = jnp.broadcast_to(m + jnp.log2(l), lse_ref.shape)

    qspec = pl.BlockSpec((tq, HEAD_PAD), lambda h, i: (i, h))
    kspec = pl.BlockSpec((n_rows, HEAD_PAD), lambda h, i: (0, h))
    return pl.pallas_call(
        body, name="attn_fwd", grid=(N_HEADS, t_lat // tq),
        out_shape=[jax.ShapeDtypeStruct((t_lat, N_HEADS * HEAD_PAD), BF16),
                   jax.ShapeDtypeStruct((t_lat, N_HEADS * HEAD_PAD), F32)],
        in_specs=[qspec, kspec, kspec], out_specs=[qspec, qspec],
        compiler_params=_cparams(("parallel", "arbitrary")),
    )(q, k, v)


def attn_bwd(q, k, v, o, do, lse, t_lat, n_rows, tq):
    scale = QK_DIM ** -0.5
    nq = t_lat // tq
    nt = (((1,), (1,)), ((), ()))
    tn = (((0,), (0,)), ((), ()))

    def body(q_ref, k_ref, v_ref, o_ref, do_ref, lse_ref, dq_ref, dk_ref, dv_ref):
        @pl.when(pl.program_id(1) == 0)
        def _():
            dk_ref[...] = jnp.zeros_like(dk_ref)
            dv_ref[...] = jnp.zeros_like(dv_ref)

        qv, kv, vv, dov = q_ref[...], k_ref[...], v_ref[...], do_ref[...]
        s = lax.dot_general(qv, kv, nt, preferred_element_type=F32)
        p = jnp.exp2(s - lse_ref[:, 0:1])
        dv_ref[...] += lax.dot_general(p.astype(BF16), dov, tn, preferred_element_type=F32)
        dp = lax.dot_general(dov, vv, nt, preferred_element_type=F32)
        delta = jnp.sum(dov.astype(F32) * o_ref[...].astype(F32), axis=-1, keepdims=True)
        ds = (p * (dp - delta)).astype(BF16)
        dq_ref[...] = jnp.dot(ds, kv, preferred_element_type=F32) * scale
        dk_ref[...] += lax.dot_general(ds, qv, tn, preferred_element_type=F32)

        @pl.when(pl.program_id(1) == nq - 1)
        def _():
            dk_ref[...] = dk_ref[...] * (scale / Q_SCALE)

    qspec = pl.BlockSpec((tq, HEAD_PAD), lambda h, i: (i, h))
    kspec = pl.BlockSpec((n_rows, HEAD_PAD), lambda h, i: (0, h))
    return pl.pallas_call(
        body, name="attn_bwd", grid=(N_HEADS, t_lat // tq),
        out_shape=[jax.ShapeDtypeStruct((t_lat, N_HEADS * HEAD_PAD), F32),
                   jax.ShapeDtypeStruct((n_rows, N_HEADS * HEAD_PAD), F32),
                   jax.ShapeDtypeStruct((n_rows, N_HEADS * HEAD_PAD), F32)],
        in_specs=[qspec, kspec, kspec, qspec, qspec, qspec], out_specs=[qspec, kspec, kspec],
        compiler_params=_cparams(("parallel", "arbitrary")),
    )(q, k, v, o, do, lse)


def adamw(name, w, g, m, v):
    r, ccols = w.shape
    tr = _best_div(r, 8, max(8, 262144 // ccols)) if r % 8 == 0 else r
    c1 = 1.0 - ADAM_B1 ** ADAM_STEP
    c2 = 1.0 - ADAM_B2 ** ADAM_STEP

    def body(w_ref, g_ref, m_ref, v_ref, d_ref, nm_ref, nv_ref):
        gv = g_ref[...]
        nm = ADAM_B1 * m_ref[...] + (1.0 - ADAM_B1) * gv
        nv = ADAM_B2 * v_ref[...] + (1.0 - ADAM_B2) * (gv * gv)
        d_ref[...] = -ADAM_LR * ((nm / c1) / (jnp.sqrt(nv / c2) + ADAM_EPS) + ADAM_WD * w_ref[...])
        nm_ref[...] = nm
        nv_ref[...] = nv

    spec = pl.BlockSpec((tr, ccols), lambda i: (i, 0))
    return pl.pallas_call(
        body, name=name, grid=(r // tr,),
        out_shape=[jax.ShapeDtypeStruct((r, ccols), F32)] * 3,
        in_specs=[spec] * 4, out_specs=[spec] * 3,
        compiler_params=_cparams(("parallel",)),
    )(w, g, m, v)


def _flat(parts, dtype, row_mult):
    v = jnp.concatenate([p.reshape(-1).astype(dtype) for p in parts])
    quantum = row_mult * FLAT_C
    total = -(-v.shape[0] // quantum) * quantum
    return jnp.pad(v, (0, total - v.shape[0])).reshape(total // FLAT_C, FLAT_C)


def _unflat(flat, shapes):
    v = flat.reshape(-1)
    out, at = [], 0
    for s in shapes:
        n = math.prod(s)
        out.append(v[at:at + n].reshape(s))
        at += n
    return out


def _gathered_to_full(name, g):
    k = g.shape[1]
    return jnp.transpose(g, (1, 0, 2)).reshape(k, N_DEV * g.shape[2])


def _full_to_chunks(name, full):
    k, n = full.shape
    return jnp.transpose(full.reshape(k, N_DEV, n // N_DEV), (1, 0, 2)).reshape(N_DEV, -1)


def _shard_to_rb(name, w):
    return w if name in ROW_SHARDED else w.T


def _rb_to_shard(name, g):
    return g if name in ROW_SHARDED else g.T


def _rb_from_gathered(name, g):
    cols = g.shape[2]
    if name == 'w_in':
        z = lambda k: jnp.zeros((k, cols), g.dtype)
        full = g.reshape(N_DEV * g.shape[1], cols)
        return jnp.concatenate([full[:Z_KR], z(QK_NOPE), full[Z_KR:Z_KR + QK_ROPE], z(HEAD_PAD - QK_DIM),
                                full[Z_KR + QK_ROPE:]], axis=0)
    if name == 'w_uq':
        return jnp.pad(g, ((0, 0), (0, HEAD_PAD - QK_DIM), (0, 0))).reshape(N_HEADS * HEAD_PAD, cols)
    if name == 'w_ukv':
        pad = lambda t: jnp.pad(t, ((0, 0), (0, HEAD_PAD - t.shape[1]), (0, 0))).reshape(N_HEADS * HEAD_PAD, cols)
        return jnp.concatenate([pad(g[:, :QK_NOPE]), pad(g[:, QK_NOPE:])], axis=0)
    if name == 'w_o_attn':
        full = g.reshape(D, N_HEADS, V_HEAD)
        return jnp.pad(full, ((0, 0), (0, 0), (0, HEAD_PAD - V_HEAD))).reshape(D, N_HEADS * HEAD_PAD)
    return g.reshape(N_DEV * g.shape[1], cols)


def _chunks_from_rb_grad(name, g):
    cols = g.shape[1]
    if name == 'w_in':
        full = jnp.concatenate([g[:Z_KR], g[Z_KR + QK_NOPE:Z_KR + QK_DIM], g[Z_XB:]], axis=0)
        return full.reshape(N_DEV, -1, cols)
    if name == 'w_uq':
        return g.reshape(N_HEADS, HEAD_PAD, cols)[:, :QK_DIM]
    if name == 'w_ukv':
        half = N_HEADS * HEAD_PAD
        gk = g[:half].reshape(N_HEADS, HEAD_PAD, cols)[:, :QK_NOPE]
        gv = g[half:].reshape(N_HEADS, HEAD_PAD, cols)[:, :V_HEAD]
        return jnp.concatenate([gk, gv], axis=1)
    if name == 'w_o_attn':
        full = g.reshape(D, N_HEADS, HEAD_PAD)[:, :, :V_HEAD].reshape(D, N_HEADS * V_HEAD)
        return full.reshape(N_DEV, D // N_DEV, N_HEADS * V_HEAD)
    return g.reshape(N_DEV, -1, cols)


def local_step(x, ctx, target, mod_l, mod_c, wt, on_grad=None, arrive=None):
    t_lat, n_ctx = x.shape[0], ctx.shape[0]
    n = t_lat + n_ctx
    tm = _pick(math.gcd(t_lat, n), (256, 128))
    tq = _pick(t_lat, (256, 128))
    row = lambda v: v.reshape(1, -1).astype(F32)
    two = lambda a, b: jnp.stack([a, b]).astype(F32)
    sh1_l, sc1_l, g1_l, sh2_l, sc2_l, g2_l = jnp.split(mod_l, 6)
    sh1_c, sc1_c = jnp.split(mod_c, 6)[:2]
    sc1, sh1 = two(sc1_l, sc1_c), two(sh1_l, sh1_c)
    g1, g2, sc2, sh2 = row(g1_l), row(g2_l), row(sc2_l), row(sh2_l)
    norm1_g, norm2_g, final_g = row(wt['norm1_g']), row(wt['norm2_g']), row(wt['final_g'])
    q_g, kv_g, b_gate = row(wt['q_norm_g']), row(wt['kv_norm_g']), row(wt['b_gate'])
    wt = dict(wt)

    def need(names, after):
        if arrive is not None:
            wt.update(arrive(names, after))
        return [wt[n] for n in names]
    lru_w_a = wt['lru_w_a'].reshape(2 * LRU_BLOCKS, LRU_BW, LRU_BW).astype(BF16)
    lru_w_x = wt['lru_w_x'].reshape(2 * LRU_BLOCKS, LRU_BW, LRU_BW).astype(BF16)
    b_a, b_x, lam = wt['lru_b_a'], wt['lru_b_x'], wt['lru_lambda']
    sp = jnp.logaddexp(-lam, 0.0)
    c_tab, s1_tab, s2_tab = _rope_tables(t_lat, n)
    rw = functools.partial(rowwise, n_rows=n, t_lat=t_lat, tm=tm)
    rw_lat = functools.partial(rowwise, n_rows=t_lat, t_lat=t_lat, tm=tm)

    xs = jnp.concatenate([x, ctx], axis=0)

    def f_norm1(is_ctx, rows, params):
        (xv,), (g, sc, sh) = rows, params
        return [_norm_mod(xv, g, _sel(is_ctx, sc), _sel(is_ctx, sh))], []

    (h,), _ = rw("norm1", f_norm1, [(xs, 0, D)], [norm1_g, sc1, sh1], [(D, BF16)], [])
    (w_in_t,) = need(('w_in',), h)
    z = matmul("w_in", h, w_in_t, 'nt', BF16)
    w_uq_t, w_ukv_t, w_o_lru = need(('w_uq', 'w_ukv', 'w_o_lru'), z)

    def f_qkv_norm(is_ctx, rows, params):
        (ql, kvl), (gq, gkv) = rows, params
        return [_rms(ql, gq), _rms(kvl, gkv)], []

    (qn, kvn), _ = rw("qkv_norm", f_qkv_norm, [(z, Z_Q, Q_RANK), (z, Z_KV, KV_RANK)], [q_g, kv_g],
                      [(Q_RANK, BF16), (KV_RANK, BF16)], [])
    qp = matmul("w_uq", qn, w_uq_t, 'nt', F32)
    kvp = matmul("w_ukv", kvn, w_ukv_t, 'nt', F32)

    def f_rope(is_ctx, rows, params):
        qv, kk, vv, kr, c, s1, s2 = rows
        krr = _rope(kr, c, s1, s2)
        qo = jnp.concatenate([_rope(qh, c, s1, s2) for qh in _heads(qv)], axis=1) * Q_SCALE
        ko = jnp.concatenate([kh + krr for kh in _heads(kk)], axis=1)
        return [qo, ko, vv], []

    hp = N_HEADS * HEAD_PAD
    (qr, kr_, vr), _ = rw("rope", f_rope,
                          [(qp, 0, hp), (kvp, 0, hp), (kvp, hp, hp), (z, Z_KR, HEAD_PAD), (c_tab, 0, HEAD_PAD),
                           (s1_tab, 0, HEAD_PAD), (s2_tab, 0, HEAD_PAD)], [], [(hp, BF16)] * 3, [])
    attn, lse = attn_fwd(qr, kr_, vr, t_lat, n, tq)

    xc = conv_fwd("lru_conv", z, Z_XB, LRU_W, wt['lru_conv_w'], row(wt['lru_conv_b']), 2, n, t_lat, F32)
    a_f, u_f, a_b, u_b = gates_fwd(xc, lru_w_a, lru_w_x, b_a, b_x, sp, n, t_lat, tm)
    h_f, hp_f = scan_fwd("scan_f", a_f, u_f, 'f', n, t_lat)
    h_b, hp_b = scan_fwd("scan_b", a_b, u_b, 'b', n, t_lat)

    def f_lru_out(is_ctx, rows, params):
        hf, hb, yb = rows
        return [(hf + hb) * _gelu(yb)], []

    (ybin,), _ = rw_lat("lru_out", f_lru_out, [(h_f, 0, LRU_W), (h_b, 0, LRU_W), (z, Z_YB, LRU_W)], [],
                        [(LRU_W, BF16)], [])
    w_o_attn_t, w_out, w_up_t, w_down = need(('w_o_attn', 'w_out', 'w_up', 'w_down'), attn)
    y_a = matmul("w_o_attn", attn, w_o_attn_t, 'nt', F32)
    y_b = matmul("w_o_lru", ybin, w_o_lru, 'nn', F32)

    def _merge(ya, yb, gl, bg):
        gates = _sigmoid(gl + bg)
        return gates[:, :D] * ya + gates[:, D:] * yb

    def f_merge(is_ctx, rows, params):
        (ya, yb, gl), (bg,) = rows, params
        return [_merge(ya, yb, gl, bg)], []

    (mrg,), _ = rw_lat("merge", f_merge, [(y_a, 0, D), (y_b, 0, D), (z, Z_GL, 2 * D)], [b_gate], [(D, BF16)], [])
    o = matmul("w_out", mrg, w_out, 'nn', F32)

    def _res_norm2(xv, ov, g1v, g, sc, sh):
        x1 = xv + g1v * ov
        return x1, _norm_mod(x1, g, sc, sh)

    def f_norm2(is_ctx, rows, params):
        (xv, ov), (g1v, g, sc, sh) = rows, params
        x1, h2v = _res_norm2(xv, ov, g1v, g, sc, sh)
        return [x1, h2v], []

    (x1, h2), _ = rw_lat("norm2", f_norm2, [(x, 0, D), (o, 0, D)], [g1, norm2_g, sc2, sh2], [(D, F32), (D, BF16)], [])
    u = matmul("w_up", h2, w_up_t, 'nt', BF16)
    ac = conv_fwd("ffn_conv", u, 0, FFN, wt['ffn_conv_w'], row(wt['ffn_conv_b']), 1, t_lat, t_lat, BF16)

    def f_ffn_act(is_ctx, rows, params):
        acv, gv = rows
        return [_silu(acv) * gv], []

    (f,), _ = rw_lat("ffn_act", f_ffn_act, [(ac, 0, FFN), (u, FFN, FFN)], [], [(FFN, BF16)], [])
    dn = matmul("w_down", f, w_down, 'nn', F32)

    def _tile_loss(x1v, dv, g2v, fg, tgt):
        y = _rms(x1v + g2v * dv, fg)
        e = y - tgt
        return 0.5 * jnp.sum(jnp.mean(e * e, axis=-1, keepdims=True), axis=0, keepdims=True)

    def f_final(is_ctx, rows, params):
        (x1v, dv, tgt), (g2v, fg) = rows, params
        lv, vjp = jax.vjp(lambda a, b, c, d: _tile_loss(a, b, c, d, tgt), x1v, dv, g2v, fg)
        dx2, dd, dg2, dfg = vjp(jnp.ones((1, 1), F32))
        return [dx2, dd], [dg2, dfg, jnp.broadcast_to(lv, (1, 128))]

    (dx2, dd), (dg2, dfinal_g, loss_v) = rw_lat("final", f_final, [(x1, 0, D), (dn, 0, D), (target, 0, D)],
                                                [g2, final_g], [(D, F32), (D, BF16)], [(1, D), (1, D), (1, 128)])
    loss = loss_v[0, 0]

    grads = {'final_g': dfinal_g}

    pending = []

    def put(name, g):
        grads[name] = g
        if on_grad is not None:
            pending.append(on_grad(name, g))

    def sent():
        tokens = list(pending)
        pending.clear()
        return tokens
    df = matmul("d_f", dd, w_down, 'nt', BF16)
    put('w_down', matmul("g_w_down", f, dd, 'tn', BF16))

    def b_ffn_act(is_ctx, rows, params):
        acv, gv, dfv = rows
        _, vjp = jax.vjp(lambda a, g: _silu(a) * g, acv, gv)
        dac, dg = vjp(dfv)
        return [dac, dg], []

    (dac, dgate), _ = rw_lat("ffn_act_bwd", b_ffn_act, [(ac, 0, FFN), (u, FFN, FFN), (df, 0, FFN)], [],
                             [(FFN, BF16), (FFN, BF16)], [])
    da, grads['ffn_conv_w'], grads['ffn_conv_b'] = conv_bwd("ffn_conv_bwd", dac, u, 0, FFN, wt['ffn_conv_w'], 1,
                                                            t_lat, t_lat)
    du = jnp.concatenate([da, dgate], axis=1)
    dh2 = matmul("d_h2", du, w_up_t, 'nn', F32, after=sent())
    put('w_up', matmul("g_w_up", du, h2, 'tn', BF16))

    def b_norm2(is_ctx, rows, params):
        (xv, ov, dh2v, dx2v), (g1v, g, sc, sh) = rows, params
        _, vjp = jax.vjp(_res_norm2, xv, ov, g1v, g, sc, sh)
        dx, do, dg1v, dg, dsc, dsh = vjp((dx2v, dh2v))
        return [dx, do], [dg1v, dg, dsc, dsh]

    (dx_res, do), (dg1, dnorm2_g, dsc2, dsh2) = rw_lat(
        "norm2_bwd", b_norm2, [(x, 0, D), (o, 0, D), (dh2, 0, D), (dx2, 0, D)], [g1, norm2_g, sc2, sh2],
        [(D, F32), (D, BF16)], [(1, D)] * 4)
    grads['norm2_g'] = dnorm2_g
    dmrg = matmul("d_merge", do, w_out, 'nt', F32, after=sent())
    put('w_out', matmul("g_w_out", mrg, do, 'tn', BF16))

    def b_merge(is_ctx, rows, params):
        (ya, yb, gl, dm), (bg,) = rows, params
        _, vjp = jax.vjp(_merge, ya, yb, gl, bg)
        dya, dyb, dgl, dbg = vjp(dm)
        return [dya, dyb, dgl], [dbg]

    (dy_a, dy_b, dgl), (grads['b_gate'],) = rw_lat(
        "merge_bwd", b_merge, [(y_a, 0, D), (y_b, 0, D), (z, Z_GL, 2 * D), (dmrg, 0, D)], [b_gate],
        [(D, BF16), (D, BF16), (2 * D, BF16)], [(1, 2 * D)])
    dattn = matmul("d_attn", dy_a, w_o_attn_t, 'nn', BF16, after=sent())
    put('w_o_attn', matmul("g_w_o_attn", dy_a, attn, 'tn', BF16))
    dybin = matmul("d_lru_out", dy_b, w_o_lru, 'nt', F32, after=sent())
    put('w_o_lru', matmul("g_w_o_lru", ybin, dy_b, 'tn', BF16))

    def b_lru_out(is_ctx, rows, params):
        hf, hb, yb, dyv = rows
        _, vjp = jax.vjp(lambda s, y: s * _gelu(y), hf + hb, yb)
        dh, dyb = vjp(dyv)
        return [dh, dyb], []

    (dh_lru, dyb), _ = rw_lat("lru_out_bwd", b_lru_out,
                              [(h_f, 0, LRU_W), (h_b, 0, LRU_W), (z, Z_YB, LRU_W), (dybin, 0, LRU_W)], [],
                              [(LRU_W, F32), (LRU_W, BF16)], [])
    du_f, da_f = scan_adj("scan_f_adj", a_f, dh_lru, hp_f, 'f', n, t_lat)
    du_b, da_b = scan_adj("scan_b_adj", a_b, dh_lru, hp_b, 'b', n, t_lat)
    dxc, (dw_a, dw_x, db_a, db_x, dsp) = gates_bwd(xc, da_f, du_f, da_b, du_b, lru_w_a, lru_w_x, b_a, b_x, sp,
                                                   n, t_lat, tm)
    grads['lru_w_a'] = dw_a.reshape(2, LRU_BLOCKS, LRU_BW, LRU_BW)
    grads['lru_w_x'] = dw_x.reshape(2, LRU_BLOCKS, LRU_BW, LRU_BW)
    grads['lru_b_a'], grads['lru_b_x'] = db_a, db_x
    grads['lru_lambda'] = -dsp * _sigmoid(-lam)
    dxb, grads['lru_conv_w'], grads['lru_conv_b'] = conv_bwd("lru_conv_bwd", dxc, z, Z_XB, LRU_W, wt['lru_conv_w'],
                                                             2, n, t_lat)

    dq, dk, dv = attn_bwd(qr, kr_, vr, attn, dattn, lse, t_lat, n, tq)

    def b_rope(is_ctx, rows, params):
        dqv, dkv, dvv, c, s1, s2 = rows
        live = jnp.where(is_ctx, 0.0, 1.0)
        dqo = jnp.concatenate([_rope_t(dqh, c, s1, s2) for dqh in _heads(dqv)], axis=1) * live
        dkh = _heads(dkv)
        dkr = dkh[0]
        for t in dkh[1:]:
            dkr = dkr + t
        lanes = lax.broadcasted_iota(jnp.int32, dkr.shape, 1)
        dkr = jnp.where((lanes >= QK_NOPE) & (lanes < QK_DIM), _rope_t(dkr, c, s1, s2), 0.0)
        return [dqo, jnp.concatenate([dkv, dvv], axis=1), dkr], []

    (dqp, dkvp, dkr), _ = rw("rope_bwd", b_rope,
                             [(dq, 0, hp), (dk, 0, hp), (dv, 0, hp), (c_tab, 0, HEAD_PAD), (s1_tab, 0, HEAD_PAD),
                              (s2_tab, 0, HEAD_PAD)], [], [(hp, BF16), (2 * hp, BF16), (HEAD_PAD, BF16)], [])
    dqn = matmul("d_qn", dqp, w_uq_t, 'nn', F32, after=sent())
    put('w_uq', matmul("g_w_uq", dqp, qn, 'tn', BF16))
    dkvn = matmul("d_kvn", dkvp, w_ukv_t, 'nn', F32, after=sent())
    put('w_ukv', matmul("g_w_ukv", dkvp, kvn, 'tn', BF16))

    def b_qkv_norm(is_ctx, rows, params):
        (ql, kvl, dqv, dkvv), (gq, gkv) = rows, params
        _, vjp_q = jax.vjp(_rms, ql, gq)
        _, vjp_kv = jax.vjp(_rms, kvl, gkv)
        dql, dgq = vjp_q(dqv)
        dkvl, dgkv = vjp_kv(dkvv)
        return [dql, dkvl], [dgq, dgkv]

    (dq_lat, dkv_lat), (grads['q_norm_g'], grads['kv_norm_g']) = rw(
        "qkv_norm_bwd", b_qkv_norm, [(z, Z_Q, Q_RANK), (z, Z_KV, KV_RANK), (dqn, 0, Q_RANK), (dkvn, 0, KV_RANK)],
        [q_g, kv_g], [(Q_RANK, BF16), (KV_RANK, BF16)], [(1, Q_RANK), (1, KV_RANK)])
    pad_ctx = lambda t: jnp.pad(t, ((0, n_ctx), (0, 0)))
    dz = jnp.concatenate([dq_lat, dkv_lat, dkr, dxb, pad_ctx(dyb), pad_ctx(dgl)], axis=1)
    put('w_in', matmul("g_w_in", dz, h, 'tn', BF16))
    dh = matmul("d_h", dz, w_in_t, 'nn', F32, after=sent())

    def b_norm1(is_ctx, rows, params):
        (xv, dhv, dxr), (g, sc, sh) = rows, params
        scv, shv = _sel(is_ctx, sc), _sel(is_ctx, sh)
        _, vjp = jax.vjp(_norm_mod, xv, g, scv, shv)
        dx, dg, dsc, dsh = vjp(dhv)
        return [dx + dxr], [dg, _seg_acc(is_ctx, dsc), _seg_acc(is_ctx, dsh)]

    (dxs,), (grads['norm1_g'], dsc1, dsh1) = rw("norm1_bwd", b_norm1, [(xs, 0, D), (dh, 0, D), (dx_res, 0, D)],
                                                [norm1_g, sc1, sh1], [(D, F32)], [(1, D), (2, D), (2, D)])
    grad_x = dxs[:t_lat]
    zero = jnp.zeros((D,), F32)
    dmod_l = jnp.concatenate([dsh1[0], dsc1[0], dg1[0], dsh2[0], dsc2[0], dg2[0]])
    dmod_c = jnp.concatenate([dsh1[1], dsc1[1], zero, zero, zero, zero])
    return loss, grad_x, grads, dmod_l, dmod_c


def kernel(x, c, ctx, c_ctx, w_mod, b_mod, norm1_g, w_in, b_gate, q_norm_g, kv_norm_g, w_uq, w_ukv, w_o_attn, lru_conv_w, lru_conv_b, lru_w_a, lru_b_a, lru_w_x, lru_b_x, lru_lambda, w_o_lru, w_out, norm2_g, w_up, ffn_conv_w, ffn_conv_b, w_down, final_g, loss_target, m_c_ctx, m_w_mod, m_b_mod, m_norm1_g, m_w_in, m_b_gate, m_q_norm_g, m_kv_norm_g, m_w_uq, m_w_ukv, m_w_o_attn, m_lru_conv_w, m_lru_conv_b, m_lru_w_a, m_lru_b_a, m_lru_w_x, m_lru_b_x, m_lru_lambda, m_w_o_lru, m_w_out, m_norm2_g, m_w_up, m_ffn_conv_w, m_ffn_conv_b, m_w_down, m_final_g, v_c_ctx, v_w_mod, v_b_mod, v_norm1_g, v_w_in, v_b_gate, v_q_norm_g, v_kv_norm_g, v_w_uq, v_w_ukv, v_w_o_attn, v_lru_conv_w, v_lru_conv_b, v_lru_w_a, v_lru_b_a, v_lru_w_x, v_lru_b_x, v_lru_lambda, v_w_o_lru, v_w_out, v_norm2_g, v_w_up, v_ffn_conv_w, v_ffn_conv_b, v_w_down, v_final_g):
    given = dict(locals())
    strip = lambda name, a: a if name in ('c_ctx', 'final_g') else a[0]
    wsh = {n: strip(n, given[n]) for n in WEIGHTS}
    msh = {n: strip(n, given['m_' + n]) for n in WEIGHTS}
    vsh = {n: strip(n, given['v_' + n]) for n in WEIGHTS}
    me = _my_index()

    rb_shards = {n: _shard_to_rb(n, wsh[n]).astype(BF16) for n in BIG_BF16}
    weights_started, weights_sent = exchange_start("weights_send", 'gather', [rb_shards[n] for n in BIG_BF16])

    small =_flat([c[0]] + [wsh[n] for n in SMALL_F32], F32, 8)
    small_all = all_gather("gather_small", small)
    shapes = [(D,)] + [wsh[n].shape for n in SMALL_F32]
    per_dev = [_unflat(small_all[p], shapes) for p in range(N_DEV)]
    c_all = jnp.stack([pd[0] for pd in per_dev])
    full = {}
    for i, n in enumerate(SMALL_F32):
        full[n] = _gathered_to_full(n, jnp.stack([pd[1 + i] for pd in per_dev]))

    cond = jnp.concatenate([c_all, c_ctx[None], jnp.zeros((7, D), F32)], axis=0)
    sil = cond * jax.nn.sigmoid(cond)
    mod_cols = matmul("mod_proj", sil, wsh['w_mod'], 'nn', F32, after=[weights_sent])
    mod_all = all_gather("gather_mod", mod_cols)
    mod_all = jnp.transpose(mod_all, (1, 0, 2)).reshape(16, 6 * D) + b_mod[0][None]
    mod_l = lax.dynamic_index_in_dim(mod_all, me, axis=0, keepdims=False)
    mod_c = mod_all[N_DEV]

    for n in REPLICATED:
        if n not in ('c_ctx', 'b_mod'):
            full[n] = wsh[n]

    def arrive(names, after):
        picked = [BIG_BF16.index(n) for n in names]
        lands = exchange_wait("weights_wait_" + names[0], 'gather',
                              tuple([part[i] for i in picked] for part in weights_started), after)
        return {n: _rb_from_gathered(n, lax.dynamic_update_slice_in_dim(land, rb_shards[n][None], me, axis=0))
                for n, land in zip(names, lands)}

    in_flight = {}

    def on_grad(n, g):
        chunks = _chunks_from_rb_grad(n, g)
        own = lax.dynamic_index_in_dim(chunks, me, axis=0, keepdims=True)
        started, token = exchange_start("grad_send_" + n, 'scatter', [chunks])
        in_flight[n] = (own, started)
        return token

    loss, grad_x, grads, dmod_l, dmod_c = local_step(x[0], ctx[0], loss_target[0], mod_l, mod_c, full, on_grad,
                                                     arrive)
    loss = lax.psum(loss, ("x", "y", "c"))

    dmod = _flat([dmod_l, dmod_c], F32, 8)
    dmod_all = all_gather("gather_dmod", dmod)
    dm = jnp.stack([jnp.stack(_unflat(dmod_all[p], [(6 * D,), (6 * D,)])) for p in range(N_DEV)])
    dmod_c_tot = dm[0, 1]
    for p in range(1, N_DEV):
        dmod_c_tot = dmod_c_tot + dm[p, 1]
    dm16 = jnp.concatenate([dm[:, 0], dmod_c_tot[None], jnp.zeros((7, 6 * D), F32)], axis=0)
    ncol = 6 * D // N_DEV
    dm16_cols = lax.dynamic_slice_in_dim(dm16.reshape(16, N_DEV, ncol), me, 1, axis=1)[:, 0]
    grad_w_mod = matmul("g_w_mod", sil, dm16_cols, 'tn', F32)
    dsil = matmul("d_cond", dm16_cols, wsh['w_mod'], 'nt', F32)
    sg = jax.nn.sigmoid(c_ctx)
    grads['c_ctx'] = dsil[N_DEV] * (sg * (1.0 + c_ctx * (1.0 - sg)))
    grads['b_mod'] = dmod_l + dmod_c

    rep_sizes = [math.prod(wsh[n].shape) for n in REPLICATED]
    chunks = [_full_to_chunks(n, grads[n].reshape(full[n].shape)) for n in SMALL_F32]
    chunks += [grads[n].reshape(N_DEV, -1) for n in REPLICATED]
    chunk_len = sum(ch.shape[1] for ch in chunks)
    quantum = 8 * FLAT_C
    padded = -(-chunk_len // quantum) * quantum
    send = jnp.pad(jnp.concatenate(chunks, axis=1), ((0, 0), (0, padded - chunk_len)))
    got = all_to_all_multi("grad_exchange", [send.reshape(N_DEV, padded // FLAT_C, FLAT_C)])
    g_final = {'w_mod': grad_w_mod}
    gsum = sum_slots("sum_small", got[0]).reshape(-1)
    for n in BIG_BF16:
        own, started = in_flight[n]
        (land,) = exchange_wait("grad_wait_" + n, 'scatter', started, gsum)
        slots = lax.dynamic_update_slice_in_dim(land, own, me, axis=0)
        g_final[n] = _rb_to_shard(n, sum_slots("sum_" + n, slots))
    at = 0
    for n in SMALL_F32:
        cnt = math.prod(wsh[n].shape)
        g_final[n] = gsum[at:at + cnt].reshape(wsh[n].shape)
        at += cnt
    rep_part = _flat([gsum[at:chunk_len]], F32, 8)
    rep_all = all_gather("gather_replicated_grads", rep_part).reshape(N_DEV, -1)
    at = 0
    for n, cnt in zip(REPLICATED, rep_sizes):
        g_final[n] = rep_all[:, at:at + cnt // N_DEV].reshape(wsh[n].shape)
        at += cnt // N_DEV

    stepped = {n: adamw("adamw_" + n, wsh[n], g_final[n], msh[n], vsh[n]) for n in ['w_mod'] + BIG_BF16}
    rest = [n for n in WEIGHTS if n not in stepped]
    flat = lambda d: _flat([d[n] for n in rest], F32, 8)
    rest_shapes = [wsh[n].shape for n in rest]
    rest_out = [_unflat(f, rest_shapes) for f in adamw("adamw_small", flat(wsh), flat(g_final), flat(msh), flat(vsh))]
    for i, n in enumerate(rest):
        stepped[n] = tuple(r[i] for r in rest_out)
    shaped = lambda n, a: a.reshape(given[n].shape)
    return (loss, grad_x[None],
            *[shaped(n, g_final[n]) for n in WEIGHTS],
            *[shaped(n, stepped[n][k]) for k in range(3) for n in WEIGHTS])
```

```python
import functools
import math

import jax
import jax.numpy as jnp
from jax import lax
from jax.experimental import pallas as pl
from jax.experimental.pallas import tpu as pltpu

F32 = jnp.float32
BF16 = jnp.bfloat16
MESH = pl.DeviceIdType.MESH

N_DEV = 8
D = 1024
N_HEADS = 8
HEAD_PAD = 128
QK_NOPE, QK_ROPE, V_HEAD = 64, 32, 64
QK_DIM = QK_NOPE + QK_ROPE
Q_RANK, KV_RANK = 384, 256
LRU_W, LRU_BLOCKS, LRU_BW = 1280, 10, 128
FFN = 2816
GRID_W = 64
ROPE_BASE = 10000.0
LRU_C = 8.0
EPS = 1e-6
Z_Q, Z_KV, Z_KR, Z_XB, Z_YB, Z_GL, Z_END = 0, 384, 640, 768, 2048, 3328, 5376
ADAM_LR, ADAM_B1, ADAM_B2, ADAM_EPS, ADAM_WD, ADAM_STEP = 0.001, 0.9, 0.999, 1e-08, 0.01, 10

VMEM_LIMIT = 52 * 1024 * 1024
FLAT_C = 512
BIG_ROWS = 256

WEIGHTS = ['c_ctx', 'w_mod', 'b_mod', 'norm1_g', 'w_in', 'b_gate', 'q_norm_g', 'kv_norm_g', 'w_uq', 'w_ukv',
           'w_o_attn', 'lru_conv_w', 'lru_conv_b', 'lru_w_a', 'lru_b_a', 'lru_w_x', 'lru_b_x', 'lru_lambda',
           'w_o_lru', 'w_out', 'norm2_g', 'w_up', 'ffn_conv_w', 'ffn_conv_b', 'w_down', 'final_g']
COL_SHARDED = ['w_in', 'w_uq', 'w_ukv', 'w_o_attn', 'lru_conv_w', 'lru_b_a', 'lru_b_x', 'lru_lambda', 'w_up',
               'ffn_conv_w']
ROW_SHARDED = ['w_o_lru', 'w_out', 'w_down']
BIG_BF16 = ['w_in', 'w_uq', 'w_ukv', 'w_o_attn', 'w_o_lru', 'w_out', 'w_up', 'w_down']
SMALL_F32 = ['lru_conv_w', 'lru_b_a', 'lru_b_x', 'lru_lambda', 'ffn_conv_w']
SHARDED = BIG_BF16 + SMALL_F32
REPLICATED = ['c_ctx', 'b_mod', 'norm1_g', 'b_gate', 'q_norm_g', 'kv_norm_g', 'lru_conv_b', 'lru_w_a', 'lru_w_x',
              'norm2_g', 'ffn_conv_b', 'final_g']


def _cparams(sem=None):
    return pltpu.CompilerParams(dimension_semantics=sem, vmem_limit_bytes=VMEM_LIMIT)


def _pick(n, cands):
    for c in cands:
        if c <= n and n % c == 0:
            return c
    return n


def _best_div(n, mult, cap):
    best = mult
    for d in range(mult, min(n, cap) + 1, mult):
        if n % d == 0:
            best = d
    return best


ROW_TILES = (1088, 1024, 544, 512, 256, 128, 64, 32, 16, 8)
LANE_TILES = (1408, 1024, 896, 768, 640, 512, 384, 256, 128)


def _my_pos():
    return lax.axis_index("x"), lax.axis_index("y"), lax.axis_index("c")


def _my_index():
    x, y, c = _my_pos()
    return 4 * x + 2 * y + c


def all_gather_multi(name, shards):
    n_arr = len(shards)
    arrays = range(n_arr)

    def body(*refs):
        x_refs, out_refs = refs[:n_arr], refs[n_arr:2 * n_arr]
        send_sems, recv_sems, local_sems = refs[2 * n_arr:]
        x, y, c = _my_pos()
        me, sibling = (x, y, c), (x, y, 1 - c)
        chips = [(1 - x, y), (x, 1 - y), (1 - x, 1 - y)]

        def slot(a, px, py, pc):
            return out_refs[a].at[4 * px + 2 * py + pc]

        def copy(a, k, block, to, src=None):
            return pltpu.make_async_remote_copy(
                src_ref=slot(a, *block) if src is None else src, dst_ref=slot(a, *block),
                send_sem=send_sems.at[7 * a + k], recv_sem=recv_sems.at[7 * a + k], device_id=to,
                device_id_type=MESH)

        mine = [pltpu.make_async_copy(x_refs[a], slot(a, *me), local_sems.at[a]) for a in arrays]
        first = [copy(a, 1 + j, me, (*chip, c), src=x_refs[a]) for j, chip in enumerate(chips) for a in arrays]
        first += [copy(a, 0, me, sibling, src=x_refs[a]) for a in arrays]
        for cp in first + mine:
            cp.start()
        passed = []
        for j, chip in enumerate(chips):
            for a in arrays:
                copy(a, 1 + j, (*chip, c), me).wait_recv()
                passed.append(copy(a, 4 + j, (*chip, c), sibling))
                passed[-1].start()
        for a in arrays:
            copy(a, 0, sibling, me).wait_recv()
            for j, chip in enumerate(chips):
                copy(a, 4 + j, (*chip, 1 - c), me).wait_recv()
        for cp in first + passed:
            cp.wait_send()
        for cp in mine:
            cp.wait()

    hbm = pl.BlockSpec(memory_space=pl.ANY)
    return pl.pallas_call(
        body, name=name,
        out_shape=[jax.ShapeDtypeStruct((N_DEV,) + s.shape, s.dtype) for s in shards],
        in_specs=[hbm] * n_arr, out_specs=[hbm] * n_arr,
        scratch_shapes=[pltpu.SemaphoreType.DMA((7 * n_arr,)), pltpu.SemaphoreType.DMA((7 * n_arr,)),
                        pltpu.SemaphoreType.DMA((n_arr,))],
    )(*shards)


def all_gather(name, shard):
    return all_gather_multi(name, [shard])[0]


def all_to_all_multi(name, chunk_arrays):
    n_arr = len(chunk_arrays)
    arrays = range(n_arr)

    def body(*refs):
        x_refs, out_refs = refs[:n_arr], refs[n_arr:2 * n_arr]
        send_sems, recv_sems, local_sems = refs[2 * n_arr:]
        x, y, c = _my_pos()
        me = 4 * x + 2 * y + c
        mine = [pltpu.make_async_copy(x_refs[a].at[me], out_refs[a].at[me], local_sems.at[a]) for a in arrays]
        sends, arrivals = [], []
        for rel in (6, 4, 2, 7, 5, 3, 1):
            dx, dy, dc = (rel >> 2) & 1, (rel >> 1) & 1, rel & 1
            px, py, pc = x ^ dx, y ^ dy, c ^ dc
            peer = 4 * px + 2 * py + pc
            for a in arrays:
                k = 7 * a + rel - 1
                sends.append(pltpu.make_async_remote_copy(
                    src_ref=x_refs[a].at[peer], dst_ref=out_refs[a].at[me],
                    send_sem=send_sems.at[k], recv_sem=recv_sems.at[k],
                    device_id=(px, py, pc), device_id_type=MESH))
                arrivals.append(pltpu.make_async_remote_copy(
                    src_ref=x_refs[a].at[peer], dst_ref=out_refs[a].at[peer],
                    send_sem=send_sems.at[k], recv_sem=recv_sems.at[k],
                    device_id=(x, y, c), device_id_type=MESH))
        for cp in sends + mine:
            cp.start()
        for cp in arrivals:
            cp.wait_recv()
        for cp in sends:
            cp.wait_send()
        for cp in mine:
            cp.wait()

    hbm = pl.BlockSpec(memory_space=pl.ANY)
    return pl.pallas_call(
        body, name=name,
        out_shape=[jax.ShapeDtypeStruct(s.shape, s.dtype) for s in chunk_arrays],
        in_specs=[hbm] * n_arr, out_specs=[hbm] * n_arr,
        scratch_shapes=[pltpu.SemaphoreType.DMA((7 * n_arr,)), pltpu.SemaphoreType.DMA((7 * n_arr,)),
                        pltpu.SemaphoreType.DMA((n_arr,))],
    )(*chunk_arrays)


def _peers():
    x, y, c = _my_pos()
    out = []
    for rel in (6, 4, 2, 7, 5, 3, 1):
        px, py, pc = x ^ ((rel >> 2) & 1), y ^ ((rel >> 1) & 1), c ^ (rel & 1)
        out.append((rel - 1, (px, py, pc), 4 * px + 2 * py + pc))
    return out


def _exchange_copies(mode, src_refs, land_refs, send_sems, recv_sems):
    x, y, c = _my_pos()
    me = 4 * x + 2 * y + c
    sends, arrivals = [], []
    for k, peer_pos, peer in _peers():
        for a, (src, land) in enumerate(zip(src_refs, land_refs)):
            piece = src.at[peer] if mode == 'scatter' else src
            sems = dict(send_sem=send_sems[a].at[k], recv_sem=recv_sems[a].at[k], device_id_type=MESH)
            sends.append(pltpu.make_async_remote_copy(src_ref=piece, dst_ref=land.at[me], device_id=peer_pos, **sems))
            arrivals.append(pltpu.make_async_remote_copy(src_ref=piece, dst_ref=land.at[peer], device_id=(x, y, c), **sems))
    return sends, arrivals


_HBM = pl.BlockSpec(memory_space=pltpu.HBM)
_SEM = pl.BlockSpec(memory_space=pltpu.SEMAPHORE)


def exchange_start(name, mode, arrays, after=()):
    n_arr, n_after = len(arrays), len(after)
    land_shapes = [a.shape if mode == 'scatter' else (N_DEV,) + a.shape for a in arrays]

    def body(*refs):
        src_refs, land_refs = refs[:n_arr], refs[n_arr:2 * n_arr]
        refs = refs[n_after:]
        send_sems, recv_sems = refs[2 * n_arr:3 * n_arr], refs[3 * n_arr:4 * n_arr]
        sends, _ = _exchange_copies(mode, src_refs, land_refs, send_sems, recv_sems)
        for cp in sends:
            cp.start()
        token = refs[-1]
        token[...] = jnp.zeros_like(token)

    sem = pltpu.SemaphoreType.DMA((N_DEV - 1,))
    res = pl.pallas_call(
        body, name=name,
        out_shape=[sem] * (2 * n_arr) + [pltpu.HBM(a.shape, a.dtype) for a in arrays]
        + [pltpu.HBM(s, a.dtype) for s, a in zip(land_shapes, arrays)] + [jax.ShapeDtypeStruct((8, 128), F32)],
        in_specs=[_HBM] * (2 * n_arr) + [pl.BlockSpec(memory_space=pl.ANY)] * n_after,
        out_specs=[_SEM] * (2 * n_arr) + [_HBM] * (2 * n_arr) + [pl.BlockSpec(memory_space=pltpu.VMEM)],
        input_output_aliases={i: 2 * n_arr + i for i in range(2 * n_arr)},
        compiler_params=pltpu.CompilerParams(has_side_effects=pltpu.SideEffectType.DATAFLOW_SIDE_EFFECTING),
    )(*[pltpu.with_memory_space_constraint(a, pltpu.HBM) for a in arrays],
      *[pltpu.with_memory_space_constraint(lax.empty(s, a.dtype), pltpu.HBM) for s, a in zip(land_shapes, arrays)],
      *after)
    return (res[:n_arr], res[n_arr:2 * n_arr], res[2 * n_arr:3 * n_arr], res[3 * n_arr:4 * n_arr]), res[-1]


def exchange_wait(name, mode, started, after):
    send_sems, recv_sems, thru, land = started
    n_arr = len(thru)

    def body(*refs):
        src_refs, land_refs = refs[:n_arr], refs[n_arr:2 * n_arr]
        s_sems, r_sems = refs[2 * n_arr:3 * n_arr], refs[3 * n_arr:4 * n_arr]
        sends, arrivals = _exchange_copies(mode, src_refs, land_refs, s_sems, r_sems)
        for cp in sends:
            cp.wait_send()
        for cp in arrivals:
            cp.wait_recv()

    res = pl.pallas_call(
        body, name=name,
        out_shape=[pltpu.HBM(a.shape, a.dtype) for a in thru] + [pltpu.HBM(a.shape, a.dtype) for a in land],
        in_specs=[_HBM] * (2 * n_arr) + [_SEM] * (2 * n_arr) + [pl.BlockSpec(memory_space=pl.ANY)],
        out_specs=[_HBM] * (2 * n_arr),
        input_output_aliases={i: i for i in range(2 * n_arr)},
        compiler_params=pltpu.CompilerParams(has_side_effects=pltpu.SideEffectType.DATAFLOW_SIDE_EFFECTING),
    )(*thru, *land, *send_sems, *recv_sems, after)
    return res[n_arr:]


def sum_slots(name, slots):
    _, r, ccols = slots.shape
    tc = _pick(ccols, (256, 128))

    def body(s_ref, o_ref):
        acc = s_ref[0].astype(F32)
        for p in range(1, N_DEV):
            acc = acc + s_ref[p].astype(F32)
        o_ref[...] = acc

    return pl.pallas_call(
        body, name=name, grid=(ccols // tc,),
        out_shape=jax.ShapeDtypeStruct((r, ccols), F32),
        in_specs=[pl.BlockSpec((N_DEV, r, tc), lambda j: (0, 0, j))],
        out_specs=pl.BlockSpec((r, tc), lambda j: (0, j)),
        compiler_params=_cparams(("parallel",)),
    )(slots)


def matmul(name, a, b, mode, out_dtype, tm=None, tn=None, tk=None, after=()):
    after = [t for t in after if t is not None]
    if mode == 'nn':
        (m, k), (k2, n) = a.shape, b.shape
    elif mode == 'nt':
        (m, k), (n, k2) = a.shape, b.shape
    else:
        (k, m), (k2, n) = a.shape, b.shape
    assert k == k2, (name, a.shape, b.shape, mode)
    if mode == 'tn':
        tm = tm or _pick(m, LANE_TILES)
        tk = tk or _pick(k, ROW_TILES)
    else:
        tm = tm or _pick(m, ROW_TILES)
        tk = tk or _pick(k, LANE_TILES)
    tn = tn or _pick(n, LANE_TILES)
    nk = k // tk
    if mode == 'nn':
        a_spec = pl.BlockSpec((tm, tk), lambda i, j, kk: (i, kk))
        b_spec = pl.BlockSpec((tk, tn), lambda i, j, kk: (kk, j))
        dn = (((1,), (0,)), ((), ()))
    elif mode == 'nt':
        a_spec = pl.BlockSpec((tm, tk), lambda i, j, kk: (i, kk))
        b_spec = pl.BlockSpec((tn, tk), lambda i, j, kk: (j, kk))
        dn = (((1,), (1,)), ((), ()))
    else:
        a_spec = pl.BlockSpec((tk, tm), lambda i, j, kk: (kk, i))
        b_spec = pl.BlockSpec((tk, tn), lambda i, j, kk: (kk, j))
        dn = (((0,), (0,)), ((), ()))

    def product(a_ref, b_ref):
        return lax.dot_general(a_ref[...].astype(BF16), b_ref[...].astype(BF16), dn, preferred_element_type=F32)

    n_after = len(after)

    def body_one(a_ref, b_ref, *rest):
        o_ref = rest[n_after]
        o_ref[...] = product(a_ref, b_ref).astype(o_ref.dtype)

    def body(a_ref, b_ref, *rest):
        o_ref, acc_ref = rest[n_after:]
        kk = pl.program_id(2)

        @pl.when(kk == 0)
        def _():
            acc_ref[...] = jnp.zeros_like(acc_ref)

        acc_ref[...] += product(a_ref, b_ref)

        @pl.when(kk == nk - 1)
        def _():
            o_ref[...] = acc_ref[...].astype(o_ref.dtype)

    return pl.pallas_call(
        body_one if nk == 1 else body, name=name, grid=(m // tm, n // tn, nk),
        out_shape=jax.ShapeDtypeStruct((m, n), out_dtype),
        in_specs=[a_spec, b_spec] + [pl.BlockSpec(memory_space=pl.ANY)] * n_after,
        out_specs=pl.BlockSpec((tm, tn), lambda i, j, kk: (i, j)),
        scratch_shapes=[] if nk == 1 else [pltpu.VMEM((tm, tn), F32)],
        compiler_params=_cparams(("parallel", "parallel", "arbitrary")),
    )(a, b, *after)


def rowwise(name, fn, rows, params, out_rows, out_accs, n_rows, t_lat, tm):
    nb = n_rows // tm
    in_specs, piece_counts = [], []
    operands = []
    for arr, off, width in rows:
        g = math.gcd(off, width) if off else width
        assert g % 128 == 0 or (off == 0 and width == arr.shape[1]), (name, off, width)
        cnt = width // g
        last = arr.shape[0] // tm - 1
        clamp = arr.shape[0] < n_rows
        for p in range(cnt):
            cb = off // g + p
            if clamp:
                in_specs.append(pl.BlockSpec((tm, g), lambda i, cb=cb, last=last: (jnp.minimum(i, last), cb)))
            else:
                in_specs.append(pl.BlockSpec((tm, g), lambda i, cb=cb: (i, cb)))
            operands.append(arr)
        piece_counts.append(cnt)
    for p in params:
        in_specs.append(pl.BlockSpec(p.shape, lambda i, nd=p.ndim: (0,) * nd))
        operands.append(p)
    n_in = sum(piece_counts)
    n_par = len(params)
    n_or = len(out_rows)
    out_shape = [jax.ShapeDtypeStruct((n_rows, w), dt) for w, dt in out_rows]
    out_shape += [jax.ShapeDtypeStruct(s, F32) for s in out_accs]
    out_specs = [pl.BlockSpec((tm, w), lambda i: (i, 0)) for w, _ in out_rows]
    out_specs += [pl.BlockSpec(s, lambda i, nd=len(s): (0,) * nd) for s in out_accs]

    def body(*refs):
        in_refs, par_refs = refs[:n_in], refs[n_in:n_in + n_par]
        orow_refs = refs[n_in + n_par:n_in + n_par + n_or]
        oacc_refs = refs[n_in + n_par + n_or:]
        i = pl.program_id(0)
        tiles, at = [], 0
        for cnt in piece_counts:
            parts = [in_refs[at + p][...].astype(F32) for p in range(cnt)]
            tiles.append(parts[0] if cnt == 1 else jnp.concatenate(parts, axis=1))
            at += cnt
        is_ctx = i * tm >= t_lat
        outs, accs = fn(is_ctx, tiles, [p[...] for p in par_refs])
        for o_ref, o in zip(orow_refs, outs):
            o_ref[...] = o.astype(o_ref.dtype)
        if oacc_refs:
            @pl.when(i == 0)
            def _():
                for a_ref in oacc_refs:
                    a_ref[...] = jnp.zeros_like(a_ref)
            for a_ref, a in zip(oacc_refs, accs):
                a_ref[...] += a.astype(F32)

    res = pl.pallas_call(
        body, name=name, grid=(nb,),
        out_shape=out_shape, in_specs=in_specs, out_specs=out_specs,
        compiler_params=_cparams(("arbitrary",)),
    )(*operands)
    return res[:n_or], res[n_or:]


def _rms(x, g):
    return x * lax.rsqrt(jnp.mean(x * x, axis=-1, keepdims=True) + EPS) * g


def _norm_mod(x, g, sc, sh):
    return _rms(x, g) * (1.0 + sc) + sh


def _sigmoid(x):
    return 1.0 / (1.0 + jnp.exp(-x))


def _silu(x):
    return x * _sigmoid(x)


def _gelu(x):
    return 0.5 * x * (1.0 + jnp.tanh(math.sqrt(2.0 / math.pi) * (x + 0.044715 * (x * x * x))))


def _sel(is_ctx, p):
    return jnp.where(is_ctx, p[1:2], p[0:1])


def _seg_acc(is_ctx, v):
    rows = lax.broadcasted_iota(jnp.int32, (2, v.shape[1]), 0)
    return jnp.where(rows == is_ctx.astype(jnp.int32), jnp.broadcast_to(v, (2, v.shape[1])), 0.0)


def _rsum(v):
    return jnp.sum(v, axis=0, keepdims=True)


def _shift_rows(x, o, t_lat, n):
    if o == 0:
        return x
    y = pltpu.roll(x, (-o) % n, 0)
    t = lax.broadcasted_iota(jnp.int32, x.shape, 0)
    src = t + o
    ok = (src >= 0) & (src < n) & ((src >= t_lat) == (t >= t_lat))
    return jnp.where(ok, y, 0.0)


def conv_fwd(name, xarr, col_off, width, w, b, left, n_rows, t_lat, out_dtype, cb=128):
    taps = w.shape[0]
    assert col_off % cb == 0 and width % cb == 0

    def body(x_ref, w_ref, b_ref, o_ref):
        x = x_ref[...].astype(F32)
        acc = jnp.broadcast_to(b_ref[...], x.shape)
        for k in range(taps):
            acc = acc + _shift_rows(x, k - left, t_lat, n_rows) * w_ref[k:k + 1, :]
        o_ref[...] = acc.astype(o_ref.dtype)

    return pl.pallas_call(
        body, name=name, grid=(width // cb,),
        out_shape=jax.ShapeDtypeStruct((n_rows, width), out_dtype),
        in_specs=[pl.BlockSpec((n_rows, cb), lambda j: (0, col_off // cb + j)),
                  pl.BlockSpec((taps, cb), lambda j: (0, j)),
                  pl.BlockSpec((1, cb), lambda j: (0, j))],
        out_specs=pl.BlockSpec((n_rows, cb), lambda j: (0, j)),
        compiler_params=_cparams(("parallel",)),
    )(xarr, w, b)


def conv_bwd(name, dout, xarr, col_off, width, w, left, n_rows, t_lat, cb=128):
    taps = w.shape[0]

    def body(d_ref, x_ref, w_ref, dx_ref, dw_ref, db_ref):
        d = d_ref[...].astype(F32)
        x = x_ref[...].astype(F32)
        dx = jnp.zeros_like(d)
        dws = []
        for k in range(taps):
            dx = dx + _shift_rows(d, left - k, t_lat, n_rows) * w_ref[k:k + 1, :]
            dws.append(_rsum(d * _shift_rows(x, k - left, t_lat, n_rows)))
        dx_ref[...] = dx.astype(dx_ref.dtype)
        dw_ref[...] = jnp.concatenate(dws, axis=0)
        db_ref[...] = _rsum(d)

    return pl.pallas_call(
        body, name=name, grid=(width // cb,),
        out_shape=[jax.ShapeDtypeStruct((n_rows, width), BF16), jax.ShapeDtypeStruct((taps, width), F32),
                   jax.ShapeDtypeStruct((1, width), F32)],
        in_specs=[pl.BlockSpec((n_rows, cb), lambda j: (0, j)),
                  pl.BlockSpec((n_rows, cb), lambda j: (0, col_off // cb + j)),
                  pl.BlockSpec((taps, cb), lambda j: (0, j))],
        out_specs=[pl.BlockSpec((n_rows, cb), lambda j: (0, j)), pl.BlockSpec((taps, cb), lambda j: (0, j)),
                   pl.BlockSpec((1, cb), lambda j: (0, j))],
        compiler_params=_cparams(("parallel",)),
    )(dout, xarr, w)


def _chunk_order(direction, nb, nbl):
    if direction == 'f':
        return lambda s: ((s + nbl) % nb, 0)
    return lambda s: (nb - 1 - s, 0)


def _adjoint_order(direction, nb, nbl):
    if direction == 'f':
        return lambda s: ((nb - 1 - s + nbl) % nb, 0)
    return lambda s: (s, 0)


SUBLANES = 8


def _chunk_scan(a, b, carry, rev):
    tc = a.shape[0]
    row = lax.broadcasted_iota(jnp.int32, a.shape, 0)
    in_tile = jnp.bitwise_and(row, SUBLANES - 1)
    for k in (1, 2, 4):
        shift = tc - k if rev else k
        edge = in_tile >= SUBLANES - k if rev else in_tile < k
        b = jnp.where(edge, b, a * pltpu.roll(b, shift, 0) + b)
        a = jnp.where(edge, a, a * pltpu.roll(a, shift, 0))
    nt = tc // SUBLANES
    hs = [None] * nt
    c = carry
    for kt in range(nt):
        k = nt - 1 - kt if rev else kt
        h = b[k * SUBLANES:(k + 1) * SUBLANES] + a[k * SUBLANES:(k + 1) * SUBLANES] * c
        hs[k] = h
        c = h[0:1] if rev else h[SUBLANES - 1:SUBLANES]
    h = jnp.concatenate(hs, axis=0)
    if rev:
        return h, jnp.where(row == tc - 1, carry, pltpu.roll(h, tc - 1, 0)), c
    return h, jnp.where(row == 0, carry, pltpu.roll(h, 1, 0)), c


def scan_fwd(name, a, u, direction, n_rows, t_lat, tc=128):
    w = a.shape[1]
    nb, nbl = n_rows // tc, t_lat // tc
    order = _chunk_order(direction, nb, nbl)
    rev = direction == 'b'

    def body(a_ref, u_ref, h_ref, hp_ref, carry):
        @pl.when(pl.program_id(0) == 0)
        def _():
            carry[...] = jnp.zeros_like(carry)

        h_ref[...], hp_ref[...], carry[...] = _chunk_scan(a_ref[...], u_ref[...], carry[...], rev)

    spec = pl.BlockSpec((tc, w), order)
    return pl.pallas_call(
        body, name=name, grid=(nb,),
        out_shape=[jax.ShapeDtypeStruct((n_rows, w), F32)] * 2,
        in_specs=[spec, spec], out_specs=[spec, spec],
        scratch_shapes=[pltpu.VMEM((1, w), F32)],
        compiler_params=_cparams(("arbitrary",)),
    )(a, u)


def scan_adj(name, a, dh, hprev, direction, n_rows, t_lat, tc=128):
    w = a.shape[1]
    nb, nbl = n_rows // tc, t_lat // tc
    order = _adjoint_order(direction, nb, nbl)
    rev = direction == 'f'

    def dh_order(s):
        c, _ = order(s)
        return (jnp.minimum(c, nbl - 1), 0)

    def body(a_ref, dh_ref, hp_ref, du_ref, da_ref, carry):
        s = pl.program_id(0)

        @pl.when(s == 0)
        def _():
            carry[...] = jnp.zeros_like(carry)

        chunk, _ = order(s)
        live = (chunk < nbl).astype(F32)

        av = a_ref[...]
        dv = dh_ref[...] * live
        _, c_next, carry[...] = _chunk_scan(av, av * dv, carry[...], rev)
        lam = dv + c_next
        du_ref[...] = lam
        da_ref[...] = lam * hp_ref[...]

    spec = pl.BlockSpec((tc, w), order)
    return pl.pallas_call(
        body, name=name, grid=(nb,),
        out_shape=[jax.ShapeDtypeStruct((n_rows, w), F32)] * 2,
        in_specs=[spec, pl.BlockSpec((tc, w), dh_order), spec], out_specs=[spec, spec],
        scratch_shapes=[pltpu.VMEM((1, w), F32)],
        compiler_params=_cparams(("arbitrary",)),
    )(a, dh, hprev)


def _neg_expm1(y):
    series = -(y * (1.0 + y * (0.5 + y * (1.0 / 6.0 + y * (1.0 / 24.0)))))
    return jnp.where(y > -0.03, series, 1.0 - jnp.exp(y))


def _gate_elem(pre_r, pre_i, xc, b_a, b_x, sp):
    r = _sigmoid(pre_r + b_a)
    i = _sigmoid(pre_i + b_x)
    log_a = (-LRU_C) * r * sp
    a = jnp.exp(log_a)
    mult = jnp.sqrt(_neg_expm1(2.0 * log_a))
    return a, mult * (i * xc)


def _gate_elem_bwd(pre_r, pre_i, xc, b_a, b_x, sp, da, du):
    r = _sigmoid(pre_r + b_a)
    i = _sigmoid(pre_i + b_x)
    log_a = (-LRU_C) * r * sp
    a = jnp.exp(log_a)
    m2 = _neg_expm1(2.0 * log_a)
    inv_mult = lax.rsqrt(m2)
    g = du * (m2 * inv_mult)
    d_mult = du * (i * xc)
    d_log_a = (da - d_mult * a * inv_mult) * a
    d_pre_r = d_log_a * ((-LRU_C) * sp) * (r * (1.0 - r))
    d_pre_i = g * xc * (i * (1.0 - i))
    return d_pre_r, d_pre_i, g * i, _rsum(d_log_a * ((-LRU_C) * r))


def _blockdiag(xb16, w_ref_val, d):
    outs = []
    for n in range(LRU_BLOCKS):
        outs.append(jnp.dot(xb16[:, n * LRU_BW:(n + 1) * LRU_BW], w_ref_val[d * LRU_BLOCKS + n],
                            preferred_element_type=F32))
    return jnp.concatenate(outs, axis=1)


def gates_fwd(xc, w_a, w_x, b_a, b_x, sp, n_rows, t_lat, tm):
    def fn(is_ctx, rows, params):
        (x,), (wa, wx, ba, bx, spv) = rows, params
        xb16 = x.astype(BF16)
        outs = []
        for d in range(2):
            a, u = _gate_elem(_blockdiag(xb16, wa, d), _blockdiag(xb16, wx, d), x,
                              ba[d:d + 1], bx[d:d + 1], spv[d:d + 1])
            outs += [a, u]
        return outs, []

    (a_f, u_f, a_b, u_b), _ = rowwise("gates_fwd", fn, [(xc, 0, LRU_W)], [w_a, w_x, b_a, b_x, sp],
                                      [(LRU_W, F32)] * 4, [], n_rows, t_lat, tm)
    return a_f, u_f, a_b, u_b


def gates_bwd(xc, da_f, du_f, da_b, du_b, w_a, w_x, b_a, b_x, sp, n_rows, t_lat, tm):
    def fn(is_ctx, rows, params):
        (x, daf, duf, dab, dub), (wa, wx, ba, bx, spv) = rows, params
        xb16 = x.astype(BF16)
        dxc = jnp.zeros_like(x)
        dwa, dwx, dba, dbx, dsp = [], [], [], [], []
        for d, (da, du) in enumerate(((daf, duf), (dab, dub))):
            dpr, dpi, dx_e, dsp_d = _gate_elem_bwd(_blockdiag(xb16, wa, d), _blockdiag(xb16, wx, d), x,
                                                   ba[d:d + 1], bx[d:d + 1], spv[d:d + 1], da, du)
            dba_d, dbx_d = _rsum(dpr), _rsum(dpi)
            dxc = dxc + dx_e
            dpr16, dpi16 = dpr.astype(BF16), dpi.astype(BF16)
            back = []
            for n in range(LRU_BLOCKS):
                sl = slice(n * LRU_BW, (n + 1) * LRU_BW)
                nt_dims = (((1,), (1,)), ((), ()))
                back.append(lax.dot_general(dpr16[:, sl], wa[d * LRU_BLOCKS + n], nt_dims, preferred_element_type=F32)
                            + lax.dot_general(dpi16[:, sl], wx[d * LRU_BLOCKS + n], nt_dims,
                                              preferred_element_type=F32))
                tn_dims = (((0,), (0,)), ((), ()))
                dwa.append(lax.dot_general(xb16[:, sl], dpr16[:, sl], tn_dims, preferred_element_type=F32)[None])
                dwx.append(lax.dot_general(xb16[:, sl], dpi16[:, sl], tn_dims, preferred_element_type=F32)[None])
            dxc = dxc + jnp.concatenate(back, axis=1)
            dba.append(dba_d)
            dbx.append(dbx_d)
            dsp.append(dsp_d)
        cat0 = lambda xs: jnp.concatenate(xs, axis=0)
        return [dxc], [cat0(dwa), cat0(dwx), cat0(dba), cat0(dbx), cat0(dsp)]

    (dxc,), accs = rowwise("gates_bwd", fn,
                           [(xc, 0, LRU_W), (da_f, 0, LRU_W), (du_f, 0, LRU_W), (da_b, 0, LRU_W), (du_b, 0, LRU_W)],
                           [w_a, w_x, b_a, b_x, sp], [(LRU_W, F32)],
                           [(2 * LRU_BLOCKS, LRU_BW, LRU_BW)] * 2 + [(2, LRU_W)] * 3, n_rows, t_lat, tm)
    return dxc, accs


def _rope_tables(t_lat, n_rows):
    rows = t_lat // GRID_W
    row_ids = jnp.repeat(jnp.arange(rows), GRID_W).astype(F32)
    col_ids = jnp.tile(jnp.arange(GRID_W), rows).astype(F32)
    axis_dim = QK_ROPE // 2
    inv = 1.0 / (ROPE_BASE ** (jnp.arange(0, axis_dim, 2, dtype=F32) / axis_dim))
    ang = jnp.concatenate([row_ids[:, None] * inv, col_ids[:, None] * inv], axis=-1)
    cos, sin = jnp.cos(ang), jnp.sin(ang)
    half = QK_ROPE // 2
    ones, zeros = jnp.ones((t_lat, QK_NOPE), F32), jnp.zeros((t_lat, QK_NOPE), F32)
    pad1, pad0 = jnp.ones((t_lat, HEAD_PAD - QK_DIM), F32), jnp.zeros((t_lat, HEAD_PAD - QK_DIM), F32)
    zh = jnp.zeros((t_lat, half), F32)
    c_tab = jnp.concatenate([ones, cos, cos, pad1], axis=1)
    s1 = jnp.concatenate([zeros, -sin, zh, pad0], axis=1)
    s2 = jnp.concatenate([zeros, zh, sin, pad0], axis=1)
    n_ctx = n_rows - t_lat
    c_tab = jnp.concatenate([c_tab, jnp.ones((n_ctx, HEAD_PAD), F32)], axis=0)
    s1 = jnp.concatenate([s1, jnp.zeros((n_ctx, HEAD_PAD), F32)], axis=0)
    s2 = jnp.concatenate([s2, jnp.zeros((n_ctx, HEAD_PAD), F32)], axis=0)
    return c_tab, s1, s2


def _rope(x, c, s1, s2):
    half = QK_ROPE // 2
    return x * c + pltpu.roll(x, HEAD_PAD - half, 1) * s1 + pltpu.roll(x, half, 1) * s2


def _rope_t(dy, c, s1, s2):
    half = QK_ROPE // 2
    return dy * c + pltpu.roll(dy * s1, half, 1) + pltpu.roll(dy * s2, HEAD_PAD - half, 1)


def _heads(x):
    return [x[:, h * HEAD_PAD:(h + 1) * HEAD_PAD] for h in range(N_HEADS)]


Q_SCALE = QK_DIM ** -0.5 * math.log2(math.e)


def attn_fwd(q, k, v, t_lat, n_rows, tq):
    def body(q_ref, k_ref, v_ref, o_ref, lse_ref):
        s = lax.dot_general(q_ref[...], k_ref[...], (((1,), (1,)), ((), ())), preferred_element_type=F32)
        m = jnp.max(s, axis=-1, keepdims=True)
        p = jnp.exp2(s - m)
        l = jnp.sum(p, axis=-1, keepdims=True)
        o = jnp.dot(p.astype(BF16), v_ref[...], preferred_element_type=F32) / l
        o_ref[...] = o.astype(o_ref.dtype)
        lse_ref[...] = jnp.broadcast_to(m + jnp.log2(l), lse_ref.shape)

    qspec = pl.BlockSpec((tq, HEAD_PAD), lambda h, i: (i, h))
    kspec = pl.BlockSpec((n_rows, HEAD_PAD), lambda h, i: (0, h))
    return pl.pallas_call(
        body, name="attn_fwd", grid=(N_HEADS, t_lat // tq),
        out_shape=[jax.ShapeDtypeStruct((t_lat, N_HEADS * HEAD_PAD), BF16),
                   jax.ShapeDtypeStruct((t_lat, N_HEADS * HEAD_PAD), F32)],
        in_specs=[qspec, kspec, kspec], out_specs=[qspec, qspec],
        compiler_params=_cparams(("parallel", "arbitrary")),
    )(q, k, v)


def attn_bwd(q, k, v, o, do, lse, t_lat, n_rows, tq):
    scale = QK_DIM ** -0.5
    nq = t_lat // tq
    nt = (((1,), (1,)), ((), ()))
    tn = (((0,), (0,)), ((), ()))

    def body(q_ref, k_ref, v_ref, o_ref, do_ref, lse_ref, dq_ref, dk_ref, dv_ref):
        @pl.when(pl.program_id(1) == 0)
        def _():
            dk_ref[...] = jnp.zeros_like(dk_ref)
            dv_ref[...] = jnp.zeros_like(dv_ref)

        qv, kv, vv, dov = q_ref[...], k_ref[...], v_ref[...], do_ref[...]
        s = lax.dot_general(qv, kv, nt, preferred_element_type=F32)
        p = jnp.exp2(s - lse_ref[:, 0:1])
        dv_ref[...] += lax.dot_general(p.astype(BF16), dov, tn, preferred_element_type=F32)
        dp = lax.dot_general(dov, vv, nt, preferred_element_type=F32)
        delta = jnp.sum(dov.astype(F32) * o_ref[...].astype(F32), axis=-1, keepdims=True)
        ds = (p * (dp - delta)).astype(BF16)
        dq_ref[...] = jnp.dot(ds, kv, preferred_element_type=F32) * scale
        dk_ref[...] += lax.dot_general(ds, qv, tn, preferred_element_type=F32)

        @pl.when(pl.program_id(1) == nq - 1)
        def _():
            dk_ref[...] = dk_ref[...] * (scale / Q_SCALE)

    qspec = pl.BlockSpec((tq, HEAD_PAD), lambda h, i: (i, h))
    kspec = pl.BlockSpec((n_rows, HEAD_PAD), lambda h, i: (0, h))
    return pl.pallas_call(
        body, name="attn_bwd", grid=(N_HEADS, t_lat // tq),
        out_shape=[jax.ShapeDtypeStruct((t_lat, N_HEADS * HEAD_PAD), F32),
                   jax.ShapeDtypeStruct((n_rows, N_HEADS * HEAD_PAD), F32),
                   jax.ShapeDtypeStruct((n_rows, N_HEADS * HEAD_PAD), F32)],
        in_specs=[qspec, kspec, kspec, qspec, qspec, qspec], out_specs=[qspec, kspec, kspec],
        compiler_params=_cparams(("parallel", "arbitrary")),
    )(q, k, v, o, do, lse)


def adamw(name, w, g, m, v):
    r, ccols = w.shape
    tr = _best_div(r, 8, max(8, 262144 // ccols)) if r % 8 == 0 else r
    c1 = 1.0 - ADAM_B1 ** ADAM_STEP
    c2 = 1.0 - ADAM_B2 ** ADAM_STEP

    def body(w_ref, g_ref, m_ref, v_ref, d_ref, nm_ref, nv_ref):
        gv = g_ref[...]
        nm = ADAM_B1 * m_ref[...] + (1.0 - ADAM_B1) * gv
        nv = ADAM_B2 * v_ref[...] + (1.0 - ADAM_B2) * (gv * gv)
        d_ref[...] = -ADAM_LR * ((nm / c1) / (jnp.sqrt(nv / c2) + ADAM_EPS) + ADAM_WD * w_ref[...])
        nm_ref[...] = nm
        nv_ref[...] = nv

    spec = pl.BlockSpec((tr, ccols), lambda i: (i, 0))
    return pl.pallas_call(
        body, name=name, grid=(r // tr,),
        out_shape=[jax.ShapeDtypeStruct((r, ccols), F32)] * 3,
        in_specs=[spec] * 4, out_specs=[spec] * 3,
        compiler_params=_cparams(("parallel",)),
    )(w, g, m, v)


def adamw_many(name, ws, gs, ms, vs):
    n = len(ws)
    c1 = 1.0 - ADAM_B1 ** ADAM_STEP
    c2 = 1.0 - ADAM_B2 ** ADAM_STEP

    def body(*refs):
        for i in range(n):
            w_ref, g_ref, m_ref, v_ref = (refs[k * n + i] for k in range(4))
            d_ref, nm_ref, nv_ref = (refs[(4 + k) * n + i] for k in range(3))
            gv = g_ref[...]
            nm = ADAM_B1 * m_ref[...] + (1.0 - ADAM_B1) * gv
            nv = ADAM_B2 * v_ref[...] + (1.0 - ADAM_B2) * (gv * gv)
            d_ref[...] = -ADAM_LR * ((nm / c1) / (jnp.sqrt(nv / c2) + ADAM_EPS) + ADAM_WD * w_ref[...])
            nm_ref[...] = nm
            nv_ref[...] = nv

    vmem = pl.BlockSpec(memory_space=pltpu.VMEM)
    res = pl.pallas_call(
        body, name=name,
        out_shape=[jax.ShapeDtypeStruct(w.shape, F32) for w in ws] * 3,
        in_specs=[vmem] * (4 * n), out_specs=[vmem] * (3 * n),
        compiler_params=_cparams(),
    )(*ws, *gs, *ms, *vs)
    return [tuple(res[k * n + i] for k in range(3)) for i in range(n)]


def _flat(parts, dtype, row_mult):
    v = jnp.concatenate([p.reshape(-1).astype(dtype) for p in parts])
    quantum = row_mult * FLAT_C
    total = -(-v.shape[0] // quantum) * quantum
    return jnp.pad(v, (0, total - v.shape[0])).reshape(total // FLAT_C, FLAT_C)


def _unflat(flat, shapes):
    v = flat.reshape(-1)
    out, at = [], 0
    for s in shapes:
        n = math.prod(s)
        out.append(v[at:at + n].reshape(s))
        at += n
    return out


def _gathered_to_full(name, g):
    k = g.shape[1]
    return jnp.transpose(g, (1, 0, 2)).reshape(k, N_DEV * g.shape[2])


def _full_to_chunks(name, full):
    k, n = full.shape
    return jnp.transpose(full.reshape(k, N_DEV, n // N_DEV), (1, 0, 2)).reshape(N_DEV, -1)


def _shard_to_rb(name, w):
    return w if name in ROW_SHARDED else w.T


def _rb_to_shard(name, g):
    return g if name in ROW_SHARDED else g.T


def _rb_from_gathered(name, g):
    cols = g.shape[2]
    if name == 'w_in':
        z = lambda k: jnp.zeros((k, cols), g.dtype)
        full = g.reshape(N_DEV * g.shape[1], cols)
        return jnp.concatenate([full[:Z_KR], z(QK_NOPE), full[Z_KR:Z_KR + QK_ROPE], z(HEAD_PAD - QK_DIM),
                                full[Z_KR + QK_ROPE:]], axis=0)
    if name == 'w_uq':
        return jnp.pad(g, ((0, 0), (0, HEAD_PAD - QK_DIM), (0, 0))).reshape(N_HEADS * HEAD_PAD, cols)
    if name == 'w_ukv':
        pad = lambda t: jnp.pad(t, ((0, 0), (0, HEAD_PAD - t.shape[1]), (0, 0))).reshape(N_HEADS * HEAD_PAD, cols)
        return jnp.concatenate([pad(g[:, :QK_NOPE]), pad(g[:, QK_NOPE:])], axis=0)
    if name == 'w_o_attn':
        full = g.reshape(D, N_HEADS, V_HEAD)
        return jnp.pad(full, ((0, 0), (0, 0), (0, HEAD_PAD - V_HEAD))).reshape(D, N_HEADS * HEAD_PAD)
    return g.reshape(N_DEV * g.shape[1], cols)


def _chunks_from_rb_grad(name, g):
    cols = g.shape[1]
    if name == 'w_in':
        full = jnp.concatenate([g[:Z_KR], g[Z_KR + QK_NOPE:Z_KR + QK_DIM], g[Z_XB:]], axis=0)
        return full.reshape(N_DEV, -1, cols)
    if name == 'w_uq':
        return g.reshape(N_HEADS, HEAD_PAD, cols)[:, :QK_DIM]
    if name == 'w_ukv':
        half = N_HEADS * HEAD_PAD
        gk = g[:half].reshape(N_HEADS, HEAD_PAD, cols)[:, :QK_NOPE]
        gv = g[half:].reshape(N_HEADS, HEAD_PAD, cols)[:, :V_HEAD]
        return jnp.concatenate([gk, gv], axis=1)
    if name == 'w_o_attn':
        full = g.reshape(D, N_HEADS, HEAD_PAD)[:, :, :V_HEAD].reshape(D, N_HEADS * V_HEAD)
        return full.reshape(N_DEV, D // N_DEV, N_HEADS * V_HEAD)
    return g.reshape(N_DEV, -1, cols)


def local_step(x, ctx, target, mod_l, mod_c, wt, on_grad=None, arrive=None):
    t_lat, n_ctx = x.shape[0], ctx.shape[0]
    n = t_lat + n_ctx
    tm = _pick(math.gcd(t_lat, n), (256, 128))
    tq = _pick(t_lat, (256, 128))
    row = lambda v: v.reshape(1, -1).astype(F32)
    two = lambda a, b: jnp.stack([a, b]).astype(F32)
    sh1_l, sc1_l, g1_l, sh2_l, sc2_l, g2_l = jnp.split(mod_l, 6)
    sh1_c, sc1_c = jnp.split(mod_c, 6)[:2]
    sc1, sh1 = two(sc1_l, sc1_c), two(sh1_l, sh1_c)
    g1, g2, sc2, sh2 = row(g1_l), row(g2_l), row(sc2_l), row(sh2_l)
    norm1_g, norm2_g, final_g = row(wt['norm1_g']), row(wt['norm2_g']), row(wt['final_g'])
    q_g, kv_g, b_gate = row(wt['q_norm_g']), row(wt['kv_norm_g']), row(wt['b_gate'])
    wt = dict(wt)
    pending = []

    def sent():
        tokens = list(pending)
        pending.clear()
        return tokens

    def need(names, after):
        if arrive is not None:
            got = arrive(names, after)
            if '_token' in got:
                pending.append(got.pop('_token'))
            wt.update(got)
        return [wt[n] for n in names]
    lru_w_a = wt['lru_w_a'].reshape(2 * LRU_BLOCKS, LRU_BW, LRU_BW).astype(BF16)
    lru_w_x = wt['lru_w_x'].reshape(2 * LRU_BLOCKS, LRU_BW, LRU_BW).astype(BF16)
    b_a, b_x, lam = wt['lru_b_a'], wt['lru_b_x'], wt['lru_lambda']
    sp = jnp.logaddexp(-lam, 0.0)
    c_tab, s1_tab, s2_tab = _rope_tables(t_lat, n)
    rw = functools.partial(rowwise, n_rows=n, t_lat=t_lat, tm=tm)
    rw_lat = functools.partial(rowwise, n_rows=t_lat, t_lat=t_lat, tm=tm)

    xs = jnp.concatenate([x, ctx], axis=0)

    def f_norm1(is_ctx, rows, params):
        (xv,), (g, sc, sh) = rows, params
        return [_norm_mod(xv, g, _sel(is_ctx, sc), _sel(is_ctx, sh))], []

    (h,), _ = rw("norm1", f_norm1, [(xs, 0, D)], [norm1_g, sc1, sh1], [(D, BF16)], [])
    (w_in_t,) = need(('w_in',), h)
    z = matmul("w_in", h, w_in_t, 'nt', BF16, after=sent())
    w_uq_t, w_ukv_t, w_o_lru = need(('w_uq', 'w_ukv', 'w_o_lru'), z)

    def f_qkv_norm(is_ctx, rows, params):
        (ql, kvl), (gq, gkv) = rows, params
        return [_rms(ql, gq), _rms(kvl, gkv)], []

    (qn, kvn), _ = rw("qkv_norm", f_qkv_norm, [(z, Z_Q, Q_RANK), (z, Z_KV, KV_RANK)], [q_g, kv_g],
                      [(Q_RANK, BF16), (KV_RANK, BF16)], [])
    qp = matmul("w_uq", qn, w_uq_t, 'nt', F32)
    kvp = matmul("w_ukv", kvn, w_ukv_t, 'nt', F32)

    def f_rope(is_ctx, rows, params):
        qv, kk, vv, kr, c, s1, s2 = rows
        krr = _rope(kr, c, s1, s2)
        qo = jnp.concatenate([_rope(qh, c, s1, s2) for qh in _heads(qv)], axis=1) * Q_SCALE
        ko = jnp.concatenate([kh + krr for kh in _heads(kk)], axis=1)
        return [qo, ko, vv], []

    hp = N_HEADS * HEAD_PAD
    (qr, kr_, vr), _ = rw("rope", f_rope,
                          [(qp, 0, hp), (kvp, 0, hp), (kvp, hp, hp), (z, Z_KR, HEAD_PAD), (c_tab, 0, HEAD_PAD),
                           (s1_tab, 0, HEAD_PAD), (s2_tab, 0, HEAD_PAD)], [], [(hp, BF16)] * 3, [])
    attn, lse = attn_fwd(qr, kr_, vr, t_lat, n, tq)

    xc = conv_fwd("lru_conv", z, Z_XB, LRU_W, wt['lru_conv_w'], row(wt['lru_conv_b']), 2, n, t_lat, F32)
    a_f, u_f, a_b, u_b = gates_fwd(xc, lru_w_a, lru_w_x, b_a, b_x, sp, n, t_lat, tm)
    h_f, hp_f = scan_fwd("scan_f", a_f, u_f, 'f', n, t_lat)
    h_b, hp_b = scan_fwd("scan_b", a_b, u_b, 'b', n, t_lat)

    def f_lru_out(is_ctx, rows, params):
        hf, hb, yb = rows
        return [(hf + hb) * _gelu(yb)], []

    (ybin,), _ = rw_lat("lru_out", f_lru_out, [(h_f, 0, LRU_W), (h_b, 0, LRU_W), (z, Z_YB, LRU_W)], [],
                        [(LRU_W, BF16)], [])
    w_o_attn_t, w_out, w_up_t, w_down = need(('w_o_attn', 'w_out', 'w_up', 'w_down'), attn)
    y_a = matmul("w_o_attn", attn, w_o_attn_t, 'nt', F32)
    y_b = matmul("w_o_lru", ybin, w_o_lru, 'nn', F32)

    def _merge(ya, yb, gl, bg):
        gates = _sigmoid(gl + bg)
        return gates[:, :D] * ya + gates[:, D:] * yb

    def f_merge(is_ctx, rows, params):
        (ya, yb, gl), (bg,) = rows, params
        return [_merge(ya, yb, gl, bg)], []

    (mrg,), _ = rw_lat("merge", f_merge, [(y_a, 0, D), (y_b, 0, D), (z, Z_GL, 2 * D)], [b_gate], [(D, BF16)], [])
    o = matmul("w_out", mrg, w_out, 'nn', F32)

    def _res_norm2(xv, ov, g1v, g, sc, sh):
        x1 = xv + g1v * ov
        return x1, _norm_mod(x1, g, sc, sh)

    def f_norm2(is_ctx, rows, params):
        (xv, ov), (g1v, g, sc, sh) = rows, params
        x1, h2v = _res_norm2(xv, ov, g1v, g, sc, sh)
        return [x1, h2v], []

    (x1, h2), _ = rw_lat("norm2", f_norm2, [(x, 0, D), (o, 0, D)], [g1, norm2_g, sc2, sh2], [(D, F32), (D, BF16)], [])
    u = matmul("w_up", h2, w_up_t, 'nt', BF16)
    ac = conv_fwd("ffn_conv", u, 0, FFN, wt['ffn_conv_w'], row(wt['ffn_conv_b']), 1, t_lat, t_lat, BF16)

    def f_ffn_act(is_ctx, rows, params):
        acv, gv = rows
        return [_silu(acv) * gv], []

    (f,), _ = rw_lat("ffn_act", f_ffn_act, [(ac, 0, FFN), (u, FFN, FFN)], [], [(FFN, BF16)], [])
    dn = matmul("w_down", f, w_down, 'nn', F32)

    def _tile_loss(x1v, dv, g2v, fg, tgt):
        y = _rms(x1v + g2v * dv, fg)
        e = y - tgt
        return 0.5 * jnp.sum(jnp.mean(e * e, axis=-1, keepdims=True), axis=0, keepdims=True)

    def f_final(is_ctx, rows, params):
        (x1v, dv, tgt), (g2v, fg) = rows, params
        lv, vjp = jax.vjp(lambda a, b, c, d: _tile_loss(a, b, c, d, tgt), x1v, dv, g2v, fg)
        dx2, dd, dg2, dfg = vjp(jnp.ones((1, 1), F32))
        return [dx2, dd], [dg2, dfg, jnp.broadcast_to(lv, (1, 128))]

    (dx2, dd), (dg2, dfinal_g, loss_v) = rw_lat("final", f_final, [(x1, 0, D), (dn, 0, D), (target, 0, D)],
                                                [g2, final_g], [(D, F32), (D, BF16)], [(1, D), (1, D), (1, 128)])
    loss = loss_v[0, 0]

    grads = {'final_g': dfinal_g}

    def put(name, g):
        grads[name] = g
        if on_grad is not None:
            pending.append(on_grad(name, g))
    df = matmul("d_f", dd, w_down, 'nt', BF16)
    put('w_down', matmul("g_w_down", f, dd, 'tn', BF16))

    def b_ffn_act(is_ctx, rows, params):
        acv, gv, dfv = rows
        _, vjp = jax.vjp(lambda a, g: _silu(a) * g, acv, gv)
        dac, dg = vjp(dfv)
        return [dac, dg], []

    (dac, dgate), _ = rw_lat("ffn_act_bwd", b_ffn_act, [(ac, 0, FFN), (u, FFN, FFN), (df, 0, FFN)], [],
                             [(FFN, BF16), (FFN, BF16)], [])
    da, grads['ffn_conv_w'], grads['ffn_conv_b'] = conv_bwd("ffn_conv_bwd", dac, u, 0, FFN, wt['ffn_conv_w'], 1,
                                                            t_lat, t_lat)
    du = jnp.concatenate([da, dgate], axis=1)
    dh2 = matmul("d_h2", du, w_up_t, 'nn', F32, after=sent())
    put('w_up', matmul("g_w_up", du, h2, 'tn', BF16))

    def b_norm2(is_ctx, rows, params):
        (xv, ov, dh2v, dx2v), (g1v, g, sc, sh) = rows, params
        _, vjp = jax.vjp(_res_norm2, xv, ov, g1v, g, sc, sh)
        dx, do, dg1v, dg, dsc, dsh = vjp((dx2v, dh2v))
        return [dx, do], [dg1v, dg, dsc, dsh]

    (dx_res, do), (dg1, dnorm2_g, dsc2, dsh2) = rw_lat(
        "norm2_bwd", b_norm2, [(x, 0, D), (o, 0, D), (dh2, 0, D), (dx2, 0, D)], [g1, norm2_g, sc2, sh2],
        [(D, F32), (D, BF16)], [(1, D)] * 4)
    grads['norm2_g'] = dnorm2_g
    dmrg = matmul("d_merge", do, w_out, 'nt', F32, after=sent())
    put('w_out', matmul("g_w_out", mrg, do, 'tn', BF16))

    def b_merge(is_ctx, rows, params):
        (ya, yb, gl, dm), (bg,) = rows, params
        _, vjp = jax.vjp(_merge, ya, yb, gl, bg)
        dya, dyb, dgl, dbg = vjp(dm)
        return [dya, dyb, dgl], [dbg]

    (dy_a, dy_b, dgl), (grads['b_gate'],) = rw_lat(
        "merge_bwd", b_merge, [(y_a, 0, D), (y_b, 0, D), (z, Z_GL, 2 * D), (dmrg, 0, D)], [b_gate],
        [(D, BF16), (D, BF16), (2 * D, BF16)], [(1, 2 * D)])
    dattn = matmul("d_attn", dy_a, w_o_attn_t, 'nn', BF16, after=sent())
    put('w_o_attn', matmul("g_w_o_attn", dy_a, attn, 'tn', BF16))
    dybin = matmul("d_lru_out", dy_b, w_o_lru, 'nt', F32, after=sent())
    put('w_o_lru', matmul("g_w_o_lru", ybin, dy_b, 'tn', BF16))

    def b_lru_out(is_ctx, rows, params):
        hf, hb, yb, dyv = rows
        _, vjp = jax.vjp(lambda s, y: s * _gelu(y), hf + hb, yb)
        dh, dyb = vjp(dyv)
        return [dh, dyb], []

    (dh_lru, dyb), _ = rw_lat("lru_out_bwd", b_lru_out,
                              [(h_f, 0, LRU_W), (h_b, 0, LRU_W), (z, Z_YB, LRU_W), (dybin, 0, LRU_W)], [],
                              [(LRU_W, F32), (LRU_W, BF16)], [])
    du_f, da_f = scan_adj("scan_f_adj", a_f, dh_lru, hp_f, 'f', n, t_lat)
    du_b, da_b = scan_adj("scan_b_adj", a_b, dh_lru, hp_b, 'b', n, t_lat)
    dxc, (dw_a, dw_x, db_a, db_x, dsp) = gates_bwd(xc, da_f, du_f, da_b, du_b, lru_w_a, lru_w_x, b_a, b_x, sp,
                                                   n, t_lat, tm)
    put('lru_w_a', dw_a.reshape(2 * LRU_BLOCKS * LRU_BW, LRU_BW).astype(BF16))
    put('lru_w_x', dw_x.reshape(2 * LRU_BLOCKS * LRU_BW, LRU_BW).astype(BF16))
    grads['lru_b_a'], grads['lru_b_x'] = db_a, db_x
    grads['lru_lambda'] = -dsp * _sigmoid(-lam)
    dxb, grads['lru_conv_w'], grads['lru_conv_b'] = conv_bwd("lru_conv_bwd", dxc, z, Z_XB, LRU_W, wt['lru_conv_w'],
                                                             2, n, t_lat)

    dq, dk, dv = attn_bwd(qr, kr_, vr, attn, dattn, lse, t_lat, n, tq)

    def b_rope(is_ctx, rows, params):
        dqv, dkv, dvv, c, s1, s2 = rows
        live = jnp.where(is_ctx, 0.0, 1.0)
        dqo = jnp.concatenate([_rope_t(dqh, c, s1, s2) for dqh in _heads(dqv)], axis=1) * live
        dkh = _heads(dkv)
        dkr = dkh[0]
        for t in dkh[1:]:
            dkr = dkr + t
        lanes = lax.broadcasted_iota(jnp.int32, dkr.shape, 1)
        dkr = jnp.where((lanes >= QK_NOPE) & (lanes < QK_DIM), _rope_t(dkr, c, s1, s2), 0.0)
        return [dqo, jnp.concatenate([dkv, dvv], axis=1), dkr], []

    (dqp, dkvp, dkr), _ = rw("rope_bwd", b_rope,
                             [(dq, 0, hp), (dk, 0, hp), (dv, 0, hp), (c_tab, 0, HEAD_PAD), (s1_tab, 0, HEAD_PAD),
                              (s2_tab, 0, HEAD_PAD)], [], [(hp, BF16), (2 * hp, BF16), (HEAD_PAD, BF16)], [])
    dqn = matmul("d_qn", dqp, w_uq_t, 'nn', F32, after=sent())
    put('w_uq', matmul("g_w_uq", dqp, qn, 'tn', BF16))
    dkvn = matmul("d_kvn", dkvp, w_ukv_t, 'nn', F32, after=sent())
    put('w_ukv', matmul("g_w_ukv", dkvp, kvn, 'tn', BF16))

    def b_qkv_norm(is_ctx, rows, params):
        (ql, kvl, dqv, dkvv), (gq, gkv) = rows, params
        _, vjp_q = jax.vjp(_rms, ql, gq)
        _, vjp_kv = jax.vjp(_rms, kvl, gkv)
        dql, dgq = vjp_q(dqv)
        dkvl, dgkv = vjp_kv(dkvv)
        return [dql, dkvl], [dgq, dgkv]

    (dq_lat, dkv_lat), (grads['q_norm_g'], grads['kv_norm_g']) = rw(
        "qkv_norm_bwd", b_qkv_norm, [(z, Z_Q, Q_RANK), (z, Z_KV, KV_RANK), (dqn, 0, Q_RANK), (dkvn, 0, KV_RANK)],
        [q_g, kv_g], [(Q_RANK, BF16), (KV_RANK, BF16)], [(1, Q_RANK), (1, KV_RANK)])
    pad_ctx = lambda t: jnp.pad(t, ((0, n_ctx), (0, 0)))
    dz = jnp.concatenate([dq_lat, dkv_lat, dkr, dxb, pad_ctx(dyb), pad_ctx(dgl)], axis=1)
    put('w_in', matmul("g_w_in", dz, h, 'tn', BF16))
    dh = matmul("d_h", dz, w_in_t, 'nn', F32, after=sent())

    def b_norm1(is_ctx, rows, params):
        (xv, dhv, dxr), (g, sc, sh) = rows, params
        scv, shv = _sel(is_ctx, sc), _sel(is_ctx, sh)
        _, vjp = jax.vjp(_norm_mod, xv, g, scv, shv)
        dx, dg, dsc, dsh = vjp(dhv)
        return [dx + dxr], [dg, _seg_acc(is_ctx, dsc), _seg_acc(is_ctx, dsh)]

    (dxs,), (grads['norm1_g'], dsc1, dsh1) = rw("norm1_bwd", b_norm1, [(xs, 0, D), (dh, 0, D), (dx_res, 0, D)],
                                                [norm1_g, sc1, sh1], [(D, F32)], [(1, D), (2, D), (2, D)])
    grad_x = dxs[:t_lat]
    zero = jnp.zeros((D,), F32)
    dmod_l = jnp.concatenate([dsh1[0], dsc1[0], dg1[0], dsh2[0], dsc2[0], dg2[0]])
    dmod_c = jnp.concatenate([dsh1[1], dsc1[1], zero, zero, zero, zero])
    return loss, grad_x, grads, dmod_l, dmod_c


def kernel(x, c, ctx, c_ctx, w_mod, b_mod, norm1_g, w_in, b_gate, q_norm_g, kv_norm_g, w_uq, w_ukv, w_o_attn, lru_conv_w, lru_conv_b, lru_w_a, lru_b_a, lru_w_x, lru_b_x, lru_lambda, w_o_lru, w_out, norm2_g, w_up, ffn_conv_w, ffn_conv_b, w_down, final_g, loss_target, m_c_ctx, m_w_mod, m_b_mod, m_norm1_g, m_w_in, m_b_gate, m_q_norm_g, m_kv_norm_g, m_w_uq, m_w_ukv, m_w_o_attn, m_lru_conv_w, m_lru_conv_b, m_lru_w_a, m_lru_b_a, m_lru_w_x, m_lru_b_x, m_lru_lambda, m_w_o_lru, m_w_out, m_norm2_g, m_w_up, m_ffn_conv_w, m_ffn_conv_b, m_w_down, m_final_g, v_c_ctx, v_w_mod, v_b_mod, v_norm1_g, v_w_in, v_b_gate, v_q_norm_g, v_kv_norm_g, v_w_uq, v_w_ukv, v_w_o_attn, v_lru_conv_w, v_lru_conv_b, v_lru_w_a, v_lru_b_a, v_lru_w_x, v_lru_b_x, v_lru_lambda, v_w_o_lru, v_w_out, v_norm2_g, v_w_up, v_ffn_conv_w, v_ffn_conv_b, v_w_down, v_final_g):
    given = dict(locals())
    strip = lambda name, a: a if name in ('c_ctx', 'final_g') else a[0]
    wsh = {n: strip(n, given[n]) for n in WEIGHTS}
    msh = {n: strip(n, given['m_' + n]) for n in WEIGHTS}
    vsh = {n: strip(n, given['v_' + n]) for n in WEIGHTS}
    me = _my_index()

    small = _flat([c[0]] + [wsh[n] for n in SMALL_F32], F32, 8)
    small_all = all_gather("gather_small", small).reshape(N_DEV, -1)
    c_all = small_all[:, :D]
    full, at = {}, D
    for n in SMALL_F32:
        cnt = math.prod(wsh[n].shape)
        full[n] = _gathered_to_full(n, small_all[:, at:at + cnt].reshape((N_DEV,) + wsh[n].shape))
        at += cnt

    cond = jnp.concatenate([c_all, c_ctx[None], jnp.zeros((7, D), F32)], axis=0)
    sil = cond * jax.nn.sigmoid(cond)
    mod_cols = matmul("mod_proj", sil, wsh['w_mod'], 'nn', F32)
    mod_all = all_gather("gather_mod", mod_cols)
    mod_all = jnp.transpose(mod_all, (1, 0, 2)).reshape(16, 6 * D) + b_mod[0][None]
    mod_l = lax.dynamic_index_in_dim(mod_all, me, axis=0, keepdims=False)
    mod_c = mod_all[N_DEV]

    rb_shards = {n: _shard_to_rb(n, wsh[n]).astype(BF16) for n in BIG_BF16}
    (w_in_blocks,) = all_gather_multi("gather_w_in", [rb_shards['w_in']])
    later = [n for n in BIG_BF16 if n != 'w_in']
    weights_started, weights_sent = exchange_start("weights_send", 'gather', [rb_shards[n] for n in later],
                                                   after=[w_in_blocks, mod_all])
    for n in REPLICATED:
        if n not in ('c_ctx', 'b_mod'):
            full[n] = wsh[n]

    def arrive(names, after):
        if names == ('w_in',):
            return {'w_in': _rb_from_gathered('w_in', w_in_blocks), '_token': weights_sent}
        picked = [later.index(n) for n in names]
        lands = exchange_wait("weights_wait_" + names[0], 'gather',
                              tuple([part[i] for i in picked] for part in weights_started), after)
        return {n: _rb_from_gathered(n, lax.dynamic_update_slice_in_dim(land, rb_shards[n][None], me, axis=0))
                for n, land in zip(names, lands)}

    in_flight = {}

    def on_grad(n, g):
        chunks = _chunks_from_rb_grad(n, g)
        own = lax.dynamic_index_in_dim(chunks, me, axis=0, keepdims=True)
        started, token = exchange_start("grad_send_" + n, 'scatter', [chunks])
        in_flight[n] = (own, started)
        return token

    loss, grad_x, grads, dmod_l, dmod_c = local_step(x[0], ctx[0], loss_target[0], mod_l, mod_c, full, on_grad,
                                                     arrive)
    loss = lax.psum(loss, ("x", "y", "c"))

    dmod = jnp.stack([dmod_l, dmod_c]).reshape(2 * 6 * D // FLAT_C, FLAT_C)
    dm = all_gather("gather_dmod", dmod).reshape(N_DEV, 2, 6 * D)
    dmod_c_tot = dm[0, 1]
    for p in range(1, N_DEV):
        dmod_c_tot = dmod_c_tot + dm[p, 1]
    dm16 = jnp.concatenate([dm[:, 0], dmod_c_tot[None], jnp.zeros((7, 6 * D), F32)], axis=0)
    ncol = 6 * D // N_DEV
    dm16_cols = lax.dynamic_slice_in_dim(dm16.reshape(16, N_DEV, ncol), me, 1, axis=1)[:, 0]
    grad_w_mod = matmul("g_w_mod", sil, dm16_cols, 'tn', F32)
    dsil = matmul("d_cond", dm16_cols, wsh['w_mod'], 'nt', F32)
    sg = jax.nn.sigmoid(c_ctx)
    grads['c_ctx'] = dsil[N_DEV] * (sg * (1.0 + c_ctx * (1.0 - sg)))
    grads['b_mod'] = dmod_l + dmod_c

    g_final = {'w_mod': grad_w_mod}
    reduced = {}
    for n in BIG_BF16 + ['lru_w_a', 'lru_w_x']:
        own, started = in_flight[n]
        (land,) = exchange_wait("grad_wait_" + n, 'scatter', started, dm)
        reduced[n] = sum_slots("sum_" + n, lax.dynamic_update_slice_in_dim(land, own, me, axis=0))
    for n in BIG_BF16:
        g_final[n] = _rb_to_shard(n, reduced[n])

    small_names = SMALL_F32 + [n for n in REPLICATED if n not in ('lru_w_a', 'lru_w_x')]
    partials = _flat([grads[n] for n in small_names], F32, 8)
    parts_all, a_all, x_all = all_gather_multi("gather_small_grads", [partials, reduced['lru_w_a'], reduced['lru_w_x']])
    small_sum = sum_slots("sum_small", parts_all).reshape(-1)
    g_final['lru_w_a'], g_final['lru_w_x'] = a_all.reshape(wsh['lru_w_a'].shape), x_all.reshape(wsh['lru_w_x'].shape)
    at = 0
    for n in small_names:
        cnt = math.prod(full[n].shape) if n in SMALL_F32 else math.prod(wsh[n].shape)
        g = small_sum[at:at + cnt]
        if n in SMALL_F32:
            k = full[n].shape[0]
            g = lax.dynamic_index_in_dim(g.reshape(k, N_DEV, -1), me, axis=1, keepdims=False)
        g_final[n] = g.reshape(wsh[n].shape)
        at += cnt

    stepped = {n: adamw("adamw_" + n, wsh[n], g_final[n], msh[n], vsh[n]) for n in ['w_mod'] + BIG_BF16}
    rest = [n for n in WEIGHTS if n not in stepped]
    as2d = lambda a: a.reshape(-1, a.shape[-1])
    rest_out = adamw_many("adamw_small", *[[as2d(d[n]) for n in rest] for d in (wsh, g_final, msh, vsh)])
    stepped.update(zip(rest, rest_out))
    shaped = lambda n, a: a.reshape(given[n].shape)
    return (loss, grad_x[None],
            *[shaped(n, g_final[n]) for n in WEIGHTS],
            *[shaped(n, stepped[n][k]) for k in range(3) for n in WEIGHTS])
```

```python
import functools
import math

import jax
import jax.numpy as jnp
from jax import lax
from jax.experimental import pallas as pl
from jax.experimental.pallas import tpu as pltpu

F32 = jnp.float32
BF16 = jnp.bfloat16
MESH = pl.DeviceIdType.MESH

N_DEV = 8
D = 1024
N_HEADS = 8
HEAD_PAD = 128
QK_NOPE, QK_ROPE, V_HEAD = 64, 32, 64
QK_DIM = QK_NOPE + QK_ROPE
Q_RANK, KV_RANK = 384, 256
LRU_W, LRU_BLOCKS, LRU_BW = 1280, 10, 128
FFN = 2816
GRID_W = 64
ROPE_BASE = 10000.0
LRU_C = 8.0
EPS = 1e-6
Z_Q, Z_KV, Z_KR, Z_XB, Z_YB, Z_GL, Z_END = 0, 384, 640, 768, 2048, 3328, 5376
ADAM_LR, ADAM_B1, ADAM_B2, ADAM_EPS, ADAM_WD, ADAM_STEP = 0.001, 0.9, 0.999, 1e-08, 0.01, 10

VMEM_LIMIT = 52 * 1024 * 1024
FLAT_C = 512
BIG_ROWS = 256

WEIGHTS = ['c_ctx', 'w_mod', 'b_mod', 'norm1_g', 'w_in', 'b_gate', 'q_norm_g', 'kv_norm_g', 'w_uq', 'w_ukv',
           'w_o_attn', 'lru_conv_w', 'lru_conv_b', 'lru_w_a', 'lru_b_a', 'lru_w_x', 'lru_b_x', 'lru_lambda',
           'w_o_lru', 'w_out', 'norm2_g', 'w_up', 'ffn_conv_w', 'ffn_conv_b', 'w_down', 'final_g']
COL_SHARDED = ['w_in', 'w_uq', 'w_ukv', 'w_o_attn', 'lru_conv_w', 'lru_b_a', 'lru_b_x', 'lru_lambda', 'w_up',
               'ffn_conv_w']
ROW_SHARDED = ['w_o_lru', 'w_out', 'w_down']
BIG_BF16 = ['w_in', 'w_uq', 'w_ukv', 'w_o_attn', 'w_o_lru', 'w_out', 'w_up', 'w_down']
SMALL_F32 = ['lru_conv_w', 'lru_b_a', 'lru_b_x', 'lru_lambda', 'ffn_conv_w']
SHARDED = BIG_BF16 + SMALL_F32
REPLICATED = ['c_ctx', 'b_mod', 'norm1_g', 'b_gate', 'q_norm_g', 'kv_norm_g', 'lru_conv_b', 'lru_w_a', 'lru_w_x',
              'norm2_g', 'ffn_conv_b', 'final_g']


def _cparams(sem=None):
    return pltpu.CompilerParams(dimension_semantics=sem, vmem_limit_bytes=VMEM_LIMIT)


def _pick(n, cands):
    for c in cands:
        if c <= n and n % c == 0:
            return c
    return n


def _best_div(n, mult, cap):
    best = mult
    for d in range(mult, min(n, cap) + 1, mult):
        if n % d == 0:
            best = d
    return best


ROW_TILES = (1088, 1024, 544, 512, 256, 128, 64, 32, 16, 8)
LANE_TILES = (1408, 1024, 896, 768, 640, 512, 384, 256, 128)


def _my_pos():
    return lax.axis_index("x"), lax.axis_index("y"), lax.axis_index("c")


def _my_index():
    x, y, c = _my_pos()
    return 4 * x + 2 * y + c


def all_gather_multi(name, shards):
    n_arr = len(shards)
    arrays = range(n_arr)

    def body(*refs):
        x_refs, out_refs = refs[:n_arr], refs[n_arr:2 * n_arr]
        send_sems, recv_sems, local_sems = refs[2 * n_arr:]
        x, y, c = _my_pos()
        me, sibling = (x, y, c), (x, y, 1 - c)
        chips = [(1 - x, y), (x, 1 - y), (1 - x, 1 - y)]

        def slot(a, px, py, pc):
            return out_refs[a].at[4 * px + 2 * py + pc]

        def copy(a, k, block, to, src=None):
            return pltpu.make_async_remote_copy(
                src_ref=slot(a, *block) if src is None else src, dst_ref=slot(a, *block),
                send_sem=send_sems.at[7 * a + k], recv_sem=recv_sems.at[7 * a + k], device_id=to,
                device_id_type=MESH)

        mine = [pltpu.make_async_copy(x_refs[a], slot(a, *me), local_sems.at[a]) for a in arrays]
        first = [copy(a, 1 + j, me, (*chip, c), src=x_refs[a]) for j, chip in enumerate(chips) for a in arrays]
        first += [copy(a, 0, me, sibling, src=x_refs[a]) for a in arrays]
        for cp in first + mine:
            cp.start()
        passed = []
        for j, chip in enumerate(chips):
            for a in arrays:
                copy(a, 1 + j, (*chip, c), me).wait_recv()
                passed.append(copy(a, 4 + j, (*chip, c), sibling))
                passed[-1].start()
        for a in arrays:
            copy(a, 0, sibling, me).wait_recv()
            for j, chip in enumerate(chips):
                copy(a, 4 + j, (*chip, 1 - c), me).wait_recv()
        for cp in first + passed:
            cp.wait_send()
        for cp in mine:
            cp.wait()

    hbm = pl.BlockSpec(memory_space=pl.ANY)
    return pl.pallas_call(
        body, name=name,
        out_shape=[jax.ShapeDtypeStruct((N_DEV,) + s.shape, s.dtype) for s in shards],
        in_specs=[hbm] * n_arr, out_specs=[hbm] * n_arr,
        scratch_shapes=[pltpu.SemaphoreType.DMA((7 * n_arr,)), pltpu.SemaphoreType.DMA((7 * n_arr,)),
                        pltpu.SemaphoreType.DMA((n_arr,))],
    )(*shards)


def all_gather(name, shard):
    return all_gather_multi(name, [shard])[0]


def all_to_all_multi(name, chunk_arrays):
    n_arr = len(chunk_arrays)
    arrays = range(n_arr)

    def body(*refs):
        x_refs, out_refs = refs[:n_arr], refs[n_arr:2 * n_arr]
        send_sems, recv_sems, local_sems = refs[2 * n_arr:]
        x, y, c = _my_pos()
        me = 4 * x + 2 * y + c
        mine = [pltpu.make_async_copy(x_refs[a].at[me], out_refs[a].at[me], local_sems.at[a]) for a in arrays]
        sends, arrivals = [], []
        for rel in (6, 4, 2, 7, 5, 3, 1):
            dx, dy, dc = (rel >> 2) & 1, (rel >> 1) & 1, rel & 1
            px, py, pc = x ^ dx, y ^ dy, c ^ dc
            peer = 4 * px + 2 * py + pc
            for a in arrays:
                k = 7 * a + rel - 1
                sends.append(pltpu.make_async_remote_copy(
                    src_ref=x_refs[a].at[peer], dst_ref=out_refs[a].at[me],
                    send_sem=send_sems.at[k], recv_sem=recv_sems.at[k],
                    device_id=(px, py, pc), device_id_type=MESH))
                arrivals.append(pltpu.make_async_remote_copy(
                    src_ref=x_refs[a].at[peer], dst_ref=out_refs[a].at[peer],
                    send_sem=send_sems.at[k], recv_sem=recv_sems.at[k],
                    device_id=(x, y, c), device_id_type=MESH))
        for cp in sends + mine:
            cp.start()
        for cp in arrivals:
            cp.wait_recv()
        for cp in sends:
            cp.wait_send()
        for cp in mine:
            cp.wait()

    hbm = pl.BlockSpec(memory_space=pl.ANY)
    return pl.pallas_call(
        body, name=name,
        out_shape=[jax.ShapeDtypeStruct(s.shape, s.dtype) for s in chunk_arrays],
        in_specs=[hbm] * n_arr, out_specs=[hbm] * n_arr,
        scratch_shapes=[pltpu.SemaphoreType.DMA((7 * n_arr,)), pltpu.SemaphoreType.DMA((7 * n_arr,)),
                        pltpu.SemaphoreType.DMA((n_arr,))],
    )(*chunk_arrays)


def _peers():
    x, y, c = _my_pos()
    out = []
    for rel in (6, 4, 2, 7, 5, 3, 1):
        px, py, pc = x ^ ((rel >> 2) & 1), y ^ ((rel >> 1) & 1), c ^ (rel & 1)
        out.append((rel - 1, (px, py, pc), 4 * px + 2 * py + pc))
    return out


def _exchange_copies(mode, src_refs, land_refs, send_sems, recv_sems):
    x, y, c = _my_pos()
    me = 4 * x + 2 * y + c
    sends, arrivals = [], []
    for k, peer_pos, peer in _peers():
        for a, (src, land) in enumerate(zip(src_refs, land_refs)):
            piece = src.at[peer] if mode == 'scatter' else src
            sems = dict(send_sem=send_sems[a].at[k], recv_sem=recv_sems[a].at[k], device_id_type=MESH)
            sends.append(pltpu.make_async_remote_copy(src_ref=piece, dst_ref=land.at[me], device_id=peer_pos, **sems))
            arrivals.append(pltpu.make_async_remote_copy(src_ref=piece, dst_ref=land.at[peer], device_id=(x, y, c), **sems))
    return sends, arrivals


_HBM = pl.BlockSpec(memory_space=pltpu.HBM)
_SEM = pl.BlockSpec(memory_space=pltpu.SEMAPHORE)


def exchange_start(name, mode, arrays, after=()):
    n_arr, n_after = len(arrays), len(after)
    land_shapes = [a.shape if mode == 'scatter' else (N_DEV,) + a.shape for a in arrays]

    def body(*refs):
        src_refs, land_refs = refs[:n_arr], refs[n_arr:2 * n_arr]
        refs = refs[n_after:]
        send_sems, recv_sems = refs[2 * n_arr:3 * n_arr], refs[3 * n_arr:4 * n_arr]
        sends, _ = _exchange_copies(mode, src_refs, land_refs, send_sems, recv_sems)
        for cp in sends:
            cp.start()
        token = refs[-1]
        token[...] = jnp.zeros_like(token)

    sem = pltpu.SemaphoreType.DMA((N_DEV - 1,))
    res = pl.pallas_call(
        body, name=name,
        out_shape=[sem] * (2 * n_arr) + [pltpu.HBM(a.shape, a.dtype) for a in arrays]
        + [pltpu.HBM(s, a.dtype) for s, a in zip(land_shapes, arrays)] + [jax.ShapeDtypeStruct((8, 128), F32)],
        in_specs=[_HBM] * (2 * n_arr) + [pl.BlockSpec(memory_space=pl.ANY)] * n_after,
        out_specs=[_SEM] * (2 * n_arr) + [_HBM] * (2 * n_arr) + [pl.BlockSpec(memory_space=pltpu.VMEM)],
        input_output_aliases={i: 2 * n_arr + i for i in range(2 * n_arr)},
        compiler_params=pltpu.CompilerParams(has_side_effects=pltpu.SideEffectType.DATAFLOW_SIDE_EFFECTING),
    )(*[pltpu.with_memory_space_constraint(a, pltpu.HBM) for a in arrays],
      *[pltpu.with_memory_space_constraint(lax.empty(s, a.dtype), pltpu.HBM) for s, a in zip(land_shapes, arrays)],
      *after)
    return (res[:n_arr], res[n_arr:2 * n_arr], res[2 * n_arr:3 * n_arr], res[3 * n_arr:4 * n_arr]), res[-1]


def exchange_wait(name, mode, started, after):
    send_sems, recv_sems, thru, land = started
    n_arr = len(thru)

    def body(*refs):
        src_refs, land_refs = refs[:n_arr], refs[n_arr:2 * n_arr]
        s_sems, r_sems = refs[2 * n_arr:3 * n_arr], refs[3 * n_arr:4 * n_arr]
        sends, arrivals = _exchange_copies(mode, src_refs, land_refs, s_sems, r_sems)
        for cp in sends:
            cp.wait_send()
        for cp in arrivals:
            cp.wait_recv()

    res = pl.pallas_call(
        body, name=name,
        out_shape=[pltpu.HBM(a.shape, a.dtype) for a in thru] + [pltpu.HBM(a.shape, a.dtype) for a in land],
        in_specs=[_HBM] * (2 * n_arr) + [_SEM] * (2 * n_arr) + [pl.BlockSpec(memory_space=pl.ANY)],
        out_specs=[_HBM] * (2 * n_arr),
        input_output_aliases={i: i for i in range(2 * n_arr)},
        compiler_params=pltpu.CompilerParams(has_side_effects=pltpu.SideEffectType.DATAFLOW_SIDE_EFFECTING),
    )(*thru, *land, *send_sems, *recv_sems, after)
    return res[n_arr:]


def sum_slots(name, slots):
    _, r, ccols = slots.shape
    tc = _pick(ccols, (256, 128))

    def body(s_ref, o_ref):
        acc = s_ref[0].astype(F32)
        for p in range(1, N_DEV):
            acc = acc + s_ref[p].astype(F32)
        o_ref[...] = acc

    return pl.pallas_call(
        body, name=name, grid=(ccols // tc,),
        out_shape=jax.ShapeDtypeStruct((r, ccols), F32),
        in_specs=[pl.BlockSpec((N_DEV, r, tc), lambda j: (0, 0, j))],
        out_specs=pl.BlockSpec((r, tc), lambda j: (0, j)),
        compiler_params=_cparams(("parallel",)),
    )(slots)


def matmul(name, a, b, mode, out_dtype, tm=None, tn=None, tk=None, after=()):
    after = [t for t in after if t is not None]
    if mode == 'nn':
        (m, k), (k2, n) = a.shape, b.shape
    elif mode == 'nt':
        (m, k), (n, k2) = a.shape, b.shape
    else:
        (k, m), (k2, n) = a.shape, b.shape
    assert k == k2, (name, a.shape, b.shape, mode)
    if mode == 'tn':
        tm = tm or _pick(m, LANE_TILES)
        tk = tk or _pick(k, ROW_TILES)
    else:
        tm = tm or _pick(m, ROW_TILES)
        tk = tk or _pick(k, LANE_TILES)
    tn = tn or _pick(n, LANE_TILES)
    nk = k // tk
    if mode == 'nn':
        a_spec = pl.BlockSpec((tm, tk), lambda i, j, kk: (i, kk))
        b_spec = pl.BlockSpec((tk, tn), lambda i, j, kk: (kk, j))
        dn = (((1,), (0,)), ((), ()))
    elif mode == 'nt':
        a_spec = pl.BlockSpec((tm, tk), lambda i, j, kk: (i, kk))
        b_spec = pl.BlockSpec((tn, tk), lambda i, j, kk: (j, kk))
        dn = (((1,), (1,)), ((), ()))
    else:
        a_spec = pl.BlockSpec((tk, tm), lambda i, j, kk: (kk, i))
        b_spec = pl.BlockSpec((tk, tn), lambda i, j, kk: (kk, j))
        dn = (((0,), (0,)), ((), ()))

    def product(a_ref, b_ref):
        return lax.dot_general(a_ref[...].astype(BF16), b_ref[...].astype(BF16), dn, preferred_element_type=F32)

    n_after = len(after)

    def body_one(a_ref, b_ref, *rest):
        o_ref = rest[n_after]
        o_ref[...] = product(a_ref, b_ref).astype(o_ref.dtype)

    def body(a_ref, b_ref, *rest):
        o_ref, acc_ref = rest[n_after:]
        kk = pl.program_id(2)

        @pl.when(kk == 0)
        def _():
            acc_ref[...] = jnp.zeros_like(acc_ref)

        acc_ref[...] += product(a_ref, b_ref)

        @pl.when(kk == nk - 1)
        def _():
            o_ref[...] = acc_ref[...].astype(o_ref.dtype)

    return pl.pallas_call(
        body_one if nk == 1 else body, name=name, grid=(m // tm, n // tn, nk),
        out_shape=jax.ShapeDtypeStruct((m, n), out_dtype),
        in_specs=[a_spec, b_spec] + [pl.BlockSpec(memory_space=pl.ANY)] * n_after,
        out_specs=pl.BlockSpec((tm, tn), lambda i, j, kk: (i, j)),
        scratch_shapes=[] if nk == 1 else [pltpu.VMEM((tm, tn), F32)],
        compiler_params=_cparams(("parallel", "parallel", "arbitrary")),
    )(a, b, *after)


def rowwise(name, fn, rows, params, out_rows, out_accs, n_rows, t_lat, tm):
    nb = n_rows // tm
    in_specs, piece_counts = [], []
    operands = []
    for arr, off, width in rows:
        g = math.gcd(off, width) if off else width
        assert g % 128 == 0 or (off == 0 and width == arr.shape[1]), (name, off, width)
        cnt = width // g
        last = arr.shape[0] // tm - 1
        clamp = arr.shape[0] < n_rows
        for p in range(cnt):
            cb = off // g + p
            if clamp:
                in_specs.append(pl.BlockSpec((tm, g), lambda i, cb=cb, last=last: (jnp.minimum(i, last), cb)))
            else:
                in_specs.append(pl.BlockSpec((tm, g), lambda i, cb=cb: (i, cb)))
            operands.append(arr)
        piece_counts.append(cnt)
    for p in params:
        in_specs.append(pl.BlockSpec(p.shape, lambda i, nd=p.ndim: (0,) * nd))
        operands.append(p)
    n_in = sum(piece_counts)
    n_par = len(params)
    n_or = len(out_rows)
    out_shape = [jax.ShapeDtypeStruct((n_rows, w), dt) for w, dt in out_rows]
    out_shape += [jax.ShapeDtypeStruct(s, F32) for s in out_accs]
    out_specs = [pl.BlockSpec((tm, w), lambda i: (i, 0)) for w, _ in out_rows]
    out_specs += [pl.BlockSpec(s, lambda i, nd=len(s): (0,) * nd) for s in out_accs]

    def body(*refs):
        in_refs, par_refs = refs[:n_in], refs[n_in:n_in + n_par]
        orow_refs = refs[n_in + n_par:n_in + n_par + n_or]
        oacc_refs = refs[n_in + n_par + n_or:]
        i = pl.program_id(0)
        tiles, at = [], 0
        for cnt in piece_counts:
            parts = [in_refs[at + p][...].astype(F32) for p in range(cnt)]
            tiles.append(parts[0] if cnt == 1 else jnp.concatenate(parts, axis=1))
            at += cnt
        is_ctx = i * tm >= t_lat
        outs, accs = fn(is_ctx, tiles, [p[...] for p in par_refs])
        for o_ref, o in zip(orow_refs, outs):
            o_ref[...] = o.astype(o_ref.dtype)
        if oacc_refs:
            @pl.when(i == 0)
            def _():
                for a_ref in oacc_refs:
                    a_ref[...] = jnp.zeros_like(a_ref)
            for a_ref, a in zip(oacc_refs, accs):
                a_ref[...] += a.astype(F32)

    res = pl.pallas_call(
        body, name=name, grid=(nb,),
        out_shape=out_shape, in_specs=in_specs, out_specs=out_specs,
        compiler_params=_cparams(("arbitrary",)),
    )(*operands)
    return res[:n_or], res[n_or:]


def _rms(x, g):
    return x * lax.rsqrt(jnp.mean(x * x, axis=-1, keepdims=True) + EPS) * g


def _norm_mod(x, g, sc, sh):
    return _rms(x, g) * (1.0 + sc) + sh


def _sigmoid(x):
    return 0.5 * jnp.tanh(0.5 * x) + 0.5


def _silu(x):
    return x * _sigmoid(x)


def _gelu(x):
    return 0.5 * x * (1.0 + jnp.tanh(math.sqrt(2.0 / math.pi) * (x + 0.044715 * (x * x * x))))


def _sel(is_ctx, p):
    return jnp.where(is_ctx, p[1:2], p[0:1])


def _seg_acc(is_ctx, v):
    rows = lax.broadcasted_iota(jnp.int32, (2, v.shape[1]), 0)
    return jnp.where(rows == is_ctx.astype(jnp.int32), jnp.broadcast_to(v, (2, v.shape[1])), 0.0)


def _rsum(v):
    return jnp.sum(v, axis=0, keepdims=True)


def _shift_rows(x, o, t_lat, n):
    if o == 0:
        return x
    y = pltpu.roll(x, (-o) % n, 0)
    t = lax.broadcasted_iota(jnp.int32, x.shape, 0)
    src = t + o
    ok = (src >= 0) & (src < n) & ((src >= t_lat) == (t >= t_lat))
    return jnp.where(ok, y, 0.0)


def conv_fwd(name, xarr, col_off, width, w, b, left, n_rows, t_lat, out_dtype, cb=128):
    taps = w.shape[0]
    assert col_off % cb == 0 and width % cb == 0

    def body(x_ref, w_ref, b_ref, o_ref):
        x = x_ref[...].astype(F32)
        acc = jnp.broadcast_to(b_ref[...], x.shape)
        for k in range(taps):
            acc = acc + _shift_rows(x, k - left, t_lat, n_rows) * w_ref[k:k + 1, :]
        o_ref[...] = acc.astype(o_ref.dtype)

    return pl.pallas_call(
        body, name=name, grid=(width // cb,),
        out_shape=jax.ShapeDtypeStruct((n_rows, width), out_dtype),
        in_specs=[pl.BlockSpec((n_rows, cb), lambda j: (0, col_off // cb + j)),
                  pl.BlockSpec((taps, cb), lambda j: (0, j)),
                  pl.BlockSpec((1, cb), lambda j: (0, j))],
        out_specs=pl.BlockSpec((n_rows, cb), lambda j: (0, j)),
        compiler_params=_cparams(("parallel",)),
    )(xarr, w, b)


def conv_bwd(name, dout, xarr, col_off, width, w, left, n_rows, t_lat, cb=128):
    taps = w.shape[0]

    def body(d_ref, x_ref, w_ref, dx_ref, dw_ref, db_ref):
        d = d_ref[...].astype(F32)
        x = x_ref[...].astype(F32)
        dx = jnp.zeros_like(d)
        dws = []
        for k in range(taps):
            dx = dx + _shift_rows(d, left - k, t_lat, n_rows) * w_ref[k:k + 1, :]
            dws.append(_rsum(d * _shift_rows(x, k - left, t_lat, n_rows)))
        dx_ref[...] = dx.astype(dx_ref.dtype)
        dw_ref[...] = jnp.concatenate(dws, axis=0)
        db_ref[...] = _rsum(d)

    return pl.pallas_call(
        body, name=name, grid=(width // cb,),
        out_shape=[jax.ShapeDtypeStruct((n_rows, width), BF16), jax.ShapeDtypeStruct((taps, width), F32),
                   jax.ShapeDtypeStruct((1, width), F32)],
        in_specs=[pl.BlockSpec((n_rows, cb), lambda j: (0, j)),
                  pl.BlockSpec((n_rows, cb), lambda j: (0, col_off // cb + j)),
                  pl.BlockSpec((taps, cb), lambda j: (0, j))],
        out_specs=[pl.BlockSpec((n_rows, cb), lambda j: (0, j)), pl.BlockSpec((taps, cb), lambda j: (0, j)),
                   pl.BlockSpec((1, cb), lambda j: (0, j))],
        compiler_params=_cparams(("parallel",)),
    )(dout, xarr, w)


def _chunk_order(direction, nb, nbl):
    if direction == 'f':
        return lambda s: ((s + nbl) % nb, 0)
    return lambda s: (nb - 1 - s, 0)


def _adjoint_order(direction, nb, nbl):
    if direction == 'f':
        return lambda s: ((nb - 1 - s + nbl) % nb, 0)
    return lambda s: (s, 0)


SUBLANES = 8


def _chunk_scan(a, b, carry, rev):
    tc = a.shape[0]
    row = lax.broadcasted_iota(jnp.int32, a.shape, 0)
    in_tile = jnp.bitwise_and(row, SUBLANES - 1)
    for k in (1, 2, 4):
        shift = tc - k if rev else k
        edge = in_tile >= SUBLANES - k if rev else in_tile < k
        b = jnp.where(edge, b, a * pltpu.roll(b, shift, 0) + b)
        a = jnp.where(edge, a, a * pltpu.roll(a, shift, 0))
    nt = tc // SUBLANES
    hs = [None] * nt
    c = carry
    for kt in range(nt):
        k = nt - 1 - kt if rev else kt
        h = b[k * SUBLANES:(k + 1) * SUBLANES] + a[k * SUBLANES:(k + 1) * SUBLANES] * c
        hs[k] = h
        c = h[0:1] if rev else h[SUBLANES - 1:SUBLANES]
    h = jnp.concatenate(hs, axis=0)
    if rev:
        return h, jnp.where(row == tc - 1, carry, pltpu.roll(h, tc - 1, 0)), c
    return h, jnp.where(row == 0, carry, pltpu.roll(h, 1, 0)), c


def scan_fwd(name, a, u, direction, n_rows, t_lat):
    w = a.shape[1]
    tc = _pick(math.gcd(t_lat, n_rows), (256, 128))
    nb, nbl = n_rows // tc, t_lat // tc
    order = _chunk_order(direction, nb, nbl)
    rev = direction == 'b'

    def body(a_ref, u_ref, h_ref, hp_ref, carry):
        @pl.when(pl.program_id(0) == 0)
        def _():
            carry[...] = jnp.zeros_like(carry)

        h_ref[...], hp_ref[...], carry[...] = _chunk_scan(a_ref[...], u_ref[...], carry[...], rev)

    spec = pl.BlockSpec((tc, w), order)
    return pl.pallas_call(
        body, name=name, grid=(nb,),
        out_shape=[jax.ShapeDtypeStruct((n_rows, w), F32)] * 2,
        in_specs=[spec, spec], out_specs=[spec, spec],
        scratch_shapes=[pltpu.VMEM((1, w), F32)],
        compiler_params=_cparams(("arbitrary",)),
    )(a, u)


def scan_adj(name, a, dh, hprev, direction, n_rows, t_lat):
    w = a.shape[1]
    tc = _pick(math.gcd(t_lat, n_rows), (256, 128))
    nb, nbl = n_rows // tc, t_lat // tc
    order = _adjoint_order(direction, nb, nbl)
    rev = direction == 'f'

    def dh_order(s):
        c, _ = order(s)
        return (jnp.minimum(c, nbl - 1), 0)

    def body(a_ref, dh_ref, hp_ref, du_ref, da_ref, carry):
        s = pl.program_id(0)

        @pl.when(s == 0)
        def _():
            carry[...] = jnp.zeros_like(carry)

        chunk, _ = order(s)
        live = (chunk < nbl).astype(F32)

        av = a_ref[...]
        dv = dh_ref[...] * live
        _, c_next, carry[...] = _chunk_scan(av, av * dv, carry[...], rev)
        lam = dv + c_next
        du_ref[...] = lam
        da_ref[...] = lam * hp_ref[...]

    spec = pl.BlockSpec((tc, w), order)
    return pl.pallas_call(
        body, name=name, grid=(nb,),
        out_shape=[jax.ShapeDtypeStruct((n_rows, w), F32)] * 2,
        in_specs=[spec, pl.BlockSpec((tc, w), dh_order), spec], out_specs=[spec, spec],
        scratch_shapes=[pltpu.VMEM((1, w), F32)],
        compiler_params=_cparams(("arbitrary",)),
    )(a, dh, hprev)


def _neg_expm1(y):
    series = -(y * (1.0 + y * (0.5 + y * (1.0 / 6.0 + y * (1.0 / 24.0)))))
    return jnp.where(y > -0.03, series, 1.0 - jnp.exp(y))


def _gate_elem(pre_r, pre_i, xc, b_a, b_x, sp):
    r = _sigmoid(pre_r + b_a)
    i = _sigmoid(pre_i + b_x)
    log_a = (-LRU_C) * r * sp
    a = jnp.exp(log_a)
    mult = jnp.sqrt(_neg_expm1(2.0 * log_a))
    return a, mult * (i * xc)


def _gate_elem_bwd(pre_r, pre_i, xc, b_a, b_x, sp, da, du):
    r = _sigmoid(pre_r + b_a)
    i = _sigmoid(pre_i + b_x)
    log_a = (-LRU_C) * r * sp
    a = jnp.exp(log_a)
    m2 = _neg_expm1(2.0 * log_a)
    inv_mult = lax.rsqrt(m2)
    g = du * (m2 * inv_mult)
    d_mult = du * (i * xc)
    d_log_a = (da - d_mult * a * inv_mult) * a
    d_pre_r = d_log_a * ((-LRU_C) * sp) * (r * (1.0 - r))
    d_pre_i = g * xc * (i * (1.0 - i))
    return d_pre_r, d_pre_i, g * i, _rsum(d_log_a * ((-LRU_C) * r))


def _blockdiag(xb16, w_ref_val, d):
    outs = []
    for n in range(LRU_BLOCKS):
        outs.append(jnp.dot(xb16[:, n * LRU_BW:(n + 1) * LRU_BW], w_ref_val[d * LRU_BLOCKS + n],
                            preferred_element_type=F32))
    return jnp.concatenate(outs, axis=1)


def gates_fwd(xc, w_a, w_x, b_a, b_x, sp, n_rows, t_lat, tm):
    def fn(is_ctx, rows, params):
        (x,), (wa, wx, ba, bx, spv) = rows, params
        xb16 = x.astype(BF16)
        outs = []
        for d in range(2):
            a, u = _gate_elem(_blockdiag(xb16, wa, d), _blockdiag(xb16, wx, d), x,
                              ba[d:d + 1], bx[d:d + 1], spv[d:d + 1])
            outs += [a, u]
        return outs, []

    (a_f, u_f, a_b, u_b), _ = rowwise("gates_fwd", fn, [(xc, 0, LRU_W)], [w_a, w_x, b_a, b_x, sp],
                                      [(LRU_W, F32)] * 4, [], n_rows, t_lat, tm)
    return a_f, u_f, a_b, u_b


def gates_bwd(xc, da_f, du_f, da_b, du_b, w_a, w_x, b_a, b_x, sp, n_rows, t_lat, tm):
    def fn(is_ctx, rows, params):
        (x, daf, duf, dab, dub), (wa, wx, ba, bx, spv) = rows, params
        xb16 = x.astype(BF16)
        dxc = jnp.zeros_like(x)
        dwa, dwx, dba, dbx, dsp = [], [], [], [], []
        for d, (da, du) in enumerate(((daf, duf), (dab, dub))):
            dpr, dpi, dx_e, dsp_d = _gate_elem_bwd(_blockdiag(xb16, wa, d), _blockdiag(xb16, wx, d), x,
                                                   ba[d:d + 1], bx[d:d + 1], spv[d:d + 1], da, du)
            dba_d, dbx_d = _rsum(dpr), _rsum(dpi)
            dxc = dxc + dx_e
            dpr16, dpi16 = dpr.astype(BF16), dpi.astype(BF16)
            back = []
            for n in range(LRU_BLOCKS):
                sl = slice(n * LRU_BW, (n + 1) * LRU_BW)
                nt_dims = (((1,), (1,)), ((), ()))
                back.append(lax.dot_general(dpr16[:, sl], wa[d * LRU_BLOCKS + n], nt_dims, preferred_element_type=F32)
                            + lax.dot_general(dpi16[:, sl], wx[d * LRU_BLOCKS + n], nt_dims,
                                              preferred_element_type=F32))
                tn_dims = (((0,), (0,)), ((), ()))
                dwa.append(lax.dot_general(xb16[:, sl], dpr16[:, sl], tn_dims, preferred_element_type=F32)[None])
                dwx.append(lax.dot_general(xb16[:, sl], dpi16[:, sl], tn_dims, preferred_element_type=F32)[None])
            dxc = dxc + jnp.concatenate(back, axis=1)
            dba.append(dba_d)
            dbx.append(dbx_d)
            dsp.append(dsp_d)
        cat0 = lambda xs: jnp.concatenate(xs, axis=0)
        return [dxc], [cat0(dwa), cat0(dwx), cat0(dba), cat0(dbx), cat0(dsp)]

    (dxc,), accs = rowwise("gates_bwd", fn,
                           [(xc, 0, LRU_W), (da_f, 0, LRU_W), (du_f, 0, LRU_W), (da_b, 0, LRU_W), (du_b, 0, LRU_W)],
                           [w_a, w_x, b_a, b_x, sp], [(LRU_W, F32)],
                           [(2 * LRU_BLOCKS, LRU_BW, LRU_BW)] * 2 + [(2, LRU_W)] * 3, n_rows, t_lat, tm)
    return dxc, accs


def _rope_tables(t_lat, n_rows):
    rows = t_lat // GRID_W
    row_ids = jnp.repeat(jnp.arange(rows), GRID_W).astype(F32)
    col_ids = jnp.tile(jnp.arange(GRID_W), rows).astype(F32)
    axis_dim = QK_ROPE // 2
    inv = 1.0 / (ROPE_BASE ** (jnp.arange(0, axis_dim, 2, dtype=F32) / axis_dim))
    ang = jnp.concatenate([row_ids[:, None] * inv, col_ids[:, None] * inv], axis=-1)
    cos, sin = jnp.cos(ang), jnp.sin(ang)
    half = QK_ROPE // 2
    ones, zeros = jnp.ones((t_lat, QK_NOPE), F32), jnp.zeros((t_lat, QK_NOPE), F32)
    pad1, pad0 = jnp.ones((t_lat, HEAD_PAD - QK_DIM), F32), jnp.zeros((t_lat, HEAD_PAD - QK_DIM), F32)
    zh = jnp.zeros((t_lat, half), F32)
    c_tab = jnp.concatenate([ones, cos, cos, pad1], axis=1)
    s1 = jnp.concatenate([zeros, -sin, zh, pad0], axis=1)
    s2 = jnp.concatenate([zeros, zh, sin, pad0], axis=1)
    n_ctx = n_rows - t_lat
    c_tab = jnp.concatenate([c_tab, jnp.ones((n_ctx, HEAD_PAD), F32)], axis=0)
    s1 = jnp.concatenate([s1, jnp.zeros((n_ctx, HEAD_PAD), F32)], axis=0)
    s2 = jnp.concatenate([s2, jnp.zeros((n_ctx, HEAD_PAD), F32)], axis=0)
    return c_tab, s1, s2


def _rope(x, c, s1, s2):
    half = QK_ROPE // 2
    return x * c + pltpu.roll(x, HEAD_PAD - half, 1) * s1 + pltpu.roll(x, half, 1) * s2


def _rope_t(dy, c, s1, s2):
    half = QK_ROPE // 2
    return dy * c + pltpu.roll(dy * s1, half, 1) + pltpu.roll(dy * s2, HEAD_PAD - half, 1)


def _heads(x):
    return [x[:, h * HEAD_PAD:(h + 1) * HEAD_PAD] for h in range(N_HEADS)]


Q_SCALE = QK_DIM ** -0.5 * math.log2(math.e)


def attn_fwd(q, k, v, t_lat, n_rows, tq):
    def body(q_ref, k_ref, v_ref, o_ref, lse_ref):
        s = lax.dot_general(q_ref[...], k_ref[...], (((1,), (1,)), ((), ())), preferred_element_type=F32)
        m = jnp.max(s, axis=-1, keepdims=True)
        p = jnp.exp2(s - m)
        l = jnp.sum(p, axis=-1, keepdims=True)
        o = jnp.dot(p.astype(BF16), v_ref[...], preferred_element_type=F32) / l
        o_ref[...] = o.astype(o_ref.dtype)
        lse_ref[...] = jnp.broadcast_to(m + jnp.log2(l), lse_ref.shape)

    qspec = pl.BlockSpec((tq, HEAD_PAD), lambda h, i: (i, h))
    kspec = pl.BlockSpec((n_rows, HEAD_PAD), lambda h, i: (0, h))
    return pl.pallas_call(
        body, name="attn_fwd", grid=(N_HEADS, t_lat // tq),
        out_shape=[jax.ShapeDtypeStruct((t_lat, N_HEADS * HEAD_PAD), BF16),
                   jax.ShapeDtypeStruct((t_lat, N_HEADS * HEAD_PAD), F32)],
        in_specs=[qspec, kspec, kspec], out_specs=[qspec, qspec],
        compiler_params=_cparams(("parallel", "arbitrary")),
    )(q, k, v)


def attn_bwd(q, k, v, o, do, lse, t_lat, n_rows, tq):
    scale = QK_DIM ** -0.5
    nq = t_lat // tq
    nt = (((1,), (1,)), ((), ()))
    tn = (((0,), (0,)), ((), ()))

    def body(q_ref, k_ref, v_ref, o_ref, do_ref, lse_ref, dq_ref, dk_ref, dv_ref):
        @pl.when(pl.program_id(1) == 0)
        def _():
            dk_ref[...] = jnp.zeros_like(dk_ref)
            dv_ref[...] = jnp.zeros_like(dv_ref)

        qv, kv, vv, dov = q_ref[...], k_ref[...], v_ref[...], do_ref[...]
        s = lax.dot_general(qv, kv, nt, preferred_element_type=F32)
        p = jnp.exp2(s - lse_ref[:, 0:1])
        dv_ref[...] += lax.dot_general(p.astype(BF16), dov, tn, preferred_element_type=F32)
        dp = lax.dot_general(dov, vv, nt, preferred_element_type=F32)
        delta = jnp.sum(dov.astype(F32) * o_ref[...].astype(F32), axis=-1, keepdims=True)
        ds = (p * (dp - delta)).astype(BF16)
        dq_ref[...] = jnp.dot(ds, kv, preferred_element_type=F32) * scale
        dk_ref[...] += lax.dot_general(ds, qv, tn, preferred_element_type=F32)

        @pl.when(pl.program_id(1) == nq - 1)
        def _():
            dk_ref[...] = dk_ref[...] * (scale / Q_SCALE)

    qspec = pl.BlockSpec((tq, HEAD_PAD), lambda h, i: (i, h))
    kspec = pl.BlockSpec((n_rows, HEAD_PAD), lambda h, i: (0, h))
    return pl.pallas_call(
        body, name="attn_bwd", grid=(N_HEADS, t_lat // tq),
        out_shape=[jax.ShapeDtypeStruct((t_lat, N_HEADS * HEAD_PAD), F32),
                   jax.ShapeDtypeStruct((n_rows, N_HEADS * HEAD_PAD), F32),
                   jax.ShapeDtypeStruct((n_rows, N_HEADS * HEAD_PAD), F32)],
        in_specs=[qspec, kspec, kspec, qspec, qspec, qspec], out_specs=[qspec, kspec, kspec],
        compiler_params=_cparams(("parallel", "arbitrary")),
    )(q, k, v, o, do, lse)


def adamw(name, w, g, m, v):
    r, ccols = w.shape
    tr = _best_div(r, 8, max(8, 262144 // ccols)) if r % 8 == 0 else r
    c1 = 1.0 - ADAM_B1 ** ADAM_STEP
    c2 = 1.0 - ADAM_B2 ** ADAM_STEP

    def body(w_ref, g_ref, m_ref, v_ref, d_ref, nm_ref, nv_ref):
        gv = g_ref[...]
        nm = ADAM_B1 * m_ref[...] + (1.0 - ADAM_B1) * gv
        nv = ADAM_B2 * v_ref[...] + (1.0 - ADAM_B2) * (gv * gv)
        d_ref[...] = -ADAM_LR * ((nm / c1) / (jnp.sqrt(nv / c2) + ADAM_EPS) + ADAM_WD * w_ref[...])
        nm_ref[...] = nm
        nv_ref[...] = nv

    spec = pl.BlockSpec((tr, ccols), lambda i: (i, 0))
    return pl.pallas_call(
        body, name=name, grid=(r // tr,),
        out_shape=[jax.ShapeDtypeStruct((r, ccols), F32)] * 3,
        in_specs=[spec] * 4, out_specs=[spec] * 3,
        compiler_params=_cparams(("parallel",)),
    )(w, g, m, v)


def adamw_many(name, ws, gs, ms, vs):
    n = len(ws)
    c1 = 1.0 - ADAM_B1 ** ADAM_STEP
    c2 = 1.0 - ADAM_B2 ** ADAM_STEP

    def body(*refs):
        for i in range(n):
            w_ref, g_ref, m_ref, v_ref = (refs[k * n + i] for k in range(4))
            d_ref, nm_ref, nv_ref = (refs[(4 + k) * n + i] for k in range(3))
            gv = g_ref[...]
            nm = ADAM_B1 * m_ref[...] + (1.0 - ADAM_B1) * gv
            nv = ADAM_B2 * v_ref[...] + (1.0 - ADAM_B2) * (gv * gv)
            d_ref[...] = -ADAM_LR * ((nm / c1) / (jnp.sqrt(nv / c2) + ADAM_EPS) + ADAM_WD * w_ref[...])
            nm_ref[...] = nm
            nv_ref[...] = nv

    vmem = pl.BlockSpec(memory_space=pltpu.VMEM)
    res = pl.pallas_call(
        body, name=name,
        out_shape=[jax.ShapeDtypeStruct(w.shape, F32) for w in ws] * 3,
        in_specs=[vmem] * (4 * n), out_specs=[vmem] * (3 * n),
        compiler_params=_cparams(),
    )(*ws, *gs, *ms, *vs)
    return [tuple(res[k * n + i] for k in range(3)) for i in range(n)]


def _flat(parts, dtype, row_mult):
    v = jnp.concatenate([p.reshape(-1).astype(dtype) for p in parts])
    quantum = row_mult * FLAT_C
    total = -(-v.shape[0] // quantum) * quantum
    return jnp.pad(v, (0, total - v.shape[0])).reshape(total // FLAT_C, FLAT_C)


def _unflat(flat, shapes):
    v = flat.reshape(-1)
    out, at = [], 0
    for s in shapes:
        n = math.prod(s)
        out.append(v[at:at + n].reshape(s))
        at += n
    return out


def _gathered_to_full(name, g):
    k = g.shape[1]
    return jnp.transpose(g, (1, 0, 2)).reshape(k, N_DEV * g.shape[2])


def _full_to_chunks(name, full):
    k, n = full.shape
    return jnp.transpose(full.reshape(k, N_DEV, n // N_DEV), (1, 0, 2)).reshape(N_DEV, -1)


def _shard_to_rb(name, w):
    return w if name in ROW_SHARDED else w.T


def _rb_to_shard(name, g):
    return g if name in ROW_SHARDED else g.T


def _rb_from_gathered(name, g):
    cols = g.shape[2]
    if name == 'w_in':
        z = lambda k: jnp.zeros((k, cols), g.dtype)
        full = g.reshape(N_DEV * g.shape[1], cols)
        return jnp.concatenate([full[:Z_KR], z(QK_NOPE), full[Z_KR:Z_KR + QK_ROPE], z(HEAD_PAD - QK_DIM),
                                full[Z_KR + QK_ROPE:]], axis=0)
    if name == 'w_uq':
        return jnp.pad(g, ((0, 0), (0, HEAD_PAD - QK_DIM), (0, 0))).reshape(N_HEADS * HEAD_PAD, cols)
    if name == 'w_ukv':
        pad = lambda t: jnp.pad(t, ((0, 0), (0, HEAD_PAD - t.shape[1]), (0, 0))).reshape(N_HEADS * HEAD_PAD, cols)
        return jnp.concatenate([pad(g[:, :QK_NOPE]), pad(g[:, QK_NOPE:])], axis=0)
    if name == 'w_o_attn':
        full = g.reshape(D, N_HEADS, V_HEAD)
        return jnp.pad(full, ((0, 0), (0, 0), (0, HEAD_PAD - V_HEAD))).reshape(D, N_HEADS * HEAD_PAD)
    return g.reshape(N_DEV * g.shape[1], cols)


def _chunks_from_rb_grad(name, g):
    cols = g.shape[1]
    if name == 'w_in':
        full = jnp.concatenate([g[:Z_KR], g[Z_KR + QK_NOPE:Z_KR + QK_DIM], g[Z_XB:]], axis=0)
        return full.reshape(N_DEV, -1, cols)
    if name == 'w_uq':
        return g.reshape(N_HEADS, HEAD_PAD, cols)[:, :QK_DIM]
    if name == 'w_ukv':
        half = N_HEADS * HEAD_PAD
        gk = g[:half].reshape(N_HEADS, HEAD_PAD, cols)[:, :QK_NOPE]
        gv = g[half:].reshape(N_HEADS, HEAD_PAD, cols)[:, :V_HEAD]
        return jnp.concatenate([gk, gv], axis=1)
    if name == 'w_o_attn':
        full = g.reshape(D, N_HEADS, HEAD_PAD)[:, :, :V_HEAD].reshape(D, N_HEADS * V_HEAD)
        return full.reshape(N_DEV, D // N_DEV, N_HEADS * V_HEAD)
    return g.reshape(N_DEV, -1, cols)


def local_step(x, ctx, target, mod_l, mod_c, wt, on_grad=None, arrive=None):
    t_lat, n_ctx = x.shape[0], ctx.shape[0]
    n = t_lat + n_ctx
    tm = _pick(math.gcd(t_lat, n), (256, 128))
    tq_fwd = _pick(t_lat, (512, 256, 128))
    tq_bwd = _pick(t_lat, (512, 256, 128))
    row = lambda v: v.reshape(1, -1).astype(F32)
    two = lambda a, b: jnp.stack([a, b]).astype(F32)
    sh1_l, sc1_l, g1_l, sh2_l, sc2_l, g2_l = jnp.split(mod_l, 6)
    sh1_c, sc1_c = jnp.split(mod_c, 6)[:2]
    sc1, sh1 = two(sc1_l, sc1_c), two(sh1_l, sh1_c)
    g1, g2, sc2, sh2 = row(g1_l), row(g2_l), row(sc2_l), row(sh2_l)
    norm1_g, norm2_g, final_g = row(wt['norm1_g']), row(wt['norm2_g']), row(wt['final_g'])
    q_g, kv_g, b_gate = row(wt['q_norm_g']), row(wt['kv_norm_g']), row(wt['b_gate'])
    wt = dict(wt)
    pending = []

    def sent():
        tokens = list(pending)
        pending.clear()
        return tokens

    def need(names, after):
        if arrive is not None:
            got = arrive(names, after)
            if '_token' in got:
                pending.append(got.pop('_token'))
            wt.update(got)
        return [wt[n] for n in names]
    lru_w_a = wt['lru_w_a'].reshape(2 * LRU_BLOCKS, LRU_BW, LRU_BW).astype(BF16)
    lru_w_x = wt['lru_w_x'].reshape(2 * LRU_BLOCKS, LRU_BW, LRU_BW).astype(BF16)
    b_a, b_x, lam = wt['lru_b_a'], wt['lru_b_x'], wt['lru_lambda']
    sp = jnp.logaddexp(-lam, 0.0)
    c_tab, s1_tab, s2_tab = _rope_tables(t_lat, n)
    rw = functools.partial(rowwise, n_rows=n, t_lat=t_lat, tm=tm)
    rw_lat = functools.partial(rowwise, n_rows=t_lat, t_lat=t_lat, tm=tm)

    xs = jnp.concatenate([x, ctx], axis=0)

    def f_norm1(is_ctx, rows, params):
        (xv,), (g, sc, sh) = rows, params
        return [_norm_mod(xv, g, _sel(is_ctx, sc), _sel(is_ctx, sh))], []

    (h,), _ = rw("norm1", f_norm1, [(xs, 0, D)], [norm1_g, sc1, sh1], [(D, BF16)], [])
    (w_in_t,) = need(('w_in',), h)
    z = matmul("w_in", h, w_in_t, 'nt', BF16, after=sent())
    w_uq_t, w_ukv_t, w_o_lru = need(('w_uq', 'w_ukv', 'w_o_lru'), z)

    def f_qkv_norm(is_ctx, rows, params):
        (ql, kvl), (gq, gkv) = rows, params
        return [_rms(ql, gq), _rms(kvl, gkv)], []

    (qn, kvn), _ = rw("qkv_norm", f_qkv_norm, [(z, Z_Q, Q_RANK), (z, Z_KV, KV_RANK)], [q_g, kv_g],
                      [(Q_RANK, BF16), (KV_RANK, BF16)], [])
    qp = matmul("w_uq", qn, w_uq_t, 'nt', BF16)
    kvp = matmul("w_ukv", kvn, w_ukv_t, 'nt', BF16)

    def f_rope(is_ctx, rows, params):
        qv, kk, vv, kr, c, s1, s2 = rows
        krr = _rope(kr, c, s1, s2)
        qo = jnp.concatenate([_rope(qh, c, s1, s2) for qh in _heads(qv)], axis=1) * Q_SCALE
        ko = jnp.concatenate([kh + krr for kh in _heads(kk)], axis=1)
        return [qo, ko, vv], []

    hp = N_HEADS * HEAD_PAD
    (qr, kr_, vr), _ = rw("rope", f_rope,
                          [(qp, 0, hp), (kvp, 0, hp), (kvp, hp, hp), (z, Z_KR, HEAD_PAD), (c_tab, 0, HEAD_PAD),
                           (s1_tab, 0, HEAD_PAD), (s2_tab, 0, HEAD_PAD)], [], [(hp, BF16)] * 3, [])
    attn, lse = attn_fwd(qr, kr_, vr, t_lat, n, tq_fwd)

    xc = conv_fwd("lru_conv", z, Z_XB, LRU_W, wt['lru_conv_w'], row(wt['lru_conv_b']), 2, n, t_lat, F32)
    a_f, u_f, a_b, u_b = gates_fwd(xc, lru_w_a, lru_w_x, b_a, b_x, sp, n, t_lat, tm)
    h_f, hp_f = scan_fwd("scan_f", a_f, u_f, 'f', n, t_lat)
    h_b, hp_b = scan_fwd("scan_b", a_b, u_b, 'b', n, t_lat)

    def f_lru_out(is_ctx, rows, params):
        hf, hb, yb = rows
        return [(hf + hb) * _gelu(yb)], []

    (ybin,), _ = rw_lat("lru_out", f_lru_out, [(h_f, 0, LRU_W), (h_b, 0, LRU_W), (z, Z_YB, LRU_W)], [],
                        [(LRU_W, BF16)], [])
    w_o_attn_t, w_out, w_up_t, w_down = need(('w_o_attn', 'w_out', 'w_up', 'w_down'), attn)
    y_a = matmul("w_o_attn", attn, w_o_attn_t, 'nt', BF16)
    y_b = matmul("w_o_lru", ybin, w_o_lru, 'nn', BF16)

    def _merge(ya, yb, gl, bg):
        gates = _sigmoid(gl + bg)
        return gates[:, :D] * ya + gates[:, D:] * yb

    def f_merge(is_ctx, rows, params):
        (ya, yb, gl), (bg,) = rows, params
        return [_merge(ya, yb, gl, bg)], []

    (mrg,), _ = rw_lat("merge", f_merge, [(y_a, 0, D), (y_b, 0, D), (z, Z_GL, 2 * D)], [b_gate], [(D, BF16)], [])
    o = matmul("w_out", mrg, w_out, 'nn', BF16)

    def _res_norm2(xv, ov, g1v, g, sc, sh):
        x1 = xv + g1v * ov
        return x1, _norm_mod(x1, g, sc, sh)

    def f_norm2(is_ctx, rows, params):
        (xv, ov), (g1v, g, sc, sh) = rows, params
        x1, h2v = _res_norm2(xv, ov, g1v, g, sc, sh)
        return [x1, h2v], []

    (x1, h2), _ = rw_lat("norm2", f_norm2, [(x, 0, D), (o, 0, D)], [g1, norm2_g, sc2, sh2], [(D, F32), (D, BF16)], [])
    u = matmul("w_up", h2, w_up_t, 'nt', BF16)
    ac = conv_fwd("ffn_conv", u, 0, FFN, wt['ffn_conv_w'], row(wt['ffn_conv_b']), 1, t_lat, t_lat, BF16)

    def f_ffn_act(is_ctx, rows, params):
        acv, gv = rows
        return [_silu(acv) * gv], []

    (f,), _ = rw_lat("ffn_act", f_ffn_act, [(ac, 0, FFN), (u, FFN, FFN)], [], [(FFN, BF16)], [])
    dn = matmul("w_down", f, w_down, 'nn', BF16)

    def _tile_loss(x1v, dv, g2v, fg, tgt):
        y = _rms(x1v + g2v * dv, fg)
        e = y - tgt
        return 0.5 * jnp.sum(jnp.mean(e * e, axis=-1, keepdims=True), axis=0, keepdims=True)

    def f_final(is_ctx, rows, params):
        (x1v, dv, tgt), (g2v, fg) = rows, params
        lv, vjp = jax.vjp(lambda a, b, c, d: _tile_loss(a, b, c, d, tgt), x1v, dv, g2v, fg)
        dx2, dd, dg2, dfg = vjp(jnp.ones((1, 1), F32))
        return [dx2, dd], [dg2, dfg, jnp.broadcast_to(lv, (1, 128))]

    (dx2, dd), (dg2, dfinal_g, loss_v) = rw_lat("final", f_final, [(x1, 0, D), (dn, 0, D), (target, 0, D)],
                                                [g2, final_g], [(D, F32), (D, BF16)], [(1, D), (1, D), (1, 128)])
    loss = loss_v[0, 0]

    grads = {'final_g': dfinal_g}

    def put(name, g):
        grads[name] = g
        if on_grad is not None:
            pending.append(on_grad(name, g))
    df = matmul("d_f", dd, w_down, 'nt', BF16)
    put('w_down', matmul("g_w_down", f, dd, 'tn', BF16))

    def b_ffn_act(is_ctx, rows, params):
        acv, gv, dfv = rows
        _, vjp = jax.vjp(lambda a, g: _silu(a) * g, acv, gv)
        dac, dg = vjp(dfv)
        return [dac, dg], []

    (dac, dgate), _ = rw_lat("ffn_act_bwd", b_ffn_act, [(ac, 0, FFN), (u, FFN, FFN), (df, 0, FFN)], [],
                             [(FFN, BF16), (FFN, BF16)], [])
    da, grads['ffn_conv_w'], grads['ffn_conv_b'] = conv_bwd("ffn_conv_bwd", dac, u, 0, FFN, wt['ffn_conv_w'], 1,
                                                            t_lat, t_lat)
    du = jnp.concatenate([da, dgate], axis=1)
    dh2 = matmul("d_h2", du, w_up_t, 'nn', BF16, after=sent())
    put('w_up', matmul("g_w_up", du, h2, 'tn', BF16))

    def b_norm2(is_ctx, rows, params):
        (xv, ov, dh2v, dx2v), (g1v, g, sc, sh) = rows, params
        _, vjp = jax.vjp(_res_norm2, xv, ov, g1v, g, sc, sh)
        dx, do, dg1v, dg, dsc, dsh = vjp((dx2v, dh2v))
        return [dx, do], [dg1v, dg, dsc, dsh]

    (dx_res, do), (dg1, dnorm2_g, dsc2, dsh2) = rw_lat(
        "norm2_bwd", b_norm2, [(x, 0, D), (o, 0, D), (dh2, 0, D), (dx2, 0, D)], [g1, norm2_g, sc2, sh2],
        [(D, F32), (D, BF16)], [(1, D)] * 4)
    grads['norm2_g'] = dnorm2_g
    dmrg = matmul("d_merge", do, w_out, 'nt', BF16, after=sent())
    put('w_out', matmul("g_w_out", mrg, do, 'tn', BF16))

    def b_merge(is_ctx, rows, params):
        (ya, yb, gl, dm), (bg,) = rows, params
        _, vjp = jax.vjp(_merge, ya, yb, gl, bg)
        dya, dyb, dgl, dbg = vjp(dm)
        return [dya, dyb, dgl], [dbg]

    (dy_a, dy_b, dgl), (grads['b_gate'],) = rw_lat(
        "merge_bwd", b_merge, [(y_a, 0, D), (y_b, 0, D), (z, Z_GL, 2 * D), (dmrg, 0, D)], [b_gate],
        [(D, BF16), (D, BF16), (2 * D, BF16)], [(1, 2 * D)])
    dattn = matmul("d_attn", dy_a, w_o_attn_t, 'nn', BF16, after=sent())
    put('w_o_attn', matmul("g_w_o_attn", dy_a, attn, 'tn', BF16))
    dybin = matmul("d_lru_out", dy_b, w_o_lru, 'nt', BF16, after=sent())
    put('w_o_lru', matmul("g_w_o_lru", ybin, dy_b, 'tn', BF16))

    def b_lru_out(is_ctx, rows, params):
        hf, hb, yb, dyv = rows
        _, vjp = jax.vjp(lambda s, y: s * _gelu(y), hf + hb, yb)
        dh, dyb = vjp(dyv)
        return [dh, dyb], []

    (dh_lru, dyb), _ = rw_lat("lru_out_bwd", b_lru_out,
                              [(h_f, 0, LRU_W), (h_b, 0, LRU_W), (z, Z_YB, LRU_W), (dybin, 0, LRU_W)], [],
                              [(LRU_W, F32), (LRU_W, BF16)], [])
    du_f, da_f = scan_adj("scan_f_adj", a_f, dh_lru, hp_f, 'f', n, t_lat)
    du_b, da_b = scan_adj("scan_b_adj", a_b, dh_lru, hp_b, 'b', n, t_lat)
    dxc, (dw_a, dw_x, db_a, db_x, dsp) = gates_bwd(xc, da_f, du_f, da_b, du_b, lru_w_a, lru_w_x, b_a, b_x, sp,
                                                   n, t_lat, tm)
    put('lru_w_a', dw_a.reshape(2 * LRU_BLOCKS * LRU_BW, LRU_BW).astype(BF16))
    put('lru_w_x', dw_x.reshape(2 * LRU_BLOCKS * LRU_BW, LRU_BW).astype(BF16))
    grads['lru_b_a'], grads['lru_b_x'] = db_a, db_x
    grads['lru_lambda'] = -dsp * _sigmoid(-lam)
    dxb, grads['lru_conv_w'], grads['lru_conv_b'] = conv_bwd("lru_conv_bwd", dxc, z, Z_XB, LRU_W, wt['lru_conv_w'],
                                                             2, n, t_lat)

    dq, dk, dv = attn_bwd(qr, kr_, vr, attn, dattn, lse, t_lat, n, tq_bwd)

    def b_rope(is_ctx, rows, params):
        dqv, dkv, dvv, c, s1, s2 = rows
        live = jnp.where(is_ctx, 0.0, 1.0)
        dqo = jnp.concatenate([_rope_t(dqh, c, s1, s2) for dqh in _heads(dqv)], axis=1) * live
        dkh = _heads(dkv)
        dkr = dkh[0]
        for t in dkh[1:]:
            dkr = dkr + t
        lanes = lax.broadcasted_iota(jnp.int32, dkr.shape, 1)
        dkr = jnp.where((lanes >= QK_NOPE) & (lanes < QK_DIM), _rope_t(dkr, c, s1, s2), 0.0)
        return [dqo, jnp.concatenate([dkv, dvv], axis=1), dkr], []

    (dqp, dkvp, dkr), _ = rw("rope_bwd", b_rope,
                             [(dq, 0, hp), (dk, 0, hp), (dv, 0, hp), (c_tab, 0, HEAD_PAD), (s1_tab, 0, HEAD_PAD),
                              (s2_tab, 0, HEAD_PAD)], [], [(hp, BF16), (2 * hp, BF16), (HEAD_PAD, BF16)], [])
    dqn = matmul("d_qn", dqp, w_uq_t, 'nn', BF16, after=sent())
    put('w_uq', matmul("g_w_uq", dqp, qn, 'tn', BF16))
    dkvn = matmul("d_kvn", dkvp, w_ukv_t, 'nn', BF16, after=sent())
    put('w_ukv', matmul("g_w_ukv", dkvp, kvn, 'tn', BF16))

    def b_qkv_norm(is_ctx, rows, params):
        (ql, kvl, dqv, dkvv), (gq, gkv) = rows, params
        _, vjp_q = jax.vjp(_rms, ql, gq)
        _, vjp_kv = jax.vjp(_rms, kvl, gkv)
        dql, dgq = vjp_q(dqv)
        dkvl, dgkv = vjp_kv(dkvv)
        return [dql, dkvl], [dgq, dgkv]

    (dq_lat, dkv_lat), (grads['q_norm_g'], grads['kv_norm_g']) = rw(
        "qkv_norm_bwd", b_qkv_norm, [(z, Z_Q, Q_RANK), (z, Z_KV, KV_RANK), (dqn, 0, Q_RANK), (dkvn, 0, KV_RANK)],
        [q_g, kv_g], [(Q_RANK, BF16), (KV_RANK, BF16)], [(1, Q_RANK), (1, KV_RANK)])
    pad_ctx = lambda t: jnp.pad(t, ((0, n_ctx), (0, 0)))
    dz = jnp.concatenate([dq_lat, dkv_lat, dkr, dxb, pad_ctx(dyb), pad_ctx(dgl)], axis=1)
    put('w_in', matmul("g_w_in", dz, h, 'tn', BF16))
    dh = matmul("d_h", dz, w_in_t, 'nn', BF16, after=sent())

    def b_norm1(is_ctx, rows, params):
        (xv, dhv, dxr), (g, sc, sh) = rows, params
        scv, shv = _sel(is_ctx, sc), _sel(is_ctx, sh)
        _, vjp = jax.vjp(_norm_mod, xv, g, scv, shv)
        dx, dg, dsc, dsh = vjp(dhv)
        return [dx + dxr], [dg, _seg_acc(is_ctx, dsc), _seg_acc(is_ctx, dsh)]

    (dxs,), (grads['norm1_g'], dsc1, dsh1) = rw("norm1_bwd", b_norm1, [(xs, 0, D), (dh, 0, D), (dx_res, 0, D)],
                                                [norm1_g, sc1, sh1], [(D, F32)], [(1, D), (2, D), (2, D)])
    grad_x = dxs[:t_lat]
    zero = jnp.zeros((D,), F32)
    dmod_l = jnp.concatenate([dsh1[0], dsc1[0], dg1[0], dsh2[0], dsc2[0], dg2[0]])
    dmod_c = jnp.concatenate([dsh1[1], dsc1[1], zero, zero, zero, zero])
    return loss, grad_x, grads, dmod_l, dmod_c


def kernel(x, c, ctx, c_ctx, w_mod, b_mod, norm1_g, w_in, b_gate, q_norm_g, kv_norm_g, w_uq, w_ukv, w_o_attn, lru_conv_w, lru_conv_b, lru_w_a, lru_b_a, lru_w_x, lru_b_x, lru_lambda, w_o_lru, w_out, norm2_g, w_up, ffn_conv_w, ffn_conv_b, w_down, final_g, loss_target, m_c_ctx, m_w_mod, m_b_mod, m_norm1_g, m_w_in, m_b_gate, m_q_norm_g, m_kv_norm_g, m_w_uq, m_w_ukv, m_w_o_attn, m_lru_conv_w, m_lru_conv_b, m_lru_w_a, m_lru_b_a, m_lru_w_x, m_lru_b_x, m_lru_lambda, m_w_o_lru, m_w_out, m_norm2_g, m_w_up, m_ffn_conv_w, m_ffn_conv_b, m_w_down, m_final_g, v_c_ctx, v_w_mod, v_b_mod, v_norm1_g, v_w_in, v_b_gate, v_q_norm_g, v_kv_norm_g, v_w_uq, v_w_ukv, v_w_o_attn, v_lru_conv_w, v_lru_conv_b, v_lru_w_a, v_lru_b_a, v_lru_w_x, v_lru_b_x, v_lru_lambda, v_w_o_lru, v_w_out, v_norm2_g, v_w_up, v_ffn_conv_w, v_ffn_conv_b, v_w_down, v_final_g):
    given = dict(locals())
    strip = lambda name, a: a if name in ('c_ctx', 'final_g') else a[0]
    wsh = {n: strip(n, given[n]) for n in WEIGHTS}
    msh = {n: strip(n, given['m_' + n]) for n in WEIGHTS}
    vsh = {n: strip(n, given['v_' + n]) for n in WEIGHTS}
    me = _my_index()

    small = _flat([c[0]] + [wsh[n] for n in SMALL_F32], F32, 8)
    small_all = all_gather("gather_small", small).reshape(N_DEV, -1)
    c_all = small_all[:, :D]
    full, at = {}, D
    for n in SMALL_F32:
        cnt = math.prod(wsh[n].shape)
        full[n] = _gathered_to_full(n, small_all[:, at:at + cnt].reshape((N_DEV,) + wsh[n].shape))
        at += cnt

    cond = jnp.concatenate([c_all, c_ctx[None], jnp.zeros((7, D), F32)], axis=0)
    sil = cond * jax.nn.sigmoid(cond)
    mod_cols = matmul("mod_proj", sil, wsh['w_mod'], 'nn', F32)
    mod_all = all_gather("gather_mod", mod_cols)
    mod_all = jnp.transpose(mod_all, (1, 0, 2)).reshape(16, 6 * D) + b_mod[0][None]
    mod_l = lax.dynamic_index_in_dim(mod_all, me, axis=0, keepdims=False)
    mod_c = mod_all[N_DEV]

    rb_shards = {n: _shard_to_rb(n, wsh[n]).astype(BF16) for n in BIG_BF16}
    (w_in_blocks,) = all_gather_multi("gather_w_in", [rb_shards['w_in']])
    later = [n for n in BIG_BF16 if n != 'w_in']
    weights_started, weights_sent = exchange_start("weights_send", 'gather', [rb_shards[n] for n in later],
                                                   after=[w_in_blocks, mod_all])
    for n in REPLICATED:
        if n not in ('c_ctx', 'b_mod'):
            full[n] = wsh[n]

    def arrive(names, after):
        if names == ('w_in',):
            return {'w_in': _rb_from_gathered('w_in', w_in_blocks), '_token': weights_sent}
        picked = [later.index(n) for n in names]
        lands = exchange_wait("weights_wait_" + names[0], 'gather',
                              tuple([part[i] for i in picked] for part in weights_started), after)
        return {n: _rb_from_gathered(n, lax.dynamic_update_slice_in_dim(land, rb_shards[n][None], me, axis=0))
                for n, land in zip(names, lands)}

    in_flight = {}

    def on_grad(n, g):
        chunks = _chunks_from_rb_grad(n, g)
        own = lax.dynamic_index_in_dim(chunks, me, axis=0, keepdims=True)
        started, token = exchange_start("grad_send_" + n, 'scatter', [chunks])
        in_flight[n] = (own, started)
        return token

    loss, grad_x, grads, dmod_l, dmod_c = local_step(x[0], ctx[0], loss_target[0], mod_l, mod_c, full, on_grad,
                                                     arrive)
    loss = lax.psum(loss, ("x", "y", "c"))

    dmod = jnp.stack([dmod_l, dmod_c]).reshape(2 * 6 * D // FLAT_C, FLAT_C)
    dm = all_gather("gather_dmod", dmod).reshape(N_DEV, 2, 6 * D)
    dmod_c_tot = dm[0, 1]
    for p in range(1, N_DEV):
        dmod_c_tot = dmod_c_tot + dm[p, 1]
    dm16 = jnp.concatenate([dm[:, 0], dmod_c_tot[None], jnp.zeros((7, 6 * D), F32)], axis=0)
    ncol = 6 * D // N_DEV
    dm16_cols = lax.dynamic_slice_in_dim(dm16.reshape(16, N_DEV, ncol), me, 1, axis=1)[:, 0]
    grad_w_mod = matmul("g_w_mod", sil, dm16_cols, 'tn', F32)
    dsil = matmul("d_cond", dm16_cols, wsh['w_mod'], 'nt', F32)
    sg = jax.nn.sigmoid(c_ctx)
    grads['c_ctx'] = dsil[N_DEV] * (sg * (1.0 + c_ctx * (1.0 - sg)))
    grads['b_mod'] = dmod_l + dmod_c

    g_final = {'w_mod': grad_w_mod}
    reduced = {}
    for n in BIG_BF16 + ['lru_w_a', 'lru_w_x']:
        own, started = in_flight[n]
        (land,) = exchange_wait("grad_wait_" + n, 'scatter', started, dm)
        reduced[n] = sum_slots("sum_" + n, lax.dynamic_update_slice_in_dim(land, own, me, axis=0))
    for n in BIG_BF16:
        g_final[n] = _rb_to_shard(n, reduced[n])

    small_names = SMALL_F32 + [n for n in REPLICATED if n not in ('lru_w_a', 'lru_w_x')]
    partials = _flat([grads[n] for n in small_names], F32, 8)
    parts_all, a_all, x_all = all_gather_multi("gather_small_grads", [partials, reduced['lru_w_a'], reduced['lru_w_x']])
    small_sum = sum_slots("sum_small", parts_all).reshape(-1)
    g_final['lru_w_a'], g_final['lru_w_x'] = a_all.reshape(wsh['lru_w_a'].shape), x_all.reshape(wsh['lru_w_x'].shape)
    at = 0
    for n in small_names:
        cnt = math.prod(full[n].shape) if n in SMALL_F32 else math.prod(wsh[n].shape)
        g = small_sum[at:at + cnt]
        if n in SMALL_F32:
            k = full[n].shape[0]
            g = lax.dynamic_index_in_dim(g.reshape(k, N_DEV, -1), me, axis=1, keepdims=False)
        g_final[n] = g.reshape(wsh[n].shape)
        at += cnt

    stepped = {n: adamw("adamw_" + n, wsh[n], g_final[n], msh[n], vsh[n]) for n in ['w_mod'] + BIG_BF16}
    rest = [n for n in WEIGHTS if n not in stepped]
    as2d = lambda a: a.reshape(-1, a.shape[-1])
    rest_out = adamw_many("adamw_small", *[[as2d(d[n]) for n in rest] for d in (wsh, g_final, msh, vsh)])
    stepped.update(zip(rest, rest_out))
    shaped = lambda n, a: a.reshape(given[n].shape)
    return (loss, grad_x[None],
            *[shaped(n, g_final[n]) for n in WEIGHTS],
            *[shaped(n, stepped[n][k]) for k in range(3) for n in WEIGHTS])
```

```python
import functools
import math

import jax
import jax.numpy as jnp
from jax import lax
from jax.experimental import pallas as pl
from jax.experimental.pallas import tpu as pltpu

F32 = jnp.float32
BF16 = jnp.bfloat16
MESH = pl.DeviceIdType.MESH

N_DEV = 8
D = 1024
N_HEADS = 8
HEAD_PAD = 128
QK_NOPE, QK_ROPE, V_HEAD = 64, 32, 64
QK_DIM = QK_NOPE + QK_ROPE
Q_RANK, KV_RANK = 384, 256
LRU_W, LRU_BLOCKS, LRU_BW = 1280, 10, 128
FFN = 2816
GRID_W = 64
ROPE_BASE = 10000.0
LRU_C = 8.0
EPS = 1e-6
Z_Q, Z_KV, Z_KR, Z_XB, Z_YB, Z_GL, Z_END = 0, 384, 640, 768, 2048, 3328, 5376
ADAM_LR, ADAM_B1, ADAM_B2, ADAM_EPS, ADAM_WD, ADAM_STEP = 0.001, 0.9, 0.999, 1e-08, 0.01, 10

VMEM_LIMIT = 52 * 1024 * 1024
FLAT_C = 512
BIG_ROWS = 256

WEIGHTS = ['c_ctx', 'w_mod', 'b_mod', 'norm1_g', 'w_in', 'b_gate', 'q_norm_g', 'kv_norm_g', 'w_uq', 'w_ukv',
           'w_o_attn', 'lru_conv_w', 'lru_conv_b', 'lru_w_a', 'lru_b_a', 'lru_w_x', 'lru_b_x', 'lru_lambda',
           'w_o_lru', 'w_out', 'norm2_g', 'w_up', 'ffn_conv_w', 'ffn_conv_b', 'w_down', 'final_g']
COL_SHARDED = ['w_in', 'w_uq', 'w_ukv', 'w_o_attn', 'lru_conv_w', 'lru_b_a', 'lru_b_x', 'lru_lambda', 'w_up',
               'ffn_conv_w']
ROW_SHARDED = ['w_o_lru', 'w_out', 'w_down']
BIG_BF16 = ['w_in', 'w_uq', 'w_ukv', 'w_o_attn', 'w_o_lru', 'w_out', 'w_up', 'w_down']
SMALL_F32 = ['lru_conv_w', 'lru_b_a', 'lru_b_x', 'lru_lambda', 'ffn_conv_w']
SHARDED = BIG_BF16 + SMALL_F32
REPLICATED = ['c_ctx', 'b_mod', 'norm1_g', 'b_gate', 'q_norm_g', 'kv_norm_g', 'lru_conv_b', 'lru_w_a', 'lru_w_x',
              'norm2_g', 'ffn_conv_b', 'final_g']


def _cparams(sem=None):
    return pltpu.CompilerParams(dimension_semantics=sem, vmem_limit_bytes=VMEM_LIMIT)


def _pick(n, cands):
    for c in cands:
        if c <= n and n % c == 0:
            return c
    return n


def _best_div(n, mult, cap):
    best = mult
    for d in range(mult, min(n, cap) + 1, mult):
        if n % d == 0:
            best = d
    return best


ROW_TILES = (1088, 1024, 544, 512, 256, 128, 64, 32, 16, 8)
LANE_TILES = (1408, 1024, 896, 768, 640, 512, 384, 256, 128)


def _my_pos():
    return lax.axis_index("x"), lax.axis_index("y"), lax.axis_index("c")


def _my_index():
    x, y, c = _my_pos()
    return 4 * x + 2 * y + c


def all_gather_multi(name, shards):
    n_arr = len(shards)
    arrays = range(n_arr)

    def body(*refs):
        x_refs, out_refs = refs[:n_arr], refs[n_arr:2 * n_arr]
        send_sems, recv_sems, local_sems = refs[2 * n_arr:]
        x, y, c = _my_pos()
        me, sibling = (x, y, c), (x, y, 1 - c)
        chips = [(1 - x, y), (x, 1 - y), (1 - x, 1 - y)]

        def slot(a, px, py, pc):
            return out_refs[a].at[4 * px + 2 * py + pc]

        def copy(a, k, block, to, src=None):
            return pltpu.make_async_remote_copy(
                src_ref=slot(a, *block) if src is None else src, dst_ref=slot(a, *block),
                send_sem=send_sems.at[7 * a + k], recv_sem=recv_sems.at[7 * a + k], device_id=to,
                device_id_type=MESH)

        mine = [pltpu.make_async_copy(x_refs[a], slot(a, *me), local_sems.at[a]) for a in arrays]
        first = [copy(a, 1 + j, me, (*chip, c), src=x_refs[a]) for j, chip in enumerate(chips) for a in arrays]
        first += [copy(a, 0, me, sibling, src=x_refs[a]) for a in arrays]
        for cp in first + mine:
            cp.start()
        passed = []
        for j, chip in enumerate(chips):
            for a in arrays:
                copy(a, 1 + j, (*chip, c), me).wait_recv()
                passed.append(copy(a, 4 + j, (*chip, c), sibling))
                passed[-1].start()
        for a in arrays:
            copy(a, 0, sibling, me).wait_recv()
            for j, chip in enumerate(chips):
                copy(a, 4 + j, (*chip, 1 - c), me).wait_recv()
        for cp in first + passed:
            cp.wait_send()
        for cp in mine:
            cp.wait()

    hbm = pl.BlockSpec(memory_space=pl.ANY)
    return pl.pallas_call(
        body, name=name,
        out_shape=[jax.ShapeDtypeStruct((N_DEV,) + s.shape, s.dtype) for s in shards],
        in_specs=[hbm] * n_arr, out_specs=[hbm] * n_arr,
        scratch_shapes=[pltpu.SemaphoreType.DMA((7 * n_arr,)), pltpu.SemaphoreType.DMA((7 * n_arr,)),
                        pltpu.SemaphoreType.DMA((n_arr,))],
    )(*shards)


def all_gather(name, shard):
    return all_gather_multi(name, [shard])[0]


def all_to_all_multi(name, chunk_arrays):
    n_arr = len(chunk_arrays)
    arrays = range(n_arr)

    def body(*refs):
        x_refs, out_refs = refs[:n_arr], refs[n_arr:2 * n_arr]
        send_sems, recv_sems, local_sems = refs[2 * n_arr:]
        x, y, c = _my_pos()
        me = 4 * x + 2 * y + c
        mine = [pltpu.make_async_copy(x_refs[a].at[me], out_refs[a].at[me], local_sems.at[a]) for a in arrays]
        sends, arrivals = [], []
        for rel in (6, 4, 2, 7, 5, 3, 1):
            dx, dy, dc = (rel >> 2) & 1, (rel >> 1) & 1, rel & 1
            px, py, pc = x ^ dx, y ^ dy, c ^ dc
            peer = 4 * px + 2 * py + pc
            for a in arrays:
                k = 7 * a + rel - 1
                sends.append(pltpu.make_async_remote_copy(
                    src_ref=x_refs[a].at[peer], dst_ref=out_refs[a].at[me],
                    send_sem=send_sems.at[k], recv_sem=recv_sems.at[k],
                    device_id=(px, py, pc), device_id_type=MESH))
                arrivals.append(pltpu.make_async_remote_copy(
                    src_ref=x_refs[a].at[peer], dst_ref=out_refs[a].at[peer],
                    send_sem=send_sems.at[k], recv_sem=recv_sems.at[k],
                    device_id=(x, y, c), device_id_type=MESH))
        for cp in sends + mine:
            cp.start()
        for cp in arrivals:
            cp.wait_recv()
        for cp in sends:
            cp.wait_send()
        for cp in mine:
            cp.wait()

    hbm = pl.BlockSpec(memory_space=pl.ANY)
    return pl.pallas_call(
        body, name=name,
        out_shape=[jax.ShapeDtypeStruct(s.shape, s.dtype) for s in chunk_arrays],
        in_specs=[hbm] * n_arr, out_specs=[hbm] * n_arr,
        scratch_shapes=[pltpu.SemaphoreType.DMA((7 * n_arr,)), pltpu.SemaphoreType.DMA((7 * n_arr,)),
                        pltpu.SemaphoreType.DMA((n_arr,))],
    )(*chunk_arrays)


def _peers():
    x, y, c = _my_pos()
    out = []
    for rel in (6, 4, 2, 7, 5, 3, 1):
        px, py, pc = x ^ ((rel >> 2) & 1), y ^ ((rel >> 1) & 1), c ^ (rel & 1)
        out.append((rel - 1, (px, py, pc), 4 * px + 2 * py + pc))
    return out


def _exchange_copies(mode, src_refs, land_refs, send_sems, recv_sems):
    x, y, c = _my_pos()
    me = 4 * x + 2 * y + c
    sends, arrivals = [], []
    for k, peer_pos, peer in _peers():
        for a, (src, land) in enumerate(zip(src_refs, land_refs)):
            piece = src.at[peer] if mode == 'scatter' else src
            sems = dict(send_sem=send_sems[a].at[k], recv_sem=recv_sems[a].at[k], device_id_type=MESH)
            sends.append(pltpu.make_async_remote_copy(src_ref=piece, dst_ref=land.at[me], device_id=peer_pos, **sems))
            arrivals.append(pltpu.make_async_remote_copy(src_ref=piece, dst_ref=land.at[peer], device_id=(x, y, c), **sems))
    return sends, arrivals


_HBM = pl.BlockSpec(memory_space=pltpu.HBM)
_SEM = pl.BlockSpec(memory_space=pltpu.SEMAPHORE)


def exchange_start(name, mode, arrays, after=()):
    n_arr, n_after = len(arrays), len(after)
    land_shapes = [a.shape if mode == 'scatter' else (N_DEV,) + a.shape for a in arrays]

    def body(*refs):
        src_refs, land_refs = refs[:n_arr], refs[n_arr:2 * n_arr]
        refs = refs[n_after:]
        send_sems, recv_sems = refs[2 * n_arr:3 * n_arr], refs[3 * n_arr:4 * n_arr]
        sends, _ = _exchange_copies(mode, src_refs, land_refs, send_sems, recv_sems)
        for cp in sends:
            cp.start()
        token = refs[-1]
        token[...] = jnp.zeros_like(token)

    sem = pltpu.SemaphoreType.DMA((N_DEV - 1,))
    res = pl.pallas_call(
        body, name=name,
        out_shape=[sem] * (2 * n_arr) + [pltpu.HBM(a.shape, a.dtype) for a in arrays]
        + [pltpu.HBM(s, a.dtype) for s, a in zip(land_shapes, arrays)] + [jax.ShapeDtypeStruct((8, 128), F32)],
        in_specs=[_HBM] * (2 * n_arr) + [pl.BlockSpec(memory_space=pl.ANY)] * n_after,
        out_specs=[_SEM] * (2 * n_arr) + [_HBM] * (2 * n_arr) + [pl.BlockSpec(memory_space=pltpu.VMEM)],
        input_output_aliases={i: 2 * n_arr + i for i in range(2 * n_arr)},
        compiler_params=pltpu.CompilerParams(has_side_effects=pltpu.SideEffectType.DATAFLOW_SIDE_EFFECTING),
    )(*[pltpu.with_memory_space_constraint(a, pltpu.HBM) for a in arrays],
      *[pltpu.with_memory_space_constraint(lax.empty(s, a.dtype), pltpu.HBM) for s, a in zip(land_shapes, arrays)],
      *after)
    return (res[:n_arr], res[n_arr:2 * n_arr], res[2 * n_arr:3 * n_arr], res[3 * n_arr:4 * n_arr]), res[-1]


def exchange_wait(name, mode, started, after):
    send_sems, recv_sems, thru, land = started
    n_arr = len(thru)

    def body(*refs):
        src_refs, land_refs = refs[:n_arr], refs[n_arr:2 * n_arr]
        s_sems, r_sems = refs[2 * n_arr:3 * n_arr], refs[3 * n_arr:4 * n_arr]
        sends, arrivals = _exchange_copies(mode, src_refs, land_refs, s_sems, r_sems)
        for cp in sends:
            cp.wait_send()
        for cp in arrivals:
            cp.wait_recv()

    res = pl.pallas_call(
        body, name=name,
        out_shape=[pltpu.HBM(a.shape, a.dtype) for a in thru] + [pltpu.HBM(a.shape, a.dtype) for a in land],
        in_specs=[_HBM] * (2 * n_arr) + [_SEM] * (2 * n_arr) + [pl.BlockSpec(memory_space=pl.ANY)],
        out_specs=[_HBM] * (2 * n_arr),
        input_output_aliases={i: i for i in range(2 * n_arr)},
        compiler_params=pltpu.CompilerParams(has_side_effects=pltpu.SideEffectType.DATAFLOW_SIDE_EFFECTING),
    )(*thru, *land, *send_sems, *recv_sems, after)
    return res[n_arr:]


def sum_slots(name, slots):
    _, r, ccols = slots.shape
    tc = _pick(ccols, (256, 128))

    def body(s_ref, o_ref):
        acc = s_ref[0].astype(F32)
        for p in range(1, N_DEV):
            acc = acc + s_ref[p].astype(F32)
        o_ref[...] = acc

    return pl.pallas_call(
        body, name=name, grid=(ccols // tc,),
        out_shape=jax.ShapeDtypeStruct((r, ccols), F32),
        in_specs=[pl.BlockSpec((N_DEV, r, tc), lambda j: (0, 0, j))],
        out_specs=pl.BlockSpec((r, tc), lambda j: (0, j)),
        compiler_params=_cparams(("parallel",)),
    )(slots)


def matmul(name, a, b, mode, out_dtype, tm=None, tn=None, tk=None, after=()):
    after = [t for t in after if t is not None]
    if mode == 'nn':
        (m, k), (k2, n) = a.shape, b.shape
    elif mode == 'nt':
        (m, k), (n, k2) = a.shape, b.shape
    else:
        (k, m), (k2, n) = a.shape, b.shape
    assert k == k2, (name, a.shape, b.shape, mode)
    if mode == 'tn':
        tm = tm or _pick(m, LANE_TILES)
        tk = tk or _pick(k, ROW_TILES)
    else:
        tm = tm or _pick(m, ROW_TILES)
        tk = tk or _pick(k, LANE_TILES)
    tn = tn or _pick(n, LANE_TILES)
    nk = k // tk
    if mode == 'nn':
        a_spec = pl.BlockSpec((tm, tk), lambda i, j, kk: (i, kk))
        b_spec = pl.BlockSpec((tk, tn), lambda i, j, kk: (kk, j))
        dn = (((1,), (0,)), ((), ()))
    elif mode == 'nt':
        a_spec = pl.BlockSpec((tm, tk), lambda i, j, kk: (i, kk))
        b_spec = pl.BlockSpec((tn, tk), lambda i, j, kk: (j, kk))
        dn = (((1,), (1,)), ((), ()))
    else:
        a_spec = pl.BlockSpec((tk, tm), lambda i, j, kk: (kk, i))
        b_spec = pl.BlockSpec((tk, tn), lambda i, j, kk: (kk, j))
        dn = (((0,), (0,)), ((), ()))

    def product(a_ref, b_ref):
        return lax.dot_general(a_ref[...].astype(BF16), b_ref[...].astype(BF16), dn, preferred_element_type=F32)

    n_after = len(after)

    def body_one(a_ref, b_ref, *rest):
        o_ref = rest[n_after]
        o_ref[...] = product(a_ref, b_ref).astype(o_ref.dtype)

    def body(a_ref, b_ref, *rest):
        o_ref, acc_ref = rest[n_after:]
        kk = pl.program_id(2)

        @pl.when(kk == 0)
        def _():
            acc_ref[...] = jnp.zeros_like(acc_ref)

        acc_ref[...] += product(a_ref, b_ref)

        @pl.when(kk == nk - 1)
        def _():
            o_ref[...] = acc_ref[...].astype(o_ref.dtype)

    return pl.pallas_call(
        body_one if nk == 1 else body, name=name, grid=(m // tm, n // tn, nk),
        out_shape=jax.ShapeDtypeStruct((m, n), out_dtype),
        in_specs=[a_spec, b_spec] + [pl.BlockSpec(memory_space=pl.ANY)] * n_after,
        out_specs=pl.BlockSpec((tm, tn), lambda i, j, kk: (i, j)),
        scratch_shapes=[] if nk == 1 else [pltpu.VMEM((tm, tn), F32)],
        compiler_params=_cparams(("parallel", "parallel", "arbitrary")),
    )(a, b, *after)


def rowwise(name, fn, rows, params, out_rows, out_accs, n_rows, t_lat, tm):
    nb = n_rows // tm
    in_specs, piece_counts = [], []
    operands = []
    for arr, off, width in rows:
        g = math.gcd(off, width) if off else width
        assert g % 128 == 0 or (off == 0 and width == arr.shape[1]), (name, off, width)
        cnt = width // g
        last = arr.shape[0] // tm - 1
        clamp = arr.shape[0] < n_rows
        for p in range(cnt):
            cb = off // g + p
            if clamp:
                in_specs.append(pl.BlockSpec((tm, g), lambda i, cb=cb, last=last: (jnp.minimum(i, last), cb)))
            else:
                in_specs.append(pl.BlockSpec((tm, g), lambda i, cb=cb: (i, cb)))
            operands.append(arr)
        piece_counts.append(cnt)
    for p in params:
        in_specs.append(pl.BlockSpec(p.shape, lambda i, nd=p.ndim: (0,) * nd))
        operands.append(p)
    n_in = sum(piece_counts)
    n_par = len(params)
    n_or = len(out_rows)
    out_shape = [jax.ShapeDtypeStruct((n_rows, w), dt) for w, dt in out_rows]
    out_shape += [jax.ShapeDtypeStruct(s, F32) for s in out_accs]
    out_specs = [pl.BlockSpec((tm, w), lambda i: (i, 0)) for w, _ in out_rows]
    out_specs += [pl.BlockSpec(s, lambda i, nd=len(s): (0,) * nd) for s in out_accs]

    def body(*refs):
        in_refs, par_refs = refs[:n_in], refs[n_in:n_in + n_par]
        orow_refs = refs[n_in + n_par:n_in + n_par + n_or]
        oacc_refs = refs[n_in + n_par + n_or:]
        i = pl.program_id(0)
        tiles, at = [], 0
        for cnt in piece_counts:
            parts = [in_refs[at + p][...].astype(F32) for p in range(cnt)]
            tiles.append(parts[0] if cnt == 1 else jnp.concatenate(parts, axis=1))
            at += cnt
        is_ctx = i * tm >= t_lat
        outs, accs = fn(is_ctx, tiles, [p[...] for p in par_refs])
        for o_ref, o in zip(orow_refs, outs):
            o_ref[...] = o.astype(o_ref.dtype)
        if oacc_refs:
            @pl.when(i == 0)
            def _():
                for a_ref in oacc_refs:
                    a_ref[...] = jnp.zeros_like(a_ref)
            for a_ref, a in zip(oacc_refs, accs):
                a_ref[...] += a.astype(F32)

    res = pl.pallas_call(
        body, name=name, grid=(nb,),
        out_shape=out_shape, in_specs=in_specs, out_specs=out_specs,
        compiler_params=_cparams(("arbitrary",)),
    )(*operands)
    return res[:n_or], res[n_or:]


def _rms(x, g):
    return x * lax.rsqrt(jnp.mean(x * x, axis=-1, keepdims=True) + EPS) * g


def _norm_mod(x, g, sc, sh):
    return _rms(x, g) * (1.0 + sc) + sh


def _sigmoid(x):
    return 0.5 * jnp.tanh(0.5 * x) + 0.5


def _silu(x):
    return x * _sigmoid(x)


def _gelu(x):
    return 0.5 * x * (1.0 + jnp.tanh(math.sqrt(2.0 / math.pi) * (x + 0.044715 * (x * x * x))))


def _sel(is_ctx, p):
    return jnp.where(is_ctx, p[1:2], p[0:1])


def _seg_acc(is_ctx, v):
    rows = lax.broadcasted_iota(jnp.int32, (2, v.shape[1]), 0)
    return jnp.where(rows == is_ctx.astype(jnp.int32), jnp.broadcast_to(v, (2, v.shape[1])), 0.0)


def _rsum(v):
    return jnp.sum(v, axis=0, keepdims=True)


def _shift_rows(x, o, t_lat, n):
    if o == 0:
        return x
    y = pltpu.roll(x, (-o) % n, 0)
    t = lax.broadcasted_iota(jnp.int32, x.shape, 0)
    src = t + o
    ok = (src >= 0) & (src < n) & ((src >= t_lat) == (t >= t_lat))
    return jnp.where(ok, y, 0.0)


def conv_fwd(name, xarr, col_off, width, w, b, left, n_rows, t_lat, out_dtype, cb=128):
    taps = w.shape[0]
    assert col_off % cb == 0 and width % cb == 0

    def body(x_ref, w_ref, b_ref, o_ref):
        x = x_ref[...].astype(F32)
        acc = jnp.broadcast_to(b_ref[...], x.shape)
        for k in range(taps):
            acc = acc + _shift_rows(x, k - left, t_lat, n_rows) * w_ref[k:k + 1, :]
        o_ref[...] = acc.astype(o_ref.dtype)

    return pl.pallas_call(
        body, name=name, grid=(width // cb,),
        out_shape=jax.ShapeDtypeStruct((n_rows, width), out_dtype),
        in_specs=[pl.BlockSpec((n_rows, cb), lambda j: (0, col_off // cb + j)),
                  pl.BlockSpec((taps, cb), lambda j: (0, j)),
                  pl.BlockSpec((1, cb), lambda j: (0, j))],
        out_specs=pl.BlockSpec((n_rows, cb), lambda j: (0, j)),
        compiler_params=_cparams(("parallel",)),
    )(xarr, w, b)


def conv_bwd(name, dout, xarr, col_off, width, w, left, n_rows, t_lat, cb=128):
    taps = w.shape[0]

    def body(d_ref, x_ref, w_ref, dx_ref, dw_ref, db_ref):
        d = d_ref[...].astype(F32)
        x = x_ref[...].astype(F32)
        dx = jnp.zeros_like(d)
        dws = []
        for k in range(taps):
            dx = dx + _shift_rows(d, left - k, t_lat, n_rows) * w_ref[k:k + 1, :]
            dws.append(_rsum(d * _shift_rows(x, k - left, t_lat, n_rows)))
        dx_ref[...] = dx.astype(dx_ref.dtype)
        dw_ref[...] = jnp.concatenate(dws, axis=0)
        db_ref[...] = _rsum(d)

    return pl.pallas_call(
        body, name=name, grid=(width // cb,),
        out_shape=[jax.ShapeDtypeStruct((n_rows, width), BF16), jax.ShapeDtypeStruct((taps, width), F32),
                   jax.ShapeDtypeStruct((1, width), F32)],
        in_specs=[pl.BlockSpec((n_rows, cb), lambda j: (0, j)),
                  pl.BlockSpec((n_rows, cb), lambda j: (0, col_off // cb + j)),
                  pl.BlockSpec((taps, cb), lambda j: (0, j))],
        out_specs=[pl.BlockSpec((n_rows, cb), lambda j: (0, j)), pl.BlockSpec((taps, cb), lambda j: (0, j)),
                   pl.BlockSpec((1, cb), lambda j: (0, j))],
        compiler_params=_cparams(("parallel",)),
    )(dout, xarr, w)


def _ffn_conv(a, w_ref, b_ref, t_lat):
    shifted = [_shift_rows(a, k - 1, t_lat, t_lat) for k in range(3)]
    ac = jnp.broadcast_to(b_ref[...], a.shape)
    for k in range(3):
        ac = ac + shifted[k] * w_ref[k:k + 1, :]
    return ac, shifted


def ffn_mix_fwd(u, w, b, t_lat, cb=128):
    nblk = FFN // cb

    def body(a_ref, g_ref, w_ref, b_ref, f_ref):
        ac, _ = _ffn_conv(a_ref[...].astype(F32), w_ref, b_ref, t_lat)
        f_ref[...] = (_silu(ac) * g_ref[...].astype(F32)).astype(f_ref.dtype)

    col = lambda shape, off=0: pl.BlockSpec(shape, lambda j: (0, off + j))
    return pl.pallas_call(
        body, name="ffn_mix", grid=(nblk,),
        out_shape=jax.ShapeDtypeStruct((t_lat, FFN), BF16),
        in_specs=[col((t_lat, cb)), col((t_lat, cb), nblk), col((3, cb)), col((1, cb))],
        out_specs=col((t_lat, cb)),
        compiler_params=_cparams(("parallel",)),
    )(u, u, w, b)


def ffn_mix_bwd(u, df, w, b, t_lat, cb=128):
    nblk = FFN // cb

    def body(a_ref, g_ref, df_ref, w_ref, b_ref, da_ref, dg_ref, dw_ref, db_ref):
        ac, shifted = _ffn_conv(a_ref[...].astype(F32), w_ref, b_ref, t_lat)
        d = df_ref[...].astype(F32)
        s = _sigmoid(ac)
        dg_ref[...] = (d * (ac * s)).astype(dg_ref.dtype)
        dac = d * g_ref[...].astype(F32) * (s * (1.0 + ac * (1.0 - s)))
        da = jnp.zeros_like(dac)
        for k in range(3):
            da = da + _shift_rows(dac, 1 - k, t_lat, t_lat) * w_ref[k:k + 1, :]
        da_ref[...] = da.astype(da_ref.dtype)
        dw_ref[...] = jnp.concatenate([_rsum(dac * shifted[k]) for k in range(3)], axis=0)
        db_ref[...] = _rsum(dac)

    col = lambda shape, off=0: pl.BlockSpec(shape, lambda j: (0, off + j))
    return pl.pallas_call(
        body, name="ffn_mix_bwd", grid=(nblk,),
        out_shape=[jax.ShapeDtypeStruct((t_lat, FFN), BF16), jax.ShapeDtypeStruct((t_lat, FFN), BF16),
                   jax.ShapeDtypeStruct((3, FFN), F32), jax.ShapeDtypeStruct((1, FFN), F32)],
        in_specs=[col((t_lat, cb)), col((t_lat, cb), nblk), col((t_lat, cb)), col((3, cb)), col((1, cb))],
        out_specs=[col((t_lat, cb)), col((t_lat, cb)), col((3, cb)), col((1, cb))],
        compiler_params=_cparams(("parallel",)),
    )(u, u, df, w, b)


def _chunk_order(direction, nb, nbl):
    if direction == 'f':
        return lambda s: ((s + nbl) % nb, 0)
    return lambda s: (nb - 1 - s, 0)


def _adjoint_order(direction, nb, nbl):
    if direction == 'f':
        return lambda s: ((nb - 1 - s + nbl) % nb, 0)
    return lambda s: (s, 0)


SUBLANES = 8


def _chunk_scan(a, b, carry, rev):
    tc = a.shape[0]
    row = lax.broadcasted_iota(jnp.int32, a.shape, 0)
    in_tile = jnp.bitwise_and(row, SUBLANES - 1)
    for k in (1, 2, 4):
        shift = tc - k if rev else k
        edge = in_tile >= SUBLANES - k if rev else in_tile < k
        b = jnp.where(edge, b, a * pltpu.roll(b, shift, 0) + b)
        a = jnp.where(edge, a, a * pltpu.roll(a, shift, 0))
    nt = tc // SUBLANES
    hs = [None] * nt
    c = carry
    for kt in range(nt):
        k = nt - 1 - kt if rev else kt
        h = b[k * SUBLANES:(k + 1) * SUBLANES] + a[k * SUBLANES:(k + 1) * SUBLANES] * c
        hs[k] = h
        c = h[0:1] if rev else h[SUBLANES - 1:SUBLANES]
    h = jnp.concatenate(hs, axis=0)
    if rev:
        return h, jnp.where(row == tc - 1, carry, pltpu.roll(h, tc - 1, 0)), c
    return h, jnp.where(row == 0, carry, pltpu.roll(h, 1, 0)), c


def scan_fwd(name, a, u, direction, n_rows, t_lat):
    w = a.shape[1]
    tc = _pick(math.gcd(t_lat, n_rows), (256, 128))
    nb, nbl = n_rows // tc, t_lat // tc
    order = _chunk_order(direction, nb, nbl)
    rev = direction == 'b'

    def body(a_ref, u_ref, h_ref, hp_ref, carry):
        @pl.when(pl.program_id(0) == 0)
        def _():
            carry[...] = jnp.zeros_like(carry)

        h_ref[...], hp_ref[...], carry[...] = _chunk_scan(a_ref[...], u_ref[...], carry[...], rev)

    spec = pl.BlockSpec((tc, w), order)
    return pl.pallas_call(
        body, name=name, grid=(nb,),
        out_shape=[jax.ShapeDtypeStruct((n_rows, w), F32)] * 2,
        in_specs=[spec, spec], out_specs=[spec, spec],
        scratch_shapes=[pltpu.VMEM((1, w), F32)],
        compiler_params=_cparams(("arbitrary",)),
    )(a, u)


def scan_adj(name, a, dh, hprev, direction, n_rows, t_lat):
    w = a.shape[1]
    tc = _pick(math.gcd(t_lat, n_rows), (256, 128))
    nb, nbl = n_rows // tc, t_lat // tc
    order = _adjoint_order(direction, nb, nbl)
    rev = direction == 'f'

    def dh_order(s):
        c, _ = order(s)
        return (jnp.minimum(c, nbl - 1), 0)

    def body(a_ref, dh_ref, hp_ref, du_ref, da_ref, carry):
        s = pl.program_id(0)

        @pl.when(s == 0)
        def _():
            carry[...] = jnp.zeros_like(carry)

        chunk, _ = order(s)
        live = (chunk < nbl).astype(F32)

        av = a_ref[...]
        dv = dh_ref[...] * live
        _, c_next, carry[...] = _chunk_scan(av, av * dv, carry[...], rev)
        lam = dv + c_next
        du_ref[...] = lam
        da_ref[...] = lam * hp_ref[...]

    spec = pl.BlockSpec((tc, w), order)
    return pl.pallas_call(
        body, name=name, grid=(nb,),
        out_shape=[jax.ShapeDtypeStruct((n_rows, w), F32)] * 2,
        in_specs=[spec, pl.BlockSpec((tc, w), dh_order), spec], out_specs=[spec, spec],
        scratch_shapes=[pltpu.VMEM((1, w), F32)],
        compiler_params=_cparams(("arbitrary",)),
    )(a, dh, hprev)


def _neg_expm1(y):
    series = -(y * (1.0 + y * (0.5 + y * (1.0 / 6.0 + y * (1.0 / 24.0)))))
    return jnp.where(y > -0.03, series, 1.0 - jnp.exp(y))


def _gate_elem(pre_r, pre_i, xc, b_a, b_x, sp):
    r = _sigmoid(pre_r + b_a)
    i = _sigmoid(pre_i + b_x)
    log_a = (-LRU_C) * r * sp
    a = jnp.exp(log_a)
    mult = jnp.sqrt(_neg_expm1(2.0 * log_a))
    return a, mult * (i * xc)


def _gate_elem_bwd(pre_r, pre_i, xc, b_a, b_x, sp, da, du):
    r = _sigmoid(pre_r + b_a)
    i = _sigmoid(pre_i + b_x)
    log_a = (-LRU_C) * r * sp
    a = jnp.exp(log_a)
    m2 = _neg_expm1(2.0 * log_a)
    inv_mult = lax.rsqrt(m2)
    g = du * (m2 * inv_mult)
    d_mult = du * (i * xc)
    d_log_a = (da - d_mult * a * inv_mult) * a
    d_pre_r = d_log_a * ((-LRU_C) * sp) * (r * (1.0 - r))
    d_pre_i = g * xc * (i * (1.0 - i))
    return d_pre_r, d_pre_i, g * i, _rsum(d_log_a * ((-LRU_C) * r))


def _blockdiag(xb16, w_ref_val, d):
    outs = []
    for n in range(LRU_BLOCKS):
        outs.append(jnp.dot(xb16[:, n * LRU_BW:(n + 1) * LRU_BW], w_ref_val[d * LRU_BLOCKS + n],
                            preferred_element_type=F32))
    return jnp.concatenate(outs, axis=1)


def gates_fwd(xc, w_a, w_x, b_a, b_x, sp, n_rows, t_lat, tm):
    def fn(is_ctx, rows, params):
        (x,), (wa, wx, ba, bx, spv) = rows, params
        xb16 = x.astype(BF16)
        outs = []
        for d in range(2):
            a, u = _gate_elem(_blockdiag(xb16, wa, d), _blockdiag(xb16, wx, d), x,
                              ba[d:d + 1], bx[d:d + 1], spv[d:d + 1])
            outs += [a, u]
        return outs, []

    (a_f, u_f, a_b, u_b), _ = rowwise("gates_fwd", fn, [(xc, 0, LRU_W)], [w_a, w_x, b_a, b_x, sp],
                                      [(LRU_W, F32)] * 4, [], n_rows, t_lat, tm)
    return a_f, u_f, a_b, u_b


def gates_bwd(xc, da_f, du_f, da_b, du_b, w_a, w_x, b_a, b_x, sp, n_rows, t_lat, tm):
    def fn(is_ctx, rows, params):
        (x, daf, duf, dab, dub), (wa, wx, ba, bx, spv) = rows, params
        xb16 = x.astype(BF16)
        dxc = jnp.zeros_like(x)
        dwa, dwx, dba, dbx, dsp = [], [], [], [], []
        for d, (da, du) in enumerate(((daf, duf), (dab, dub))):
            dpr, dpi, dx_e, dsp_d = _gate_elem_bwd(_blockdiag(xb16, wa, d), _blockdiag(xb16, wx, d), x,
                                                   ba[d:d + 1], bx[d:d + 1], spv[d:d + 1], da, du)
            dba_d, dbx_d = _rsum(dpr), _rsum(dpi)
            dxc = dxc + dx_e
            dpr16, dpi16 = dpr.astype(BF16), dpi.astype(BF16)
            back = []
            for n in range(LRU_BLOCKS):
                sl = slice(n * LRU_BW, (n + 1) * LRU_BW)
                nt_dims = (((1,), (1,)), ((), ()))
                back.append(lax.dot_general(dpr16[:, sl], wa[d * LRU_BLOCKS + n], nt_dims, preferred_element_type=F32)
                            + lax.dot_general(dpi16[:, sl], wx[d * LRU_BLOCKS + n], nt_dims,
                                              preferred_element_type=F32))
                tn_dims = (((0,), (0,)), ((), ()))
                dwa.append(lax.dot_general(xb16[:, sl], dpr16[:, sl], tn_dims, preferred_element_type=F32)[None])
                dwx.append(lax.dot_general(xb16[:, sl], dpi16[:, sl], tn_dims, preferred_element_type=F32)[None])
            dxc = dxc + jnp.concatenate(back, axis=1)
            dba.append(dba_d)
            dbx.append(dbx_d)
            dsp.append(dsp_d)
        cat0 = lambda xs: jnp.concatenate(xs, axis=0)
        return [dxc], [cat0(dwa), cat0(dwx), cat0(dba), cat0(dbx), cat0(dsp)]

    (dxc,), accs = rowwise("gates_bwd", fn,
                           [(xc, 0, LRU_W), (da_f, 0, LRU_W), (du_f, 0, LRU_W), (da_b, 0, LRU_W), (du_b, 0, LRU_W)],
                           [w_a, w_x, b_a, b_x, sp], [(LRU_W, F32)],
                           [(2 * LRU_BLOCKS, LRU_BW, LRU_BW)] * 2 + [(2, LRU_W)] * 3, n_rows, t_lat, tm)
    return dxc, accs


def _rope_tables(t_lat, n_rows):
    rows = t_lat // GRID_W
    row_ids = jnp.repeat(jnp.arange(rows), GRID_W).astype(F32)
    col_ids = jnp.tile(jnp.arange(GRID_W), rows).astype(F32)
    axis_dim = QK_ROPE // 2
    inv = 1.0 / (ROPE_BASE ** (jnp.arange(0, axis_dim, 2, dtype=F32) / axis_dim))
    ang = jnp.concatenate([row_ids[:, None] * inv, col_ids[:, None] * inv], axis=-1)
    cos, sin = jnp.cos(ang), jnp.sin(ang)
    half = QK_ROPE // 2
    ones, zeros = jnp.ones((t_lat, QK_NOPE), F32), jnp.zeros((t_lat, QK_NOPE), F32)
    pad1, pad0 = jnp.ones((t_lat, HEAD_PAD - QK_DIM), F32), jnp.zeros((t_lat, HEAD_PAD - QK_DIM), F32)
    zh = jnp.zeros((t_lat, half), F32)
    c_tab = jnp.concatenate([ones, cos, cos, pad1], axis=1)
    s1 = jnp.concatenate([zeros, -sin, zh, pad0], axis=1)
    s2 = jnp.concatenate([zeros, zh, sin, pad0], axis=1)
    n_ctx = n_rows - t_lat
    c_tab = jnp.concatenate([c_tab, jnp.ones((n_ctx, HEAD_PAD), F32)], axis=0)
    s1 = jnp.concatenate([s1, jnp.zeros((n_ctx, HEAD_PAD), F32)], axis=0)
    s2 = jnp.concatenate([s2, jnp.zeros((n_ctx, HEAD_PAD), F32)], axis=0)
    return c_tab, s1, s2


def _rope(x, c, s1, s2):
    half = QK_ROPE // 2
    return x * c + pltpu.roll(x, HEAD_PAD - half, 1) * s1 + pltpu.roll(x, half, 1) * s2


def _rope_t(dy, c, s1, s2):
    half = QK_ROPE // 2
    return dy * c + pltpu.roll(dy * s1, half, 1) + pltpu.roll(dy * s2, HEAD_PAD - half, 1)


def _heads(x):
    return [x[:, h * HEAD_PAD:(h + 1) * HEAD_PAD] for h in range(N_HEADS)]


Q_SCALE = QK_DIM ** -0.5 * math.log2(math.e)


def attn_fwd(q, k, v, t_lat, n_rows, tq):
    def body(q_ref, k_ref, v_ref, o_ref, lse_ref):
        s = lax.dot_general(q_ref[...], k_ref[...], (((1,), (1,)), ((), ())), preferred_element_type=F32)
        m = jnp.max(s, axis=-1, keepdims=True)
        p = jnp.exp2(s - m)
        l = jnp.sum(p, axis=-1, keepdims=True)
        o = jnp.dot(p.astype(BF16), v_ref[...], preferred_element_type=F32) / l
        o_ref[...] = o.astype(o_ref.dtype)
        lse_ref[...] = jnp.broadcast_to(m + jnp.log2(l), lse_ref.shape)

    qspec = pl.BlockSpec((tq, HEAD_PAD), lambda h, i: (i, h))
    kspec = pl.BlockSpec((n_rows, HEAD_PAD), lambda h, i: (0, h))
    return pl.pallas_call(
        body, name="attn_fwd", grid=(N_HEADS, t_lat // tq),
        out_shape=[jax.ShapeDtypeStruct((t_lat, N_HEADS * HEAD_PAD), BF16),
                   jax.ShapeDtypeStruct((t_lat, N_HEADS * HEAD_PAD), F32)],
        in_specs=[qspec, kspec, kspec], out_specs=[qspec, qspec],
        compiler_params=_cparams(("parallel", "arbitrary")),
    )(q, k, v)


def attn_bwd(q, k, v, o, do, lse, t_lat, n_rows, tq):
    scale = QK_DIM ** -0.5
    nq = t_lat // tq
    nt = (((1,), (1,)), ((), ()))
    tn = (((0,), (0,)), ((), ()))

    def body(q_ref, k_ref, v_ref, o_ref, do_ref, lse_ref, dq_ref, dk_ref, dv_ref):
        @pl.when(pl.program_id(1) == 0)
        def _():
            dk_ref[...] = jnp.zeros_like(dk_ref)
            dv_ref[...] = jnp.zeros_like(dv_ref)

        qv, kv, vv, dov = q_ref[...], k_ref[...], v_ref[...], do_ref[...]
        s = lax.dot_general(qv, kv, nt, preferred_element_type=F32)
        p = jnp.exp2(s - lse_ref[:, 0:1])
        dv_ref[...] += lax.dot_general(p.astype(BF16), dov, tn, preferred_element_type=F32)
        dp = lax.dot_general(dov, vv, nt, preferred_element_type=F32)
        delta = jnp.sum(dov.astype(F32) * o_ref[...].astype(F32), axis=-1, keepdims=True)
        ds = (p * (dp - delta)).astype(BF16)
        dq_ref[...] = jnp.dot(ds, kv, preferred_element_type=F32) * scale
        dk_ref[...] += lax.dot_general(ds, qv, tn, preferred_element_type=F32)

        @pl.when(pl.program_id(1) == nq - 1)
        def _():
            dk_ref[...] = dk_ref[...] * (scale / Q_SCALE)

    qspec = pl.BlockSpec((tq, HEAD_PAD), lambda h, i: (i, h))
    kspec = pl.BlockSpec((n_rows, HEAD_PAD), lambda h, i: (0, h))
    return pl.pallas_call(
        body, name="attn_bwd", grid=(N_HEADS, t_lat // tq),
        out_shape=[jax.ShapeDtypeStruct((t_lat, N_HEADS * HEAD_PAD), F32),
                   jax.ShapeDtypeStruct((n_rows, N_HEADS * HEAD_PAD), F32),
                   jax.ShapeDtypeStruct((n_rows, N_HEADS * HEAD_PAD), F32)],
        in_specs=[qspec, kspec, kspec, qspec, qspec, qspec], out_specs=[qspec, kspec, kspec],
        compiler_params=_cparams(("parallel", "arbitrary")),
    )(q, k, v, o, do, lse)


def adamw(name, w, g, m, v):
    r, ccols = w.shape
    tr = _best_div(r, 8, max(8, 262144 // ccols)) if r % 8 == 0 else r
    c1 = 1.0 - ADAM_B1 ** ADAM_STEP
    c2 = 1.0 - ADAM_B2 ** ADAM_STEP

    def body(w_ref, g_ref, m_ref, v_ref, d_ref, nm_ref, nv_ref):
        gv = g_ref[...]
        nm = ADAM_B1 * m_ref[...] + (1.0 - ADAM_B1) * gv
        nv = ADAM_B2 * v_ref[...] + (1.0 - ADAM_B2) * (gv * gv)
        d_ref[...] = -ADAM_LR * ((nm / c1) / (jnp.sqrt(nv / c2) + ADAM_EPS) + ADAM_WD * w_ref[...])
        nm_ref[...] = nm
        nv_ref[...] = nv

    spec = pl.BlockSpec((tr, ccols), lambda i: (i, 0))
    return pl.pallas_call(
        body, name=name, grid=(r // tr,),
        out_shape=[jax.ShapeDtypeStruct((r, ccols), F32)] * 3,
        in_specs=[spec] * 4, out_specs=[spec] * 3,
        compiler_params=_cparams(("parallel",)),
    )(w, g, m, v)


def adamw_many(name, ws, gs, ms, vs):
    n = len(ws)
    c1 = 1.0 - ADAM_B1 ** ADAM_STEP
    c2 = 1.0 - ADAM_B2 ** ADAM_STEP

    def body(*refs):
        for i in range(n):
            w_ref, g_ref, m_ref, v_ref = (refs[k * n + i] for k in range(4))
            d_ref, nm_ref, nv_ref = (refs[(4 + k) * n + i] for k in range(3))
            gv = g_ref[...]
            nm = ADAM_B1 * m_ref[...] + (1.0 - ADAM_B1) * gv
            nv = ADAM_B2 * v_ref[...] + (1.0 - ADAM_B2) * (gv * gv)
            d_ref[...] = -ADAM_LR * ((nm / c1) / (jnp.sqrt(nv / c2) + ADAM_EPS) + ADAM_WD * w_ref[...])
            nm_ref[...] = nm
            nv_ref[...] = nv

    vmem = pl.BlockSpec(memory_space=pltpu.VMEM)
    res = pl.pallas_call(
        body, name=name,
        out_shape=[jax.ShapeDtypeStruct(w.shape, F32) for w in ws] * 3,
        in_specs=[vmem] * (4 * n), out_specs=[vmem] * (3 * n),
        compiler_params=_cparams(),
    )(*ws, *gs, *ms, *vs)
    return [tuple(res[k * n + i] for k in range(3)) for i in range(n)]


def _flat(parts, dtype, row_mult):
    v = jnp.concatenate([p.reshape(-1).astype(dtype) for p in parts])
    quantum = row_mult * FLAT_C
    total = -(-v.shape[0] // quantum) * quantum
    return jnp.pad(v, (0, total - v.shape[0])).reshape(total // FLAT_C, FLAT_C)


def _unflat(flat, shapes):
    v = flat.reshape(-1)
    out, at = [], 0
    for s in shapes:
        n = math.prod(s)
        out.append(v[at:at + n].reshape(s))
        at += n
    return out


def _gathered_to_full(name, g):
    k = g.shape[1]
    return jnp.transpose(g, (1, 0, 2)).reshape(k, N_DEV * g.shape[2])


def _full_to_chunks(name, full):
    k, n = full.shape
    return jnp.transpose(full.reshape(k, N_DEV, n // N_DEV), (1, 0, 2)).reshape(N_DEV, -1)


def _shard_to_rb(name, w):
    return w if name in ROW_SHARDED else w.T


def _rb_to_shard(name, g):
    return g if name in ROW_SHARDED else g.T


def _rb_from_gathered(name, g):
    cols = g.shape[2]
    if name == 'w_in':
        z = lambda k: jnp.zeros((k, cols), g.dtype)
        full = g.reshape(N_DEV * g.shape[1], cols)
        return jnp.concatenate([full[:Z_KR], z(QK_NOPE), full[Z_KR:Z_KR + QK_ROPE], z(HEAD_PAD - QK_DIM),
                                full[Z_KR + QK_ROPE:]], axis=0)
    if name == 'w_uq':
        return jnp.pad(g, ((0, 0), (0, HEAD_PAD - QK_DIM), (0, 0))).reshape(N_HEADS * HEAD_PAD, cols)
    if name == 'w_ukv':
        pad = lambda t: jnp.pad(t, ((0, 0), (0, HEAD_PAD - t.shape[1]), (0, 0))).reshape(N_HEADS * HEAD_PAD, cols)
        return jnp.concatenate([pad(g[:, :QK_NOPE]), pad(g[:, QK_NOPE:])], axis=0)
    if name == 'w_o_attn':
        full = g.reshape(D, N_HEADS, V_HEAD)
        return jnp.pad(full, ((0, 0), (0, 0), (0, HEAD_PAD - V_HEAD))).reshape(D, N_HEADS * HEAD_PAD)
    return g.reshape(N_DEV * g.shape[1], cols)


def _chunks_from_rb_grad(name, g):
    cols = g.shape[1]
    if name == 'w_in':
        full = jnp.concatenate([g[:Z_KR], g[Z_KR + QK_NOPE:Z_KR + QK_DIM], g[Z_XB:]], axis=0)
        return full.reshape(N_DEV, -1, cols)
    if name == 'w_uq':
        return g.reshape(N_HEADS, HEAD_PAD, cols)[:, :QK_DIM]
    if name == 'w_ukv':
        half = N_HEADS * HEAD_PAD
        gk = g[:half].reshape(N_HEADS, HEAD_PAD, cols)[:, :QK_NOPE]
        gv = g[half:].reshape(N_HEADS, HEAD_PAD, cols)[:, :V_HEAD]
        return jnp.concatenate([gk, gv], axis=1)
    if name == 'w_o_attn':
        full = g.reshape(D, N_HEADS, HEAD_PAD)[:, :, :V_HEAD].reshape(D, N_HEADS * V_HEAD)
        return full.reshape(N_DEV, D // N_DEV, N_HEADS * V_HEAD)
    return g.reshape(N_DEV, -1, cols)


def local_step(x, ctx, target, mod_l, mod_c, wt, on_grad=None, arrive=None):
    t_lat, n_ctx = x.shape[0], ctx.shape[0]
    n = t_lat + n_ctx
    tm = _pick(math.gcd(t_lat, n), (256, 128))
    tq_fwd = _pick(t_lat, (256, 128))
    tq_bwd = _pick(t_lat, (512, 256, 128))
    row = lambda v: v.reshape(1, -1).astype(F32)
    two = lambda a, b: jnp.stack([a, b]).astype(F32)
    sh1_l, sc1_l, g1_l, sh2_l, sc2_l, g2_l = jnp.split(mod_l, 6)
    sh1_c, sc1_c = jnp.split(mod_c, 6)[:2]
    sc1, sh1 = two(sc1_l, sc1_c), two(sh1_l, sh1_c)
    g1, g2, sc2, sh2 = row(g1_l), row(g2_l), row(sc2_l), row(sh2_l)
    norm1_g, norm2_g, final_g = row(wt['norm1_g']), row(wt['norm2_g']), row(wt['final_g'])
    q_g, kv_g, b_gate = row(wt['q_norm_g']), row(wt['kv_norm_g']), row(wt['b_gate'])
    wt = dict(wt)
    pending = []

    def sent():
        tokens = list(pending)
        pending.clear()
        return tokens

    def need(names, after):
        if arrive is not None:
            got = arrive(names, after)
            if '_token' in got:
                pending.append(got.pop('_token'))
            wt.update(got)
        return [wt[n] for n in names]
    lru_w_a = wt['lru_w_a'].reshape(2 * LRU_BLOCKS, LRU_BW, LRU_BW).astype(BF16)
    lru_w_x = wt['lru_w_x'].reshape(2 * LRU_BLOCKS, LRU_BW, LRU_BW).astype(BF16)
    b_a, b_x, lam = wt['lru_b_a'], wt['lru_b_x'], wt['lru_lambda']
    sp = jnp.logaddexp(-lam, 0.0)
    c_tab, s1_tab, s2_tab = _rope_tables(t_lat, n)
    rw = functools.partial(rowwise, n_rows=n, t_lat=t_lat, tm=tm)
    rw_lat = functools.partial(rowwise, n_rows=t_lat, t_lat=t_lat, tm=tm)

    xs = jnp.concatenate([x, ctx], axis=0)

    def f_norm1(is_ctx, rows, params):
        (xv,), (g, sc, sh) = rows, params
        return [_norm_mod(xv, g, _sel(is_ctx, sc), _sel(is_ctx, sh))], []

    (h,), _ = rw("norm1", f_norm1, [(xs, 0, D)], [norm1_g, sc1, sh1], [(D, BF16)], [])
    (w_in_t,) = need(('w_in',), h)
    z = matmul("w_in", h, w_in_t, 'nt', BF16, after=sent())
    w_uq_t, w_ukv_t, w_o_lru = need(('w_uq', 'w_ukv', 'w_o_lru'), z)

    def f_qkv_norm(is_ctx, rows, params):
        (ql, kvl), (gq, gkv) = rows, params
        return [_rms(ql, gq), _rms(kvl, gkv)], []

    (qn, kvn), _ = rw("qkv_norm", f_qkv_norm, [(z, Z_Q, Q_RANK), (z, Z_KV, KV_RANK)], [q_g, kv_g],
                      [(Q_RANK, BF16), (KV_RANK, BF16)], [])
    qp = matmul("w_uq", qn, w_uq_t, 'nt', BF16)
    kvp = matmul("w_ukv", kvn, w_ukv_t, 'nt', BF16)

    def f_rope(is_ctx, rows, params):
        qv, kk, vv, kr, c, s1, s2 = rows
        krr = _rope(kr, c, s1, s2)
        qo = jnp.concatenate([_rope(qh, c, s1, s2) for qh in _heads(qv)], axis=1) * Q_SCALE
        ko = jnp.concatenate([kh + krr for kh in _heads(kk)], axis=1)
        return [qo, ko, vv], []

    hp = N_HEADS * HEAD_PAD
    (qr, kr_, vr), _ = rw("rope", f_rope,
                          [(qp, 0, hp), (kvp, 0, hp), (kvp, hp, hp), (z, Z_KR, HEAD_PAD), (c_tab, 0, HEAD_PAD),
                           (s1_tab, 0, HEAD_PAD), (s2_tab, 0, HEAD_PAD)], [], [(hp, BF16)] * 3, [])
    attn, lse = attn_fwd(qr, kr_, vr, t_lat, n, tq_fwd)

    xc = conv_fwd("lru_conv", z, Z_XB, LRU_W, wt['lru_conv_w'], row(wt['lru_conv_b']), 2, n, t_lat, F32)
    a_f, u_f, a_b, u_b = gates_fwd(xc, lru_w_a, lru_w_x, b_a, b_x, sp, n, t_lat, tm)
    h_f, hp_f = scan_fwd("scan_f", a_f, u_f, 'f', n, t_lat)
    h_b, hp_b = scan_fwd("scan_b", a_b, u_b, 'b', n, t_lat)

    def f_lru_out(is_ctx, rows, params):
        hf, hb, yb = rows
        return [(hf + hb) * _gelu(yb)], []

    (ybin,), _ = rw_lat("lru_out", f_lru_out, [(h_f, 0, LRU_W), (h_b, 0, LRU_W), (z, Z_YB, LRU_W)], [],
                        [(LRU_W, BF16)], [])
    w_o_attn_t, w_out, w_up_t, w_down = need(('w_o_attn', 'w_out', 'w_up', 'w_down'), attn)
    y_a = matmul("w_o_attn", attn, w_o_attn_t, 'nt', BF16)
    y_b = matmul("w_o_lru", ybin, w_o_lru, 'nn', BF16)

    def _merge(ya, yb, gl, bg):
        gates = _sigmoid(gl + bg)
        return gates[:, :D] * ya + gates[:, D:] * yb

    def f_merge(is_ctx, rows, params):
        (ya, yb, gl), (bg,) = rows, params
        return [_merge(ya, yb, gl, bg)], []

    (mrg,), _ = rw_lat("merge", f_merge, [(y_a, 0, D), (y_b, 0, D), (z, Z_GL, 2 * D)], [b_gate], [(D, BF16)], [])
    o = matmul("w_out", mrg, w_out, 'nn', BF16)

    def _res_norm2(xv, ov, g1v, g, sc, sh):
        x1 = xv + g1v * ov
        return x1, _norm_mod(x1, g, sc, sh)

    def f_norm2(is_ctx, rows, params):
        (xv, ov), (g1v, g, sc, sh) = rows, params
        x1, h2v = _res_norm2(xv, ov, g1v, g, sc, sh)
        return [x1, h2v], []

    (x1, h2), _ = rw_lat("norm2", f_norm2, [(x, 0, D), (o, 0, D)], [g1, norm2_g, sc2, sh2], [(D, F32), (D, BF16)], [])
    u = matmul("w_up", h2, w_up_t, 'nt', BF16)
    f = ffn_mix_fwd(u, wt['ffn_conv_w'], row(wt['ffn_conv_b']), t_lat)
    dn = matmul("w_down", f, w_down, 'nn', BF16)

    def _tile_loss(x1v, dv, g2v, fg, tgt):
        y = _rms(x1v + g2v * dv, fg)
        e = y - tgt
        return 0.5 * jnp.sum(jnp.mean(e * e, axis=-1, keepdims=True), axis=0, keepdims=True)

    def f_final(is_ctx, rows, params):
        (x1v, dv, tgt), (g2v, fg) = rows, params
        lv, vjp = jax.vjp(lambda a, b, c, d: _tile_loss(a, b, c, d, tgt), x1v, dv, g2v, fg)
        dx2, dd, dg2, dfg = vjp(jnp.ones((1, 1), F32))
        return [dx2, dd], [dg2, dfg, jnp.broadcast_to(lv, (1, 128))]

    (dx2, dd), (dg2, dfinal_g, loss_v) = rw_lat("final", f_final, [(x1, 0, D), (dn, 0, D), (target, 0, D)],
                                                [g2, final_g], [(D, F32), (D, BF16)], [(1, D), (1, D), (1, 128)])
    loss = loss_v[0, 0]

    grads = {'final_g': dfinal_g}

    def put(name, g):
        grads[name] = g
        if on_grad is not None:
            pending.append(on_grad(name, g))
    df = matmul("d_f", dd, w_down, 'nt', BF16)
    put('w_down', matmul("g_w_down", f, dd, 'tn', BF16))

    da, dgate, grads['ffn_conv_w'], grads['ffn_conv_b'] = ffn_mix_bwd(u, df, wt['ffn_conv_w'],
                                                                      row(wt['ffn_conv_b']), t_lat)
    du = jnp.concatenate([da, dgate], axis=1)
    dh2 = matmul("d_h2", du, w_up_t, 'nn', BF16, after=sent())
    put('w_up', matmul("g_w_up", du, h2, 'tn', BF16))

    def b_norm2(is_ctx, rows, params):
        (xv, ov, dh2v, dx2v), (g1v, g, sc, sh) = rows, params
        _, vjp = jax.vjp(_res_norm2, xv, ov, g1v, g, sc, sh)
        dx, do, dg1v, dg, dsc, dsh = vjp((dx2v, dh2v))
        return [dx, do], [dg1v, dg, dsc, dsh]

    (dx_res, do), (dg1, dnorm2_g, dsc2, dsh2) = rw_lat(
        "norm2_bwd", b_norm2, [(x, 0, D), (o, 0, D), (dh2, 0, D), (dx2, 0, D)], [g1, norm2_g, sc2, sh2],
        [(D, F32), (D, BF16)], [(1, D)] * 4)
    grads['norm2_g'] = dnorm2_g
    dmrg = matmul("d_merge", do, w_out, 'nt', BF16, after=sent())
    put('w_out', matmul("g_w_out", mrg, do, 'tn', BF16))

    def b_merge(is_ctx, rows, params):
        (ya, yb, gl, dm), (bg,) = rows, params
        _, vjp = jax.vjp(_merge, ya, yb, gl, bg)
        dya, dyb, dgl, dbg = vjp(dm)
        return [dya, dyb, dgl], [dbg]

    (dy_a, dy_b, dgl), (grads['b_gate'],) = rw_lat(
        "merge_bwd", b_merge, [(y_a, 0, D), (y_b, 0, D), (z, Z_GL, 2 * D), (dmrg, 0, D)], [b_gate],
        [(D, BF16), (D, BF16), (2 * D, BF16)], [(1, 2 * D)])
    dattn = matmul("d_attn", dy_a, w_o_attn_t, 'nn', BF16, after=sent())
    put('w_o_attn', matmul("g_w_o_attn", dy_a, attn, 'tn', BF16))
    dybin = matmul("d_lru_out", dy_b, w_o_lru, 'nt', BF16, after=sent())
    put('w_o_lru', matmul("g_w_o_lru", ybin, dy_b, 'tn', BF16))

    def b_lru_out(is_ctx, rows, params):
        hf, hb, yb, dyv = rows
        _, vjp = jax.vjp(lambda s, y: s * _gelu(y), hf + hb, yb)
        dh, dyb = vjp(dyv)
        return [dh, dyb], []

    (dh_lru, dyb), _ = rw_lat("lru_out_bwd", b_lru_out,
                              [(h_f, 0, LRU_W), (h_b, 0, LRU_W), (z, Z_YB, LRU_W), (dybin, 0, LRU_W)], [],
                              [(LRU_W, F32), (LRU_W, BF16)], [])
    du_f, da_f = scan_adj("scan_f_adj", a_f, dh_lru, hp_f, 'f', n, t_lat)
    du_b, da_b = scan_adj("scan_b_adj", a_b, dh_lru, hp_b, 'b', n, t_lat)
    dxc, (dw_a, dw_x, db_a, db_x, dsp) = gates_bwd(xc, da_f, du_f, da_b, du_b, lru_w_a, lru_w_x, b_a, b_x, sp,
                                                   n, t_lat, tm)
    put('lru_w_a', dw_a.reshape(2 * LRU_BLOCKS * LRU_BW, LRU_BW).astype(BF16))
    put('lru_w_x', dw_x.reshape(2 * LRU_BLOCKS * LRU_BW, LRU_BW).astype(BF16))
    grads['lru_b_a'], grads['lru_b_x'] = db_a, db_x
    grads['lru_lambda'] = -dsp * _sigmoid(-lam)
    dxb, grads['lru_conv_w'], grads['lru_conv_b'] = conv_bwd("lru_conv_bwd", dxc, z, Z_XB, LRU_W, wt['lru_conv_w'],
                                                             2, n, t_lat)

    dq, dk, dv = attn_bwd(qr, kr_, vr, attn, dattn, lse, t_lat, n, tq_bwd)

    def b_rope(is_ctx, rows, params):
        dqv, dkv, dvv, c, s1, s2 = rows
        live = jnp.where(is_ctx, 0.0, 1.0)
        dqo = jnp.concatenate([_rope_t(dqh, c, s1, s2) for dqh in _heads(dqv)], axis=1) * live
        dkh = _heads(dkv)
        dkr = dkh[0]
        for t in dkh[1:]:
            dkr = dkr + t
        lanes = lax.broadcasted_iota(jnp.int32, dkr.shape, 1)
        dkr = jnp.where((lanes >= QK_NOPE) & (lanes < QK_DIM), _rope_t(dkr, c, s1, s2), 0.0)
        return [dqo, jnp.concatenate([dkv, dvv], axis=1), dkr], []

    (dqp, dkvp, dkr), _ = rw("rope_bwd", b_rope,
                             [(dq, 0, hp), (dk, 0, hp), (dv, 0, hp), (c_tab, 0, HEAD_PAD), (s1_tab, 0, HEAD_PAD),
                              (s2_tab, 0, HEAD_PAD)], [], [(hp, BF16), (2 * hp, BF16), (HEAD_PAD, BF16)], [])
    dqn = matmul("d_qn", dqp, w_uq_t, 'nn', BF16, after=sent())
    put('w_uq', matmul("g_w_uq", dqp, qn, 'tn', BF16))
    dkvn = matmul("d_kvn", dkvp, w_ukv_t, 'nn', BF16, after=sent())
    put('w_ukv', matmul("g_w_ukv", dkvp, kvn, 'tn', BF16))

    def b_qkv_norm(is_ctx, rows, params):
        (ql, kvl, dqv, dkvv), (gq, gkv) = rows, params
        _, vjp_q = jax.vjp(_rms, ql, gq)
        _, vjp_kv = jax.vjp(_rms, kvl, gkv)
        dql, dgq = vjp_q(dqv)
        dkvl, dgkv = vjp_kv(dkvv)
        return [dql, dkvl], [dgq, dgkv]

    (dq_lat, dkv_lat), (grads['q_norm_g'], grads['kv_norm_g']) = rw(
        "qkv_norm_bwd", b_qkv_norm, [(z, Z_Q, Q_RANK), (z, Z_KV, KV_RANK), (dqn, 0, Q_RANK), (dkvn, 0, KV_RANK)],
        [q_g, kv_g], [(Q_RANK, BF16), (KV_RANK, BF16)], [(1, Q_RANK), (1, KV_RANK)])
    pad_ctx = lambda t: jnp.pad(t, ((0, n_ctx), (0, 0)))
    dz = jnp.concatenate([dq_lat, dkv_lat, dkr, dxb, pad_ctx(dyb), pad_ctx(dgl)], axis=1)
    put('w_in', matmul("g_w_in", dz, h, 'tn', BF16))
    dh = matmul("d_h", dz, w_in_t, 'nn', BF16, after=sent())

    def b_norm1(is_ctx, rows, params):
        (xv, dhv, dxr), (g, sc, sh) = rows, params
        scv, shv = _sel(is_ctx, sc), _sel(is_ctx, sh)
        _, vjp = jax.vjp(_norm_mod, xv, g, scv, shv)
        dx, dg, dsc, dsh = vjp(dhv)
        return [dx + dxr], [dg, _seg_acc(is_ctx, dsc), _seg_acc(is_ctx, dsh)]

    (dxs,), (grads['norm1_g'], dsc1, dsh1) = rw("norm1_bwd", b_norm1, [(xs, 0, D), (dh, 0, D), (dx_res, 0, D)],
                                                [norm1_g, sc1, sh1], [(D, F32)], [(1, D), (2, D), (2, D)])
    grad_x = dxs[:t_lat]
    zero = jnp.zeros((D,), F32)
    dmod_l = jnp.concatenate([dsh1[0], dsc1[0], dg1[0], dsh2[0], dsc2[0], dg2[0]])
    dmod_c = jnp.concatenate([dsh1[1], dsc1[1], zero, zero, zero, zero])
    return loss, grad_x, grads, dmod_l, dmod_c


def kernel(x, c, ctx, c_ctx, w_mod, b_mod, norm1_g, w_in, b_gate, q_norm_g, kv_norm_g, w_uq, w_ukv, w_o_attn, lru_conv_w, lru_conv_b, lru_w_a, lru_b_a, lru_w_x, lru_b_x, lru_lambda, w_o_lru, w_out, norm2_g, w_up, ffn_conv_w, ffn_conv_b, w_down, final_g, loss_target, m_c_ctx, m_w_mod, m_b_mod, m_norm1_g, m_w_in, m_b_gate, m_q_norm_g, m_kv_norm_g, m_w_uq, m_w_ukv, m_w_o_attn, m_lru_conv_w, m_lru_conv_b, m_lru_w_a, m_lru_b_a, m_lru_w_x, m_lru_b_x, m_lru_lambda, m_w_o_lru, m_w_out, m_norm2_g, m_w_up, m_ffn_conv_w, m_ffn_conv_b, m_w_down, m_final_g, v_c_ctx, v_w_mod, v_b_mod, v_norm1_g, v_w_in, v_b_gate, v_q_norm_g, v_kv_norm_g, v_w_uq, v_w_ukv, v_w_o_attn, v_lru_conv_w, v_lru_conv_b, v_lru_w_a, v_lru_b_a, v_lru_w_x, v_lru_b_x, v_lru_lambda, v_w_o_lru, v_w_out, v_norm2_g, v_w_up, v_ffn_conv_w, v_ffn_conv_b, v_w_down, v_final_g):
    given = dict(locals())
    strip = lambda name, a: a if name in ('c_ctx', 'final_g') else a[0]
    wsh = {n: strip(n, given[n]) for n in WEIGHTS}
    msh = {n: strip(n, given['m_' + n]) for n in WEIGHTS}
    vsh = {n: strip(n, given['v_' + n]) for n in WEIGHTS}
    me = _my_index()

    small = _flat([c[0]] + [wsh[n] for n in SMALL_F32], F32, 8)
    small_all = all_gather("gather_small", small).reshape(N_DEV, -1)
    c_all = small_all[:, :D]
    full, at = {}, D
    for n in SMALL_F32:
        cnt = math.prod(wsh[n].shape)
        full[n] = _gathered_to_full(n, small_all[:, at:at + cnt].reshape((N_DEV,) + wsh[n].shape))
        at += cnt

    cond = jnp.concatenate([c_all, c_ctx[None], jnp.zeros((7, D), F32)], axis=0)
    sil = cond * jax.nn.sigmoid(cond)
    mod_cols = matmul("mod_proj", sil, wsh['w_mod'], 'nn', F32)
    mod_all = all_gather("gather_mod", mod_cols)
    mod_all = jnp.transpose(mod_all, (1, 0, 2)).reshape(16, 6 * D) + b_mod[0][None]
    mod_l = lax.dynamic_index_in_dim(mod_all, me, axis=0, keepdims=False)
    mod_c = mod_all[N_DEV]

    rb_shards = {n: _shard_to_rb(n, wsh[n]).astype(BF16) for n in BIG_BF16}
    (w_in_blocks,) = all_gather_multi("gather_w_in", [rb_shards['w_in']])
    later = [n for n in BIG_BF16 if n != 'w_in']
    weights_started, weights_sent = exchange_start("weights_send", 'gather', [rb_shards[n] for n in later],
                                                   after=[w_in_blocks, mod_all])
    for n in REPLICATED:
        if n not in ('c_ctx', 'b_mod'):
            full[n] = wsh[n]

    def arrive(names, after):
        if names == ('w_in',):
            return {'w_in': _rb_from_gathered('w_in', w_in_blocks), '_token': weights_sent}
        picked = [later.index(n) for n in names]
        lands = exchange_wait("weights_wait_" + names[0], 'gather',
                              tuple([part[i] for i in picked] for part in weights_started), after)
        return {n: _rb_from_gathered(n, lax.dynamic_update_slice_in_dim(land, rb_shards[n][None], me, axis=0))
                for n, land in zip(names, lands)}

    in_flight = {}

    def on_grad(n, g):
        chunks = _chunks_from_rb_grad(n, g)
        own = lax.dynamic_index_in_dim(chunks, me, axis=0, keepdims=True)
        started, token = exchange_start("grad_send_" + n, 'scatter', [chunks])
        in_flight[n] = (own, started)
        return token

    loss, grad_x, grads, dmod_l, dmod_c = local_step(x[0], ctx[0], loss_target[0], mod_l, mod_c, full, on_grad,
                                                     arrive)
    loss = lax.psum(loss, ("x", "y", "c"))

    dmod = jnp.stack([dmod_l, dmod_c]).reshape(2 * 6 * D // FLAT_C, FLAT_C)
    dm = all_gather("gather_dmod", dmod).reshape(N_DEV, 2, 6 * D)
    dmod_c_tot = dm[0, 1]
    for p in range(1, N_DEV):
        dmod_c_tot = dmod_c_tot + dm[p, 1]
    dm16 = jnp.concatenate([dm[:, 0], dmod_c_tot[None], jnp.zeros((7, 6 * D), F32)], axis=0)
    ncol = 6 * D // N_DEV
    dm16_cols = lax.dynamic_slice_in_dim(dm16.reshape(16, N_DEV, ncol), me, 1, axis=1)[:, 0]
    grad_w_mod = matmul("g_w_mod", sil, dm16_cols, 'tn', F32)
    dsil = matmul("d_cond", dm16_cols, wsh['w_mod'], 'nt', F32)
    sg = jax.nn.sigmoid(c_ctx)
    grads['c_ctx'] = dsil[N_DEV] * (sg * (1.0 + c_ctx * (1.0 - sg)))
    grads['b_mod'] = dmod_l + dmod_c

    g_final = {'w_mod': grad_w_mod}
    reduced = {}
    for n in BIG_BF16 + ['lru_w_a', 'lru_w_x']:
        own, started = in_flight[n]
        (land,) = exchange_wait("grad_wait_" + n, 'scatter', started, dm)
        reduced[n] = sum_slots("sum_" + n, lax.dynamic_update_slice_in_dim(land, own, me, axis=0))
    for n in BIG_BF16:
        g_final[n] = _rb_to_shard(n, reduced[n])

    small_names = SMALL_F32 + [n for n in REPLICATED if n not in ('lru_w_a', 'lru_w_x')]
    partials = _flat([grads[n] for n in small_names], F32, 8)
    parts_all, a_all, x_all = all_gather_multi("gather_small_grads", [partials, reduced['lru_w_a'], reduced['lru_w_x']])
    small_sum = sum_slots("sum_small", parts_all).reshape(-1)
    g_final['lru_w_a'], g_final['lru_w_x'] = a_all.reshape(wsh['lru_w_a'].shape), x_all.reshape(wsh['lru_w_x'].shape)
    at = 0
    for n in small_names:
        cnt = math.prod(full[n].shape) if n in SMALL_F32 else math.prod(wsh[n].shape)
        g = small_sum[at:at + cnt]
        if n in SMALL_F32:
            k = full[n].shape[0]
            g = lax.dynamic_index_in_dim(g.reshape(k, N_DEV, -1), me, axis=1, keepdims=False)
        g_final[n] = g.reshape(wsh[n].shape)
        at += cnt

    stepped = {n: adamw("adamw_" + n, wsh[n], g_final[n], msh[n], vsh[n]) for n in ['w_mod'] + BIG_BF16}
    rest = [n for n in WEIGHTS if n not in stepped]
    as2d = lambda a: a.reshape(-1, a.shape[-1])
    rest_out = adamw_many("adamw_small", *[[as2d(d[n]) for n in rest] for d in (wsh, g_final, msh, vsh)])
    stepped.update(zip(rest, rest_out))
    shaped = lambda n, a: a.reshape(given[n].shape)
    return (loss, grad_x[None],
            *[shaped(n, g_final[n]) for n in WEIGHTS],
            *[shaped(n, stepped[n][k]) for k in range(3) for n in WEIGHTS])
```

```python
import functools
import math

import jax
import jax.numpy as jnp
from jax import lax
from jax.experimental import pallas as pl
from jax.experimental.pallas import tpu as pltpu

F32 = jnp.float32
BF16 = jnp.bfloat16
MESH = pl.DeviceIdType.MESH

N_DEV = 8
D = 1024
N_HEADS = 8
HEAD_PAD = 128
QK_NOPE, QK_ROPE, V_HEAD = 64, 32, 64
QK_DIM = QK_NOPE + QK_ROPE
Q_RANK, KV_RANK = 384, 256
LRU_W, LRU_BLOCKS, LRU_BW = 1280, 10, 128
FFN = 2816
GRID_W = 64
ROPE_BASE = 10000.0
LRU_C = 8.0
EPS = 1e-6
Z_Q, Z_KV, Z_KR, Z_XB, Z_YB, Z_GL, Z_END = 0, 384, 640, 768, 2048, 3328, 5376
ADAM_LR, ADAM_B1, ADAM_B2, ADAM_EPS, ADAM_WD, ADAM_STEP = 0.001, 0.9, 0.999, 1e-08, 0.01, 10

VMEM_LIMIT = 52 * 1024 * 1024
FLAT_C = 512
BIG_ROWS = 256

WEIGHTS = ['c_ctx', 'w_mod', 'b_mod', 'norm1_g', 'w_in', 'b_gate', 'q_norm_g', 'kv_norm_g', 'w_uq', 'w_ukv',
           'w_o_attn', 'lru_conv_w', 'lru_conv_b', 'lru_w_a', 'lru_b_a', 'lru_w_x', 'lru_b_x', 'lru_lambda',
           'w_o_lru', 'w_out', 'norm2_g', 'w_up', 'ffn_conv_w', 'ffn_conv_b', 'w_down', 'final_g']
COL_SHARDED = ['w_in', 'w_uq', 'w_ukv', 'w_o_attn', 'lru_conv_w', 'lru_b_a', 'lru_b_x', 'lru_lambda', 'w_up',
               'ffn_conv_w']
ROW_SHARDED = ['w_o_lru', 'w_out', 'w_down']
BIG_BF16 = ['w_in', 'w_uq', 'w_ukv', 'w_o_attn', 'w_o_lru', 'w_out', 'w_up', 'w_down']
SMALL_F32 = ['lru_conv_w', 'lru_b_a', 'lru_b_x', 'lru_lambda', 'ffn_conv_w']
SHARDED = BIG_BF16 + SMALL_F32
REPLICATED = ['c_ctx', 'b_mod', 'norm1_g', 'b_gate', 'q_norm_g', 'kv_norm_g', 'lru_conv_b', 'lru_w_a', 'lru_w_x',
              'norm2_g', 'ffn_conv_b', 'final_g']


def _cparams(sem=None):
    return pltpu.CompilerParams(dimension_semantics=sem, vmem_limit_bytes=VMEM_LIMIT)


def _pick(n, cands):
    for c in cands:
        if c <= n and n % c == 0:
            return c
    return n


def _best_div(n, mult, cap):
    best = mult
    for d in range(mult, min(n, cap) + 1, mult):
        if n % d == 0:
            best = d
    return best


ROW_TILES = (1088, 1024, 544, 512, 256, 128, 64, 32, 16, 8)
LANE_TILES = (1408, 1024, 896, 768, 640, 512, 384, 256, 128)


def _my_pos():
    return lax.axis_index("x"), lax.axis_index("y"), lax.axis_index("c")


def _my_index():
    x, y, c = _my_pos()
    return 4 * x + 2 * y + c


def all_gather_multi(name, shards):
    n_arr = len(shards)
    arrays = range(n_arr)

    def body(*refs):
        x_refs, out_refs = refs[:n_arr], refs[n_arr:2 * n_arr]
        send_sems, recv_sems, local_sems = refs[2 * n_arr:]
        x, y, c = _my_pos()
        me, sibling = (x, y, c), (x, y, 1 - c)
        chips = [(1 - x, y), (x, 1 - y), (1 - x, 1 - y)]

        def slot(a, px, py, pc):
            return out_refs[a].at[4 * px + 2 * py + pc]

        def copy(a, k, block, to, src=None):
            return pltpu.make_async_remote_copy(
                src_ref=slot(a, *block) if src is None else src, dst_ref=slot(a, *block),
                send_sem=send_sems.at[7 * a + k], recv_sem=recv_sems.at[7 * a + k], device_id=to,
                device_id_type=MESH)

        mine = [pltpu.make_async_copy(x_refs[a], slot(a, *me), local_sems.at[a]) for a in arrays]
        first = [copy(a, 1 + j, me, (*chip, c), src=x_refs[a]) for j, chip in enumerate(chips) for a in arrays]
        first += [copy(a, 0, me, sibling, src=x_refs[a]) for a in arrays]
        for cp in first + mine:
            cp.start()
        passed = []
        for j, chip in enumerate(chips):
            for a in arrays:
                copy(a, 1 + j, (*chip, c), me).wait_recv()
                passed.append(copy(a, 4 + j, (*chip, c), sibling))
                passed[-1].start()
        for a in arrays:
            copy(a, 0, sibling, me).wait_recv()
            for j, chip in enumerate(chips):
                copy(a, 4 + j, (*chip, 1 - c), me).wait_recv()
        for cp in first + passed:
            cp.wait_send()
        for cp in mine:
            cp.wait()

    hbm = pl.BlockSpec(memory_space=pl.ANY)
    return pl.pallas_call(
        body, name=name,
        out_shape=[jax.ShapeDtypeStruct((N_DEV,) + s.shape, s.dtype) for s in shards],
        in_specs=[hbm] * n_arr, out_specs=[hbm] * n_arr,
        scratch_shapes=[pltpu.SemaphoreType.DMA((7 * n_arr,)), pltpu.SemaphoreType.DMA((7 * n_arr,)),
                        pltpu.SemaphoreType.DMA((n_arr,))],
    )(*shards)


def all_gather(name, shard):
    return all_gather_multi(name, [shard])[0]


def all_to_all_multi(name, chunk_arrays):
    n_arr = len(chunk_arrays)
    arrays = range(n_arr)

    def body(*refs):
        x_refs, out_refs = refs[:n_arr], refs[n_arr:2 * n_arr]
        send_sems, recv_sems, local_sems = refs[2 * n_arr:]
        x, y, c = _my_pos()
        me = 4 * x + 2 * y + c
        mine = [pltpu.make_async_copy(x_refs[a].at[me], out_refs[a].at[me], local_sems.at[a]) for a in arrays]
        sends, arrivals = [], []
        for rel in (6, 4, 2, 7, 5, 3, 1):
            dx, dy, dc = (rel >> 2) & 1, (rel >> 1) & 1, rel & 1
            px, py, pc = x ^ dx, y ^ dy, c ^ dc
            peer = 4 * px + 2 * py + pc
            for a in arrays:
                k = 7 * a + rel - 1
                sends.append(pltpu.make_async_remote_copy(
                    src_ref=x_refs[a].at[peer], dst_ref=out_refs[a].at[me],
                    send_sem=send_sems.at[k], recv_sem=recv_sems.at[k],
                    device_id=(px, py, pc), device_id_type=MESH))
                arrivals.append(pltpu.make_async_remote_copy(
                    src_ref=x_refs[a].at[peer], dst_ref=out_refs[a].at[peer],
                    send_sem=send_sems.at[k], recv_sem=recv_sems.at[k],
                    device_id=(x, y, c), device_id_type=MESH))
        for cp in sends + mine:
            cp.start()
        for cp in arrivals:
            cp.wait_recv()
        for cp in sends:
            cp.wait_send()
        for cp in mine:
            cp.wait()

    hbm = pl.BlockSpec(memory_space=pl.ANY)
    return pl.pallas_call(
        body, name=name,
        out_shape=[jax.ShapeDtypeStruct(s.shape, s.dtype) for s in chunk_arrays],
        in_specs=[hbm] * n_arr, out_specs=[hbm] * n_arr,
        scratch_shapes=[pltpu.SemaphoreType.DMA((7 * n_arr,)), pltpu.SemaphoreType.DMA((7 * n_arr,)),
                        pltpu.SemaphoreType.DMA((n_arr,))],
    )(*chunk_arrays)


def _peers():
    x, y, c = _my_pos()
    out = []
    for rel in (6, 4, 2, 7, 5, 3, 1):
        px, py, pc = x ^ ((rel >> 2) & 1), y ^ ((rel >> 1) & 1), c ^ (rel & 1)
        out.append((rel - 1, (px, py, pc), 4 * px + 2 * py + pc))
    return out


def _exchange_copies(mode, src_refs, land_refs, send_sems, recv_sems):
    x, y, c = _my_pos()
    me = 4 * x + 2 * y + c
    sends, arrivals = [], []
    for k, peer_pos, peer in _peers():
        for a, (src, land) in enumerate(zip(src_refs, land_refs)):
            piece = src.at[peer] if mode == 'scatter' else src
            sems = dict(send_sem=send_sems[a].at[k], recv_sem=recv_sems[a].at[k], device_id_type=MESH)
            sends.append(pltpu.make_async_remote_copy(src_ref=piece, dst_ref=land.at[me], device_id=peer_pos, **sems))
            arrivals.append(pltpu.make_async_remote_copy(src_ref=piece, dst_ref=land.at[peer], device_id=(x, y, c), **sems))
    return sends, arrivals


_HBM = pl.BlockSpec(memory_space=pltpu.HBM)
_SEM = pl.BlockSpec(memory_space=pltpu.SEMAPHORE)


def exchange_start(name, mode, arrays, after=()):
    n_arr, n_after = len(arrays), len(after)
    land_shapes = [a.shape if mode == 'scatter' else (N_DEV,) + a.shape for a in arrays]

    def body(*refs):
        src_refs, land_refs = refs[:n_arr], refs[n_arr:2 * n_arr]
        refs = refs[n_after:]
        send_sems, recv_sems = refs[2 * n_arr:3 * n_arr], refs[3 * n_arr:4 * n_arr]
        sends, _ = _exchange_copies(mode, src_refs, land_refs, send_sems, recv_sems)
        for cp in sends:
            cp.start()
        token = refs[-1]
        token[...] = jnp.zeros_like(token)

    sem = pltpu.SemaphoreType.DMA((N_DEV - 1,))
    res = pl.pallas_call(
        body, name=name,
        out_shape=[sem] * (2 * n_arr) + [pltpu.HBM(a.shape, a.dtype) for a in arrays]
        + [pltpu.HBM(s, a.dtype) for s, a in zip(land_shapes, arrays)] + [jax.ShapeDtypeStruct((8, 128), F32)],
        in_specs=[_HBM] * (2 * n_arr) + [pl.BlockSpec(memory_space=pl.ANY)] * n_after,
        out_specs=[_SEM] * (2 * n_arr) + [_HBM] * (2 * n_arr) + [pl.BlockSpec(memory_space=pltpu.VMEM)],
        input_output_aliases={i: 2 * n_arr + i for i in range(2 * n_arr)},
        compiler_params=pltpu.CompilerParams(has_side_effects=pltpu.SideEffectType.DATAFLOW_SIDE_EFFECTING),
    )(*[pltpu.with_memory_space_constraint(a, pltpu.HBM) for a in arrays],
      *[pltpu.with_memory_space_constraint(lax.empty(s, a.dtype), pltpu.HBM) for s, a in zip(land_shapes, arrays)],
      *after)
    return (res[:n_arr], res[n_arr:2 * n_arr], res[2 * n_arr:3 * n_arr], res[3 * n_arr:4 * n_arr]), res[-1]


def exchange_wait(name, mode, started, after):
    send_sems, recv_sems, thru, land = started
    n_arr = len(thru)

    def body(*refs):
        src_refs, land_refs = refs[:n_arr], refs[n_arr:2 * n_arr]
        s_sems, r_sems = refs[2 * n_arr:3 * n_arr], refs[3 * n_arr:4 * n_arr]
        sends, arrivals = _exchange_copies(mode, src_refs, land_refs, s_sems, r_sems)
        for cp in sends:
            cp.wait_send()
        for cp in arrivals:
            cp.wait_recv()

    res = pl.pallas_call(
        body, name=name,
        out_shape=[pltpu.HBM(a.shape, a.dtype) for a in thru] + [pltpu.HBM(a.shape, a.dtype) for a in land],
        in_specs=[_HBM] * (2 * n_arr) + [_SEM] * (2 * n_arr) + [pl.BlockSpec(memory_space=pl.ANY)],
        out_specs=[_HBM] * (2 * n_arr),
        input_output_aliases={i: i for i in range(2 * n_arr)},
        compiler_params=pltpu.CompilerParams(has_side_effects=pltpu.SideEffectType.DATAFLOW_SIDE_EFFECTING),
    )(*thru, *land, *send_sems, *recv_sems, after)
    return res[n_arr:]


def sum_slots(name, slots):
    _, r, ccols = slots.shape
    tc = _pick(ccols, (256, 128))

    def body(s_ref, o_ref):
        acc = s_ref[0].astype(F32)
        for p in range(1, N_DEV):
            acc = acc + s_ref[p].astype(F32)
        o_ref[...] = acc

    return pl.pallas_call(
        body, name=name, grid=(ccols // tc,),
        out_shape=jax.ShapeDtypeStruct((r, ccols), F32),
        in_specs=[pl.BlockSpec((N_DEV, r, tc), lambda j: (0, 0, j))],
        out_specs=pl.BlockSpec((r, tc), lambda j: (0, j)),
        compiler_params=_cparams(("parallel",)),
    )(slots)


def matmul(name, a, b, mode, out_dtype, tm=None, tn=None, tk=None, after=()):
    after = [t for t in after if t is not None]
    pieces, a_rows, a_cols = (1,) + a.shape if a.ndim == 2 else a.shape
    if mode == 'nn':
        (m, k), (k2, n) = (a_rows, pieces * a_cols), b.shape
    elif mode == 'nt':
        (m, k), (n, k2) = (a_rows, pieces * a_cols), b.shape
    else:
        (k, m), (k2, n) = (a_rows, pieces * a_cols), b.shape
    assert k == k2, (name, a.shape, b.shape, mode)
    if mode == 'tn':
        tm = tm or _pick(a_cols, LANE_TILES)
        tk = tk or _pick(k, ROW_TILES)
    else:
        tm = tm or _pick(m, ROW_TILES)
        tk = tk or _pick(a_cols, LANE_TILES)
    tn = tn or _pick(n, LANE_TILES)
    nk = k // tk
    per_piece = a_cols // (tm if mode == 'tn' else tk)
    if a.ndim == 2:
        a_block = lambda rows, cols, at: pl.BlockSpec((rows, cols), at)
    else:
        a_block = lambda rows, cols, at: pl.BlockSpec(
            (None, rows, cols), lambda i, j, kk: (at(i, j, kk)[1] // per_piece, at(i, j, kk)[0],
                                                  at(i, j, kk)[1] % per_piece))
    if mode == 'nn':
        a_spec = a_block(tm, tk, lambda i, j, kk: (i, kk))
        b_spec = pl.BlockSpec((tk, tn), lambda i, j, kk: (kk, j))
        dn = (((1,), (0,)), ((), ()))
    elif mode == 'nt':
        a_spec = a_block(tm, tk, lambda i, j, kk: (i, kk))
        b_spec = pl.BlockSpec((tn, tk), lambda i, j, kk: (j, kk))
        dn = (((1,), (1,)), ((), ()))
    else:
        a_spec = a_block(tk, tm, lambda i, j, kk: (kk, i))
        b_spec = pl.BlockSpec((tk, tn), lambda i, j, kk: (kk, j))
        dn = (((0,), (0,)), ((), ()))

    def product(a_ref, b_ref):
        return lax.dot_general(a_ref[...].astype(BF16), b_ref[...].astype(BF16), dn, preferred_element_type=F32)

    n_after = len(after)

    def body_one(a_ref, b_ref, *rest):
        o_ref = rest[n_after]
        o_ref[...] = product(a_ref, b_ref).astype(o_ref.dtype)

    def body(a_ref, b_ref, *rest):
        o_ref, acc_ref = rest[n_after:]
        kk = pl.program_id(2)

        @pl.when(kk == 0)
        def _():
            acc_ref[...] = jnp.zeros_like(acc_ref)

        acc_ref[...] += product(a_ref, b_ref)

        @pl.when(kk == nk - 1)
        def _():
            o_ref[...] = acc_ref[...].astype(o_ref.dtype)

    return pl.pallas_call(
        body_one if nk == 1 else body, name=name, grid=(m // tm, n // tn, nk),
        out_shape=jax.ShapeDtypeStruct((m, n), out_dtype),
        in_specs=[a_spec, b_spec] + [pl.BlockSpec(memory_space=pl.ANY)] * n_after,
        out_specs=pl.BlockSpec((tm, tn), lambda i, j, kk: (i, j)),
        scratch_shapes=[] if nk == 1 else [pltpu.VMEM((tm, tn), F32)],
        compiler_params=_cparams(("parallel", "parallel", "arbitrary")),
    )(a, b, *after)


def rowwise(name, fn, rows, params, out_rows, out_accs, n_rows, t_lat, tm):
    nb = n_rows // tm
    in_specs, piece_counts = [], []
    operands = []
    for arr, off, width in rows:
        g = math.gcd(off, width) if off else width
        assert g % 128 == 0 or (off == 0 and width == arr.shape[1]), (name, off, width)
        cnt = width // g
        last = arr.shape[0] // tm - 1
        clamp = arr.shape[0] < n_rows
        for p in range(cnt):
            cb = off // g + p
            if clamp:
                in_specs.append(pl.BlockSpec((tm, g), lambda i, cb=cb, last=last: (jnp.minimum(i, last), cb)))
            else:
                in_specs.append(pl.BlockSpec((tm, g), lambda i, cb=cb: (i, cb)))
            operands.append(arr)
        piece_counts.append(cnt)
    for p in params:
        in_specs.append(pl.BlockSpec(p.shape, lambda i, nd=p.ndim: (0,) * nd))
        operands.append(p)
    n_in = sum(piece_counts)
    n_par = len(params)
    n_or = len(out_rows)
    out_shape = [jax.ShapeDtypeStruct((n_rows, w), dt) for w, dt in out_rows]
    out_shape += [jax.ShapeDtypeStruct(s, F32) for s in out_accs]
    out_specs = [pl.BlockSpec((tm, w), lambda i: (i, 0)) for w, _ in out_rows]
    out_specs += [pl.BlockSpec(s, lambda i, nd=len(s): (0,) * nd) for s in out_accs]

    def body(*refs):
        in_refs, par_refs = refs[:n_in], refs[n_in:n_in + n_par]
        orow_refs = refs[n_in + n_par:n_in + n_par + n_or]
        oacc_refs = refs[n_in + n_par + n_or:]
        i = pl.program_id(0)
        tiles, at = [], 0
        for cnt in piece_counts:
            parts = [in_refs[at + p][...].astype(F32) for p in range(cnt)]
            tiles.append(parts[0] if cnt == 1 else jnp.concatenate(parts, axis=1))
            at += cnt
        is_ctx = i * tm >= t_lat
        outs, accs = fn(is_ctx, tiles, [p[...] for p in par_refs])
        for o_ref, o in zip(orow_refs, outs):
            o_ref[...] = o.astype(o_ref.dtype)
        if oacc_refs:
            @pl.when(i == 0)
            def _():
                for a_ref in oacc_refs:
                    a_ref[...] = jnp.zeros_like(a_ref)
            for a_ref, a in zip(oacc_refs, accs):
                a_ref[...] += a.astype(F32)

    res = pl.pallas_call(
        body, name=name, grid=(nb,),
        out_shape=out_shape, in_specs=in_specs, out_specs=out_specs,
        compiler_params=_cparams(("arbitrary",)),
    )(*operands)
    return res[:n_or], res[n_or:]


def _rms(x, g):
    return x * lax.rsqrt(jnp.mean(x * x, axis=-1, keepdims=True) + EPS) * g


def _norm_mod(x, g, sc, sh):
    return _rms(x, g) * (1.0 + sc) + sh


def _sigmoid(x):
    return 0.5 * jnp.tanh(0.5 * x) + 0.5


def _silu(x):
    return x * _sigmoid(x)


def _gelu(x):
    return 0.5 * x * (1.0 + jnp.tanh(math.sqrt(2.0 / math.pi) * (x + 0.044715 * (x * x * x))))


def _sel(is_ctx, p):
    return jnp.where(is_ctx, p[1:2], p[0:1])


def _seg_acc(is_ctx, v):
    rows = lax.broadcasted_iota(jnp.int32, (2, v.shape[1]), 0)
    return jnp.where(rows == is_ctx.astype(jnp.int32), jnp.broadcast_to(v, (2, v.shape[1])), 0.0)


def _rsum(v):
    return jnp.sum(v, axis=0, keepdims=True)


def _shift_rows(x, o, t_lat, n):
    if o == 0:
        return x
    y = pltpu.roll(x, (-o) % n, 0)
    t = lax.broadcasted_iota(jnp.int32, x.shape, 0)
    src = t + o
    ok = (src >= 0) & (src < n) & ((src >= t_lat) == (t >= t_lat))
    return jnp.where(ok, y, 0.0)


def conv_fwd(name, xarr, col_off, width, w, b, left, n_rows, t_lat, out_dtype, cb=128):
    taps = w.shape[0]
    assert col_off % cb == 0 and width % cb == 0

    def body(x_ref, w_ref, b_ref, o_ref):
        x = x_ref[...].astype(F32)
        acc = jnp.broadcast_to(b_ref[...], x.shape)
        for k in range(taps):
            acc = acc + _shift_rows(x, k - left, t_lat, n_rows) * w_ref[k:k + 1, :]
        o_ref[...] = acc.astype(o_ref.dtype)

    return pl.pallas_call(
        body, name=name, grid=(width // cb,),
        out_shape=jax.ShapeDtypeStruct((n_rows, width), out_dtype),
        in_specs=[pl.BlockSpec((n_rows, cb), lambda j: (0, col_off // cb + j)),
                  pl.BlockSpec((taps, cb), lambda j: (0, j)),
                  pl.BlockSpec((1, cb), lambda j: (0, j))],
        out_specs=pl.BlockSpec((n_rows, cb), lambda j: (0, j)),
        compiler_params=_cparams(("parallel",)),
    )(xarr, w, b)


def conv_bwd(name, dout, xarr, col_off, width, w, left, n_rows, t_lat, cb=128):
    taps = w.shape[0]

    def body(d_ref, x_ref, w_ref, dx_ref, dw_ref, db_ref):
        d = d_ref[...].astype(F32)
        x = x_ref[...].astype(F32)
        dx = jnp.zeros_like(d)
        dws = []
        for k in range(taps):
            dx = dx + _shift_rows(d, left - k, t_lat, n_rows) * w_ref[k:k + 1, :]
            dws.append(_rsum(d * _shift_rows(x, k - left, t_lat, n_rows)))
        dx_ref[...] = dx.astype(dx_ref.dtype)
        dw_ref[...] = jnp.concatenate(dws, axis=0)
        db_ref[...] = _rsum(d)

    return pl.pallas_call(
        body, name=name, grid=(width // cb,),
        out_shape=[jax.ShapeDtypeStruct((n_rows, width), BF16), jax.ShapeDtypeStruct((taps, width), F32),
                   jax.ShapeDtypeStruct((1, width), F32)],
        in_specs=[pl.BlockSpec((n_rows, cb), lambda j: (0, j)),
                  pl.BlockSpec((n_rows, cb), lambda j: (0, col_off // cb + j)),
                  pl.BlockSpec((taps, cb), lambda j: (0, j))],
        out_specs=[pl.BlockSpec((n_rows, cb), lambda j: (0, j)), pl.BlockSpec((taps, cb), lambda j: (0, j)),
                   pl.BlockSpec((1, cb), lambda j: (0, j))],
        compiler_params=_cparams(("parallel",)),
    )(dout, xarr, w)


def _ffn_conv(a, w_ref, b_ref, t_lat):
    shifted = [_shift_rows(a, k - 1, t_lat, t_lat) for k in range(3)]
    ac = jnp.broadcast_to(b_ref[...], a.shape)
    for k in range(3):
        ac = ac + shifted[k] * w_ref[k:k + 1, :]
    return ac, shifted


def ffn_mix_fwd(u, w, b, t_lat, cb=128):
    nblk = FFN // cb

    def body(a_ref, g_ref, w_ref, b_ref, f_ref):
        ac, _ = _ffn_conv(a_ref[...].astype(F32), w_ref, b_ref, t_lat)
        f_ref[...] = (_silu(ac) * g_ref[...].astype(F32)).astype(f_ref.dtype)

    col = lambda shape, off=0: pl.BlockSpec(shape, lambda j: (0, off + j))
    return pl.pallas_call(
        body, name="ffn_mix", grid=(nblk,),
        out_shape=jax.ShapeDtypeStruct((t_lat, FFN), BF16),
        in_specs=[col((t_lat, cb)), col((t_lat, cb), nblk), col((3, cb)), col((1, cb))],
        out_specs=col((t_lat, cb)),
        compiler_params=_cparams(("parallel",)),
    )(u, u, w, b)


def ffn_mix_bwd(u, df, w, b, t_lat, cb=128):
    nblk = FFN // cb

    def body(a_ref, g_ref, df_ref, w_ref, b_ref, du_ref, dw_ref, db_ref):
        ac, shifted = _ffn_conv(a_ref[...].astype(F32), w_ref, b_ref, t_lat)
        d = df_ref[...].astype(F32)
        s = _sigmoid(ac)
        du_ref[1] = (d * (ac * s)).astype(du_ref.dtype)
        dac = d * g_ref[...].astype(F32) * (s * (1.0 + ac * (1.0 - s)))
        da = jnp.zeros_like(dac)
        for k in range(3):
            da = da + _shift_rows(dac, 1 - k, t_lat, t_lat) * w_ref[k:k + 1, :]
        du_ref[0] = da.astype(du_ref.dtype)
        dw_ref[...] = jnp.concatenate([_rsum(dac * shifted[k]) for k in range(3)], axis=0)
        db_ref[...] = _rsum(dac)

    col = lambda shape, off=0: pl.BlockSpec(shape, lambda j: (0, off + j))
    return pl.pallas_call(
        body, name="ffn_mix_bwd", grid=(nblk,),
        out_shape=[jax.ShapeDtypeStruct((2, t_lat, FFN), BF16),
                   jax.ShapeDtypeStruct((3, FFN), F32), jax.ShapeDtypeStruct((1, FFN), F32)],
        in_specs=[col((t_lat, cb)), col((t_lat, cb), nblk), col((t_lat, cb)), col((3, cb)), col((1, cb))],
        out_specs=[pl.BlockSpec((2, t_lat, cb), lambda j: (0, 0, j)), col((3, cb)), col((1, cb))],
        compiler_params=_cparams(("parallel",)),
    )(u, u, df, w, b)


def _chunk_order(direction, nb, nbl):
    if direction == 'f':
        return lambda s: ((s + nbl) % nb, 0)
    return lambda s: (nb - 1 - s, 0)


def _adjoint_order(direction, nb, nbl):
    if direction == 'f':
        return lambda s: ((nb - 1 - s + nbl) % nb, 0)
    return lambda s: (s, 0)


SUBLANES = 8


def _chunk_scan(a, b, carry, rev):
    tc = a.shape[0]
    row = lax.broadcasted_iota(jnp.int32, a.shape, 0)
    in_tile = jnp.bitwise_and(row, SUBLANES - 1)
    for k in (1, 2, 4):
        shift = tc - k if rev else k
        edge = in_tile >= SUBLANES - k if rev else in_tile < k
        b = jnp.where(edge, b, a * pltpu.roll(b, shift, 0) + b)
        a = jnp.where(edge, a, a * pltpu.roll(a, shift, 0))
    nt = tc // SUBLANES
    hs = [None] * nt
    c = carry
    for kt in range(nt):
        k = nt - 1 - kt if rev else kt
        h = b[k * SUBLANES:(k + 1) * SUBLANES] + a[k * SUBLANES:(k + 1) * SUBLANES] * c
        hs[k] = h
        c = h[0:1] if rev else h[SUBLANES - 1:SUBLANES]
    h = jnp.concatenate(hs, axis=0)
    if rev:
        return h, jnp.where(row == tc - 1, carry, pltpu.roll(h, tc - 1, 0)), c
    return h, jnp.where(row == 0, carry, pltpu.roll(h, 1, 0)), c


def scan_fwd(name, a, u, direction, n_rows, t_lat):
    w = a.shape[1]
    tc = _pick(math.gcd(t_lat, n_rows), (256, 128))
    nb, nbl = n_rows // tc, t_lat // tc
    order = _chunk_order(direction, nb, nbl)
    rev = direction == 'b'

    def body(a_ref, u_ref, h_ref, hp_ref, carry):
        @pl.when(pl.program_id(0) == 0)
        def _():
            carry[...] = jnp.zeros_like(carry)

        h_ref[...], hp_ref[...], carry[...] = _chunk_scan(a_ref[...], u_ref[...], carry[...], rev)

    spec = pl.BlockSpec((tc, w), order)
    return pl.pallas_call(
        body, name=name, grid=(nb,),
        out_shape=[jax.ShapeDtypeStruct((n_rows, w), F32)] * 2,
        in_specs=[spec, spec], out_specs=[spec, spec],
        scratch_shapes=[pltpu.VMEM((1, w), F32)],
        compiler_params=_cparams(("arbitrary",)),
    )(a, u)


def scan_adj(name, a, dh, hprev, direction, n_rows, t_lat):
    w = a.shape[1]
    tc = _pick(math.gcd(t_lat, n_rows), (256, 128))
    nb, nbl = n_rows // tc, t_lat // tc
    order = _adjoint_order(direction, nb, nbl)
    rev = direction == 'f'

    def dh_order(s):
        c, _ = order(s)
        return (jnp.minimum(c, nbl - 1), 0)

    def body(a_ref, dh_ref, hp_ref, du_ref, da_ref, carry):
        s = pl.program_id(0)

        @pl.when(s == 0)
        def _():
            carry[...] = jnp.zeros_like(carry)

        chunk, _ = order(s)
        live = (chunk < nbl).astype(F32)

        av = a_ref[...]
        dv = dh_ref[...] * live
        _, c_next, carry[...] = _chunk_scan(av, av * dv, carry[...], rev)
        lam = dv + c_next
        du_ref[...] = lam
        da_ref[...] = lam * hp_ref[...]

    spec = pl.BlockSpec((tc, w), order)
    return pl.pallas_call(
        body, name=name, grid=(nb,),
        out_shape=[jax.ShapeDtypeStruct((n_rows, w), F32)] * 2,
        in_specs=[spec, pl.BlockSpec((tc, w), dh_order), spec], out_specs=[spec, spec],
        scratch_shapes=[pltpu.VMEM((1, w), F32)],
        compiler_params=_cparams(("arbitrary",)),
    )(a, dh, hprev)


def _neg_expm1(y):
    series = -(y * (1.0 + y * (0.5 + y * (1.0 / 6.0 + y * (1.0 / 24.0)))))
    return jnp.where(y > -0.03, series, 1.0 - jnp.exp(y))


def _gate_elem(pre_r, pre_i, xc, b_a, b_x, sp):
    r = _sigmoid(pre_r + b_a)
    i = _sigmoid(pre_i + b_x)
    log_a = (-LRU_C) * r * sp
    a = jnp.exp(log_a)
    mult = jnp.sqrt(_neg_expm1(2.0 * log_a))
    return a, mult * (i * xc)


def _gate_elem_bwd(pre_r, pre_i, xc, b_a, b_x, sp, da, du):
    r = _sigmoid(pre_r + b_a)
    i = _sigmoid(pre_i + b_x)
    log_a = (-LRU_C) * r * sp
    a = jnp.exp(log_a)
    m2 = _neg_expm1(2.0 * log_a)
    inv_mult = lax.rsqrt(m2)
    g = du * (m2 * inv_mult)
    d_mult = du * (i * xc)
    d_log_a = (da - d_mult * a * inv_mult) * a
    d_pre_r = d_log_a * ((-LRU_C) * sp) * (r * (1.0 - r))
    d_pre_i = g * xc * (i * (1.0 - i))
    return d_pre_r, d_pre_i, g * i, _rsum(d_log_a * ((-LRU_C) * r))


def _blockdiag(xb16, w_ref_val, d):
    outs = []
    for n in range(LRU_BLOCKS):
        outs.append(jnp.dot(xb16[:, n * LRU_BW:(n + 1) * LRU_BW], w_ref_val[d * LRU_BLOCKS + n],
                            preferred_element_type=F32))
    return jnp.concatenate(outs, axis=1)


def gates_fwd(xc, w_a, w_x, b_a, b_x, sp, n_rows, t_lat, tm):
    def fn(is_ctx, rows, params):
        (x,), (wa, wx, ba, bx, spv) = rows, params
        xb16 = x.astype(BF16)
        outs = []
        for d in range(2):
            a, u = _gate_elem(_blockdiag(xb16, wa, d), _blockdiag(xb16, wx, d), x,
                              ba[d:d + 1], bx[d:d + 1], spv[d:d + 1])
            outs += [a, u]
        return outs, []

    (a_f, u_f, a_b, u_b), _ = rowwise("gates_fwd", fn, [(xc, 0, LRU_W)], [w_a, w_x, b_a, b_x, sp],
                                      [(LRU_W, F32)] * 4, [], n_rows, t_lat, tm)
    return a_f, u_f, a_b, u_b


def gates_bwd(xc, da_f, du_f, da_b, du_b, w_a, w_x, b_a, b_x, sp, n_rows, t_lat, tm):
    def fn(is_ctx, rows, params):
        (x, daf, duf, dab, dub), (wa, wx, ba, bx, spv) = rows, params
        xb16 = x.astype(BF16)
        dxc = jnp.zeros_like(x)
        dwa, dwx, dba, dbx, dsp = [], [], [], [], []
        for d, (da, du) in enumerate(((daf, duf), (dab, dub))):
            dpr, dpi, dx_e, dsp_d = _gate_elem_bwd(_blockdiag(xb16, wa, d), _blockdiag(xb16, wx, d), x,
                                                   ba[d:d + 1], bx[d:d + 1], spv[d:d + 1], da, du)
            dba_d, dbx_d = _rsum(dpr), _rsum(dpi)
            dxc = dxc + dx_e
            dpr16, dpi16 = dpr.astype(BF16), dpi.astype(BF16)
            back = []
            for n in range(LRU_BLOCKS):
                sl = slice(n * LRU_BW, (n + 1) * LRU_BW)
                nt_dims = (((1,), (1,)), ((), ()))
                back.append(lax.dot_general(dpr16[:, sl], wa[d * LRU_BLOCKS + n], nt_dims, preferred_element_type=F32)
                            + lax.dot_general(dpi16[:, sl], wx[d * LRU_BLOCKS + n], nt_dims,
                                              preferred_element_type=F32))
                tn_dims = (((0,), (0,)), ((), ()))
                dwa.append(lax.dot_general(xb16[:, sl], dpr16[:, sl], tn_dims, preferred_element_type=F32)[None])
                dwx.append(lax.dot_general(xb16[:, sl], dpi16[:, sl], tn_dims, preferred_element_type=F32)[None])
            dxc = dxc + jnp.concatenate(back, axis=1)
            dba.append(dba_d)
            dbx.append(dbx_d)
            dsp.append(dsp_d)
        cat0 = lambda xs: jnp.concatenate(xs, axis=0)
        return [dxc], [cat0(dwa), cat0(dwx), cat0(dba), cat0(dbx), cat0(dsp)]

    (dxc,), accs = rowwise("gates_bwd", fn,
                           [(xc, 0, LRU_W), (da_f, 0, LRU_W), (du_f, 0, LRU_W), (da_b, 0, LRU_W), (du_b, 0, LRU_W)],
                           [w_a, w_x, b_a, b_x, sp], [(LRU_W, F32)],
                           [(2 * LRU_BLOCKS, LRU_BW, LRU_BW)] * 2 + [(2, LRU_W)] * 3, n_rows, t_lat, tm)
    return dxc, accs


def _rope_tables(t_lat, n_rows):
    rows = t_lat // GRID_W
    row_ids = jnp.repeat(jnp.arange(rows), GRID_W).astype(F32)
    col_ids = jnp.tile(jnp.arange(GRID_W), rows).astype(F32)
    axis_dim = QK_ROPE // 2
    inv = 1.0 / (ROPE_BASE ** (jnp.arange(0, axis_dim, 2, dtype=F32) / axis_dim))
    ang = jnp.concatenate([row_ids[:, None] * inv, col_ids[:, None] * inv], axis=-1)
    cos, sin = jnp.cos(ang), jnp.sin(ang)
    half = QK_ROPE // 2
    ones, zeros = jnp.ones((t_lat, QK_NOPE), F32), jnp.zeros((t_lat, QK_NOPE), F32)
    pad1, pad0 = jnp.ones((t_lat, HEAD_PAD - QK_DIM), F32), jnp.zeros((t_lat, HEAD_PAD - QK_DIM), F32)
    zh = jnp.zeros((t_lat, half), F32)
    c_tab = jnp.concatenate([ones, cos, cos, pad1], axis=1)
    s1 = jnp.concatenate([zeros, -sin, zh, pad0], axis=1)
    s2 = jnp.concatenate([zeros, zh, sin, pad0], axis=1)
    n_ctx = n_rows - t_lat
    c_tab = jnp.concatenate([c_tab, jnp.ones((n_ctx, HEAD_PAD), F32)], axis=0)
    s1 = jnp.concatenate([s1, jnp.zeros((n_ctx, HEAD_PAD), F32)], axis=0)
    s2 = jnp.concatenate([s2, jnp.zeros((n_ctx, HEAD_PAD), F32)], axis=0)
    return c_tab, s1, s2


def _rope(x, c, s1, s2):
    half = QK_ROPE // 2
    return x * c + pltpu.roll(x, HEAD_PAD - half, 1) * s1 + pltpu.roll(x, half, 1) * s2


def _rope_t(dy, c, s1, s2):
    half = QK_ROPE // 2
    return dy * c + pltpu.roll(dy * s1, half, 1) + pltpu.roll(dy * s2, HEAD_PAD - half, 1)


def _heads(x):
    return [x[:, h * HEAD_PAD:(h + 1) * HEAD_PAD] for h in range(N_HEADS)]


Q_SCALE = QK_DIM ** -0.5 * math.log2(math.e)


def attn_fwd(q, k, v, t_lat, n_rows, tq):
    def body(q_ref, k_ref, v_ref, o_ref, lse_ref):
        s = lax.dot_general(q_ref[...], k_ref[...], (((1,), (1,)), ((), ())), preferred_element_type=F32)
        m = jnp.max(s, axis=-1, keepdims=True)
        p = jnp.exp2(s - m)
        l = jnp.sum(p, axis=-1, keepdims=True)
        o = jnp.dot(p.astype(BF16), v_ref[...], preferred_element_type=F32) / l
        o_ref[...] = o.astype(o_ref.dtype)
        lse_ref[...] = jnp.broadcast_to(m + jnp.log2(l), lse_ref.shape)

    qspec = pl.BlockSpec((tq, HEAD_PAD), lambda h, i: (i, h))
    kspec = pl.BlockSpec((n_rows, HEAD_PAD), lambda h, i: (0, h))
    return pl.pallas_call(
        body, name="attn_fwd", grid=(N_HEADS, t_lat // tq),
        out_shape=[jax.ShapeDtypeStruct((t_lat, N_HEADS * HEAD_PAD), BF16),
                   jax.ShapeDtypeStruct((t_lat, N_HEADS * HEAD_PAD), F32)],
        in_specs=[qspec, kspec, kspec], out_specs=[qspec, qspec],
        compiler_params=_cparams(("parallel", "arbitrary")),
    )(q, k, v)


def attn_bwd(q, k, v, o, do, lse, t_lat, n_rows, tq):
    scale = QK_DIM ** -0.5
    nq = t_lat // tq
    nt = (((1,), (1,)), ((), ()))
    tn = (((0,), (0,)), ((), ()))

    def body(q_ref, k_ref, v_ref, o_ref, do_ref, lse_ref, dq_ref, dk_ref, dv_ref):
        @pl.when(pl.program_id(1) == 0)
        def _():
            dk_ref[...] = jnp.zeros_like(dk_ref)
            dv_ref[...] = jnp.zeros_like(dv_ref)

        qv, kv, vv, dov = q_ref[...], k_ref[...], v_ref[...], do_ref[...]
        s = lax.dot_general(qv, kv, nt, preferred_element_type=F32)
        p = jnp.exp2(s - lse_ref[:, 0:1])
        dv_ref[...] += lax.dot_general(p.astype(BF16), dov, tn, preferred_element_type=F32)
        dp = lax.dot_general(dov, vv, nt, preferred_element_type=F32)
        delta = jnp.sum(dov.astype(F32) * o_ref[...].astype(F32), axis=-1, keepdims=True)
        ds = (p * (dp - delta)).astype(BF16)
        dq_ref[...] = jnp.dot(ds, kv, preferred_element_type=F32) * scale
        dk_ref[...] += lax.dot_general(ds, qv, tn, preferred_element_type=F32)

        @pl.when(pl.program_id(1) == nq - 1)
        def _():
            dk_ref[...] = dk_ref[...] * (scale / Q_SCALE)

    qspec = pl.BlockSpec((tq, HEAD_PAD), lambda h, i: (i, h))
    kspec = pl.BlockSpec((n_rows, HEAD_PAD), lambda h, i: (0, h))
    return pl.pallas_call(
        body, name="attn_bwd", grid=(N_HEADS, t_lat // tq),
        out_shape=[jax.ShapeDtypeStruct((t_lat, N_HEADS * HEAD_PAD), F32),
                   jax.ShapeDtypeStruct((n_rows, N_HEADS * HEAD_PAD), F32),
                   jax.ShapeDtypeStruct((n_rows, N_HEADS * HEAD_PAD), F32)],
        in_specs=[qspec, kspec, kspec, qspec, qspec, qspec], out_specs=[qspec, kspec, kspec],
        compiler_params=_cparams(("parallel", "arbitrary")),
    )(q, k, v, o, do, lse)


def adamw(name, w, g, m, v):
    r, ccols = w.shape
    if r % 8 == 0:
        tr, tcol = _best_div(r, 8, max(8, 262144 // ccols)), ccols
    else:
        tr, tcol = r, _pick(ccols, (256, 128))
    c1 = 1.0 - ADAM_B1 ** ADAM_STEP
    c2 = 1.0 - ADAM_B2 ** ADAM_STEP

    def body(w_ref, g_ref, m_ref, v_ref, d_ref, nm_ref, nv_ref):
        gv = g_ref[...]
        nm = ADAM_B1 * m_ref[...] + (1.0 - ADAM_B1) * gv
        nv = ADAM_B2 * v_ref[...] + (1.0 - ADAM_B2) * (gv * gv)
        d_ref[...] = -ADAM_LR * ((nm / c1) / (jnp.sqrt(nv / c2) + ADAM_EPS) + ADAM_WD * w_ref[...])
        nm_ref[...] = nm
        nv_ref[...] = nv

    spec = pl.BlockSpec((tr, tcol), lambda i, j: (i, j))
    return pl.pallas_call(
        body, name=name, grid=(r // tr, ccols // tcol),
        out_shape=[jax.ShapeDtypeStruct((r, ccols), F32)] * 3,
        in_specs=[spec] * 4, out_specs=[spec] * 3,
        compiler_params=_cparams(("parallel", "parallel")),
    )(w, g, m, v)


def adamw_many(name, ws, gs, ms, vs):
    n = len(ws)
    c1 = 1.0 - ADAM_B1 ** ADAM_STEP
    c2 = 1.0 - ADAM_B2 ** ADAM_STEP

    def body(*refs):
        for i in range(n):
            w_ref, g_ref, m_ref, v_ref = (refs[k * n + i] for k in range(4))
            d_ref, nm_ref, nv_ref = (refs[(4 + k) * n + i] for k in range(3))
            gv = g_ref[...]
            nm = ADAM_B1 * m_ref[...] + (1.0 - ADAM_B1) * gv
            nv = ADAM_B2 * v_ref[...] + (1.0 - ADAM_B2) * (gv * gv)
            d_ref[...] = -ADAM_LR * ((nm / c1) / (jnp.sqrt(nv / c2) + ADAM_EPS) + ADAM_WD * w_ref[...])
            nm_ref[...] = nm
            nv_ref[...] = nv

    vmem = pl.BlockSpec(memory_space=pltpu.VMEM)
    res = pl.pallas_call(
        body, name=name,
        out_shape=[jax.ShapeDtypeStruct(w.shape, F32) for w in ws] * 3,
        in_specs=[vmem] * (4 * n), out_specs=[vmem] * (3 * n),
        compiler_params=_cparams(),
    )(*ws, *gs, *ms, *vs)
    return [tuple(res[k * n + i] for k in range(3)) for i in range(n)]


def _flat(parts, dtype, row_mult):
    v = jnp.concatenate([p.reshape(-1).astype(dtype) for p in parts])
    quantum = row_mult * FLAT_C
    total = -(-v.shape[0] // quantum) * quantum
    return jnp.pad(v, (0, total - v.shape[0])).reshape(total // FLAT_C, FLAT_C)


def _unflat(flat, shapes):
    v = flat.reshape(-1)
    out, at = [], 0
    for s in shapes:
        n = math.prod(s)
        out.append(v[at:at + n].reshape(s))
        at += n
    return out


def _gathered_to_full(name, g):
    k = g.shape[1]
    return jnp.transpose(g, (1, 0, 2)).reshape(k, N_DEV * g.shape[2])


def _full_to_chunks(name, full):
    k, n = full.shape
    return jnp.transpose(full.reshape(k, N_DEV, n // N_DEV), (1, 0, 2)).reshape(N_DEV, -1)


def _shard_to_rb(name, w):
    return w if name in ROW_SHARDED else w.T


def _rb_to_shard(name, g):
    return g if name in ROW_SHARDED else g.T


def _rb_from_gathered(name, g):
    cols = g.shape[2]
    if name == 'w_in':
        z = lambda k: jnp.zeros((k, cols), g.dtype)
        full = g.reshape(N_DEV * g.shape[1], cols)
        return jnp.concatenate([full[:Z_KR], z(QK_NOPE), full[Z_KR:Z_KR + QK_ROPE], z(HEAD_PAD - QK_DIM),
                                full[Z_KR + QK_ROPE:]], axis=0)
    if name == 'w_uq':
        return jnp.pad(g, ((0, 0), (0, HEAD_PAD - QK_DIM), (0, 0))).reshape(N_HEADS * HEAD_PAD, cols)
    if name == 'w_ukv':
        pad = lambda t: jnp.pad(t, ((0, 0), (0, HEAD_PAD - t.shape[1]), (0, 0))).reshape(N_HEADS * HEAD_PAD, cols)
        return jnp.concatenate([pad(g[:, :QK_NOPE]), pad(g[:, QK_NOPE:])], axis=0)
    if name == 'w_o_attn':
        full = g.reshape(D, N_HEADS, V_HEAD)
        return jnp.pad(full, ((0, 0), (0, 0), (0, HEAD_PAD - V_HEAD))).reshape(D, N_HEADS * HEAD_PAD)
    return g.reshape(N_DEV * g.shape[1], cols)


def _chunks_from_rb_grad(name, g):
    cols = g.shape[1]
    if name == 'w_in':
        full = jnp.concatenate([g[:Z_KR], g[Z_KR + QK_NOPE:Z_KR + QK_DIM], g[Z_XB:]], axis=0)
        return full.reshape(N_DEV, -1, cols)
    if name == 'w_uq':
        return g.reshape(N_HEADS, HEAD_PAD, cols)[:, :QK_DIM]
    if name == 'w_ukv':
        half = N_HEADS * HEAD_PAD
        gk = g[:half].reshape(N_HEADS, HEAD_PAD, cols)[:, :QK_NOPE]
        gv = g[half:].reshape(N_HEADS, HEAD_PAD, cols)[:, :V_HEAD]
        return jnp.concatenate([gk, gv], axis=1)
    if name == 'w_o_attn':
        full = g.reshape(D, N_HEADS, HEAD_PAD)[:, :, :V_HEAD].reshape(D, N_HEADS * V_HEAD)
        return full.reshape(N_DEV, D // N_DEV, N_HEADS * V_HEAD)
    return g.reshape(N_DEV, -1, cols)


def local_step(x, ctx, target, mod_l, mod_c, wt, on_grad=None, arrive=None):
    t_lat, n_ctx = x.shape[0], ctx.shape[0]
    n = t_lat + n_ctx
    tm = _pick(math.gcd(t_lat, n), (256, 128))
    tq_fwd = _pick(t_lat, (256, 128))
    tq_bwd = _pick(t_lat, (512, 256, 128))
    row = lambda v: v.reshape(1, -1).astype(F32)
    two = lambda a, b: jnp.stack([a, b]).astype(F32)
    sh1_l, sc1_l, g1_l, sh2_l, sc2_l, g2_l = jnp.split(mod_l, 6)
    sh1_c, sc1_c = jnp.split(mod_c, 6)[:2]
    sc1, sh1 = two(sc1_l, sc1_c), two(sh1_l, sh1_c)
    g1, g2, sc2, sh2 = row(g1_l), row(g2_l), row(sc2_l), row(sh2_l)
    norm1_g, norm2_g, final_g = row(wt['norm1_g']), row(wt['norm2_g']), row(wt['final_g'])
    q_g, kv_g, b_gate = row(wt['q_norm_g']), row(wt['kv_norm_g']), row(wt['b_gate'])
    wt = dict(wt)
    pending = []

    def sent():
        tokens = list(pending)
        pending.clear()
        return tokens

    def need(names, after):
        if arrive is not None:
            got = arrive(names, after)
            if '_token' in got:
                pending.append(got.pop('_token'))
            wt.update(got)
        return [wt[n] for n in names]
    lru_w_a = wt['lru_w_a'].reshape(2 * LRU_BLOCKS, LRU_BW, LRU_BW).astype(BF16)
    lru_w_x = wt['lru_w_x'].reshape(2 * LRU_BLOCKS, LRU_BW, LRU_BW).astype(BF16)
    b_a, b_x, lam = wt['lru_b_a'], wt['lru_b_x'], wt['lru_lambda']
    sp = jnp.logaddexp(-lam, 0.0)
    c_tab, s1_tab, s2_tab = _rope_tables(t_lat, n)
    rw = functools.partial(rowwise, n_rows=n, t_lat=t_lat, tm=tm)
    rw_lat = functools.partial(rowwise, n_rows=t_lat, t_lat=t_lat, tm=tm)

    xs = jnp.concatenate([x, ctx], axis=0)

    def f_norm1(is_ctx, rows, params):
        (xv,), (g, sc, sh) = rows, params
        return [_norm_mod(xv, g, _sel(is_ctx, sc), _sel(is_ctx, sh))], []

    (h,), _ = rw("norm1", f_norm1, [(xs, 0, D)], [norm1_g, sc1, sh1], [(D, BF16)], [])
    (w_in_t,) = need(('w_in',), h)
    z = matmul("w_in", h, w_in_t, 'nt', BF16, after=sent())
    w_uq_t, w_ukv_t, w_o_lru = need(('w_uq', 'w_ukv', 'w_o_lru'), z)

    def f_qkv_norm(is_ctx, rows, params):
        (ql, kvl), (gq, gkv) = rows, params
        return [_rms(ql, gq), _rms(kvl, gkv)], []

    (qn, kvn), _ = rw("qkv_norm", f_qkv_norm, [(z, Z_Q, Q_RANK), (z, Z_KV, KV_RANK)], [q_g, kv_g],
                      [(Q_RANK, BF16), (KV_RANK, BF16)], [])
    qp = matmul("w_uq", qn, w_uq_t, 'nt', BF16)
    kvp = matmul("w_ukv", kvn, w_ukv_t, 'nt', BF16)

    def f_rope(is_ctx, rows, params):
        qv, kk, vv, kr, c, s1, s2 = rows
        krr = _rope(kr, c, s1, s2)
        qo = jnp.concatenate([_rope(qh, c, s1, s2) for qh in _heads(qv)], axis=1) * Q_SCALE
        ko = jnp.concatenate([kh + krr for kh in _heads(kk)], axis=1)
        return [qo, ko, vv], []

    hp = N_HEADS * HEAD_PAD
    (qr, kr_, vr), _ = rw("rope", f_rope,
                          [(qp, 0, hp), (kvp, 0, hp), (kvp, hp, hp), (z, Z_KR, HEAD_PAD), (c_tab, 0, HEAD_PAD),
                           (s1_tab, 0, HEAD_PAD), (s2_tab, 0, HEAD_PAD)], [], [(hp, BF16)] * 3, [])
    attn, lse = attn_fwd(qr, kr_, vr, t_lat, n, tq_fwd)

    xc = conv_fwd("lru_conv", z, Z_XB, LRU_W, wt['lru_conv_w'], row(wt['lru_conv_b']), 2, n, t_lat, F32)
    a_f, u_f, a_b, u_b = gates_fwd(xc, lru_w_a, lru_w_x, b_a, b_x, sp, n, t_lat, tm)
    h_f, hp_f = scan_fwd("scan_f", a_f, u_f, 'f', n, t_lat)
    h_b, hp_b = scan_fwd("scan_b", a_b, u_b, 'b', n, t_lat)

    def f_lru_out(is_ctx, rows, params):
        hf, hb, yb = rows
        return [(hf + hb) * _gelu(yb)], []

    (ybin,), _ = rw_lat("lru_out", f_lru_out, [(h_f, 0, LRU_W), (h_b, 0, LRU_W), (z, Z_YB, LRU_W)], [],
                        [(LRU_W, BF16)], [])
    w_o_attn_t, w_out, w_up_t, w_down = need(('w_o_attn', 'w_out', 'w_up', 'w_down'), attn)
    y_a = matmul("w_o_attn", attn, w_o_attn_t, 'nt', BF16)
    y_b = matmul("w_o_lru", ybin, w_o_lru, 'nn', BF16)

    def _merge(ya, yb, gl, bg):
        gates = _sigmoid(gl + bg)
        return gates[:, :D] * ya + gates[:, D:] * yb

    def f_merge(is_ctx, rows, params):
        (ya, yb, gl), (bg,) = rows, params
        return [_merge(ya, yb, gl, bg)], []

    (mrg,), _ = rw_lat("merge", f_merge, [(y_a, 0, D), (y_b, 0, D), (z, Z_GL, 2 * D)], [b_gate], [(D, BF16)], [])
    o = matmul("w_out", mrg, w_out, 'nn', BF16)

    def _res_norm2(xv, ov, g1v, g, sc, sh):
        x1 = xv + g1v * ov
        return x1, _norm_mod(x1, g, sc, sh)

    def f_norm2(is_ctx, rows, params):
        (xv, ov), (g1v, g, sc, sh) = rows, params
        x1, h2v = _res_norm2(xv, ov, g1v, g, sc, sh)
        return [x1, h2v], []

    (x1, h2), _ = rw_lat("norm2", f_norm2, [(x, 0, D), (o, 0, D)], [g1, norm2_g, sc2, sh2], [(D, F32), (D, BF16)], [])
    u = matmul("w_up", h2, w_up_t, 'nt', BF16)
    f = ffn_mix_fwd(u, wt['ffn_conv_w'], row(wt['ffn_conv_b']), t_lat)
    dn = matmul("w_down", f, w_down, 'nn', BF16)

    def _tile_loss(x1v, dv, g2v, fg, tgt):
        y = _rms(x1v + g2v * dv, fg)
        e = y - tgt
        return 0.5 * jnp.sum(jnp.mean(e * e, axis=-1, keepdims=True), axis=0, keepdims=True)

    def f_final(is_ctx, rows, params):
        (x1v, dv, tgt), (g2v, fg) = rows, params
        lv, vjp = jax.vjp(lambda a, b, c, d: _tile_loss(a, b, c, d, tgt), x1v, dv, g2v, fg)
        dx2, dd, dg2, dfg = vjp(jnp.ones((1, 1), F32))
        return [dx2, dd], [dg2, dfg, jnp.broadcast_to(lv, (1, 128))]

    (dx2, dd), (dg2, dfinal_g, loss_v) = rw_lat("final", f_final, [(x1, 0, D), (dn, 0, D), (target, 0, D)],
                                                [g2, final_g], [(D, F32), (D, BF16)], [(1, D), (1, D), (1, 128)])
    loss = loss_v[0, 0]

    grads = {'final_g': dfinal_g}

    def put(name, g):
        grads[name] = g
        if on_grad is not None:
            pending.append(on_grad(name, g))
    df = matmul("d_f", dd, w_down, 'nt', BF16)
    put('w_down', matmul("g_w_down", f, dd, 'tn', BF16))

    du, grads['ffn_conv_w'], grads['ffn_conv_b'] = ffn_mix_bwd(u, df, wt['ffn_conv_w'], row(wt['ffn_conv_b']),
                                                               t_lat)
    dh2 = matmul("d_h2", du, w_up_t, 'nn', BF16, after=sent())
    put('w_up', matmul("g_w_up", du, h2, 'tn', BF16))

    def b_norm2(is_ctx, rows, params):
        (xv, ov, dh2v, dx2v), (g1v, g, sc, sh) = rows, params
        _, vjp = jax.vjp(_res_norm2, xv, ov, g1v, g, sc, sh)
        dx, do, dg1v, dg, dsc, dsh = vjp((dx2v, dh2v))
        return [dx, do], [dg1v, dg, dsc, dsh]

    (dx_res, do), (dg1, dnorm2_g, dsc2, dsh2) = rw_lat(
        "norm2_bwd", b_norm2, [(x, 0, D), (o, 0, D), (dh2, 0, D), (dx2, 0, D)], [g1, norm2_g, sc2, sh2],
        [(D, F32), (D, BF16)], [(1, D)] * 4)
    grads['norm2_g'] = dnorm2_g
    dmrg = matmul("d_merge", do, w_out, 'nt', BF16, after=sent())
    put('w_out', matmul("g_w_out", mrg, do, 'tn', BF16))

    def b_merge(is_ctx, rows, params):
        (ya, yb, gl, dm), (bg,) = rows, params
        _, vjp = jax.vjp(_merge, ya, yb, gl, bg)
        dya, dyb, dgl, dbg = vjp(dm)
        return [dya, dyb, dgl], [dbg]

    (dy_a, dy_b, dgl), (grads['b_gate'],) = rw_lat(
        "merge_bwd", b_merge, [(y_a, 0, D), (y_b, 0, D), (z, Z_GL, 2 * D), (dmrg, 0, D)], [b_gate],
        [(D, BF16), (D, BF16), (2 * D, BF16)], [(1, 2 * D)])
    dattn = matmul("d_attn", dy_a, w_o_attn_t, 'nn', BF16, after=sent())
    put('w_o_attn', matmul("g_w_o_attn", dy_a, attn, 'tn', BF16))
    dybin = matmul("d_lru_out", dy_b, w_o_lru, 'nt', BF16, after=sent())
    put('w_o_lru', matmul("g_w_o_lru", ybin, dy_b, 'tn', BF16))

    def b_lru_out(is_ctx, rows, params):
        hf, hb, yb, dyv = rows
        _, vjp = jax.vjp(lambda s, y: s * _gelu(y), hf + hb, yb)
        dh, dyb = vjp(dyv)
        return [dh, dyb], []

    (dh_lru, dyb), _ = rw_lat("lru_out_bwd", b_lru_out,
                              [(h_f, 0, LRU_W), (h_b, 0, LRU_W), (z, Z_YB, LRU_W), (dybin, 0, LRU_W)], [],
                              [(LRU_W, F32), (LRU_W, BF16)], [])
    du_f, da_f = scan_adj("scan_f_adj", a_f, dh_lru, hp_f, 'f', n, t_lat)
    du_b, da_b = scan_adj("scan_b_adj", a_b, dh_lru, hp_b, 'b', n, t_lat)
    dxc, (dw_a, dw_x, db_a, db_x, dsp) = gates_bwd(xc, da_f, du_f, da_b, du_b, lru_w_a, lru_w_x, b_a, b_x, sp,
                                                   n, t_lat, tm)
    put('lru_w_a', dw_a.reshape(2 * LRU_BLOCKS * LRU_BW, LRU_BW).astype(BF16))
    put('lru_w_x', dw_x.reshape(2 * LRU_BLOCKS * LRU_BW, LRU_BW).astype(BF16))
    grads['lru_b_a'], grads['lru_b_x'] = db_a, db_x
    grads['lru_lambda'] = -dsp * _sigmoid(-lam)
    dxb, grads['lru_conv_w'], grads['lru_conv_b'] = conv_bwd("lru_conv_bwd", dxc, z, Z_XB, LRU_W, wt['lru_conv_w'],
                                                             2, n, t_lat)

    dq, dk, dv = attn_bwd(qr, kr_, vr, attn, dattn, lse, t_lat, n, tq_bwd)

    def b_rope(is_ctx, rows, params):
        dqv, dkv, dvv, c, s1, s2 = rows
        live = jnp.where(is_ctx, 0.0, 1.0)
        dqo = jnp.concatenate([_rope_t(dqh, c, s1, s2) for dqh in _heads(dqv)], axis=1) * live
        dkh = _heads(dkv)
        dkr = dkh[0]
        for t in dkh[1:]:
            dkr = dkr + t
        lanes = lax.broadcasted_iota(jnp.int32, dkr.shape, 1)
        dkr = jnp.where((lanes >= QK_NOPE) & (lanes < QK_DIM), _rope_t(dkr, c, s1, s2), 0.0)
        return [dqo, jnp.concatenate([dkv, dvv], axis=1), dkr], []

    (dqp, dkvp, dkr), _ = rw("rope_bwd", b_rope,
                             [(dq, 0, hp), (dk, 0, hp), (dv, 0, hp), (c_tab, 0, HEAD_PAD), (s1_tab, 0, HEAD_PAD),
                              (s2_tab, 0, HEAD_PAD)], [], [(hp, BF16), (2 * hp, BF16), (HEAD_PAD, BF16)], [])
    dqn = matmul("d_qn", dqp, w_uq_t, 'nn', BF16, after=sent())
    put('w_uq', matmul("g_w_uq", dqp, qn, 'tn', BF16))
    dkvn = matmul("d_kvn", dkvp, w_ukv_t, 'nn', BF16, after=sent())
    put('w_ukv', matmul("g_w_ukv", dkvp, kvn, 'tn', BF16))

    def b_qkv_norm(is_ctx, rows, params):
        (ql, kvl, dqv, dkvv), (gq, gkv) = rows, params
        _, vjp_q = jax.vjp(_rms, ql, gq)
        _, vjp_kv = jax.vjp(_rms, kvl, gkv)
        dql, dgq = vjp_q(dqv)
        dkvl, dgkv = vjp_kv(dkvv)
        return [dql, dkvl], [dgq, dgkv]

    (dq_lat, dkv_lat), (grads['q_norm_g'], grads['kv_norm_g']) = rw(
        "qkv_norm_bwd", b_qkv_norm, [(z, Z_Q, Q_RANK), (z, Z_KV, KV_RANK), (dqn, 0, Q_RANK), (dkvn, 0, KV_RANK)],
        [q_g, kv_g], [(Q_RANK, BF16), (KV_RANK, BF16)], [(1, Q_RANK), (1, KV_RANK)])
    pad_ctx = lambda t: jnp.pad(t, ((0, n_ctx), (0, 0)))
    dz = jnp.concatenate([dq_lat, dkv_lat, dkr, dxb, pad_ctx(dyb), pad_ctx(dgl)], axis=1)
    put('w_in', matmul("g_w_in", dz, h, 'tn', BF16))
    dh = matmul("d_h", dz, w_in_t, 'nn', BF16, after=sent())

    def b_norm1(is_ctx, rows, params):
        (xv, dhv, dxr), (g, sc, sh) = rows, params
        scv, shv = _sel(is_ctx, sc), _sel(is_ctx, sh)
        _, vjp = jax.vjp(_norm_mod, xv, g, scv, shv)
        dx, dg, dsc, dsh = vjp(dhv)
        return [dx + dxr], [dg, _seg_acc(is_ctx, dsc), _seg_acc(is_ctx, dsh)]

    (dxs,), (grads['norm1_g'], dsc1, dsh1) = rw("norm1_bwd", b_norm1, [(xs, 0, D), (dh, 0, D), (dx_res, 0, D)],
                                                [norm1_g, sc1, sh1], [(D, F32)], [(1, D), (2, D), (2, D)])
    grad_x = dxs[:t_lat]
    zero = jnp.zeros((D,), F32)
    dmod_l = jnp.concatenate([dsh1[0], dsc1[0], dg1[0], dsh2[0], dsc2[0], dg2[0]])
    dmod_c = jnp.concatenate([dsh1[1], dsc1[1], zero, zero, zero, zero])
    return loss, grad_x, grads, dmod_l, dmod_c


def kernel(x, c, ctx, c_ctx, w_mod, b_mod, norm1_g, w_in, b_gate, q_norm_g, kv_norm_g, w_uq, w_ukv, w_o_attn, lru_conv_w, lru_conv_b, lru_w_a, lru_b_a, lru_w_x, lru_b_x, lru_lambda, w_o_lru, w_out, norm2_g, w_up, ffn_conv_w, ffn_conv_b, w_down, final_g, loss_target, m_c_ctx, m_w_mod, m_b_mod, m_norm1_g, m_w_in, m_b_gate, m_q_norm_g, m_kv_norm_g, m_w_uq, m_w_ukv, m_w_o_attn, m_lru_conv_w, m_lru_conv_b, m_lru_w_a, m_lru_b_a, m_lru_w_x, m_lru_b_x, m_lru_lambda, m_w_o_lru, m_w_out, m_norm2_g, m_w_up, m_ffn_conv_w, m_ffn_conv_b, m_w_down, m_final_g, v_c_ctx, v_w_mod, v_b_mod, v_norm1_g, v_w_in, v_b_gate, v_q_norm_g, v_kv_norm_g, v_w_uq, v_w_ukv, v_w_o_attn, v_lru_conv_w, v_lru_conv_b, v_lru_w_a, v_lru_b_a, v_lru_w_x, v_lru_b_x, v_lru_lambda, v_w_o_lru, v_w_out, v_norm2_g, v_w_up, v_ffn_conv_w, v_ffn_conv_b, v_w_down, v_final_g):
    given = dict(locals())
    strip = lambda name, a: a if name in ('c_ctx', 'final_g') else a[0]
    wsh = {n: strip(n, given[n]) for n in WEIGHTS}
    msh = {n: strip(n, given['m_' + n]) for n in WEIGHTS}
    vsh = {n: strip(n, given['v_' + n]) for n in WEIGHTS}
    me = _my_index()

    small = _flat([c[0]] + [wsh[n] for n in SMALL_F32], F32, 8)
    small_all = all_gather("gather_small", small).reshape(N_DEV, -1)
    c_all = small_all[:, :D]
    full, at = {}, D
    for n in SMALL_F32:
        cnt = math.prod(wsh[n].shape)
        full[n] = _gathered_to_full(n, small_all[:, at:at + cnt].reshape((N_DEV,) + wsh[n].shape))
        at += cnt

    cond = jnp.concatenate([c_all, c_ctx[None], jnp.zeros((7, D), F32)], axis=0)
    sil = cond * jax.nn.sigmoid(cond)
    mod_cols = matmul("mod_proj", sil, wsh['w_mod'], 'nn', F32)
    mod_all = all_gather("gather_mod", mod_cols)
    mod_all = jnp.transpose(mod_all, (1, 0, 2)).reshape(16, 6 * D) + b_mod[0][None]
    mod_l = lax.dynamic_index_in_dim(mod_all, me, axis=0, keepdims=False)
    mod_c = mod_all[N_DEV]

    rb_shards = {n: _shard_to_rb(n, wsh[n]).astype(BF16) for n in BIG_BF16}
    (w_in_blocks,) = all_gather_multi("gather_w_in", [rb_shards['w_in']])
    later = [n for n in BIG_BF16 if n != 'w_in']
    weights_started, weights_sent = exchange_start("weights_send", 'gather', [rb_shards[n] for n in later],
                                                   after=[w_in_blocks, mod_all])
    for n in REPLICATED:
        if n not in ('c_ctx', 'b_mod'):
            full[n] = wsh[n]

    def arrive(names, after):
        if names == ('w_in',):
            return {'w_in': _rb_from_gathered('w_in', w_in_blocks), '_token': weights_sent}
        picked = [later.index(n) for n in names]
        lands = exchange_wait("weights_wait_" + names[0], 'gather',
                              tuple([part[i] for i in picked] for part in weights_started), after)
        return {n: _rb_from_gathered(n, lax.dynamic_update_slice_in_dim(land, rb_shards[n][None], me, axis=0))
                for n, land in zip(names, lands)}

    in_flight = {}

    def on_grad(n, g):
        chunks = _chunks_from_rb_grad(n, g)
        own = lax.dynamic_index_in_dim(chunks, me, axis=0, keepdims=True)
        started, token = exchange_start("grad_send_" + n, 'scatter', [chunks])
        in_flight[n] = (own, started)
        return token

    loss, grad_x, grads, dmod_l, dmod_c = local_step(x[0], ctx[0], loss_target[0], mod_l, mod_c, full, on_grad,
                                                     arrive)
    dmod = jnp.stack([dmod_l, dmod_c]).reshape(2 * 6 * D // FLAT_C, FLAT_C)
    dm = all_gather("gather_dmod", dmod).reshape(N_DEV, 2, 6 * D)
    dmod_c_tot = dm[0, 1]
    for p in range(1, N_DEV):
        dmod_c_tot = dmod_c_tot + dm[p, 1]
    dm16 = jnp.concatenate([dm[:, 0], dmod_c_tot[None], jnp.zeros((7, 6 * D), F32)], axis=0)
    ncol = 6 * D // N_DEV
    dm16_cols = lax.dynamic_slice_in_dim(dm16.reshape(16, N_DEV, ncol), me, 1, axis=1)[:, 0]
    grad_w_mod = matmul("g_w_mod", sil, dm16_cols, 'tn', F32)
    dsil = matmul("d_cond", dm16_cols, wsh['w_mod'], 'nt', F32)
    sg = jax.nn.sigmoid(c_ctx)
    grads['c_ctx'] = dsil[N_DEV] * (sg * (1.0 + c_ctx * (1.0 - sg)))
    grads['b_mod'] = dmod_l + dmod_c

    g_final = {'w_mod': grad_w_mod}
    reduced = {}
    for n in BIG_BF16 + ['lru_w_a', 'lru_w_x']:
        own, started = in_flight[n]
        (land,) = exchange_wait("grad_wait_" + n, 'scatter', started, dm)
        reduced[n] = sum_slots("sum_" + n, lax.dynamic_update_slice_in_dim(land, own, me, axis=0))
    for n in BIG_BF16:
        g_final[n] = _rb_to_shard(n, reduced[n])

    small_names = SMALL_F32 + [n for n in REPLICATED if n not in ('lru_w_a', 'lru_w_x')]
    partials = _flat([grads[n] for n in small_names] + [loss], F32, 8)
    parts_all, a_all, x_all = all_gather_multi("gather_small_grads", [partials, reduced['lru_w_a'], reduced['lru_w_x']])
    small_sum = sum_slots("sum_small", parts_all).reshape(-1)
    g_final['lru_w_a'], g_final['lru_w_x'] = a_all.reshape(wsh['lru_w_a'].shape), x_all.reshape(wsh['lru_w_x'].shape)
    at = 0
    for n in small_names:
        cnt = math.prod(full[n].shape) if n in SMALL_F32 else math.prod(wsh[n].shape)
        g = small_sum[at:at + cnt]
        if n in SMALL_F32:
            k = full[n].shape[0]
            g = lax.dynamic_index_in_dim(g.reshape(k, N_DEV, -1), me, axis=1, keepdims=False)
        g_final[n] = g.reshape(wsh[n].shape)
        at += cnt
    loss = small_sum[at]

    stepped = {}
    for n in ['w_mod'] + BIG_BF16:
        if n in COL_SHARDED and wsh[n].shape[1] % 128:
            outs = adamw("adamw_" + n, wsh[n].T, reduced[n], msh[n].T, vsh[n].T)
            stepped[n] = tuple(o.T for o in outs)
        else:
            stepped[n] = adamw("adamw_" + n, wsh[n], g_final[n], msh[n], vsh[n])
    rest = [n for n in WEIGHTS if n not in stepped]
    as2d = lambda a: a.reshape(-1, a.shape[-1])
    rest_out = adamw_many("adamw_small", *[[as2d(d[n]) for n in rest] for d in (wsh, g_final, msh, vsh)])
    stepped.update(zip(rest, rest_out))
    shaped = lambda n, a: a.reshape(given[n].shape)
    return (loss, grad_x[None],
            *[shaped(n, g_final[n]) for n in WEIGHTS],
            *[shaped(n, stepped[n][k]) for k in range(3) for n in WEIGHTS])
```

```python
import functools
import math

import jax
import jax.numpy as jnp
from jax import lax
from jax.experimental import pallas as pl
from jax.experimental.pallas import tpu as pltpu

F32 = jnp.float32
BF16 = jnp.bfloat16
MESH = pl.DeviceIdType.MESH

N_DEV = 8
D = 1024
N_HEADS = 8
HEAD_PAD = 128
QK_NOPE, QK_ROPE, V_HEAD = 64, 32, 64
QK_DIM = QK_NOPE + QK_ROPE
Q_RANK, KV_RANK = 384, 256
LRU_W, LRU_BLOCKS, LRU_BW = 1280, 10, 128
FFN = 2816
GRID_W = 64
ROPE_BASE = 10000.0
LRU_C = 8.0
EPS = 1e-6
Z_Q, Z_KV, Z_KR, Z_XB, Z_YB, Z_GL, Z_END = 0, 384, 640, 768, 2048, 3328, 5376
ADAM_LR, ADAM_B1, ADAM_B2, ADAM_EPS, ADAM_WD, ADAM_STEP = 0.001, 0.9, 0.999, 1e-08, 0.01, 10

VMEM_LIMIT = 52 * 1024 * 1024
FLAT_C = 512
BIG_ROWS = 256

WEIGHTS = ['c_ctx', 'w_mod', 'b_mod', 'norm1_g', 'w_in', 'b_gate', 'q_norm_g', 'kv_norm_g', 'w_uq', 'w_ukv',
           'w_o_attn', 'lru_conv_w', 'lru_conv_b', 'lru_w_a', 'lru_b_a', 'lru_w_x', 'lru_b_x', 'lru_lambda',
           'w_o_lru', 'w_out', 'norm2_g', 'w_up', 'ffn_conv_w', 'ffn_conv_b', 'w_down', 'final_g']
COL_SHARDED = ['w_in', 'w_uq', 'w_ukv', 'w_o_attn', 'lru_conv_w', 'lru_b_a', 'lru_b_x', 'lru_lambda', 'w_up',
               'ffn_conv_w']
ROW_SHARDED = ['w_o_lru', 'w_out', 'w_down']
BIG_BF16 = ['w_in', 'w_uq', 'w_ukv', 'w_o_attn', 'w_o_lru', 'w_out', 'w_up', 'w_down']
SMALL_F32 = ['lru_conv_w', 'lru_b_a', 'lru_b_x', 'lru_lambda', 'ffn_conv_w']
SHARDED = BIG_BF16 + SMALL_F32
REPLICATED = ['c_ctx', 'b_mod', 'norm1_g', 'b_gate', 'q_norm_g', 'kv_norm_g', 'lru_conv_b', 'lru_w_a', 'lru_w_x',
              'norm2_g', 'ffn_conv_b', 'final_g']


def _cparams(sem=None):
    return pltpu.CompilerParams(dimension_semantics=sem, vmem_limit_bytes=VMEM_LIMIT)


def _pick(n, cands):
    for c in cands:
        if c <= n and n % c == 0:
            return c
    return n


def _best_div(n, mult, cap):
    best = mult
    for d in range(mult, min(n, cap) + 1, mult):
        if n % d == 0:
            best = d
    return best


MXU_DIM = 256
ROW_TILES = (1088, 1024, 544, 512, 256, 128, 64, 32, 16, 8)
LANE_TILES = (2816, 1792, 1536, 1280, 1024, 768, 512, 256, 1408, 896, 640, 384, 128)
DEPTH_ROW_TILES = (2176, 2048, 1024, 512, 256, 1088, 128, 64, 32, 16, 8)
MATMUL_VMEM_BUDGET = 40 * 1024 * 1024
MXU_FILL_OK = 0.9


def _my_pos():
    return lax.axis_index("x"), lax.axis_index("y"), lax.axis_index("c")


def _my_index():
    x, y, c = _my_pos()
    return 4 * x + 2 * y + c


def all_gather_multi(name, shards):
    n_arr = len(shards)
    arrays = range(n_arr)

    def body(*refs):
        x_refs, out_refs = refs[:n_arr], refs[n_arr:2 * n_arr]
        send_sems, recv_sems, local_sems = refs[2 * n_arr:]
        x, y, c = _my_pos()
        me, sibling = (x, y, c), (x, y, 1 - c)
        chips = [(1 - x, y), (x, 1 - y), (1 - x, 1 - y)]

        def slot(a, px, py, pc):
            return out_refs[a].at[4 * px + 2 * py + pc]

        def copy(a, k, block, to, src=None):
            return pltpu.make_async_remote_copy(
                src_ref=slot(a, *block) if src is None else src, dst_ref=slot(a, *block),
                send_sem=send_sems.at[7 * a + k], recv_sem=recv_sems.at[7 * a + k], device_id=to,
                device_id_type=MESH)

        mine = [pltpu.make_async_copy(x_refs[a], slot(a, *me), local_sems.at[a]) for a in arrays]
        first = [copy(a, 1 + j, me, (*chip, c), src=x_refs[a]) for j, chip in enumerate(chips) for a in arrays]
        first += [copy(a, 0, me, sibling, src=x_refs[a]) for a in arrays]
        for cp in first + mine:
            cp.start()
        passed = []
        for j, chip in enumerate(chips):
            for a in arrays:
                copy(a, 1 + j, (*chip, c), me).wait_recv()
                passed.append(copy(a, 4 + j, (*chip, c), sibling))
                passed[-1].start()
        for a in arrays:
            copy(a, 0, sibling, me).wait_recv()
            for j, chip in enumerate(chips):
                copy(a, 4 + j, (*chip, 1 - c), me).wait_recv()
        for cp in first + passed:
            cp.wait_send()
        for cp in mine:
            cp.wait()

    hbm = pl.BlockSpec(memory_space=pl.ANY)
    return pl.pallas_call(
        body, name=name,
        out_shape=[jax.ShapeDtypeStruct((N_DEV,) + s.shape, s.dtype) for s in shards],
        in_specs=[hbm] * n_arr, out_specs=[hbm] * n_arr,
        scratch_shapes=[pltpu.SemaphoreType.DMA((7 * n_arr,)), pltpu.SemaphoreType.DMA((7 * n_arr,)),
                        pltpu.SemaphoreType.DMA((n_arr,))],
    )(*shards)


def all_gather(name, shard):
    return all_gather_multi(name, [shard])[0]


def all_to_all_multi(name, chunk_arrays):
    n_arr = len(chunk_arrays)
    arrays = range(n_arr)

    def body(*refs):
        x_refs, out_refs = refs[:n_arr], refs[n_arr:2 * n_arr]
        send_sems, recv_sems, local_sems = refs[2 * n_arr:]
        x, y, c = _my_pos()
        me = 4 * x + 2 * y + c
        mine = [pltpu.make_async_copy(x_refs[a].at[me], out_refs[a].at[me], local_sems.at[a]) for a in arrays]
        sends, arrivals = [], []
        for rel in (6, 4, 2, 7, 5, 3, 1):
            dx, dy, dc = (rel >> 2) & 1, (rel >> 1) & 1, rel & 1
            px, py, pc = x ^ dx, y ^ dy, c ^ dc
            peer = 4 * px + 2 * py + pc
            for a in arrays:
                k = 7 * a + rel - 1
                sends.append(pltpu.make_async_remote_copy(
                    src_ref=x_refs[a].at[peer], dst_ref=out_refs[a].at[me],
                    send_sem=send_sems.at[k], recv_sem=recv_sems.at[k],
                    device_id=(px, py, pc), device_id_type=MESH))
                arrivals.append(pltpu.make_async_remote_copy(
                    src_ref=x_refs[a].at[peer], dst_ref=out_refs[a].at[peer],
                    send_sem=send_sems.at[k], recv_sem=recv_sems.at[k],
                    device_id=(x, y, c), device_id_type=MESH))
        for cp in sends + mine:
            cp.start()
        for cp in arrivals:
            cp.wait_recv()
        for cp in sends:
            cp.wait_send()
        for cp in mine:
            cp.wait()

    hbm = pl.BlockSpec(memory_space=pl.ANY)
    return pl.pallas_call(
        body, name=name,
        out_shape=[jax.ShapeDtypeStruct(s.shape, s.dtype) for s in chunk_arrays],
        in_specs=[hbm] * n_arr, out_specs=[hbm] * n_arr,
        scratch_shapes=[pltpu.SemaphoreType.DMA((7 * n_arr,)), pltpu.SemaphoreType.DMA((7 * n_arr,)),
                        pltpu.SemaphoreType.DMA((n_arr,))],
    )(*chunk_arrays)


def _peers():
    x, y, c = _my_pos()
    out = []
    for rel in (6, 4, 2, 7, 5, 3, 1):
        px, py, pc = x ^ ((rel >> 2) & 1), y ^ ((rel >> 1) & 1), c ^ (rel & 1)
        out.append((rel - 1, (px, py, pc), 4 * px + 2 * py + pc))
    return out


def _exchange_copies(mode, src_refs, land_refs, send_sems, recv_sems):
    x, y, c = _my_pos()
    me = 4 * x + 2 * y + c
    sends, arrivals = [], []
    for k, peer_pos, peer in _peers():
        for a, (src, land) in enumerate(zip(src_refs, land_refs)):
            piece = src.at[peer] if mode == 'scatter' else src
            sems = dict(send_sem=send_sems[a].at[k], recv_sem=recv_sems[a].at[k], device_id_type=MESH)
            sends.append(pltpu.make_async_remote_copy(src_ref=piece, dst_ref=land.at[me], device_id=peer_pos, **sems))
            arrivals.append(pltpu.make_async_remote_copy(src_ref=piece, dst_ref=land.at[peer], device_id=(x, y, c), **sems))
    return sends, arrivals


_HBM = pl.BlockSpec(memory_space=pltpu.HBM)
_SEM = pl.BlockSpec(memory_space=pltpu.SEMAPHORE)


def exchange_start(name, mode, arrays, after=()):
    n_arr, n_after = len(arrays), len(after)
    land_shapes = [a.shape if mode == 'scatter' else (N_DEV,) + a.shape for a in arrays]

    def body(*refs):
        src_refs, land_refs = refs[:n_arr], refs[n_arr:2 * n_arr]
        refs = refs[n_after:]
        send_sems, recv_sems = refs[2 * n_arr:3 * n_arr], refs[3 * n_arr:4 * n_arr]
        sends, _ = _exchange_copies(mode, src_refs, land_refs, send_sems, recv_sems)
        for cp in sends:
            cp.start()
        token = refs[-1]
        token[...] = jnp.zeros_like(token)

    sem = pltpu.SemaphoreType.DMA((N_DEV - 1,))
    res = pl.pallas_call(
        body, name=name,
        out_shape=[sem] * (2 * n_arr) + [pltpu.HBM(a.shape, a.dtype) for a in arrays]
        + [pltpu.HBM(s, a.dtype) for s, a in zip(land_shapes, arrays)] + [jax.ShapeDtypeStruct((8, 128), F32)],
        in_specs=[_HBM] * (2 * n_arr) + [pl.BlockSpec(memory_space=pl.ANY)] * n_after,
        out_specs=[_SEM] * (2 * n_arr) + [_HBM] * (2 * n_arr) + [pl.BlockSpec(memory_space=pltpu.VMEM)],
        input_output_aliases={i: 2 * n_arr + i for i in range(2 * n_arr)},
        compiler_params=pltpu.CompilerParams(has_side_effects=pltpu.SideEffectType.DATAFLOW_SIDE_EFFECTING),
    )(*[pltpu.with_memory_space_constraint(a, pltpu.HBM) for a in arrays],
      *[pltpu.with_memory_space_constraint(lax.empty(s, a.dtype), pltpu.HBM) for s, a in zip(land_shapes, arrays)],
      *after)
    return (res[:n_arr], res[n_arr:2 * n_arr], res[2 * n_arr:3 * n_arr], res[3 * n_arr:4 * n_arr]), res[-1]


def exchange_wait(name, mode, started, after):
    send_sems, recv_sems, thru, land = started
    n_arr = len(thru)

    def body(*refs):
        src_refs, land_refs = refs[:n_arr], refs[n_arr:2 * n_arr]
        s_sems, r_sems = refs[2 * n_arr:3 * n_arr], refs[3 * n_arr:4 * n_arr]
        sends, arrivals = _exchange_copies(mode, src_refs, land_refs, s_sems, r_sems)
        for cp in sends:
            cp.wait_send()
        for cp in arrivals:
            cp.wait_recv()

    res = pl.pallas_call(
        body, name=name,
        out_shape=[pltpu.HBM(a.shape, a.dtype) for a in thru] + [pltpu.HBM(a.shape, a.dtype) for a in land],
        in_specs=[_HBM] * (2 * n_arr) + [_SEM] * (2 * n_arr) + [pl.BlockSpec(memory_space=pl.ANY)],
        out_specs=[_HBM] * (2 * n_arr),
        input_output_aliases={i: i for i in range(2 * n_arr)},
        compiler_params=pltpu.CompilerParams(has_side_effects=pltpu.SideEffectType.DATAFLOW_SIDE_EFFECTING),
    )(*thru, *land, *send_sems, *recv_sems, after)
    return res[n_arr:]


def sum_slots(name, slots):
    _, r, ccols = slots.shape
    tc = _pick(ccols, (256, 128))

    def body(s_ref, o_ref):
        acc = s_ref[0].astype(F32)
        for p in range(1, N_DEV):
            acc = acc + s_ref[p].astype(F32)
        o_ref[...] = acc

    return pl.pallas_call(
        body, name=name, grid=(ccols // tc,),
        out_shape=jax.ShapeDtypeStruct((r, ccols), F32),
        in_specs=[pl.BlockSpec((N_DEV, r, tc), lambda j: (0, 0, j))],
        out_specs=pl.BlockSpec((r, tc), lambda j: (0, j)),
        compiler_params=_cparams(("parallel",)),
    )(slots)


def _mxu_fill(t):
    return t / (-(-t // MXU_DIM) * MXU_DIM)


def _matmul_tiles(mode, m_extent, n, k_extent, k_total, itemsizes):
    a_bytes, b_bytes, o_bytes = itemsizes
    m_cands = [c for c in (LANE_TILES if mode == 'tn' else ROW_TILES) if m_extent % c == 0] or [m_extent]
    k_cands = [c for c in (DEPTH_ROW_TILES if mode == 'tn' else LANE_TILES) if k_extent % c == 0] or [k_extent]
    n_cands = [c for c in LANE_TILES if n % c == 0] or [n]
    best = None
    for tm in m_cands:
        for tk in k_cands:
            for tn in n_cands:
                f32_tiles = 2 if k_total // tk > 1 else 1
                vmem = 2 * (tm * tk * a_bytes + tk * tn * b_bytes + tm * tn * o_bytes) + tm * tn * 4 * f32_tiles
                if vmem > MATMUL_VMEM_BUDGET:
                    continue
                key = (_mxu_fill(tn) * _mxu_fill(tk) >= MXU_FILL_OK, tm * tn * tk)
                if best is None or key > best[0]:
                    best = (key, (tm, tn, tk))
    assert best is not None, (mode, m_extent, n, k_extent)
    return best[1]


def matmul(name, a, b, mode, out_dtype, after=()):
    after = [t for t in after if t is not None]
    pieces, a_rows, a_cols = (1,) + a.shape if a.ndim == 2 else a.shape
    if mode == 'nn':
        (m, k), (k2, n) = (a_rows, pieces * a_cols), b.shape
    elif mode == 'nt':
        (m, k), (n, k2) = (a_rows, pieces * a_cols), b.shape
    else:
        (k, m), (k2, n) = (a_rows, pieces * a_cols), b.shape
    assert k == k2, (name, a.shape, b.shape, mode)
    tm, tn, tk = _matmul_tiles(mode, a_cols if mode == 'tn' else m, n, k if mode == 'tn' else a_cols, k,
                               (a.dtype.itemsize, b.dtype.itemsize, jnp.dtype(out_dtype).itemsize))
    nk = k // tk
    per_piece = a_cols // (tm if mode == 'tn' else tk)
    if a.ndim == 2:
        a_block = lambda rows, cols, at: pl.BlockSpec((rows, cols), at)
    else:
        a_block = lambda rows, cols, at: pl.BlockSpec(
            (None, rows, cols), lambda i, j, kk: (at(i, j, kk)[1] // per_piece, at(i, j, kk)[0],
                                                  at(i, j, kk)[1] % per_piece))
    if mode == 'nn':
        a_spec = a_block(tm, tk, lambda i, j, kk: (i, kk))
        b_spec = pl.BlockSpec((tk, tn), lambda i, j, kk: (kk, j))
        dn = (((1,), (0,)), ((), ()))
    elif mode == 'nt':
        a_spec = a_block(tm, tk, lambda i, j, kk: (i, kk))
        b_spec = pl.BlockSpec((tn, tk), lambda i, j, kk: (j, kk))
        dn = (((1,), (1,)), ((), ()))
    else:
        a_spec = a_block(tk, tm, lambda i, j, kk: (kk, i))
        b_spec = pl.BlockSpec((tk, tn), lambda i, j, kk: (kk, j))
        dn = (((0,), (0,)), ((), ()))

    def product(a_ref, b_ref):
        return lax.dot_general(a_ref[...].astype(BF16), b_ref[...].astype(BF16), dn, preferred_element_type=F32)

    n_after = len(after)

    def body_one(a_ref, b_ref, *rest):
        o_ref = rest[n_after]
        o_ref[...] = product(a_ref, b_ref).astype(o_ref.dtype)

    def body(a_ref, b_ref, *rest):
        o_ref, acc_ref = rest[n_after:]
        kk = pl.program_id(2)

        @pl.when(kk == 0)
        def _():
            acc_ref[...] = jnp.zeros_like(acc_ref)

        acc_ref[...] += product(a_ref, b_ref)

        @pl.when(kk == nk - 1)
        def _():
            o_ref[...] = acc_ref[...].astype(o_ref.dtype)

    return pl.pallas_call(
        body_one if nk == 1 else body, name=name, grid=(m // tm, n // tn, nk),
        out_shape=jax.ShapeDtypeStruct((m, n), out_dtype),
        in_specs=[a_spec, b_spec] + [pl.BlockSpec(memory_space=pl.ANY)] * n_after,
        out_specs=pl.BlockSpec((tm, tn), lambda i, j, kk: (i, j)),
        scratch_shapes=[] if nk == 1 else [pltpu.VMEM((tm, tn), F32)],
        compiler_params=_cparams(("parallel", "parallel", "arbitrary")),
    )(a, b, *after)


def rowwise(name, fn, rows, params, out_rows, out_accs, n_rows, t_lat, tm):
    nb = n_rows // tm
    in_specs, piece_counts = [], []
    operands = []
    for arr, off, width in rows:
        g = math.gcd(off, width) if off else width
        assert g % 128 == 0 or (off == 0 and width == arr.shape[1]), (name, off, width)
        cnt = width // g
        last = arr.shape[0] // tm - 1
        clamp = arr.shape[0] < n_rows
        for p in range(cnt):
            cb = off // g + p
            if clamp:
                in_specs.append(pl.BlockSpec((tm, g), lambda i, cb=cb, last=last: (jnp.minimum(i, last), cb)))
            else:
                in_specs.append(pl.BlockSpec((tm, g), lambda i, cb=cb: (i, cb)))
            operands.append(arr)
        piece_counts.append(cnt)
    for p in params:
        in_specs.append(pl.BlockSpec(p.shape, lambda i, nd=p.ndim: (0,) * nd))
        operands.append(p)
    n_in = sum(piece_counts)
    n_par = len(params)
    n_or = len(out_rows)
    out_shape = [jax.ShapeDtypeStruct((n_rows, w), dt) for w, dt in out_rows]
    out_shape += [jax.ShapeDtypeStruct(s, F32) for s in out_accs]
    out_specs = [pl.BlockSpec((tm, w), lambda i: (i, 0)) for w, _ in out_rows]
    out_specs += [pl.BlockSpec(s, lambda i, nd=len(s): (0,) * nd) for s in out_accs]

    def body(*refs):
        in_refs, par_refs = refs[:n_in], refs[n_in:n_in + n_par]
        orow_refs = refs[n_in + n_par:n_in + n_par + n_or]
        oacc_refs = refs[n_in + n_par + n_or:]
        i = pl.program_id(0)
        tiles, at = [], 0
        for cnt in piece_counts:
            parts = [in_refs[at + p][...].astype(F32) for p in range(cnt)]
            tiles.append(parts[0] if cnt == 1 else jnp.concatenate(parts, axis=1))
            at += cnt
        is_ctx = i * tm >= t_lat
        outs, accs = fn(is_ctx, tiles, [p[...] for p in par_refs])
        for o_ref, o in zip(orow_refs, outs):
            o_ref[...] = o.astype(o_ref.dtype)
        if oacc_refs:
            @pl.when(i == 0)
            def _():
                for a_ref in oacc_refs:
                    a_ref[...] = jnp.zeros_like(a_ref)
            for a_ref, a in zip(oacc_refs, accs):
                a_ref[...] += a.astype(F32)

    res = pl.pallas_call(
        body, name=name, grid=(nb,),
        out_shape=out_shape, in_specs=in_specs, out_specs=out_specs,
        compiler_params=_cparams(("arbitrary",)),
    )(*operands)
    return res[:n_or], res[n_or:]


def _rms(x, g):
    return x * lax.rsqrt(jnp.mean(x * x, axis=-1, keepdims=True) + EPS) * g


def _norm_mod(x, g, sc, sh):
    return _rms(x, g) * (1.0 + sc) + sh


def _sigmoid(x):
    return 0.5 * jnp.tanh(0.5 * x) + 0.5


def _silu(x):
    return x * _sigmoid(x)


def _gelu(x):
    return 0.5 * x * (1.0 + jnp.tanh(math.sqrt(2.0 / math.pi) * (x + 0.044715 * (x * x * x))))


def _sel(is_ctx, p):
    return jnp.where(is_ctx, p[1:2], p[0:1])


def _seg_acc(is_ctx, v):
    rows = lax.broadcasted_iota(jnp.int32, (2, v.shape[1]), 0)
    return jnp.where(rows == is_ctx.astype(jnp.int32), jnp.broadcast_to(v, (2, v.shape[1])), 0.0)


def _rsum(v):
    return jnp.sum(v, axis=0, keepdims=True)


def _shift_rows(x, o, t_lat, n):
    if o == 0:
        return x
    y = pltpu.roll(x, (-o) % n, 0)
    t = lax.broadcasted_iota(jnp.int32, x.shape, 0)
    src = t + o
    ok = (src >= 0) & (src < n) & ((src >= t_lat) == (t >= t_lat))
    return jnp.where(ok, y, 0.0)


def conv_fwd(name, xarr, col_off, width, w, b, left, n_rows, t_lat, out_dtype, cb=128):
    taps = w.shape[0]
    assert col_off % cb == 0 and width % cb == 0

    def body(x_ref, w_ref, b_ref, o_ref):
        x = x_ref[...].astype(F32)
        acc = jnp.broadcast_to(b_ref[...], x.shape)
        for k in range(taps):
            acc = acc + _shift_rows(x, k - left, t_lat, n_rows) * w_ref[k:k + 1, :]
        o_ref[...] = acc.astype(o_ref.dtype)

    return pl.pallas_call(
        body, name=name, grid=(width // cb,),
        out_shape=jax.ShapeDtypeStruct((n_rows, width), out_dtype),
        in_specs=[pl.BlockSpec((n_rows, cb), lambda j: (0, col_off // cb + j)),
                  pl.BlockSpec((taps, cb), lambda j: (0, j)),
                  pl.BlockSpec((1, cb), lambda j: (0, j))],
        out_specs=pl.BlockSpec((n_rows, cb), lambda j: (0, j)),
        compiler_params=_cparams(("parallel",)),
    )(xarr, w, b)


def conv_bwd(name, dout, xarr, col_off, width, w, left, n_rows, t_lat, cb=128):
    taps = w.shape[0]

    def body(d_ref, x_ref, w_ref, dx_ref, dw_ref, db_ref):
        d = d_ref[...].astype(F32)
        x = x_ref[...].astype(F32)
        dx = jnp.zeros_like(d)
        dws = []
        for k in range(taps):
            dx = dx + _shift_rows(d, left - k, t_lat, n_rows) * w_ref[k:k + 1, :]
            dws.append(_rsum(d * _shift_rows(x, k - left, t_lat, n_rows)))
        dx_ref[...] = dx.astype(dx_ref.dtype)
        dw_ref[...] = jnp.concatenate(dws, axis=0)
        db_ref[...] = _rsum(d)

    return pl.pallas_call(
        body, name=name, grid=(width // cb,),
        out_shape=[jax.ShapeDtypeStruct((n_rows, width), BF16), jax.ShapeDtypeStruct((taps, width), F32),
                   jax.ShapeDtypeStruct((1, width), F32)],
        in_specs=[pl.BlockSpec((n_rows, cb), lambda j: (0, j)),
                  pl.BlockSpec((n_rows, cb), lambda j: (0, col_off // cb + j)),
                  pl.BlockSpec((taps, cb), lambda j: (0, j))],
        out_specs=[pl.BlockSpec((n_rows, cb), lambda j: (0, j)), pl.BlockSpec((taps, cb), lambda j: (0, j)),
                   pl.BlockSpec((1, cb), lambda j: (0, j))],
        compiler_params=_cparams(("parallel",)),
    )(dout, xarr, w)


def _ffn_conv(a, w_ref, b_ref, t_lat):
    shifted = [_shift_rows(a, k - 1, t_lat, t_lat) for k in range(3)]
    ac = jnp.broadcast_to(b_ref[...], a.shape)
    for k in range(3):
        ac = ac + shifted[k] * w_ref[k:k + 1, :]
    return ac, shifted


def ffn_mix_fwd(u, w, b, t_lat, cb=128):
    nblk = FFN // cb

    def body(a_ref, g_ref, w_ref, b_ref, f_ref):
        ac, _ = _ffn_conv(a_ref[...].astype(F32), w_ref, b_ref, t_lat)
        f_ref[...] = (_silu(ac) * g_ref[...].astype(F32)).astype(f_ref.dtype)

    col = lambda shape, off=0: pl.BlockSpec(shape, lambda j: (0, off + j))
    return pl.pallas_call(
        body, name="ffn_mix", grid=(nblk,),
        out_shape=jax.ShapeDtypeStruct((t_lat, FFN), BF16),
        in_specs=[col((t_lat, cb)), col((t_lat, cb), nblk), col((3, cb)), col((1, cb))],
        out_specs=col((t_lat, cb)),
        compiler_params=_cparams(("parallel",)),
    )(u, u, w, b)


def ffn_mix_bwd(u, df, w, b, t_lat, cb=128):
    nblk = FFN // cb

    def body(a_ref, g_ref, df_ref, w_ref, b_ref, du_ref, dw_ref, db_ref):
        ac, shifted = _ffn_conv(a_ref[...].astype(F32), w_ref, b_ref, t_lat)
        d = df_ref[...].astype(F32)
        s = _sigmoid(ac)
        du_ref[1] = (d * (ac * s)).astype(du_ref.dtype)
        dac = d * g_ref[...].astype(F32) * (s * (1.0 + ac * (1.0 - s)))
        da = jnp.zeros_like(dac)
        for k in range(3):
            da = da + _shift_rows(dac, 1 - k, t_lat, t_lat) * w_ref[k:k + 1, :]
        du_ref[0] = da.astype(du_ref.dtype)
        dw_ref[...] = jnp.concatenate([_rsum(dac * shifted[k]) for k in range(3)], axis=0)
        db_ref[...] = _rsum(dac)

    col = lambda shape, off=0: pl.BlockSpec(shape, lambda j: (0, off + j))
    return pl.pallas_call(
        body, name="ffn_mix_bwd", grid=(nblk,),
        out_shape=[jax.ShapeDtypeStruct((2, t_lat, FFN), BF16),
                   jax.ShapeDtypeStruct((3, FFN), F32), jax.ShapeDtypeStruct((1, FFN), F32)],
        in_specs=[col((t_lat, cb)), col((t_lat, cb), nblk), col((t_lat, cb)), col((3, cb)), col((1, cb))],
        out_specs=[pl.BlockSpec((2, t_lat, cb), lambda j: (0, 0, j)), col((3, cb)), col((1, cb))],
        compiler_params=_cparams(("parallel",)),
    )(u, u, df, w, b)


def _chunk_order(direction, nb, nbl):
    if direction == 'f':
        return lambda s: ((s + nbl) % nb, 0)
    return lambda s: (nb - 1 - s, 0)


def _adjoint_order(direction, nb, nbl):
    if direction == 'f':
        return lambda s: ((nb - 1 - s + nbl) % nb, 0)
    return lambda s: (s, 0)


SUBLANES = 8


def _chunk_scan(a, b, carry, rev):
    tc = a.shape[0]
    row = lax.broadcasted_iota(jnp.int32, a.shape, 0)
    in_tile = jnp.bitwise_and(row, SUBLANES - 1)
    for k in (1, 2, 4):
        shift = tc - k if rev else k
        edge = in_tile >= SUBLANES - k if rev else in_tile < k
        b = jnp.where(edge, b, a * pltpu.roll(b, shift, 0) + b)
        a = jnp.where(edge, a, a * pltpu.roll(a, shift, 0))
    nt = tc // SUBLANES
    hs = [None] * nt
    c = carry
    for kt in range(nt):
        k = nt - 1 - kt if rev else kt
        h = b[k * SUBLANES:(k + 1) * SUBLANES] + a[k * SUBLANES:(k + 1) * SUBLANES] * c
        hs[k] = h
        c = h[0:1] if rev else h[SUBLANES - 1:SUBLANES]
    h = jnp.concatenate(hs, axis=0)
    if rev:
        return h, jnp.where(row == tc - 1, carry, pltpu.roll(h, tc - 1, 0)), c
    return h, jnp.where(row == 0, carry, pltpu.roll(h, 1, 0)), c


def scan_fwd(name, a, u, direction, n_rows, t_lat):
    w = a.shape[1]
    tc = _pick(math.gcd(t_lat, n_rows), (256, 128))
    nb, nbl = n_rows // tc, t_lat // tc
    order = _chunk_order(direction, nb, nbl)
    rev = direction == 'b'

    def body(a_ref, u_ref, h_ref, hp_ref, carry):
        @pl.when(pl.program_id(0) == 0)
        def _():
            carry[...] = jnp.zeros_like(carry)

        h_ref[...], hp_ref[...], carry[...] = _chunk_scan(a_ref[...], u_ref[...], carry[...], rev)

    spec = pl.BlockSpec((tc, w), order)
    return pl.pallas_call(
        body, name=name, grid=(nb,),
        out_shape=[jax.ShapeDtypeStruct((n_rows, w), F32)] * 2,
        in_specs=[spec, spec], out_specs=[spec, spec],
        scratch_shapes=[pltpu.VMEM((1, w), F32)],
        compiler_params=_cparams(("arbitrary",)),
    )(a, u)


def scan_adj(name, a, dh, hprev, direction, n_rows, t_lat):
    w = a.shape[1]
    tc = _pick(math.gcd(t_lat, n_rows), (256, 128))
    nb, nbl = n_rows // tc, t_lat // tc
    order = _adjoint_order(direction, nb, nbl)
    rev = direction == 'f'

    def dh_order(s):
        c, _ = order(s)
        return (jnp.minimum(c, nbl - 1), 0)

    def body(a_ref, dh_ref, hp_ref, du_ref, da_ref, carry):
        s = pl.program_id(0)

        @pl.when(s == 0)
        def _():
            carry[...] = jnp.zeros_like(carry)

        chunk, _ = order(s)
        live = (chunk < nbl).astype(F32)

        av = a_ref[...]
        dv = dh_ref[...] * live
        _, c_next, carry[...] = _chunk_scan(av, av * dv, carry[...], rev)
        lam = dv + c_next
        du_ref[...] = lam
        da_ref[...] = lam * hp_ref[...]

    spec = pl.BlockSpec((tc, w), order)
    return pl.pallas_call(
        body, name=name, grid=(nb,),
        out_shape=[jax.ShapeDtypeStruct((n_rows, w), F32)] * 2,
        in_specs=[spec, pl.BlockSpec((tc, w), dh_order), spec], out_specs=[spec, spec],
        scratch_shapes=[pltpu.VMEM((1, w), F32)],
        compiler_params=_cparams(("arbitrary",)),
    )(a, dh, hprev)


def _neg_expm1(y):
    series = -(y * (1.0 + y * (0.5 + y * (1.0 / 6.0 + y * (1.0 / 24.0)))))
    return jnp.where(y > -0.03, series, 1.0 - jnp.exp(y))


def _gate_elem(pre_r, pre_i, xc, b_a, b_x, sp):
    r = _sigmoid(pre_r + b_a)
    i = _sigmoid(pre_i + b_x)
    log_a = (-LRU_C) * r * sp
    a = jnp.exp(log_a)
    mult = jnp.sqrt(_neg_expm1(2.0 * log_a))
    return a, mult * (i * xc)


def _gate_elem_bwd(pre_r, pre_i, xc, b_a, b_x, sp, da, du):
    r = _sigmoid(pre_r + b_a)
    i = _sigmoid(pre_i + b_x)
    log_a = (-LRU_C) * r * sp
    a = jnp.exp(log_a)
    m2 = _neg_expm1(2.0 * log_a)
    inv_mult = lax.rsqrt(m2)
    g = du * (m2 * inv_mult)
    d_mult = du * (i * xc)
    d_log_a = (da - d_mult * a * inv_mult) * a
    d_pre_r = d_log_a * ((-LRU_C) * sp) * (r * (1.0 - r))
    d_pre_i = g * xc * (i * (1.0 - i))
    return d_pre_r, d_pre_i, g * i, _rsum(d_log_a * ((-LRU_C) * r))


def _blockdiag(xb16, w_ref_val, d):
    outs = []
    for n in range(LRU_BLOCKS):
        outs.append(jnp.dot(xb16[:, n * LRU_BW:(n + 1) * LRU_BW], w_ref_val[d * LRU_BLOCKS + n],
                            preferred_element_type=F32))
    return jnp.concatenate(outs, axis=1)


def gates_fwd(xc, w_a, w_x, b_a, b_x, sp, n_rows, t_lat, tm):
    def fn(is_ctx, rows, params):
        (x,), (wa, wx, ba, bx, spv) = rows, params
        xb16 = x.astype(BF16)
        outs = []
        for d in range(2):
            a, u = _gate_elem(_blockdiag(xb16, wa, d), _blockdiag(xb16, wx, d), x,
                              ba[d:d + 1], bx[d:d + 1], spv[d:d + 1])
            outs += [a, u]
        return outs, []

    (a_f, u_f, a_b, u_b), _ = rowwise("gates_fwd", fn, [(xc, 0, LRU_W)], [w_a, w_x, b_a, b_x, sp],
                                      [(LRU_W, F32)] * 4, [], n_rows, t_lat, tm)
    return a_f, u_f, a_b, u_b


def gates_bwd(xc, da_f, du_f, da_b, du_b, w_a, w_x, b_a, b_x, sp, n_rows, t_lat, tm):
    def fn(is_ctx, rows, params):
        (x, daf, duf, dab, dub), (wa, wx, ba, bx, spv) = rows, params
        xb16 = x.astype(BF16)
        dxc = jnp.zeros_like(x)
        dwa, dwx, dba, dbx, dsp = [], [], [], [], []
        for d, (da, du) in enumerate(((daf, duf), (dab, dub))):
            dpr, dpi, dx_e, dsp_d = _gate_elem_bwd(_blockdiag(xb16, wa, d), _blockdiag(xb16, wx, d), x,
                                                   ba[d:d + 1], bx[d:d + 1], spv[d:d + 1], da, du)
            dba_d, dbx_d = _rsum(dpr), _rsum(dpi)
            dxc = dxc + dx_e
            dpr16, dpi16 = dpr.astype(BF16), dpi.astype(BF16)
            back = []
            for n in range(LRU_BLOCKS):
                sl = slice(n * LRU_BW, (n + 1) * LRU_BW)
                nt_dims = (((1,), (1,)), ((), ()))
                back.append(lax.dot_general(dpr16[:, sl], wa[d * LRU_BLOCKS + n], nt_dims, preferred_element_type=F32)
                            + lax.dot_general(dpi16[:, sl], wx[d * LRU_BLOCKS + n], nt_dims,
                                              preferred_element_type=F32))
                tn_dims = (((0,), (0,)), ((), ()))
                dwa.append(lax.dot_general(xb16[:, sl], dpr16[:, sl], tn_dims, preferred_element_type=F32)[None])
                dwx.append(lax.dot_general(xb16[:, sl], dpi16[:, sl], tn_dims, preferred_element_type=F32)[None])
            dxc = dxc + jnp.concatenate(back, axis=1)
            dba.append(dba_d)
            dbx.append(dbx_d)
            dsp.append(dsp_d)
        cat0 = lambda xs: jnp.concatenate(xs, axis=0)
        return [dxc], [cat0(dwa), cat0(dwx), cat0(dba), cat0(dbx), cat0(dsp)]

    (dxc,), accs = rowwise("gates_bwd", fn,
                           [(xc, 0, LRU_W), (da_f, 0, LRU_W), (du_f, 0, LRU_W), (da_b, 0, LRU_W), (du_b, 0, LRU_W)],
                           [w_a, w_x, b_a, b_x, sp], [(LRU_W, F32)],
                           [(2 * LRU_BLOCKS, LRU_BW, LRU_BW)] * 2 + [(2, LRU_W)] * 3, n_rows, t_lat, tm)
    return dxc, accs


def _rope_tables(t_lat, n_rows):
    rows = t_lat // GRID_W
    row_ids = jnp.repeat(jnp.arange(rows), GRID_W).astype(F32)
    col_ids = jnp.tile(jnp.arange(GRID_W), rows).astype(F32)
    axis_dim = QK_ROPE // 2
    inv = 1.0 / (ROPE_BASE ** (jnp.arange(0, axis_dim, 2, dtype=F32) / axis_dim))
    ang = jnp.concatenate([row_ids[:, None] * inv, col_ids[:, None] * inv], axis=-1)
    cos, sin = jnp.cos(ang), jnp.sin(ang)
    half = QK_ROPE // 2
    ones, zeros = jnp.ones((t_lat, QK_NOPE), F32), jnp.zeros((t_lat, QK_NOPE), F32)
    pad1, pad0 = jnp.ones((t_lat, HEAD_PAD - QK_DIM), F32), jnp.zeros((t_lat, HEAD_PAD - QK_DIM), F32)
    zh = jnp.zeros((t_lat, half), F32)
    c_tab = jnp.concatenate([ones, cos, cos, pad1], axis=1)
    s1 = jnp.concatenate([zeros, -sin, zh, pad0], axis=1)
    s2 = jnp.concatenate([zeros, zh, sin, pad0], axis=1)
    n_ctx = n_rows - t_lat
    c_tab = jnp.concatenate([c_tab, jnp.ones((n_ctx, HEAD_PAD), F32)], axis=0)
    s1 = jnp.concatenate([s1, jnp.zeros((n_ctx, HEAD_PAD), F32)], axis=0)
    s2 = jnp.concatenate([s2, jnp.zeros((n_ctx, HEAD_PAD), F32)], axis=0)
    return c_tab, s1, s2


def _rope(x, c, s1, s2):
    half = QK_ROPE // 2
    return x * c + pltpu.roll(x, HEAD_PAD - half, 1) * s1 + pltpu.roll(x, half, 1) * s2


def _rope_t(dy, c, s1, s2):
    half = QK_ROPE // 2
    return dy * c + pltpu.roll(dy * s1, half, 1) + pltpu.roll(dy * s2, HEAD_PAD - half, 1)


def _heads(x):
    return [x[:, h * HEAD_PAD:(h + 1) * HEAD_PAD] for h in range(N_HEADS)]


Q_SCALE = QK_DIM ** -0.5 * math.log2(math.e)


def attn_fwd(q, k, v, t_lat, n_rows, tq):
    def body(q_ref, k_ref, v_ref, o_ref, lse_ref):
        s = lax.dot_general(q_ref[...], k_ref[...], (((1,), (1,)), ((), ())), preferred_element_type=F32)
        m = jnp.max(s, axis=-1, keepdims=True)
        p = jnp.exp2(s - m)
        l = jnp.sum(p, axis=-1, keepdims=True)
        o = jnp.dot(p.astype(BF16), v_ref[...], preferred_element_type=F32) / l
        o_ref[...] = o.astype(o_ref.dtype)
        lse_ref[...] = jnp.broadcast_to(m + jnp.log2(l), lse_ref.shape)

    qspec = pl.BlockSpec((tq, HEAD_PAD), lambda h, i: (i, h))
    kspec = pl.BlockSpec((n_rows, HEAD_PAD), lambda h, i: (0, h))
    return pl.pallas_call(
        body, name="attn_fwd", grid=(N_HEADS, t_lat // tq),
        out_shape=[jax.ShapeDtypeStruct((t_lat, N_HEADS * HEAD_PAD), BF16),
                   jax.ShapeDtypeStruct((t_lat, N_HEADS * HEAD_PAD), F32)],
        in_specs=[qspec, kspec, kspec], out_specs=[qspec, qspec],
        compiler_params=_cparams(("parallel", "arbitrary")),
    )(q, k, v)


def attn_bwd(q, k, v, o, do, lse, t_lat, n_rows, tq):
    scale = QK_DIM ** -0.5
    nq = t_lat // tq
    nt = (((1,), (1,)), ((), ()))
    tn = (((0,), (0,)), ((), ()))

    def body(q_ref, k_ref, v_ref, o_ref, do_ref, lse_ref, dq_ref, dk_ref, dv_ref):
        @pl.when(pl.program_id(1) == 0)
        def _():
            dk_ref[...] = jnp.zeros_like(dk_ref)
            dv_ref[...] = jnp.zeros_like(dv_ref)

        qv, kv, vv, dov = q_ref[...], k_ref[...], v_ref[...], do_ref[...]
        s = lax.dot_general(qv, kv, nt, preferred_element_type=F32)
        p = jnp.exp2(s - lse_ref[:, 0:1])
        dv_ref[...] += lax.dot_general(p.astype(BF16), dov, tn, preferred_element_type=F32)
        dp = lax.dot_general(dov, vv, nt, preferred_element_type=F32)
        delta = jnp.sum(dov.astype(F32) * o_ref[...].astype(F32), axis=-1, keepdims=True)
        ds = (p * (dp - delta)).astype(BF16)
        dq_ref[...] = jnp.dot(ds, kv, preferred_element_type=F32) * scale
        dk_ref[...] += lax.dot_general(ds, qv, tn, preferred_element_type=F32)

        @pl.when(pl.program_id(1) == nq - 1)
        def _():
            dk_ref[...] = dk_ref[...] * (scale / Q_SCALE)

    qspec = pl.BlockSpec((tq, HEAD_PAD), lambda h, i: (i, h))
    kspec = pl.BlockSpec((n_rows, HEAD_PAD), lambda h, i: (0, h))
    return pl.pallas_call(
        body, name="attn_bwd", grid=(N_HEADS, t_lat // tq),
        out_shape=[jax.ShapeDtypeStruct((t_lat, N_HEADS * HEAD_PAD), F32),
                   jax.ShapeDtypeStruct((n_rows, N_HEADS * HEAD_PAD), F32),
                   jax.ShapeDtypeStruct((n_rows, N_HEADS * HEAD_PAD), F32)],
        in_specs=[qspec, kspec, kspec, qspec, qspec, qspec], out_specs=[qspec, kspec, kspec],
        compiler_params=_cparams(("parallel", "arbitrary")),
    )(q, k, v, o, do, lse)


def adamw(name, w, g, m, v):
    r, ccols = w.shape
    if r % 8 == 0:
        tr, tcol = _best_div(r, 8, max(8, 262144 // ccols)), ccols
    else:
        tr, tcol = r, _pick(ccols, (256, 128))
    c1 = 1.0 - ADAM_B1 ** ADAM_STEP
    c2 = 1.0 - ADAM_B2 ** ADAM_STEP

    def body(w_ref, g_ref, m_ref, v_ref, d_ref, nm_ref, nv_ref):
        gv = g_ref[...]
        nm = ADAM_B1 * m_ref[...] + (1.0 - ADAM_B1) * gv
        nv = ADAM_B2 * v_ref[...] + (1.0 - ADAM_B2) * (gv * gv)
        d_ref[...] = -ADAM_LR * ((nm / c1) / (jnp.sqrt(nv / c2) + ADAM_EPS) + ADAM_WD * w_ref[...])
        nm_ref[...] = nm
        nv_ref[...] = nv

    spec = pl.BlockSpec((tr, tcol), lambda i, j: (i, j))
    return pl.pallas_call(
        body, name=name, grid=(r // tr, ccols // tcol),
        out_shape=[jax.ShapeDtypeStruct((r, ccols), F32)] * 3,
        in_specs=[spec] * 4, out_specs=[spec] * 3,
        compiler_params=_cparams(("parallel", "parallel")),
    )(w, g, m, v)


def adamw_many(name, ws, gs, ms, vs):
    n = len(ws)
    c1 = 1.0 - ADAM_B1 ** ADAM_STEP
    c2 = 1.0 - ADAM_B2 ** ADAM_STEP

    def body(*refs):
        for i in range(n):
            w_ref, g_ref, m_ref, v_ref = (refs[k * n + i] for k in range(4))
            d_ref, nm_ref, nv_ref = (refs[(4 + k) * n + i] for k in range(3))
            gv = g_ref[...]
            nm = ADAM_B1 * m_ref[...] + (1.0 - ADAM_B1) * gv
            nv = ADAM_B2 * v_ref[...] + (1.0 - ADAM_B2) * (gv * gv)
            d_ref[...] = -ADAM_LR * ((nm / c1) / (jnp.sqrt(nv / c2) + ADAM_EPS) + ADAM_WD * w_ref[...])
            nm_ref[...] = nm
            nv_ref[...] = nv

    vmem = pl.BlockSpec(memory_space=pltpu.VMEM)
    res = pl.pallas_call(
        body, name=name,
        out_shape=[jax.ShapeDtypeStruct(w.shape, F32) for w in ws] * 3,
        in_specs=[vmem] * (4 * n), out_specs=[vmem] * (3 * n),
        compiler_params=_cparams(),
    )(*ws, *gs, *ms, *vs)
    return [tuple(res[k * n + i] for k in range(3)) for i in range(n)]


def _flat(parts, dtype, row_mult):
    v = jnp.concatenate([p.reshape(-1).astype(dtype) for p in parts])
    quantum = row_mult * FLAT_C
    total = -(-v.shape[0] // quantum) * quantum
    return jnp.pad(v, (0, total - v.shape[0])).reshape(total // FLAT_C, FLAT_C)


def _unflat(flat, shapes):
    v = flat.reshape(-1)
    out, at = [], 0
    for s in shapes:
        n = math.prod(s)
        out.append(v[at:at + n].reshape(s))
        at += n
    return out


def _gathered_to_full(name, g):
    k = g.shape[1]
    return jnp.transpose(g, (1, 0, 2)).reshape(k, N_DEV * g.shape[2])


def _full_to_chunks(name, full):
    k, n = full.shape
    return jnp.transpose(full.reshape(k, N_DEV, n // N_DEV), (1, 0, 2)).reshape(N_DEV, -1)


def _shard_to_rb(name, w):
    return w if name in ROW_SHARDED else w.T


def _rb_to_shard(name, g):
    return g if name in ROW_SHARDED else g.T


def _rb_from_gathered(name, g):
    cols = g.shape[2]
    if name == 'w_in':
        z = lambda k: jnp.zeros((k, cols), g.dtype)
        full = g.reshape(N_DEV * g.shape[1], cols)
        return jnp.concatenate([full[:Z_KR], z(QK_NOPE), full[Z_KR:Z_KR + QK_ROPE], z(HEAD_PAD - QK_DIM),
                                full[Z_KR + QK_ROPE:]], axis=0)
    if name == 'w_uq':
        return jnp.pad(g, ((0, 0), (0, HEAD_PAD - QK_DIM), (0, 0))).reshape(N_HEADS * HEAD_PAD, cols)
    if name == 'w_ukv':
        pad = lambda t: jnp.pad(t, ((0, 0), (0, HEAD_PAD - t.shape[1]), (0, 0))).reshape(N_HEADS * HEAD_PAD, cols)
        return jnp.concatenate([pad(g[:, :QK_NOPE]), pad(g[:, QK_NOPE:])], axis=0)
    if name == 'w_o_attn':
        full = g.reshape(D, N_HEADS, V_HEAD)
        return jnp.pad(full, ((0, 0), (0, 0), (0, HEAD_PAD - V_HEAD))).reshape(D, N_HEADS * HEAD_PAD)
    return g.reshape(N_DEV * g.shape[1], cols)


def _chunks_from_rb_grad(name, g):
    cols = g.shape[1]
    if name == 'w_in':
        full = jnp.concatenate([g[:Z_KR], g[Z_KR + QK_NOPE:Z_KR + QK_DIM], g[Z_XB:]], axis=0)
        return full.reshape(N_DEV, -1, cols)
    if name == 'w_uq':
        return g.reshape(N_HEADS, HEAD_PAD, cols)[:, :QK_DIM]
    if name == 'w_ukv':
        half = N_HEADS * HEAD_PAD
        gk = g[:half].reshape(N_HEADS, HEAD_PAD, cols)[:, :QK_NOPE]
        gv = g[half:].reshape(N_HEADS, HEAD_PAD, cols)[:, :V_HEAD]
        return jnp.concatenate([gk, gv], axis=1)
    if name == 'w_o_attn':
        full = g.reshape(D, N_HEADS, HEAD_PAD)[:, :, :V_HEAD].reshape(D, N_HEADS * V_HEAD)
        return full.reshape(N_DEV, D // N_DEV, N_HEADS * V_HEAD)
    return g.reshape(N_DEV, -1, cols)


def local_step(x, ctx, target, mod_l, mod_c, wt, on_grad=None, arrive=None):
    t_lat, n_ctx = x.shape[0], ctx.shape[0]
    n = t_lat + n_ctx
    tm = _pick(math.gcd(t_lat, n), (256, 128))
    tq_fwd = _pick(t_lat, (256, 128))
    tq_bwd = _pick(t_lat, (512, 256, 128))
    row = lambda v: v.reshape(1, -1).astype(F32)
    two = lambda a, b: jnp.stack([a, b]).astype(F32)
    sh1_l, sc1_l, g1_l, sh2_l, sc2_l, g2_l = jnp.split(mod_l, 6)
    sh1_c, sc1_c = jnp.split(mod_c, 6)[:2]
    sc1, sh1 = two(sc1_l, sc1_c), two(sh1_l, sh1_c)
    g1, g2, sc2, sh2 = row(g1_l), row(g2_l), row(sc2_l), row(sh2_l)
    norm1_g, norm2_g, final_g = row(wt['norm1_g']), row(wt['norm2_g']), row(wt['final_g'])
    q_g, kv_g, b_gate = row(wt['q_norm_g']), row(wt['kv_norm_g']), row(wt['b_gate'])
    wt = dict(wt)
    pending = []

    def sent():
        tokens = list(pending)
        pending.clear()
        return tokens

    def need(names, after):
        if arrive is not None:
            got = arrive(names, after)
            if '_token' in got:
                pending.append(got.pop('_token'))
            wt.update(got)
        return [wt[n] for n in names]
    lru_w_a = wt['lru_w_a'].reshape(2 * LRU_BLOCKS, LRU_BW, LRU_BW).astype(BF16)
    lru_w_x = wt['lru_w_x'].reshape(2 * LRU_BLOCKS, LRU_BW, LRU_BW).astype(BF16)
    b_a, b_x, lam = wt['lru_b_a'], wt['lru_b_x'], wt['lru_lambda']
    sp = jnp.logaddexp(-lam, 0.0)
    c_tab, s1_tab, s2_tab = _rope_tables(t_lat, n)
    rw = functools.partial(rowwise, n_rows=n, t_lat=t_lat, tm=tm)
    rw_lat = functools.partial(rowwise, n_rows=t_lat, t_lat=t_lat, tm=tm)

    xs = jnp.concatenate([x, ctx], axis=0)

    def f_norm1(is_ctx, rows, params):
        (xv,), (g, sc, sh) = rows, params
        return [_norm_mod(xv, g, _sel(is_ctx, sc), _sel(is_ctx, sh))], []

    (h,), _ = rw("norm1", f_norm1, [(xs, 0, D)], [norm1_g, sc1, sh1], [(D, BF16)], [])
    (w_in_t,) = need(('w_in',), h)
    z = matmul("w_in", h, w_in_t, 'nt', BF16, after=sent())
    w_uq_t, w_ukv_t, w_o_lru = need(('w_uq', 'w_ukv', 'w_o_lru'), z)

    def f_qkv_norm(is_ctx, rows, params):
        (ql, kvl), (gq, gkv) = rows, params
        return [_rms(ql, gq), _rms(kvl, gkv)], []

    (qn, kvn), _ = rw("qkv_norm", f_qkv_norm, [(z, Z_Q, Q_RANK), (z, Z_KV, KV_RANK)], [q_g, kv_g],
                      [(Q_RANK, BF16), (KV_RANK, BF16)], [])
    qp = matmul("w_uq", qn, w_uq_t, 'nt', BF16)
    kvp = matmul("w_ukv", kvn, w_ukv_t, 'nt', BF16)

    def f_rope(is_ctx, rows, params):
        qv, kk, vv, kr, c, s1, s2 = rows
        krr = _rope(kr, c, s1, s2)
        qo = jnp.concatenate([_rope(qh, c, s1, s2) for qh in _heads(qv)], axis=1) * Q_SCALE
        ko = jnp.concatenate([kh + krr for kh in _heads(kk)], axis=1)
        return [qo, ko, vv], []

    hp = N_HEADS * HEAD_PAD
    (qr, kr_, vr), _ = rw("rope", f_rope,
                          [(qp, 0, hp), (kvp, 0, hp), (kvp, hp, hp), (z, Z_KR, HEAD_PAD), (c_tab, 0, HEAD_PAD),
                           (s1_tab, 0, HEAD_PAD), (s2_tab, 0, HEAD_PAD)], [], [(hp, BF16)] * 3, [])
    attn, lse = attn_fwd(qr, kr_, vr, t_lat, n, tq_fwd)

    xc = conv_fwd("lru_conv", z, Z_XB, LRU_W, wt['lru_conv_w'], row(wt['lru_conv_b']), 2, n, t_lat, F32)
    a_f, u_f, a_b, u_b = gates_fwd(xc, lru_w_a, lru_w_x, b_a, b_x, sp, n, t_lat, tm)
    h_f, hp_f = scan_fwd("scan_f", a_f, u_f, 'f', n, t_lat)
    h_b, hp_b = scan_fwd("scan_b", a_b, u_b, 'b', n, t_lat)

    def f_lru_out(is_ctx, rows, params):
        hf, hb, yb = rows
        return [(hf + hb) * _gelu(yb)], []

    (ybin,), _ = rw_lat("lru_out", f_lru_out, [(h_f, 0, LRU_W), (h_b, 0, LRU_W), (z, Z_YB, LRU_W)], [],
                        [(LRU_W, BF16)], [])
    w_o_attn_t, w_out, w_up_t, w_down = need(('w_o_attn', 'w_out', 'w_up', 'w_down'), attn)
    y_a = matmul("w_o_attn", attn, w_o_attn_t, 'nt', BF16)
    y_b = matmul("w_o_lru", ybin, w_o_lru, 'nn', BF16)

    def _merge(ya, yb, gl, bg):
        gates = _sigmoid(gl + bg)
        return gates[:, :D] * ya + gates[:, D:] * yb

    def f_merge(is_ctx, rows, params):
        (ya, yb, gl), (bg,) = rows, params
        return [_merge(ya, yb, gl, bg)], []

    (mrg,), _ = rw_lat("merge", f_merge, [(y_a, 0, D), (y_b, 0, D), (z, Z_GL, 2 * D)], [b_gate], [(D, BF16)], [])
    o = matmul("w_out", mrg, w_out, 'nn', BF16)

    def _res_norm2(xv, ov, g1v, g, sc, sh):
        x1 = xv + g1v * ov
        return x1, _norm_mod(x1, g, sc, sh)

    def f_norm2(is_ctx, rows, params):
        (xv, ov), (g1v, g, sc, sh) = rows, params
        x1, h2v = _res_norm2(xv, ov, g1v, g, sc, sh)
        return [x1, h2v], []

    (x1, h2), _ = rw_lat("norm2", f_norm2, [(x, 0, D), (o, 0, D)], [g1, norm2_g, sc2, sh2], [(D, F32), (D, BF16)], [])
    u = matmul("w_up", h2, w_up_t, 'nt', BF16)
    f = ffn_mix_fwd(u, wt['ffn_conv_w'], row(wt['ffn_conv_b']), t_lat)
    dn = matmul("w_down", f, w_down, 'nn', BF16)

    def _tile_loss(x1v, dv, g2v, fg, tgt):
        y = _rms(x1v + g2v * dv, fg)
        e = y - tgt
        return 0.5 * jnp.sum(jnp.mean(e * e, axis=-1, keepdims=True), axis=0, keepdims=True)

    def f_final(is_ctx, rows, params):
        (x1v, dv, tgt), (g2v, fg) = rows, params
        lv, vjp = jax.vjp(lambda a, b, c, d: _tile_loss(a, b, c, d, tgt), x1v, dv, g2v, fg)
        dx2, dd, dg2, dfg = vjp(jnp.ones((1, 1), F32))
        return [dx2, dd], [dg2, dfg, jnp.broadcast_to(lv, (1, 128))]

    (dx2, dd), (dg2, dfinal_g, loss_v) = rw_lat("final", f_final, [(x1, 0, D), (dn, 0, D), (target, 0, D)],
                                                [g2, final_g], [(D, F32), (D, BF16)], [(1, D), (1, D), (1, 128)])
    loss = loss_v[0, 0]

    grads = {'final_g': dfinal_g}

    def put(name, g):
        grads[name] = g
        if on_grad is not None:
            pending.append(on_grad(name, g))
    df = matmul("d_f", dd, w_down, 'nt', BF16)
    put('w_down', matmul("g_w_down", f, dd, 'tn', BF16))

    du, grads['ffn_conv_w'], grads['ffn_conv_b'] = ffn_mix_bwd(u, df, wt['ffn_conv_w'], row(wt['ffn_conv_b']),
                                                               t_lat)
    dh2 = matmul("d_h2", du, w_up_t, 'nn', BF16, after=sent())
    put('w_up', matmul("g_w_up", du, h2, 'tn', BF16))

    def b_norm2(is_ctx, rows, params):
        (xv, ov, dh2v, dx2v), (g1v, g, sc, sh) = rows, params
        _, vjp = jax.vjp(_res_norm2, xv, ov, g1v, g, sc, sh)
        dx, do, dg1v, dg, dsc, dsh = vjp((dx2v, dh2v))
        return [dx, do], [dg1v, dg, dsc, dsh]

    (dx_res, do), (dg1, dnorm2_g, dsc2, dsh2) = rw_lat(
        "norm2_bwd", b_norm2, [(x, 0, D), (o, 0, D), (dh2, 0, D), (dx2, 0, D)], [g1, norm2_g, sc2, sh2],
        [(D, F32), (D, BF16)], [(1, D)] * 4)
    grads['norm2_g'] = dnorm2_g
    dmrg = matmul("d_merge", do, w_out, 'nt', BF16, after=sent())
    put('w_out', matmul("g_w_out", mrg, do, 'tn', BF16))

    def b_merge(is_ctx, rows, params):
        (ya, yb, gl, dm), (bg,) = rows, params
        _, vjp = jax.vjp(_merge, ya, yb, gl, bg)
        dya, dyb, dgl, dbg = vjp(dm)
        return [dya, dyb, dgl], [dbg]

    (dy_a, dy_b, dgl), (grads['b_gate'],) = rw_lat(
        "merge_bwd", b_merge, [(y_a, 0, D), (y_b, 0, D), (z, Z_GL, 2 * D), (dmrg, 0, D)], [b_gate],
        [(D, BF16), (D, BF16), (2 * D, BF16)], [(1, 2 * D)])
    dattn = matmul("d_attn", dy_a, w_o_attn_t, 'nn', BF16, after=sent())
    put('w_o_attn', matmul("g_w_o_attn", dy_a, attn, 'tn', BF16))
    dybin = matmul("d_lru_out", dy_b, w_o_lru, 'nt', BF16, after=sent())
    put('w_o_lru', matmul("g_w_o_lru", ybin, dy_b, 'tn', BF16))

    def b_lru_out(is_ctx, rows, params):
        hf, hb, yb, dyv = rows
        _, vjp = jax.vjp(lambda s, y: s * _gelu(y), hf + hb, yb)
        dh, dyb = vjp(dyv)
        return [dh, dyb], []

    (dh_lru, dyb), _ = rw_lat("lru_out_bwd", b_lru_out,
                              [(h_f, 0, LRU_W), (h_b, 0, LRU_W), (z, Z_YB, LRU_W), (dybin, 0, LRU_W)], [],
                              [(LRU_W, F32), (LRU_W, BF16)], [])
    du_f, da_f = scan_adj("scan_f_adj", a_f, dh_lru, hp_f, 'f', n, t_lat)
    du_b, da_b = scan_adj("scan_b_adj", a_b, dh_lru, hp_b, 'b', n, t_lat)
    dxc, (dw_a, dw_x, db_a, db_x, dsp) = gates_bwd(xc, da_f, du_f, da_b, du_b, lru_w_a, lru_w_x, b_a, b_x, sp,
                                                   n, t_lat, tm)
    put('lru_w_a', dw_a.reshape(2 * LRU_BLOCKS * LRU_BW, LRU_BW).astype(BF16))
    put('lru_w_x', dw_x.reshape(2 * LRU_BLOCKS * LRU_BW, LRU_BW).astype(BF16))
    grads['lru_b_a'], grads['lru_b_x'] = db_a, db_x
    grads['lru_lambda'] = -dsp * _sigmoid(-lam)
    dxb, grads['lru_conv_w'], grads['lru_conv_b'] = conv_bwd("lru_conv_bwd", dxc, z, Z_XB, LRU_W, wt['lru_conv_w'],
                                                             2, n, t_lat)

    dq, dk, dv = attn_bwd(qr, kr_, vr, attn, dattn, lse, t_lat, n, tq_bwd)

    def b_rope(is_ctx, rows, params):
        dqv, dkv, dvv, c, s1, s2 = rows
        live = jnp.where(is_ctx, 0.0, 1.0)
        dqo = jnp.concatenate([_rope_t(dqh, c, s1, s2) for dqh in _heads(dqv)], axis=1) * live
        dkh = _heads(dkv)
        dkr = dkh[0]
        for t in dkh[1:]:
            dkr = dkr + t
        lanes = lax.broadcasted_iota(jnp.int32, dkr.shape, 1)
        dkr = jnp.where((lanes >= QK_NOPE) & (lanes < QK_DIM), _rope_t(dkr, c, s1, s2), 0.0)
        return [dqo, jnp.concatenate([dkv, dvv], axis=1), dkr], []

    (dqp, dkvp, dkr), _ = rw("rope_bwd", b_rope,
                             [(dq, 0, hp), (dk, 0, hp), (dv, 0, hp), (c_tab, 0, HEAD_PAD), (s1_tab, 0, HEAD_PAD),
                              (s2_tab, 0, HEAD_PAD)], [], [(hp, BF16), (2 * hp, BF16), (HEAD_PAD, BF16)], [])
    dqn = matmul("d_qn", dqp, w_uq_t, 'nn', BF16, after=sent())
    put('w_uq', matmul("g_w_uq", dqp, qn, 'tn', BF16))
    dkvn = matmul("d_kvn", dkvp, w_ukv_t, 'nn', BF16, after=sent())
    put('w_ukv', matmul("g_w_ukv", dkvp, kvn, 'tn', BF16))

    def b_qkv_norm(is_ctx, rows, params):
        (ql, kvl, dqv, dkvv), (gq, gkv) = rows, params
        _, vjp_q = jax.vjp(_rms, ql, gq)
        _, vjp_kv = jax.vjp(_rms, kvl, gkv)
        dql, dgq = vjp_q(dqv)
        dkvl, dgkv = vjp_kv(dkvv)
        return [dql, dkvl], [dgq, dgkv]

    (dq_lat, dkv_lat), (grads['q_norm_g'], grads['kv_norm_g']) = rw(
        "qkv_norm_bwd", b_qkv_norm, [(z, Z_Q, Q_RANK), (z, Z_KV, KV_RANK), (dqn, 0, Q_RANK), (dkvn, 0, KV_RANK)],
        [q_g, kv_g], [(Q_RANK, BF16), (KV_RANK, BF16)], [(1, Q_RANK), (1, KV_RANK)])
    pad_ctx = lambda t: jnp.pad(t, ((0, n_ctx), (0, 0)))
    dz = jnp.concatenate([dq_lat, dkv_lat, dkr, dxb, pad_ctx(dyb), pad_ctx(dgl)], axis=1)
    put('w_in', matmul("g_w_in", dz, h, 'tn', BF16))
    dh = matmul("d_h", dz, w_in_t, 'nn', BF16, after=sent())

    def b_norm1(is_ctx, rows, params):
        (xv, dhv, dxr), (g, sc, sh) = rows, params
        scv, shv = _sel(is_ctx, sc), _sel(is_ctx, sh)
        _, vjp = jax.vjp(_norm_mod, xv, g, scv, shv)
        dx, dg, dsc, dsh = vjp(dhv)
        return [dx + dxr], [dg, _seg_acc(is_ctx, dsc), _seg_acc(is_ctx, dsh)]

    (dxs,), (grads['norm1_g'], dsc1, dsh1) = rw("norm1_bwd", b_norm1, [(xs, 0, D), (dh, 0, D), (dx_res, 0, D)],
                                                [norm1_g, sc1, sh1], [(D, F32)], [(1, D), (2, D), (2, D)])
    grad_x = dxs[:t_lat]
    zero = jnp.zeros((D,), F32)
    dmod_l = jnp.concatenate([dsh1[0], dsc1[0], dg1[0], dsh2[0], dsc2[0], dg2[0]])
    dmod_c = jnp.concatenate([dsh1[1], dsc1[1], zero, zero, zero, zero])
    return loss, grad_x, grads, dmod_l, dmod_c


def kernel(x, c, ctx, c_ctx, w_mod, b_mod, norm1_g, w_in, b_gate, q_norm_g, kv_norm_g, w_uq, w_ukv, w_o_attn, lru_conv_w, lru_conv_b, lru_w_a, lru_b_a, lru_w_x, lru_b_x, lru_lambda, w_o_lru, w_out, norm2_g, w_up, ffn_conv_w, ffn_conv_b, w_down, final_g, loss_target, m_c_ctx, m_w_mod, m_b_mod, m_norm1_g, m_w_in, m_b_gate, m_q_norm_g, m_kv_norm_g, m_w_uq, m_w_ukv, m_w_o_attn, m_lru_conv_w, m_lru_conv_b, m_lru_w_a, m_lru_b_a, m_lru_w_x, m_lru_b_x, m_lru_lambda, m_w_o_lru, m_w_out, m_norm2_g, m_w_up, m_ffn_conv_w, m_ffn_conv_b, m_w_down, m_final_g, v_c_ctx, v_w_mod, v_b_mod, v_norm1_g, v_w_in, v_b_gate, v_q_norm_g, v_kv_norm_g, v_w_uq, v_w_ukv, v_w_o_attn, v_lru_conv_w, v_lru_conv_b, v_lru_w_a, v_lru_b_a, v_lru_w_x, v_lru_b_x, v_lru_lambda, v_w_o_lru, v_w_out, v_norm2_g, v_w_up, v_ffn_conv_w, v_ffn_conv_b, v_w_down, v_final_g):
    given = dict(locals())
    strip = lambda name, a: a if name in ('c_ctx', 'final_g') else a[0]
    wsh = {n: strip(n, given[n]) for n in WEIGHTS}
    msh = {n: strip(n, given['m_' + n]) for n in WEIGHTS}
    vsh = {n: strip(n, given['v_' + n]) for n in WEIGHTS}
    me = _my_index()

    small = _flat([c[0]] + [wsh[n] for n in SMALL_F32], F32, 8)
    small_all = all_gather("gather_small", small).reshape(N_DEV, -1)
    c_all = small_all[:, :D]
    full, at = {}, D
    for n in SMALL_F32:
        cnt = math.prod(wsh[n].shape)
        full[n] = _gathered_to_full(n, small_all[:, at:at + cnt].reshape((N_DEV,) + wsh[n].shape))
        at += cnt

    cond = jnp.concatenate([c_all, c_ctx[None], jnp.zeros((7, D), F32)], axis=0)
    sil = cond * jax.nn.sigmoid(cond)
    mod_cols = matmul("mod_proj", sil, wsh['w_mod'], 'nn', F32)
    mod_all = all_gather("gather_mod", mod_cols)
    mod_all = jnp.transpose(mod_all, (1, 0, 2)).reshape(16, 6 * D) + b_mod[0][None]
    mod_l = lax.dynamic_index_in_dim(mod_all, me, axis=0, keepdims=False)
    mod_c = mod_all[N_DEV]

    rb_shards = {n: _shard_to_rb(n, wsh[n]).astype(BF16) for n in BIG_BF16}
    (w_in_blocks,) = all_gather_multi("gather_w_in", [rb_shards['w_in']])
    later = [n for n in BIG_BF16 if n != 'w_in']
    weights_started, weights_sent = exchange_start("weights_send", 'gather', [rb_shards[n] for n in later],
                                                   after=[w_in_blocks, mod_all])
    for n in REPLICATED:
        if n not in ('c_ctx', 'b_mod'):
            full[n] = wsh[n]

    def arrive(names, after):
        if names == ('w_in',):
            return {'w_in': _rb_from_gathered('w_in', w_in_blocks), '_token': weights_sent}
        picked = [later.index(n) for n in names]
        lands = exchange_wait("weights_wait_" + names[0], 'gather',
                              tuple([part[i] for i in picked] for part in weights_started), after)
        return {n: _rb_from_gathered(n, lax.dynamic_update_slice_in_dim(land, rb_shards[n][None], me, axis=0))
                for n, land in zip(names, lands)}

    in_flight = {}

    def on_grad(n, g):
        chunks = _chunks_from_rb_grad(n, g)
        own = lax.dynamic_index_in_dim(chunks, me, axis=0, keepdims=True)
        started, token = exchange_start("grad_send_" + n, 'scatter', [chunks])
        in_flight[n] = (own, started)
        return token

    loss, grad_x, grads, dmod_l, dmod_c = local_step(x[0], ctx[0], loss_target[0], mod_l, mod_c, full, on_grad,
                                                     arrive)
    dmod = jnp.stack([dmod_l, dmod_c]).reshape(2 * 6 * D // FLAT_C, FLAT_C)
    dm = all_gather("gather_dmod", dmod).reshape(N_DEV, 2, 6 * D)
    dmod_c_tot = dm[0, 1]
    for p in range(1, N_DEV):
        dmod_c_tot = dmod_c_tot + dm[p, 1]
    dm16 = jnp.concatenate([dm[:, 0], dmod_c_tot[None], jnp.zeros((7, 6 * D), F32)], axis=0)
    ncol = 6 * D // N_DEV
    dm16_cols = lax.dynamic_slice_in_dim(dm16.reshape(16, N_DEV, ncol), me, 1, axis=1)[:, 0]
    grad_w_mod = matmul("g_w_mod", sil, dm16_cols, 'tn', F32)
    dsil = matmul("d_cond", dm16_cols, wsh['w_mod'], 'nt', F32)
    sg = jax.nn.sigmoid(c_ctx)
    grads['c_ctx'] = dsil[N_DEV] * (sg * (1.0 + c_ctx * (1.0 - sg)))
    grads['b_mod'] = dmod_l + dmod_c

    g_final = {'w_mod': grad_w_mod}
    reduced = {}
    for n in BIG_BF16 + ['lru_w_a', 'lru_w_x']:
        own, started = in_flight[n]
        (land,) = exchange_wait("grad_wait_" + n, 'scatter', started, dm)
        reduced[n] = sum_slots("sum_" + n, lax.dynamic_update_slice_in_dim(land, own, me, axis=0))
    for n in BIG_BF16:
        g_final[n] = _rb_to_shard(n, reduced[n])

    small_names = SMALL_F32 + [n for n in REPLICATED if n not in ('lru_w_a', 'lru_w_x')]
    partials = _flat([grads[n] for n in small_names] + [loss], F32, 8)
    parts_all, a_all, x_all = all_gather_multi("gather_small_grads", [partials, reduced['lru_w_a'], reduced['lru_w_x']])
    small_sum = sum_slots("sum_small", parts_all).reshape(-1)
    g_final['lru_w_a'], g_final['lru_w_x'] = a_all.reshape(wsh['lru_w_a'].shape), x_all.reshape(wsh['lru_w_x'].shape)
    at = 0
    for n in small_names:
        cnt = math.prod(full[n].shape) if n in SMALL_F32 else math.prod(wsh[n].shape)
        g = small_sum[at:at + cnt]
        if n in SMALL_F32:
            k = full[n].shape[0]
            g = lax.dynamic_index_in_dim(g.reshape(k, N_DEV, -1), me, axis=1, keepdims=False)
        g_final[n] = g.reshape(wsh[n].shape)
        at += cnt
    loss = small_sum[at]

    stepped = {}
    for n in ['w_mod'] + BIG_BF16:
        if n in COL_SHARDED and wsh[n].shape[1] % 128:
            outs = adamw("adamw_" + n, wsh[n].T, reduced[n], msh[n].T, vsh[n].T)
            stepped[n] = tuple(o.T for o in outs)
        else:
            stepped[n] = adamw("adamw_" + n, wsh[n], g_final[n], msh[n], vsh[n])
    rest = [n for n in WEIGHTS if n not in stepped]
    as2d = lambda a: a.reshape(-1, a.shape[-1])
    rest_out = adamw_many("adamw_small", *[[as2d(d[n]) for n in rest] for d in (wsh, g_final, msh, vsh)])
    stepped.update(zip(rest, rest_out))
    shaped = lambda n, a: a.reshape(given[n].shape)
    return (loss, grad_x[None],
            *[shaped(n, g_final[n]) for n in WEIGHTS],
            *[shaped(n, stepped[n][k]) for k in range(3) for n in WEIGHTS])
```

```python
import functools
import math

import jax
import jax.numpy as jnp
from jax import lax
from jax.experimental import pallas as pl
from jax.experimental.pallas import tpu as pltpu

F32 = jnp.float32
BF16 = jnp.bfloat16
MESH = pl.DeviceIdType.MESH

N_DEV = 8
D = 1024
N_HEADS = 8
HEAD_PAD = 128
QK_NOPE, QK_ROPE, V_HEAD = 64, 32, 64
QK_DIM = QK_NOPE + QK_ROPE
Q_RANK, KV_RANK = 384, 256
LRU_W, LRU_BLOCKS, LRU_BW = 1280, 10, 128
FFN = 2816
GRID_W = 64
ROPE_BASE = 10000.0
LRU_C = 8.0
EPS = 1e-6
Z_Q, Z_KV, Z_KR, Z_XB, Z_YB, Z_GL, Z_END = 0, 384, 640, 768, 2048, 3328, 5376
ADAM_LR, ADAM_B1, ADAM_B2, ADAM_EPS, ADAM_WD, ADAM_STEP = 0.001, 0.9, 0.999, 1e-08, 0.01, 10

VMEM_LIMIT = 52 * 1024 * 1024
FLAT_C = 512
BIG_ROWS = 256

WEIGHTS = ['c_ctx', 'w_mod', 'b_mod', 'norm1_g', 'w_in', 'b_gate', 'q_norm_g', 'kv_norm_g', 'w_uq', 'w_ukv',
           'w_o_attn', 'lru_conv_w', 'lru_conv_b', 'lru_w_a', 'lru_b_a', 'lru_w_x', 'lru_b_x', 'lru_lambda',
           'w_o_lru', 'w_out', 'norm2_g', 'w_up', 'ffn_conv_w', 'ffn_conv_b', 'w_down', 'final_g']
COL_SHARDED = ['w_in', 'w_uq', 'w_ukv', 'w_o_attn', 'lru_conv_w', 'lru_b_a', 'lru_b_x', 'lru_lambda', 'w_up',
               'ffn_conv_w']
ROW_SHARDED = ['w_o_lru', 'w_out', 'w_down']
BIG_BF16 = ['w_in', 'w_uq', 'w_ukv', 'w_o_attn', 'w_o_lru', 'w_out', 'w_up', 'w_down']
SMALL_F32 = ['lru_conv_w', 'lru_b_a', 'lru_b_x', 'lru_lambda', 'ffn_conv_w']
SHARDED = BIG_BF16 + SMALL_F32
REPLICATED = ['c_ctx', 'b_mod', 'norm1_g', 'b_gate', 'q_norm_g', 'kv_norm_g', 'lru_conv_b', 'lru_w_a', 'lru_w_x',
              'norm2_g', 'ffn_conv_b', 'final_g']


def _cparams(sem=None):
    return pltpu.CompilerParams(dimension_semantics=sem, vmem_limit_bytes=VMEM_LIMIT)


def _pick(n, cands):
    for c in cands:
        if c <= n and n % c == 0:
            return c
    return n


def _best_div(n, mult, cap):
    best = mult
    for d in range(mult, min(n, cap) + 1, mult):
        if n % d == 0:
            best = d
    return best


MXU_DIM = 256
ROW_TILES = (1088, 1024, 544, 512, 256, 128, 64, 32, 16, 8)
LANE_TILES = (2816, 1792, 1536, 1280, 1024, 768, 512, 256, 1408, 896, 640, 384, 128)
DEPTH_ROW_TILES = (2176, 2048, 1024, 512, 256, 1088, 128, 64, 32, 16, 8)
MATMUL_VMEM_BUDGET = 40 * 1024 * 1024
MXU_FILL_OK = 0.9


def _my_pos():
    return lax.axis_index("x"), lax.axis_index("y"), lax.axis_index("c")


def _my_index():
    x, y, c = _my_pos()
    return 4 * x + 2 * y + c


def all_gather_multi(name, shards):
    n_arr = len(shards)
    arrays = range(n_arr)

    def body(*refs):
        x_refs, out_refs = refs[:n_arr], refs[n_arr:2 * n_arr]
        send_sems, recv_sems, local_sems = refs[2 * n_arr:]
        x, y, c = _my_pos()
        me, sibling = (x, y, c), (x, y, 1 - c)
        chips = [(1 - x, y), (x, 1 - y), (1 - x, 1 - y)]

        def slot(a, px, py, pc):
            return out_refs[a].at[4 * px + 2 * py + pc]

        def copy(a, k, block, to, src=None):
            return pltpu.make_async_remote_copy(
                src_ref=slot(a, *block) if src is None else src, dst_ref=slot(a, *block),
                send_sem=send_sems.at[7 * a + k], recv_sem=recv_sems.at[7 * a + k], device_id=to,
                device_id_type=MESH)

        mine = [pltpu.make_async_copy(x_refs[a], slot(a, *me), local_sems.at[a]) for a in arrays]
        first = [copy(a, 1 + j, me, (*chip, c), src=x_refs[a]) for j, chip in enumerate(chips) for a in arrays]
        first += [copy(a, 0, me, sibling, src=x_refs[a]) for a in arrays]
        for cp in first + mine:
            cp.start()
        passed = []
        for j, chip in enumerate(chips):
            for a in arrays:
                copy(a, 1 + j, (*chip, c), me).wait_recv()
                passed.append(copy(a, 4 + j, (*chip, c), sibling))
                passed[-1].start()
        for a in arrays:
            copy(a, 0, sibling, me).wait_recv()
            for j, chip in enumerate(chips):
                copy(a, 4 + j, (*chip, 1 - c), me).wait_recv()
        for cp in first + passed:
            cp.wait_send()
        for cp in mine:
            cp.wait()

    hbm = pl.BlockSpec(memory_space=pl.ANY)
    return pl.pallas_call(
        body, name=name,
        out_shape=[jax.ShapeDtypeStruct((N_DEV,) + s.shape, s.dtype) for s in shards],
        in_specs=[hbm] * n_arr, out_specs=[hbm] * n_arr,
        scratch_shapes=[pltpu.SemaphoreType.DMA((7 * n_arr,)), pltpu.SemaphoreType.DMA((7 * n_arr,)),
                        pltpu.SemaphoreType.DMA((n_arr,))],
    )(*shards)


def all_gather(name, shard):
    return all_gather_multi(name, [shard])[0]


def all_to_all_multi(name, chunk_arrays):
    n_arr = len(chunk_arrays)
    arrays = range(n_arr)

    def body(*refs):
        x_refs, out_refs = refs[:n_arr], refs[n_arr:2 * n_arr]
        send_sems, recv_sems, local_sems = refs[2 * n_arr:]
        x, y, c = _my_pos()
        me = 4 * x + 2 * y + c
        mine = [pltpu.make_async_copy(x_refs[a].at[me], out_refs[a].at[me], local_sems.at[a]) for a in arrays]
        sends, arrivals = [], []
        for rel in (6, 4, 2, 7, 5, 3, 1):
            dx, dy, dc = (rel >> 2) & 1, (rel >> 1) & 1, rel & 1
            px, py, pc = x ^ dx, y ^ dy, c ^ dc
            peer = 4 * px + 2 * py + pc
            for a in arrays:
                k = 7 * a + rel - 1
                sends.append(pltpu.make_async_remote_copy(
                    src_ref=x_refs[a].at[peer], dst_ref=out_refs[a].at[me],
                    send_sem=send_sems.at[k], recv_sem=recv_sems.at[k],
                    device_id=(px, py, pc), device_id_type=MESH))
                arrivals.append(pltpu.make_async_remote_copy(
                    src_ref=x_refs[a].at[peer], dst_ref=out_refs[a].at[peer],
                    send_sem=send_sems.at[k], recv_sem=recv_sems.at[k],
                    device_id=(x, y, c), device_id_type=MESH))
        for cp in sends + mine:
            cp.start()
        for cp in arrivals:
            cp.wait_recv()
        for cp in sends:
            cp.wait_send()
        for cp in mine:
            cp.wait()

    hbm = pl.BlockSpec(memory_space=pl.ANY)
    return pl.pallas_call(
        body, name=name,
        out_shape=[jax.ShapeDtypeStruct(s.shape, s.dtype) for s in chunk_arrays],
        in_specs=[hbm] * n_arr, out_specs=[hbm] * n_arr,
        scratch_shapes=[pltpu.SemaphoreType.DMA((7 * n_arr,)), pltpu.SemaphoreType.DMA((7 * n_arr,)),
                        pltpu.SemaphoreType.DMA((n_arr,))],
    )(*chunk_arrays)


def _peers():
    x, y, c = _my_pos()
    out = []
    for rel in (6, 4, 2, 7, 5, 3, 1):
        px, py, pc = x ^ ((rel >> 2) & 1), y ^ ((rel >> 1) & 1), c ^ (rel & 1)
        out.append((rel - 1, (px, py, pc), 4 * px + 2 * py + pc))
    return out


def _exchange_copies(mode, src_refs, land_refs, send_sems, recv_sems):
    x, y, c = _my_pos()
    me = 4 * x + 2 * y + c
    sends, arrivals = [], []
    for k, peer_pos, peer in _peers():
        for a, (src, land) in enumerate(zip(src_refs, land_refs)):
            piece = src.at[peer] if mode == 'scatter' else src
            sems = dict(send_sem=send_sems[a].at[k], recv_sem=recv_sems[a].at[k], device_id_type=MESH)
            sends.append(pltpu.make_async_remote_copy(src_ref=piece, dst_ref=land.at[me], device_id=peer_pos, **sems))
            arrivals.append(pltpu.make_async_remote_copy(src_ref=piece, dst_ref=land.at[peer], device_id=(x, y, c), **sems))
    return sends, arrivals


_HBM = pl.BlockSpec(memory_space=pltpu.HBM)
_SEM = pl.BlockSpec(memory_space=pltpu.SEMAPHORE)


def exchange_start(name, mode, arrays, after=()):
    n_arr, n_after = len(arrays), len(after)
    land_shapes = [a.shape if mode == 'scatter' else (N_DEV,) + a.shape for a in arrays]

    def body(*refs):
        src_refs, land_refs = refs[:n_arr], refs[n_arr:2 * n_arr]
        refs = refs[n_after:]
        send_sems, recv_sems = refs[2 * n_arr:3 * n_arr], refs[3 * n_arr:4 * n_arr]
        sends, _ = _exchange_copies(mode, src_refs, land_refs, send_sems, recv_sems)
        for cp in sends:
            cp.start()
        token = refs[-1]
        token[...] = jnp.zeros_like(token)

    sem = pltpu.SemaphoreType.DMA((N_DEV - 1,))
    res = pl.pallas_call(
        body, name=name,
        out_shape=[sem] * (2 * n_arr) + [pltpu.HBM(a.shape, a.dtype) for a in arrays]
        + [pltpu.HBM(s, a.dtype) for s, a in zip(land_shapes, arrays)] + [jax.ShapeDtypeStruct((8, 128), F32)],
        in_specs=[_HBM] * (2 * n_arr) + [pl.BlockSpec(memory_space=pl.ANY)] * n_after,
        out_specs=[_SEM] * (2 * n_arr) + [_HBM] * (2 * n_arr) + [pl.BlockSpec(memory_space=pltpu.VMEM)],
        input_output_aliases={i: 2 * n_arr + i for i in range(2 * n_arr)},
        compiler_params=pltpu.CompilerParams(has_side_effects=pltpu.SideEffectType.DATAFLOW_SIDE_EFFECTING),
    )(*[pltpu.with_memory_space_constraint(a, pltpu.HBM) for a in arrays],
      *[pltpu.with_memory_space_constraint(lax.empty(s, a.dtype), pltpu.HBM) for s, a in zip(land_shapes, arrays)],
      *after)
    return (res[:n_arr], res[n_arr:2 * n_arr], res[2 * n_arr:3 * n_arr], res[3 * n_arr:4 * n_arr]), res[-1]


def exchange_wait(name, mode, started, after):
    send_sems, recv_sems, thru, land = started
    n_arr = len(thru)

    def body(*refs):
        src_refs, land_refs = refs[:n_arr], refs[n_arr:2 * n_arr]
        s_sems, r_sems = refs[2 * n_arr:3 * n_arr], refs[3 * n_arr:4 * n_arr]
        sends, arrivals = _exchange_copies(mode, src_refs, land_refs, s_sems, r_sems)
        for cp in sends:
            cp.wait_send()
        for cp in arrivals:
            cp.wait_recv()

    res = pl.pallas_call(
        body, name=name,
        out_shape=[pltpu.HBM(a.shape, a.dtype) for a in thru] + [pltpu.HBM(a.shape, a.dtype) for a in land],
        in_specs=[_HBM] * (2 * n_arr) + [_SEM] * (2 * n_arr) + [pl.BlockSpec(memory_space=pl.ANY)],
        out_specs=[_HBM] * (2 * n_arr),
        input_output_aliases={i: i for i in range(2 * n_arr)},
        compiler_params=pltpu.CompilerParams(has_side_effects=pltpu.SideEffectType.DATAFLOW_SIDE_EFFECTING),
    )(*thru, *land, *send_sems, *recv_sems, after)
    return res[n_arr:]


def _sum_with_own(slot_ref, own_ref):
    x, y, c = _my_pos()
    me = 4 * x + 2 * y + c
    acc = None
    for p in range(N_DEV):
        v = jnp.where(me == p, own_ref[0], slot_ref[p]).astype(F32)
        acc = v if acc is None else acc + v
    return acc


def reduce_slots(name, slots, own, step=None):
    _, r, ccols = slots.shape
    tc = _pick(ccols, (256, 128))
    c1 = 1.0 - ADAM_B1 ** ADAM_STEP
    c2 = 1.0 - ADAM_B2 ** ADAM_STEP

    def body(s_ref, own_ref, *refs):
        g = _sum_with_own(s_ref, own_ref)
        if step is None:
            refs[0][...] = g
            return
        w_ref, m_ref, v_ref, g_ref, d_ref, nm_ref, nv_ref = refs
        nm = ADAM_B1 * m_ref[...] + (1.0 - ADAM_B1) * g
        nv = ADAM_B2 * v_ref[...] + (1.0 - ADAM_B2) * (g * g)
        g_ref[...] = g
        d_ref[...] = -ADAM_LR * ((nm / c1) / (jnp.sqrt(nv / c2) + ADAM_EPS) + ADAM_WD * w_ref[...])
        nm_ref[...] = nm
        nv_ref[...] = nv

    col = pl.BlockSpec((r, tc), lambda j: (0, j))
    n_out = 1 if step is None else 4
    res = pl.pallas_call(
        body, name=name, grid=(ccols // tc,),
        out_shape=[jax.ShapeDtypeStruct((r, ccols), F32)] * n_out,
        in_specs=[pl.BlockSpec((N_DEV, r, tc), lambda j: (0, 0, j)), pl.BlockSpec((1, r, tc), lambda j: (0, 0, j))]
        + [col] * (0 if step is None else 3),
        out_specs=[col] * n_out,
        compiler_params=_cparams(("parallel",)),
    )(slots, own, *(step or ()))
    return res[0] if step is None else res


def sum_slots(name, slots):
    _, r, ccols = slots.shape
    tc = _pick(ccols, (256, 128))

    def body(s_ref, o_ref):
        acc = s_ref[0].astype(F32)
        for p in range(1, N_DEV):
            acc = acc + s_ref[p].astype(F32)
        o_ref[...] = acc

    return pl.pallas_call(
        body, name=name, grid=(ccols // tc,),
        out_shape=jax.ShapeDtypeStruct((r, ccols), F32),
        in_specs=[pl.BlockSpec((N_DEV, r, tc), lambda j: (0, 0, j))],
        out_specs=pl.BlockSpec((r, tc), lambda j: (0, j)),
        compiler_params=_cparams(("parallel",)),
    )(slots)


def _mxu_fill(t):
    return t / (-(-t // MXU_DIM) * MXU_DIM)


def _matmul_tiles(mode, m_extent, n, k_extent, k_total, itemsizes):
    a_bytes, b_bytes, o_bytes = itemsizes
    m_cands = [c for c in (LANE_TILES if mode == 'tn' else ROW_TILES) if m_extent % c == 0] or [m_extent]
    k_cands = [c for c in (DEPTH_ROW_TILES if mode == 'tn' else LANE_TILES) if k_extent % c == 0] or [k_extent]
    n_cands = [c for c in LANE_TILES if n % c == 0] or [n]
    best = None
    for tm in m_cands:
        for tk in k_cands:
            for tn in n_cands:
                f32_tiles = 2 if k_total // tk > 1 else 1
                vmem = 2 * (tm * tk * a_bytes + tk * tn * b_bytes + tm * tn * o_bytes) + tm * tn * 4 * f32_tiles
                if vmem > MATMUL_VMEM_BUDGET:
                    continue
                key = (_mxu_fill(tn) * _mxu_fill(tk) >= MXU_FILL_OK, tm * tn * tk)
                if best is None or key > best[0]:
                    best = (key, (tm, tn, tk))
    assert best is not None, (mode, m_extent, n, k_extent)
    return best[1]


def matmul(name, a, b, mode, out_dtype, after=()):
    after = [t for t in after if t is not None]
    pieces, a_rows, a_cols = (1,) + a.shape if a.ndim == 2 else a.shape
    if mode == 'nn':
        (m, k), (k2, n) = (a_rows, pieces * a_cols), b.shape
    elif mode == 'nt':
        (m, k), (n, k2) = (a_rows, pieces * a_cols), b.shape
    else:
        (k, m), (k2, n) = (a_rows, pieces * a_cols), b.shape
    assert k == k2, (name, a.shape, b.shape, mode)
    tm, tn, tk = _matmul_tiles(mode, a_cols if mode == 'tn' else m, n, k if mode == 'tn' else a_cols, k,
                               (a.dtype.itemsize, b.dtype.itemsize, jnp.dtype(out_dtype).itemsize))
    nk = k // tk
    per_piece = a_cols // (tm if mode == 'tn' else tk)
    if a.ndim == 2:
        a_block = lambda rows, cols, at: pl.BlockSpec((rows, cols), at)
    else:
        a_block = lambda rows, cols, at: pl.BlockSpec(
            (None, rows, cols), lambda i, j, kk: (at(i, j, kk)[1] // per_piece, at(i, j, kk)[0],
                                                  at(i, j, kk)[1] % per_piece))
    if mode == 'nn':
        a_spec = a_block(tm, tk, lambda i, j, kk: (i, kk))
        b_spec = pl.BlockSpec((tk, tn), lambda i, j, kk: (kk, j))
        dn = (((1,), (0,)), ((), ()))
    elif mode == 'nt':
        a_spec = a_block(tm, tk, lambda i, j, kk: (i, kk))
        b_spec = pl.BlockSpec((tn, tk), lambda i, j, kk: (j, kk))
        dn = (((1,), (1,)), ((), ()))
    else:
        a_spec = a_block(tk, tm, lambda i, j, kk: (kk, i))
        b_spec = pl.BlockSpec((tk, tn), lambda i, j, kk: (kk, j))
        dn = (((0,), (0,)), ((), ()))

    def product(a_ref, b_ref):
        return lax.dot_general(a_ref[...].astype(BF16), b_ref[...].astype(BF16), dn, preferred_element_type=F32)

    n_after = len(after)

    def body_one(a_ref, b_ref, *rest):
        o_ref = rest[n_after]
        o_ref[...] = product(a_ref, b_ref).astype(o_ref.dtype)

    def body(a_ref, b_ref, *rest):
        o_ref, acc_ref = rest[n_after:]
        kk = pl.program_id(2)

        @pl.when(kk == 0)
        def _():
            acc_ref[...] = jnp.zeros_like(acc_ref)

        acc_ref[...] += product(a_ref, b_ref)

        @pl.when(kk == nk - 1)
        def _():
            o_ref[...] = acc_ref[...].astype(o_ref.dtype)

    return pl.pallas_call(
        body_one if nk == 1 else body, name=name, grid=(m // tm, n // tn, nk),
        out_shape=jax.ShapeDtypeStruct((m, n), out_dtype),
        in_specs=[a_spec, b_spec] + [pl.BlockSpec(memory_space=pl.ANY)] * n_after,
        out_specs=pl.BlockSpec((tm, tn), lambda i, j, kk: (i, j)),
        scratch_shapes=[] if nk == 1 else [pltpu.VMEM((tm, tn), F32)],
        compiler_params=_cparams(("parallel", "parallel", "arbitrary")),
    )(a, b, *after)


def rowwise(name, fn, rows, params, out_rows, out_accs, n_rows, t_lat, tm):
    nb, nbl = n_rows // tm, t_lat // tm
    in_specs, piece_counts = [], []
    operands = []
    for arr, off, width, *kind in rows:
        g = math.gcd(off, width) if off else width
        assert g % 128 == 0 or (off == 0 and width == arr.shape[1]), (name, off, width)
        cnt = width // g
        last = arr.shape[0] // tm - 1
        clamp = arr.shape[0] < n_rows
        for p in range(cnt):
            cb = off // g + p
            if kind == ['ctx']:
                in_specs.append(pl.BlockSpec(
                    (tm, g), lambda i, cb=cb, last=last: (jnp.clip(i - nbl, 0, last), cb)))
            elif clamp:
                in_specs.append(pl.BlockSpec((tm, g), lambda i, cb=cb, last=last: (jnp.minimum(i, last), cb)))
            else:
                in_specs.append(pl.BlockSpec((tm, g), lambda i, cb=cb: (i, cb)))
            operands.append(arr)
        piece_counts.append(cnt)
    for p in params:
        in_specs.append(pl.BlockSpec(p.shape, lambda i, nd=p.ndim: (0,) * nd))
        operands.append(p)
    n_in = sum(piece_counts)
    n_par = len(params)
    n_or = len(out_rows)
    lat_only = [kind == ['lat'] for _, _, *kind in out_rows]
    out_shape = [jax.ShapeDtypeStruct((t_lat if lat else n_rows, w), dt)
                 for (w, dt, *_), lat in zip(out_rows, lat_only)]
    out_shape += [jax.ShapeDtypeStruct(s, F32) for s in out_accs]
    out_specs = [pl.BlockSpec((tm, w), (lambda i: (jnp.minimum(i, nbl - 1), 0)) if lat else (lambda i: (i, 0)))
                 for (w, *_), lat in zip(out_rows, lat_only)]
    out_specs += [pl.BlockSpec(s, lambda i, nd=len(s): (0,) * nd) for s in out_accs]

    def body(*refs):
        in_refs, par_refs = refs[:n_in], refs[n_in:n_in + n_par]
        orow_refs = refs[n_in + n_par:n_in + n_par + n_or]
        oacc_refs = refs[n_in + n_par + n_or:]
        i = pl.program_id(0)
        tiles, at = [], 0
        for cnt in piece_counts:
            parts = [in_refs[at + p][...].astype(F32) for p in range(cnt)]
            tiles.append(parts[0] if cnt == 1 else jnp.concatenate(parts, axis=1))
            at += cnt
        is_ctx = i * tm >= t_lat
        outs, accs = fn(is_ctx, tiles, [p[...] for p in par_refs])
        for o_ref, o, lat in zip(orow_refs, outs, lat_only):
            if lat:
                @pl.when(jnp.logical_not(is_ctx))
                def _(o_ref=o_ref, o=o):
                    o_ref[...] = o.astype(o_ref.dtype)
            else:
                o_ref[...] = o.astype(o_ref.dtype)
        if oacc_refs:
            @pl.when(i == 0)
            def _():
                for a_ref in oacc_refs:
                    a_ref[...] = jnp.zeros_like(a_ref)
            for a_ref, a in zip(oacc_refs, accs):
                a_ref[...] += a.astype(F32)

    res = pl.pallas_call(
        body, name=name, grid=(nb,),
        out_shape=out_shape, in_specs=in_specs, out_specs=out_specs,
        compiler_params=_cparams(("arbitrary",)),
    )(*operands)
    return res[:n_or], res[n_or:]


def _rms(x, g):
    return x * lax.rsqrt(jnp.mean(x * x, axis=-1, keepdims=True) + EPS) * g


def _norm_mod(x, g, sc, sh):
    return _rms(x, g) * (1.0 + sc) + sh


def _sigmoid(x):
    return 0.5 * jnp.tanh(0.5 * x) + 0.5


def _silu(x):
    return x * _sigmoid(x)


def _gelu(x):
    return 0.5 * x * (1.0 + jnp.tanh(math.sqrt(2.0 / math.pi) * (x + 0.044715 * (x * x * x))))


def _sel(is_ctx, p):
    return jnp.where(is_ctx, p[1:2], p[0:1])


def _seg_acc(is_ctx, v):
    rows = lax.broadcasted_iota(jnp.int32, (2, v.shape[1]), 0)
    return jnp.where(rows == is_ctx.astype(jnp.int32), jnp.broadcast_to(v, (2, v.shape[1])), 0.0)


def _rsum(v):
    return jnp.sum(v, axis=0, keepdims=True)


def _shift_rows(x, o, t_lat, n):
    if o == 0:
        return x
    y = pltpu.roll(x, (-o) % n, 0)
    t = lax.broadcasted_iota(jnp.int32, x.shape, 0)
    src = t + o
    ok = (src >= 0) & (src < n) & ((src >= t_lat) == (t >= t_lat))
    return jnp.where(ok, y, 0.0)


def conv_fwd(name, xarr, col_off, width, w, b, left, n_rows, t_lat, out_dtype, cb=128):
    taps = w.shape[0]
    assert col_off % cb == 0 and width % cb == 0

    def body(x_ref, w_ref, b_ref, o_ref):
        x = x_ref[...].astype(F32)
        acc = jnp.broadcast_to(b_ref[...], x.shape)
        for k in range(taps):
            acc = acc + _shift_rows(x, k - left, t_lat, n_rows) * w_ref[k:k + 1, :]
        o_ref[...] = acc.astype(o_ref.dtype)

    return pl.pallas_call(
        body, name=name, grid=(width // cb,),
        out_shape=jax.ShapeDtypeStruct((n_rows, width), out_dtype),
        in_specs=[pl.BlockSpec((n_rows, cb), lambda j: (0, col_off // cb + j)),
                  pl.BlockSpec((taps, cb), lambda j: (0, j)),
                  pl.BlockSpec((1, cb), lambda j: (0, j))],
        out_specs=pl.BlockSpec((n_rows, cb), lambda j: (0, j)),
        compiler_params=_cparams(("parallel",)),
    )(xarr, w, b)


def conv_bwd(name, dout, xarr, col_off, width, w, left, n_rows, t_lat, cb=128):
    taps = w.shape[0]

    def body(d_ref, x_ref, w_ref, dx_ref, dw_ref, db_ref):
        d = d_ref[...].astype(F32)
        x = x_ref[...].astype(F32)
        dx = jnp.zeros_like(d)
        dws = []
        for k in range(taps):
            dx = dx + _shift_rows(d, left - k, t_lat, n_rows) * w_ref[k:k + 1, :]
            dws.append(_rsum(d * _shift_rows(x, k - left, t_lat, n_rows)))
        dx_ref[...] = dx.astype(dx_ref.dtype)
        dw_ref[...] = jnp.concatenate(dws, axis=0)
        db_ref[...] = _rsum(d)

    return pl.pallas_call(
        body, name=name, grid=(width // cb,),
        out_shape=[jax.ShapeDtypeStruct((n_rows, width), BF16), jax.ShapeDtypeStruct((taps, width), F32),
                   jax.ShapeDtypeStruct((1, width), F32)],
        in_specs=[pl.BlockSpec((n_rows, cb), lambda j: (0, j)),
                  pl.BlockSpec((n_rows, cb), lambda j: (0, col_off // cb + j)),
                  pl.BlockSpec((taps, cb), lambda j: (0, j))],
        out_specs=[pl.BlockSpec((n_rows, cb), lambda j: (0, j)), pl.BlockSpec((taps, cb), lambda j: (0, j)),
                   pl.BlockSpec((1, cb), lambda j: (0, j))],
        compiler_params=_cparams(("parallel",)),
    )(dout, xarr, w)


def _ffn_conv(a, w_ref, b_ref, t_lat):
    shifted = [_shift_rows(a, k - 1, t_lat, t_lat) for k in range(3)]
    ac = jnp.broadcast_to(b_ref[...], a.shape)
    for k in range(3):
        ac = ac + shifted[k] * w_ref[k:k + 1, :]
    return ac, shifted


def ffn_mix_fwd(u, w, b, t_lat, cb=128):
    nblk = FFN // cb

    def body(a_ref, g_ref, w_ref, b_ref, f_ref):
        ac, _ = _ffn_conv(a_ref[...].astype(F32), w_ref, b_ref, t_lat)
        f_ref[...] = (_silu(ac) * g_ref[...].astype(F32)).astype(f_ref.dtype)

    col = lambda shape, off=0: pl.BlockSpec(shape, lambda j: (0, off + j))
    return pl.pallas_call(
        body, name="ffn_mix", grid=(nblk,),
        out_shape=jax.ShapeDtypeStruct((t_lat, FFN), BF16),
        in_specs=[col((t_lat, cb)), col((t_lat, cb), nblk), col((3, cb)), col((1, cb))],
        out_specs=col((t_lat, cb)),
        compiler_params=_cparams(("parallel",)),
    )(u, u, w, b)


def ffn_mix_bwd(u, df, w, b, t_lat, cb=128):
    nblk = FFN // cb

    def body(a_ref, g_ref, df_ref, w_ref, b_ref, du_ref, dw_ref, db_ref):
        ac, shifted = _ffn_conv(a_ref[...].astype(F32), w_ref, b_ref, t_lat)
        d = df_ref[...].astype(F32)
        s = _sigmoid(ac)
        du_ref[1] = (d * (ac * s)).astype(du_ref.dtype)
        dac = d * g_ref[...].astype(F32) * (s * (1.0 + ac * (1.0 - s)))
        da = jnp.zeros_like(dac)
        for k in range(3):
            da = da + _shift_rows(dac, 1 - k, t_lat, t_lat) * w_ref[k:k + 1, :]
        du_ref[0] = da.astype(du_ref.dtype)
        dw_ref[...] = jnp.concatenate([_rsum(dac * shifted[k]) for k in range(3)], axis=0)
        db_ref[...] = _rsum(dac)

    col = lambda shape, off=0: pl.BlockSpec(shape, lambda j: (0, off + j))
    return pl.pallas_call(
        body, name="ffn_mix_bwd", grid=(nblk,),
        out_shape=[jax.ShapeDtypeStruct((2, t_lat, FFN), BF16),
                   jax.ShapeDtypeStruct((3, FFN), F32), jax.ShapeDtypeStruct((1, FFN), F32)],
        in_specs=[col((t_lat, cb)), col((t_lat, cb), nblk), col((t_lat, cb)), col((3, cb)), col((1, cb))],
        out_specs=[pl.BlockSpec((2, t_lat, cb), lambda j: (0, 0, j)), col((3, cb)), col((1, cb))],
        compiler_params=_cparams(("parallel",)),
    )(u, u, df, w, b)


def _chunk_order(direction, nb, nbl):
    if direction == 'f':
        return lambda s: ((s + nbl) % nb, 0)
    return lambda s: (nb - 1 - s, 0)


def _adjoint_order(direction, nb, nbl):
    if direction == 'f':
        return lambda s: ((nb - 1 - s + nbl) % nb, 0)
    return lambda s: (s, 0)


SUBLANES = 8


def _chunk_scan(a, b, carry, rev):
    tc = a.shape[0]
    row = lax.broadcasted_iota(jnp.int32, a.shape, 0)
    in_tile = jnp.bitwise_and(row, SUBLANES - 1)
    for k in (1, 2, 4):
        shift = tc - k if rev else k
        edge = in_tile >= SUBLANES - k if rev else in_tile < k
        b = jnp.where(edge, b, a * pltpu.roll(b, shift, 0) + b)
        a = jnp.where(edge, a, a * pltpu.roll(a, shift, 0))
    nt = tc // SUBLANES
    hs = [None] * nt
    c = carry
    for kt in range(nt):
        k = nt - 1 - kt if rev else kt
        h = b[k * SUBLANES:(k + 1) * SUBLANES] + a[k * SUBLANES:(k + 1) * SUBLANES] * c
        hs[k] = h
        c = h[0:1] if rev else h[SUBLANES - 1:SUBLANES]
    h = jnp.concatenate(hs, axis=0)
    if rev:
        return h, jnp.where(row == tc - 1, carry, pltpu.roll(h, tc - 1, 0)), c
    return h, jnp.where(row == 0, carry, pltpu.roll(h, 1, 0)), c


def scan_fwd(name, a, u, direction, n_rows, t_lat):
    w = a.shape[1]
    tc = _pick(math.gcd(t_lat, n_rows), (256, 128))
    nb, nbl = n_rows // tc, t_lat // tc
    order = _chunk_order(direction, nb, nbl)
    rev = direction == 'b'

    def body(a_ref, u_ref, h_ref, hp_ref, carry):
        @pl.when(pl.program_id(0) == 0)
        def _():
            carry[...] = jnp.zeros_like(carry)

        h_ref[...], hp_ref[...], carry[...] = _chunk_scan(a_ref[...], u_ref[...], carry[...], rev)

    spec = pl.BlockSpec((tc, w), order)
    return pl.pallas_call(
        body, name=name, grid=(nb,),
        out_shape=[jax.ShapeDtypeStruct((n_rows, w), F32)] * 2,
        in_specs=[spec, spec], out_specs=[spec, spec],
        scratch_shapes=[pltpu.VMEM((1, w), F32)],
        compiler_params=_cparams(("arbitrary",)),
    )(a, u)


def scan_adj(name, a, dh, hprev, direction, n_rows, t_lat):
    w = a.shape[1]
    tc = _pick(math.gcd(t_lat, n_rows), (256, 128))
    nb, nbl = n_rows // tc, t_lat // tc
    order = _adjoint_order(direction, nb, nbl)
    rev = direction == 'f'

    def dh_order(s):
        c, _ = order(s)
        return (jnp.minimum(c, nbl - 1), 0)

    def body(a_ref, dh_ref, hp_ref, du_ref, da_ref, carry):
        s = pl.program_id(0)

        @pl.when(s == 0)
        def _():
            carry[...] = jnp.zeros_like(carry)

        chunk, _ = order(s)
        live = (chunk < nbl).astype(F32)

        av = a_ref[...]
        dv = dh_ref[...] * live
        _, c_next, carry[...] = _chunk_scan(av, av * dv, carry[...], rev)
        lam = dv + c_next
        du_ref[...] = lam
        da_ref[...] = lam * hp_ref[...]

    spec = pl.BlockSpec((tc, w), order)
    return pl.pallas_call(
        body, name=name, grid=(nb,),
        out_shape=[jax.ShapeDtypeStruct((n_rows, w), F32)] * 2,
        in_specs=[spec, pl.BlockSpec((tc, w), dh_order), spec], out_specs=[spec, spec],
        scratch_shapes=[pltpu.VMEM((1, w), F32)],
        compiler_params=_cparams(("arbitrary",)),
    )(a, dh, hprev)


def _neg_expm1(y):
    series = -(y * (1.0 + y * (0.5 + y * (1.0 / 6.0 + y * (1.0 / 24.0)))))
    return jnp.where(y > -0.03, series, 1.0 - jnp.exp(y))


def _gate_elem(pre_r, pre_i, xc, b_a, b_x, sp):
    r = _sigmoid(pre_r + b_a)
    i = _sigmoid(pre_i + b_x)
    log_a = (-LRU_C) * r * sp
    a = jnp.exp(log_a)
    mult = jnp.sqrt(_neg_expm1(2.0 * log_a))
    return a, mult * (i * xc)


def _gate_elem_bwd(pre_r, pre_i, xc, b_a, b_x, sp, da, du):
    r = _sigmoid(pre_r + b_a)
    i = _sigmoid(pre_i + b_x)
    log_a = (-LRU_C) * r * sp
    a = jnp.exp(log_a)
    m2 = _neg_expm1(2.0 * log_a)
    inv_mult = lax.rsqrt(m2)
    g = du * (m2 * inv_mult)
    d_mult = du * (i * xc)
    d_log_a = (da - d_mult * a * inv_mult) * a
    d_pre_r = d_log_a * ((-LRU_C) * sp) * (r * (1.0 - r))
    d_pre_i = g * xc * (i * (1.0 - i))
    return d_pre_r, d_pre_i, g * i, _rsum(d_log_a * ((-LRU_C) * r))


def _blockdiag(xb16, w_ref_val, d):
    outs = []
    for n in range(LRU_BLOCKS):
        outs.append(jnp.dot(xb16[:, n * LRU_BW:(n + 1) * LRU_BW], w_ref_val[d * LRU_BLOCKS + n],
                            preferred_element_type=F32))
    return jnp.concatenate(outs, axis=1)


def gates_fwd(xc, w_a, w_x, b_a, b_x, sp, n_rows, t_lat, tm):
    def fn(is_ctx, rows, params):
        (x,), (wa, wx, ba, bx, spv) = rows, params
        xb16 = x.astype(BF16)
        outs = []
        for d in range(2):
            a, u = _gate_elem(_blockdiag(xb16, wa, d), _blockdiag(xb16, wx, d), x,
                              ba[d:d + 1], bx[d:d + 1], spv[d:d + 1])
            outs += [a, u]
        return outs, []

    (a_f, u_f, a_b, u_b), _ = rowwise("gates_fwd", fn, [(xc, 0, LRU_W)], [w_a, w_x, b_a, b_x, sp],
                                      [(LRU_W, F32)] * 4, [], n_rows, t_lat, tm)
    return a_f, u_f, a_b, u_b


def gates_bwd(xc, da_f, du_f, da_b, du_b, w_a, w_x, b_a, b_x, sp, n_rows, t_lat, tm):
    def fn(is_ctx, rows, params):
        (x, daf, duf, dab, dub), (wa, wx, ba, bx, spv) = rows, params
        xb16 = x.astype(BF16)
        dxc = jnp.zeros_like(x)
        dwa, dwx, dba, dbx, dsp = [], [], [], [], []
        for d, (da, du) in enumerate(((daf, duf), (dab, dub))):
            dpr, dpi, dx_e, dsp_d = _gate_elem_bwd(_blockdiag(xb16, wa, d), _blockdiag(xb16, wx, d), x,
                                                   ba[d:d + 1], bx[d:d + 1], spv[d:d + 1], da, du)
            dba_d, dbx_d = _rsum(dpr), _rsum(dpi)
            dxc = dxc + dx_e
            dpr16, dpi16 = dpr.astype(BF16), dpi.astype(BF16)
            back = []
            for n in range(LRU_BLOCKS):
                sl = slice(n * LRU_BW, (n + 1) * LRU_BW)
                nt_dims = (((1,), (1,)), ((), ()))
                back.append(lax.dot_general(dpr16[:, sl], wa[d * LRU_BLOCKS + n], nt_dims, preferred_element_type=F32)
                            + lax.dot_general(dpi16[:, sl], wx[d * LRU_BLOCKS + n], nt_dims,
                                              preferred_element_type=F32))
                tn_dims = (((0,), (0,)), ((), ()))
                dwa.append(lax.dot_general(xb16[:, sl], dpr16[:, sl], tn_dims, preferred_element_type=F32)[None])
                dwx.append(lax.dot_general(xb16[:, sl], dpi16[:, sl], tn_dims, preferred_element_type=F32)[None])
            dxc = dxc + jnp.concatenate(back, axis=1)
            dba.append(dba_d)
            dbx.append(dbx_d)
            dsp.append(dsp_d)
        cat0 = lambda xs: jnp.concatenate(xs, axis=0)
        return [dxc], [cat0(dwa), cat0(dwx), cat0(dba), cat0(dbx), cat0(dsp)]

    (dxc,), accs = rowwise("gates_bwd", fn,
                           [(xc, 0, LRU_W), (da_f, 0, LRU_W), (du_f, 0, LRU_W), (da_b, 0, LRU_W), (du_b, 0, LRU_W)],
                           [w_a, w_x, b_a, b_x, sp], [(LRU_W, F32)],
                           [(2 * LRU_BLOCKS, LRU_BW, LRU_BW)] * 2 + [(2, LRU_W)] * 3, n_rows, t_lat, tm)
    return dxc, accs


def _rope_tables(t_lat, n_rows):
    rows = t_lat // GRID_W
    row_ids = jnp.repeat(jnp.arange(rows), GRID_W).astype(F32)
    col_ids = jnp.tile(jnp.arange(GRID_W), rows).astype(F32)
    axis_dim = QK_ROPE // 2
    inv = 1.0 / (ROPE_BASE ** (jnp.arange(0, axis_dim, 2, dtype=F32) / axis_dim))
    ang = jnp.concatenate([row_ids[:, None] * inv, col_ids[:, None] * inv], axis=-1)
    cos, sin = jnp.cos(ang), jnp.sin(ang)
    half = QK_ROPE // 2
    ones, zeros = jnp.ones((t_lat, QK_NOPE), F32), jnp.zeros((t_lat, QK_NOPE), F32)
    pad1, pad0 = jnp.ones((t_lat, HEAD_PAD - QK_DIM), F32), jnp.zeros((t_lat, HEAD_PAD - QK_DIM), F32)
    zh = jnp.zeros((t_lat, half), F32)
    c_tab = jnp.concatenate([ones, cos, cos, pad1], axis=1)
    s1 = jnp.concatenate([zeros, -sin, zh, pad0], axis=1)
    s2 = jnp.concatenate([zeros, zh, sin, pad0], axis=1)
    n_ctx = n_rows - t_lat
    c_tab = jnp.concatenate([c_tab, jnp.ones((n_ctx, HEAD_PAD), F32)], axis=0)
    s1 = jnp.concatenate([s1, jnp.zeros((n_ctx, HEAD_PAD), F32)], axis=0)
    s2 = jnp.concatenate([s2, jnp.zeros((n_ctx, HEAD_PAD), F32)], axis=0)
    return c_tab, s1, s2


def _rope(x, c, s1, s2):
    half = QK_ROPE // 2
    return x * c + pltpu.roll(x, HEAD_PAD - half, 1) * s1 + pltpu.roll(x, half, 1) * s2


def _rope_t(dy, c, s1, s2):
    half = QK_ROPE // 2
    return dy * c + pltpu.roll(dy * s1, half, 1) + pltpu.roll(dy * s2, HEAD_PAD - half, 1)


def _heads(x):
    return [x[:, h * HEAD_PAD:(h + 1) * HEAD_PAD] for h in range(N_HEADS)]


Q_SCALE = QK_DIM ** -0.5 * math.log2(math.e)
ATTN_SUB_ROWS = 128


def attn_fwd(q, k, v, t_lat, n_rows, tq):
    sub = _pick(tq, (ATTN_SUB_ROWS,))

    def body(q_ref, k_ref, v_ref, o_ref, lse_ref):
        kv, vv = k_ref[...], v_ref[...]
        for r in range(tq // sub):
            rows = slice(r * sub, (r + 1) * sub)
            s = lax.dot_general(q_ref[rows, :], kv, (((1,), (1,)), ((), ())), preferred_element_type=F32)
            m = jnp.max(s, axis=-1, keepdims=True)
            p = jnp.exp2(s - m)
            l = jnp.sum(p, axis=-1, keepdims=True)
            o = jnp.dot(p.astype(BF16), vv, preferred_element_type=F32) / l
            o_ref[rows, :] = o.astype(o_ref.dtype)
            lse_ref[rows, :] = jnp.broadcast_to(m + jnp.log2(l), (sub, HEAD_PAD))

    qspec = pl.BlockSpec((tq, HEAD_PAD), lambda h, i: (i, h))
    kspec = pl.BlockSpec((n_rows, HEAD_PAD), lambda h, i: (0, h))
    return pl.pallas_call(
        body, name="attn_fwd", grid=(N_HEADS, t_lat // tq),
        out_shape=[jax.ShapeDtypeStruct((t_lat, N_HEADS * HEAD_PAD), BF16),
                   jax.ShapeDtypeStruct((t_lat, N_HEADS * HEAD_PAD), F32)],
        in_specs=[qspec, kspec, kspec], out_specs=[qspec, qspec],
        compiler_params=_cparams(("parallel", "arbitrary")),
    )(q, k, v)


def attn_bwd(q, k, v, o, do, lse, t_lat, n_rows, tq):
    scale = QK_DIM ** -0.5
    nq = t_lat // tq
    nt = (((1,), (1,)), ((), ()))
    tn = (((0,), (0,)), ((), ()))

    def body(q_ref, k_ref, v_ref, o_ref, do_ref, lse_ref, dq_ref, dk_ref, dv_ref):
        @pl.when(pl.program_id(1) == 0)
        def _():
            dk_ref[...] = jnp.zeros_like(dk_ref)
            dv_ref[...] = jnp.zeros_like(dv_ref)

        qv, kv, vv, dov = q_ref[...], k_ref[...], v_ref[...], do_ref[...]
        s = lax.dot_general(qv, kv, nt, preferred_element_type=F32)
        p = jnp.exp2(s - lse_ref[:, 0:1])
        dv_ref[...] += lax.dot_general(p.astype(BF16), dov, tn, preferred_element_type=F32)
        dp = lax.dot_general(dov, vv, nt, preferred_element_type=F32)
        delta = jnp.sum(dov.astype(F32) * o_ref[...].astype(F32), axis=-1, keepdims=True)
        ds = (p * (dp - delta)).astype(BF16)
        dq_ref[...] = (jnp.dot(ds, kv, preferred_element_type=F32) * scale).astype(dq_ref.dtype)
        dk_ref[...] += lax.dot_general(ds, qv, tn, preferred_element_type=F32)

        @pl.when(pl.program_id(1) == nq - 1)
        def _():
            dk_ref[...] = dk_ref[...] * (scale / Q_SCALE)

    qspec = pl.BlockSpec((tq, HEAD_PAD), lambda h, i: (i, h))
    kspec = pl.BlockSpec((n_rows, HEAD_PAD), lambda h, i: (0, h))
    return pl.pallas_call(
        body, name="attn_bwd", grid=(N_HEADS, t_lat // tq),
        out_shape=[jax.ShapeDtypeStruct((t_lat, N_HEADS * HEAD_PAD), BF16),
                   jax.ShapeDtypeStruct((n_rows, N_HEADS * HEAD_PAD), F32),
                   jax.ShapeDtypeStruct((n_rows, N_HEADS * HEAD_PAD), F32)],
        in_specs=[qspec, kspec, kspec, qspec, qspec, qspec], out_specs=[qspec, kspec, kspec],
        compiler_params=_cparams(("parallel", "arbitrary")),
    )(q, k, v, o, do, lse)


def adamw(name, w, g, m, v):
    r, ccols = w.shape
    if r % 8 == 0:
        tr, tcol = _best_div(r, 8, max(8, 262144 // ccols)), ccols
    else:
        tr, tcol = r, _pick(ccols, (256, 128))
    c1 = 1.0 - ADAM_B1 ** ADAM_STEP
    c2 = 1.0 - ADAM_B2 ** ADAM_STEP

    def body(w_ref, g_ref, m_ref, v_ref, d_ref, nm_ref, nv_ref):
        gv = g_ref[...]
        nm = ADAM_B1 * m_ref[...] + (1.0 - ADAM_B1) * gv
        nv = ADAM_B2 * v_ref[...] + (1.0 - ADAM_B2) * (gv * gv)
        d_ref[...] = -ADAM_LR * ((nm / c1) / (jnp.sqrt(nv / c2) + ADAM_EPS) + ADAM_WD * w_ref[...])
        nm_ref[...] = nm
        nv_ref[...] = nv

    spec = pl.BlockSpec((tr, tcol), lambda i, j: (i, j))
    return pl.pallas_call(
        body, name=name, grid=(r // tr, ccols // tcol),
        out_shape=[jax.ShapeDtypeStruct((r, ccols), F32)] * 3,
        in_specs=[spec] * 4, out_specs=[spec] * 3,
        compiler_params=_cparams(("parallel", "parallel")),
    )(w, g, m, v)


def adamw_many(name, ws, gs, ms, vs):
    n = len(ws)
    c1 = 1.0 - ADAM_B1 ** ADAM_STEP
    c2 = 1.0 - ADAM_B2 ** ADAM_STEP

    def body(*refs):
        for i in range(n):
            w_ref, g_ref, m_ref, v_ref = (refs[k * n + i] for k in range(4))
            d_ref, nm_ref, nv_ref = (refs[(4 + k) * n + i] for k in range(3))
            gv = g_ref[...]
            nm = ADAM_B1 * m_ref[...] + (1.0 - ADAM_B1) * gv
            nv = ADAM_B2 * v_ref[...] + (1.0 - ADAM_B2) * (gv * gv)
            d_ref[...] = -ADAM_LR * ((nm / c1) / (jnp.sqrt(nv / c2) + ADAM_EPS) + ADAM_WD * w_ref[...])
            nm_ref[...] = nm
            nv_ref[...] = nv

    vmem = pl.BlockSpec(memory_space=pltpu.VMEM)
    res = pl.pallas_call(
        body, name=name,
        out_shape=[jax.ShapeDtypeStruct(w.shape, F32) for w in ws] * 3,
        in_specs=[vmem] * (4 * n), out_specs=[vmem] * (3 * n),
        compiler_params=_cparams(),
    )(*ws, *gs, *ms, *vs)
    return [tuple(res[k * n + i] for k in range(3)) for i in range(n)]


def _flat(parts, dtype, row_mult):
    v = jnp.concatenate([p.reshape(-1).astype(dtype) for p in parts])
    quantum = row_mult * FLAT_C
    total = -(-v.shape[0] // quantum) * quantum
    return jnp.pad(v, (0, total - v.shape[0])).reshape(total // FLAT_C, FLAT_C)


def _unflat(flat, shapes):
    v = flat.reshape(-1)
    out, at = [], 0
    for s in shapes:
        n = math.prod(s)
        out.append(v[at:at + n].reshape(s))
        at += n
    return out


def _gathered_to_full(name, g):
    k = g.shape[1]
    return jnp.transpose(g, (1, 0, 2)).reshape(k, N_DEV * g.shape[2])


def _full_to_chunks(name, full):
    k, n = full.shape
    return jnp.transpose(full.reshape(k, N_DEV, n // N_DEV), (1, 0, 2)).reshape(N_DEV, -1)


def _shard_to_rb(name, w):
    return w if name in ROW_SHARDED else w.T


def _rb_to_shard(name, g):
    return g if name in ROW_SHARDED else g.T


def _rb_from_gathered(name, g):
    cols = g.shape[2]
    if name == 'w_in':
        z = lambda k: jnp.zeros((k, cols), g.dtype)
        full = g.reshape(N_DEV * g.shape[1], cols)
        return jnp.concatenate([full[:Z_KR], z(QK_NOPE), full[Z_KR:Z_KR + QK_ROPE], z(HEAD_PAD - QK_DIM),
                                full[Z_KR + QK_ROPE:]], axis=0)
    if name == 'w_uq':
        return jnp.pad(g, ((0, 0), (0, HEAD_PAD - QK_DIM), (0, 0))).reshape(N_HEADS * HEAD_PAD, cols)
    if name == 'w_ukv':
        pad = lambda t: jnp.pad(t, ((0, 0), (0, HEAD_PAD - t.shape[1]), (0, 0))).reshape(N_HEADS * HEAD_PAD, cols)
        return jnp.concatenate([pad(g[:, :QK_NOPE]), pad(g[:, QK_NOPE:])], axis=0)
    if name == 'w_o_attn':
        full = g.reshape(D, N_HEADS, V_HEAD)
        return jnp.pad(full, ((0, 0), (0, 0), (0, HEAD_PAD - V_HEAD))).reshape(D, N_HEADS * HEAD_PAD)
    return g.reshape(N_DEV * g.shape[1], cols)


def _chunks_from_rb_grad(name, g):
    cols = g.shape[1]
    if name == 'w_in':
        full = jnp.concatenate([g[:Z_KR], g[Z_KR + QK_NOPE:Z_KR + QK_DIM], g[Z_XB:]], axis=0)
        return full.reshape(N_DEV, -1, cols)
    if name == 'w_uq':
        return g.reshape(N_HEADS, HEAD_PAD, cols)[:, :QK_DIM]
    if name == 'w_ukv':
        half = N_HEADS * HEAD_PAD
        gk = g[:half].reshape(N_HEADS, HEAD_PAD, cols)[:, :QK_NOPE]
        gv = g[half:].reshape(N_HEADS, HEAD_PAD, cols)[:, :V_HEAD]
        return jnp.concatenate([gk, gv], axis=1)
    if name == 'w_o_attn':
        full = g.reshape(D, N_HEADS, HEAD_PAD)[:, :, :V_HEAD].reshape(D, N_HEADS * V_HEAD)
        return full.reshape(N_DEV, D // N_DEV, N_HEADS * V_HEAD)
    return g.reshape(N_DEV, -1, cols)


def local_step(x, ctx, target, mod_l, mod_c, wt, on_grad=None, arrive=None):
    t_lat, n_ctx = x.shape[0], ctx.shape[0]
    n = t_lat + n_ctx
    tm = _pick(math.gcd(t_lat, n), (256, 128))
    tq_fwd = _pick(t_lat, (256, 128))
    tq_bwd = _pick(t_lat, (512, 256, 128))
    row = lambda v: v.reshape(1, -1).astype(F32)
    two = lambda a, b: jnp.stack([a, b]).astype(F32)
    sh1_l, sc1_l, g1_l, sh2_l, sc2_l, g2_l = jnp.split(mod_l, 6)
    sh1_c, sc1_c = jnp.split(mod_c, 6)[:2]
    sc1, sh1 = two(sc1_l, sc1_c), two(sh1_l, sh1_c)
    g1, g2, sc2, sh2 = row(g1_l), row(g2_l), row(sc2_l), row(sh2_l)
    norm1_g, norm2_g, final_g = row(wt['norm1_g']), row(wt['norm2_g']), row(wt['final_g'])
    q_g, kv_g, b_gate = row(wt['q_norm_g']), row(wt['kv_norm_g']), row(wt['b_gate'])
    wt = dict(wt)
    pending = []

    def sent():
        tokens = list(pending)
        pending.clear()
        return tokens

    def need(names, after):
        if arrive is not None:
            got = arrive(names, after)
            if '_token' in got:
                pending.append(got.pop('_token'))
            wt.update(got)
        return [wt[n] for n in names]
    lru_w_a = wt['lru_w_a'].reshape(2 * LRU_BLOCKS, LRU_BW, LRU_BW).astype(BF16)
    lru_w_x = wt['lru_w_x'].reshape(2 * LRU_BLOCKS, LRU_BW, LRU_BW).astype(BF16)
    b_a, b_x, lam = wt['lru_b_a'], wt['lru_b_x'], wt['lru_lambda']
    sp = jnp.logaddexp(-lam, 0.0)
    c_tab, s1_tab, s2_tab = _rope_tables(t_lat, n)
    rw = functools.partial(rowwise, n_rows=n, t_lat=t_lat, tm=tm)
    rw_lat = functools.partial(rowwise, n_rows=t_lat, t_lat=t_lat, tm=tm)

    stream = [(x, 0, D), (ctx, 0, D, 'ctx')]

    def f_norm1(is_ctx, rows, params):
        (xl, xc_), (g, sc, sh) = rows, params
        return [_norm_mod(jnp.where(is_ctx, xc_, xl), g, _sel(is_ctx, sc), _sel(is_ctx, sh))], []

    (h,), _ = rw("norm1", f_norm1, stream, [norm1_g, sc1, sh1], [(D, BF16)], [])
    (w_in_t,) = need(('w_in',), h)
    z = matmul("w_in", h, w_in_t, 'nt', BF16, after=sent())
    w_uq_t, w_ukv_t, w_o_lru = need(('w_uq', 'w_ukv', 'w_o_lru'), z)

    def f_qkv_norm(is_ctx, rows, params):
        (ql, kvl), (gq, gkv) = rows, params
        return [_rms(ql, gq), _rms(kvl, gkv)], []

    (qn, kvn), _ = rw("qkv_norm", f_qkv_norm, [(z, Z_Q, Q_RANK), (z, Z_KV, KV_RANK)], [q_g, kv_g],
                      [(Q_RANK, BF16), (KV_RANK, BF16)], [])
    qp = matmul("w_uq", qn, w_uq_t, 'nt', BF16)
    kvp = matmul("w_ukv", kvn, w_ukv_t, 'nt', BF16)

    def f_rope(is_ctx, rows, params):
        qv, kk, vv, kr, c, s1, s2 = rows
        krr = _rope(kr, c, s1, s2)
        qo = jnp.concatenate([_rope(qh, c, s1, s2) for qh in _heads(qv)], axis=1) * Q_SCALE
        ko = jnp.concatenate([kh + krr for kh in _heads(kk)], axis=1)
        return [qo, ko, vv], []

    hp = N_HEADS * HEAD_PAD
    (qr, kr_, vr), _ = rw("rope", f_rope,
                          [(qp, 0, hp), (kvp, 0, hp), (kvp, hp, hp), (z, Z_KR, HEAD_PAD), (c_tab, 0, HEAD_PAD),
                           (s1_tab, 0, HEAD_PAD), (s2_tab, 0, HEAD_PAD)], [], [(hp, BF16)] * 3, [])
    attn, lse = attn_fwd(qr, kr_, vr, t_lat, n, tq_fwd)

    xc = conv_fwd("lru_conv", z, Z_XB, LRU_W, wt['lru_conv_w'], row(wt['lru_conv_b']), 2, n, t_lat, F32)
    a_f, u_f, a_b, u_b = gates_fwd(xc, lru_w_a, lru_w_x, b_a, b_x, sp, n, t_lat, tm)
    h_f, hp_f = scan_fwd("scan_f", a_f, u_f, 'f', n, t_lat)
    h_b, hp_b = scan_fwd("scan_b", a_b, u_b, 'b', n, t_lat)

    def f_lru_out(is_ctx, rows, params):
        hf, hb, yb = rows
        return [(hf + hb) * _gelu(yb)], []

    (ybin,), _ = rw_lat("lru_out", f_lru_out, [(h_f, 0, LRU_W), (h_b, 0, LRU_W), (z, Z_YB, LRU_W)], [],
                        [(LRU_W, BF16)], [])
    w_o_attn_t, w_out, w_up_t, w_down = need(('w_o_attn', 'w_out', 'w_up', 'w_down'), attn)
    y_a = matmul("w_o_attn", attn, w_o_attn_t, 'nt', BF16)
    y_b = matmul("w_o_lru", ybin, w_o_lru, 'nn', BF16)

    def _merge(ya, yb, gl, bg):
        gates = _sigmoid(gl + bg)
        return gates[:, :D] * ya + gates[:, D:] * yb

    def f_merge(is_ctx, rows, params):
        (ya, yb, gl), (bg,) = rows, params
        return [_merge(ya, yb, gl, bg)], []

    (mrg,), _ = rw_lat("merge", f_merge, [(y_a, 0, D), (y_b, 0, D), (z, Z_GL, 2 * D)], [b_gate], [(D, BF16)], [])
    o = matmul("w_out", mrg, w_out, 'nn', BF16)

    def _res_norm2(xv, ov, g1v, g, sc, sh):
        x1 = xv + g1v * ov
        return x1, _norm_mod(x1, g, sc, sh)

    def f_norm2(is_ctx, rows, params):
        (xv, ov), (g1v, g, sc, sh) = rows, params
        x1, h2v = _res_norm2(xv, ov, g1v, g, sc, sh)
        return [x1, h2v], []

    (x1, h2), _ = rw_lat("norm2", f_norm2, [(x, 0, D), (o, 0, D)], [g1, norm2_g, sc2, sh2], [(D, F32), (D, BF16)], [])
    u = matmul("w_up", h2, w_up_t, 'nt', BF16)
    f = ffn_mix_fwd(u, wt['ffn_conv_w'], row(wt['ffn_conv_b']), t_lat)
    dn = matmul("w_down", f, w_down, 'nn', BF16)

    def _tile_loss(x1v, dv, g2v, fg, tgt):
        y = _rms(x1v + g2v * dv, fg)
        e = y - tgt
        return 0.5 * jnp.sum(jnp.mean(e * e, axis=-1, keepdims=True), axis=0, keepdims=True)

    def f_final(is_ctx, rows, params):
        (x1v, dv, tgt), (g2v, fg) = rows, params
        lv, vjp = jax.vjp(lambda a, b, c, d: _tile_loss(a, b, c, d, tgt), x1v, dv, g2v, fg)
        dx2, dd, dg2, dfg = vjp(jnp.ones((1, 1), F32))
        return [dx2, dd], [dg2, dfg, jnp.broadcast_to(lv, (1, 128))]

    (dx2, dd), (dg2, dfinal_g, loss_v) = rw_lat("final", f_final, [(x1, 0, D), (dn, 0, D), (target, 0, D)],
                                                [g2, final_g], [(D, F32), (D, BF16)], [(1, D), (1, D), (1, 128)])
    loss = loss_v[0, 0]

    grads = {'final_g': dfinal_g}

    def put(name, g):
        grads[name] = g
        if on_grad is not None:
            pending.append(on_grad(name, g))
    df = matmul("d_f", dd, w_down, 'nt', BF16)
    put('w_down', matmul("g_w_down", f, dd, 'tn', BF16))

    du, grads['ffn_conv_w'], grads['ffn_conv_b'] = ffn_mix_bwd(u, df, wt['ffn_conv_w'], row(wt['ffn_conv_b']),
                                                               t_lat)
    dh2 = matmul("d_h2", du, w_up_t, 'nn', BF16, after=sent())
    put('w_up', matmul("g_w_up", du, h2, 'tn', BF16))

    def b_norm2(is_ctx, rows, params):
        (xv, ov, dh2v, dx2v), (g1v, g, sc, sh) = rows, params
        _, vjp = jax.vjp(_res_norm2, xv, ov, g1v, g, sc, sh)
        dx, do, dg1v, dg, dsc, dsh = vjp((dx2v, dh2v))
        return [dx, do], [dg1v, dg, dsc, dsh]

    (dx_res, do), (dg1, dnorm2_g, dsc2, dsh2) = rw_lat(
        "norm2_bwd", b_norm2, [(x, 0, D), (o, 0, D), (dh2, 0, D), (dx2, 0, D)], [g1, norm2_g, sc2, sh2],
        [(D, F32), (D, BF16)], [(1, D)] * 4)
    grads['norm2_g'] = dnorm2_g
    dmrg = matmul("d_merge", do, w_out, 'nt', BF16, after=sent())
    put('w_out', matmul("g_w_out", mrg, do, 'tn', BF16))

    def b_merge(is_ctx, rows, params):
        (ya, yb, gl, dm), (bg,) = rows, params
        _, vjp = jax.vjp(_merge, ya, yb, gl, bg)
        dya, dyb, dgl, dbg = vjp(dm)
        return [dya, dyb, dgl], [dbg]

    (dy_a, dy_b, dgl), (grads['b_gate'],) = rw_lat(
        "merge_bwd", b_merge, [(y_a, 0, D), (y_b, 0, D), (z, Z_GL, 2 * D), (dmrg, 0, D)], [b_gate],
        [(D, BF16), (D, BF16), (2 * D, BF16)], [(1, 2 * D)])
    dattn = matmul("d_attn", dy_a, w_o_attn_t, 'nn', BF16, after=sent())
    put('w_o_attn', matmul("g_w_o_attn", dy_a, attn, 'tn', BF16))
    dybin = matmul("d_lru_out", dy_b, w_o_lru, 'nt', BF16, after=sent())
    put('w_o_lru', matmul("g_w_o_lru", ybin, dy_b, 'tn', BF16))

    def b_lru_out(is_ctx, rows, params):
        hf, hb, yb, dyv = rows
        _, vjp = jax.vjp(lambda s, y: s * _gelu(y), hf + hb, yb)
        dh, dyb = vjp(dyv)
        return [dh, dyb], []

    (dh_lru, dyb), _ = rw_lat("lru_out_bwd", b_lru_out,
                              [(h_f, 0, LRU_W), (h_b, 0, LRU_W), (z, Z_YB, LRU_W), (dybin, 0, LRU_W)], [],
                              [(LRU_W, F32), (LRU_W, BF16)], [])
    du_f, da_f = scan_adj("scan_f_adj", a_f, dh_lru, hp_f, 'f', n, t_lat)
    du_b, da_b = scan_adj("scan_b_adj", a_b, dh_lru, hp_b, 'b', n, t_lat)
    dxc, (dw_a, dw_x, db_a, db_x, dsp) = gates_bwd(xc, da_f, du_f, da_b, du_b, lru_w_a, lru_w_x, b_a, b_x, sp,
                                                   n, t_lat, tm)
    put('lru_w_a', dw_a.reshape(2 * LRU_BLOCKS * LRU_BW, LRU_BW).astype(BF16))
    put('lru_w_x', dw_x.reshape(2 * LRU_BLOCKS * LRU_BW, LRU_BW).astype(BF16))
    grads['lru_b_a'], grads['lru_b_x'] = db_a, db_x
    grads['lru_lambda'] = -dsp * _sigmoid(-lam)
    dxb, grads['lru_conv_w'], grads['lru_conv_b'] = conv_bwd("lru_conv_bwd", dxc, z, Z_XB, LRU_W, wt['lru_conv_w'],
                                                             2, n, t_lat)

    dq, dk, dv = attn_bwd(qr, kr_, vr, attn, dattn, lse, t_lat, n, tq_bwd)

    def b_rope(is_ctx, rows, params):
        dqv, dkv, dvv, c, s1, s2 = rows
        live = jnp.where(is_ctx, 0.0, 1.0)
        dqo = jnp.concatenate([_rope_t(dqh, c, s1, s2) for dqh in _heads(dqv)], axis=1) * live
        dkh = _heads(dkv)
        dkr = dkh[0]
        for t in dkh[1:]:
            dkr = dkr + t
        lanes = lax.broadcasted_iota(jnp.int32, dkr.shape, 1)
        dkr = jnp.where((lanes >= QK_NOPE) & (lanes < QK_DIM), _rope_t(dkr, c, s1, s2), 0.0)
        return [dqo, jnp.concatenate([dkv, dvv], axis=1), dkr], []

    (dqp, dkvp, dkr), _ = rw("rope_bwd", b_rope,
                             [(dq, 0, hp), (dk, 0, hp), (dv, 0, hp), (c_tab, 0, HEAD_PAD), (s1_tab, 0, HEAD_PAD),
                              (s2_tab, 0, HEAD_PAD)], [], [(hp, BF16), (2 * hp, BF16), (HEAD_PAD, BF16)], [])
    dqn = matmul("d_qn", dqp, w_uq_t, 'nn', BF16, after=sent())
    put('w_uq', matmul("g_w_uq", dqp, qn, 'tn', BF16))
    dkvn = matmul("d_kvn", dkvp, w_ukv_t, 'nn', BF16, after=sent())
    put('w_ukv', matmul("g_w_ukv", dkvp, kvn, 'tn', BF16))

    def b_qkv_norm(is_ctx, rows, params):
        (ql, kvl, dqv, dkvv), (gq, gkv) = rows, params
        _, vjp_q = jax.vjp(_rms, ql, gq)
        _, vjp_kv = jax.vjp(_rms, kvl, gkv)
        dql, dgq = vjp_q(dqv)
        dkvl, dgkv = vjp_kv(dkvv)
        return [dql, dkvl], [dgq, dgkv]

    (dq_lat, dkv_lat), (grads['q_norm_g'], grads['kv_norm_g']) = rw(
        "qkv_norm_bwd", b_qkv_norm, [(z, Z_Q, Q_RANK), (z, Z_KV, KV_RANK), (dqn, 0, Q_RANK), (dkvn, 0, KV_RANK)],
        [q_g, kv_g], [(Q_RANK, BF16), (KV_RANK, BF16)], [(1, Q_RANK), (1, KV_RANK)])
    pad_ctx = lambda t: jnp.pad(t, ((0, n_ctx), (0, 0)))
    dz = jnp.concatenate([dq_lat, dkv_lat, dkr, dxb, pad_ctx(dyb), pad_ctx(dgl)], axis=1)
    put('w_in', matmul("g_w_in", dz, h, 'tn', BF16))
    dh = matmul("d_h", dz, w_in_t, 'nn', BF16, after=sent())

    def b_norm1(is_ctx, rows, params):
        (xl, xc_, dhv, dxr), (g, sc, sh) = rows, params
        scv, shv = _sel(is_ctx, sc), _sel(is_ctx, sh)
        _, vjp = jax.vjp(_norm_mod, jnp.where(is_ctx, xc_, xl), g, scv, shv)
        dx, dg, dsc, dsh = vjp(dhv)
        return [dx + dxr], [dg, _seg_acc(is_ctx, dsc), _seg_acc(is_ctx, dsh)]

    (grad_x,), (grads['norm1_g'], dsc1, dsh1) = rw("norm1_bwd", b_norm1, stream + [(dh, 0, D), (dx_res, 0, D)],
                                                   [norm1_g, sc1, sh1], [(D, F32, 'lat')],
                                                   [(1, D), (2, D), (2, D)])
    zero = jnp.zeros((D,), F32)
    dmod_l = jnp.concatenate([dsh1[0], dsc1[0], dg1[0], dsh2[0], dsc2[0], dg2[0]])
    dmod_c = jnp.concatenate([dsh1[1], dsc1[1], zero, zero, zero, zero])
    return loss, grad_x, grads, dmod_l, dmod_c


def kernel(x, c, ctx, c_ctx, w_mod, b_mod, norm1_g, w_in, b_gate, q_norm_g, kv_norm_g, w_uq, w_ukv, w_o_attn, lru_conv_w, lru_conv_b, lru_w_a, lru_b_a, lru_w_x, lru_b_x, lru_lambda, w_o_lru, w_out, norm2_g, w_up, ffn_conv_w, ffn_conv_b, w_down, final_g, loss_target, m_c_ctx, m_w_mod, m_b_mod, m_norm1_g, m_w_in, m_b_gate, m_q_norm_g, m_kv_norm_g, m_w_uq, m_w_ukv, m_w_o_attn, m_lru_conv_w, m_lru_conv_b, m_lru_w_a, m_lru_b_a, m_lru_w_x, m_lru_b_x, m_lru_lambda, m_w_o_lru, m_w_out, m_norm2_g, m_w_up, m_ffn_conv_w, m_ffn_conv_b, m_w_down, m_final_g, v_c_ctx, v_w_mod, v_b_mod, v_norm1_g, v_w_in, v_b_gate, v_q_norm_g, v_kv_norm_g, v_w_uq, v_w_ukv, v_w_o_attn, v_lru_conv_w, v_lru_conv_b, v_lru_w_a, v_lru_b_a, v_lru_w_x, v_lru_b_x, v_lru_lambda, v_w_o_lru, v_w_out, v_norm2_g, v_w_up, v_ffn_conv_w, v_ffn_conv_b, v_w_down, v_final_g):
    given = dict(locals())
    strip = lambda name, a: a if name in ('c_ctx', 'final_g') else a[0]
    wsh = {n: strip(n, given[n]) for n in WEIGHTS}
    msh = {n: strip(n, given['m_' + n]) for n in WEIGHTS}
    vsh = {n: strip(n, given['v_' + n]) for n in WEIGHTS}
    me = _my_index()

    small = _flat([c[0]] + [wsh[n] for n in SMALL_F32], F32, 8)
    small_all = all_gather("gather_small", small).reshape(N_DEV, -1)
    c_all = small_all[:, :D]
    full, at = {}, D
    for n in SMALL_F32:
        cnt = math.prod(wsh[n].shape)
        full[n] = _gathered_to_full(n, small_all[:, at:at + cnt].reshape((N_DEV,) + wsh[n].shape))
        at += cnt

    cond = jnp.concatenate([c_all, c_ctx[None], jnp.zeros((7, D), F32)], axis=0)
    sil = cond * jax.nn.sigmoid(cond)
    mod_cols = matmul("mod_proj", sil, wsh['w_mod'], 'nn', F32)
    mod_all = all_gather("gather_mod", mod_cols)
    mod_all = jnp.transpose(mod_all, (1, 0, 2)).reshape(16, 6 * D) + b_mod[0][None]
    mod_l = lax.dynamic_index_in_dim(mod_all, me, axis=0, keepdims=False)
    mod_c = mod_all[N_DEV]

    rb_shards = {n: _shard_to_rb(n, wsh[n]).astype(BF16) for n in BIG_BF16}
    (w_in_blocks,) = all_gather_multi("gather_w_in", [rb_shards['w_in']])
    later = [n for n in BIG_BF16 if n != 'w_in']
    weights_started, weights_sent = exchange_start("weights_send", 'gather', [rb_shards[n] for n in later],
                                                   after=[w_in_blocks, mod_all])
    for n in REPLICATED:
        if n not in ('c_ctx', 'b_mod'):
            full[n] = wsh[n]

    def arrive(names, after):
        if names == ('w_in',):
            return {'w_in': _rb_from_gathered('w_in', w_in_blocks), '_token': weights_sent}
        picked = [later.index(n) for n in names]
        lands = exchange_wait("weights_wait_" + names[0], 'gather',
                              tuple([part[i] for i in picked] for part in weights_started), after)
        return {n: _rb_from_gathered(n, lax.dynamic_update_slice_in_dim(land, rb_shards[n][None], me, axis=0))
                for n, land in zip(names, lands)}

    in_flight = {}

    def on_grad(n, g):
        chunks = _chunks_from_rb_grad(n, g)
        own = lax.dynamic_index_in_dim(chunks, me, axis=0, keepdims=True)
        started, token = exchange_start("grad_send_" + n, 'scatter', [chunks])
        in_flight[n] = (own, started)
        return token

    loss, grad_x, grads, dmod_l, dmod_c = local_step(x[0], ctx[0], loss_target[0], mod_l, mod_c, full, on_grad,
                                                     arrive)
    dmod = jnp.stack([dmod_l, dmod_c]).reshape(2 * 6 * D // FLAT_C, FLAT_C)
    dm = all_gather("gather_dmod", dmod).reshape(N_DEV, 2, 6 * D)
    dmod_c_tot = dm[0, 1]
    for p in range(1, N_DEV):
        dmod_c_tot = dmod_c_tot + dm[p, 1]
    dm16 = jnp.concatenate([dm[:, 0], dmod_c_tot[None], jnp.zeros((7, 6 * D), F32)], axis=0)
    ncol = 6 * D // N_DEV
    dm16_cols = lax.dynamic_slice_in_dim(dm16.reshape(16, N_DEV, ncol), me, 1, axis=1)[:, 0]
    grad_w_mod = matmul("g_w_mod", sil, dm16_cols, 'tn', F32)
    dsil = matmul("d_cond", dm16_cols, wsh['w_mod'], 'nt', F32)
    sg = jax.nn.sigmoid(c_ctx)
    grads['c_ctx'] = dsil[N_DEV] * (sg * (1.0 + c_ctx * (1.0 - sg)))
    grads['b_mod'] = dmod_l + dmod_c

    g_final = {'w_mod': grad_w_mod}
    reduced, stepped = {}, {}
    for n in BIG_BF16 + ['lru_w_a', 'lru_w_x']:
        own, started = in_flight[n]
        (land,) = exchange_wait("grad_wait_" + n, 'scatter', started, dm)
        if n in ROW_SHARDED:
            g_final[n], *stepped[n] = reduce_slots("step_" + n, land, own, (wsh[n], msh[n], vsh[n]))
        elif n in COL_SHARDED and wsh[n].shape[1] % 128:
            g_t, *outs = reduce_slots("step_" + n, land, own, (wsh[n].T, msh[n].T, vsh[n].T))
            g_final[n], stepped[n] = g_t.T, [o.T for o in outs]
        else:
            reduced[n] = reduce_slots("sum_" + n, land, own)
            if n in BIG_BF16:
                g_final[n] = _rb_to_shard(n, reduced[n])

    small_names = SMALL_F32 + [n for n in REPLICATED if n not in ('lru_w_a', 'lru_w_x')]
    partials = _flat([grads[n] for n in small_names] + [loss], F32, 8)
    parts_all, a_all, x_all = all_gather_multi("gather_small_grads", [partials, reduced['lru_w_a'], reduced['lru_w_x']])
    small_sum = sum_slots("sum_small", parts_all).reshape(-1)
    g_final['lru_w_a'], g_final['lru_w_x'] = a_all.reshape(wsh['lru_w_a'].shape), x_all.reshape(wsh['lru_w_x'].shape)
    at = 0
    for n in small_names:
        cnt = math.prod(full[n].shape) if n in SMALL_F32 else math.prod(wsh[n].shape)
        g = small_sum[at:at + cnt]
        if n in SMALL_F32:
            k = full[n].shape[0]
            g = lax.dynamic_index_in_dim(g.reshape(k, N_DEV, -1), me, axis=1, keepdims=False)
        g_final[n] = g.reshape(wsh[n].shape)
        at += cnt
    loss = small_sum[at]

    for n in ['w_mod'] + BIG_BF16:
        if n not in stepped:
            stepped[n] = adamw("adamw_" + n, wsh[n], g_final[n], msh[n], vsh[n])
    rest = [n for n in WEIGHTS if n not in stepped]
    as2d = lambda a: a.reshape(-1, a.shape[-1])
    rest_out = adamw_many("adamw_small", *[[as2d(d[n]) for n in rest] for d in (wsh, g_final, msh, vsh)])
    stepped.update(zip(rest, rest_out))
    shaped = lambda n, a: a.reshape(given[n].shape)
    return (loss, grad_x[None],
            *[shaped(n, g_final[n]) for n in WEIGHTS],
            *[shaped(n, stepped[n][k]) for k in range(3) for n in WEIGHTS])
```

```python
import functools
import math

import jax
import jax.numpy as jnp
from jax import lax
from jax.experimental import pallas as pl
from jax.experimental.pallas import tpu as pltpu

F32 = jnp.float32
BF16 = jnp.bfloat16
MESH = pl.DeviceIdType.MESH

N_DEV = 8
D = 1024
N_HEADS = 8
HEAD_PAD = 128
QK_NOPE, QK_ROPE, V_HEAD = 64, 32, 64
QK_DIM = QK_NOPE + QK_ROPE
Q_RANK, KV_RANK = 384, 256
LRU_W, LRU_BLOCKS, LRU_BW = 1280, 10, 128
FFN = 2816
GRID_W = 64
ROPE_BASE = 10000.0
LRU_C = 8.0
EPS = 1e-6
Z_Q, Z_KV, Z_KR, Z_XB, Z_YB, Z_GL, Z_END = 0, 384, 640, 768, 2048, 3328, 5376
ADAM_LR, ADAM_B1, ADAM_B2, ADAM_EPS, ADAM_WD, ADAM_STEP = 0.001, 0.9, 0.999, 1e-08, 0.01, 10

VMEM_LIMIT = 52 * 1024 * 1024
FLAT_C = 512
BIG_ROWS = 256

WEIGHTS = ['c_ctx', 'w_mod', 'b_mod', 'norm1_g', 'w_in', 'b_gate', 'q_norm_g', 'kv_norm_g', 'w_uq', 'w_ukv',
           'w_o_attn', 'lru_conv_w', 'lru_conv_b', 'lru_w_a', 'lru_b_a', 'lru_w_x', 'lru_b_x', 'lru_lambda',
           'w_o_lru', 'w_out', 'norm2_g', 'w_up', 'ffn_conv_w', 'ffn_conv_b', 'w_down', 'final_g']
COL_SHARDED = ['w_in', 'w_uq', 'w_ukv', 'w_o_attn', 'lru_conv_w', 'lru_b_a', 'lru_b_x', 'lru_lambda', 'w_up',
               'ffn_conv_w']
ROW_SHARDED = ['w_o_lru', 'w_out', 'w_down']
BIG_BF16 = ['w_in', 'w_uq', 'w_ukv', 'w_o_attn', 'w_o_lru', 'w_out', 'w_up', 'w_down']
SMALL_F32 = ['lru_conv_w', 'lru_b_a', 'lru_b_x', 'lru_lambda', 'ffn_conv_w']
SHARDED = BIG_BF16 + SMALL_F32
REPLICATED = ['c_ctx', 'b_mod', 'norm1_g', 'b_gate', 'q_norm_g', 'kv_norm_g', 'lru_conv_b', 'lru_w_a', 'lru_w_x',
              'norm2_g', 'ffn_conv_b', 'final_g']


def _cparams(sem=None):
    return pltpu.CompilerParams(dimension_semantics=sem, vmem_limit_bytes=VMEM_LIMIT)


def _pick(n, cands):
    for c in cands:
        if c <= n and n % c == 0:
            return c
    return n


def _best_div(n, mult, cap):
    best = mult
    for d in range(mult, min(n, cap) + 1, mult):
        if n % d == 0:
            best = d
    return best


MXU_DIM = 256
ROW_TILES = (1088, 1024, 544, 512, 256, 128, 64, 32, 16, 8)
LANE_TILES = (2816, 1792, 1536, 1280, 1024, 768, 512, 256, 1408, 896, 640, 384, 128)
DEPTH_ROW_TILES = (2176, 2048, 1024, 512, 256, 1088, 128, 64, 32, 16, 8)
MATMUL_VMEM_BUDGET = 40 * 1024 * 1024
MXU_FILL_OK = 0.9


def _my_pos():
    return lax.axis_index("x"), lax.axis_index("y"), lax.axis_index("c")


def _my_index():
    x, y, c = _my_pos()
    return 4 * x + 2 * y + c


def all_gather_multi(name, shards):
    n_arr = len(shards)
    arrays = range(n_arr)

    def body(*refs):
        x_refs, out_refs = refs[:n_arr], refs[n_arr:2 * n_arr]
        send_sems, recv_sems, local_sems = refs[2 * n_arr:]
        x, y, c = _my_pos()
        me, sibling = (x, y, c), (x, y, 1 - c)
        chips = [(1 - x, y), (x, 1 - y), (1 - x, 1 - y)]

        def slot(a, px, py, pc):
            return out_refs[a].at[4 * px + 2 * py + pc]

        def copy(a, k, block, to, src=None):
            return pltpu.make_async_remote_copy(
                src_ref=slot(a, *block) if src is None else src, dst_ref=slot(a, *block),
                send_sem=send_sems.at[7 * a + k], recv_sem=recv_sems.at[7 * a + k], device_id=to,
                device_id_type=MESH)

        mine = [pltpu.make_async_copy(x_refs[a], slot(a, *me), local_sems.at[a]) for a in arrays]
        first = [copy(a, 1 + j, me, (*chip, c), src=x_refs[a]) for j, chip in enumerate(chips) for a in arrays]
        first += [copy(a, 0, me, sibling, src=x_refs[a]) for a in arrays]
        for cp in first + mine:
            cp.start()
        passed = []
        for j, chip in enumerate(chips):
            for a in arrays:
                copy(a, 1 + j, (*chip, c), me).wait_recv()
                passed.append(copy(a, 4 + j, (*chip, c), sibling))
                passed[-1].start()
        for a in arrays:
            copy(a, 0, sibling, me).wait_recv()
            for j, chip in enumerate(chips):
                copy(a, 4 + j, (*chip, 1 - c), me).wait_recv()
        for cp in first + passed:
            cp.wait_send()
        for cp in mine:
            cp.wait()

    hbm = pl.BlockSpec(memory_space=pl.ANY)
    return pl.pallas_call(
        body, name=name,
        out_shape=[jax.ShapeDtypeStruct((N_DEV,) + s.shape, s.dtype) for s in shards],
        in_specs=[hbm] * n_arr, out_specs=[hbm] * n_arr,
        scratch_shapes=[pltpu.SemaphoreType.DMA((7 * n_arr,)), pltpu.SemaphoreType.DMA((7 * n_arr,)),
                        pltpu.SemaphoreType.DMA((n_arr,))],
    )(*shards)


def all_gather(name, shard):
    return all_gather_multi(name, [shard])[0]


def all_to_all_multi(name, chunk_arrays):
    n_arr = len(chunk_arrays)
    arrays = range(n_arr)

    def body(*refs):
        x_refs, out_refs = refs[:n_arr], refs[n_arr:2 * n_arr]
        send_sems, recv_sems, local_sems = refs[2 * n_arr:]
        x, y, c = _my_pos()
        me = 4 * x + 2 * y + c
        mine = [pltpu.make_async_copy(x_refs[a].at[me], out_refs[a].at[me], local_sems.at[a]) for a in arrays]
        sends, arrivals = [], []
        for rel in (6, 4, 2, 7, 5, 3, 1):
            dx, dy, dc = (rel >> 2) & 1, (rel >> 1) & 1, rel & 1
            px, py, pc = x ^ dx, y ^ dy, c ^ dc
            peer = 4 * px + 2 * py + pc
            for a in arrays:
                k = 7 * a + rel - 1
                sends.append(pltpu.make_async_remote_copy(
                    src_ref=x_refs[a].at[peer], dst_ref=out_refs[a].at[me],
                    send_sem=send_sems.at[k], recv_sem=recv_sems.at[k],
                    device_id=(px, py, pc), device_id_type=MESH))
                arrivals.append(pltpu.make_async_remote_copy(
                    src_ref=x_refs[a].at[peer], dst_ref=out_refs[a].at[peer],
                    send_sem=send_sems.at[k], recv_sem=recv_sems.at[k],
                    device_id=(x, y, c), device_id_type=MESH))
        for cp in sends + mine:
            cp.start()
        for cp in arrivals:
            cp.wait_recv()
        for cp in sends:
            cp.wait_send()
        for cp in mine:
            cp.wait()

    hbm = pl.BlockSpec(memory_space=pl.ANY)
    return pl.pallas_call(
        body, name=name,
        out_shape=[jax.ShapeDtypeStruct(s.shape, s.dtype) for s in chunk_arrays],
        in_specs=[hbm] * n_arr, out_specs=[hbm] * n_arr,
        scratch_shapes=[pltpu.SemaphoreType.DMA((7 * n_arr,)), pltpu.SemaphoreType.DMA((7 * n_arr,)),
                        pltpu.SemaphoreType.DMA((n_arr,))],
    )(*chunk_arrays)


def _peers():
    x, y, c = _my_pos()
    out = []
    for rel in (6, 4, 2, 7, 5, 3, 1):
        px, py, pc = x ^ ((rel >> 2) & 1), y ^ ((rel >> 1) & 1), c ^ (rel & 1)
        out.append((rel - 1, (px, py, pc), 4 * px + 2 * py + pc))
    return out


def _exchange_copies(mode, src_refs, land_refs, send_sems, recv_sems):
    x, y, c = _my_pos()
    me = 4 * x + 2 * y + c
    sends, arrivals = [], []
    for k, peer_pos, peer in _peers():
        for a, (src, land) in enumerate(zip(src_refs, land_refs)):
            piece = src.at[peer] if mode == 'scatter' else src
            sems = dict(send_sem=send_sems[a].at[k], recv_sem=recv_sems[a].at[k], device_id_type=MESH)
            sends.append(pltpu.make_async_remote_copy(src_ref=piece, dst_ref=land.at[me], device_id=peer_pos, **sems))
            arrivals.append(pltpu.make_async_remote_copy(src_ref=piece, dst_ref=land.at[peer], device_id=(x, y, c), **sems))
    return sends, arrivals


_HBM = pl.BlockSpec(memory_space=pltpu.HBM)
_SEM = pl.BlockSpec(memory_space=pltpu.SEMAPHORE)


def exchange_start(name, mode, arrays, after=()):
    n_arr, n_after = len(arrays), len(after)
    land_shapes = [a.shape if mode == 'scatter' else (N_DEV,) + a.shape for a in arrays]

    def body(*refs):
        src_refs, land_refs = refs[:n_arr], refs[n_arr:2 * n_arr]
        refs = refs[n_after:]
        send_sems, recv_sems = refs[2 * n_arr:3 * n_arr], refs[3 * n_arr:4 * n_arr]
        sends, _ = _exchange_copies(mode, src_refs, land_refs, send_sems, recv_sems)
        for cp in sends:
            cp.start()
        token = refs[-1]
        token[...] = jnp.zeros_like(token)

    sem = pltpu.SemaphoreType.DMA((N_DEV - 1,))
    res = pl.pallas_call(
        body, name=name,
        out_shape=[sem] * (2 * n_arr) + [pltpu.HBM(a.shape, a.dtype) for a in arrays]
        + [pltpu.HBM(s, a.dtype) for s, a in zip(land_shapes, arrays)] + [jax.ShapeDtypeStruct((8, 128), F32)],
        in_specs=[_HBM] * (2 * n_arr) + [pl.BlockSpec(memory_space=pl.ANY)] * n_after,
        out_specs=[_SEM] * (2 * n_arr) + [_HBM] * (2 * n_arr) + [pl.BlockSpec(memory_space=pltpu.VMEM)],
        input_output_aliases={i: 2 * n_arr + i for i in range(2 * n_arr)},
        compiler_params=pltpu.CompilerParams(has_side_effects=pltpu.SideEffectType.DATAFLOW_SIDE_EFFECTING),
    )(*[pltpu.with_memory_space_constraint(a, pltpu.HBM) for a in arrays],
      *[pltpu.with_memory_space_constraint(lax.empty(s, a.dtype), pltpu.HBM) for s, a in zip(land_shapes, arrays)],
      *after)
    return (res[:n_arr], res[n_arr:2 * n_arr], res[2 * n_arr:3 * n_arr], res[3 * n_arr:4 * n_arr]), res[-1]


def exchange_wait(name, mode, started, after):
    send_sems, recv_sems, thru, land = started
    n_arr = len(thru)

    def body(*refs):
        src_refs, land_refs = refs[:n_arr], refs[n_arr:2 * n_arr]
        s_sems, r_sems = refs[2 * n_arr:3 * n_arr], refs[3 * n_arr:4 * n_arr]
        sends, arrivals = _exchange_copies(mode, src_refs, land_refs, s_sems, r_sems)
        for cp in sends:
            cp.wait_send()
        for cp in arrivals:
            cp.wait_recv()

    res = pl.pallas_call(
        body, name=name,
        out_shape=[pltpu.HBM(a.shape, a.dtype) for a in thru] + [pltpu.HBM(a.shape, a.dtype) for a in land],
        in_specs=[_HBM] * (2 * n_arr) + [_SEM] * (2 * n_arr) + [pl.BlockSpec(memory_space=pl.ANY)],
        out_specs=[_HBM] * (2 * n_arr),
        input_output_aliases={i: i for i in range(2 * n_arr)},
        compiler_params=pltpu.CompilerParams(has_side_effects=pltpu.SideEffectType.DATAFLOW_SIDE_EFFECTING),
    )(*thru, *land, *send_sems, *recv_sems, after)
    return res[n_arr:]


def _sum_with_own(slot_ref, own_ref):
    x, y, c = _my_pos()
    me = 4 * x + 2 * y + c
    acc = None
    for p in range(N_DEV):
        v = jnp.where(me == p, own_ref[0], slot_ref[p]).astype(F32)
        acc = v if acc is None else acc + v
    return acc


def reduce_slots(name, slots, own, step=None):
    _, r, ccols = slots.shape
    tc = _pick(ccols, (256, 128))
    c1 = 1.0 - ADAM_B1 ** ADAM_STEP
    c2 = 1.0 - ADAM_B2 ** ADAM_STEP

    def body(s_ref, own_ref, *refs):
        g = _sum_with_own(s_ref, own_ref)
        if step is None:
            refs[0][...] = g
            return
        w_ref, m_ref, v_ref, g_ref, d_ref, nm_ref, nv_ref = refs
        nm = ADAM_B1 * m_ref[...] + (1.0 - ADAM_B1) * g
        nv = ADAM_B2 * v_ref[...] + (1.0 - ADAM_B2) * (g * g)
        g_ref[...] = g
        d_ref[...] = -ADAM_LR * ((nm / c1) / (jnp.sqrt(nv / c2) + ADAM_EPS) + ADAM_WD * w_ref[...])
        nm_ref[...] = nm
        nv_ref[...] = nv

    col = pl.BlockSpec((r, tc), lambda j: (0, j))
    n_out = 1 if step is None else 4
    res = pl.pallas_call(
        body, name=name, grid=(ccols // tc,),
        out_shape=[jax.ShapeDtypeStruct((r, ccols), F32)] * n_out,
        in_specs=[pl.BlockSpec((N_DEV, r, tc), lambda j: (0, 0, j)), pl.BlockSpec((1, r, tc), lambda j: (0, 0, j))]
        + [col] * (0 if step is None else 3),
        out_specs=[col] * n_out,
        compiler_params=_cparams(("parallel",)),
    )(slots, own, *(step or ()))
    return res[0] if step is None else res


def sum_slots(name, slots):
    _, r, ccols = slots.shape
    tc = _pick(ccols, (256, 128))

    def body(s_ref, o_ref):
        acc = s_ref[0].astype(F32)
        for p in range(1, N_DEV):
            acc = acc + s_ref[p].astype(F32)
        o_ref[...] = acc

    return pl.pallas_call(
        body, name=name, grid=(ccols // tc,),
        out_shape=jax.ShapeDtypeStruct((r, ccols), F32),
        in_specs=[pl.BlockSpec((N_DEV, r, tc), lambda j: (0, 0, j))],
        out_specs=pl.BlockSpec((r, tc), lambda j: (0, j)),
        compiler_params=_cparams(("parallel",)),
    )(slots)


def _mxu_fill(t):
    return t / (-(-t // MXU_DIM) * MXU_DIM)


def _matmul_tiles(mode, m_extent, n, k_extent, k_total, itemsizes):
    a_bytes, b_bytes, o_bytes = itemsizes
    m_cands = [c for c in (LANE_TILES if mode == 'tn' else ROW_TILES) if m_extent % c == 0] or [m_extent]
    k_cands = [c for c in (DEPTH_ROW_TILES if mode == 'tn' else LANE_TILES) if k_extent % c == 0] or [k_extent]
    n_cands = [c for c in LANE_TILES if n % c == 0] or [n]
    best = None
    for tm in m_cands:
        for tk in k_cands:
            for tn in n_cands:
                f32_tiles = 2 if k_total // tk > 1 else 1
                vmem = 2 * (tm * tk * a_bytes + tk * tn * b_bytes + tm * tn * o_bytes) + tm * tn * 4 * f32_tiles
                if vmem > MATMUL_VMEM_BUDGET:
                    continue
                key = (_mxu_fill(tn) * _mxu_fill(tk) >= MXU_FILL_OK, tm * tn * tk)
                if best is None or key > best[0]:
                    best = (key, (tm, tn, tk))
    assert best is not None, (mode, m_extent, n, k_extent)
    return best[1]


def matmul(name, a, b, mode, out_dtype, after=()):
    after = [t for t in after if t is not None]
    pieces, a_rows, a_cols = (1,) + a.shape if a.ndim == 2 else a.shape
    if mode == 'nn':
        (m, k), (k2, n) = (a_rows, pieces * a_cols), b.shape
    elif mode == 'nt':
        (m, k), (n, k2) = (a_rows, pieces * a_cols), b.shape
    else:
        (k, m), (k2, n) = (a_rows, pieces * a_cols), b.shape
    assert k == k2, (name, a.shape, b.shape, mode)
    tm, tn, tk = _matmul_tiles(mode, a_cols if mode == 'tn' else m, n, k if mode == 'tn' else a_cols, k,
                               (a.dtype.itemsize, b.dtype.itemsize, jnp.dtype(out_dtype).itemsize))
    nk = k // tk
    per_piece = a_cols // (tm if mode == 'tn' else tk)
    if a.ndim == 2:
        a_block = lambda rows, cols, at: pl.BlockSpec((rows, cols), at)
    else:
        a_block = lambda rows, cols, at: pl.BlockSpec(
            (None, rows, cols), lambda i, j, kk: (at(i, j, kk)[1] // per_piece, at(i, j, kk)[0],
                                                  at(i, j, kk)[1] % per_piece))
    if mode == 'nn':
        a_spec = a_block(tm, tk, lambda i, j, kk: (i, kk))
        b_spec = pl.BlockSpec((tk, tn), lambda i, j, kk: (kk, j))
        dn = (((1,), (0,)), ((), ()))
    elif mode == 'nt':
        a_spec = a_block(tm, tk, lambda i, j, kk: (i, kk))
        b_spec = pl.BlockSpec((tn, tk), lambda i, j, kk: (j, kk))
        dn = (((1,), (1,)), ((), ()))
    else:
        a_spec = a_block(tk, tm, lambda i, j, kk: (kk, i))
        b_spec = pl.BlockSpec((tk, tn), lambda i, j, kk: (kk, j))
        dn = (((0,), (0,)), ((), ()))

    def product(a_ref, b_ref):
        return lax.dot_general(a_ref[...].astype(BF16), b_ref[...].astype(BF16), dn, preferred_element_type=F32)

    n_after = len(after)

    def body_one(a_ref, b_ref, *rest):
        o_ref = rest[n_after]
        o_ref[...] = product(a_ref, b_ref).astype(o_ref.dtype)

    def body(a_ref, b_ref, *rest):
        o_ref, acc_ref = rest[n_after:]
        kk = pl.program_id(2)

        @pl.when(kk == 0)
        def _():
            acc_ref[...] = jnp.zeros_like(acc_ref)

        acc_ref[...] += product(a_ref, b_ref)

        @pl.when(kk == nk - 1)
        def _():
            o_ref[...] = acc_ref[...].astype(o_ref.dtype)

    return pl.pallas_call(
        body_one if nk == 1 else body, name=name, grid=(m // tm, n // tn, nk),
        out_shape=jax.ShapeDtypeStruct((m, n), out_dtype),
        in_specs=[a_spec, b_spec] + [pl.BlockSpec(memory_space=pl.ANY)] * n_after,
        out_specs=pl.BlockSpec((tm, tn), lambda i, j, kk: (i, j)),
        scratch_shapes=[] if nk == 1 else [pltpu.VMEM((tm, tn), F32)],
        compiler_params=_cparams(("parallel", "parallel", "arbitrary")),
    )(a, b, *after)


def rowwise(name, fn, rows, params, out_rows, out_accs, n_rows, t_lat, tm):
    nb, nbl = n_rows // tm, t_lat // tm
    in_specs, piece_counts = [], []
    operands = []
    for arr, off, width, *kind in rows:
        g = math.gcd(off, width) if off else width
        assert g % 128 == 0 or (off == 0 and width == arr.shape[1]), (name, off, width)
        cnt = width // g
        last = arr.shape[0] // tm - 1
        clamp = arr.shape[0] < n_rows
        for p in range(cnt):
            cb = off // g + p
            if kind == ['ctx']:
                in_specs.append(pl.BlockSpec(
                    (tm, g), lambda i, cb=cb, last=last: (jnp.clip(i - nbl, 0, last), cb)))
            elif clamp:
                in_specs.append(pl.BlockSpec((tm, g), lambda i, cb=cb, last=last: (jnp.minimum(i, last), cb)))
            else:
                in_specs.append(pl.BlockSpec((tm, g), lambda i, cb=cb: (i, cb)))
            operands.append(arr)
        piece_counts.append(cnt)
    for p in params:
        in_specs.append(pl.BlockSpec(p.shape, lambda i, nd=p.ndim: (0,) * nd))
        operands.append(p)
    n_in = sum(piece_counts)
    n_par = len(params)
    n_or = len(out_rows)
    lat_only = [kind == ['lat'] for _, _, *kind in out_rows]
    out_shape = [jax.ShapeDtypeStruct((t_lat if lat else n_rows, w), dt)
                 for (w, dt, *_), lat in zip(out_rows, lat_only)]
    out_shape += [jax.ShapeDtypeStruct(s, F32) for s in out_accs]
    out_specs = [pl.BlockSpec((tm, w), (lambda i: (jnp.minimum(i, nbl - 1), 0)) if lat else (lambda i: (i, 0)))
                 for (w, *_), lat in zip(out_rows, lat_only)]
    out_specs += [pl.BlockSpec(s, lambda i, nd=len(s): (0,) * nd) for s in out_accs]

    def body(*refs):
        in_refs, par_refs = refs[:n_in], refs[n_in:n_in + n_par]
        orow_refs = refs[n_in + n_par:n_in + n_par + n_or]
        oacc_refs = refs[n_in + n_par + n_or:]
        i = pl.program_id(0)
        tiles, at = [], 0
        for cnt in piece_counts:
            parts = [in_refs[at + p][...].astype(F32) for p in range(cnt)]
            tiles.append(parts[0] if cnt == 1 else jnp.concatenate(parts, axis=1))
            at += cnt
        is_ctx = i * tm >= t_lat
        outs, accs = fn(is_ctx, tiles, [p[...] for p in par_refs])
        for o_ref, o, lat in zip(orow_refs, outs, lat_only):
            if lat:
                @pl.when(jnp.logical_not(is_ctx))
                def _(o_ref=o_ref, o=o):
                    o_ref[...] = o.astype(o_ref.dtype)
            else:
                o_ref[...] = o.astype(o_ref.dtype)
        if oacc_refs:
            @pl.when(i == 0)
            def _():
                for a_ref in oacc_refs:
                    a_ref[...] = jnp.zeros_like(a_ref)
            for a_ref, a in zip(oacc_refs, accs):
                a_ref[...] += a.astype(F32)

    res = pl.pallas_call(
        body, name=name, grid=(nb,),
        out_shape=out_shape, in_specs=in_specs, out_specs=out_specs,
        compiler_params=_cparams(("arbitrary",)),
    )(*operands)
    return res[:n_or], res[n_or:]


def _rms(x, g):
    return x * lax.rsqrt(jnp.mean(x * x, axis=-1, keepdims=True) + EPS) * g


def _norm_mod(x, g, sc, sh):
    return _rms(x, g) * (1.0 + sc) + sh


def _sigmoid(x):
    return 0.5 * jnp.tanh(0.5 * x) + 0.5


def _silu(x):
    return x * _sigmoid(x)


def _gelu(x):
    return 0.5 * x * (1.0 + jnp.tanh(math.sqrt(2.0 / math.pi) * (x + 0.044715 * (x * x * x))))


def _sel(is_ctx, p):
    return jnp.where(is_ctx, p[1:2], p[0:1])


def _seg_acc(is_ctx, v):
    rows = lax.broadcasted_iota(jnp.int32, (2, v.shape[1]), 0)
    return jnp.where(rows == is_ctx.astype(jnp.int32), jnp.broadcast_to(v, (2, v.shape[1])), 0.0)


def _rsum(v):
    return jnp.sum(v, axis=0, keepdims=True)


def _shift_rows(x, o, t_lat, n):
    if o == 0:
        return x
    y = pltpu.roll(x, (-o) % n, 0)
    t = lax.broadcasted_iota(jnp.int32, x.shape, 0)
    src = t + o
    ok = (src >= 0) & (src < n) & ((src >= t_lat) == (t >= t_lat))
    return jnp.where(ok, y, 0.0)


def conv_fwd(name, xarr, col_off, width, w, b, left, n_rows, t_lat, out_dtype, cb=128):
    taps = w.shape[0]
    assert col_off % cb == 0 and width % cb == 0

    def body(x_ref, w_ref, b_ref, o_ref):
        x = x_ref[...].astype(F32)
        acc = jnp.broadcast_to(b_ref[...], x.shape)
        for k in range(taps):
            acc = acc + _shift_rows(x, k - left, t_lat, n_rows) * w_ref[k:k + 1, :]
        o_ref[...] = acc.astype(o_ref.dtype)

    return pl.pallas_call(
        body, name=name, grid=(width // cb,),
        out_shape=jax.ShapeDtypeStruct((n_rows, width), out_dtype),
        in_specs=[pl.BlockSpec((n_rows, cb), lambda j: (0, col_off // cb + j)),
                  pl.BlockSpec((taps, cb), lambda j: (0, j)),
                  pl.BlockSpec((1, cb), lambda j: (0, j))],
        out_specs=pl.BlockSpec((n_rows, cb), lambda j: (0, j)),
        compiler_params=_cparams(("parallel",)),
    )(xarr, w, b)


def conv_bwd(name, dout, xarr, col_off, width, w, left, n_rows, t_lat, cb=128):
    taps = w.shape[0]

    def body(d_ref, x_ref, w_ref, dx_ref, dw_ref, db_ref):
        d = d_ref[...].astype(F32)
        x = x_ref[...].astype(F32)
        dx = jnp.zeros_like(d)
        dws = []
        for k in range(taps):
            dx = dx + _shift_rows(d, left - k, t_lat, n_rows) * w_ref[k:k + 1, :]
            dws.append(_rsum(d * _shift_rows(x, k - left, t_lat, n_rows)))
        dx_ref[...] = dx.astype(dx_ref.dtype)
        dw_ref[...] = jnp.concatenate(dws, axis=0)
        db_ref[...] = _rsum(d)

    return pl.pallas_call(
        body, name=name, grid=(width // cb,),
        out_shape=[jax.ShapeDtypeStruct((n_rows, width), BF16), jax.ShapeDtypeStruct((taps, width), F32),
                   jax.ShapeDtypeStruct((1, width), F32)],
        in_specs=[pl.BlockSpec((n_rows, cb), lambda j: (0, j)),
                  pl.BlockSpec((n_rows, cb), lambda j: (0, col_off // cb + j)),
                  pl.BlockSpec((taps, cb), lambda j: (0, j))],
        out_specs=[pl.BlockSpec((n_rows, cb), lambda j: (0, j)), pl.BlockSpec((taps, cb), lambda j: (0, j)),
                   pl.BlockSpec((1, cb), lambda j: (0, j))],
        compiler_params=_cparams(("parallel",)),
    )(dout, xarr, w)


def _ffn_conv(a, w_ref, b_ref, t_lat):
    shifted = [_shift_rows(a, k - 1, t_lat, t_lat) for k in range(3)]
    ac = jnp.broadcast_to(b_ref[...], a.shape)
    for k in range(3):
        ac = ac + shifted[k] * w_ref[k:k + 1, :]
    return ac, shifted


def ffn_mix_fwd(u, w, b, t_lat, cb=128):
    nblk = FFN // cb

    def body(a_ref, g_ref, w_ref, b_ref, f_ref):
        ac, _ = _ffn_conv(a_ref[...].astype(F32), w_ref, b_ref, t_lat)
        f_ref[...] = (_silu(ac) * g_ref[...].astype(F32)).astype(f_ref.dtype)

    col = lambda shape, off=0: pl.BlockSpec(shape, lambda j: (0, off + j))
    return pl.pallas_call(
        body, name="ffn_mix", grid=(nblk,),
        out_shape=jax.ShapeDtypeStruct((t_lat, FFN), BF16),
        in_specs=[col((t_lat, cb)), col((t_lat, cb), nblk), col((3, cb)), col((1, cb))],
        out_specs=col((t_lat, cb)),
        compiler_params=_cparams(("parallel",)),
    )(u, u, w, b)


def ffn_mix_bwd(u, df, w, b, t_lat, cb=128):
    nblk = FFN // cb

    def body(a_ref, g_ref, df_ref, w_ref, b_ref, du_ref, dw_ref, db_ref):
        ac, shifted = _ffn_conv(a_ref[...].astype(F32), w_ref, b_ref, t_lat)
        d = df_ref[...].astype(F32)
        s = _sigmoid(ac)
        du_ref[1] = (d * (ac * s)).astype(du_ref.dtype)
        dac = d * g_ref[...].astype(F32) * (s * (1.0 + ac * (1.0 - s)))
        da = jnp.zeros_like(dac)
        for k in range(3):
            da = da + _shift_rows(dac, 1 - k, t_lat, t_lat) * w_ref[k:k + 1, :]
        du_ref[0] = da.astype(du_ref.dtype)
        dw_ref[...] = jnp.concatenate([_rsum(dac * shifted[k]) for k in range(3)], axis=0)
        db_ref[...] = _rsum(dac)

    col = lambda shape, off=0: pl.BlockSpec(shape, lambda j: (0, off + j))
    return pl.pallas_call(
        body, name="ffn_mix_bwd", grid=(nblk,),
        out_shape=[jax.ShapeDtypeStruct((2, t_lat, FFN), BF16),
                   jax.ShapeDtypeStruct((3, FFN), F32), jax.ShapeDtypeStruct((1, FFN), F32)],
        in_specs=[col((t_lat, cb)), col((t_lat, cb), nblk), col((t_lat, cb)), col((3, cb)), col((1, cb))],
        out_specs=[pl.BlockSpec((2, t_lat, cb), lambda j: (0, 0, j)), col((3, cb)), col((1, cb))],
        compiler_params=_cparams(("parallel",)),
    )(u, u, df, w, b)


def _chunk_order(direction, nb, nbl):
    if direction == 'f':
        return lambda s: ((s + nbl) % nb, 0)
    return lambda s: (nb - 1 - s, 0)


def _adjoint_order(direction, nb, nbl):
    if direction == 'f':
        return lambda s: ((nb - 1 - s + nbl) % nb, 0)
    return lambda s: (s, 0)


SUBLANES = 8


def _chunk_scan(a, b, carry, rev):
    tc = a.shape[0]
    row = lax.broadcasted_iota(jnp.int32, a.shape, 0)
    in_tile = jnp.bitwise_and(row, SUBLANES - 1)
    for k in (1, 2, 4):
        shift = tc - k if rev else k
        edge = in_tile >= SUBLANES - k if rev else in_tile < k
        b = jnp.where(edge, b, a * pltpu.roll(b, shift, 0) + b)
        a = jnp.where(edge, a, a * pltpu.roll(a, shift, 0))
    nt = tc // SUBLANES
    hs = [None] * nt
    c = carry
    for kt in range(nt):
        k = nt - 1 - kt if rev else kt
        h = b[k * SUBLANES:(k + 1) * SUBLANES] + a[k * SUBLANES:(k + 1) * SUBLANES] * c
        hs[k] = h
        c = h[0:1] if rev else h[SUBLANES - 1:SUBLANES]
    h = jnp.concatenate(hs, axis=0)
    if rev:
        return h, jnp.where(row == tc - 1, carry, pltpu.roll(h, tc - 1, 0)), c
    return h, jnp.where(row == 0, carry, pltpu.roll(h, 1, 0)), c


def scan_fwd(name, a, u, direction, n_rows, t_lat):
    w = a.shape[1]
    tc = _pick(math.gcd(t_lat, n_rows), (256, 128))
    nb, nbl = n_rows // tc, t_lat // tc
    order = _chunk_order(direction, nb, nbl)
    rev = direction == 'b'

    def body(a_ref, u_ref, h_ref, hp_ref, carry):
        @pl.when(pl.program_id(0) == 0)
        def _():
            carry[...] = jnp.zeros_like(carry)

        h_ref[...], hp_ref[...], carry[...] = _chunk_scan(a_ref[...], u_ref[...], carry[...], rev)

    spec = pl.BlockSpec((tc, w), order)
    return pl.pallas_call(
        body, name=name, grid=(nb,),
        out_shape=[jax.ShapeDtypeStruct((n_rows, w), F32)] * 2,
        in_specs=[spec, spec], out_specs=[spec, spec],
        scratch_shapes=[pltpu.VMEM((1, w), F32)],
        compiler_params=_cparams(("arbitrary",)),
    )(a, u)


def scan_adj(name, a, dh, hprev, direction, n_rows, t_lat):
    w = a.shape[1]
    tc = _pick(math.gcd(t_lat, n_rows), (256, 128))
    nb, nbl = n_rows // tc, t_lat // tc
    order = _adjoint_order(direction, nb, nbl)
    rev = direction == 'f'

    def dh_order(s):
        c, _ = order(s)
        return (jnp.minimum(c, nbl - 1), 0)

    def body(a_ref, dh_ref, hp_ref, du_ref, da_ref, carry):
        s = pl.program_id(0)

        @pl.when(s == 0)
        def _():
            carry[...] = jnp.zeros_like(carry)

        chunk, _ = order(s)
        live = (chunk < nbl).astype(F32)

        av = a_ref[...]
        dv = dh_ref[...] * live
        _, c_next, carry[...] = _chunk_scan(av, av * dv, carry[...], rev)
        lam = dv + c_next
        du_ref[...] = lam
        da_ref[...] = lam * hp_ref[...]

    spec = pl.BlockSpec((tc, w), order)
    return pl.pallas_call(
        body, name=name, grid=(nb,),
        out_shape=[jax.ShapeDtypeStruct((n_rows, w), F32)] * 2,
        in_specs=[spec, pl.BlockSpec((tc, w), dh_order), spec], out_specs=[spec, spec],
        scratch_shapes=[pltpu.VMEM((1, w), F32)],
        compiler_params=_cparams(("arbitrary",)),
    )(a, dh, hprev)


def _one_minus_a_squared(log_a, a):
    return (1.0 + a * a) * jnp.tanh(-log_a)


def _gate_elem(pre_r, pre_i, xc, b_a, b_x, sp):
    r = _sigmoid(pre_r + b_a)
    i = _sigmoid(pre_i + b_x)
    log_a = (-LRU_C) * r * sp
    a = jnp.exp(log_a)
    m2 = _one_minus_a_squared(log_a, a)
    mult = jnp.where(m2 > 0.0, m2 * lax.rsqrt(m2), 0.0)
    return a, mult * (i * xc)


def _gate_elem_bwd(pre_r, pre_i, xc, b_a, b_x, sp, da, du):
    r = _sigmoid(pre_r + b_a)
    i = _sigmoid(pre_i + b_x)
    log_a = (-LRU_C) * r * sp
    a = jnp.exp(log_a)
    m2 = _one_minus_a_squared(log_a, a)
    inv_mult = lax.rsqrt(m2)
    g = du * (m2 * inv_mult)
    d_mult = du * (i * xc)
    d_log_a = (da - d_mult * a * inv_mult) * a
    d_pre_r = d_log_a * ((-LRU_C) * sp) * (r * (1.0 - r))
    d_pre_i = g * xc * (i * (1.0 - i))
    return d_pre_r, d_pre_i, g * i, _rsum(d_log_a * ((-LRU_C) * r))


def _blockdiag(xb16, w_ref_val, d):
    outs = []
    for n in range(LRU_BLOCKS):
        outs.append(jnp.dot(xb16[:, n * LRU_BW:(n + 1) * LRU_BW], w_ref_val[d * LRU_BLOCKS + n],
                            preferred_element_type=F32))
    return jnp.concatenate(outs, axis=1)


def gates_fwd(xc, w_a, w_x, b_a, b_x, sp, n_rows, t_lat, tm):
    def fn(is_ctx, rows, params):
        (x,), (wa, wx, ba, bx, spv) = rows, params
        xb16 = x.astype(BF16)
        outs = []
        for d in range(2):
            a, u = _gate_elem(_blockdiag(xb16, wa, d), _blockdiag(xb16, wx, d), x,
                              ba[d:d + 1], bx[d:d + 1], spv[d:d + 1])
            outs += [a, u]
        return outs, []

    (a_f, u_f, a_b, u_b), _ = rowwise("gates_fwd", fn, [(xc, 0, LRU_W)], [w_a, w_x, b_a, b_x, sp],
                                      [(LRU_W, F32)] * 4, [], n_rows, t_lat, tm)
    return a_f, u_f, a_b, u_b


def gates_bwd(xc, da_f, du_f, da_b, du_b, w_a, w_x, b_a, b_x, sp, n_rows, t_lat, tm):
    def fn(is_ctx, rows, params):
        (x, daf, duf, dab, dub), (wa, wx, ba, bx, spv) = rows, params
        xb16 = x.astype(BF16)
        dxc = jnp.zeros_like(x)
        dwa, dwx, dba, dbx, dsp = [], [], [], [], []
        for d, (da, du) in enumerate(((daf, duf), (dab, dub))):
            dpr, dpi, dx_e, dsp_d = _gate_elem_bwd(_blockdiag(xb16, wa, d), _blockdiag(xb16, wx, d), x,
                                                   ba[d:d + 1], bx[d:d + 1], spv[d:d + 1], da, du)
            dba_d, dbx_d = _rsum(dpr), _rsum(dpi)
            dxc = dxc + dx_e
            dpr16, dpi16 = dpr.astype(BF16), dpi.astype(BF16)
            back = []
            for n in range(LRU_BLOCKS):
                sl = slice(n * LRU_BW, (n + 1) * LRU_BW)
                nt_dims = (((1,), (1,)), ((), ()))
                back.append(lax.dot_general(dpr16[:, sl], wa[d * LRU_BLOCKS + n], nt_dims, preferred_element_type=F32)
                            + lax.dot_general(dpi16[:, sl], wx[d * LRU_BLOCKS + n], nt_dims,
                                              preferred_element_type=F32))
                tn_dims = (((0,), (0,)), ((), ()))
                dwa.append(lax.dot_general(xb16[:, sl], dpr16[:, sl], tn_dims, preferred_element_type=F32)[None])
                dwx.append(lax.dot_general(xb16[:, sl], dpi16[:, sl], tn_dims, preferred_element_type=F32)[None])
            dxc = dxc + jnp.concatenate(back, axis=1)
            dba.append(dba_d)
            dbx.append(dbx_d)
            dsp.append(dsp_d)
        cat0 = lambda xs: jnp.concatenate(xs, axis=0)
        return [dxc], [cat0(dwa), cat0(dwx), cat0(dba), cat0(dbx), cat0(dsp)]

    (dxc,), accs = rowwise("gates_bwd", fn,
                           [(xc, 0, LRU_W), (da_f, 0, LRU_W), (du_f, 0, LRU_W), (da_b, 0, LRU_W), (du_b, 0, LRU_W)],
                           [w_a, w_x, b_a, b_x, sp], [(LRU_W, F32)],
                           [(2 * LRU_BLOCKS, LRU_BW, LRU_BW)] * 2 + [(2, LRU_W)] * 3, n_rows, t_lat, tm)
    return dxc, accs


def _rope_tables(t_lat, n_rows):
    rows = t_lat // GRID_W
    row_ids = jnp.repeat(jnp.arange(rows), GRID_W).astype(F32)
    col_ids = jnp.tile(jnp.arange(GRID_W), rows).astype(F32)
    axis_dim = QK_ROPE // 2
    inv = 1.0 / (ROPE_BASE ** (jnp.arange(0, axis_dim, 2, dtype=F32) / axis_dim))
    ang = jnp.concatenate([row_ids[:, None] * inv, col_ids[:, None] * inv], axis=-1)
    cos, sin = jnp.cos(ang), jnp.sin(ang)
    half = QK_ROPE // 2
    ones, zeros = jnp.ones((t_lat, QK_NOPE), F32), jnp.zeros((t_lat, QK_NOPE), F32)
    pad1, pad0 = jnp.ones((t_lat, HEAD_PAD - QK_DIM), F32), jnp.zeros((t_lat, HEAD_PAD - QK_DIM), F32)
    zh = jnp.zeros((t_lat, half), F32)
    c_tab = jnp.concatenate([ones, cos, cos, pad1], axis=1)
    s1 = jnp.concatenate([zeros, -sin, zh, pad0], axis=1)
    s2 = jnp.concatenate([zeros, zh, sin, pad0], axis=1)
    n_ctx = n_rows - t_lat
    c_tab = jnp.concatenate([c_tab, jnp.ones((n_ctx, HEAD_PAD), F32)], axis=0)
    s1 = jnp.concatenate([s1, jnp.zeros((n_ctx, HEAD_PAD), F32)], axis=0)
    s2 = jnp.concatenate([s2, jnp.zeros((n_ctx, HEAD_PAD), F32)], axis=0)
    return c_tab, s1, s2


def _rope(x, c, s1, s2):
    half = QK_ROPE // 2
    return x * c + pltpu.roll(x, HEAD_PAD - half, 1) * s1 + pltpu.roll(x, half, 1) * s2


def _rope_t(dy, c, s1, s2):
    half = QK_ROPE // 2
    return dy * c + pltpu.roll(dy * s1, half, 1) + pltpu.roll(dy * s2, HEAD_PAD - half, 1)


def _heads(x):
    return [x[:, h * HEAD_PAD:(h + 1) * HEAD_PAD] for h in range(N_HEADS)]


Q_SCALE = QK_DIM ** -0.5 * math.log2(math.e)

def attn_fwd(q, k, v, t_lat, n_rows, tq):
    def body(q_ref, k_ref, v_ref, o_ref, lse_ref):
        s = lax.dot_general(q_ref[...], k_ref[...], (((1,), (1,)), ((), ())), preferred_element_type=F32)
        m = jnp.max(s, axis=-1, keepdims=True)
        p = jnp.exp2(s - m)
        l = jnp.sum(p, axis=-1, keepdims=True)
        o = jnp.dot(p.astype(BF16), v_ref[...], preferred_element_type=F32) / l
        o_ref[...] = o.astype(o_ref.dtype)
        lse_ref[...] = jnp.broadcast_to(m + jnp.log2(l), lse_ref.shape)

    qspec = pl.BlockSpec((tq, HEAD_PAD), lambda h, i: (i, h))
    kspec = pl.BlockSpec((n_rows, HEAD_PAD), lambda h, i: (0, h))
    return pl.pallas_call(
        body, name="attn_fwd", grid=(N_HEADS, t_lat // tq),
        out_shape=[jax.ShapeDtypeStruct((t_lat, N_HEADS * HEAD_PAD), BF16),
                   jax.ShapeDtypeStruct((t_lat, N_HEADS * HEAD_PAD), F32)],
        in_specs=[qspec, kspec, kspec], out_specs=[qspec, qspec],
        compiler_params=_cparams(("parallel", "arbitrary")),
    )(q, k, v)


def attn_bwd(q, k, v, o, do, lse, t_lat, n_rows, tq):
    scale = QK_DIM ** -0.5
    nq = t_lat // tq
    nt = (((1,), (1,)), ((), ()))
    tn = (((0,), (0,)), ((), ()))

    def body(q_ref, k_ref, v_ref, o_ref, do_ref, lse_ref, dq_ref, dk_ref, dv_ref):
        @pl.when(pl.program_id(1) == 0)
        def _():
            dk_ref[...] = jnp.zeros_like(dk_ref)
            dv_ref[...] = jnp.zeros_like(dv_ref)

        qv, kv, vv, dov = q_ref[...], k_ref[...], v_ref[...], do_ref[...]
        s = lax.dot_general(qv, kv, nt, preferred_element_type=F32)
        p = jnp.exp2(s - lse_ref[:, 0:1])
        dv_ref[...] += lax.dot_general(p.astype(BF16), dov, tn, preferred_element_type=F32)
        dp = lax.dot_general(dov, vv, nt, preferred_element_type=F32)
        delta = jnp.sum(dov.astype(F32) * o_ref[...].astype(F32), axis=-1, keepdims=True)
        ds = (p * (dp - delta)).astype(BF16)
        dq_ref[...] = (jnp.dot(ds, kv, preferred_element_type=F32) * scale).astype(dq_ref.dtype)
        dk_ref[...] += lax.dot_general(ds, qv, tn, preferred_element_type=F32)

        @pl.when(pl.program_id(1) == nq - 1)
        def _():
            dk_ref[...] = dk_ref[...] * (scale / Q_SCALE)

    qspec = pl.BlockSpec((tq, HEAD_PAD), lambda h, i: (i, h))
    kspec = pl.BlockSpec((n_rows, HEAD_PAD), lambda h, i: (0, h))
    return pl.pallas_call(
        body, name="attn_bwd", grid=(N_HEADS, t_lat // tq),
        out_shape=[jax.ShapeDtypeStruct((t_lat, N_HEADS * HEAD_PAD), BF16),
                   jax.ShapeDtypeStruct((n_rows, N_HEADS * HEAD_PAD), F32),
                   jax.ShapeDtypeStruct((n_rows, N_HEADS * HEAD_PAD), F32)],
        in_specs=[qspec, kspec, kspec, qspec, qspec, qspec], out_specs=[qspec, kspec, kspec],
        compiler_params=_cparams(("parallel", "arbitrary")),
    )(q, k, v, o, do, lse)


def adamw(name, w, g, m, v):
    r, ccols = w.shape
    if r % 8 == 0:
        tr, tcol = _best_div(r, 8, max(8, 262144 // ccols)), ccols
    else:
        tr, tcol = r, _pick(ccols, (256, 128))
    c1 = 1.0 - ADAM_B1 ** ADAM_STEP
    c2 = 1.0 - ADAM_B2 ** ADAM_STEP

    def body(w_ref, g_ref, m_ref, v_ref, d_ref, nm_ref, nv_ref):
        gv = g_ref[...]
        nm = ADAM_B1 * m_ref[...] + (1.0 - ADAM_B1) * gv
        nv = ADAM_B2 * v_ref[...] + (1.0 - ADAM_B2) * (gv * gv)
        d_ref[...] = -ADAM_LR * ((nm / c1) / (jnp.sqrt(nv / c2) + ADAM_EPS) + ADAM_WD * w_ref[...])
        nm_ref[...] = nm
        nv_ref[...] = nv

    spec = pl.BlockSpec((tr, tcol), lambda i, j: (i, j))
    return pl.pallas_call(
        body, name=name, grid=(r // tr, ccols // tcol),
        out_shape=[jax.ShapeDtypeStruct((r, ccols), F32)] * 3,
        in_specs=[spec] * 4, out_specs=[spec] * 3,
        compiler_params=_cparams(("parallel", "parallel")),
    )(w, g, m, v)


def adamw_many(name, ws, gs, ms, vs):
    n = len(ws)
    c1 = 1.0 - ADAM_B1 ** ADAM_STEP
    c2 = 1.0 - ADAM_B2 ** ADAM_STEP

    def body(*refs):
        for i in range(n):
            w_ref, g_ref, m_ref, v_ref = (refs[k * n + i] for k in range(4))
            d_ref, nm_ref, nv_ref = (refs[(4 + k) * n + i] for k in range(3))
            gv = g_ref[...]
            nm = ADAM_B1 * m_ref[...] + (1.0 - ADAM_B1) * gv
            nv = ADAM_B2 * v_ref[...] + (1.0 - ADAM_B2) * (gv * gv)
            d_ref[...] = -ADAM_LR * ((nm / c1) / (jnp.sqrt(nv / c2) + ADAM_EPS) + ADAM_WD * w_ref[...])
            nm_ref[...] = nm
            nv_ref[...] = nv

    vmem = pl.BlockSpec(memory_space=pltpu.VMEM)
    res = pl.pallas_call(
        body, name=name,
        out_shape=[jax.ShapeDtypeStruct(w.shape, F32) for w in ws] * 3,
        in_specs=[vmem] * (4 * n), out_specs=[vmem] * (3 * n),
        compiler_params=_cparams(),
    )(*ws, *gs, *ms, *vs)
    return [tuple(res[k * n + i] for k in range(3)) for i in range(n)]


def _flat(parts, dtype, row_mult):
    v = jnp.concatenate([p.reshape(-1).astype(dtype) for p in parts])
    quantum = row_mult * FLAT_C
    total = -(-v.shape[0] // quantum) * quantum
    return jnp.pad(v, (0, total - v.shape[0])).reshape(total // FLAT_C, FLAT_C)


def _unflat(flat, shapes):
    v = flat.reshape(-1)
    out, at = [], 0
    for s in shapes:
        n = math.prod(s)
        out.append(v[at:at + n].reshape(s))
        at += n
    return out


def _gathered_to_full(name, g):
    k = g.shape[1]
    return jnp.transpose(g, (1, 0, 2)).reshape(k, N_DEV * g.shape[2])


def _full_to_chunks(name, full):
    k, n = full.shape
    return jnp.transpose(full.reshape(k, N_DEV, n // N_DEV), (1, 0, 2)).reshape(N_DEV, -1)


def _shard_to_rb(name, w):
    return w if name in ROW_SHARDED else w.T


def _rb_to_shard(name, g):
    return g if name in ROW_SHARDED else g.T


def _rb_from_gathered(name, g):
    cols = g.shape[2]
    if name == 'w_in':
        z = lambda k: jnp.zeros((k, cols), g.dtype)
        full = g.reshape(N_DEV * g.shape[1], cols)
        return jnp.concatenate([full[:Z_KR], z(QK_NOPE), full[Z_KR:Z_KR + QK_ROPE], z(HEAD_PAD - QK_DIM),
                                full[Z_KR + QK_ROPE:]], axis=0)
    if name == 'w_uq':
        return jnp.pad(g, ((0, 0), (0, HEAD_PAD - QK_DIM), (0, 0))).reshape(N_HEADS * HEAD_PAD, cols)
    if name == 'w_ukv':
        pad = lambda t: jnp.pad(t, ((0, 0), (0, HEAD_PAD - t.shape[1]), (0, 0))).reshape(N_HEADS * HEAD_PAD, cols)
        return jnp.concatenate([pad(g[:, :QK_NOPE]), pad(g[:, QK_NOPE:])], axis=0)
    if name == 'w_o_attn':
        full = g.reshape(D, N_HEADS, V_HEAD)
        return jnp.pad(full, ((0, 0), (0, 0), (0, HEAD_PAD - V_HEAD))).reshape(D, N_HEADS * HEAD_PAD)
    return g.reshape(N_DEV * g.shape[1], cols)


def _chunks_from_rb_grad(name, g):
    cols = g.shape[1]
    if name == 'w_in':
        full = jnp.concatenate([g[:Z_KR], g[Z_KR + QK_NOPE:Z_KR + QK_DIM], g[Z_XB:]], axis=0)
        return full.reshape(N_DEV, -1, cols)
    if name == 'w_uq':
        return g.reshape(N_HEADS, HEAD_PAD, cols)[:, :QK_DIM]
    if name == 'w_ukv':
        half = N_HEADS * HEAD_PAD
        gk = g[:half].reshape(N_HEADS, HEAD_PAD, cols)[:, :QK_NOPE]
        gv = g[half:].reshape(N_HEADS, HEAD_PAD, cols)[:, :V_HEAD]
        return jnp.concatenate([gk, gv], axis=1)
    if name == 'w_o_attn':
        full = g.reshape(D, N_HEADS, HEAD_PAD)[:, :, :V_HEAD].reshape(D, N_HEADS * V_HEAD)
        return full.reshape(N_DEV, D // N_DEV, N_HEADS * V_HEAD)
    return g.reshape(N_DEV, -1, cols)


def local_step(x, ctx, target, mod_l, mod_c, wt, on_grad=None, arrive=None):
    t_lat, n_ctx = x.shape[0], ctx.shape[0]
    n = t_lat + n_ctx
    tm = _pick(math.gcd(t_lat, n), (256, 128))
    tq_fwd = _pick(t_lat, (256, 128))
    tq_bwd = _pick(t_lat, (512, 256, 128))
    row = lambda v: v.reshape(1, -1).astype(F32)
    two = lambda a, b: jnp.stack([a, b]).astype(F32)
    sh1_l, sc1_l, g1_l, sh2_l, sc2_l, g2_l = jnp.split(mod_l, 6)
    sh1_c, sc1_c = jnp.split(mod_c, 6)[:2]
    sc1, sh1 = two(sc1_l, sc1_c), two(sh1_l, sh1_c)
    g1, g2, sc2, sh2 = row(g1_l), row(g2_l), row(sc2_l), row(sh2_l)
    norm1_g, norm2_g, final_g = row(wt['norm1_g']), row(wt['norm2_g']), row(wt['final_g'])
    q_g, kv_g, b_gate = row(wt['q_norm_g']), row(wt['kv_norm_g']), row(wt['b_gate'])
    wt = dict(wt)
    pending = []

    def sent():
        tokens = list(pending)
        pending.clear()
        return tokens

    def need(names, after):
        if arrive is not None:
            got = arrive(names, after)
            if '_token' in got:
                pending.append(got.pop('_token'))
            wt.update(got)
        return [wt[n] for n in names]
    lru_w_a = wt['lru_w_a'].reshape(2 * LRU_BLOCKS, LRU_BW, LRU_BW).astype(BF16)
    lru_w_x = wt['lru_w_x'].reshape(2 * LRU_BLOCKS, LRU_BW, LRU_BW).astype(BF16)
    b_a, b_x, lam = wt['lru_b_a'], wt['lru_b_x'], wt['lru_lambda']
    sp = jnp.logaddexp(-lam, 0.0)
    c_tab, s1_tab, s2_tab = _rope_tables(t_lat, n)
    rw = functools.partial(rowwise, n_rows=n, t_lat=t_lat, tm=tm)
    rw_lat = functools.partial(rowwise, n_rows=t_lat, t_lat=t_lat, tm=tm)

    stream = [(x, 0, D), (ctx, 0, D, 'ctx')]

    def f_norm1(is_ctx, rows, params):
        (xl, xc_), (g, sc, sh) = rows, params
        return [_norm_mod(jnp.where(is_ctx, xc_, xl), g, _sel(is_ctx, sc), _sel(is_ctx, sh))], []

    (h,), _ = rw("norm1", f_norm1, stream, [norm1_g, sc1, sh1], [(D, BF16)], [])
    (w_in_t,) = need(('w_in',), h)
    z = matmul("w_in", h, w_in_t, 'nt', BF16, after=sent())
    w_uq_t, w_ukv_t, w_o_lru = need(('w_uq', 'w_ukv', 'w_o_lru'), z)

    def f_qkv_norm(is_ctx, rows, params):
        (ql, kvl), (gq, gkv) = rows, params
        return [_rms(ql, gq), _rms(kvl, gkv)], []

    (qn, kvn), _ = rw("qkv_norm", f_qkv_norm, [(z, Z_Q, Q_RANK), (z, Z_KV, KV_RANK)], [q_g, kv_g],
                      [(Q_RANK, BF16), (KV_RANK, BF16)], [])
    qp = matmul("w_uq", qn, w_uq_t, 'nt', BF16)
    kvp = matmul("w_ukv", kvn, w_ukv_t, 'nt', BF16)

    def f_rope(is_ctx, rows, params):
        qv, kk, vv, kr, c, s1, s2 = rows
        krr = _rope(kr, c, s1, s2)
        qo = jnp.concatenate([_rope(qh, c, s1, s2) for qh in _heads(qv)], axis=1) * Q_SCALE
        ko = jnp.concatenate([kh + krr for kh in _heads(kk)], axis=1)
        return [qo, ko, vv], []

    hp = N_HEADS * HEAD_PAD
    (qr, kr_, vr), _ = rw("rope", f_rope,
                          [(qp, 0, hp), (kvp, 0, hp), (kvp, hp, hp), (z, Z_KR, HEAD_PAD), (c_tab, 0, HEAD_PAD),
                           (s1_tab, 0, HEAD_PAD), (s2_tab, 0, HEAD_PAD)], [], [(hp, BF16)] * 3, [])
    attn, lse = attn_fwd(qr, kr_, vr, t_lat, n, tq_fwd)

    xc = conv_fwd("lru_conv", z, Z_XB, LRU_W, wt['lru_conv_w'], row(wt['lru_conv_b']), 2, n, t_lat, F32)
    a_f, u_f, a_b, u_b = gates_fwd(xc, lru_w_a, lru_w_x, b_a, b_x, sp, n, t_lat, tm)
    h_f, hp_f = scan_fwd("scan_f", a_f, u_f, 'f', n, t_lat)
    h_b, hp_b = scan_fwd("scan_b", a_b, u_b, 'b', n, t_lat)

    def f_lru_out(is_ctx, rows, params):
        hf, hb, yb = rows
        return [(hf + hb) * _gelu(yb)], []

    (ybin,), _ = rw_lat("lru_out", f_lru_out, [(h_f, 0, LRU_W), (h_b, 0, LRU_W), (z, Z_YB, LRU_W)], [],
                        [(LRU_W, BF16)], [])
    w_o_attn_t, w_out, w_up_t, w_down = need(('w_o_attn', 'w_out', 'w_up', 'w_down'), attn)
    y_a = matmul("w_o_attn", attn, w_o_attn_t, 'nt', BF16)
    y_b = matmul("w_o_lru", ybin, w_o_lru, 'nn', BF16)

    def _merge(ya, yb, gl, bg):
        gates = _sigmoid(gl + bg)
        return gates[:, :D] * ya + gates[:, D:] * yb

    def f_merge(is_ctx, rows, params):
        (ya, yb, gl), (bg,) = rows, params
        return [_merge(ya, yb, gl, bg)], []

    (mrg,), _ = rw_lat("merge", f_merge, [(y_a, 0, D), (y_b, 0, D), (z, Z_GL, 2 * D)], [b_gate], [(D, BF16)], [])
    o = matmul("w_out", mrg, w_out, 'nn', BF16)

    def _res_norm2(xv, ov, g1v, g, sc, sh):
        x1 = xv + g1v * ov
        return x1, _norm_mod(x1, g, sc, sh)

    def f_norm2(is_ctx, rows, params):
        (xv, ov), (g1v, g, sc, sh) = rows, params
        x1, h2v = _res_norm2(xv, ov, g1v, g, sc, sh)
        return [x1, h2v], []

    (x1, h2), _ = rw_lat("norm2", f_norm2, [(x, 0, D), (o, 0, D)], [g1, norm2_g, sc2, sh2], [(D, F32), (D, BF16)], [])
    u = matmul("w_up", h2, w_up_t, 'nt', BF16)
    f = ffn_mix_fwd(u, wt['ffn_conv_w'], row(wt['ffn_conv_b']), t_lat)
    dn = matmul("w_down", f, w_down, 'nn', BF16)

    def _tile_loss(x1v, dv, g2v, fg, tgt):
        y = _rms(x1v + g2v * dv, fg)
        e = y - tgt
        return 0.5 * jnp.sum(jnp.mean(e * e, axis=-1, keepdims=True), axis=0, keepdims=True)

    def f_final(is_ctx, rows, params):
        (x1v, dv, tgt), (g2v, fg) = rows, params
        lv, vjp = jax.vjp(lambda a, b, c, d: _tile_loss(a, b, c, d, tgt), x1v, dv, g2v, fg)
        dx2, dd, dg2, dfg = vjp(jnp.ones((1, 1), F32))
        return [dx2, dd], [dg2, dfg, jnp.broadcast_to(lv, (1, 128))]

    (dx2, dd), (dg2, dfinal_g, loss_v) = rw_lat("final", f_final, [(x1, 0, D), (dn, 0, D), (target, 0, D)],
                                                [g2, final_g], [(D, F32), (D, BF16)], [(1, D), (1, D), (1, 128)])
    loss = loss_v[0, 0]

    grads = {'final_g': dfinal_g}

    def put(name, g):
        grads[name] = g
        if on_grad is not None:
            pending.append(on_grad(name, g))
    df = matmul("d_f", dd, w_down, 'nt', BF16)
    put('w_down', matmul("g_w_down", f, dd, 'tn', BF16))

    du, grads['ffn_conv_w'], grads['ffn_conv_b'] = ffn_mix_bwd(u, df, wt['ffn_conv_w'], row(wt['ffn_conv_b']),
                                                               t_lat)
    dh2 = matmul("d_h2", du, w_up_t, 'nn', BF16, after=sent())
    put('w_up', matmul("g_w_up", du, h2, 'tn', BF16))

    def b_norm2(is_ctx, rows, params):
        (xv, ov, dh2v, dx2v), (g1v, g, sc, sh) = rows, params
        _, vjp = jax.vjp(_res_norm2, xv, ov, g1v, g, sc, sh)
        dx, do, dg1v, dg, dsc, dsh = vjp((dx2v, dh2v))
        return [dx, do], [dg1v, dg, dsc, dsh]

    (dx_res, do), (dg1, dnorm2_g, dsc2, dsh2) = rw_lat(
        "norm2_bwd", b_norm2, [(x, 0, D), (o, 0, D), (dh2, 0, D), (dx2, 0, D)], [g1, norm2_g, sc2, sh2],
        [(D, F32), (D, BF16)], [(1, D)] * 4)
    grads['norm2_g'] = dnorm2_g
    dmrg = matmul("d_merge", do, w_out, 'nt', BF16, after=sent())
    put('w_out', matmul("g_w_out", mrg, do, 'tn', BF16))

    def b_merge(is_ctx, rows, params):
        (ya, yb, gl, dm), (bg,) = rows, params
        _, vjp = jax.vjp(_merge, ya, yb, gl, bg)
        dya, dyb, dgl, dbg = vjp(dm)
        return [dya, dyb, dgl], [dbg]

    (dy_a, dy_b, dgl), (grads['b_gate'],) = rw_lat(
        "merge_bwd", b_merge, [(y_a, 0, D), (y_b, 0, D), (z, Z_GL, 2 * D), (dmrg, 0, D)], [b_gate],
        [(D, BF16), (D, BF16), (2 * D, BF16)], [(1, 2 * D)])
    dattn = matmul("d_attn", dy_a, w_o_attn_t, 'nn', BF16, after=sent())
    put('w_o_attn', matmul("g_w_o_attn", dy_a, attn, 'tn', BF16))
    dybin = matmul("d_lru_out", dy_b, w_o_lru, 'nt', BF16, after=sent())
    put('w_o_lru', matmul("g_w_o_lru", ybin, dy_b, 'tn', BF16))

    def b_lru_out(is_ctx, rows, params):
        hf, hb, yb, dyv = rows
        _, vjp = jax.vjp(lambda s, y: s * _gelu(y), hf + hb, yb)
        dh, dyb = vjp(dyv)
        return [dh, dyb], []

    (dh_lru, dyb), _ = rw_lat("lru_out_bwd", b_lru_out,
                              [(h_f, 0, LRU_W), (h_b, 0, LRU_W), (z, Z_YB, LRU_W), (dybin, 0, LRU_W)], [],
                              [(LRU_W, F32), (LRU_W, BF16)], [])
    du_f, da_f = scan_adj("scan_f_adj", a_f, dh_lru, hp_f, 'f', n, t_lat)
    du_b, da_b = scan_adj("scan_b_adj", a_b, dh_lru, hp_b, 'b', n, t_lat)
    dxc, (dw_a, dw_x, db_a, db_x, dsp) = gates_bwd(xc, da_f, du_f, da_b, du_b, lru_w_a, lru_w_x, b_a, b_x, sp,
                                                   n, t_lat, tm)
    put('lru_w_a', dw_a.reshape(2 * LRU_BLOCKS * LRU_BW, LRU_BW).astype(BF16))
    put('lru_w_x', dw_x.reshape(2 * LRU_BLOCKS * LRU_BW, LRU_BW).astype(BF16))
    grads['lru_b_a'], grads['lru_b_x'] = db_a, db_x
    grads['lru_lambda'] = -dsp * _sigmoid(-lam)
    dxb, grads['lru_conv_w'], grads['lru_conv_b'] = conv_bwd("lru_conv_bwd", dxc, z, Z_XB, LRU_W, wt['lru_conv_w'],
                                                             2, n, t_lat)

    dq, dk, dv = attn_bwd(qr, kr_, vr, attn, dattn, lse, t_lat, n, tq_bwd)

    def b_rope(is_ctx, rows, params):
        dqv, dkv, dvv, c, s1, s2 = rows
        live = jnp.where(is_ctx, 0.0, 1.0)
        dqo = jnp.concatenate([_rope_t(dqh, c, s1, s2) for dqh in _heads(dqv)], axis=1) * live
        dkh = _heads(dkv)
        dkr = dkh[0]
        for t in dkh[1:]:
            dkr = dkr + t
        lanes = lax.broadcasted_iota(jnp.int32, dkr.shape, 1)
        dkr = jnp.where((lanes >= QK_NOPE) & (lanes < QK_DIM), _rope_t(dkr, c, s1, s2), 0.0)
        return [dqo, jnp.concatenate([dkv, dvv], axis=1), dkr], []

    (dqp, dkvp, dkr), _ = rw("rope_bwd", b_rope,
                             [(dq, 0, hp), (dk, 0, hp), (dv, 0, hp), (c_tab, 0, HEAD_PAD), (s1_tab, 0, HEAD_PAD),
                              (s2_tab, 0, HEAD_PAD)], [], [(hp, BF16), (2 * hp, BF16), (HEAD_PAD, BF16)], [])
    dqn = matmul("d_qn", dqp, w_uq_t, 'nn', BF16, after=sent())
    put('w_uq', matmul("g_w_uq", dqp, qn, 'tn', BF16))
    dkvn = matmul("d_kvn", dkvp, w_ukv_t, 'nn', BF16, after=sent())
    put('w_ukv', matmul("g_w_ukv", dkvp, kvn, 'tn', BF16))

    def b_qkv_norm(is_ctx, rows, params):
        (ql, kvl, dqv, dkvv), (gq, gkv) = rows, params
        _, vjp_q = jax.vjp(_rms, ql, gq)
        _, vjp_kv = jax.vjp(_rms, kvl, gkv)
        dql, dgq = vjp_q(dqv)
        dkvl, dgkv = vjp_kv(dkvv)
        return [dql, dkvl], [dgq, dgkv]

    (dq_lat, dkv_lat), (grads['q_norm_g'], grads['kv_norm_g']) = rw(
        "qkv_norm_bwd", b_qkv_norm, [(z, Z_Q, Q_RANK), (z, Z_KV, KV_RANK), (dqn, 0, Q_RANK), (dkvn, 0, KV_RANK)],
        [q_g, kv_g], [(Q_RANK, BF16), (KV_RANK, BF16)], [(1, Q_RANK), (1, KV_RANK)])
    pad_ctx = lambda t: jnp.pad(t, ((0, n_ctx), (0, 0)))
    dz = jnp.concatenate([dq_lat, dkv_lat, dkr, dxb, pad_ctx(dyb), pad_ctx(dgl)], axis=1)
    put('w_in', matmul("g_w_in", dz, h, 'tn', BF16))
    dh = matmul("d_h", dz, w_in_t, 'nn', BF16, after=sent())

    def b_norm1(is_ctx, rows, params):
        (xl, xc_, dhv, dxr), (g, sc, sh) = rows, params
        scv, shv = _sel(is_ctx, sc), _sel(is_ctx, sh)
        _, vjp = jax.vjp(_norm_mod, jnp.where(is_ctx, xc_, xl), g, scv, shv)
        dx, dg, dsc, dsh = vjp(dhv)
        return [dx + dxr], [dg, _seg_acc(is_ctx, dsc), _seg_acc(is_ctx, dsh)]

    (grad_x,), (grads['norm1_g'], dsc1, dsh1) = rw("norm1_bwd", b_norm1, stream + [(dh, 0, D), (dx_res, 0, D)],
                                                   [norm1_g, sc1, sh1], [(D, F32, 'lat')],
                                                   [(1, D), (2, D), (2, D)])
    zero = jnp.zeros((D,), F32)
    dmod_l = jnp.concatenate([dsh1[0], dsc1[0], dg1[0], dsh2[0], dsc2[0], dg2[0]])
    dmod_c = jnp.concatenate([dsh1[1], dsc1[1], zero, zero, zero, zero])
    return loss, grad_x, grads, dmod_l, dmod_c


def kernel(x, c, ctx, c_ctx, w_mod, b_mod, norm1_g, w_in, b_gate, q_norm_g, kv_norm_g, w_uq, w_ukv, w_o_attn, lru_conv_w, lru_conv_b, lru_w_a, lru_b_a, lru_w_x, lru_b_x, lru_lambda, w_o_lru, w_out, norm2_g, w_up, ffn_conv_w, ffn_conv_b, w_down, final_g, loss_target, m_c_ctx, m_w_mod, m_b_mod, m_norm1_g, m_w_in, m_b_gate, m_q_norm_g, m_kv_norm_g, m_w_uq, m_w_ukv, m_w_o_attn, m_lru_conv_w, m_lru_conv_b, m_lru_w_a, m_lru_b_a, m_lru_w_x, m_lru_b_x, m_lru_lambda, m_w_o_lru, m_w_out, m_norm2_g, m_w_up, m_ffn_conv_w, m_ffn_conv_b, m_w_down, m_final_g, v_c_ctx, v_w_mod, v_b_mod, v_norm1_g, v_w_in, v_b_gate, v_q_norm_g, v_kv_norm_g, v_w_uq, v_w_ukv, v_w_o_attn, v_lru_conv_w, v_lru_conv_b, v_lru_w_a, v_lru_b_a, v_lru_w_x, v_lru_b_x, v_lru_lambda, v_w_o_lru, v_w_out, v_norm2_g, v_w_up, v_ffn_conv_w, v_ffn_conv_b, v_w_down, v_final_g):
    given = dict(locals())
    strip = lambda name, a: a if name in ('c_ctx', 'final_g') else a[0]
    wsh = {n: strip(n, given[n]) for n in WEIGHTS}
    msh = {n: strip(n, given['m_' + n]) for n in WEIGHTS}
    vsh = {n: strip(n, given['v_' + n]) for n in WEIGHTS}
    me = _my_index()

    small = _flat([c[0]] + [wsh[n] for n in SMALL_F32], F32, 8)
    small_all = all_gather("gather_small", small).reshape(N_DEV, -1)
    c_all = small_all[:, :D]
    full, at = {}, D
    for n in SMALL_F32:
        cnt = math.prod(wsh[n].shape)
        full[n] = _gathered_to_full(n, small_all[:, at:at + cnt].reshape((N_DEV,) + wsh[n].shape))
        at += cnt

    cond = jnp.concatenate([c_all, c_ctx[None], jnp.zeros((7, D), F32)], axis=0)
    sil = cond * jax.nn.sigmoid(cond)
    mod_cols = matmul("mod_proj", sil, wsh['w_mod'], 'nn', F32)
    mod_all = all_gather("gather_mod", mod_cols)
    mod_all = jnp.transpose(mod_all, (1, 0, 2)).reshape(16, 6 * D) + b_mod[0][None]
    mod_l = lax.dynamic_index_in_dim(mod_all, me, axis=0, keepdims=False)
    mod_c = mod_all[N_DEV]

    rb_shards = {n: _shard_to_rb(n, wsh[n]).astype(BF16) for n in BIG_BF16}
    (w_in_blocks,) = all_gather_multi("gather_w_in", [rb_shards['w_in']])
    later = [n for n in BIG_BF16 if n != 'w_in']
    weights_started, weights_sent = exchange_start("weights_send", 'gather', [rb_shards[n] for n in later],
                                                   after=[w_in_blocks, mod_all])
    for n in REPLICATED:
        if n not in ('c_ctx', 'b_mod'):
            full[n] = wsh[n]

    def arrive(names, after):
        if names == ('w_in',):
            return {'w_in': _rb_from_gathered('w_in', w_in_blocks), '_token': weights_sent}
        picked = [later.index(n) for n in names]
        lands = exchange_wait("weights_wait_" + names[0], 'gather',
                              tuple([part[i] for i in picked] for part in weights_started), after)
        return {n: _rb_from_gathered(n, lax.dynamic_update_slice_in_dim(land, rb_shards[n][None], me, axis=0))
                for n, land in zip(names, lands)}

    in_flight = {}

    def on_grad(n, g):
        chunks = _chunks_from_rb_grad(n, g)
        own = lax.dynamic_index_in_dim(chunks, me, axis=0, keepdims=True)
        started, token = exchange_start("grad_send_" + n, 'scatter', [chunks])
        in_flight[n] = (own, started)
        return token

    loss, grad_x, grads, dmod_l, dmod_c = local_step(x[0], ctx[0], loss_target[0], mod_l, mod_c, full, on_grad,
                                                     arrive)
    dmod = jnp.stack([dmod_l, dmod_c]).reshape(2 * 6 * D // FLAT_C, FLAT_C)
    dm = all_gather("gather_dmod", dmod).reshape(N_DEV, 2, 6 * D)
    dmod_c_tot = dm[0, 1]
    for p in range(1, N_DEV):
        dmod_c_tot = dmod_c_tot + dm[p, 1]
    dm16 = jnp.concatenate([dm[:, 0], dmod_c_tot[None], jnp.zeros((7, 6 * D), F32)], axis=0)
    ncol = 6 * D // N_DEV
    dm16_cols = lax.dynamic_slice_in_dim(dm16.reshape(16, N_DEV, ncol), me, 1, axis=1)[:, 0]
    grad_w_mod = matmul("g_w_mod", sil, dm16_cols, 'tn', F32)
    dsil = matmul("d_cond", dm16_cols, wsh['w_mod'], 'nt', F32)
    sg = jax.nn.sigmoid(c_ctx)
    grads['c_ctx'] = dsil[N_DEV] * (sg * (1.0 + c_ctx * (1.0 - sg)))
    grads['b_mod'] = dmod_l + dmod_c

    g_final = {'w_mod': grad_w_mod}
    reduced, stepped = {}, {}
    for n in BIG_BF16 + ['lru_w_a', 'lru_w_x']:
        own, started = in_flight[n]
        (land,) = exchange_wait("grad_wait_" + n, 'scatter', started, dm)
        if n in ROW_SHARDED:
            g_final[n], *stepped[n] = reduce_slots("step_" + n, land, own, (wsh[n], msh[n], vsh[n]))
        elif n in COL_SHARDED and wsh[n].shape[1] % 128:
            g_t, *outs = reduce_slots("step_" + n, land, own, (wsh[n].T, msh[n].T, vsh[n].T))
            g_final[n], stepped[n] = g_t.T, [o.T for o in outs]
        else:
            reduced[n] = reduce_slots("sum_" + n, land, own)
            if n in BIG_BF16:
                g_final[n] = _rb_to_shard(n, reduced[n])

    small_names = SMALL_F32 + [n for n in REPLICATED if n not in ('lru_w_a', 'lru_w_x')]
    partials = _flat([grads[n] for n in small_names] + [loss], F32, 8)
    parts_all, a_all, x_all = all_gather_multi("gather_small_grads", [partials, reduced['lru_w_a'], reduced['lru_w_x']])
    small_sum = sum_slots("sum_small", parts_all).reshape(-1)
    g_final['lru_w_a'], g_final['lru_w_x'] = a_all.reshape(wsh['lru_w_a'].shape), x_all.reshape(wsh['lru_w_x'].shape)
    at = 0
    for n in small_names:
        cnt = math.prod(full[n].shape) if n in SMALL_F32 else math.prod(wsh[n].shape)
        g = small_sum[at:at + cnt]
        if n in SMALL_F32:
            k = full[n].shape[0]
            g = lax.dynamic_index_in_dim(g.reshape(k, N_DEV, -1), me, axis=1, keepdims=False)
        g_final[n] = g.reshape(wsh[n].shape)
        at += cnt
    loss = small_sum[at]

    for n in ['w_mod'] + BIG_BF16:
        if n not in stepped:
            stepped[n] = adamw("adamw_" + n, wsh[n], g_final[n], msh[n], vsh[n])
    rest = [n for n in WEIGHTS if n not in stepped]
    as2d = lambda a: a.reshape(-1, a.shape[-1])
    rest_out = adamw_many("adamw_small", *[[as2d(d[n]) for n in rest] for d in (wsh, g_final, msh, vsh)])
    stepped.update(zip(rest, rest_out))
    shaped = lambda n, a: a.reshape(given[n].shape)
    return (loss, grad_x[None],
            *[shaped(n, g_final[n]) for n in WEIGHTS],
            *[shaped(n, stepped[n][k]) for k in range(3) for n in WEIGHTS])
```

```python
import functools
import math

import jax
import jax.numpy as jnp
from jax import lax
from jax.experimental import pallas as pl
from jax.experimental.pallas import tpu as pltpu

F32 = jnp.float32
BF16 = jnp.bfloat16
MESH = pl.DeviceIdType.MESH

N_DEV = 8
D = 1024
N_HEADS = 8
HEAD_PAD = 128
QK_NOPE, QK_ROPE, V_HEAD = 64, 32, 64
QK_DIM = QK_NOPE + QK_ROPE
Q_RANK, KV_RANK = 384, 256
LRU_W, LRU_BLOCKS, LRU_BW = 1280, 10, 128
FFN = 2816
GRID_W = 64
ROPE_BASE = 10000.0
LRU_C = 8.0
EPS = 1e-6
Z_Q, Z_KV, Z_KR, Z_XB, Z_YB, Z_GL, Z_END = 0, 384, 640, 768, 2048, 3328, 5376
ADAM_LR, ADAM_B1, ADAM_B2, ADAM_EPS, ADAM_WD, ADAM_STEP = 0.001, 0.9, 0.999, 1e-08, 0.01, 10

VMEM_LIMIT = 52 * 1024 * 1024
FLAT_C = 512

WEIGHTS = ['c_ctx', 'w_mod', 'b_mod', 'norm1_g', 'w_in', 'b_gate', 'q_norm_g', 'kv_norm_g', 'w_uq', 'w_ukv',
           'w_o_attn', 'lru_conv_w', 'lru_conv_b', 'lru_w_a', 'lru_b_a', 'lru_w_x', 'lru_b_x', 'lru_lambda',
           'w_o_lru', 'w_out', 'norm2_g', 'w_up', 'ffn_conv_w', 'ffn_conv_b', 'w_down', 'final_g']
COL_SHARDED = ['w_in', 'w_uq', 'w_ukv', 'w_o_attn', 'lru_conv_w', 'lru_b_a', 'lru_b_x', 'lru_lambda', 'w_up',
               'ffn_conv_w']
ROW_SHARDED = ['w_o_lru', 'w_out', 'w_down']
BIG_BF16 = ['w_in', 'w_uq', 'w_ukv', 'w_o_attn', 'w_o_lru', 'w_out', 'w_up', 'w_down']
SMALL_F32 = ['lru_conv_w', 'lru_b_a', 'lru_b_x', 'lru_lambda', 'ffn_conv_w']
REPLICATED = ['c_ctx', 'b_mod', 'norm1_g', 'b_gate', 'q_norm_g', 'kv_norm_g', 'lru_conv_b', 'lru_w_a', 'lru_w_x',
              'norm2_g', 'ffn_conv_b', 'final_g']


def _cparams(sem=None):
    return pltpu.CompilerParams(dimension_semantics=sem, vmem_limit_bytes=VMEM_LIMIT)


def _pick(n, cands):
    for c in cands:
        if c <= n and n % c == 0:
            return c
    return n


def _best_div(n, mult, cap):
    best = mult
    for d in range(mult, min(n, cap) + 1, mult):
        if n % d == 0:
            best = d
    return best


MXU_DIM = 256
ROW_TILES = (1088, 1024, 544, 512, 256, 128, 64, 32, 16, 8)
LANE_TILES = (2816, 1792, 1536, 1280, 1024, 768, 512, 256, 1408, 896, 640, 384, 128)
DEPTH_ROW_TILES = (2176, 2048, 1024, 512, 256, 1088, 128, 64, 32, 16, 8)
MATMUL_VMEM_BUDGET = 40 * 1024 * 1024
MXU_FILL_OK = 0.9


def _my_pos():
    return lax.axis_index("x"), lax.axis_index("y"), lax.axis_index("c")


def _my_index():
    x, y, c = _my_pos()
    return 4 * x + 2 * y + c


def all_gather_multi(name, shards):
    n_arr = len(shards)
    arrays = range(n_arr)

    def body(*refs):
        x_refs, out_refs = refs[:n_arr], refs[n_arr:2 * n_arr]
        send_sems, recv_sems, local_sems = refs[2 * n_arr:]
        x, y, c = _my_pos()
        me, sibling = (x, y, c), (x, y, 1 - c)
        chips = [(1 - x, y), (x, 1 - y), (1 - x, 1 - y)]

        def slot(a, px, py, pc):
            return out_refs[a].at[4 * px + 2 * py + pc]

        def copy(a, k, block, to, src=None):
            return pltpu.make_async_remote_copy(
                src_ref=slot(a, *block) if src is None else src, dst_ref=slot(a, *block),
                send_sem=send_sems.at[7 * a + k], recv_sem=recv_sems.at[7 * a + k], device_id=to,
                device_id_type=MESH)

        mine = [pltpu.make_async_copy(x_refs[a], slot(a, *me), local_sems.at[a]) for a in arrays]
        first = [copy(a, 1 + j, me, (*chip, c), src=x_refs[a]) for j, chip in enumerate(chips) for a in arrays]
        first += [copy(a, 0, me, sibling, src=x_refs[a]) for a in arrays]
        for cp in first + mine:
            cp.start()
        passed = []
        for j, chip in enumerate(chips):
            for a in arrays:
                copy(a, 1 + j, (*chip, c), me).wait_recv()
                passed.append(copy(a, 4 + j, (*chip, c), sibling))
                passed[-1].start()
        for a in arrays:
            copy(a, 0, sibling, me).wait_recv()
            for j, chip in enumerate(chips):
                copy(a, 4 + j, (*chip, 1 - c), me).wait_recv()
        for cp in first + passed:
            cp.wait_send()
        for cp in mine:
            cp.wait()

    hbm = pl.BlockSpec(memory_space=pl.ANY)
    return pl.pallas_call(
        body, name=name,
        out_shape=[jax.ShapeDtypeStruct((N_DEV,) + s.shape, s.dtype) for s in shards],
        in_specs=[hbm] * n_arr, out_specs=[hbm] * n_arr,
        scratch_shapes=[pltpu.SemaphoreType.DMA((7 * n_arr,)), pltpu.SemaphoreType.DMA((7 * n_arr,)),
                        pltpu.SemaphoreType.DMA((n_arr,))],
    )(*shards)


def all_gather(name, shard):
    return all_gather_multi(name, [shard])[0]


def _peers():
    x, y, c = _my_pos()
    out = []
    for rel in (6, 4, 2, 7, 5, 3, 1):
        px, py, pc = x ^ ((rel >> 2) & 1), y ^ ((rel >> 1) & 1), c ^ (rel & 1)
        out.append((rel - 1, (px, py, pc), 4 * px + 2 * py + pc))
    return out


def _exchange_copies(mode, src_refs, land_refs, send_sems, recv_sems):
    x, y, c = _my_pos()
    me = 4 * x + 2 * y + c
    sends, arrivals = [], []
    for k, peer_pos, peer in _peers():
        for a, (src, land) in enumerate(zip(src_refs, land_refs)):
            piece = src.at[peer] if mode == 'scatter' else src
            sems = dict(send_sem=send_sems[a].at[k], recv_sem=recv_sems[a].at[k], device_id_type=MESH)
            sends.append(pltpu.make_async_remote_copy(src_ref=piece, dst_ref=land.at[me], device_id=peer_pos, **sems))
            arrivals.append(pltpu.make_async_remote_copy(src_ref=piece, dst_ref=land.at[peer], device_id=(x, y, c), **sems))
    return sends, arrivals


_HBM = pl.BlockSpec(memory_space=pltpu.HBM)
_SEM = pl.BlockSpec(memory_space=pltpu.SEMAPHORE)


def exchange_start(name, mode, arrays, after=()):
    n_arr, n_after = len(arrays), len(after)
    land_shapes = [a.shape if mode == 'scatter' else (N_DEV,) + a.shape for a in arrays]

    def body(*refs):
        src_refs, land_refs = refs[:n_arr], refs[n_arr:2 * n_arr]
        refs = refs[n_after:]
        send_sems, recv_sems = refs[2 * n_arr:3 * n_arr], refs[3 * n_arr:4 * n_arr]
        sends, _ = _exchange_copies(mode, src_refs, land_refs, send_sems, recv_sems)
        for cp in sends:
            cp.start()
        token = refs[-1]
        token[...] = jnp.zeros_like(token)

    sem = pltpu.SemaphoreType.DMA((N_DEV - 1,))
    res = pl.pallas_call(
        body, name=name,
        out_shape=[sem] * (2 * n_arr) + [pltpu.HBM(a.shape, a.dtype) for a in arrays]
        + [pltpu.HBM(s, a.dtype) for s, a in zip(land_shapes, arrays)] + [jax.ShapeDtypeStruct((8, 128), F32)],
        in_specs=[_HBM] * (2 * n_arr) + [pl.BlockSpec(memory_space=pl.ANY)] * n_after,
        out_specs=[_SEM] * (2 * n_arr) + [_HBM] * (2 * n_arr) + [pl.BlockSpec(memory_space=pltpu.VMEM)],
        input_output_aliases={i: 2 * n_arr + i for i in range(2 * n_arr)},
        compiler_params=pltpu.CompilerParams(has_side_effects=pltpu.SideEffectType.DATAFLOW_SIDE_EFFECTING),
    )(*[pltpu.with_memory_space_constraint(a, pltpu.HBM) for a in arrays],
      *[pltpu.with_memory_space_constraint(lax.empty(s, a.dtype), pltpu.HBM) for s, a in zip(land_shapes, arrays)],
      *after)
    return (res[:n_arr], res[n_arr:2 * n_arr], res[2 * n_arr:3 * n_arr], res[3 * n_arr:4 * n_arr]), res[-1]


def exchange_wait(name, mode, started, after):
    send_sems, recv_sems, thru, land = started
    n_arr = len(thru)

    def body(*refs):
        src_refs, land_refs = refs[:n_arr], refs[n_arr:2 * n_arr]
        s_sems, r_sems = refs[2 * n_arr:3 * n_arr], refs[3 * n_arr:4 * n_arr]
        sends, arrivals = _exchange_copies(mode, src_refs, land_refs, s_sems, r_sems)
        for cp in sends:
            cp.wait_send()
        for cp in arrivals:
            cp.wait_recv()

    res = pl.pallas_call(
        body, name=name,
        out_shape=[pltpu.HBM(a.shape, a.dtype) for a in thru] + [pltpu.HBM(a.shape, a.dtype) for a in land],
        in_specs=[_HBM] * (2 * n_arr) + [_SEM] * (2 * n_arr) + [pl.BlockSpec(memory_space=pl.ANY)],
        out_specs=[_HBM] * (2 * n_arr),
        input_output_aliases={i: i for i in range(2 * n_arr)},
        compiler_params=pltpu.CompilerParams(has_side_effects=pltpu.SideEffectType.DATAFLOW_SIDE_EFFECTING),
    )(*thru, *land, *send_sems, *recv_sems, after)
    return res[n_arr:]


def _sum_with_own(slot_ref, own_ref):
    x, y, c = _my_pos()
    me = 4 * x + 2 * y + c
    acc = None
    for p in range(N_DEV):
        v = jnp.where(me == p, own_ref[0], slot_ref[p]).astype(F32)
        acc = v if acc is None else acc + v
    return acc


def reduce_slots(name, slots, own, step=None):
    _, r, ccols = slots.shape
    tc = _pick(ccols, (256, 128))
    c1 = 1.0 - ADAM_B1 ** ADAM_STEP
    c2 = 1.0 - ADAM_B2 ** ADAM_STEP

    def body(s_ref, own_ref, *refs):
        g = _sum_with_own(s_ref, own_ref)
        if step is None:
            refs[0][...] = g
            return
        w_ref, m_ref, v_ref, g_ref, d_ref, nm_ref, nv_ref = refs
        nm = ADAM_B1 * m_ref[...] + (1.0 - ADAM_B1) * g
        nv = ADAM_B2 * v_ref[...] + (1.0 - ADAM_B2) * (g * g)
        g_ref[...] = g
        d_ref[...] = -ADAM_LR * ((nm / c1) / (jnp.sqrt(nv / c2) + ADAM_EPS) + ADAM_WD * w_ref[...])
        nm_ref[...] = nm
        nv_ref[...] = nv

    col = pl.BlockSpec((r, tc), lambda j: (0, j))
    n_out = 1 if step is None else 4
    res = pl.pallas_call(
        body, name=name, grid=(ccols // tc,),
        out_shape=[jax.ShapeDtypeStruct((r, ccols), F32)] * n_out,
        in_specs=[pl.BlockSpec((N_DEV, r, tc), lambda j: (0, 0, j)), pl.BlockSpec((1, r, tc), lambda j: (0, 0, j))]
        + [col] * (0 if step is None else 3),
        out_specs=[col] * n_out,
        compiler_params=_cparams(("parallel",)),
    )(slots, own, *(step or ()))
    return res[0] if step is None else res


def sum_slots(name, slots):
    _, r, ccols = slots.shape
    tc = _pick(ccols, (256, 128))

    def body(s_ref, o_ref):
        acc = s_ref[0].astype(F32)
        for p in range(1, N_DEV):
            acc = acc + s_ref[p].astype(F32)
        o_ref[...] = acc

    return pl.pallas_call(
        body, name=name, grid=(ccols // tc,),
        out_shape=jax.ShapeDtypeStruct((r, ccols), F32),
        in_specs=[pl.BlockSpec((N_DEV, r, tc), lambda j: (0, 0, j))],
        out_specs=pl.BlockSpec((r, tc), lambda j: (0, j)),
        compiler_params=_cparams(("parallel",)),
    )(slots)


def _mxu_fill(t):
    return t / (-(-t // MXU_DIM) * MXU_DIM)


def _matmul_tiles(mode, m_extent, n, k_extent, k_total, itemsizes):
    a_bytes, b_bytes, o_bytes = itemsizes
    m_cands = [c for c in (LANE_TILES if mode == 'tn' else ROW_TILES) if m_extent % c == 0] or [m_extent]
    k_cands = [c for c in (DEPTH_ROW_TILES if mode == 'tn' else LANE_TILES) if k_extent % c == 0] or [k_extent]
    n_cands = [c for c in LANE_TILES if n % c == 0] or [n]
    best = None
    for tm in m_cands:
        for tk in k_cands:
            for tn in n_cands:
                f32_tiles = 2 if k_total // tk > 1 else 1
                vmem = 2 * (tm * tk * a_bytes + tk * tn * b_bytes + tm * tn * o_bytes) + tm * tn * 4 * f32_tiles
                if vmem > MATMUL_VMEM_BUDGET:
                    continue
                key = (_mxu_fill(tn) * _mxu_fill(tk) >= MXU_FILL_OK, tm * tn * tk)
                if best is None or key > best[0]:
                    best = (key, (tm, tn, tk))
    assert best is not None, (mode, m_extent, n, k_extent)
    return best[1]


def matmul(name, a, b, mode, out_dtype, after=()):
    after = [t for t in after if t is not None]
    pieces, a_rows, a_cols = (1,) + a.shape if a.ndim == 2 else a.shape
    if mode == 'nn':
        (m, k), (k2, n) = (a_rows, pieces * a_cols), b.shape
    elif mode == 'nt':
        (m, k), (n, k2) = (a_rows, pieces * a_cols), b.shape
    else:
        (k, m), (k2, n) = (a_rows, pieces * a_cols), b.shape
    assert k == k2, (name, a.shape, b.shape, mode)
    tm, tn, tk = _matmul_tiles(mode, a_cols if mode == 'tn' else m, n, k if mode == 'tn' else a_cols, k,
                               (a.dtype.itemsize, b.dtype.itemsize, jnp.dtype(out_dtype).itemsize))
    nk = k // tk
    per_piece = a_cols // (tm if mode == 'tn' else tk)
    if a.ndim == 2:
        a_block = lambda rows, cols, at: pl.BlockSpec((rows, cols), at)
    else:
        a_block = lambda rows, cols, at: pl.BlockSpec(
            (None, rows, cols), lambda i, j, kk: (at(i, j, kk)[1] // per_piece, at(i, j, kk)[0],
                                                  at(i, j, kk)[1] % per_piece))
    if mode == 'nn':
        a_spec = a_block(tm, tk, lambda i, j, kk: (i, kk))
        b_spec = pl.BlockSpec((tk, tn), lambda i, j, kk: (kk, j))
        dn = (((1,), (0,)), ((), ()))
    elif mode == 'nt':
        a_spec = a_block(tm, tk, lambda i, j, kk: (i, kk))
        b_spec = pl.BlockSpec((tn, tk), lambda i, j, kk: (j, kk))
        dn = (((1,), (1,)), ((), ()))
    else:
        a_spec = a_block(tk, tm, lambda i, j, kk: (kk, i))
        b_spec = pl.BlockSpec((tk, tn), lambda i, j, kk: (kk, j))
        dn = (((0,), (0,)), ((), ()))

    def product(a_ref, b_ref):
        return lax.dot_general(a_ref[...].astype(BF16), b_ref[...].astype(BF16), dn, preferred_element_type=F32)

    n_after = len(after)

    def body_one(a_ref, b_ref, *rest):
        o_ref = rest[n_after]
        o_ref[...] = product(a_ref, b_ref).astype(o_ref.dtype)

    def body(a_ref, b_ref, *rest):
        o_ref, acc_ref = rest[n_after:]
        kk = pl.program_id(2)

        @pl.when(kk == 0)
        def _():
            acc_ref[...] = jnp.zeros_like(acc_ref)

        acc_ref[...] += product(a_ref, b_ref)

        @pl.when(kk == nk - 1)
        def _():
            o_ref[...] = acc_ref[...].astype(o_ref.dtype)

    return pl.pallas_call(
        body_one if nk == 1 else body, name=name, grid=(m // tm, n // tn, nk),
        out_shape=jax.ShapeDtypeStruct((m, n), out_dtype),
        in_specs=[a_spec, b_spec] + [pl.BlockSpec(memory_space=pl.ANY)] * n_after,
        out_specs=pl.BlockSpec((tm, tn), lambda i, j, kk: (i, j)),
        scratch_shapes=[] if nk == 1 else [pltpu.VMEM((tm, tn), F32)],
        compiler_params=_cparams(("parallel", "parallel", "arbitrary")),
    )(a, b, *after)


def rowwise(name, fn, rows, params, out_rows, out_accs, n_rows, t_lat, tm):
    nb, nbl = n_rows // tm, t_lat // tm
    in_specs, piece_counts = [], []
    operands = []
    for arr, off, width, *kind in rows:
        g = math.gcd(off, width) if off else width
        assert g % 128 == 0 or (off == 0 and width == arr.shape[1]), (name, off, width)
        cnt = width // g
        last = arr.shape[0] // tm - 1
        clamp = arr.shape[0] < n_rows
        for p in range(cnt):
            cb = off // g + p
            if kind == ['ctx']:
                in_specs.append(pl.BlockSpec(
                    (tm, g), lambda i, cb=cb, last=last: (jnp.clip(i - nbl, 0, last), cb)))
            elif clamp:
                in_specs.append(pl.BlockSpec((tm, g), lambda i, cb=cb, last=last: (jnp.minimum(i, last), cb)))
            else:
                in_specs.append(pl.BlockSpec((tm, g), lambda i, cb=cb: (i, cb)))
            operands.append(arr)
        piece_counts.append(cnt)
    for p in params:
        in_specs.append(pl.BlockSpec(p.shape, lambda i, nd=p.ndim: (0,) * nd))
        operands.append(p)
    n_in = sum(piece_counts)
    n_par = len(params)
    n_or = len(out_rows)
    lat_only = [kind == ['lat'] for _, _, *kind in out_rows]
    out_shape = [jax.ShapeDtypeStruct((t_lat if lat else n_rows, w), dt)
                 for (w, dt, *_), lat in zip(out_rows, lat_only)]
    out_shape += [jax.ShapeDtypeStruct(s, F32) for s in out_accs]
    out_specs = [pl.BlockSpec((tm, w), (lambda i: (jnp.minimum(i, nbl - 1), 0)) if lat else (lambda i: (i, 0)))
                 for (w, *_), lat in zip(out_rows, lat_only)]
    out_specs += [pl.BlockSpec(s, lambda i, nd=len(s): (0,) * nd) for s in out_accs]

    def body(*refs):
        in_refs, par_refs = refs[:n_in], refs[n_in:n_in + n_par]
        orow_refs = refs[n_in + n_par:n_in + n_par + n_or]
        oacc_refs = refs[n_in + n_par + n_or:]
        i = pl.program_id(0)
        tiles, at = [], 0
        for cnt in piece_counts:
            parts = [in_refs[at + p][...].astype(F32) for p in range(cnt)]
            tiles.append(parts[0] if cnt == 1 else jnp.concatenate(parts, axis=1))
            at += cnt
        is_ctx = i * tm >= t_lat
        outs, accs = fn(is_ctx, tiles, [p[...] for p in par_refs])
        for o_ref, o, lat in zip(orow_refs, outs, lat_only):
            if lat:
                @pl.when(jnp.logical_not(is_ctx))
                def _(o_ref=o_ref, o=o):
                    o_ref[...] = o.astype(o_ref.dtype)
            else:
                o_ref[...] = o.astype(o_ref.dtype)
        if oacc_refs:
            @pl.when(i == 0)
            def _():
                for a_ref in oacc_refs:
                    a_ref[...] = jnp.zeros_like(a_ref)
            for a_ref, a in zip(oacc_refs, accs):
                a_ref[...] += a.astype(F32)

    res = pl.pallas_call(
        body, name=name, grid=(nb,),
        out_shape=out_shape, in_specs=in_specs, out_specs=out_specs,
        compiler_params=_cparams(("arbitrary",)),
    )(*operands)
    return res[:n_or], res[n_or:]


def _rms(x, g):
    return x * lax.rsqrt(jnp.mean(x * x, axis=-1, keepdims=True) + EPS) * g


def _norm_mod(x, g, sc, sh):
    return _rms(x, g) * (1.0 + sc) + sh


def _sigmoid(x):
    return 0.5 * jnp.tanh(0.5 * x) + 0.5


def _silu(x):
    return x * _sigmoid(x)


def _gelu(x):
    return 0.5 * x * (1.0 + jnp.tanh(math.sqrt(2.0 / math.pi) * (x + 0.044715 * (x * x * x))))


def _sel(is_ctx, p):
    return jnp.where(is_ctx, p[1:2], p[0:1])


def _seg_acc(is_ctx, v):
    rows = lax.broadcasted_iota(jnp.int32, (2, v.shape[1]), 0)
    return jnp.where(rows == is_ctx.astype(jnp.int32), jnp.broadcast_to(v, (2, v.shape[1])), 0.0)


def _rsum(v):
    return jnp.sum(v, axis=0, keepdims=True)


def _shift_rows(x, o, t_lat, n):
    if o == 0:
        return x
    y = pltpu.roll(x, (-o) % n, 0)
    t = lax.broadcasted_iota(jnp.int32, x.shape, 0)
    if o > 0:
        ok = t < n - o
        if t_lat < n:
            ok = ok & ((t < t_lat - o) | (t >= t_lat))
    else:
        ok = t >= -o
        if t_lat < n:
            ok = ok & ((t < t_lat) | (t >= t_lat - o))
    return jnp.where(ok, y, 0.0)


def conv_fwd(name, xarr, col_off, width, w, b, left, n_rows, t_lat, out_dtype, cb=128):
    taps = w.shape[0]
    assert col_off % cb == 0 and width % cb == 0

    def body(x_ref, w_ref, b_ref, o_ref):
        x = x_ref[...].astype(F32)
        acc = jnp.broadcast_to(b_ref[...], x.shape)
        for k in range(taps):
            acc = acc + _shift_rows(x, k - left, t_lat, n_rows) * w_ref[k:k + 1, :]
        o_ref[...] = acc.astype(o_ref.dtype)

    return pl.pallas_call(
        body, name=name, grid=(width // cb,),
        out_shape=jax.ShapeDtypeStruct((n_rows, width), out_dtype),
        in_specs=[pl.BlockSpec((n_rows, cb), lambda j: (0, col_off // cb + j)),
                  pl.BlockSpec((taps, cb), lambda j: (0, j)),
                  pl.BlockSpec((1, cb), lambda j: (0, j))],
        out_specs=pl.BlockSpec((n_rows, cb), lambda j: (0, j)),
        compiler_params=_cparams(("parallel",)),
    )(xarr, w, b)


def conv_bwd(name, dout, xarr, col_off, width, w, left, n_rows, t_lat, cb=128):
    taps = w.shape[0]

    def body(d_ref, x_ref, w_ref, dx_ref, dw_ref, db_ref):
        d = d_ref[...].astype(F32)
        x = x_ref[...].astype(F32)
        dx = jnp.zeros_like(d)
        dws = []
        for k in range(taps):
            dx = dx + _shift_rows(d, left - k, t_lat, n_rows) * w_ref[k:k + 1, :]
            dws.append(_rsum(d * _shift_rows(x, k - left, t_lat, n_rows)))
        dx_ref[...] = dx.astype(dx_ref.dtype)
        dw_ref[...] = jnp.concatenate(dws, axis=0)
        db_ref[...] = _rsum(d)

    return pl.pallas_call(
        body, name=name, grid=(width // cb,),
        out_shape=[jax.ShapeDtypeStruct((n_rows, width), BF16), jax.ShapeDtypeStruct((taps, width), F32),
                   jax.ShapeDtypeStruct((1, width), F32)],
        in_specs=[pl.BlockSpec((n_rows, cb), lambda j: (0, j)),
                  pl.BlockSpec((n_rows, cb), lambda j: (0, col_off // cb + j)),
                  pl.BlockSpec((taps, cb), lambda j: (0, j))],
        out_specs=[pl.BlockSpec((n_rows, cb), lambda j: (0, j)), pl.BlockSpec((taps, cb), lambda j: (0, j)),
                   pl.BlockSpec((1, cb), lambda j: (0, j))],
        compiler_params=_cparams(("parallel",)),
    )(dout, xarr, w)


def _ffn_conv(a, w_ref, b_ref, t_lat):
    shifted = [_shift_rows(a, k - 1, t_lat, t_lat) for k in range(3)]
    ac = jnp.broadcast_to(b_ref[...], a.shape)
    for k in range(3):
        ac = ac + shifted[k] * w_ref[k:k + 1, :]
    return ac, shifted


def ffn_mix_fwd(u, w, b, t_lat, cb=128):
    nblk = FFN // cb

    def body(a_ref, g_ref, w_ref, b_ref, f_ref):
        ac, _ = _ffn_conv(a_ref[...].astype(F32), w_ref, b_ref, t_lat)
        f_ref[...] = (_silu(ac) * g_ref[...].astype(F32)).astype(f_ref.dtype)

    col = lambda shape, off=0: pl.BlockSpec(shape, lambda j: (0, off + j))
    return pl.pallas_call(
        body, name="ffn_mix", grid=(nblk,),
        out_shape=jax.ShapeDtypeStruct((t_lat, FFN), BF16),
        in_specs=[col((t_lat, cb)), col((t_lat, cb), nblk), col((3, cb)), col((1, cb))],
        out_specs=col((t_lat, cb)),
        compiler_params=_cparams(("parallel",)),
    )(u, u, w, b)


def ffn_mix_bwd(u, df, w, b, t_lat, cb=128):
    nblk = FFN // cb

    def body(a_ref, g_ref, df_ref, w_ref, b_ref, du_ref, dw_ref, db_ref):
        ac, shifted = _ffn_conv(a_ref[...].astype(F32), w_ref, b_ref, t_lat)
        d = df_ref[...].astype(F32)
        s = _sigmoid(ac)
        du_ref[1] = (d * (ac * s)).astype(du_ref.dtype)
        dac = d * g_ref[...].astype(F32) * (s * (1.0 + ac * (1.0 - s)))
        da = jnp.zeros_like(dac)
        for k in range(3):
            da = da + _shift_rows(dac, 1 - k, t_lat, t_lat) * w_ref[k:k + 1, :]
        du_ref[0] = da.astype(du_ref.dtype)
        dw_ref[...] = jnp.concatenate([_rsum(dac * shifted[k]) for k in range(3)], axis=0)
        db_ref[...] = _rsum(dac)

    col = lambda shape, off=0: pl.BlockSpec(shape, lambda j: (0, off + j))
    return pl.pallas_call(
        body, name="ffn_mix_bwd", grid=(nblk,),
        out_shape=[jax.ShapeDtypeStruct((2, t_lat, FFN), BF16),
                   jax.ShapeDtypeStruct((3, FFN), F32), jax.ShapeDtypeStruct((1, FFN), F32)],
        in_specs=[col((t_lat, cb)), col((t_lat, cb), nblk), col((t_lat, cb)), col((3, cb)), col((1, cb))],
        out_specs=[pl.BlockSpec((2, t_lat, cb), lambda j: (0, 0, j)), col((3, cb)), col((1, cb))],
        compiler_params=_cparams(("parallel",)),
    )(u, u, df, w, b)


def _chunk_order(direction, nb, nbl):
    if direction == 'f':
        return lambda s: ((s + nbl) % nb, 0)
    return lambda s: (nb - 1 - s, 0)


def _adjoint_order(direction, nb, nbl):
    if direction == 'f':
        return lambda s: ((nb - 1 - s + nbl) % nb, 0)
    return lambda s: (s, 0)


SUBLANES = 8


def _chunk_scan(a, b, carry, rev):
    tc = a.shape[0]
    row = lax.broadcasted_iota(jnp.int32, a.shape, 0)
    in_tile = jnp.bitwise_and(row, SUBLANES - 1)
    for k in (1, 2, 4):
        shift = tc - k if rev else k
        edge = in_tile >= SUBLANES - k if rev else in_tile < k
        b = jnp.where(edge, b, a * pltpu.roll(b, shift, 0) + b)
        a = jnp.where(edge, a, a * pltpu.roll(a, shift, 0))
    nt = tc // SUBLANES
    hs = [None] * nt
    c = carry
    for kt in range(nt):
        k = nt - 1 - kt if rev else kt
        h = b[k * SUBLANES:(k + 1) * SUBLANES] + a[k * SUBLANES:(k + 1) * SUBLANES] * c
        hs[k] = h
        c = h[0:1] if rev else h[SUBLANES - 1:SUBLANES]
    h = jnp.concatenate(hs, axis=0)
    if rev:
        return h, jnp.where(row == tc - 1, carry, pltpu.roll(h, tc - 1, 0)), c
    return h, jnp.where(row == 0, carry, pltpu.roll(h, 1, 0)), c


def scan_fwd(name, a, u, direction, n_rows, t_lat):
    w = a.shape[1]
    tc = _pick(math.gcd(t_lat, n_rows), (256, 128))
    nb, nbl = n_rows // tc, t_lat // tc
    order = _chunk_order(direction, nb, nbl)
    rev = direction == 'b'

    def body(a_ref, u_ref, h_ref, hp_ref, carry):
        @pl.when(pl.program_id(0) == 0)
        def _():
            carry[...] = jnp.zeros_like(carry)

        h_ref[...], hp_ref[...], carry[...] = _chunk_scan(a_ref[...], u_ref[...], carry[...], rev)

    spec = pl.BlockSpec((tc, w), order)
    return pl.pallas_call(
        body, name=name, grid=(nb,),
        out_shape=[jax.ShapeDtypeStruct((n_rows, w), F32)] * 2,
        in_specs=[spec, spec], out_specs=[spec, spec],
        scratch_shapes=[pltpu.VMEM((1, w), F32)],
        compiler_params=_cparams(("arbitrary",)),
    )(a, u)


def scan_adj(name, a, dh, hprev, direction, n_rows, t_lat):
    w = a.shape[1]
    tc = _pick(math.gcd(t_lat, n_rows), (256, 128))
    nb, nbl = n_rows // tc, t_lat // tc
    order = _adjoint_order(direction, nb, nbl)
    rev = direction == 'f'

    def dh_order(s):
        c, _ = order(s)
        return (jnp.minimum(c, nbl - 1), 0)

    def body(a_ref, dh_ref, hp_ref, du_ref, da_ref, carry):
        s = pl.program_id(0)

        @pl.when(s == 0)
        def _():
            carry[...] = jnp.zeros_like(carry)

        chunk, _ = order(s)
        live = (chunk < nbl).astype(F32)

        av = a_ref[...]
        dv = dh_ref[...] * live
        _, c_next, carry[...] = _chunk_scan(av, av * dv, carry[...], rev)
        lam = dv + c_next
        du_ref[...] = lam
        da_ref[...] = lam * hp_ref[...]

    spec = pl.BlockSpec((tc, w), order)
    return pl.pallas_call(
        body, name=name, grid=(nb,),
        out_shape=[jax.ShapeDtypeStruct((n_rows, w), F32)] * 2,
        in_specs=[spec, pl.BlockSpec((tc, w), dh_order), spec], out_specs=[spec, spec],
        scratch_shapes=[pltpu.VMEM((1, w), F32)],
        compiler_params=_cparams(("arbitrary",)),
    )(a, dh, hprev)


def _one_minus_a_squared(log_a, a):
    return (1.0 + a * a) * jnp.tanh(-log_a)


def _gate_elem(pre_r, pre_i, xc, b_a, b_x, sp):
    r = _sigmoid(pre_r + b_a)
    i = _sigmoid(pre_i + b_x)
    log_a = (-LRU_C) * r * sp
    a = jnp.exp(log_a)
    m2 = _one_minus_a_squared(log_a, a)
    mult = jnp.where(m2 > 0.0, m2 * lax.rsqrt(m2), 0.0)
    return a, mult * (i * xc)


def _gate_elem_bwd(pre_r, pre_i, xc, b_a, b_x, sp, da, du):
    r = _sigmoid(pre_r + b_a)
    i = _sigmoid(pre_i + b_x)
    log_a = (-LRU_C) * r * sp
    a = jnp.exp(log_a)
    m2 = _one_minus_a_squared(log_a, a)
    inv_mult = lax.rsqrt(m2)
    g = du * (m2 * inv_mult)
    d_mult = du * (i * xc)
    d_log_a = (da - d_mult * a * inv_mult) * a
    d_pre_r = d_log_a * ((-LRU_C) * sp) * (r * (1.0 - r))
    d_pre_i = g * xc * (i * (1.0 - i))
    return d_pre_r, d_pre_i, g * i, _rsum(d_log_a * ((-LRU_C) * r))


def _blockdiag(xb16, w_ref_val, d):
    outs = []
    for n in range(LRU_BLOCKS):
        outs.append(jnp.dot(xb16[:, n * LRU_BW:(n + 1) * LRU_BW], w_ref_val[d * LRU_BLOCKS + n],
                            preferred_element_type=F32))
    return jnp.concatenate(outs, axis=1)


def gates_fwd(xc, w_a, w_x, b_a, b_x, sp, n_rows, t_lat, tm):
    def fn(is_ctx, rows, params):
        (x,), (wa, wx, ba, bx, spv) = rows, params
        xb16 = x.astype(BF16)
        outs = []
        for d in range(2):
            a, u = _gate_elem(_blockdiag(xb16, wa, d), _blockdiag(xb16, wx, d), x,
                              ba[d:d + 1], bx[d:d + 1], spv[d:d + 1])
            outs += [a, u]
        return outs, []

    (a_f, u_f, a_b, u_b), _ = rowwise("gates_fwd", fn, [(xc, 0, LRU_W)], [w_a, w_x, b_a, b_x, sp],
                                      [(LRU_W, F32)] * 4, [], n_rows, t_lat, tm)
    return a_f, u_f, a_b, u_b


def gates_bwd(xc, da_f, du_f, da_b, du_b, w_a, w_x, b_a, b_x, sp, n_rows, t_lat, tm):
    def fn(is_ctx, rows, params):
        (x, daf, duf, dab, dub), (wa, wx, ba, bx, spv) = rows, params
        xb16 = x.astype(BF16)
        dxc = jnp.zeros_like(x)
        dwa, dwx, dba, dbx, dsp = [], [], [], [], []
        for d, (da, du) in enumerate(((daf, duf), (dab, dub))):
            dpr, dpi, dx_e, dsp_d = _gate_elem_bwd(_blockdiag(xb16, wa, d), _blockdiag(xb16, wx, d), x,
                                                   ba[d:d + 1], bx[d:d + 1], spv[d:d + 1], da, du)
            dba_d, dbx_d = _rsum(dpr), _rsum(dpi)
            dxc = dxc + dx_e
            dpr16, dpi16 = dpr.astype(BF16), dpi.astype(BF16)
            back = []
            for n in range(LRU_BLOCKS):
                sl = slice(n * LRU_BW, (n + 1) * LRU_BW)
                nt_dims = (((1,), (1,)), ((), ()))
                back.append(lax.dot_general(dpr16[:, sl], wa[d * LRU_BLOCKS + n], nt_dims, preferred_element_type=F32)
                            + lax.dot_general(dpi16[:, sl], wx[d * LRU_BLOCKS + n], nt_dims,
                                              preferred_element_type=F32))
                tn_dims = (((0,), (0,)), ((), ()))
                dwa.append(lax.dot_general(xb16[:, sl], dpr16[:, sl], tn_dims, preferred_element_type=F32)[None])
                dwx.append(lax.dot_general(xb16[:, sl], dpi16[:, sl], tn_dims, preferred_element_type=F32)[None])
            dxc = dxc + jnp.concatenate(back, axis=1)
            dba.append(dba_d)
            dbx.append(dbx_d)
            dsp.append(dsp_d)
        cat0 = lambda xs: jnp.concatenate(xs, axis=0)
        return [dxc], [cat0(dwa), cat0(dwx), cat0(dba), cat0(dbx), cat0(dsp)]

    (dxc,), accs = rowwise("gates_bwd", fn,
                           [(xc, 0, LRU_W), (da_f, 0, LRU_W), (du_f, 0, LRU_W), (da_b, 0, LRU_W), (du_b, 0, LRU_W)],
                           [w_a, w_x, b_a, b_x, sp], [(LRU_W, F32)],
                           [(2 * LRU_BLOCKS, LRU_BW, LRU_BW)] * 2 + [(2, LRU_W)] * 3, n_rows, t_lat, tm)
    return dxc, accs


def _rope_tables(t_lat, n_rows):
    rows = t_lat // GRID_W
    row_ids = jnp.repeat(jnp.arange(rows), GRID_W).astype(F32)
    col_ids = jnp.tile(jnp.arange(GRID_W), rows).astype(F32)
    axis_dim = QK_ROPE // 2
    inv = 1.0 / (ROPE_BASE ** (jnp.arange(0, axis_dim, 2, dtype=F32) / axis_dim))
    ang = jnp.concatenate([row_ids[:, None] * inv, col_ids[:, None] * inv], axis=-1)
    cos, sin = jnp.cos(ang), jnp.sin(ang)
    half = QK_ROPE // 2
    ones, zeros = jnp.ones((t_lat, QK_NOPE), F32), jnp.zeros((t_lat, QK_NOPE), F32)
    pad1, pad0 = jnp.ones((t_lat, HEAD_PAD - QK_DIM), F32), jnp.zeros((t_lat, HEAD_PAD - QK_DIM), F32)
    zh = jnp.zeros((t_lat, half), F32)
    c_tab = jnp.concatenate([ones, cos, cos, pad1], axis=1)
    s1 = jnp.concatenate([zeros, -sin, zh, pad0], axis=1)
    s2 = jnp.concatenate([zeros, zh, sin, pad0], axis=1)
    n_ctx = n_rows - t_lat
    c_tab = jnp.concatenate([c_tab, jnp.ones((n_ctx, HEAD_PAD), F32)], axis=0)
    s1 = jnp.concatenate([s1, jnp.zeros((n_ctx, HEAD_PAD), F32)], axis=0)
    s2 = jnp.concatenate([s2, jnp.zeros((n_ctx, HEAD_PAD), F32)], axis=0)
    return c_tab, s1, s2


def _rope(x, c, s1, s2):
    half = QK_ROPE // 2
    return x * c + pltpu.roll(x, HEAD_PAD - half, 1) * s1 + pltpu.roll(x, half, 1) * s2


def _rope_t(dy, c, s1, s2):
    half = QK_ROPE // 2
    return dy * c + pltpu.roll(dy * s1, half, 1) + pltpu.roll(dy * s2, HEAD_PAD - half, 1)


def _heads(x):
    return [x[:, h * HEAD_PAD:(h + 1) * HEAD_PAD] for h in range(N_HEADS)]


Q_SCALE = QK_DIM ** -0.5 * math.log2(math.e)

def attn_fwd(q, k, v, t_lat, n_rows, tq):
    def body(q_ref, k_ref, v_ref, o_ref, lse_ref):
        s = lax.dot_general(q_ref[...], k_ref[...], (((1,), (1,)), ((), ())), preferred_element_type=F32)
        m = jnp.max(s, axis=-1, keepdims=True)
        p = jnp.exp2(s - m)
        l = jnp.sum(p, axis=-1, keepdims=True)
        o = jnp.dot(p.astype(BF16), v_ref[...], preferred_element_type=F32) / l
        o_ref[...] = o.astype(o_ref.dtype)
        lse_ref[...] = jnp.broadcast_to(m + jnp.log2(l), lse_ref.shape)

    qspec = pl.BlockSpec((tq, HEAD_PAD), lambda h, i: (i, h))
    kspec = pl.BlockSpec((n_rows, HEAD_PAD), lambda h, i: (0, h))
    return pl.pallas_call(
        body, name="attn_fwd", grid=(N_HEADS, t_lat // tq),
        out_shape=[jax.ShapeDtypeStruct((t_lat, N_HEADS * HEAD_PAD), BF16),
                   jax.ShapeDtypeStruct((t_lat, N_HEADS * HEAD_PAD), F32)],
        in_specs=[qspec, kspec, kspec], out_specs=[qspec, qspec],
        compiler_params=_cparams(("parallel", "arbitrary")),
    )(q, k, v)


def attn_bwd(q, k, v, o, do, lse, t_lat, n_rows, tq):
    scale = QK_DIM ** -0.5
    nq = t_lat // tq
    nt = (((1,), (1,)), ((), ()))
    tn = (((0,), (0,)), ((), ()))

    def body(q_ref, k_ref, v_ref, o_ref, do_ref, lse_ref, dq_ref, dk_ref, dv_ref):
        @pl.when(pl.program_id(1) == 0)
        def _():
            dk_ref[...] = jnp.zeros_like(dk_ref)
            dv_ref[...] = jnp.zeros_like(dv_ref)

        qv, kv, vv, dov = q_ref[...], k_ref[...], v_ref[...], do_ref[...]
        s = lax.dot_general(qv, kv, nt, preferred_element_type=F32)
        p = jnp.exp2(s - lse_ref[:, 0:1])
        dv_ref[...] += lax.dot_general(p.astype(BF16), dov, tn, preferred_element_type=F32)
        dp = lax.dot_general(dov, vv, nt, preferred_element_type=F32)
        delta = jnp.sum(dov.astype(F32) * o_ref[...].astype(F32), axis=-1, keepdims=True)
        ds = (p * (dp - delta)).astype(BF16)
        dq_ref[...] = (jnp.dot(ds, kv, preferred_element_type=F32) * scale).astype(dq_ref.dtype)
        dk_ref[...] += lax.dot_general(ds, qv, tn, preferred_element_type=F32)

        @pl.when(pl.program_id(1) == nq - 1)
        def _():
            dk_ref[...] = dk_ref[...] * (scale / Q_SCALE)

    qspec = pl.BlockSpec((tq, HEAD_PAD), lambda h, i: (i, h))
    kspec = pl.BlockSpec((n_rows, HEAD_PAD), lambda h, i: (0, h))
    return pl.pallas_call(
        body, name="attn_bwd", grid=(N_HEADS, t_lat // tq),
        out_shape=[jax.ShapeDtypeStruct((t_lat, N_HEADS * HEAD_PAD), BF16),
                   jax.ShapeDtypeStruct((n_rows, N_HEADS * HEAD_PAD), F32),
                   jax.ShapeDtypeStruct((n_rows, N_HEADS * HEAD_PAD), F32)],
        in_specs=[qspec, kspec, kspec, qspec, qspec, qspec], out_specs=[qspec, kspec, kspec],
        compiler_params=_cparams(("parallel", "arbitrary")),
    )(q, k, v, o, do, lse)


def adamw(name, w, g, m, v):
    r, ccols = w.shape
    if r % 8 == 0:
        tr, tcol = _best_div(r, 8, max(8, 262144 // ccols)), ccols
    else:
        tr, tcol = r, _pick(ccols, (256, 128))
    c1 = 1.0 - ADAM_B1 ** ADAM_STEP
    c2 = 1.0 - ADAM_B2 ** ADAM_STEP

    def body(w_ref, g_ref, m_ref, v_ref, d_ref, nm_ref, nv_ref):
        gv = g_ref[...]
        nm = ADAM_B1 * m_ref[...] + (1.0 - ADAM_B1) * gv
        nv = ADAM_B2 * v_ref[...] + (1.0 - ADAM_B2) * (gv * gv)
        d_ref[...] = -ADAM_LR * ((nm / c1) / (jnp.sqrt(nv / c2) + ADAM_EPS) + ADAM_WD * w_ref[...])
        nm_ref[...] = nm
        nv_ref[...] = nv

    spec = pl.BlockSpec((tr, tcol), lambda i, j: (i, j))
    return pl.pallas_call(
        body, name=name, grid=(r // tr, ccols // tcol),
        out_shape=[jax.ShapeDtypeStruct((r, ccols), F32)] * 3,
        in_specs=[spec] * 4, out_specs=[spec] * 3,
        compiler_params=_cparams(("parallel", "parallel")),
    )(w, g, m, v)


def adamw_many(name, ws, gs, ms, vs):
    n = len(ws)
    c1 = 1.0 - ADAM_B1 ** ADAM_STEP
    c2 = 1.0 - ADAM_B2 ** ADAM_STEP

    def body(*refs):
        for i in range(n):
            w_ref, g_ref, m_ref, v_ref = (refs[k * n + i] for k in range(4))
            d_ref, nm_ref, nv_ref = (refs[(4 + k) * n + i] for k in range(3))
            gv = g_ref[...]
            nm = ADAM_B1 * m_ref[...] + (1.0 - ADAM_B1) * gv
            nv = ADAM_B2 * v_ref[...] + (1.0 - ADAM_B2) * (gv * gv)
            d_ref[...] = -ADAM_LR * ((nm / c1) / (jnp.sqrt(nv / c2) + ADAM_EPS) + ADAM_WD * w_ref[...])
            nm_ref[...] = nm
            nv_ref[...] = nv

    vmem = pl.BlockSpec(memory_space=pltpu.VMEM)
    res = pl.pallas_call(
        body, name=name,
        out_shape=[jax.ShapeDtypeStruct(w.shape, F32) for w in ws] * 3,
        in_specs=[vmem] * (4 * n), out_specs=[vmem] * (3 * n),
        compiler_params=_cparams(),
    )(*ws, *gs, *ms, *vs)
    return [tuple(res[k * n + i] for k in range(3)) for i in range(n)]


def _flat(parts, dtype, row_mult):
    v = jnp.concatenate([p.reshape(-1).astype(dtype) for p in parts])
    quantum = row_mult * FLAT_C
    total = -(-v.shape[0] // quantum) * quantum
    return jnp.pad(v, (0, total - v.shape[0])).reshape(total // FLAT_C, FLAT_C)


def _gathered_to_full(name, g):
    k = g.shape[1]
    return jnp.transpose(g, (1, 0, 2)).reshape(k, N_DEV * g.shape[2])


def _shard_to_rb(name, w):
    return w if name in ROW_SHARDED else w.T


def _rb_to_shard(name, g):
    return g if name in ROW_SHARDED else g.T


def _rb_from_gathered(name, g):
    cols = g.shape[2]
    if name == 'w_in':
        z = lambda k: jnp.zeros((k, cols), g.dtype)
        full = g.reshape(N_DEV * g.shape[1], cols)
        return jnp.concatenate([full[:Z_KR], z(QK_NOPE), full[Z_KR:Z_KR + QK_ROPE], z(HEAD_PAD - QK_DIM),
                                full[Z_KR + QK_ROPE:]], axis=0)
    if name == 'w_uq':
        return jnp.pad(g, ((0, 0), (0, HEAD_PAD - QK_DIM), (0, 0))).reshape(N_HEADS * HEAD_PAD, cols)
    if name == 'w_ukv':
        pad = lambda t: jnp.pad(t, ((0, 0), (0, HEAD_PAD - t.shape[1]), (0, 0))).reshape(N_HEADS * HEAD_PAD, cols)
        return jnp.concatenate([pad(g[:, :QK_NOPE]), pad(g[:, QK_NOPE:])], axis=0)
    if name == 'w_o_attn':
        full = g.reshape(D, N_HEADS, V_HEAD)
        return jnp.pad(full, ((0, 0), (0, 0), (0, HEAD_PAD - V_HEAD))).reshape(D, N_HEADS * HEAD_PAD)
    return g.reshape(N_DEV * g.shape[1], cols)


def _chunks_from_rb_grad(name, g):
    cols = g.shape[1]
    if name == 'w_in':
        full = jnp.concatenate([g[:Z_KR], g[Z_KR + QK_NOPE:Z_KR + QK_DIM], g[Z_XB:]], axis=0)
        return full.reshape(N_DEV, -1, cols)
    if name == 'w_uq':
        return g.reshape(N_HEADS, HEAD_PAD, cols)[:, :QK_DIM]
    if name == 'w_ukv':
        half = N_HEADS * HEAD_PAD
        gk = g[:half].reshape(N_HEADS, HEAD_PAD, cols)[:, :QK_NOPE]
        gv = g[half:].reshape(N_HEADS, HEAD_PAD, cols)[:, :V_HEAD]
        return jnp.concatenate([gk, gv], axis=1)
    if name == 'w_o_attn':
        full = g.reshape(D, N_HEADS, HEAD_PAD)[:, :, :V_HEAD].reshape(D, N_HEADS * V_HEAD)
        return full.reshape(N_DEV, D // N_DEV, N_HEADS * V_HEAD)
    return g.reshape(N_DEV, -1, cols)


def local_step(x, ctx, target, mod_l, mod_c, wt, on_grad=None, arrive=None):
    t_lat, n_ctx = x.shape[0], ctx.shape[0]
    n = t_lat + n_ctx
    tm = _pick(math.gcd(t_lat, n), (256, 128))
    tq_fwd = _pick(t_lat, (256, 128))
    tq_bwd = _pick(t_lat, (512, 256, 128))
    row = lambda v: v.reshape(1, -1).astype(F32)
    two = lambda a, b: jnp.stack([a, b]).astype(F32)
    sh1_l, sc1_l, g1_l, sh2_l, sc2_l, g2_l = jnp.split(mod_l, 6)
    sh1_c, sc1_c = jnp.split(mod_c, 6)[:2]
    sc1, sh1 = two(sc1_l, sc1_c), two(sh1_l, sh1_c)
    g1, g2, sc2, sh2 = row(g1_l), row(g2_l), row(sc2_l), row(sh2_l)
    norm1_g, norm2_g, final_g = row(wt['norm1_g']), row(wt['norm2_g']), row(wt['final_g'])
    q_g, kv_g, b_gate = row(wt['q_norm_g']), row(wt['kv_norm_g']), row(wt['b_gate'])
    wt = dict(wt)
    pending = []

    def sent():
        tokens = list(pending)
        pending.clear()
        return tokens

    def need(names, after):
        if arrive is not None:
            got = arrive(names, after)
            if '_token' in got:
                pending.append(got.pop('_token'))
            wt.update(got)
        return [wt[n] for n in names]
    lru_w_a = wt['lru_w_a'].reshape(2 * LRU_BLOCKS, LRU_BW, LRU_BW).astype(BF16)
    lru_w_x = wt['lru_w_x'].reshape(2 * LRU_BLOCKS, LRU_BW, LRU_BW).astype(BF16)
    b_a, b_x, lam = wt['lru_b_a'], wt['lru_b_x'], wt['lru_lambda']
    sp = jnp.logaddexp(-lam, 0.0)
    c_tab, s1_tab, s2_tab = _rope_tables(t_lat, n)
    rw = functools.partial(rowwise, n_rows=n, t_lat=t_lat, tm=tm)
    rw_lat = functools.partial(rowwise, n_rows=t_lat, t_lat=t_lat, tm=tm)

    stream = [(x, 0, D), (ctx, 0, D, 'ctx')]

    def f_norm1(is_ctx, rows, params):
        (xl, xc_), (g, sc, sh) = rows, params
        return [_norm_mod(jnp.where(is_ctx, xc_, xl), g, _sel(is_ctx, sc), _sel(is_ctx, sh))], []

    (h,), _ = rw("norm1", f_norm1, stream, [norm1_g, sc1, sh1], [(D, BF16)], [])
    (w_in_t,) = need(('w_in',), h)
    z = matmul("w_in", h, w_in_t, 'nt', BF16, after=sent())
    w_uq_t, w_ukv_t, w_o_lru = need(('w_uq', 'w_ukv', 'w_o_lru'), z)

    def f_qkv_norm(is_ctx, rows, params):
        (ql, kvl), (gq, gkv) = rows, params
        return [_rms(ql, gq), _rms(kvl, gkv)], []

    (qn, kvn), _ = rw("qkv_norm", f_qkv_norm, [(z, Z_Q, Q_RANK), (z, Z_KV, KV_RANK)], [q_g, kv_g],
                      [(Q_RANK, BF16), (KV_RANK, BF16)], [])
    qp = matmul("w_uq", qn, w_uq_t, 'nt', BF16)
    kvp = matmul("w_ukv", kvn, w_ukv_t, 'nt', BF16)

    def f_rope(is_ctx, rows, params):
        qv, kk, vv, kr, c, s1, s2 = rows
        krr = _rope(kr, c, s1, s2)
        qo = jnp.concatenate([_rope(qh, c, s1, s2) for qh in _heads(qv)], axis=1) * Q_SCALE
        ko = jnp.concatenate([kh + krr for kh in _heads(kk)], axis=1)
        return [qo, ko, vv], []

    hp = N_HEADS * HEAD_PAD
    (qr, kr_, vr), _ = rw("rope", f_rope,
                          [(qp, 0, hp), (kvp, 0, hp), (kvp, hp, hp), (z, Z_KR, HEAD_PAD), (c_tab, 0, HEAD_PAD),
                           (s1_tab, 0, HEAD_PAD), (s2_tab, 0, HEAD_PAD)], [], [(hp, BF16)] * 3, [])
    attn, lse = attn_fwd(qr, kr_, vr, t_lat, n, tq_fwd)

    xc = conv_fwd("lru_conv", z, Z_XB, LRU_W, wt['lru_conv_w'], row(wt['lru_conv_b']), 2, n, t_lat, F32)
    a_f, u_f, a_b, u_b = gates_fwd(xc, lru_w_a, lru_w_x, b_a, b_x, sp, n, t_lat, tm)
    h_f, hp_f = scan_fwd("scan_f", a_f, u_f, 'f', n, t_lat)
    h_b, hp_b = scan_fwd("scan_b", a_b, u_b, 'b', n, t_lat)

    def f_lru_out(is_ctx, rows, params):
        hf, hb, yb = rows
        return [(hf + hb) * _gelu(yb)], []

    (ybin,), _ = rw_lat("lru_out", f_lru_out, [(h_f, 0, LRU_W), (h_b, 0, LRU_W), (z, Z_YB, LRU_W)], [],
                        [(LRU_W, BF16)], [])
    w_o_attn_t, w_out, w_up_t, w_down = need(('w_o_attn', 'w_out', 'w_up', 'w_down'), attn)
    y_a = matmul("w_o_attn", attn, w_o_attn_t, 'nt', BF16)
    y_b = matmul("w_o_lru", ybin, w_o_lru, 'nn', BF16)

    def _merge(ya, yb, gl, bg):
        gates = _sigmoid(gl + bg)
        return gates[:, :D] * ya + gates[:, D:] * yb

    def f_merge(is_ctx, rows, params):
        (ya, yb, gl), (bg,) = rows, params
        return [_merge(ya, yb, gl, bg)], []

    (mrg,), _ = rw_lat("merge", f_merge, [(y_a, 0, D), (y_b, 0, D), (z, Z_GL, 2 * D)], [b_gate], [(D, BF16)], [])
    o = matmul("w_out", mrg, w_out, 'nn', BF16)

    def _res_norm2(xv, ov, g1v, g, sc, sh):
        x1 = xv + g1v * ov
        return x1, _norm_mod(x1, g, sc, sh)

    def f_norm2(is_ctx, rows, params):
        (xv, ov), (g1v, g, sc, sh) = rows, params
        x1, h2v = _res_norm2(xv, ov, g1v, g, sc, sh)
        return [x1, h2v], []

    (x1, h2), _ = rw_lat("norm2", f_norm2, [(x, 0, D), (o, 0, D)], [g1, norm2_g, sc2, sh2], [(D, F32), (D, BF16)], [])
    u = matmul("w_up", h2, w_up_t, 'nt', BF16)
    f = ffn_mix_fwd(u, wt['ffn_conv_w'], row(wt['ffn_conv_b']), t_lat)
    dn = matmul("w_down", f, w_down, 'nn', BF16)

    def _tile_loss(x1v, dv, g2v, fg, tgt):
        y = _rms(x1v + g2v * dv, fg)
        e = y - tgt
        return 0.5 * jnp.sum(jnp.mean(e * e, axis=-1, keepdims=True), axis=0, keepdims=True)

    def f_final(is_ctx, rows, params):
        (x1v, dv, tgt), (g2v, fg) = rows, params
        lv, vjp = jax.vjp(lambda a, b, c, d: _tile_loss(a, b, c, d, tgt), x1v, dv, g2v, fg)
        dx2, dd, dg2, dfg = vjp(jnp.ones((1, 1), F32))
        return [dx2, dd], [dg2, dfg, jnp.broadcast_to(lv, (1, 128))]

    (dx2, dd), (dg2, dfinal_g, loss_v) = rw_lat("final", f_final, [(x1, 0, D), (dn, 0, D), (target, 0, D)],
                                                [g2, final_g], [(D, F32), (D, BF16)], [(1, D), (1, D), (1, 128)])
    loss = loss_v[0, 0]

    grads = {'final_g': dfinal_g}

    def put(name, g):
        grads[name] = g
        if on_grad is not None:
            pending.append(on_grad(name, g))
    df = matmul("d_f", dd, w_down, 'nt', BF16)
    put('w_down', matmul("g_w_down", f, dd, 'tn', BF16))

    du, grads['ffn_conv_w'], grads['ffn_conv_b'] = ffn_mix_bwd(u, df, wt['ffn_conv_w'], row(wt['ffn_conv_b']),
                                                               t_lat)
    dh2 = matmul("d_h2", du, w_up_t, 'nn', BF16, after=sent())
    put('w_up', matmul("g_w_up", du, h2, 'tn', BF16))

    def b_norm2(is_ctx, rows, params):
        (xv, ov, dh2v, dx2v), (g1v, g, sc, sh) = rows, params
        _, vjp = jax.vjp(_res_norm2, xv, ov, g1v, g, sc, sh)
        dx, do, dg1v, dg, dsc, dsh = vjp((dx2v, dh2v))
        return [dx, do], [dg1v, dg, dsc, dsh]

    (dx_res, do), (dg1, dnorm2_g, dsc2, dsh2) = rw_lat(
        "norm2_bwd", b_norm2, [(x, 0, D), (o, 0, D), (dh2, 0, D), (dx2, 0, D)], [g1, norm2_g, sc2, sh2],
        [(D, F32), (D, BF16)], [(1, D)] * 4)
    grads['norm2_g'] = dnorm2_g
    dmrg = matmul("d_merge", do, w_out, 'nt', BF16, after=sent())
    put('w_out', matmul("g_w_out", mrg, do, 'tn', BF16))

    def b_merge(is_ctx, rows, params):
        (ya, yb, gl, dm), (bg,) = rows, params
        _, vjp = jax.vjp(_merge, ya, yb, gl, bg)
        dya, dyb, dgl, dbg = vjp(dm)
        return [dya, dyb, dgl], [dbg]

    (dy_a, dy_b, dgl), (grads['b_gate'],) = rw_lat(
        "merge_bwd", b_merge, [(y_a, 0, D), (y_b, 0, D), (z, Z_GL, 2 * D), (dmrg, 0, D)], [b_gate],
        [(D, BF16), (D, BF16), (2 * D, BF16)], [(1, 2 * D)])
    dattn = matmul("d_attn", dy_a, w_o_attn_t, 'nn', BF16, after=sent())
    put('w_o_attn', matmul("g_w_o_attn", dy_a, attn, 'tn', BF16))
    dybin = matmul("d_lru_out", dy_b, w_o_lru, 'nt', BF16, after=sent())
    put('w_o_lru', matmul("g_w_o_lru", ybin, dy_b, 'tn', BF16))

    def b_lru_out(is_ctx, rows, params):
        hf, hb, yb, dyv = rows
        _, vjp = jax.vjp(lambda s, y: s * _gelu(y), hf + hb, yb)
        dh, dyb = vjp(dyv)
        return [dh, dyb], []

    (dh_lru, dyb), _ = rw_lat("lru_out_bwd", b_lru_out,
                              [(h_f, 0, LRU_W), (h_b, 0, LRU_W), (z, Z_YB, LRU_W), (dybin, 0, LRU_W)], [],
                              [(LRU_W, F32), (LRU_W, BF16)], [])
    du_f, da_f = scan_adj("scan_f_adj", a_f, dh_lru, hp_f, 'f', n, t_lat)
    du_b, da_b = scan_adj("scan_b_adj", a_b, dh_lru, hp_b, 'b', n, t_lat)
    dxc, (dw_a, dw_x, db_a, db_x, dsp) = gates_bwd(xc, da_f, du_f, da_b, du_b, lru_w_a, lru_w_x, b_a, b_x, sp,
                                                   n, t_lat, tm)
    put('lru_w_a', dw_a.reshape(2 * LRU_BLOCKS * LRU_BW, LRU_BW).astype(BF16))
    put('lru_w_x', dw_x.reshape(2 * LRU_BLOCKS * LRU_BW, LRU_BW).astype(BF16))
    grads['lru_b_a'], grads['lru_b_x'] = db_a, db_x
    grads['lru_lambda'] = -dsp * _sigmoid(-lam)
    dxb, grads['lru_conv_w'], grads['lru_conv_b'] = conv_bwd("lru_conv_bwd", dxc, z, Z_XB, LRU_W, wt['lru_conv_w'],
                                                             2, n, t_lat)

    dq, dk, dv = attn_bwd(qr, kr_, vr, attn, dattn, lse, t_lat, n, tq_bwd)

    def b_rope(is_ctx, rows, params):
        dqv, dkv, dvv, c, s1, s2 = rows
        live = jnp.where(is_ctx, 0.0, 1.0)
        dqo = jnp.concatenate([_rope_t(dqh, c, s1, s2) for dqh in _heads(dqv)], axis=1) * live
        dkh = _heads(dkv)
        dkr = dkh[0]
        for t in dkh[1:]:
            dkr = dkr + t
        lanes = lax.broadcasted_iota(jnp.int32, dkr.shape, 1)
        dkr = jnp.where((lanes >= QK_NOPE) & (lanes < QK_DIM), _rope_t(dkr, c, s1, s2), 0.0)
        return [dqo, jnp.concatenate([dkv, dvv], axis=1), dkr], []

    (dqp, dkvp, dkr), _ = rw("rope_bwd", b_rope,
                             [(dq, 0, hp), (dk, 0, hp), (dv, 0, hp), (c_tab, 0, HEAD_PAD), (s1_tab, 0, HEAD_PAD),
                              (s2_tab, 0, HEAD_PAD)], [], [(hp, BF16), (2 * hp, BF16), (HEAD_PAD, BF16)], [])
    dqn = matmul("d_qn", dqp, w_uq_t, 'nn', BF16, after=sent())
    put('w_uq', matmul("g_w_uq", dqp, qn, 'tn', BF16))
    dkvn = matmul("d_kvn", dkvp, w_ukv_t, 'nn', BF16, after=sent())
    put('w_ukv', matmul("g_w_ukv", dkvp, kvn, 'tn', BF16))

    def b_qkv_norm(is_ctx, rows, params):
        (ql, kvl, dqv, dkvv), (gq, gkv) = rows, params
        _, vjp_q = jax.vjp(_rms, ql, gq)
        _, vjp_kv = jax.vjp(_rms, kvl, gkv)
        dql, dgq = vjp_q(dqv)
        dkvl, dgkv = vjp_kv(dkvv)
        return [dql, dkvl], [dgq, dgkv]

    (dq_lat, dkv_lat), (grads['q_norm_g'], grads['kv_norm_g']) = rw(
        "qkv_norm_bwd", b_qkv_norm, [(z, Z_Q, Q_RANK), (z, Z_KV, KV_RANK), (dqn, 0, Q_RANK), (dkvn, 0, KV_RANK)],
        [q_g, kv_g], [(Q_RANK, BF16), (KV_RANK, BF16)], [(1, Q_RANK), (1, KV_RANK)])
    pad_ctx = lambda t: jnp.pad(t, ((0, n_ctx), (0, 0)))
    dz = jnp.concatenate([dq_lat, dkv_lat, dkr, dxb, pad_ctx(dyb), pad_ctx(dgl)], axis=1)
    put('w_in', matmul("g_w_in", dz, h, 'tn', BF16))
    dh = matmul("d_h", dz, w_in_t, 'nn', BF16, after=sent())

    def b_norm1(is_ctx, rows, params):
        (xl, xc_, dhv, dxr), (g, sc, sh) = rows, params
        scv, shv = _sel(is_ctx, sc), _sel(is_ctx, sh)
        _, vjp = jax.vjp(_norm_mod, jnp.where(is_ctx, xc_, xl), g, scv, shv)
        dx, dg, dsc, dsh = vjp(dhv)
        return [dx + dxr], [dg, _seg_acc(is_ctx, dsc), _seg_acc(is_ctx, dsh)]

    (grad_x,), (grads['norm1_g'], dsc1, dsh1) = rw("norm1_bwd", b_norm1, stream + [(dh, 0, D), (dx_res, 0, D)],
                                                   [norm1_g, sc1, sh1], [(D, F32, 'lat')],
                                                   [(1, D), (2, D), (2, D)])
    zero = jnp.zeros((D,), F32)
    dmod_l = jnp.concatenate([dsh1[0], dsc1[0], dg1[0], dsh2[0], dsc2[0], dg2[0]])
    dmod_c = jnp.concatenate([dsh1[1], dsc1[1], zero, zero, zero, zero])
    return loss, grad_x, grads, dmod_l, dmod_c


def kernel(x, c, ctx, c_ctx, w_mod, b_mod, norm1_g, w_in, b_gate, q_norm_g, kv_norm_g, w_uq, w_ukv, w_o_attn, lru_conv_w, lru_conv_b, lru_w_a, lru_b_a, lru_w_x, lru_b_x, lru_lambda, w_o_lru, w_out, norm2_g, w_up, ffn_conv_w, ffn_conv_b, w_down, final_g, loss_target, m_c_ctx, m_w_mod, m_b_mod, m_norm1_g, m_w_in, m_b_gate, m_q_norm_g, m_kv_norm_g, m_w_uq, m_w_ukv, m_w_o_attn, m_lru_conv_w, m_lru_conv_b, m_lru_w_a, m_lru_b_a, m_lru_w_x, m_lru_b_x, m_lru_lambda, m_w_o_lru, m_w_out, m_norm2_g, m_w_up, m_ffn_conv_w, m_ffn_conv_b, m_w_down, m_final_g, v_c_ctx, v_w_mod, v_b_mod, v_norm1_g, v_w_in, v_b_gate, v_q_norm_g, v_kv_norm_g, v_w_uq, v_w_ukv, v_w_o_attn, v_lru_conv_w, v_lru_conv_b, v_lru_w_a, v_lru_b_a, v_lru_w_x, v_lru_b_x, v_lru_lambda, v_w_o_lru, v_w_out, v_norm2_g, v_w_up, v_ffn_conv_w, v_ffn_conv_b, v_w_down, v_final_g):
    given = dict(locals())
    strip = lambda name, a: a if name in ('c_ctx', 'final_g') else a[0]
    wsh = {n: strip(n, given[n]) for n in WEIGHTS}
    msh = {n: strip(n, given['m_' + n]) for n in WEIGHTS}
    vsh = {n: strip(n, given['v_' + n]) for n in WEIGHTS}
    me = _my_index()

    small = _flat([c[0]] + [wsh[n] for n in SMALL_F32], F32, 8)
    small_all = all_gather("gather_small", small).reshape(N_DEV, -1)
    c_all = small_all[:, :D]
    full, at = {}, D
    for n in SMALL_F32:
        cnt = math.prod(wsh[n].shape)
        full[n] = _gathered_to_full(n, small_all[:, at:at + cnt].reshape((N_DEV,) + wsh[n].shape))
        at += cnt

    cond = jnp.concatenate([c_all, c_ctx[None], jnp.zeros((7, D), F32)], axis=0)
    sil = cond * jax.nn.sigmoid(cond)
    mod_cols = matmul("mod_proj", sil, wsh['w_mod'], 'nn', F32)
    mod_all = all_gather("gather_mod", mod_cols)
    mod_all = jnp.transpose(mod_all, (1, 0, 2)).reshape(16, 6 * D) + b_mod[0][None]
    mod_l = lax.dynamic_index_in_dim(mod_all, me, axis=0, keepdims=False)
    mod_c = mod_all[N_DEV]

    rb_shards = {n: _shard_to_rb(n, wsh[n]).astype(BF16) for n in BIG_BF16}
    (w_in_blocks,) = all_gather_multi("gather_w_in", [rb_shards['w_in']])
    later = [n for n in BIG_BF16 if n != 'w_in']
    weights_started, weights_sent = exchange_start("weights_send", 'gather', [rb_shards[n] for n in later],
                                                   after=[w_in_blocks, mod_all])
    for n in REPLICATED:
        if n not in ('c_ctx', 'b_mod'):
            full[n] = wsh[n]

    def arrive(names, after):
        if names == ('w_in',):
            return {'w_in': _rb_from_gathered('w_in', w_in_blocks), '_token': weights_sent}
        picked = [later.index(n) for n in names]
        lands = exchange_wait("weights_wait_" + names[0], 'gather',
                              tuple([part[i] for i in picked] for part in weights_started), after)
        return {n: _rb_from_gathered(n, lax.dynamic_update_slice_in_dim(land, rb_shards[n][None], me, axis=0))
                for n, land in zip(names, lands)}

    in_flight = {}

    def on_grad(n, g):
        chunks = _chunks_from_rb_grad(n, g)
        own = lax.dynamic_index_in_dim(chunks, me, axis=0, keepdims=True)
        started, token = exchange_start("grad_send_" + n, 'scatter', [chunks])
        in_flight[n] = (own, started)
        return token

    loss, grad_x, grads, dmod_l, dmod_c = local_step(x[0], ctx[0], loss_target[0], mod_l, mod_c, full, on_grad,
                                                     arrive)
    dmod = jnp.stack([dmod_l, dmod_c]).reshape(2 * 6 * D // FLAT_C, FLAT_C)
    dm = all_gather("gather_dmod", dmod).reshape(N_DEV, 2, 6 * D)
    dmod_c_tot = dm[0, 1]
    for p in range(1, N_DEV):
        dmod_c_tot = dmod_c_tot + dm[p, 1]
    dm16 = jnp.concatenate([dm[:, 0], dmod_c_tot[None], jnp.zeros((7, 6 * D), F32)], axis=0)
    ncol = 6 * D // N_DEV
    dm16_cols = lax.dynamic_slice_in_dim(dm16.reshape(16, N_DEV, ncol), me, 1, axis=1)[:, 0]
    grad_w_mod = matmul("g_w_mod", sil, dm16_cols, 'tn', F32)
    dsil = matmul("d_cond", dm16_cols, wsh['w_mod'], 'nt', F32)
    sg = jax.nn.sigmoid(c_ctx)
    grads['c_ctx'] = dsil[N_DEV] * (sg * (1.0 + c_ctx * (1.0 - sg)))
    grads['b_mod'] = dmod_l + dmod_c

    g_final = {'w_mod': grad_w_mod}
    reduced, stepped = {}, {}
    for n in BIG_BF16 + ['lru_w_a', 'lru_w_x']:
        own, started = in_flight[n]
        (land,) = exchange_wait("grad_wait_" + n, 'scatter', started, dm)
        if n in ROW_SHARDED:
            g_final[n], *stepped[n] = reduce_slots("step_" + n, land, own, (wsh[n], msh[n], vsh[n]))
        elif n in COL_SHARDED and wsh[n].shape[1] % 128:
            g_t, *outs = reduce_slots("step_" + n, land, own, (wsh[n].T, msh[n].T, vsh[n].T))
            g_final[n], stepped[n] = g_t.T, [o.T for o in outs]
        else:
            reduced[n] = reduce_slots("sum_" + n, land, own)
            if n in BIG_BF16:
                g_final[n] = _rb_to_shard(n, reduced[n])

    small_names = SMALL_F32 + [n for n in REPLICATED if n not in ('lru_w_a', 'lru_w_x')]
    partials = _flat([grads[n] for n in small_names] + [loss], F32, 8)
    parts_all, a_all, x_all = all_gather_multi("gather_small_grads", [partials, reduced['lru_w_a'], reduced['lru_w_x']])
    small_sum = sum_slots("sum_small", parts_all).reshape(-1)
    g_final['lru_w_a'], g_final['lru_w_x'] = a_all.reshape(wsh['lru_w_a'].shape), x_all.reshape(wsh['lru_w_x'].shape)
    at = 0
    for n in small_names:
        cnt = math.prod(full[n].shape) if n in SMALL_F32 else math.prod(wsh[n].shape)
        g = small_sum[at:at + cnt]
        if n in SMALL_F32:
            k = full[n].shape[0]
            g = lax.dynamic_index_in_dim(g.reshape(k, N_DEV, -1), me, axis=1, keepdims=False)
        g_final[n] = g.reshape(wsh[n].shape)
        at += cnt
    loss = small_sum[at]

    for n in ['w_mod'] + BIG_BF16:
        if n not in stepped:
            stepped[n] = adamw("adamw_" + n, wsh[n], g_final[n], msh[n], vsh[n])
    rest = [n for n in WEIGHTS if n not in stepped]
    as2d = lambda a: a.reshape(-1, a.shape[-1])
    rest_out = adamw_many("adamw_small", *[[as2d(d[n]) for n in rest] for d in (wsh, g_final, msh, vsh)])
    stepped.update(zip(rest, rest_out))
    shaped = lambda n, a: a.reshape(given[n].shape)
    return (loss, grad_x[None],
            *[shaped(n, g_final[n]) for n in WEIGHTS],
            *[shaped(n, stepped[n][k]) for k in range(3) for n in WEIGHTS])
```

```python
import functools
import math

import jax
import jax.numpy as jnp
from jax import lax
from jax.experimental import pallas as pl
from jax.experimental.pallas import tpu as pltpu

F32 = jnp.float32
BF16 = jnp.bfloat16
MESH = pl.DeviceIdType.MESH

N_DEV = 8
D = 1024
N_HEADS = 8
HEAD_PAD = 128
QK_NOPE, QK_ROPE, V_HEAD = 64, 32, 64
QK_DIM = QK_NOPE + QK_ROPE
Q_RANK, KV_RANK = 384, 256
LRU_W, LRU_BLOCKS, LRU_BW = 1280, 10, 128
FFN = 2816
GRID_W = 64
ROPE_BASE = 10000.0
LRU_C = 8.0
EPS = 1e-6
Z_Q, Z_KV, Z_KR, Z_XB, Z_YB, Z_GL, Z_END = 0, 384, 640, 768, 2048, 3328, 5376
ADAM_LR, ADAM_B1, ADAM_B2, ADAM_EPS, ADAM_WD, ADAM_STEP = 0.001, 0.9, 0.999, 1e-08, 0.01, 10

VMEM_LIMIT = 52 * 1024 * 1024
FLAT_C = 512

WEIGHTS = ['c_ctx', 'w_mod', 'b_mod', 'norm1_g', 'w_in', 'b_gate', 'q_norm_g', 'kv_norm_g', 'w_uq', 'w_ukv',
           'w_o_attn', 'lru_conv_w', 'lru_conv_b', 'lru_w_a', 'lru_b_a', 'lru_w_x', 'lru_b_x', 'lru_lambda',
           'w_o_lru', 'w_out', 'norm2_g', 'w_up', 'ffn_conv_w', 'ffn_conv_b', 'w_down', 'final_g']
COL_SHARDED = ['w_in', 'w_uq', 'w_ukv', 'w_o_attn', 'lru_conv_w', 'lru_b_a', 'lru_b_x', 'lru_lambda', 'w_up',
               'ffn_conv_w']
ROW_SHARDED = ['w_o_lru', 'w_out', 'w_down']
BIG_BF16 = ['w_in', 'w_uq', 'w_ukv', 'w_o_attn', 'w_o_lru', 'w_out', 'w_up', 'w_down']
SMALL_F32 = ['lru_conv_w', 'lru_b_a', 'lru_b_x', 'lru_lambda', 'ffn_conv_w']
REPLICATED = ['c_ctx', 'b_mod', 'norm1_g', 'b_gate', 'q_norm_g', 'kv_norm_g', 'lru_conv_b', 'lru_w_a', 'lru_w_x',
              'norm2_g', 'ffn_conv_b', 'final_g']


def _cparams(sem=None):
    return pltpu.CompilerParams(dimension_semantics=sem, vmem_limit_bytes=VMEM_LIMIT)


def _pick(n, cands):
    for c in cands:
        if c <= n and n % c == 0:
            return c
    return n


def _best_div(n, mult, cap):
    best = mult
    for d in range(mult, min(n, cap) + 1, mult):
        if n % d == 0:
            best = d
    return best


MXU_DIM = 256
ROW_TILES = (1088, 1024, 544, 512, 256, 128, 64, 32, 16, 8)
LANE_TILES = (2816, 1792, 1536, 1280, 1024, 768, 512, 256, 1408, 896, 640, 384, 128)
DEPTH_ROW_TILES = (2176, 2048, 1024, 512, 256, 1088, 128, 64, 32, 16, 8)
MATMUL_VMEM_BUDGET = 40 * 1024 * 1024
MXU_FILL_OK = 0.9


def _my_pos():
    return lax.axis_index("x"), lax.axis_index("y"), lax.axis_index("c")


def _my_index():
    x, y, c = _my_pos()
    return 4 * x + 2 * y + c


def all_gather_multi(name, shards):
    n_arr = len(shards)
    arrays = range(n_arr)

    def body(*refs):
        x_refs, out_refs = refs[:n_arr], refs[n_arr:2 * n_arr]
        send_sems, recv_sems, local_sems = refs[2 * n_arr:]
        x, y, c = _my_pos()
        me, sibling = (x, y, c), (x, y, 1 - c)
        chips = [(1 - x, y), (x, 1 - y), (1 - x, 1 - y)]

        def slot(a, px, py, pc):
            return out_refs[a].at[4 * px + 2 * py + pc]

        def copy(a, k, block, to, src=None):
            return pltpu.make_async_remote_copy(
                src_ref=slot(a, *block) if src is None else src, dst_ref=slot(a, *block),
                send_sem=send_sems.at[7 * a + k], recv_sem=recv_sems.at[7 * a + k], device_id=to,
                device_id_type=MESH)

        mine = [pltpu.make_async_copy(x_refs[a], slot(a, *me), local_sems.at[a]) for a in arrays]
        first = [copy(a, 1 + j, me, (*chip, c), src=x_refs[a]) for j, chip in enumerate(chips) for a in arrays]
        first += [copy(a, 0, me, sibling, src=x_refs[a]) for a in arrays]
        for cp in first + mine:
            cp.start()
        passed = []
        for j, chip in enumerate(chips):
            for a in arrays:
                copy(a, 1 + j, (*chip, c), me).wait_recv()
                passed.append(copy(a, 4 + j, (*chip, c), sibling))
                passed[-1].start()
        for a in arrays:
            copy(a, 0, sibling, me).wait_recv()
            for j, chip in enumerate(chips):
                copy(a, 4 + j, (*chip, 1 - c), me).wait_recv()
        for cp in first + passed:
            cp.wait_send()
        for cp in mine:
            cp.wait()

    hbm = pl.BlockSpec(memory_space=pl.ANY)
    return pl.pallas_call(
        body, name=name,
        out_shape=[jax.ShapeDtypeStruct((N_DEV,) + s.shape, s.dtype) for s in shards],
        in_specs=[hbm] * n_arr, out_specs=[hbm] * n_arr,
        scratch_shapes=[pltpu.SemaphoreType.DMA((7 * n_arr,)), pltpu.SemaphoreType.DMA((7 * n_arr,)),
                        pltpu.SemaphoreType.DMA((n_arr,))],
    )(*shards)


def all_gather(name, shard):
    return all_gather_multi(name, [shard])[0]


def _peers():
    x, y, c = _my_pos()
    out = []
    for rel in (6, 4, 2, 7, 5, 3, 1):
        px, py, pc = x ^ ((rel >> 2) & 1), y ^ ((rel >> 1) & 1), c ^ (rel & 1)
        out.append((rel - 1, (px, py, pc), 4 * px + 2 * py + pc))
    return out


def _exchange_copies(mode, src_refs, land_refs, send_sems, recv_sems):
    x, y, c = _my_pos()
    me = 4 * x + 2 * y + c
    sends, arrivals = [], []
    for k, peer_pos, peer in _peers():
        for a, (src, land) in enumerate(zip(src_refs, land_refs)):
            piece = src.at[peer] if mode == 'scatter' else src
            sems = dict(send_sem=send_sems[a].at[k], recv_sem=recv_sems[a].at[k], device_id_type=MESH)
            sends.append(pltpu.make_async_remote_copy(src_ref=piece, dst_ref=land.at[me], device_id=peer_pos, **sems))
            arrivals.append(pltpu.make_async_remote_copy(src_ref=piece, dst_ref=land.at[peer], device_id=(x, y, c), **sems))
    return sends, arrivals


_HBM = pl.BlockSpec(memory_space=pltpu.HBM)
_SEM = pl.BlockSpec(memory_space=pltpu.SEMAPHORE)


def exchange_start(name, mode, arrays, after=()):
    n_arr, n_after = len(arrays), len(after)
    land_shapes = [a.shape if mode == 'scatter' else (N_DEV,) + a.shape for a in arrays]

    def body(*refs):
        src_refs, land_refs = refs[:n_arr], refs[n_arr:2 * n_arr]
        refs = refs[n_after:]
        send_sems, recv_sems = refs[2 * n_arr:3 * n_arr], refs[3 * n_arr:4 * n_arr]
        sends, _ = _exchange_copies(mode, src_refs, land_refs, send_sems, recv_sems)
        for cp in sends:
            cp.start()
        token = refs[-1]
        token[...] = jnp.zeros_like(token)

    sem = pltpu.SemaphoreType.DMA((N_DEV - 1,))
    res = pl.pallas_call(
        body, name=name,
        out_shape=[sem] * (2 * n_arr) + [pltpu.HBM(a.shape, a.dtype) for a in arrays]
        + [pltpu.HBM(s, a.dtype) for s, a in zip(land_shapes, arrays)] + [jax.ShapeDtypeStruct((8, 128), F32)],
        in_specs=[_HBM] * (2 * n_arr) + [pl.BlockSpec(memory_space=pl.ANY)] * n_after,
        out_specs=[_SEM] * (2 * n_arr) + [_HBM] * (2 * n_arr) + [pl.BlockSpec(memory_space=pltpu.VMEM)],
        input_output_aliases={i: 2 * n_arr + i for i in range(2 * n_arr)},
        compiler_params=pltpu.CompilerParams(has_side_effects=pltpu.SideEffectType.DATAFLOW_SIDE_EFFECTING),
    )(*[pltpu.with_memory_space_constraint(a, pltpu.HBM) for a in arrays],
      *[pltpu.with_memory_space_constraint(lax.empty(s, a.dtype), pltpu.HBM) for s, a in zip(land_shapes, arrays)],
      *after)
    return (res[:n_arr], res[n_arr:2 * n_arr], res[2 * n_arr:3 * n_arr], res[3 * n_arr:4 * n_arr]), res[-1]


def exchange_wait(name, mode, started, after):
    send_sems, recv_sems, thru, land = started
    n_arr = len(thru)

    def body(*refs):
        src_refs, land_refs = refs[:n_arr], refs[n_arr:2 * n_arr]
        s_sems, r_sems = refs[2 * n_arr:3 * n_arr], refs[3 * n_arr:4 * n_arr]
        sends, arrivals = _exchange_copies(mode, src_refs, land_refs, s_sems, r_sems)
        for cp in sends:
            cp.wait_send()
        for cp in arrivals:
            cp.wait_recv()

    res = pl.pallas_call(
        body, name=name,
        out_shape=[pltpu.HBM(a.shape, a.dtype) for a in thru] + [pltpu.HBM(a.shape, a.dtype) for a in land],
        in_specs=[_HBM] * (2 * n_arr) + [_SEM] * (2 * n_arr) + [pl.BlockSpec(memory_space=pl.ANY)],
        out_specs=[_HBM] * (2 * n_arr),
        input_output_aliases={i: i for i in range(2 * n_arr)},
        compiler_params=pltpu.CompilerParams(has_side_effects=pltpu.SideEffectType.DATAFLOW_SIDE_EFFECTING),
    )(*thru, *land, *send_sems, *recv_sems, after)
    return res[n_arr:]


def _sum_with_own(slot_ref, own_ref):
    x, y, c = _my_pos()
    me = 4 * x + 2 * y + c
    acc = None
    for p in range(N_DEV):
        v = jnp.where(me == p, own_ref[0], slot_ref[p]).astype(F32)
        acc = v if acc is None else acc + v
    return acc


def reduce_slots(name, slots, own, step=None):
    _, r, ccols = slots.shape
    tc = _pick(ccols, (256, 128))
    c1 = 1.0 - ADAM_B1 ** ADAM_STEP
    c2 = 1.0 - ADAM_B2 ** ADAM_STEP

    def body(s_ref, own_ref, *refs):
        g = _sum_with_own(s_ref, own_ref)
        if step is None:
            refs[0][...] = g
            return
        w_ref, m_ref, v_ref, g_ref, d_ref, nm_ref, nv_ref = refs
        nm = ADAM_B1 * m_ref[...] + (1.0 - ADAM_B1) * g
        nv = ADAM_B2 * v_ref[...] + (1.0 - ADAM_B2) * (g * g)
        g_ref[...] = g
        d_ref[...] = -ADAM_LR * ((nm / c1) / (jnp.sqrt(nv / c2) + ADAM_EPS) + ADAM_WD * w_ref[...])
        nm_ref[...] = nm
        nv_ref[...] = nv

    col = pl.BlockSpec((r, tc), lambda j: (0, j))
    n_out = 1 if step is None else 4
    res = pl.pallas_call(
        body, name=name, grid=(ccols // tc,),
        out_shape=[jax.ShapeDtypeStruct((r, ccols), F32)] * n_out,
        in_specs=[pl.BlockSpec((N_DEV, r, tc), lambda j: (0, 0, j)), pl.BlockSpec((1, r, tc), lambda j: (0, 0, j))]
        + [col] * (0 if step is None else 3),
        out_specs=[col] * n_out,
        compiler_params=_cparams(("parallel",)),
    )(slots, own, *(step or ()))
    return res[0] if step is None else res


def sum_slots(name, slots):
    _, r, ccols = slots.shape
    tc = _pick(ccols, (256, 128))

    def body(s_ref, o_ref):
        acc = s_ref[0].astype(F32)
        for p in range(1, N_DEV):
            acc = acc + s_ref[p].astype(F32)
        o_ref[...] = acc

    return pl.pallas_call(
        body, name=name, grid=(ccols // tc,),
        out_shape=jax.ShapeDtypeStruct((r, ccols), F32),
        in_specs=[pl.BlockSpec((N_DEV, r, tc), lambda j: (0, 0, j))],
        out_specs=pl.BlockSpec((r, tc), lambda j: (0, j)),
        compiler_params=_cparams(("parallel",)),
    )(slots)


def _mxu_fill(t):
    return t / (-(-t // MXU_DIM) * MXU_DIM)


def _matmul_tiles(mode, m_extent, n, k_extent, k_total, itemsizes):
    a_bytes, b_bytes, o_bytes = itemsizes
    m_cands = [c for c in (LANE_TILES if mode == 'tn' else ROW_TILES) if m_extent % c == 0] or [m_extent]
    k_cands = [c for c in (DEPTH_ROW_TILES if mode == 'tn' else LANE_TILES) if k_extent % c == 0] or [k_extent]
    n_cands = [c for c in LANE_TILES if n % c == 0] or [n]
    best = None
    for tm in m_cands:
        for tk in k_cands:
            for tn in n_cands:
                f32_tiles = 2 if k_total // tk > 1 else 1
                vmem = 2 * (tm * tk * a_bytes + tk * tn * b_bytes + tm * tn * o_bytes) + tm * tn * 4 * f32_tiles
                if vmem > MATMUL_VMEM_BUDGET:
                    continue
                key = (_mxu_fill(tn) * _mxu_fill(tk) >= MXU_FILL_OK, tm * tn * tk)
                if best is None or key > best[0]:
                    best = (key, (tm, tn, tk))
    assert best is not None, (mode, m_extent, n, k_extent)
    return best[1]


def matmul(name, a, b, mode, out_dtype, after=()):
    after = [t for t in after if t is not None]
    pieces, a_rows, a_cols = (1,) + a.shape if a.ndim == 2 else a.shape
    if mode == 'nn':
        (m, k), (k2, n) = (a_rows, pieces * a_cols), b.shape
    elif mode == 'nt':
        (m, k), (n, k2) = (a_rows, pieces * a_cols), b.shape
    else:
        (k, m), (k2, n) = (a_rows, pieces * a_cols), b.shape
    assert k == k2, (name, a.shape, b.shape, mode)
    tm, tn, tk = _matmul_tiles(mode, a_cols if mode == 'tn' else m, n, k if mode == 'tn' else a_cols, k,
                               (a.dtype.itemsize, b.dtype.itemsize, jnp.dtype(out_dtype).itemsize))
    nk = k // tk
    per_piece = a_cols // (tm if mode == 'tn' else tk)
    if a.ndim == 2:
        a_block = lambda rows, cols, at: pl.BlockSpec((rows, cols), at)
    else:
        a_block = lambda rows, cols, at: pl.BlockSpec(
            (None, rows, cols), lambda i, j, kk: (at(i, j, kk)[1] // per_piece, at(i, j, kk)[0],
                                                  at(i, j, kk)[1] % per_piece))
    if mode == 'nn':
        a_spec = a_block(tm, tk, lambda i, j, kk: (i, kk))
        b_spec = pl.BlockSpec((tk, tn), lambda i, j, kk: (kk, j))
        dn = (((1,), (0,)), ((), ()))
    elif mode == 'nt':
        a_spec = a_block(tm, tk, lambda i, j, kk: (i, kk))
        b_spec = pl.BlockSpec((tn, tk), lambda i, j, kk: (j, kk))
        dn = (((1,), (1,)), ((), ()))
    else:
        a_spec = a_block(tk, tm, lambda i, j, kk: (kk, i))
        b_spec = pl.BlockSpec((tk, tn), lambda i, j, kk: (kk, j))
        dn = (((0,), (0,)), ((), ()))

    def product(a_ref, b_ref):
        return lax.dot_general(a_ref[...].astype(BF16), b_ref[...].astype(BF16), dn, preferred_element_type=F32)

    n_after = len(after)

    def body_one(a_ref, b_ref, *rest):
        o_ref = rest[n_after]
        o_ref[...] = product(a_ref, b_ref).astype(o_ref.dtype)

    def body(a_ref, b_ref, *rest):
        o_ref, acc_ref = rest[n_after:]
        kk = pl.program_id(2)

        @pl.when(kk == 0)
        def _():
            acc_ref[...] = jnp.zeros_like(acc_ref)

        acc_ref[...] += product(a_ref, b_ref)

        @pl.when(kk == nk - 1)
        def _():
            o_ref[...] = acc_ref[...].astype(o_ref.dtype)

    return pl.pallas_call(
        body_one if nk == 1 else body, name=name, grid=(m // tm, n // tn, nk),
        out_shape=jax.ShapeDtypeStruct((m, n), out_dtype),
        in_specs=[a_spec, b_spec] + [pl.BlockSpec(memory_space=pl.ANY)] * n_after,
        out_specs=pl.BlockSpec((tm, tn), lambda i, j, kk: (i, j)),
        scratch_shapes=[] if nk == 1 else [pltpu.VMEM((tm, tn), F32)],
        compiler_params=_cparams(("parallel", "parallel", "arbitrary")),
    )(a, b, *after)


def rowwise(name, fn, rows, params, out_rows, out_accs, n_rows, t_lat, tm):
    nb, nbl = n_rows // tm, t_lat // tm
    in_specs, piece_counts = [], []
    operands = []
    for arr, off, width, *kind in rows:
        g = math.gcd(off, width) if off else width
        assert g % 128 == 0 or (off == 0 and width == arr.shape[1]), (name, off, width)
        cnt = width // g
        last = arr.shape[0] // tm - 1
        clamp = arr.shape[0] < n_rows
        for p in range(cnt):
            cb = off // g + p
            if kind == ['ctx']:
                in_specs.append(pl.BlockSpec(
                    (tm, g), lambda i, cb=cb, last=last: (jnp.clip(i - nbl, 0, last), cb)))
            elif clamp:
                in_specs.append(pl.BlockSpec((tm, g), lambda i, cb=cb, last=last: (jnp.minimum(i, last), cb)))
            else:
                in_specs.append(pl.BlockSpec((tm, g), lambda i, cb=cb: (i, cb)))
            operands.append(arr)
        piece_counts.append(cnt)
    for p in params:
        in_specs.append(pl.BlockSpec(p.shape, lambda i, nd=p.ndim: (0,) * nd))
        operands.append(p)
    n_in = sum(piece_counts)
    n_par = len(params)
    n_or = len(out_rows)
    lat_only = [kind == ['lat'] for _, _, *kind in out_rows]
    out_shape = [jax.ShapeDtypeStruct((t_lat if lat else n_rows, w), dt)
                 for (w, dt, *_), lat in zip(out_rows, lat_only)]
    out_shape += [jax.ShapeDtypeStruct(s, F32) for s in out_accs]
    out_specs = [pl.BlockSpec((tm, w), (lambda i: (jnp.minimum(i, nbl - 1), 0)) if lat else (lambda i: (i, 0)))
                 for (w, *_), lat in zip(out_rows, lat_only)]
    out_specs += [pl.BlockSpec(s, lambda i, nd=len(s): (0,) * nd) for s in out_accs]

    def body(*refs):
        in_refs, par_refs = refs[:n_in], refs[n_in:n_in + n_par]
        orow_refs = refs[n_in + n_par:n_in + n_par + n_or]
        oacc_refs = refs[n_in + n_par + n_or:]
        i = pl.program_id(0)
        tiles, at = [], 0
        for cnt in piece_counts:
            parts = [in_refs[at + p][...].astype(F32) for p in range(cnt)]
            tiles.append(parts[0] if cnt == 1 else jnp.concatenate(parts, axis=1))
            at += cnt
        is_ctx = i * tm >= t_lat
        outs, accs = fn(is_ctx, tiles, [p[...] for p in par_refs])
        for o_ref, o, lat in zip(orow_refs, outs, lat_only):
            if lat:
                @pl.when(jnp.logical_not(is_ctx))
                def _(o_ref=o_ref, o=o):
                    o_ref[...] = o.astype(o_ref.dtype)
            else:
                o_ref[...] = o.astype(o_ref.dtype)
        if oacc_refs:
            @pl.when(i == 0)
            def _():
                for a_ref in oacc_refs:
                    a_ref[...] = jnp.zeros_like(a_ref)
            for a_ref, a in zip(oacc_refs, accs):
                a_ref[...] += a.astype(F32)

    res = pl.pallas_call(
        body, name=name, grid=(nb,),
        out_shape=out_shape, in_specs=in_specs, out_specs=out_specs,
        compiler_params=_cparams(("arbitrary",)),
    )(*operands)
    return res[:n_or], res[n_or:]


def _rms(x, g):
    return x * lax.rsqrt(jnp.mean(x * x, axis=-1, keepdims=True) + EPS) * g


def _norm_mod(x, g, sc, sh):
    return _rms(x, g) * (1.0 + sc) + sh


def _sigmoid(x):
    return 0.5 * jnp.tanh(0.5 * x) + 0.5


def _silu(x):
    return x * _sigmoid(x)


def _gelu(x):
    return 0.5 * x * (1.0 + jnp.tanh(math.sqrt(2.0 / math.pi) * (x + 0.044715 * (x * x * x))))


def _sel(is_ctx, p):
    return jnp.where(is_ctx, p[1:2], p[0:1])


def _seg_acc(is_ctx, v):
    rows = lax.broadcasted_iota(jnp.int32, (2, v.shape[1]), 0)
    return jnp.where(rows == is_ctx.astype(jnp.int32), jnp.broadcast_to(v, (2, v.shape[1])), 0.0)


def _rsum(v):
    return jnp.sum(v, axis=0, keepdims=True)


def _shift_rows(x, o, t_lat, n):
    if o == 0:
        return x
    y = pltpu.roll(x, (-o) % n, 0)
    t = lax.broadcasted_iota(jnp.int32, x.shape, 0)
    if o > 0:
        ok = t < n - o
        if t_lat < n:
            ok = ok & ((t < t_lat - o) | (t >= t_lat))
    else:
        ok = t >= -o
        if t_lat < n:
            ok = ok & ((t < t_lat) | (t >= t_lat - o))
    return jnp.where(ok, y, 0.0)


def conv_fwd(name, xarr, col_off, width, w, b, left, n_rows, t_lat, out_dtype, cb=128):
    taps = w.shape[0]
    assert col_off % cb == 0 and width % cb == 0

    def body(x_ref, w_ref, b_ref, o_ref):
        x = x_ref[...].astype(F32)
        acc = jnp.broadcast_to(b_ref[...], x.shape)
        for k in range(taps):
            acc = acc + _shift_rows(x, k - left, t_lat, n_rows) * w_ref[k:k + 1, :]
        o_ref[...] = acc.astype(o_ref.dtype)

    return pl.pallas_call(
        body, name=name, grid=(width // cb,),
        out_shape=jax.ShapeDtypeStruct((n_rows, width), out_dtype),
        in_specs=[pl.BlockSpec((n_rows, cb), lambda j: (0, col_off // cb + j)),
                  pl.BlockSpec((taps, cb), lambda j: (0, j)),
                  pl.BlockSpec((1, cb), lambda j: (0, j))],
        out_specs=pl.BlockSpec((n_rows, cb), lambda j: (0, j)),
        compiler_params=_cparams(("parallel",)),
    )(xarr, w, b)


def conv_bwd(name, dout, xarr, col_off, width, w, left, n_rows, t_lat, cb=128):
    taps = w.shape[0]

    def body(d_ref, x_ref, w_ref, dx_ref, dw_ref, db_ref):
        d = d_ref[...].astype(F32)
        x = x_ref[...].astype(F32)
        dx = jnp.zeros_like(d)
        dws = []
        for k in range(taps):
            dx = dx + _shift_rows(d, left - k, t_lat, n_rows) * w_ref[k:k + 1, :]
            dws.append(_rsum(d * _shift_rows(x, k - left, t_lat, n_rows)))
        dx_ref[...] = dx.astype(dx_ref.dtype)
        dw_ref[...] = jnp.concatenate(dws, axis=0)
        db_ref[...] = _rsum(d)

    return pl.pallas_call(
        body, name=name, grid=(width // cb,),
        out_shape=[jax.ShapeDtypeStruct((n_rows, width), BF16), jax.ShapeDtypeStruct((taps, width), F32),
                   jax.ShapeDtypeStruct((1, width), F32)],
        in_specs=[pl.BlockSpec((n_rows, cb), lambda j: (0, j)),
                  pl.BlockSpec((n_rows, cb), lambda j: (0, col_off // cb + j)),
                  pl.BlockSpec((taps, cb), lambda j: (0, j))],
        out_specs=[pl.BlockSpec((n_rows, cb), lambda j: (0, j)), pl.BlockSpec((taps, cb), lambda j: (0, j)),
                   pl.BlockSpec((1, cb), lambda j: (0, j))],
        compiler_params=_cparams(("parallel",)),
    )(dout, xarr, w)


def _ffn_conv(a, w_ref, b_ref, t_lat):
    shifted = [_shift_rows(a, k - 1, t_lat, t_lat) for k in range(3)]
    ac = jnp.broadcast_to(b_ref[...], a.shape)
    for k in range(3):
        ac = ac + shifted[k] * w_ref[k:k + 1, :]
    return ac, shifted


def ffn_mix_fwd(u, w, b, t_lat, cb=128):
    nblk = FFN // cb

    def body(a_ref, g_ref, w_ref, b_ref, f_ref):
        ac, _ = _ffn_conv(a_ref[...].astype(F32), w_ref, b_ref, t_lat)
        f_ref[...] = (_silu(ac) * g_ref[...].astype(F32)).astype(f_ref.dtype)

    col = lambda shape, off=0: pl.BlockSpec(shape, lambda j: (0, off + j))
    return pl.pallas_call(
        body, name="ffn_mix", grid=(nblk,),
        out_shape=jax.ShapeDtypeStruct((t_lat, FFN), BF16),
        in_specs=[col((t_lat, cb)), col((t_lat, cb), nblk), col((3, cb)), col((1, cb))],
        out_specs=col((t_lat, cb)),
        compiler_params=_cparams(("parallel",)),
    )(u, u, w, b)


def ffn_mix_bwd(u, df, w, b, t_lat, cb=128):
    nblk = FFN // cb

    def body(a_ref, g_ref, df_ref, w_ref, b_ref, du_ref, dw_ref, db_ref):
        ac, shifted = _ffn_conv(a_ref[...].astype(F32), w_ref, b_ref, t_lat)
        d = df_ref[...].astype(F32)
        s = _sigmoid(ac)
        du_ref[1] = (d * (ac * s)).astype(du_ref.dtype)
        dac = d * g_ref[...].astype(F32) * (s * (1.0 + ac * (1.0 - s)))
        da = jnp.zeros_like(dac)
        for k in range(3):
            da = da + _shift_rows(dac, 1 - k, t_lat, t_lat) * w_ref[k:k + 1, :]
        du_ref[0] = da.astype(du_ref.dtype)
        dw_ref[...] = jnp.concatenate([_rsum(dac * shifted[k]) for k in range(3)], axis=0)
        db_ref[...] = _rsum(dac)

    col = lambda shape, off=0: pl.BlockSpec(shape, lambda j: (0, off + j))
    return pl.pallas_call(
        body, name="ffn_mix_bwd", grid=(nblk,),
        out_shape=[jax.ShapeDtypeStruct((2, t_lat, FFN), BF16),
                   jax.ShapeDtypeStruct((3, FFN), F32), jax.ShapeDtypeStruct((1, FFN), F32)],
        in_specs=[col((t_lat, cb)), col((t_lat, cb), nblk), col((t_lat, cb)), col((3, cb)), col((1, cb))],
        out_specs=[pl.BlockSpec((2, t_lat, cb), lambda j: (0, 0, j)), col((3, cb)), col((1, cb))],
        compiler_params=_cparams(("parallel",)),
    )(u, u, df, w, b)


def _chunk_order(direction, nb, nbl):
    if direction == 'f':
        return lambda s: ((s + nbl) % nb, 0)
    return lambda s: (nb - 1 - s, 0)


def _adjoint_order(direction, nb, nbl):
    if direction == 'f':
        return lambda s: ((nb - 1 - s + nbl) % nb, 0)
    return lambda s: (s, 0)


SUBLANES = 8


def _chunk_scan(a, b, carry, rev):
    tc, width = a.shape
    nt = tc // SUBLANES
    row = lax.broadcasted_iota(jnp.int32, a.shape, 0)
    a, b = a.reshape(nt, SUBLANES, width), b.reshape(nt, SUBLANES, width)
    in_tile = lax.broadcasted_iota(jnp.int32, a.shape, 1)
    for k in (1, 2, 4):
        shift = SUBLANES - k if rev else k
        edge = in_tile >= SUBLANES - k if rev else in_tile < k
        b = jnp.where(edge, b, a * pltpu.roll(b, shift, 1) + b)
        a = jnp.where(edge, a, a * pltpu.roll(a, shift, 1))
    a, b = a.reshape(tc, width), b.reshape(tc, width)
    hs = [None] * nt
    c = carry
    for kt in range(nt):
        k = nt - 1 - kt if rev else kt
        h = b[k * SUBLANES:(k + 1) * SUBLANES] + a[k * SUBLANES:(k + 1) * SUBLANES] * c
        hs[k] = h
        c = h[0:1] if rev else h[SUBLANES - 1:SUBLANES]
    h = jnp.concatenate(hs, axis=0)
    if rev:
        return h, jnp.where(row == tc - 1, carry, pltpu.roll(h, tc - 1, 0)), c
    return h, jnp.where(row == 0, carry, pltpu.roll(h, 1, 0)), c


def scan_fwd(name, a, u, direction, n_rows, t_lat):
    w = a.shape[1]
    tc = _pick(math.gcd(t_lat, n_rows), (256, 128))
    nb, nbl = n_rows // tc, t_lat // tc
    order = _chunk_order(direction, nb, nbl)
    rev = direction == 'b'

    def body(a_ref, u_ref, h_ref, hp_ref, carry):
        @pl.when(pl.program_id(0) == 0)
        def _():
            carry[...] = jnp.zeros_like(carry)

        h_ref[...], hp_ref[...], carry[...] = _chunk_scan(a_ref[...], u_ref[...], carry[...], rev)

    spec = pl.BlockSpec((tc, w), order)
    return pl.pallas_call(
        body, name=name, grid=(nb,),
        out_shape=[jax.ShapeDtypeStruct((n_rows, w), F32)] * 2,
        in_specs=[spec, spec], out_specs=[spec, spec],
        scratch_shapes=[pltpu.VMEM((1, w), F32)],
        compiler_params=_cparams(("arbitrary",)),
    )(a, u)


def scan_adj(name, a, dh, hprev, direction, n_rows, t_lat):
    w = a.shape[1]
    tc = _pick(math.gcd(t_lat, n_rows), (256, 128))
    nb, nbl = n_rows // tc, t_lat // tc
    order = _adjoint_order(direction, nb, nbl)
    rev = direction == 'f'

    def dh_order(s):
        c, _ = order(s)
        return (jnp.minimum(c, nbl - 1), 0)

    def body(a_ref, dh_ref, hp_ref, du_ref, da_ref, carry):
        s = pl.program_id(0)

        @pl.when(s == 0)
        def _():
            carry[...] = jnp.zeros_like(carry)

        chunk, _ = order(s)
        live = (chunk < nbl).astype(F32)

        av = a_ref[...]
        dv = dh_ref[...] * live
        _, c_next, carry[...] = _chunk_scan(av, av * dv, carry[...], rev)
        lam = dv + c_next
        du_ref[...] = lam
        da_ref[...] = lam * hp_ref[...]

    spec = pl.BlockSpec((tc, w), order)
    return pl.pallas_call(
        body, name=name, grid=(nb,),
        out_shape=[jax.ShapeDtypeStruct((n_rows, w), F32)] * 2,
        in_specs=[spec, pl.BlockSpec((tc, w), dh_order), spec], out_specs=[spec, spec],
        scratch_shapes=[pltpu.VMEM((1, w), F32)],
        compiler_params=_cparams(("arbitrary",)),
    )(a, dh, hprev)


def _one_minus_a_squared(log_a, a):
    return (1.0 + a * a) * jnp.tanh(-log_a)


def _gate_elem(pre_r, pre_i, xc, b_a, b_x, sp):
    r = _sigmoid(pre_r + b_a)
    i = _sigmoid(pre_i + b_x)
    log_a = (-LRU_C) * r * sp
    a = jnp.exp(log_a)
    m2 = _one_minus_a_squared(log_a, a)
    mult = jnp.where(m2 > 0.0, m2 * lax.rsqrt(m2), 0.0)
    return a, mult * (i * xc)


def _gate_elem_bwd(pre_r, pre_i, xc, b_a, b_x, sp, da, du):
    r = _sigmoid(pre_r + b_a)
    i = _sigmoid(pre_i + b_x)
    log_a = (-LRU_C) * r * sp
    a = jnp.exp(log_a)
    m2 = _one_minus_a_squared(log_a, a)
    inv_mult = lax.rsqrt(m2)
    g = du * (m2 * inv_mult)
    d_mult = du * (i * xc)
    d_log_a = (da - d_mult * a * inv_mult) * a
    d_pre_r = d_log_a * ((-LRU_C) * sp) * (r * (1.0 - r))
    d_pre_i = g * xc * (i * (1.0 - i))
    return d_pre_r, d_pre_i, g * i, _rsum(d_log_a * ((-LRU_C) * r))


def _blockdiag(xb16, w_ref_val, d):
    outs = []
    for n in range(LRU_BLOCKS):
        outs.append(jnp.dot(xb16[:, n * LRU_BW:(n + 1) * LRU_BW], w_ref_val[d * LRU_BLOCKS + n],
                            preferred_element_type=F32))
    return jnp.concatenate(outs, axis=1)


def gates_fwd(xc, w_a, w_x, b_a, b_x, sp, n_rows, t_lat, tm):
    def fn(is_ctx, rows, params):
        (x,), (wa, wx, ba, bx, spv) = rows, params
        xb16 = x.astype(BF16)
        outs = []
        for d in range(2):
            a, u = _gate_elem(_blockdiag(xb16, wa, d), _blockdiag(xb16, wx, d), x,
                              ba[d:d + 1], bx[d:d + 1], spv[d:d + 1])
            outs += [a, u]
        return outs, []

    (a_f, u_f, a_b, u_b), _ = rowwise("gates_fwd", fn, [(xc, 0, LRU_W)], [w_a, w_x, b_a, b_x, sp],
                                      [(LRU_W, F32)] * 4, [], n_rows, t_lat, tm)
    return a_f, u_f, a_b, u_b


def gates_bwd(xc, da_f, du_f, da_b, du_b, w_a, w_x, b_a, b_x, sp, n_rows, t_lat, tm):
    def fn(is_ctx, rows, params):
        (x, daf, duf, dab, dub), (wa, wx, ba, bx, spv) = rows, params
        xb16 = x.astype(BF16)
        dxc = jnp.zeros_like(x)
        dwa, dwx, dba, dbx, dsp = [], [], [], [], []
        for d, (da, du) in enumerate(((daf, duf), (dab, dub))):
            dpr, dpi, dx_e, dsp_d = _gate_elem_bwd(_blockdiag(xb16, wa, d), _blockdiag(xb16, wx, d), x,
                                                   ba[d:d + 1], bx[d:d + 1], spv[d:d + 1], da, du)
            dba_d, dbx_d = _rsum(dpr), _rsum(dpi)
            dxc = dxc + dx_e
            dpr16, dpi16 = dpr.astype(BF16), dpi.astype(BF16)
            back = []
            for n in range(LRU_BLOCKS):
                sl = slice(n * LRU_BW, (n + 1) * LRU_BW)
                nt_dims = (((1,), (1,)), ((), ()))
                back.append(lax.dot_general(dpr16[:, sl], wa[d * LRU_BLOCKS + n], nt_dims, preferred_element_type=F32)
                            + lax.dot_general(dpi16[:, sl], wx[d * LRU_BLOCKS + n], nt_dims,
                                              preferred_element_type=F32))
                tn_dims = (((0,), (0,)), ((), ()))
                dwa.append(lax.dot_general(xb16[:, sl], dpr16[:, sl], tn_dims, preferred_element_type=F32)[None])
                dwx.append(lax.dot_general(xb16[:, sl], dpi16[:, sl], tn_dims, preferred_element_type=F32)[None])
            dxc = dxc + jnp.concatenate(back, axis=1)
            dba.append(dba_d)
            dbx.append(dbx_d)
            dsp.append(dsp_d)
        cat0 = lambda xs: jnp.concatenate(xs, axis=0)
        return [dxc], [cat0(dwa), cat0(dwx), cat0(dba), cat0(dbx), cat0(dsp)]

    (dxc,), accs = rowwise("gates_bwd", fn,
                           [(xc, 0, LRU_W), (da_f, 0, LRU_W), (du_f, 0, LRU_W), (da_b, 0, LRU_W), (du_b, 0, LRU_W)],
                           [w_a, w_x, b_a, b_x, sp], [(LRU_W, F32)],
                           [(2 * LRU_BLOCKS, LRU_BW, LRU_BW)] * 2 + [(2, LRU_W)] * 3, n_rows, t_lat, tm)
    return dxc, accs


def _rope_tables(t_lat, n_rows):
    rows = t_lat // GRID_W
    row_ids = jnp.repeat(jnp.arange(rows), GRID_W).astype(F32)
    col_ids = jnp.tile(jnp.arange(GRID_W), rows).astype(F32)
    axis_dim = QK_ROPE // 2
    inv = 1.0 / (ROPE_BASE ** (jnp.arange(0, axis_dim, 2, dtype=F32) / axis_dim))
    ang = jnp.concatenate([row_ids[:, None] * inv, col_ids[:, None] * inv], axis=-1)
    cos, sin = jnp.cos(ang), jnp.sin(ang)
    half = QK_ROPE // 2
    ones, zeros = jnp.ones((t_lat, QK_NOPE), F32), jnp.zeros((t_lat, QK_NOPE), F32)
    pad1, pad0 = jnp.ones((t_lat, HEAD_PAD - QK_DIM), F32), jnp.zeros((t_lat, HEAD_PAD - QK_DIM), F32)
    zh = jnp.zeros((t_lat, half), F32)
    c_tab = jnp.concatenate([ones, cos, cos, pad1], axis=1)
    s1 = jnp.concatenate([zeros, -sin, zh, pad0], axis=1)
    s2 = jnp.concatenate([zeros, zh, sin, pad0], axis=1)
    n_ctx = n_rows - t_lat
    c_tab = jnp.concatenate([c_tab, jnp.ones((n_ctx, HEAD_PAD), F32)], axis=0)
    s1 = jnp.concatenate([s1, jnp.zeros((n_ctx, HEAD_PAD), F32)], axis=0)
    s2 = jnp.concatenate([s2, jnp.zeros((n_ctx, HEAD_PAD), F32)], axis=0)
    return c_tab, s1, s2


def _rope(x, c, s1, s2):
    half = QK_ROPE // 2
    return x * c + pltpu.roll(x, HEAD_PAD - half, 1) * s1 + pltpu.roll(x, half, 1) * s2


def _rope_t(dy, c, s1, s2):
    half = QK_ROPE // 2
    return dy * c + pltpu.roll(dy * s1, half, 1) + pltpu.roll(dy * s2, HEAD_PAD - half, 1)


def _heads(x):
    return [x[:, h * HEAD_PAD:(h + 1) * HEAD_PAD] for h in range(N_HEADS)]


Q_SCALE = QK_DIM ** -0.5 * math.log2(math.e)

def attn_fwd(q, k, v, t_lat, n_rows, tq):
    def body(q_ref, k_ref, v_ref, o_ref, lse_ref):
        s = lax.dot_general(q_ref[...], k_ref[...], (((1,), (1,)), ((), ())), preferred_element_type=F32)
        m = jnp.max(s, axis=-1, keepdims=True)
        p = jnp.exp2(s - m)
        l = jnp.sum(p, axis=-1, keepdims=True)
        o = jnp.dot(p.astype(BF16), v_ref[...], preferred_element_type=F32) / l
        o_ref[...] = o.astype(o_ref.dtype)
        lse_ref[...] = jnp.broadcast_to(m + jnp.log2(l), lse_ref.shape)

    qspec = pl.BlockSpec((tq, HEAD_PAD), lambda h, i: (i, h))
    kspec = pl.BlockSpec((n_rows, HEAD_PAD), lambda h, i: (0, h))
    return pl.pallas_call(
        body, name="attn_fwd", grid=(N_HEADS, t_lat // tq),
        out_shape=[jax.ShapeDtypeStruct((t_lat, N_HEADS * HEAD_PAD), BF16),
                   jax.ShapeDtypeStruct((t_lat, N_HEADS * HEAD_PAD), F32)],
        in_specs=[qspec, kspec, kspec], out_specs=[qspec, qspec],
        compiler_params=_cparams(("parallel", "arbitrary")),
    )(q, k, v)


def attn_bwd(q, k, v, o, do, lse, t_lat, n_rows, tq):
    scale = QK_DIM ** -0.5
    nq = t_lat // tq
    nt = (((1,), (1,)), ((), ()))
    tn = (((0,), (0,)), ((), ()))

    def body(q_ref, k_ref, v_ref, o_ref, do_ref, lse_ref, dq_ref, dk_ref, dv_ref):
        @pl.when(pl.program_id(1) == 0)
        def _():
            dk_ref[...] = jnp.zeros_like(dk_ref)
            dv_ref[...] = jnp.zeros_like(dv_ref)

        qv, kv, vv, dov = q_ref[...], k_ref[...], v_ref[...], do_ref[...]
        s = lax.dot_general(qv, kv, nt, preferred_element_type=F32)
        p = jnp.exp2(s - lse_ref[:, 0:1])
        dv_ref[...] += lax.dot_general(p.astype(BF16), dov, tn, preferred_element_type=F32)
        dp = lax.dot_general(dov, vv, nt, preferred_element_type=F32)
        delta = jnp.sum(dov.astype(F32) * o_ref[...].astype(F32), axis=-1, keepdims=True)
        ds = (p * (dp - delta)).astype(BF16)
        dq_ref[...] = (jnp.dot(ds, kv, preferred_element_type=F32) * scale).astype(dq_ref.dtype)
        dk_ref[...] += lax.dot_general(ds, qv, tn, preferred_element_type=F32)

        @pl.when(pl.program_id(1) == nq - 1)
        def _():
            dk_ref[...] = dk_ref[...] * (scale / Q_SCALE)

    qspec = pl.BlockSpec((tq, HEAD_PAD), lambda h, i: (i, h))
    kspec = pl.BlockSpec((n_rows, HEAD_PAD), lambda h, i: (0, h))
    return pl.pallas_call(
        body, name="attn_bwd", grid=(N_HEADS, t_lat // tq),
        out_shape=[jax.ShapeDtypeStruct((t_lat, N_HEADS * HEAD_PAD), BF16),
                   jax.ShapeDtypeStruct((n_rows, N_HEADS * HEAD_PAD), F32),
                   jax.ShapeDtypeStruct((n_rows, N_HEADS * HEAD_PAD), F32)],
        in_specs=[qspec, kspec, kspec, qspec, qspec, qspec], out_specs=[qspec, kspec, kspec],
        compiler_params=_cparams(("parallel", "arbitrary")),
    )(q, k, v, o, do, lse)


def adamw(name, w, g, m, v):
    r, ccols = w.shape
    if r % 8 == 0:
        tr, tcol = _best_div(r, 8, max(8, 262144 // ccols)), ccols
    else:
        tr, tcol = r, _pick(ccols, (256, 128))
    c1 = 1.0 - ADAM_B1 ** ADAM_STEP
    c2 = 1.0 - ADAM_B2 ** ADAM_STEP

    def body(w_ref, g_ref, m_ref, v_ref, d_ref, nm_ref, nv_ref):
        gv = g_ref[...]
        nm = ADAM_B1 * m_ref[...] + (1.0 - ADAM_B1) * gv
        nv = ADAM_B2 * v_ref[...] + (1.0 - ADAM_B2) * (gv * gv)
        d_ref[...] = -ADAM_LR * ((nm / c1) / (jnp.sqrt(nv / c2) + ADAM_EPS) + ADAM_WD * w_ref[...])
        nm_ref[...] = nm
        nv_ref[...] = nv

    spec = pl.BlockSpec((tr, tcol), lambda i, j: (i, j))
    return pl.pallas_call(
        body, name=name, grid=(r // tr, ccols // tcol),
        out_shape=[jax.ShapeDtypeStruct((r, ccols), F32)] * 3,
        in_specs=[spec] * 4, out_specs=[spec] * 3,
        compiler_params=_cparams(("parallel", "parallel")),
    )(w, g, m, v)


def adamw_many(name, ws, gs, ms, vs):
    n = len(ws)
    c1 = 1.0 - ADAM_B1 ** ADAM_STEP
    c2 = 1.0 - ADAM_B2 ** ADAM_STEP

    def body(*refs):
        for i in range(n):
            w_ref, g_ref, m_ref, v_ref = (refs[k * n + i] for k in range(4))
            d_ref, nm_ref, nv_ref = (refs[(4 + k) * n + i] for k in range(3))
            gv = g_ref[...]
            nm = ADAM_B1 * m_ref[...] + (1.0 - ADAM_B1) * gv
            nv = ADAM_B2 * v_ref[...] + (1.0 - ADAM_B2) * (gv * gv)
            d_ref[...] = -ADAM_LR * ((nm / c1) / (jnp.sqrt(nv / c2) + ADAM_EPS) + ADAM_WD * w_ref[...])
            nm_ref[...] = nm
            nv_ref[...] = nv

    vmem = pl.BlockSpec(memory_space=pltpu.VMEM)
    res = pl.pallas_call(
        body, name=name,
        out_shape=[jax.ShapeDtypeStruct(w.shape, F32) for w in ws] * 3,
        in_specs=[vmem] * (4 * n), out_specs=[vmem] * (3 * n),
        compiler_params=_cparams(),
    )(*ws, *gs, *ms, *vs)
    return [tuple(res[k * n + i] for k in range(3)) for i in range(n)]


def _flat(parts, dtype, row_mult):
    v = jnp.concatenate([p.reshape(-1).astype(dtype) for p in parts])
    quantum = row_mult * FLAT_C
    total = -(-v.shape[0] // quantum) * quantum
    return jnp.pad(v, (0, total - v.shape[0])).reshape(total // FLAT_C, FLAT_C)


def _gathered_to_full(name, g):
    k = g.shape[1]
    return jnp.transpose(g, (1, 0, 2)).reshape(k, N_DEV * g.shape[2])


def _shard_to_rb(name, w):
    return w if name in ROW_SHARDED else w.T


def _rb_to_shard(name, g):
    return g if name in ROW_SHARDED else g.T


def _rb_from_gathered(name, g):
    cols = g.shape[2]
    if name == 'w_in':
        z = lambda k: jnp.zeros((k, cols), g.dtype)
        full = g.reshape(N_DEV * g.shape[1], cols)
        return jnp.concatenate([full[:Z_KR], z(QK_NOPE), full[Z_KR:Z_KR + QK_ROPE], z(HEAD_PAD - QK_DIM),
                                full[Z_KR + QK_ROPE:]], axis=0)
    if name == 'w_uq':
        return jnp.pad(g, ((0, 0), (0, HEAD_PAD - QK_DIM), (0, 0))).reshape(N_HEADS * HEAD_PAD, cols)
    if name == 'w_ukv':
        pad = lambda t: jnp.pad(t, ((0, 0), (0, HEAD_PAD - t.shape[1]), (0, 0))).reshape(N_HEADS * HEAD_PAD, cols)
        return jnp.concatenate([pad(g[:, :QK_NOPE]), pad(g[:, QK_NOPE:])], axis=0)
    if name == 'w_o_attn':
        full = g.reshape(D, N_HEADS, V_HEAD)
        return jnp.pad(full, ((0, 0), (0, 0), (0, HEAD_PAD - V_HEAD))).reshape(D, N_HEADS * HEAD_PAD)
    return g.reshape(N_DEV * g.shape[1], cols)


def _chunks_from_rb_grad(name, g):
    cols = g.shape[1]
    if name == 'w_in':
        full = jnp.concatenate([g[:Z_KR], g[Z_KR + QK_NOPE:Z_KR + QK_DIM], g[Z_XB:]], axis=0)
        return full.reshape(N_DEV, -1, cols)
    if name == 'w_uq':
        return g.reshape(N_HEADS, HEAD_PAD, cols)[:, :QK_DIM]
    if name == 'w_ukv':
        half = N_HEADS * HEAD_PAD
        gk = g[:half].reshape(N_HEADS, HEAD_PAD, cols)[:, :QK_NOPE]
        gv = g[half:].reshape(N_HEADS, HEAD_PAD, cols)[:, :V_HEAD]
        return jnp.concatenate([gk, gv], axis=1)
    if name == 'w_o_attn':
        full = g.reshape(D, N_HEADS, HEAD_PAD)[:, :, :V_HEAD].reshape(D, N_HEADS * V_HEAD)
        return full.reshape(N_DEV, D // N_DEV, N_HEADS * V_HEAD)
    return g.reshape(N_DEV, -1, cols)


def local_step(x, ctx, target, mod_l, mod_c, wt, on_grad=None, arrive=None):
    t_lat, n_ctx = x.shape[0], ctx.shape[0]
    n = t_lat + n_ctx
    tm = _pick(math.gcd(t_lat, n), (256, 128))
    tq_fwd = _pick(t_lat, (256, 128))
    tq_bwd = _pick(t_lat, (512, 256, 128))
    row = lambda v: v.reshape(1, -1).astype(F32)
    two = lambda a, b: jnp.stack([a, b]).astype(F32)
    sh1_l, sc1_l, g1_l, sh2_l, sc2_l, g2_l = jnp.split(mod_l, 6)
    sh1_c, sc1_c = jnp.split(mod_c, 6)[:2]
    sc1, sh1 = two(sc1_l, sc1_c), two(sh1_l, sh1_c)
    g1, g2, sc2, sh2 = row(g1_l), row(g2_l), row(sc2_l), row(sh2_l)
    norm1_g, norm2_g, final_g = row(wt['norm1_g']), row(wt['norm2_g']), row(wt['final_g'])
    q_g, kv_g, b_gate = row(wt['q_norm_g']), row(wt['kv_norm_g']), row(wt['b_gate'])
    wt = dict(wt)
    pending = []

    def sent():
        tokens = list(pending)
        pending.clear()
        return tokens

    def need(names, after):
        if arrive is not None:
            got = arrive(names, after)
            if '_token' in got:
                pending.append(got.pop('_token'))
            wt.update(got)
        return [wt[n] for n in names]
    lru_w_a = wt['lru_w_a'].reshape(2 * LRU_BLOCKS, LRU_BW, LRU_BW).astype(BF16)
    lru_w_x = wt['lru_w_x'].reshape(2 * LRU_BLOCKS, LRU_BW, LRU_BW).astype(BF16)
    b_a, b_x, lam = wt['lru_b_a'], wt['lru_b_x'], wt['lru_lambda']
    sp = jnp.logaddexp(-lam, 0.0)
    c_tab, s1_tab, s2_tab = _rope_tables(t_lat, n)
    rw = functools.partial(rowwise, n_rows=n, t_lat=t_lat, tm=tm)
    rw_lat = functools.partial(rowwise, n_rows=t_lat, t_lat=t_lat, tm=tm)

    stream = [(x, 0, D), (ctx, 0, D, 'ctx')]

    def f_norm1(is_ctx, rows, params):
        (xl, xc_), (g, sc, sh) = rows, params
        return [_norm_mod(jnp.where(is_ctx, xc_, xl), g, _sel(is_ctx, sc), _sel(is_ctx, sh))], []

    (h,), _ = rw("norm1", f_norm1, stream, [norm1_g, sc1, sh1], [(D, BF16)], [])
    (w_in_t,) = need(('w_in',), h)
    z = matmul("w_in", h, w_in_t, 'nt', BF16, after=sent())
    w_uq_t, w_ukv_t, w_o_lru = need(('w_uq', 'w_ukv', 'w_o_lru'), z)

    def f_qkv_norm(is_ctx, rows, params):
        (ql, kvl), (gq, gkv) = rows, params
        return [_rms(ql, gq), _rms(kvl, gkv)], []

    (qn, kvn), _ = rw("qkv_norm", f_qkv_norm, [(z, Z_Q, Q_RANK), (z, Z_KV, KV_RANK)], [q_g, kv_g],
                      [(Q_RANK, BF16), (KV_RANK, BF16)], [])
    qp = matmul("w_uq", qn, w_uq_t, 'nt', BF16)
    kvp = matmul("w_ukv", kvn, w_ukv_t, 'nt', BF16)

    def f_rope(is_ctx, rows, params):
        qv, kk, vv, kr, c, s1, s2 = rows
        krr = _rope(kr, c, s1, s2)
        qo = jnp.concatenate([_rope(qh, c, s1, s2) for qh in _heads(qv)], axis=1) * Q_SCALE
        ko = jnp.concatenate([kh + krr for kh in _heads(kk)], axis=1)
        return [qo, ko, vv], []

    hp = N_HEADS * HEAD_PAD
    (qr, kr_, vr), _ = rw("rope", f_rope,
                          [(qp, 0, hp), (kvp, 0, hp), (kvp, hp, hp), (z, Z_KR, HEAD_PAD), (c_tab, 0, HEAD_PAD),
                           (s1_tab, 0, HEAD_PAD), (s2_tab, 0, HEAD_PAD)], [], [(hp, BF16)] * 3, [])
    attn, lse = attn_fwd(qr, kr_, vr, t_lat, n, tq_fwd)

    xc = conv_fwd("lru_conv", z, Z_XB, LRU_W, wt['lru_conv_w'], row(wt['lru_conv_b']), 2, n, t_lat, F32)
    a_f, u_f, a_b, u_b = gates_fwd(xc, lru_w_a, lru_w_x, b_a, b_x, sp, n, t_lat, tm)
    h_f, hp_f = scan_fwd("scan_f", a_f, u_f, 'f', n, t_lat)
    h_b, hp_b = scan_fwd("scan_b", a_b, u_b, 'b', n, t_lat)

    def f_lru_out(is_ctx, rows, params):
        hf, hb, yb = rows
        return [(hf + hb) * _gelu(yb)], []

    (ybin,), _ = rw_lat("lru_out", f_lru_out, [(h_f, 0, LRU_W), (h_b, 0, LRU_W), (z, Z_YB, LRU_W)], [],
                        [(LRU_W, BF16)], [])
    w_o_attn_t, w_out, w_up_t, w_down = need(('w_o_attn', 'w_out', 'w_up', 'w_down'), attn)
    y_a = matmul("w_o_attn", attn, w_o_attn_t, 'nt', BF16)
    y_b = matmul("w_o_lru", ybin, w_o_lru, 'nn', BF16)

    def _merge(ya, yb, gl, bg):
        gates = _sigmoid(gl + bg)
        return gates[:, :D] * ya + gates[:, D:] * yb

    def f_merge(is_ctx, rows, params):
        (ya, yb, gl), (bg,) = rows, params
        return [_merge(ya, yb, gl, bg)], []

    (mrg,), _ = rw_lat("merge", f_merge, [(y_a, 0, D), (y_b, 0, D), (z, Z_GL, 2 * D)], [b_gate], [(D, BF16)], [])
    o = matmul("w_out", mrg, w_out, 'nn', BF16)

    def _res_norm2(xv, ov, g1v, g, sc, sh):
        x1 = xv + g1v * ov
        return x1, _norm_mod(x1, g, sc, sh)

    def f_norm2(is_ctx, rows, params):
        (xv, ov), (g1v, g, sc, sh) = rows, params
        x1, h2v = _res_norm2(xv, ov, g1v, g, sc, sh)
        return [x1, h2v], []

    (x1, h2), _ = rw_lat("norm2", f_norm2, [(x, 0, D), (o, 0, D)], [g1, norm2_g, sc2, sh2], [(D, F32), (D, BF16)], [])
    u = matmul("w_up", h2, w_up_t, 'nt', BF16)
    f = ffn_mix_fwd(u, wt['ffn_conv_w'], row(wt['ffn_conv_b']), t_lat)
    dn = matmul("w_down", f, w_down, 'nn', BF16)

    def _tile_loss(x1v, dv, g2v, fg, tgt):
        y = _rms(x1v + g2v * dv, fg)
        e = y - tgt
        return 0.5 * jnp.sum(jnp.mean(e * e, axis=-1, keepdims=True), axis=0, keepdims=True)

    def f_final(is_ctx, rows, params):
        (x1v, dv, tgt), (g2v, fg) = rows, params
        lv, vjp = jax.vjp(lambda a, b, c, d: _tile_loss(a, b, c, d, tgt), x1v, dv, g2v, fg)
        dx2, dd, dg2, dfg = vjp(jnp.ones((1, 1), F32))
        return [dx2, dd], [dg2, dfg, jnp.broadcast_to(lv, (1, 128))]

    (dx2, dd), (dg2, dfinal_g, loss_v) = rw_lat("final", f_final, [(x1, 0, D), (dn, 0, D), (target, 0, D)],
                                                [g2, final_g], [(D, F32), (D, BF16)], [(1, D), (1, D), (1, 128)])
    loss = loss_v[0, 0]

    grads = {'final_g': dfinal_g}

    def put(name, g):
        grads[name] = g
        if on_grad is not None:
            pending.append(on_grad(name, g))
    df = matmul("d_f", dd, w_down, 'nt', BF16)
    put('w_down', matmul("g_w_down", f, dd, 'tn', BF16))

    du, grads['ffn_conv_w'], grads['ffn_conv_b'] = ffn_mix_bwd(u, df, wt['ffn_conv_w'], row(wt['ffn_conv_b']),
                                                               t_lat)
    dh2 = matmul("d_h2", du, w_up_t, 'nn', BF16, after=sent())
    put('w_up', matmul("g_w_up", du, h2, 'tn', BF16))

    def b_norm2(is_ctx, rows, params):
        (xv, ov, dh2v, dx2v), (g1v, g, sc, sh) = rows, params
        _, vjp = jax.vjp(_res_norm2, xv, ov, g1v, g, sc, sh)
        dx, do, dg1v, dg, dsc, dsh = vjp((dx2v, dh2v))
        return [dx, do], [dg1v, dg, dsc, dsh]

    (dx_res, do), (dg1, dnorm2_g, dsc2, dsh2) = rw_lat(
        "norm2_bwd", b_norm2, [(x, 0, D), (o, 0, D), (dh2, 0, D), (dx2, 0, D)], [g1, norm2_g, sc2, sh2],
        [(D, F32), (D, BF16)], [(1, D)] * 4)
    grads['norm2_g'] = dnorm2_g
    dmrg = matmul("d_merge", do, w_out, 'nt', BF16, after=sent())
    put('w_out', matmul("g_w_out", mrg, do, 'tn', BF16))

    def b_merge(is_ctx, rows, params):
        (ya, yb, gl, dm), (bg,) = rows, params
        _, vjp = jax.vjp(_merge, ya, yb, gl, bg)
        dya, dyb, dgl, dbg = vjp(dm)
        return [dya, dyb, dgl], [dbg]

    (dy_a, dy_b, dgl), (grads['b_gate'],) = rw_lat(
        "merge_bwd", b_merge, [(y_a, 0, D), (y_b, 0, D), (z, Z_GL, 2 * D), (dmrg, 0, D)], [b_gate],
        [(D, BF16), (D, BF16), (2 * D, BF16)], [(1, 2 * D)])
    dattn = matmul("d_attn", dy_a, w_o_attn_t, 'nn', BF16, after=sent())
    put('w_o_attn', matmul("g_w_o_attn", dy_a, attn, 'tn', BF16))
    dybin = matmul("d_lru_out", dy_b, w_o_lru, 'nt', BF16, after=sent())
    put('w_o_lru', matmul("g_w_o_lru", ybin, dy_b, 'tn', BF16))

    def b_lru_out(is_ctx, rows, params):
        hf, hb, yb, dyv = rows
        _, vjp = jax.vjp(lambda s, y: s * _gelu(y), hf + hb, yb)
        dh, dyb = vjp(dyv)
        return [dh, dyb], []

    (dh_lru, dyb), _ = rw_lat("lru_out_bwd", b_lru_out,
                              [(h_f, 0, LRU_W), (h_b, 0, LRU_W), (z, Z_YB, LRU_W), (dybin, 0, LRU_W)], [],
                              [(LRU_W, F32), (LRU_W, BF16)], [])
    du_f, da_f = scan_adj("scan_f_adj", a_f, dh_lru, hp_f, 'f', n, t_lat)
    du_b, da_b = scan_adj("scan_b_adj", a_b, dh_lru, hp_b, 'b', n, t_lat)
    dxc, (dw_a, dw_x, db_a, db_x, dsp) = gates_bwd(xc, da_f, du_f, da_b, du_b, lru_w_a, lru_w_x, b_a, b_x, sp,
                                                   n, t_lat, tm)
    put('lru_w_a', dw_a.reshape(2 * LRU_BLOCKS * LRU_BW, LRU_BW).astype(BF16))
    put('lru_w_x', dw_x.reshape(2 * LRU_BLOCKS * LRU_BW, LRU_BW).astype(BF16))
    grads['lru_b_a'], grads['lru_b_x'] = db_a, db_x
    grads['lru_lambda'] = -dsp * _sigmoid(-lam)
    dxb, grads['lru_conv_w'], grads['lru_conv_b'] = conv_bwd("lru_conv_bwd", dxc, z, Z_XB, LRU_W, wt['lru_conv_w'],
                                                             2, n, t_lat)

    dq, dk, dv = attn_bwd(qr, kr_, vr, attn, dattn, lse, t_lat, n, tq_bwd)

    def b_rope(is_ctx, rows, params):
        dqv, dkv, dvv, c, s1, s2 = rows
        live = jnp.where(is_ctx, 0.0, 1.0)
        dqo = jnp.concatenate([_rope_t(dqh, c, s1, s2) for dqh in _heads(dqv)], axis=1) * live
        dkh = _heads(dkv)
        dkr = dkh[0]
        for t in dkh[1:]:
            dkr = dkr + t
        lanes = lax.broadcasted_iota(jnp.int32, dkr.shape, 1)
        dkr = jnp.where((lanes >= QK_NOPE) & (lanes < QK_DIM), _rope_t(dkr, c, s1, s2), 0.0)
        return [dqo, jnp.concatenate([dkv, dvv], axis=1), dkr], []

    (dqp, dkvp, dkr), _ = rw("rope_bwd", b_rope,
                             [(dq, 0, hp), (dk, 0, hp), (dv, 0, hp), (c_tab, 0, HEAD_PAD), (s1_tab, 0, HEAD_PAD),
                              (s2_tab, 0, HEAD_PAD)], [], [(hp, BF16), (2 * hp, BF16), (HEAD_PAD, BF16)], [])
    dqn = matmul("d_qn", dqp, w_uq_t, 'nn', BF16, after=sent())
    put('w_uq', matmul("g_w_uq", dqp, qn, 'tn', BF16))
    dkvn = matmul("d_kvn", dkvp, w_ukv_t, 'nn', BF16, after=sent())
    put('w_ukv', matmul("g_w_ukv", dkvp, kvn, 'tn', BF16))

    def b_qkv_norm(is_ctx, rows, params):
        (ql, kvl, dqv, dkvv), (gq, gkv) = rows, params
        _, vjp_q = jax.vjp(_rms, ql, gq)
        _, vjp_kv = jax.vjp(_rms, kvl, gkv)
        dql, dgq = vjp_q(dqv)
        dkvl, dgkv = vjp_kv(dkvv)
        return [dql, dkvl], [dgq, dgkv]

    (dq_lat, dkv_lat), (grads['q_norm_g'], grads['kv_norm_g']) = rw(
        "qkv_norm_bwd", b_qkv_norm, [(z, Z_Q, Q_RANK), (z, Z_KV, KV_RANK), (dqn, 0, Q_RANK), (dkvn, 0, KV_RANK)],
        [q_g, kv_g], [(Q_RANK, BF16), (KV_RANK, BF16)], [(1, Q_RANK), (1, KV_RANK)])
    pad_ctx = lambda t: jnp.pad(t, ((0, n_ctx), (0, 0)))
    dz = jnp.concatenate([dq_lat, dkv_lat, dkr, dxb, pad_ctx(dyb), pad_ctx(dgl)], axis=1)
    put('w_in', matmul("g_w_in", dz, h, 'tn', BF16))
    dh = matmul("d_h", dz, w_in_t, 'nn', BF16, after=sent())

    def b_norm1(is_ctx, rows, params):
        (xl, xc_, dhv, dxr), (g, sc, sh) = rows, params
        scv, shv = _sel(is_ctx, sc), _sel(is_ctx, sh)
        _, vjp = jax.vjp(_norm_mod, jnp.where(is_ctx, xc_, xl), g, scv, shv)
        dx, dg, dsc, dsh = vjp(dhv)
        return [dx + dxr], [dg, _seg_acc(is_ctx, dsc), _seg_acc(is_ctx, dsh)]

    (grad_x,), (grads['norm1_g'], dsc1, dsh1) = rw("norm1_bwd", b_norm1, stream + [(dh, 0, D), (dx_res, 0, D)],
                                                   [norm1_g, sc1, sh1], [(D, F32, 'lat')],
                                                   [(1, D), (2, D), (2, D)])
    zero = jnp.zeros((D,), F32)
    dmod_l = jnp.concatenate([dsh1[0], dsc1[0], dg1[0], dsh2[0], dsc2[0], dg2[0]])
    dmod_c = jnp.concatenate([dsh1[1], dsc1[1], zero, zero, zero, zero])
    return loss, grad_x, grads, dmod_l, dmod_c


def kernel(x, c, ctx, c_ctx, w_mod, b_mod, norm1_g, w_in, b_gate, q_norm_g, kv_norm_g, w_uq, w_ukv, w_o_attn, lru_conv_w, lru_conv_b, lru_w_a, lru_b_a, lru_w_x, lru_b_x, lru_lambda, w_o_lru, w_out, norm2_g, w_up, ffn_conv_w, ffn_conv_b, w_down, final_g, loss_target, m_c_ctx, m_w_mod, m_b_mod, m_norm1_g, m_w_in, m_b_gate, m_q_norm_g, m_kv_norm_g, m_w_uq, m_w_ukv, m_w_o_attn, m_lru_conv_w, m_lru_conv_b, m_lru_w_a, m_lru_b_a, m_lru_w_x, m_lru_b_x, m_lru_lambda, m_w_o_lru, m_w_out, m_norm2_g, m_w_up, m_ffn_conv_w, m_ffn_conv_b, m_w_down, m_final_g, v_c_ctx, v_w_mod, v_b_mod, v_norm1_g, v_w_in, v_b_gate, v_q_norm_g, v_kv_norm_g, v_w_uq, v_w_ukv, v_w_o_attn, v_lru_conv_w, v_lru_conv_b, v_lru_w_a, v_lru_b_a, v_lru_w_x, v_lru_b_x, v_lru_lambda, v_w_o_lru, v_w_out, v_norm2_g, v_w_up, v_ffn_conv_w, v_ffn_conv_b, v_w_down, v_final_g):
    given = dict(locals())
    strip = lambda name, a: a if name in ('c_ctx', 'final_g') else a[0]
    wsh = {n: strip(n, given[n]) for n in WEIGHTS}
    msh = {n: strip(n, given['m_' + n]) for n in WEIGHTS}
    vsh = {n: strip(n, given['v_' + n]) for n in WEIGHTS}
    me = _my_index()

    small = _flat([c[0]] + [wsh[n] for n in SMALL_F32], F32, 8)
    small_all = all_gather("gather_small", small).reshape(N_DEV, -1)
    c_all = small_all[:, :D]
    full, at = {}, D
    for n in SMALL_F32:
        cnt = math.prod(wsh[n].shape)
        full[n] = _gathered_to_full(n, small_all[:, at:at + cnt].reshape((N_DEV,) + wsh[n].shape))
        at += cnt

    cond = jnp.concatenate([c_all, c_ctx[None], jnp.zeros((7, D), F32)], axis=0)
    sil = cond * jax.nn.sigmoid(cond)
    mod_cols = matmul("mod_proj", sil, wsh['w_mod'], 'nn', F32)
    mod_all = all_gather("gather_mod", mod_cols)
    mod_all = jnp.transpose(mod_all, (1, 0, 2)).reshape(16, 6 * D) + b_mod[0][None]
    mod_l = lax.dynamic_index_in_dim(mod_all, me, axis=0, keepdims=False)
    mod_c = mod_all[N_DEV]

    rb_shards = {n: _shard_to_rb(n, wsh[n]).astype(BF16) for n in BIG_BF16}
    (w_in_blocks,) = all_gather_multi("gather_w_in", [rb_shards['w_in']])
    later = [n for n in BIG_BF16 if n != 'w_in']
    weights_started, weights_sent = exchange_start("weights_send", 'gather', [rb_shards[n] for n in later],
                                                   after=[w_in_blocks, mod_all])
    for n in REPLICATED:
        if n not in ('c_ctx', 'b_mod'):
            full[n] = wsh[n]

    def arrive(names, after):
        if names == ('w_in',):
            return {'w_in': _rb_from_gathered('w_in', w_in_blocks), '_token': weights_sent}
        picked = [later.index(n) for n in names]
        lands = exchange_wait("weights_wait_" + names[0], 'gather',
                              tuple([part[i] for i in picked] for part in weights_started), after)
        return {n: _rb_from_gathered(n, lax.dynamic_update_slice_in_dim(land, rb_shards[n][None], me, axis=0))
                for n, land in zip(names, lands)}

    in_flight = {}

    def on_grad(n, g):
        chunks = _chunks_from_rb_grad(n, g)
        own = lax.dynamic_index_in_dim(chunks, me, axis=0, keepdims=True)
        started, token = exchange_start("grad_send_" + n, 'scatter', [chunks])
        in_flight[n] = (own, started)
        return token

    loss, grad_x, grads, dmod_l, dmod_c = local_step(x[0], ctx[0], loss_target[0], mod_l, mod_c, full, on_grad,
                                                     arrive)
    dmod = jnp.stack([dmod_l, dmod_c]).reshape(2 * 6 * D // FLAT_C, FLAT_C)
    dm = all_gather("gather_dmod", dmod).reshape(N_DEV, 2, 6 * D)
    dmod_c_tot = dm[0, 1]
    for p in range(1, N_DEV):
        dmod_c_tot = dmod_c_tot + dm[p, 1]
    dm16 = jnp.concatenate([dm[:, 0], dmod_c_tot[None], jnp.zeros((7, 6 * D), F32)], axis=0)
    ncol = 6 * D // N_DEV
    dm16_cols = lax.dynamic_slice_in_dim(dm16.reshape(16, N_DEV, ncol), me, 1, axis=1)[:, 0]
    grad_w_mod = matmul("g_w_mod", sil, dm16_cols, 'tn', F32)
    dsil = matmul("d_cond", dm16_cols, wsh['w_mod'], 'nt', F32)
    sg = jax.nn.sigmoid(c_ctx)
    grads['c_ctx'] = dsil[N_DEV] * (sg * (1.0 + c_ctx * (1.0 - sg)))
    grads['b_mod'] = dmod_l + dmod_c

    g_final = {'w_mod': grad_w_mod}
    reduced, stepped = {}, {}
    for n in BIG_BF16 + ['lru_w_a', 'lru_w_x']:
        own, started = in_flight[n]
        (land,) = exchange_wait("grad_wait_" + n, 'scatter', started, dm)
        if n in ROW_SHARDED:
            g_final[n], *stepped[n] = reduce_slots("step_" + n, land, own, (wsh[n], msh[n], vsh[n]))
        elif n in COL_SHARDED and wsh[n].shape[1] % 128:
            g_t, *outs = reduce_slots("step_" + n, land, own, (wsh[n].T, msh[n].T, vsh[n].T))
            g_final[n], stepped[n] = g_t.T, [o.T for o in outs]
        else:
            reduced[n] = reduce_slots("sum_" + n, land, own)
            if n in BIG_BF16:
                g_final[n] = _rb_to_shard(n, reduced[n])

    small_names = SMALL_F32 + [n for n in REPLICATED if n not in ('lru_w_a', 'lru_w_x')]
    partials = _flat([grads[n] for n in small_names] + [loss], F32, 8)
    parts_all, a_all, x_all = all_gather_multi("gather_small_grads", [partials, reduced['lru_w_a'], reduced['lru_w_x']])
    small_sum = sum_slots("sum_small", parts_all).reshape(-1)
    g_final['lru_w_a'], g_final['lru_w_x'] = a_all.reshape(wsh['lru_w_a'].shape), x_all.reshape(wsh['lru_w_x'].shape)
    at = 0
    for n in small_names:
        cnt = math.prod(full[n].shape) if n in SMALL_F32 else math.prod(wsh[n].shape)
        g = small_sum[at:at + cnt]
        if n in SMALL_F32:
            k = full[n].shape[0]
            g = lax.dynamic_index_in_dim(g.reshape(k, N_DEV, -1), me, axis=1, keepdims=False)
        g_final[n] = g.reshape(wsh[n].shape)
        at += cnt
    loss = small_sum[at]

    for n in ['w_mod'] + BIG_BF16:
        if n not in stepped:
            stepped[n] = adamw("adamw_" + n, wsh[n], g_final[n], msh[n], vsh[n])
    rest = [n for n in WEIGHTS if n not in stepped]
    as2d = lambda a: a.reshape(-1, a.shape[-1])
    rest_out = adamw_many("adamw_small", *[[as2d(d[n]) for n in rest] for d in (wsh, g_final, msh, vsh)])
    stepped.update(zip(rest, rest_out))
    shaped = lambda n, a: a.reshape(given[n].shape)
    return (loss, grad_x[None],
            *[shaped(n, g_final[n]) for n in WEIGHTS],
            *[shaped(n, stepped[n][k]) for k in range(3) for n in WEIGHTS])
```

```python
import functools
import math

import jax
import jax.numpy as jnp
from jax import lax
from jax.experimental import pallas as pl
from jax.experimental.pallas import tpu as pltpu

F32 = jnp.float32
BF16 = jnp.bfloat16
MESH = pl.DeviceIdType.MESH

N_DEV = 8
D = 1024
N_HEADS = 8
HEAD_PAD = 128
QK_NOPE, QK_ROPE, V_HEAD = 64, 32, 64
QK_DIM = QK_NOPE + QK_ROPE
Q_RANK, KV_RANK = 384, 256
LRU_W, LRU_BLOCKS, LRU_BW = 1280, 10, 128
FFN = 2816
GRID_W = 64
ROPE_BASE = 10000.0
LRU_C = 8.0
EPS = 1e-6
Z_Q, Z_KV, Z_KR, Z_XB, Z_YB, Z_GL, Z_END = 0, 384, 640, 768, 2048, 3328, 5376
ADAM_LR, ADAM_B1, ADAM_B2, ADAM_EPS, ADAM_WD, ADAM_STEP = 0.001, 0.9, 0.999, 1e-08, 0.01, 10

VMEM_LIMIT = 52 * 1024 * 1024
FLAT_C = 512

WEIGHTS = ['c_ctx', 'w_mod', 'b_mod', 'norm1_g', 'w_in', 'b_gate', 'q_norm_g', 'kv_norm_g', 'w_uq', 'w_ukv',
           'w_o_attn', 'lru_conv_w', 'lru_conv_b', 'lru_w_a', 'lru_b_a', 'lru_w_x', 'lru_b_x', 'lru_lambda',
           'w_o_lru', 'w_out', 'norm2_g', 'w_up', 'ffn_conv_w', 'ffn_conv_b', 'w_down', 'final_g']
COL_SHARDED = ['w_in', 'w_uq', 'w_ukv', 'w_o_attn', 'lru_conv_w', 'lru_b_a', 'lru_b_x', 'lru_lambda', 'w_up',
               'ffn_conv_w']
ROW_SHARDED = ['w_o_lru', 'w_out', 'w_down']
BIG_BF16 = ['w_in', 'w_uq', 'w_ukv', 'w_o_attn', 'w_o_lru', 'w_out', 'w_up', 'w_down']
SMALL_F32 = ['lru_conv_w', 'lru_b_a', 'lru_b_x', 'lru_lambda', 'ffn_conv_w']
REPLICATED = ['c_ctx', 'b_mod', 'norm1_g', 'b_gate', 'q_norm_g', 'kv_norm_g', 'lru_conv_b', 'lru_w_a', 'lru_w_x',
              'norm2_g', 'ffn_conv_b', 'final_g']


def _cparams(sem=None):
    return pltpu.CompilerParams(dimension_semantics=sem, vmem_limit_bytes=VMEM_LIMIT)


def _pick(n, cands):
    for c in cands:
        if c <= n and n % c == 0:
            return c
    return n


def _best_div(n, mult, cap):
    best = mult
    for d in range(mult, min(n, cap) + 1, mult):
        if n % d == 0:
            best = d
    return best


MXU_DIM = 256
ROW_TILES = (1088, 1024, 544, 512, 256, 128, 64, 32, 16, 8)
LANE_TILES = (2816, 1792, 1536, 1280, 1024, 768, 512, 256, 1408, 896, 640, 384, 128)
DEPTH_ROW_TILES = (2176, 2048, 1024, 512, 256, 1088, 128, 64, 32, 16, 8)
MATMUL_VMEM_BUDGET = 40 * 1024 * 1024
MXU_FILL_OK = 0.9


def _my_pos():
    return lax.axis_index("x"), lax.axis_index("y"), lax.axis_index("c")


def _my_index():
    x, y, c = _my_pos()
    return 4 * x + 2 * y + c


def all_gather_multi(name, shards):
    n_arr = len(shards)
    arrays = range(n_arr)

    def body(*refs):
        x_refs, out_refs = refs[:n_arr], refs[n_arr:2 * n_arr]
        send_sems, recv_sems, local_sems = refs[2 * n_arr:]
        x, y, c = _my_pos()
        me, sibling = (x, y, c), (x, y, 1 - c)
        chips = [(1 - x, y), (x, 1 - y), (1 - x, 1 - y)]

        def slot(a, px, py, pc):
            return out_refs[a].at[4 * px + 2 * py + pc]

        def copy(a, k, block, to, src=None):
            return pltpu.make_async_remote_copy(
                src_ref=slot(a, *block) if src is None else src, dst_ref=slot(a, *block),
                send_sem=send_sems.at[7 * a + k], recv_sem=recv_sems.at[7 * a + k], device_id=to,
                device_id_type=MESH)

        mine = [pltpu.make_async_copy(x_refs[a], slot(a, *me), local_sems.at[a]) for a in arrays]
        first = [copy(a, 1 + j, me, (*chip, c), src=x_refs[a]) for j, chip in enumerate(chips) for a in arrays]
        first += [copy(a, 0, me, sibling, src=x_refs[a]) for a in arrays]
        for cp in first + mine:
            cp.start()
        passed = []
        for j, chip in enumerate(chips):
            for a in arrays:
                copy(a, 1 + j, (*chip, c), me).wait_recv()
                passed.append(copy(a, 4 + j, (*chip, c), sibling))
                passed[-1].start()
        for a in arrays:
            copy(a, 0, sibling, me).wait_recv()
            for j, chip in enumerate(chips):
                copy(a, 4 + j, (*chip, 1 - c), me).wait_recv()
        for cp in first + passed:
            cp.wait_send()
        for cp in mine:
            cp.wait()

    hbm = pl.BlockSpec(memory_space=pl.ANY)
    return pl.pallas_call(
        body, name=name,
        out_shape=[jax.ShapeDtypeStruct((N_DEV,) + s.shape, s.dtype) for s in shards],
        in_specs=[hbm] * n_arr, out_specs=[hbm] * n_arr,
        scratch_shapes=[pltpu.SemaphoreType.DMA((7 * n_arr,)), pltpu.SemaphoreType.DMA((7 * n_arr,)),
                        pltpu.SemaphoreType.DMA((n_arr,))],
    )(*shards)


def all_gather(name, shard):
    def body(x_ref, out_ref, send_sems, recv_sems, local_sem):
        x, y, c = _my_pos()
        me = 4 * x + 2 * y + c
        mine = pltpu.make_async_copy(x_ref, out_ref.at[me], local_sem)
        mine.start()
        sends, arrivals = [], []
        for k, peer_pos, peer in _peers():
            sems = dict(send_sem=send_sems.at[k], recv_sem=recv_sems.at[k], device_id_type=MESH)
            sends.append(pltpu.make_async_remote_copy(src_ref=x_ref, dst_ref=out_ref.at[me], device_id=peer_pos, **sems))
            arrivals.append(pltpu.make_async_remote_copy(src_ref=x_ref, dst_ref=out_ref.at[peer],
                                                         device_id=(x, y, c), **sems))
        for cp in sends:
            cp.start()
        for cp in arrivals:
            cp.wait_recv()
        for cp in sends:
            cp.wait_send()
        mine.wait()

    hbm = pl.BlockSpec(memory_space=pl.ANY)
    return pl.pallas_call(
        body, name=name,
        out_shape=jax.ShapeDtypeStruct((N_DEV,) + shard.shape, shard.dtype),
        in_specs=[hbm], out_specs=hbm,
        scratch_shapes=[pltpu.SemaphoreType.DMA((N_DEV - 1,)), pltpu.SemaphoreType.DMA((N_DEV - 1,)),
                        pltpu.SemaphoreType.DMA],
    )(shard)


def _peers():
    x, y, c = _my_pos()
    out = []
    for rel in (6, 4, 2, 7, 5, 3, 1):
        px, py, pc = x ^ ((rel >> 2) & 1), y ^ ((rel >> 1) & 1), c ^ (rel & 1)
        out.append((rel - 1, (px, py, pc), 4 * px + 2 * py + pc))
    return out


def _exchange_copies(mode, src_refs, land_refs, send_sems, recv_sems):
    x, y, c = _my_pos()
    me = 4 * x + 2 * y + c
    sends, arrivals = [], []
    for k, peer_pos, peer in _peers():
        for a, (src, land) in enumerate(zip(src_refs, land_refs)):
            piece = src.at[peer] if mode == 'scatter' else src
            sems = dict(send_sem=send_sems[a].at[k], recv_sem=recv_sems[a].at[k], device_id_type=MESH)
            sends.append(pltpu.make_async_remote_copy(src_ref=piece, dst_ref=land.at[me], device_id=peer_pos, **sems))
            arrivals.append(pltpu.make_async_remote_copy(src_ref=piece, dst_ref=land.at[peer], device_id=(x, y, c), **sems))
    return sends, arrivals


_HBM = pl.BlockSpec(memory_space=pltpu.HBM)
_SEM = pl.BlockSpec(memory_space=pltpu.SEMAPHORE)


def exchange_start(name, mode, arrays, after=()):
    n_arr, n_after = len(arrays), len(after)
    land_shapes = [a.shape if mode == 'scatter' else (N_DEV,) + a.shape for a in arrays]

    def body(*refs):
        src_refs, land_refs = refs[:n_arr], refs[n_arr:2 * n_arr]
        refs = refs[n_after:]
        send_sems, recv_sems = refs[2 * n_arr:3 * n_arr], refs[3 * n_arr:4 * n_arr]
        sends, _ = _exchange_copies(mode, src_refs, land_refs, send_sems, recv_sems)
        for cp in sends:
            cp.start()
        token = refs[-1]
        token[...] = jnp.zeros_like(token)

    sem = pltpu.SemaphoreType.DMA((N_DEV - 1,))
    res = pl.pallas_call(
        body, name=name,
        out_shape=[sem] * (2 * n_arr) + [pltpu.HBM(a.shape, a.dtype) for a in arrays]
        + [pltpu.HBM(s, a.dtype) for s, a in zip(land_shapes, arrays)] + [jax.ShapeDtypeStruct((8, 128), F32)],
        in_specs=[_HBM] * (2 * n_arr) + [pl.BlockSpec(memory_space=pl.ANY)] * n_after,
        out_specs=[_SEM] * (2 * n_arr) + [_HBM] * (2 * n_arr) + [pl.BlockSpec(memory_space=pltpu.VMEM)],
        input_output_aliases={i: 2 * n_arr + i for i in range(2 * n_arr)},
        compiler_params=pltpu.CompilerParams(has_side_effects=pltpu.SideEffectType.DATAFLOW_SIDE_EFFECTING),
    )(*[pltpu.with_memory_space_constraint(a, pltpu.HBM) for a in arrays],
      *[pltpu.with_memory_space_constraint(lax.empty(s, a.dtype), pltpu.HBM) for s, a in zip(land_shapes, arrays)],
      *after)
    return (res[:n_arr], res[n_arr:2 * n_arr], res[2 * n_arr:3 * n_arr], res[3 * n_arr:4 * n_arr]), res[-1]


def exchange_wait(name, mode, started, after):
    send_sems, recv_sems, thru, land = started
    n_arr = len(thru)

    def body(*refs):
        src_refs, land_refs = refs[:n_arr], refs[n_arr:2 * n_arr]
        s_sems, r_sems = refs[2 * n_arr:3 * n_arr], refs[3 * n_arr:4 * n_arr]
        sends, arrivals = _exchange_copies(mode, src_refs, land_refs, s_sems, r_sems)
        for cp in sends:
            cp.wait_send()
        for cp in arrivals:
            cp.wait_recv()

    res = pl.pallas_call(
        body, name=name,
        out_shape=[pltpu.HBM(a.shape, a.dtype) for a in thru] + [pltpu.HBM(a.shape, a.dtype) for a in land],
        in_specs=[_HBM] * (2 * n_arr) + [_SEM] * (2 * n_arr) + [pl.BlockSpec(memory_space=pl.ANY)],
        out_specs=[_HBM] * (2 * n_arr),
        input_output_aliases={i: i for i in range(2 * n_arr)},
        compiler_params=pltpu.CompilerParams(has_side_effects=pltpu.SideEffectType.DATAFLOW_SIDE_EFFECTING),
    )(*thru, *land, *send_sems, *recv_sems, after)
    return res[n_arr:]


def _sum_with_own(slot_ref, own_ref):
    x, y, c = _my_pos()
    me = 4 * x + 2 * y + c
    acc = None
    for p in range(N_DEV):
        v = jnp.where(me == p, own_ref[0], slot_ref[p]).astype(F32)
        acc = v if acc is None else acc + v
    return acc


def reduce_slots(name, slots, own, step=None):
    _, r, ccols = slots.shape
    tc = _pick(ccols, (256, 128))
    c1 = 1.0 - ADAM_B1 ** ADAM_STEP
    c2 = 1.0 - ADAM_B2 ** ADAM_STEP

    def body(s_ref, own_ref, *refs):
        g = _sum_with_own(s_ref, own_ref)
        if step is None:
            refs[0][...] = g
            return
        w_ref, m_ref, v_ref, g_ref, d_ref, nm_ref, nv_ref = refs
        nm = ADAM_B1 * m_ref[...] + (1.0 - ADAM_B1) * g
        nv = ADAM_B2 * v_ref[...] + (1.0 - ADAM_B2) * (g * g)
        g_ref[...] = g
        d_ref[...] = -ADAM_LR * ((nm / c1) / (jnp.sqrt(nv / c2) + ADAM_EPS) + ADAM_WD * w_ref[...])
        nm_ref[...] = nm
        nv_ref[...] = nv

    col = pl.BlockSpec((r, tc), lambda j: (0, j))
    n_out = 1 if step is None else 4
    res = pl.pallas_call(
        body, name=name, grid=(ccols // tc,),
        out_shape=[jax.ShapeDtypeStruct((r, ccols), F32)] * n_out,
        in_specs=[pl.BlockSpec((N_DEV, r, tc), lambda j: (0, 0, j)), pl.BlockSpec((1, r, tc), lambda j: (0, 0, j))]
        + [col] * (0 if step is None else 3),
        out_specs=[col] * n_out,
        compiler_params=_cparams(("parallel",)),
    )(slots, own, *(step or ()))
    return res[0] if step is None else res


def sum_slots(name, slots):
    _, r, ccols = slots.shape
    tc = _pick(ccols, (256, 128))

    def body(s_ref, o_ref):
        acc = s_ref[0].astype(F32)
        for p in range(1, N_DEV):
            acc = acc + s_ref[p].astype(F32)
        o_ref[...] = acc

    return pl.pallas_call(
        body, name=name, grid=(ccols // tc,),
        out_shape=jax.ShapeDtypeStruct((r, ccols), F32),
        in_specs=[pl.BlockSpec((N_DEV, r, tc), lambda j: (0, 0, j))],
        out_specs=pl.BlockSpec((r, tc), lambda j: (0, j)),
        compiler_params=_cparams(("parallel",)),
    )(slots)


def _mxu_fill(t):
    return t / (-(-t // MXU_DIM) * MXU_DIM)


def _matmul_tiles(mode, m_extent, n, k_extent, k_total, itemsizes):
    a_bytes, b_bytes, o_bytes = itemsizes
    m_cands = [c for c in (LANE_TILES if mode == 'tn' else ROW_TILES) if m_extent % c == 0] or [m_extent]
    k_cands = [c for c in (DEPTH_ROW_TILES if mode == 'tn' else LANE_TILES) if k_extent % c == 0] or [k_extent]
    n_cands = [c for c in LANE_TILES if n % c == 0] or [n]
    best = None
    for tm in m_cands:
        for tk in k_cands:
            for tn in n_cands:
                f32_tiles = 2 if k_total // tk > 1 else 1
                vmem = 2 * (tm * tk * a_bytes + tk * tn * b_bytes + tm * tn * o_bytes) + tm * tn * 4 * f32_tiles
                if vmem > MATMUL_VMEM_BUDGET:
                    continue
                key = (_mxu_fill(tn) * _mxu_fill(tk) >= MXU_FILL_OK, tm * tn * tk)
                if best is None or key > best[0]:
                    best = (key, (tm, tn, tk))
    assert best is not None, (mode, m_extent, n, k_extent)
    return best[1]


def matmul(name, a, b, mode, out_dtype, after=()):
    after = [t for t in after if t is not None]
    pieces, a_rows, a_cols = (1,) + a.shape if a.ndim == 2 else a.shape
    if mode == 'nn':
        (m, k), (k2, n) = (a_rows, pieces * a_cols), b.shape
    elif mode == 'nt':
        (m, k), (n, k2) = (a_rows, pieces * a_cols), b.shape
    else:
        (k, m), (k2, n) = (a_rows, pieces * a_cols), b.shape
    assert k == k2, (name, a.shape, b.shape, mode)
    tm, tn, tk = _matmul_tiles(mode, a_cols if mode == 'tn' else m, n, k if mode == 'tn' else a_cols, k,
                               (a.dtype.itemsize, b.dtype.itemsize, jnp.dtype(out_dtype).itemsize))
    nk = k // tk
    per_piece = a_cols // (tm if mode == 'tn' else tk)
    if a.ndim == 2:
        a_block = lambda rows, cols, at: pl.BlockSpec((rows, cols), at)
    else:
        a_block = lambda rows, cols, at: pl.BlockSpec(
            (None, rows, cols), lambda i, j, kk: (at(i, j, kk)[1] // per_piece, at(i, j, kk)[0],
                                                  at(i, j, kk)[1] % per_piece))
    if mode == 'nn':
        a_spec = a_block(tm, tk, lambda i, j, kk: (i, kk))
        b_spec = pl.BlockSpec((tk, tn), lambda i, j, kk: (kk, j))
        dn = (((1,), (0,)), ((), ()))
    elif mode == 'nt':
        a_spec = a_block(tm, tk, lambda i, j, kk: (i, kk))
        b_spec = pl.BlockSpec((tn, tk), lambda i, j, kk: (j, kk))
        dn = (((1,), (1,)), ((), ()))
    else:
        a_spec = a_block(tk, tm, lambda i, j, kk: (kk, i))
        b_spec = pl.BlockSpec((tk, tn), lambda i, j, kk: (kk, j))
        dn = (((0,), (0,)), ((), ()))

    def product(a_ref, b_ref):
        return lax.dot_general(a_ref[...].astype(BF16), b_ref[...].astype(BF16), dn, preferred_element_type=F32)

    n_after = len(after)

    def body_one(a_ref, b_ref, *rest):
        o_ref = rest[n_after]
        o_ref[...] = product(a_ref, b_ref).astype(o_ref.dtype)

    def body(a_ref, b_ref, *rest):
        o_ref, acc_ref = rest[n_after:]
        kk = pl.program_id(2)

        @pl.when(kk == 0)
        def _():
            acc_ref[...] = jnp.zeros_like(acc_ref)

        acc_ref[...] += product(a_ref, b_ref)

        @pl.when(kk == nk - 1)
        def _():
            o_ref[...] = acc_ref[...].astype(o_ref.dtype)

    return pl.pallas_call(
        body_one if nk == 1 else body, name=name, grid=(m // tm, n // tn, nk),
        out_shape=jax.ShapeDtypeStruct((m, n), out_dtype),
        in_specs=[a_spec, b_spec] + [pl.BlockSpec(memory_space=pl.ANY)] * n_after,
        out_specs=pl.BlockSpec((tm, tn), lambda i, j, kk: (i, j)),
        scratch_shapes=[] if nk == 1 else [pltpu.VMEM((tm, tn), F32)],
        compiler_params=_cparams(("parallel", "parallel", "arbitrary")),
    )(a, b, *after)


def rowwise(name, fn, rows, params, out_rows, out_accs, n_rows, t_lat, tm):
    nb, nbl = n_rows // tm, t_lat // tm
    in_specs, piece_counts = [], []
    operands = []
    for arr, off, width, *kind in rows:
        g = math.gcd(off, width) if off else width
        assert g % 128 == 0 or (off == 0 and width == arr.shape[1]), (name, off, width)
        cnt = width // g
        last = arr.shape[0] // tm - 1
        clamp = arr.shape[0] < n_rows
        for p in range(cnt):
            cb = off // g + p
            if kind == ['ctx']:
                in_specs.append(pl.BlockSpec(
                    (tm, g), lambda i, cb=cb, last=last: (jnp.clip(i - nbl, 0, last), cb)))
            elif clamp:
                in_specs.append(pl.BlockSpec((tm, g), lambda i, cb=cb, last=last: (jnp.minimum(i, last), cb)))
            else:
                in_specs.append(pl.BlockSpec((tm, g), lambda i, cb=cb: (i, cb)))
            operands.append(arr)
        piece_counts.append(cnt)
    for p in params:
        in_specs.append(pl.BlockSpec(p.shape, lambda i, nd=p.ndim: (0,) * nd))
        operands.append(p)
    n_in = sum(piece_counts)
    n_par = len(params)
    n_or = len(out_rows)
    lat_only = [kind == ['lat'] for _, _, *kind in out_rows]
    out_shape = [jax.ShapeDtypeStruct((t_lat if lat else n_rows, w), dt)
                 for (w, dt, *_), lat in zip(out_rows, lat_only)]
    out_shape += [jax.ShapeDtypeStruct(s, F32) for s in out_accs]
    out_specs = [pl.BlockSpec((tm, w), (lambda i: (jnp.minimum(i, nbl - 1), 0)) if lat else (lambda i: (i, 0)))
                 for (w, *_), lat in zip(out_rows, lat_only)]
    out_specs += [pl.BlockSpec(s, lambda i, nd=len(s): (0,) * nd) for s in out_accs]

    def body(*refs):
        in_refs, par_refs = refs[:n_in], refs[n_in:n_in + n_par]
        orow_refs = refs[n_in + n_par:n_in + n_par + n_or]
        oacc_refs = refs[n_in + n_par + n_or:]
        i = pl.program_id(0)
        tiles, at = [], 0
        for cnt in piece_counts:
            parts = [in_refs[at + p][...].astype(F32) for p in range(cnt)]
            tiles.append(parts[0] if cnt == 1 else jnp.concatenate(parts, axis=1))
            at += cnt
        is_ctx = i * tm >= t_lat
        outs, accs = fn(is_ctx, tiles, [p[...] for p in par_refs])
        for o_ref, o, lat in zip(orow_refs, outs, lat_only):
            if lat:
                @pl.when(jnp.logical_not(is_ctx))
                def _(o_ref=o_ref, o=o):
                    o_ref[...] = o.astype(o_ref.dtype)
            else:
                o_ref[...] = o.astype(o_ref.dtype)
        if oacc_refs:
            @pl.when(i == 0)
            def _():
                for a_ref in oacc_refs:
                    a_ref[...] = jnp.zeros_like(a_ref)
            for a_ref, a in zip(oacc_refs, accs):
                a_ref[...] += a.astype(F32)

    res = pl.pallas_call(
        body, name=name, grid=(nb,),
        out_shape=out_shape, in_specs=in_specs, out_specs=out_specs,
        compiler_params=_cparams(("arbitrary",)),
    )(*operands)
    return res[:n_or], res[n_or:]


def _rms(x, g):
    return x * lax.rsqrt(jnp.mean(x * x, axis=-1, keepdims=True) + EPS) * g


def _norm_mod(x, g, sc, sh):
    return _rms(x, g) * (1.0 + sc) + sh


def _sigmoid(x):
    return 0.5 * jnp.tanh(0.5 * x) + 0.5


def _silu(x):
    return x * _sigmoid(x)


def _gelu(x):
    return 0.5 * x * (1.0 + jnp.tanh(math.sqrt(2.0 / math.pi) * (x + 0.044715 * (x * x * x))))


def _sel(is_ctx, p):
    return jnp.where(is_ctx, p[1:2], p[0:1])


def _seg_acc(is_ctx, v):
    rows = lax.broadcasted_iota(jnp.int32, (2, v.shape[1]), 0)
    return jnp.where(rows == is_ctx.astype(jnp.int32), jnp.broadcast_to(v, (2, v.shape[1])), 0.0)


def _rsum(v):
    return jnp.sum(v, axis=0, keepdims=True)


def _shift_rows(x, o, t_lat, n):
    if o == 0:
        return x
    y = pltpu.roll(x, (-o) % n, 0)
    t = lax.broadcasted_iota(jnp.int32, x.shape, 0)
    if o > 0:
        ok = t < n - o
        if t_lat < n:
            ok = ok & ((t < t_lat - o) | (t >= t_lat))
    else:
        ok = t >= -o
        if t_lat < n:
            ok = ok & ((t < t_lat) | (t >= t_lat - o))
    return jnp.where(ok, y, 0.0)


def conv_fwd(name, xarr, col_off, width, w, b, left, n_rows, t_lat, out_dtype, cb=128):
    taps = w.shape[0]
    assert col_off % cb == 0 and width % cb == 0

    def body(x_ref, w_ref, b_ref, o_ref):
        x = x_ref[...].astype(F32)
        acc = jnp.broadcast_to(b_ref[...], x.shape)
        for k in range(taps):
            acc = acc + _shift_rows(x, k - left, t_lat, n_rows) * w_ref[k:k + 1, :]
        o_ref[...] = acc.astype(o_ref.dtype)

    return pl.pallas_call(
        body, name=name, grid=(width // cb,),
        out_shape=jax.ShapeDtypeStruct((n_rows, width), out_dtype),
        in_specs=[pl.BlockSpec((n_rows, cb), lambda j: (0, col_off // cb + j)),
                  pl.BlockSpec((taps, cb), lambda j: (0, j)),
                  pl.BlockSpec((1, cb), lambda j: (0, j))],
        out_specs=pl.BlockSpec((n_rows, cb), lambda j: (0, j)),
        compiler_params=_cparams(("parallel",)),
    )(xarr, w, b)


def conv_bwd(name, dout, xarr, col_off, width, w, left, n_rows, t_lat, cb=128):
    taps = w.shape[0]

    def body(d_ref, x_ref, w_ref, dx_ref, dw_ref, db_ref):
        d = d_ref[...].astype(F32)
        x = x_ref[...].astype(F32)
        dx = jnp.zeros_like(d)
        dws = []
        for k in range(taps):
            dx = dx + _shift_rows(d, left - k, t_lat, n_rows) * w_ref[k:k + 1, :]
            dws.append(_rsum(d * _shift_rows(x, k - left, t_lat, n_rows)))
        dx_ref[...] = dx.astype(dx_ref.dtype)
        dw_ref[...] = jnp.concatenate(dws, axis=0)
        db_ref[...] = _rsum(d)

    return pl.pallas_call(
        body, name=name, grid=(width // cb,),
        out_shape=[jax.ShapeDtypeStruct((n_rows, width), BF16), jax.ShapeDtypeStruct((taps, width), F32),
                   jax.ShapeDtypeStruct((1, width), F32)],
        in_specs=[pl.BlockSpec((n_rows, cb), lambda j: (0, j)),
                  pl.BlockSpec((n_rows, cb), lambda j: (0, col_off // cb + j)),
                  pl.BlockSpec((taps, cb), lambda j: (0, j))],
        out_specs=[pl.BlockSpec((n_rows, cb), lambda j: (0, j)), pl.BlockSpec((taps, cb), lambda j: (0, j)),
                   pl.BlockSpec((1, cb), lambda j: (0, j))],
        compiler_params=_cparams(("parallel",)),
    )(dout, xarr, w)


def _ffn_conv(a, w_ref, b_ref, t_lat):
    shifted = [_shift_rows(a, k - 1, t_lat, t_lat) for k in range(3)]
    ac = jnp.broadcast_to(b_ref[...], a.shape)
    for k in range(3):
        ac = ac + shifted[k] * w_ref[k:k + 1, :]
    return ac, shifted


def ffn_mix_fwd(u, w, b, t_lat, cb=128):
    nblk = FFN // cb

    def body(a_ref, g_ref, w_ref, b_ref, f_ref):
        ac, _ = _ffn_conv(a_ref[...].astype(F32), w_ref, b_ref, t_lat)
        f_ref[...] = (_silu(ac) * g_ref[...].astype(F32)).astype(f_ref.dtype)

    col = lambda shape, off=0: pl.BlockSpec(shape, lambda j: (0, off + j))
    return pl.pallas_call(
        body, name="ffn_mix", grid=(nblk,),
        out_shape=jax.ShapeDtypeStruct((t_lat, FFN), BF16),
        in_specs=[col((t_lat, cb)), col((t_lat, cb), nblk), col((3, cb)), col((1, cb))],
        out_specs=col((t_lat, cb)),
        compiler_params=_cparams(("parallel",)),
    )(u, u, w, b)


def ffn_mix_bwd(u, df, w, b, t_lat, cb=128):
    nblk = FFN // cb

    def body(a_ref, g_ref, df_ref, w_ref, b_ref, du_ref, dw_ref, db_ref):
        ac, shifted = _ffn_conv(a_ref[...].astype(F32), w_ref, b_ref, t_lat)
        d = df_ref[...].astype(F32)
        s = _sigmoid(ac)
        du_ref[1] = (d * (ac * s)).astype(du_ref.dtype)
        dac = d * g_ref[...].astype(F32) * (s * (1.0 + ac * (1.0 - s)))
        da = jnp.zeros_like(dac)
        for k in range(3):
            da = da + _shift_rows(dac, 1 - k, t_lat, t_lat) * w_ref[k:k + 1, :]
        du_ref[0] = da.astype(du_ref.dtype)
        dw_ref[...] = jnp.concatenate([_rsum(dac * shifted[k]) for k in range(3)], axis=0)
        db_ref[...] = _rsum(dac)

    col = lambda shape, off=0: pl.BlockSpec(shape, lambda j: (0, off + j))
    return pl.pallas_call(
        body, name="ffn_mix_bwd", grid=(nblk,),
        out_shape=[jax.ShapeDtypeStruct((2, t_lat, FFN), BF16),
                   jax.ShapeDtypeStruct((3, FFN), F32), jax.ShapeDtypeStruct((1, FFN), F32)],
        in_specs=[col((t_lat, cb)), col((t_lat, cb), nblk), col((t_lat, cb)), col((3, cb)), col((1, cb))],
        out_specs=[pl.BlockSpec((2, t_lat, cb), lambda j: (0, 0, j)), col((3, cb)), col((1, cb))],
        compiler_params=_cparams(("parallel",)),
    )(u, u, df, w, b)


def _chunk_order(direction, nb, nbl):
    if direction == 'f':
        return lambda s: ((s + nbl) % nb, 0)
    return lambda s: (nb - 1 - s, 0)


def _adjoint_order(direction, nb, nbl):
    if direction == 'f':
        return lambda s: ((nb - 1 - s + nbl) % nb, 0)
    return lambda s: (s, 0)


SUBLANES = 8


def _chunk_scan(a, b, carry, rev):
    tc, width = a.shape
    nt = tc // SUBLANES
    row = lax.broadcasted_iota(jnp.int32, a.shape, 0)
    a, b = a.reshape(nt, SUBLANES, width), b.reshape(nt, SUBLANES, width)
    in_tile = lax.broadcasted_iota(jnp.int32, a.shape, 1)
    for k in (1, 2, 4):
        shift = SUBLANES - k if rev else k
        edge = in_tile >= SUBLANES - k if rev else in_tile < k
        b = jnp.where(edge, b, a * pltpu.roll(b, shift, 1) + b)
        a = jnp.where(edge, a, a * pltpu.roll(a, shift, 1))
    a, b = a.reshape(tc, width), b.reshape(tc, width)
    hs = [None] * nt
    c = carry
    for kt in range(nt):
        k = nt - 1 - kt if rev else kt
        h = b[k * SUBLANES:(k + 1) * SUBLANES] + a[k * SUBLANES:(k + 1) * SUBLANES] * c
        hs[k] = h
        c = h[0:1] if rev else h[SUBLANES - 1:SUBLANES]
    h = jnp.concatenate(hs, axis=0)
    if rev:
        return h, jnp.where(row == tc - 1, carry, pltpu.roll(h, tc - 1, 0)), c
    return h, jnp.where(row == 0, carry, pltpu.roll(h, 1, 0)), c


def scan_fwd(name, a, u, direction, n_rows, t_lat):
    w = a.shape[1]
    tc = _pick(math.gcd(t_lat, n_rows), (256, 128))
    nb, nbl = n_rows // tc, t_lat // tc
    order = _chunk_order(direction, nb, nbl)
    rev = direction == 'b'

    def body(a_ref, u_ref, h_ref, hp_ref, carry):
        @pl.when(pl.program_id(0) == 0)
        def _():
            carry[...] = jnp.zeros_like(carry)

        h_ref[...], hp_ref[...], carry[...] = _chunk_scan(a_ref[...], u_ref[...], carry[...], rev)

    spec = pl.BlockSpec((tc, w), order)
    return pl.pallas_call(
        body, name=name, grid=(nb,),
        out_shape=[jax.ShapeDtypeStruct((n_rows, w), F32)] * 2,
        in_specs=[spec, spec], out_specs=[spec, spec],
        scratch_shapes=[pltpu.VMEM((1, w), F32)],
        compiler_params=_cparams(("arbitrary",)),
    )(a, u)


def scan_adj(name, a, dh, hprev, direction, n_rows, t_lat):
    w = a.shape[1]
    tc = _pick(math.gcd(t_lat, n_rows), (256, 128))
    nb, nbl = n_rows // tc, t_lat // tc
    order = _adjoint_order(direction, nb, nbl)
    rev = direction == 'f'

    def dh_order(s):
        c, _ = order(s)
        return (jnp.minimum(c, nbl - 1), 0)

    def body(a_ref, dh_ref, hp_ref, du_ref, da_ref, carry):
        s = pl.program_id(0)

        @pl.when(s == 0)
        def _():
            carry[...] = jnp.zeros_like(carry)

        chunk, _ = order(s)
        live = (chunk < nbl).astype(F32)

        av = a_ref[...]
        dv = dh_ref[...] * live
        _, c_next, carry[...] = _chunk_scan(av, av * dv, carry[...], rev)
        lam = dv + c_next
        du_ref[...] = lam
        da_ref[...] = lam * hp_ref[...]

    spec = pl.BlockSpec((tc, w), order)
    return pl.pallas_call(
        body, name=name, grid=(nb,),
        out_shape=[jax.ShapeDtypeStruct((n_rows, w), F32)] * 2,
        in_specs=[spec, pl.BlockSpec((tc, w), dh_order), spec], out_specs=[spec, spec],
        scratch_shapes=[pltpu.VMEM((1, w), F32)],
        compiler_params=_cparams(("arbitrary",)),
    )(a, dh, hprev)


def _one_minus_a_squared(log_a, a):
    return (1.0 + a * a) * jnp.tanh(-log_a)


def _gate_elem(pre_r, pre_i, xc, b_a, b_x, sp):
    r = _sigmoid(pre_r + b_a)
    i = _sigmoid(pre_i + b_x)
    log_a = (-LRU_C) * r * sp
    a = jnp.exp(log_a)
    m2 = _one_minus_a_squared(log_a, a)
    mult = jnp.where(m2 > 0.0, m2 * lax.rsqrt(m2), 0.0)
    return a, mult * (i * xc)


def _gate_elem_bwd(pre_r, pre_i, xc, b_a, b_x, sp, da, du):
    r = _sigmoid(pre_r + b_a)
    i = _sigmoid(pre_i + b_x)
    log_a = (-LRU_C) * r * sp
    a = jnp.exp(log_a)
    m2 = _one_minus_a_squared(log_a, a)
    inv_mult = lax.rsqrt(m2)
    g = du * (m2 * inv_mult)
    d_mult = du * (i * xc)
    d_log_a = (da - d_mult * a * inv_mult) * a
    d_pre_r = d_log_a * ((-LRU_C) * sp) * (r * (1.0 - r))
    d_pre_i = g * xc * (i * (1.0 - i))
    return d_pre_r, d_pre_i, g * i, _rsum(d_log_a * ((-LRU_C) * r))


def _blockdiag(xb16, w_ref_val, d):
    outs = []
    for n in range(LRU_BLOCKS):
        outs.append(jnp.dot(xb16[:, n * LRU_BW:(n + 1) * LRU_BW], w_ref_val[d * LRU_BLOCKS + n],
                            preferred_element_type=F32))
    return jnp.concatenate(outs, axis=1)


def gates_fwd(xc, w_a, w_x, b_a, b_x, sp, n_rows, t_lat, tm):
    def fn(is_ctx, rows, params):
        (x,), (wa, wx, ba, bx, spv) = rows, params
        xb16 = x.astype(BF16)
        outs = []
        for d in range(2):
            a, u = _gate_elem(_blockdiag(xb16, wa, d), _blockdiag(xb16, wx, d), x,
                              ba[d:d + 1], bx[d:d + 1], spv[d:d + 1])
            outs += [a, u]
        return outs, []

    (a_f, u_f, a_b, u_b), _ = rowwise("gates_fwd", fn, [(xc, 0, LRU_W)], [w_a, w_x, b_a, b_x, sp],
                                      [(LRU_W, F32)] * 4, [], n_rows, t_lat, tm)
    return a_f, u_f, a_b, u_b


def gates_bwd(xc, da_f, du_f, da_b, du_b, w_a, w_x, b_a, b_x, sp, n_rows, t_lat, tm):
    def fn(is_ctx, rows, params):
        (x, daf, duf, dab, dub), (wa, wx, ba, bx, spv) = rows, params
        xb16 = x.astype(BF16)
        dxc = jnp.zeros_like(x)
        dwa, dwx, dba, dbx, dsp = [], [], [], [], []
        for d, (da, du) in enumerate(((daf, duf), (dab, dub))):
            dpr, dpi, dx_e, dsp_d = _gate_elem_bwd(_blockdiag(xb16, wa, d), _blockdiag(xb16, wx, d), x,
                                                   ba[d:d + 1], bx[d:d + 1], spv[d:d + 1], da, du)
            dba_d, dbx_d = _rsum(dpr), _rsum(dpi)
            dxc = dxc + dx_e
            dpr16, dpi16 = dpr.astype(BF16), dpi.astype(BF16)
            back = []
            for n in range(LRU_BLOCKS):
                sl = slice(n * LRU_BW, (n + 1) * LRU_BW)
                nt_dims = (((1,), (1,)), ((), ()))
                back.append(lax.dot_general(dpr16[:, sl], wa[d * LRU_BLOCKS + n], nt_dims, preferred_element_type=F32)
                            + lax.dot_general(dpi16[:, sl], wx[d * LRU_BLOCKS + n], nt_dims,
                                              preferred_element_type=F32))
                tn_dims = (((0,), (0,)), ((), ()))
                dwa.append(lax.dot_general(xb16[:, sl], dpr16[:, sl], tn_dims, preferred_element_type=F32)[None])
                dwx.append(lax.dot_general(xb16[:, sl], dpi16[:, sl], tn_dims, preferred_element_type=F32)[None])
            dxc = dxc + jnp.concatenate(back, axis=1)
            dba.append(dba_d)
            dbx.append(dbx_d)
            dsp.append(dsp_d)
        cat0 = lambda xs: jnp.concatenate(xs, axis=0)
        return [dxc], [cat0(dwa), cat0(dwx), cat0(dba), cat0(dbx), cat0(dsp)]

    (dxc,), accs = rowwise("gates_bwd", fn,
                           [(xc, 0, LRU_W), (da_f, 0, LRU_W), (du_f, 0, LRU_W), (da_b, 0, LRU_W), (du_b, 0, LRU_W)],
                           [w_a, w_x, b_a, b_x, sp], [(LRU_W, F32)],
                           [(2 * LRU_BLOCKS, LRU_BW, LRU_BW)] * 2 + [(2, LRU_W)] * 3, n_rows, t_lat, tm)
    return dxc, accs


def _rope_tables(t_lat, n_rows):
    rows = t_lat // GRID_W
    row_ids = jnp.repeat(jnp.arange(rows), GRID_W).astype(F32)
    col_ids = jnp.tile(jnp.arange(GRID_W), rows).astype(F32)
    axis_dim = QK_ROPE // 2
    inv = 1.0 / (ROPE_BASE ** (jnp.arange(0, axis_dim, 2, dtype=F32) / axis_dim))
    ang = jnp.concatenate([row_ids[:, None] * inv, col_ids[:, None] * inv], axis=-1)
    cos, sin = jnp.cos(ang), jnp.sin(ang)
    half = QK_ROPE // 2
    ones, zeros = jnp.ones((t_lat, QK_NOPE), F32), jnp.zeros((t_lat, QK_NOPE), F32)
    pad1, pad0 = jnp.ones((t_lat, HEAD_PAD - QK_DIM), F32), jnp.zeros((t_lat, HEAD_PAD - QK_DIM), F32)
    zh = jnp.zeros((t_lat, half), F32)
    c_tab = jnp.concatenate([ones, cos, cos, pad1], axis=1)
    s1 = jnp.concatenate([zeros, -sin, zh, pad0], axis=1)
    s2 = jnp.concatenate([zeros, zh, sin, pad0], axis=1)
    n_ctx = n_rows - t_lat
    c_tab = jnp.concatenate([c_tab, jnp.ones((n_ctx, HEAD_PAD), F32)], axis=0)
    s1 = jnp.concatenate([s1, jnp.zeros((n_ctx, HEAD_PAD), F32)], axis=0)
    s2 = jnp.concatenate([s2, jnp.zeros((n_ctx, HEAD_PAD), F32)], axis=0)
    return c_tab, s1, s2


def _rope(x, c, s1, s2):
    half = QK_ROPE // 2
    return x * c + pltpu.roll(x, HEAD_PAD - half, 1) * s1 + pltpu.roll(x, half, 1) * s2


def _rope_t(dy, c, s1, s2):
    half = QK_ROPE // 2
    return dy * c + pltpu.roll(dy * s1, half, 1) + pltpu.roll(dy * s2, HEAD_PAD - half, 1)


def _heads(x):
    return [x[:, h * HEAD_PAD:(h + 1) * HEAD_PAD] for h in range(N_HEADS)]


Q_SCALE = QK_DIM ** -0.5 * math.log2(math.e)

def attn_fwd(q, k, v, t_lat, n_rows, tq):
    def body(q_ref, k_ref, v_ref, o_ref, lse_ref):
        s = lax.dot_general(q_ref[...], k_ref[...], (((1,), (1,)), ((), ())), preferred_element_type=F32)
        m = jnp.max(s, axis=-1, keepdims=True)
        p = jnp.exp2(s - m)
        l = jnp.sum(p, axis=-1, keepdims=True)
        o = jnp.dot(p.astype(BF16), v_ref[...], preferred_element_type=F32) / l
        o_ref[...] = o.astype(o_ref.dtype)
        lse_ref[...] = jnp.broadcast_to(m + jnp.log2(l), lse_ref.shape)

    qspec = pl.BlockSpec((tq, HEAD_PAD), lambda h, i: (i, h))
    kspec = pl.BlockSpec((n_rows, HEAD_PAD), lambda h, i: (0, h))
    return pl.pallas_call(
        body, name="attn_fwd", grid=(N_HEADS, t_lat // tq),
        out_shape=[jax.ShapeDtypeStruct((t_lat, N_HEADS * HEAD_PAD), BF16),
                   jax.ShapeDtypeStruct((t_lat, N_HEADS * HEAD_PAD), F32)],
        in_specs=[qspec, kspec, kspec], out_specs=[qspec, qspec],
        compiler_params=_cparams(("parallel", "arbitrary")),
    )(q, k, v)


def attn_bwd(q, k, v, o, do, lse, t_lat, n_rows, tq):
    scale = QK_DIM ** -0.5
    nq = t_lat // tq
    nt = (((1,), (1,)), ((), ()))
    tn = (((0,), (0,)), ((), ()))

    def body(q_ref, k_ref, v_ref, o_ref, do_ref, lse_ref, dq_ref, dk_ref, dv_ref):
        @pl.when(pl.program_id(1) == 0)
        def _():
            dk_ref[...] = jnp.zeros_like(dk_ref)
            dv_ref[...] = jnp.zeros_like(dv_ref)

        qv, kv, vv, dov = q_ref[...], k_ref[...], v_ref[...], do_ref[...]
        s = lax.dot_general(qv, kv, nt, preferred_element_type=F32)
        p = jnp.exp2(s - lse_ref[:, 0:1])
        dv_ref[...] += lax.dot_general(p.astype(BF16), dov, tn, preferred_element_type=F32)
        dp = lax.dot_general(dov, vv, nt, preferred_element_type=F32)
        delta = jnp.sum(dov.astype(F32) * o_ref[...].astype(F32), axis=-1, keepdims=True)
        ds = (p * (dp - delta)).astype(BF16)
        dq_ref[...] = (jnp.dot(ds, kv, preferred_element_type=F32) * scale).astype(dq_ref.dtype)
        dk_ref[...] += lax.dot_general(ds, qv, tn, preferred_element_type=F32)

        @pl.when(pl.program_id(1) == nq - 1)
        def _():
            dk_ref[...] = dk_ref[...] * (scale / Q_SCALE)

    qspec = pl.BlockSpec((tq, HEAD_PAD), lambda h, i: (i, h))
    kspec = pl.BlockSpec((n_rows, HEAD_PAD), lambda h, i: (0, h))
    return pl.pallas_call(
        body, name="attn_bwd", grid=(N_HEADS, t_lat // tq),
        out_shape=[jax.ShapeDtypeStruct((t_lat, N_HEADS * HEAD_PAD), BF16),
                   jax.ShapeDtypeStruct((n_rows, N_HEADS * HEAD_PAD), F32),
                   jax.ShapeDtypeStruct((n_rows, N_HEADS * HEAD_PAD), F32)],
        in_specs=[qspec, kspec, kspec, qspec, qspec, qspec], out_specs=[qspec, kspec, kspec],
        compiler_params=_cparams(("parallel", "arbitrary")),
    )(q, k, v, o, do, lse)


def adamw(name, w, g, m, v):
    r, ccols = w.shape
    if r % 8 == 0:
        tr, tcol = _best_div(r, 8, max(8, 262144 // ccols)), ccols
    else:
        tr, tcol = r, _pick(ccols, (256, 128))
    c1 = 1.0 - ADAM_B1 ** ADAM_STEP
    c2 = 1.0 - ADAM_B2 ** ADAM_STEP

    def body(w_ref, g_ref, m_ref, v_ref, d_ref, nm_ref, nv_ref):
        gv = g_ref[...]
        nm = ADAM_B1 * m_ref[...] + (1.0 - ADAM_B1) * gv
        nv = ADAM_B2 * v_ref[...] + (1.0 - ADAM_B2) * (gv * gv)
        d_ref[...] = -ADAM_LR * ((nm / c1) / (jnp.sqrt(nv / c2) + ADAM_EPS) + ADAM_WD * w_ref[...])
        nm_ref[...] = nm
        nv_ref[...] = nv

    spec = pl.BlockSpec((tr, tcol), lambda i, j: (i, j))
    return pl.pallas_call(
        body, name=name, grid=(r // tr, ccols // tcol),
        out_shape=[jax.ShapeDtypeStruct((r, ccols), F32)] * 3,
        in_specs=[spec] * 4, out_specs=[spec] * 3,
        compiler_params=_cparams(("parallel", "parallel")),
    )(w, g, m, v)


def adamw_many(name, ws, gs, ms, vs):
    n = len(ws)
    c1 = 1.0 - ADAM_B1 ** ADAM_STEP
    c2 = 1.0 - ADAM_B2 ** ADAM_STEP

    def body(*refs):
        for i in range(n):
            w_ref, g_ref, m_ref, v_ref = (refs[k * n + i] for k in range(4))
            d_ref, nm_ref, nv_ref = (refs[(4 + k) * n + i] for k in range(3))
            gv = g_ref[...]
            nm = ADAM_B1 * m_ref[...] + (1.0 - ADAM_B1) * gv
            nv = ADAM_B2 * v_ref[...] + (1.0 - ADAM_B2) * (gv * gv)
            d_ref[...] = -ADAM_LR * ((nm / c1) / (jnp.sqrt(nv / c2) + ADAM_EPS) + ADAM_WD * w_ref[...])
            nm_ref[...] = nm
            nv_ref[...] = nv

    vmem = pl.BlockSpec(memory_space=pltpu.VMEM)
    res = pl.pallas_call(
        body, name=name,
        out_shape=[jax.ShapeDtypeStruct(w.shape, F32) for w in ws] * 3,
        in_specs=[vmem] * (4 * n), out_specs=[vmem] * (3 * n),
        compiler_params=_cparams(),
    )(*ws, *gs, *ms, *vs)
    return [tuple(res[k * n + i] for k in range(3)) for i in range(n)]


def _flat(parts, dtype, row_mult):
    v = jnp.concatenate([p.reshape(-1).astype(dtype) for p in parts])
    quantum = row_mult * FLAT_C
    total = -(-v.shape[0] // quantum) * quantum
    return jnp.pad(v, (0, total - v.shape[0])).reshape(total // FLAT_C, FLAT_C)


def _gathered_to_full(name, g):
    k = g.shape[1]
    return jnp.transpose(g, (1, 0, 2)).reshape(k, N_DEV * g.shape[2])


def _shard_to_rb(name, w):
    return w if name in ROW_SHARDED else w.T


def _rb_to_shard(name, g):
    return g if name in ROW_SHARDED else g.T


def _rb_from_gathered(name, g):
    cols = g.shape[2]
    if name == 'w_in':
        z = lambda k: jnp.zeros((k, cols), g.dtype)
        full = g.reshape(N_DEV * g.shape[1], cols)
        return jnp.concatenate([full[:Z_KR], z(QK_NOPE), full[Z_KR:Z_KR + QK_ROPE], z(HEAD_PAD - QK_DIM),
                                full[Z_KR + QK_ROPE:]], axis=0)
    if name == 'w_uq':
        return jnp.pad(g, ((0, 0), (0, HEAD_PAD - QK_DIM), (0, 0))).reshape(N_HEADS * HEAD_PAD, cols)
    if name == 'w_ukv':
        pad = lambda t: jnp.pad(t, ((0, 0), (0, HEAD_PAD - t.shape[1]), (0, 0))).reshape(N_HEADS * HEAD_PAD, cols)
        return jnp.concatenate([pad(g[:, :QK_NOPE]), pad(g[:, QK_NOPE:])], axis=0)
    if name == 'w_o_attn':
        full = g.reshape(D, N_HEADS, V_HEAD)
        return jnp.pad(full, ((0, 0), (0, 0), (0, HEAD_PAD - V_HEAD))).reshape(D, N_HEADS * HEAD_PAD)
    return g.reshape(N_DEV * g.shape[1], cols)


def _chunks_from_rb_grad(name, g):
    cols = g.shape[1]
    if name == 'w_in':
        full = jnp.concatenate([g[:Z_KR], g[Z_KR + QK_NOPE:Z_KR + QK_DIM], g[Z_XB:]], axis=0)
        return full.reshape(N_DEV, -1, cols)
    if name == 'w_uq':
        return g.reshape(N_HEADS, HEAD_PAD, cols)[:, :QK_DIM]
    if name == 'w_ukv':
        half = N_HEADS * HEAD_PAD
        gk = g[:half].reshape(N_HEADS, HEAD_PAD, cols)[:, :QK_NOPE]
        gv = g[half:].reshape(N_HEADS, HEAD_PAD, cols)[:, :V_HEAD]
        return jnp.concatenate([gk, gv], axis=1)
    if name == 'w_o_attn':
        full = g.reshape(D, N_HEADS, HEAD_PAD)[:, :, :V_HEAD].reshape(D, N_HEADS * V_HEAD)
        return full.reshape(N_DEV, D // N_DEV, N_HEADS * V_HEAD)
    return g.reshape(N_DEV, -1, cols)


def local_step(x, ctx, target, mod_l, mod_c, wt, on_grad=None, arrive=None):
    t_lat, n_ctx = x.shape[0], ctx.shape[0]
    n = t_lat + n_ctx
    tm = _pick(math.gcd(t_lat, n), (256, 128))
    tq_fwd = _pick(t_lat, (256, 128))
    tq_bwd = _pick(t_lat, (512, 256, 128))
    row = lambda v: v.reshape(1, -1).astype(F32)
    two = lambda a, b: jnp.stack([a, b]).astype(F32)
    sh1_l, sc1_l, g1_l, sh2_l, sc2_l, g2_l = jnp.split(mod_l, 6)
    sh1_c, sc1_c = jnp.split(mod_c, 6)[:2]
    sc1, sh1 = two(sc1_l, sc1_c), two(sh1_l, sh1_c)
    g1, g2, sc2, sh2 = row(g1_l), row(g2_l), row(sc2_l), row(sh2_l)
    norm1_g, norm2_g, final_g = row(wt['norm1_g']), row(wt['norm2_g']), row(wt['final_g'])
    q_g, kv_g, b_gate = row(wt['q_norm_g']), row(wt['kv_norm_g']), row(wt['b_gate'])
    wt = dict(wt)
    pending = []

    def sent():
        tokens = list(pending)
        pending.clear()
        return tokens

    def need(names, after):
        if arrive is not None:
            got = arrive(names, after)
            if '_token' in got:
                pending.append(got.pop('_token'))
            wt.update(got)
        return [wt[n] for n in names]
    lru_w_a = wt['lru_w_a'].reshape(2 * LRU_BLOCKS, LRU_BW, LRU_BW).astype(BF16)
    lru_w_x = wt['lru_w_x'].reshape(2 * LRU_BLOCKS, LRU_BW, LRU_BW).astype(BF16)
    b_a, b_x, lam = wt['lru_b_a'], wt['lru_b_x'], wt['lru_lambda']
    sp = jnp.logaddexp(-lam, 0.0)
    c_tab, s1_tab, s2_tab = _rope_tables(t_lat, n)
    rw = functools.partial(rowwise, n_rows=n, t_lat=t_lat, tm=tm)
    rw_lat = functools.partial(rowwise, n_rows=t_lat, t_lat=t_lat, tm=tm)

    stream = [(x, 0, D), (ctx, 0, D, 'ctx')]

    def f_norm1(is_ctx, rows, params):
        (xl, xc_), (g, sc, sh) = rows, params
        return [_norm_mod(jnp.where(is_ctx, xc_, xl), g, _sel(is_ctx, sc), _sel(is_ctx, sh))], []

    (h,), _ = rw("norm1", f_norm1, stream, [norm1_g, sc1, sh1], [(D, BF16)], [])
    (w_in_t,) = need(('w_in',), h)
    z = matmul("w_in", h, w_in_t, 'nt', BF16, after=sent())
    w_uq_t, w_ukv_t, w_o_lru = need(('w_uq', 'w_ukv', 'w_o_lru'), z)

    def f_qkv_norm(is_ctx, rows, params):
        (ql, kvl), (gq, gkv) = rows, params
        return [_rms(ql, gq), _rms(kvl, gkv)], []

    (qn, kvn), _ = rw("qkv_norm", f_qkv_norm, [(z, Z_Q, Q_RANK), (z, Z_KV, KV_RANK)], [q_g, kv_g],
                      [(Q_RANK, BF16), (KV_RANK, BF16)], [])
    qp = matmul("w_uq", qn, w_uq_t, 'nt', BF16)
    kvp = matmul("w_ukv", kvn, w_ukv_t, 'nt', BF16)

    def f_rope(is_ctx, rows, params):
        qv, kk, vv, kr, c, s1, s2 = rows
        krr = _rope(kr, c, s1, s2)
        qo = jnp.concatenate([_rope(qh, c, s1, s2) for qh in _heads(qv)], axis=1) * Q_SCALE
        ko = jnp.concatenate([kh + krr for kh in _heads(kk)], axis=1)
        return [qo, ko, vv], []

    hp = N_HEADS * HEAD_PAD
    (qr, kr_, vr), _ = rw("rope", f_rope,
                          [(qp, 0, hp), (kvp, 0, hp), (kvp, hp, hp), (z, Z_KR, HEAD_PAD), (c_tab, 0, HEAD_PAD),
                           (s1_tab, 0, HEAD_PAD), (s2_tab, 0, HEAD_PAD)], [], [(hp, BF16)] * 3, [])
    attn, lse = attn_fwd(qr, kr_, vr, t_lat, n, tq_fwd)

    xc = conv_fwd("lru_conv", z, Z_XB, LRU_W, wt['lru_conv_w'], row(wt['lru_conv_b']), 2, n, t_lat, F32)
    a_f, u_f, a_b, u_b = gates_fwd(xc, lru_w_a, lru_w_x, b_a, b_x, sp, n, t_lat, tm)
    h_f, hp_f = scan_fwd("scan_f", a_f, u_f, 'f', n, t_lat)
    h_b, hp_b = scan_fwd("scan_b", a_b, u_b, 'b', n, t_lat)

    def f_lru_out(is_ctx, rows, params):
        hf, hb, yb = rows
        return [(hf + hb) * _gelu(yb)], []

    (ybin,), _ = rw_lat("lru_out", f_lru_out, [(h_f, 0, LRU_W), (h_b, 0, LRU_W), (z, Z_YB, LRU_W)], [],
                        [(LRU_W, BF16)], [])
    w_o_attn_t, w_out, w_up_t, w_down = need(('w_o_attn', 'w_out', 'w_up', 'w_down'), attn)
    y_a = matmul("w_o_attn", attn, w_o_attn_t, 'nt', BF16)
    y_b = matmul("w_o_lru", ybin, w_o_lru, 'nn', BF16)

    def _merge(ya, yb, gl, bg):
        gates = _sigmoid(gl + bg)
        return gates[:, :D] * ya + gates[:, D:] * yb

    def f_merge(is_ctx, rows, params):
        (ya, yb, gl), (bg,) = rows, params
        return [_merge(ya, yb, gl, bg)], []

    (mrg,), _ = rw_lat("merge", f_merge, [(y_a, 0, D), (y_b, 0, D), (z, Z_GL, 2 * D)], [b_gate], [(D, BF16)], [])
    o = matmul("w_out", mrg, w_out, 'nn', BF16)

    def _res_norm2(xv, ov, g1v, g, sc, sh):
        x1 = xv + g1v * ov
        return x1, _norm_mod(x1, g, sc, sh)

    def f_norm2(is_ctx, rows, params):
        (xv, ov), (g1v, g, sc, sh) = rows, params
        x1, h2v = _res_norm2(xv, ov, g1v, g, sc, sh)
        return [x1, h2v], []

    (x1, h2), _ = rw_lat("norm2", f_norm2, [(x, 0, D), (o, 0, D)], [g1, norm2_g, sc2, sh2], [(D, F32), (D, BF16)], [])
    u = matmul("w_up", h2, w_up_t, 'nt', BF16)
    f = ffn_mix_fwd(u, wt['ffn_conv_w'], row(wt['ffn_conv_b']), t_lat)
    dn = matmul("w_down", f, w_down, 'nn', BF16)

    def _tile_loss(x1v, dv, g2v, fg, tgt):
        y = _rms(x1v + g2v * dv, fg)
        e = y - tgt
        return 0.5 * jnp.sum(jnp.mean(e * e, axis=-1, keepdims=True), axis=0, keepdims=True)

    def f_final(is_ctx, rows, params):
        (x1v, dv, tgt), (g2v, fg) = rows, params
        lv, vjp = jax.vjp(lambda a, b, c, d: _tile_loss(a, b, c, d, tgt), x1v, dv, g2v, fg)
        dx2, dd, dg2, dfg = vjp(jnp.ones((1, 1), F32))
        return [dx2, dd], [dg2, dfg, jnp.broadcast_to(lv, (1, 128))]

    (dx2, dd), (dg2, dfinal_g, loss_v) = rw_lat("final", f_final, [(x1, 0, D), (dn, 0, D), (target, 0, D)],
                                                [g2, final_g], [(D, F32), (D, BF16)], [(1, D), (1, D), (1, 128)])
    loss = loss_v[0, 0]

    grads = {'final_g': dfinal_g}

    def put(name, g):
        grads[name] = g
        if on_grad is not None:
            pending.append(on_grad(name, g))
    df = matmul("d_f", dd, w_down, 'nt', BF16)
    put('w_down', matmul("g_w_down", f, dd, 'tn', BF16))

    du, grads['ffn_conv_w'], grads['ffn_conv_b'] = ffn_mix_bwd(u, df, wt['ffn_conv_w'], row(wt['ffn_conv_b']),
                                                               t_lat)
    dh2 = matmul("d_h2", du, w_up_t, 'nn', BF16, after=sent())
    put('w_up', matmul("g_w_up", du, h2, 'tn', BF16))

    def b_norm2(is_ctx, rows, params):
        (xv, ov, dh2v, dx2v), (g1v, g, sc, sh) = rows, params
        _, vjp = jax.vjp(_res_norm2, xv, ov, g1v, g, sc, sh)
        dx, do, dg1v, dg, dsc, dsh = vjp((dx2v, dh2v))
        return [dx, do], [dg1v, dg, dsc, dsh]

    (dx_res, do), (dg1, dnorm2_g, dsc2, dsh2) = rw_lat(
        "norm2_bwd", b_norm2, [(x, 0, D), (o, 0, D), (dh2, 0, D), (dx2, 0, D)], [g1, norm2_g, sc2, sh2],
        [(D, F32), (D, BF16)], [(1, D)] * 4)
    grads['norm2_g'] = dnorm2_g
    dmrg = matmul("d_merge", do, w_out, 'nt', BF16, after=sent())
    put('w_out', matmul("g_w_out", mrg, do, 'tn', BF16))

    def b_merge(is_ctx, rows, params):
        (ya, yb, gl, dm), (bg,) = rows, params
        _, vjp = jax.vjp(_merge, ya, yb, gl, bg)
        dya, dyb, dgl, dbg = vjp(dm)
        return [dya, dyb, dgl], [dbg]

    (dy_a, dy_b, dgl), (grads['b_gate'],) = rw_lat(
        "merge_bwd", b_merge, [(y_a, 0, D), (y_b, 0, D), (z, Z_GL, 2 * D), (dmrg, 0, D)], [b_gate],
        [(D, BF16), (D, BF16), (2 * D, BF16)], [(1, 2 * D)])
    dattn = matmul("d_attn", dy_a, w_o_attn_t, 'nn', BF16, after=sent())
    put('w_o_attn', matmul("g_w_o_attn", dy_a, attn, 'tn', BF16))
    dybin = matmul("d_lru_out", dy_b, w_o_lru, 'nt', BF16, after=sent())
    put('w_o_lru', matmul("g_w_o_lru", ybin, dy_b, 'tn', BF16))

    def b_lru_out(is_ctx, rows, params):
        hf, hb, yb, dyv = rows
        _, vjp = jax.vjp(lambda s, y: s * _gelu(y), hf + hb, yb)
        dh, dyb = vjp(dyv)
        return [dh, dyb], []

    (dh_lru, dyb), _ = rw_lat("lru_out_bwd", b_lru_out,
                              [(h_f, 0, LRU_W), (h_b, 0, LRU_W), (z, Z_YB, LRU_W), (dybin, 0, LRU_W)], [],
                              [(LRU_W, F32), (LRU_W, BF16)], [])
    du_f, da_f = scan_adj("scan_f_adj", a_f, dh_lru, hp_f, 'f', n, t_lat)
    du_b, da_b = scan_adj("scan_b_adj", a_b, dh_lru, hp_b, 'b', n, t_lat)
    dxc, (dw_a, dw_x, db_a, db_x, dsp) = gates_bwd(xc, da_f, du_f, da_b, du_b, lru_w_a, lru_w_x, b_a, b_x, sp,
                                                   n, t_lat, tm)
    put('lru_w_a', dw_a.reshape(2 * LRU_BLOCKS * LRU_BW, LRU_BW).astype(BF16))
    put('lru_w_x', dw_x.reshape(2 * LRU_BLOCKS * LRU_BW, LRU_BW).astype(BF16))
    grads['lru_b_a'], grads['lru_b_x'] = db_a, db_x
    grads['lru_lambda'] = -dsp * _sigmoid(-lam)
    dxb, grads['lru_conv_w'], grads['lru_conv_b'] = conv_bwd("lru_conv_bwd", dxc, z, Z_XB, LRU_W, wt['lru_conv_w'],
                                                             2, n, t_lat)

    dq, dk, dv = attn_bwd(qr, kr_, vr, attn, dattn, lse, t_lat, n, tq_bwd)

    def b_rope(is_ctx, rows, params):
        dqv, dkv, dvv, c, s1, s2 = rows
        live = jnp.where(is_ctx, 0.0, 1.0)
        dqo = jnp.concatenate([_rope_t(dqh, c, s1, s2) for dqh in _heads(dqv)], axis=1) * live
        dkh = _heads(dkv)
        dkr = dkh[0]
        for t in dkh[1:]:
            dkr = dkr + t
        lanes = lax.broadcasted_iota(jnp.int32, dkr.shape, 1)
        dkr = jnp.where((lanes >= QK_NOPE) & (lanes < QK_DIM), _rope_t(dkr, c, s1, s2), 0.0)
        return [dqo, jnp.concatenate([dkv, dvv], axis=1), dkr], []

    (dqp, dkvp, dkr), _ = rw("rope_bwd", b_rope,
                             [(dq, 0, hp), (dk, 0, hp), (dv, 0, hp), (c_tab, 0, HEAD_PAD), (s1_tab, 0, HEAD_PAD),
                              (s2_tab, 0, HEAD_PAD)], [], [(hp, BF16), (2 * hp, BF16), (HEAD_PAD, BF16)], [])
    dqn = matmul("d_qn", dqp, w_uq_t, 'nn', BF16, after=sent())
    put('w_uq', matmul("g_w_uq", dqp, qn, 'tn', BF16))
    dkvn = matmul("d_kvn", dkvp, w_ukv_t, 'nn', BF16, after=sent())
    put('w_ukv', matmul("g_w_ukv", dkvp, kvn, 'tn', BF16))

    def b_qkv_norm(is_ctx, rows, params):
        (ql, kvl, dqv, dkvv), (gq, gkv) = rows, params
        _, vjp_q = jax.vjp(_rms, ql, gq)
        _, vjp_kv = jax.vjp(_rms, kvl, gkv)
        dql, dgq = vjp_q(dqv)
        dkvl, dgkv = vjp_kv(dkvv)
        return [dql, dkvl], [dgq, dgkv]

    (dq_lat, dkv_lat), (grads['q_norm_g'], grads['kv_norm_g']) = rw(
        "qkv_norm_bwd", b_qkv_norm, [(z, Z_Q, Q_RANK), (z, Z_KV, KV_RANK), (dqn, 0, Q_RANK), (dkvn, 0, KV_RANK)],
        [q_g, kv_g], [(Q_RANK, BF16), (KV_RANK, BF16)], [(1, Q_RANK), (1, KV_RANK)])
    pad_ctx = lambda t: jnp.pad(t, ((0, n_ctx), (0, 0)))
    dz = jnp.concatenate([dq_lat, dkv_lat, dkr, dxb, pad_ctx(dyb), pad_ctx(dgl)], axis=1)
    put('w_in', matmul("g_w_in", dz, h, 'tn', BF16))
    dh = matmul("d_h", dz, w_in_t, 'nn', BF16, after=sent())

    def b_norm1(is_ctx, rows, params):
        (xl, xc_, dhv, dxr), (g, sc, sh) = rows, params
        scv, shv = _sel(is_ctx, sc), _sel(is_ctx, sh)
        _, vjp = jax.vjp(_norm_mod, jnp.where(is_ctx, xc_, xl), g, scv, shv)
        dx, dg, dsc, dsh = vjp(dhv)
        return [dx + dxr], [dg, _seg_acc(is_ctx, dsc), _seg_acc(is_ctx, dsh)]

    (grad_x,), (grads['norm1_g'], dsc1, dsh1) = rw("norm1_bwd", b_norm1, stream + [(dh, 0, D), (dx_res, 0, D)],
                                                   [norm1_g, sc1, sh1], [(D, F32, 'lat')],
                                                   [(1, D), (2, D), (2, D)])
    zero = jnp.zeros((D,), F32)
    dmod_l = jnp.concatenate([dsh1[0], dsc1[0], dg1[0], dsh2[0], dsc2[0], dg2[0]])
    dmod_c = jnp.concatenate([dsh1[1], dsc1[1], zero, zero, zero, zero])
    return loss, grad_x, grads, dmod_l, dmod_c


def kernel(x, c, ctx, c_ctx, w_mod, b_mod, norm1_g, w_in, b_gate, q_norm_g, kv_norm_g, w_uq, w_ukv, w_o_attn, lru_conv_w, lru_conv_b, lru_w_a, lru_b_a, lru_w_x, lru_b_x, lru_lambda, w_o_lru, w_out, norm2_g, w_up, ffn_conv_w, ffn_conv_b, w_down, final_g, loss_target, m_c_ctx, m_w_mod, m_b_mod, m_norm1_g, m_w_in, m_b_gate, m_q_norm_g, m_kv_norm_g, m_w_uq, m_w_ukv, m_w_o_attn, m_lru_conv_w, m_lru_conv_b, m_lru_w_a, m_lru_b_a, m_lru_w_x, m_lru_b_x, m_lru_lambda, m_w_o_lru, m_w_out, m_norm2_g, m_w_up, m_ffn_conv_w, m_ffn_conv_b, m_w_down, m_final_g, v_c_ctx, v_w_mod, v_b_mod, v_norm1_g, v_w_in, v_b_gate, v_q_norm_g, v_kv_norm_g, v_w_uq, v_w_ukv, v_w_o_attn, v_lru_conv_w, v_lru_conv_b, v_lru_w_a, v_lru_b_a, v_lru_w_x, v_lru_b_x, v_lru_lambda, v_w_o_lru, v_w_out, v_norm2_g, v_w_up, v_ffn_conv_w, v_ffn_conv_b, v_w_down, v_final_g):
    given = dict(locals())
    strip = lambda name, a: a if name in ('c_ctx', 'final_g') else a[0]
    wsh = {n: strip(n, given[n]) for n in WEIGHTS}
    msh = {n: strip(n, given['m_' + n]) for n in WEIGHTS}
    vsh = {n: strip(n, given['v_' + n]) for n in WEIGHTS}
    me = _my_index()

    small = _flat([c[0]] + [wsh[n] for n in SMALL_F32], F32, 8)
    small_all = all_gather("gather_small", small).reshape(N_DEV, -1)
    c_all = small_all[:, :D]
    full, at = {}, D
    for n in SMALL_F32:
        cnt = math.prod(wsh[n].shape)
        full[n] = _gathered_to_full(n, small_all[:, at:at + cnt].reshape((N_DEV,) + wsh[n].shape))
        at += cnt

    cond = jnp.concatenate([c_all, c_ctx[None], jnp.zeros((7, D), F32)], axis=0)
    sil = cond * jax.nn.sigmoid(cond)
    mod_cols = matmul("mod_proj", sil, wsh['w_mod'], 'nn', F32)
    mod_all = all_gather("gather_mod", mod_cols)
    mod_all = jnp.transpose(mod_all, (1, 0, 2)).reshape(16, 6 * D) + b_mod[0][None]
    mod_l = lax.dynamic_index_in_dim(mod_all, me, axis=0, keepdims=False)
    mod_c = mod_all[N_DEV]

    rb_shards = {n: _shard_to_rb(n, wsh[n]).astype(BF16) for n in BIG_BF16}
    (w_in_blocks,) = all_gather_multi("gather_w_in", [rb_shards['w_in']])
    later = [n for n in BIG_BF16 if n != 'w_in']
    weights_started, weights_sent = exchange_start("weights_send", 'gather', [rb_shards[n] for n in later],
                                                   after=[w_in_blocks, mod_all])
    for n in REPLICATED:
        if n not in ('c_ctx', 'b_mod'):
            full[n] = wsh[n]

    def arrive(names, after):
        if names == ('w_in',):
            return {'w_in': _rb_from_gathered('w_in', w_in_blocks), '_token': weights_sent}
        picked = [later.index(n) for n in names]
        lands = exchange_wait("weights_wait_" + names[0], 'gather',
                              tuple([part[i] for i in picked] for part in weights_started), after)
        return {n: _rb_from_gathered(n, lax.dynamic_update_slice_in_dim(land, rb_shards[n][None], me, axis=0))
                for n, land in zip(names, lands)}

    in_flight = {}

    def on_grad(n, g):
        chunks = _chunks_from_rb_grad(n, g)
        own = lax.dynamic_index_in_dim(chunks, me, axis=0, keepdims=True)
        started, token = exchange_start("grad_send_" + n, 'scatter', [chunks])
        in_flight[n] = (own, started)
        return token

    loss, grad_x, grads, dmod_l, dmod_c = local_step(x[0], ctx[0], loss_target[0], mod_l, mod_c, full, on_grad,
                                                     arrive)
    dmod = jnp.stack([dmod_l, dmod_c]).reshape(2 * 6 * D // FLAT_C, FLAT_C)
    dm = all_gather("gather_dmod", dmod).reshape(N_DEV, 2, 6 * D)
    dmod_c_tot = dm[0, 1]
    for p in range(1, N_DEV):
        dmod_c_tot = dmod_c_tot + dm[p, 1]
    dm16 = jnp.concatenate([dm[:, 0], dmod_c_tot[None], jnp.zeros((7, 6 * D), F32)], axis=0)
    ncol = 6 * D // N_DEV
    dm16_cols = lax.dynamic_slice_in_dim(dm16.reshape(16, N_DEV, ncol), me, 1, axis=1)[:, 0]
    grad_w_mod = matmul("g_w_mod", sil, dm16_cols, 'tn', F32)
    dsil = matmul("d_cond", dm16_cols, wsh['w_mod'], 'nt', F32)
    sg = jax.nn.sigmoid(c_ctx)
    grads['c_ctx'] = dsil[N_DEV] * (sg * (1.0 + c_ctx * (1.0 - sg)))
    grads['b_mod'] = dmod_l + dmod_c

    g_final = {'w_mod': grad_w_mod}
    reduced, stepped = {}, {}
    for n in BIG_BF16 + ['lru_w_a', 'lru_w_x']:
        own, started = in_flight[n]
        (land,) = exchange_wait("grad_wait_" + n, 'scatter', started, dm)
        if n in ROW_SHARDED:
            g_final[n], *stepped[n] = reduce_slots("step_" + n, land, own, (wsh[n], msh[n], vsh[n]))
        elif n in COL_SHARDED and wsh[n].shape[1] % 128:
            g_t, *outs = reduce_slots("step_" + n, land, own, (wsh[n].T, msh[n].T, vsh[n].T))
            g_final[n], stepped[n] = g_t.T, [o.T for o in outs]
        else:
            reduced[n] = reduce_slots("sum_" + n, land, own)
            if n in BIG_BF16:
                g_final[n] = _rb_to_shard(n, reduced[n])

    small_names = SMALL_F32 + [n for n in REPLICATED if n not in ('lru_w_a', 'lru_w_x')]
    partials = _flat([grads[n] for n in small_names] + [loss], F32, 8)
    parts_all, a_all, x_all = all_gather_multi("gather_small_grads", [partials, reduced['lru_w_a'], reduced['lru_w_x']])
    small_sum = sum_slots("sum_small", parts_all).reshape(-1)
    g_final['lru_w_a'], g_final['lru_w_x'] = a_all.reshape(wsh['lru_w_a'].shape), x_all.reshape(wsh['lru_w_x'].shape)
    at = 0
    for n in small_names:
        cnt = math.prod(full[n].shape) if n in SMALL_F32 else math.prod(wsh[n].shape)
        g = small_sum[at:at + cnt]
        if n in SMALL_F32:
            k = full[n].shape[0]
            g = lax.dynamic_index_in_dim(g.reshape(k, N_DEV, -1), me, axis=1, keepdims=False)
        g_final[n] = g.reshape(wsh[n].shape)
        at += cnt
    loss = small_sum[at]

    for n in ['w_mod'] + BIG_BF16:
        if n not in stepped:
            stepped[n] = adamw("adamw_" + n, wsh[n], g_final[n], msh[n], vsh[n])
    rest = [n for n in WEIGHTS if n not in stepped]
    as2d = lambda a: a.reshape(-1, a.shape[-1])
    rest_out = adamw_many("adamw_small", *[[as2d(d[n]) for n in rest] for d in (wsh, g_final, msh, vsh)])
    stepped.update(zip(rest, rest_out))
    shaped = lambda n, a: a.reshape(given[n].shape)
    return (loss, grad_x[None],
            *[shaped(n, g_final[n]) for n in WEIGHTS],
            *[shaped(n, stepped[n][k]) for k in range(3) for n in WEIGHTS])
```

```python
import functools
import math

import jax
import jax.numpy as jnp
from jax import lax
from jax.experimental import pallas as pl
from jax.experimental.pallas import tpu as pltpu

F32 = jnp.float32
BF16 = jnp.bfloat16
MESH = pl.DeviceIdType.MESH

N_DEV = 8
D = 1024
N_HEADS = 8
HEAD_PAD = 128
QK_NOPE, QK_ROPE, V_HEAD = 64, 32, 64
QK_DIM = QK_NOPE + QK_ROPE
Q_RANK, KV_RANK = 384, 256
LRU_W, LRU_BLOCKS, LRU_BW = 1280, 10, 128
FFN = 2816
GRID_W = 64
ROPE_BASE = 10000.0
LRU_C = 8.0
EPS = 1e-6
Z_Q, Z_KV, Z_KR, Z_XB, Z_YB, Z_GL, Z_END = 0, 384, 640, 768, 2048, 3328, 5376
ADAM_LR, ADAM_B1, ADAM_B2, ADAM_EPS, ADAM_WD, ADAM_STEP = 0.001, 0.9, 0.999, 1e-08, 0.01, 10

VMEM_LIMIT = 52 * 1024 * 1024
FLAT_C = 512

WEIGHTS = ['c_ctx', 'w_mod', 'b_mod', 'norm1_g', 'w_in', 'b_gate', 'q_norm_g', 'kv_norm_g', 'w_uq', 'w_ukv',
           'w_o_attn', 'lru_conv_w', 'lru_conv_b', 'lru_w_a', 'lru_b_a', 'lru_w_x', 'lru_b_x', 'lru_lambda',
           'w_o_lru', 'w_out', 'norm2_g', 'w_up', 'ffn_conv_w', 'ffn_conv_b', 'w_down', 'final_g']
COL_SHARDED = ['w_in', 'w_uq', 'w_ukv', 'w_o_attn', 'lru_conv_w', 'lru_b_a', 'lru_b_x', 'lru_lambda', 'w_up',
               'ffn_conv_w']
ROW_SHARDED = ['w_o_lru', 'w_out', 'w_down']
BIG_BF16 = ['w_in', 'w_uq', 'w_ukv', 'w_o_attn', 'w_o_lru', 'w_out', 'w_up', 'w_down']
SMALL_F32 = ['lru_conv_w', 'lru_b_a', 'lru_b_x', 'lru_lambda', 'ffn_conv_w']
REPLICATED = ['c_ctx', 'b_mod', 'norm1_g', 'b_gate', 'q_norm_g', 'kv_norm_g', 'lru_conv_b', 'lru_w_a', 'lru_w_x',
              'norm2_g', 'ffn_conv_b', 'final_g']


def _cparams(sem=None):
    return pltpu.CompilerParams(dimension_semantics=sem, vmem_limit_bytes=VMEM_LIMIT)


def _pick(n, cands):
    for c in cands:
        if c <= n and n % c == 0:
            return c
    return n


def _best_div(n, mult, cap):
    best = mult
    for d in range(mult, min(n, cap) + 1, mult):
        if n % d == 0:
            best = d
    return best


MXU_DIM = 256
ROW_TILES = (1088, 1024, 544, 512, 256, 128, 64, 32, 16, 8)
LANE_TILES = (2816, 1792, 1536, 1280, 1024, 768, 512, 256, 1408, 896, 640, 384, 128)
DEPTH_ROW_TILES = (2176, 2048, 1024, 512, 256, 1088, 128, 64, 32, 16, 8)
MATMUL_VMEM_BUDGET = 40 * 1024 * 1024
MXU_FILL_OK = 0.9


def _my_pos():
    return lax.axis_index("x"), lax.axis_index("y"), lax.axis_index("c")


def _my_index():
    x, y, c = _my_pos()
    return 4 * x + 2 * y + c


def all_gather_multi(name, shards):
    n_arr = len(shards)
    arrays = range(n_arr)

    def body(*refs):
        x_refs, out_refs = refs[:n_arr], refs[n_arr:2 * n_arr]
        send_sems, recv_sems, local_sems = refs[2 * n_arr:]
        x, y, c = _my_pos()
        me, sibling = (x, y, c), (x, y, 1 - c)
        chips = [(1 - x, y), (x, 1 - y), (1 - x, 1 - y)]

        def slot(a, px, py, pc):
            return out_refs[a].at[4 * px + 2 * py + pc]

        def copy(a, k, block, to, src=None):
            return pltpu.make_async_remote_copy(
                src_ref=slot(a, *block) if src is None else src, dst_ref=slot(a, *block),
                send_sem=send_sems.at[7 * a + k], recv_sem=recv_sems.at[7 * a + k], device_id=to,
                device_id_type=MESH)

        mine = [pltpu.make_async_copy(x_refs[a], slot(a, *me), local_sems.at[a]) for a in arrays]
        first = [copy(a, 1 + j, me, (*chip, c), src=x_refs[a]) for j, chip in enumerate(chips) for a in arrays]
        first += [copy(a, 0, me, sibling, src=x_refs[a]) for a in arrays]
        for cp in first + mine:
            cp.start()
        passed = []
        for j, chip in enumerate(chips):
            for a in arrays:
                copy(a, 1 + j, (*chip, c), me).wait_recv()
                passed.append(copy(a, 4 + j, (*chip, c), sibling))
                passed[-1].start()
        for a in arrays:
            copy(a, 0, sibling, me).wait_recv()
            for j, chip in enumerate(chips):
                copy(a, 4 + j, (*chip, 1 - c), me).wait_recv()
        for cp in first + passed:
            cp.wait_send()
        for cp in mine:
            cp.wait()

    hbm = pl.BlockSpec(memory_space=pl.ANY)
    return pl.pallas_call(
        body, name=name,
        out_shape=[jax.ShapeDtypeStruct((N_DEV,) + s.shape, s.dtype) for s in shards],
        in_specs=[hbm] * n_arr, out_specs=[hbm] * n_arr,
        scratch_shapes=[pltpu.SemaphoreType.DMA((7 * n_arr,)), pltpu.SemaphoreType.DMA((7 * n_arr,)),
                        pltpu.SemaphoreType.DMA((n_arr,))],
    )(*shards)


def all_gather(name, shard):
    return all_gather_multi(name, [shard])[0]


def _peers():
    x, y, c = _my_pos()
    out = []
    for rel in (6, 4, 2, 7, 5, 3, 1):
        px, py, pc = x ^ ((rel >> 2) & 1), y ^ ((rel >> 1) & 1), c ^ (rel & 1)
        out.append((rel - 1, (px, py, pc), 4 * px + 2 * py + pc))
    return out


def _exchange_copies(mode, src_refs, land_refs, send_sems, recv_sems):
    x, y, c = _my_pos()
    me = 4 * x + 2 * y + c
    sends, arrivals = [], []
    for k, peer_pos, peer in _peers():
        for a, (src, land) in enumerate(zip(src_refs, land_refs)):
            piece = src.at[peer] if mode == 'scatter' else src
            sems = dict(send_sem=send_sems[a].at[k], recv_sem=recv_sems[a].at[k], device_id_type=MESH)
            sends.append(pltpu.make_async_remote_copy(src_ref=piece, dst_ref=land.at[me], device_id=peer_pos, **sems))
            arrivals.append(pltpu.make_async_remote_copy(src_ref=piece, dst_ref=land.at[peer], device_id=(x, y, c), **sems))
    return sends, arrivals


_HBM = pl.BlockSpec(memory_space=pltpu.HBM)
_SEM = pl.BlockSpec(memory_space=pltpu.SEMAPHORE)


def exchange_start(name, mode, arrays, after=()):
    n_arr, n_after = len(arrays), len(after)
    land_shapes = [a.shape if mode == 'scatter' else (N_DEV,) + a.shape for a in arrays]

    def body(*refs):
        src_refs, land_refs = refs[:n_arr], refs[n_arr:2 * n_arr]
        refs = refs[n_after:]
        send_sems, recv_sems = refs[2 * n_arr:3 * n_arr], refs[3 * n_arr:4 * n_arr]
        sends, _ = _exchange_copies(mode, src_refs, land_refs, send_sems, recv_sems)
        for cp in sends:
            cp.start()
        token = refs[-1]
        token[...] = jnp.zeros_like(token)

    sem = pltpu.SemaphoreType.DMA((N_DEV - 1,))
    res = pl.pallas_call(
        body, name=name,
        out_shape=[sem] * (2 * n_arr) + [pltpu.HBM(a.shape, a.dtype) for a in arrays]
        + [pltpu.HBM(s, a.dtype) for s, a in zip(land_shapes, arrays)] + [jax.ShapeDtypeStruct((8, 128), F32)],
        in_specs=[_HBM] * (2 * n_arr) + [pl.BlockSpec(memory_space=pl.ANY)] * n_after,
        out_specs=[_SEM] * (2 * n_arr) + [_HBM] * (2 * n_arr) + [pl.BlockSpec(memory_space=pltpu.VMEM)],
        input_output_aliases={i: 2 * n_arr + i for i in range(2 * n_arr)},
        compiler_params=pltpu.CompilerParams(has_side_effects=pltpu.SideEffectType.DATAFLOW_SIDE_EFFECTING),
    )(*[pltpu.with_memory_space_constraint(a, pltpu.HBM) for a in arrays],
      *[pltpu.with_memory_space_constraint(lax.empty(s, a.dtype), pltpu.HBM) for s, a in zip(land_shapes, arrays)],
      *after)
    return (res[:n_arr], res[n_arr:2 * n_arr], res[2 * n_arr:3 * n_arr], res[3 * n_arr:4 * n_arr]), res[-1]


def exchange_wait(name, mode, started, after):
    send_sems, recv_sems, thru, land = started
    n_arr = len(thru)

    def body(*refs):
        src_refs, land_refs = refs[:n_arr], refs[n_arr:2 * n_arr]
        s_sems, r_sems = refs[2 * n_arr:3 * n_arr], refs[3 * n_arr:4 * n_arr]
        sends, arrivals = _exchange_copies(mode, src_refs, land_refs, s_sems, r_sems)
        for cp in sends:
            cp.wait_send()
        for cp in arrivals:
            cp.wait_recv()

    res = pl.pallas_call(
        body, name=name,
        out_shape=[pltpu.HBM(a.shape, a.dtype) for a in thru] + [pltpu.HBM(a.shape, a.dtype) for a in land],
        in_specs=[_HBM] * (2 * n_arr) + [_SEM] * (2 * n_arr) + [pl.BlockSpec(memory_space=pl.ANY)],
        out_specs=[_HBM] * (2 * n_arr),
        input_output_aliases={i: i for i in range(2 * n_arr)},
        compiler_params=pltpu.CompilerParams(has_side_effects=pltpu.SideEffectType.DATAFLOW_SIDE_EFFECTING),
    )(*thru, *land, *send_sems, *recv_sems, after)
    return res[n_arr:]


def _sum_with_own(slot_ref, own_ref):
    x, y, c = _my_pos()
    me = 4 * x + 2 * y + c
    acc = None
    for p in range(N_DEV):
        v = jnp.where(me == p, own_ref[0], slot_ref[p]).astype(F32)
        acc = v if acc is None else acc + v
    return acc


def reduce_slots(name, slots, own, step=None):
    _, r, ccols = slots.shape
    tc = _pick(ccols, (256, 128))
    c1 = 1.0 - ADAM_B1 ** ADAM_STEP
    c2 = 1.0 - ADAM_B2 ** ADAM_STEP

    def body(s_ref, own_ref, *refs):
        g = _sum_with_own(s_ref, own_ref)
        if step is None:
            refs[0][...] = g
            return
        w_ref, m_ref, v_ref, g_ref, d_ref, nm_ref, nv_ref = refs
        nm = ADAM_B1 * m_ref[...] + (1.0 - ADAM_B1) * g
        nv = ADAM_B2 * v_ref[...] + (1.0 - ADAM_B2) * (g * g)
        g_ref[...] = g
        d_ref[...] = -ADAM_LR * ((nm / c1) / (jnp.sqrt(nv / c2) + ADAM_EPS) + ADAM_WD * w_ref[...])
        nm_ref[...] = nm
        nv_ref[...] = nv

    col = pl.BlockSpec((r, tc), lambda j: (0, j))
    n_out = 1 if step is None else 4
    res = pl.pallas_call(
        body, name=name, grid=(ccols // tc,),
        out_shape=[jax.ShapeDtypeStruct((r, ccols), F32)] * n_out,
        in_specs=[pl.BlockSpec((N_DEV, r, tc), lambda j: (0, 0, j)), pl.BlockSpec((1, r, tc), lambda j: (0, 0, j))]
        + [col] * (0 if step is None else 3),
        out_specs=[col] * n_out,
        compiler_params=_cparams(("parallel",)),
    )(slots, own, *(step or ()))
    return res[0] if step is None else res


def sum_slots(name, slots):
    _, r, ccols = slots.shape
    tc = _pick(ccols, (256, 128))

    def body(s_ref, o_ref):
        acc = s_ref[0].astype(F32)
        for p in range(1, N_DEV):
            acc = acc + s_ref[p].astype(F32)
        o_ref[...] = acc

    return pl.pallas_call(
        body, name=name, grid=(ccols // tc,),
        out_shape=jax.ShapeDtypeStruct((r, ccols), F32),
        in_specs=[pl.BlockSpec((N_DEV, r, tc), lambda j: (0, 0, j))],
        out_specs=pl.BlockSpec((r, tc), lambda j: (0, j)),
        compiler_params=_cparams(("parallel",)),
    )(slots)


def _mxu_fill(t):
    return t / (-(-t // MXU_DIM) * MXU_DIM)


def _matmul_tiles(mode, m_extent, n, k_extent, k_total, itemsizes):
    a_bytes, b_bytes, o_bytes = itemsizes
    m_cands = [c for c in (LANE_TILES if mode == 'tn' else ROW_TILES) if m_extent % c == 0] or [m_extent]
    k_cands = [c for c in (DEPTH_ROW_TILES if mode == 'tn' else LANE_TILES) if k_extent % c == 0] or [k_extent]
    n_cands = [c for c in LANE_TILES if n % c == 0] or [n]
    best = None
    for tm in m_cands:
        for tk in k_cands:
            for tn in n_cands:
                f32_tiles = 2 if k_total // tk > 1 else 1
                vmem = 2 * (tm * tk * a_bytes + tk * tn * b_bytes + tm * tn * o_bytes) + tm * tn * 4 * f32_tiles
                if vmem > MATMUL_VMEM_BUDGET:
                    continue
                key = (_mxu_fill(tn) * _mxu_fill(tk) >= MXU_FILL_OK, tm * tn * tk)
                if best is None or key > best[0]:
                    best = (key, (tm, tn, tk))
    assert best is not None, (mode, m_extent, n, k_extent)
    return best[1]


def matmul(name, a, b, mode, out_dtype, after=()):
    after = [t for t in after if t is not None]
    pieces, a_rows, a_cols = (1,) + a.shape if a.ndim == 2 else a.shape
    if mode == 'nn':
        (m, k), (k2, n) = (a_rows, pieces * a_cols), b.shape
    elif mode == 'nt':
        (m, k), (n, k2) = (a_rows, pieces * a_cols), b.shape
    else:
        (k, m), (k2, n) = (a_rows, pieces * a_cols), b.shape
    assert k == k2, (name, a.shape, b.shape, mode)
    tm, tn, tk = _matmul_tiles(mode, a_cols if mode == 'tn' else m, n, k if mode == 'tn' else a_cols, k,
                               (a.dtype.itemsize, b.dtype.itemsize, jnp.dtype(out_dtype).itemsize))
    nk = k // tk
    per_piece = a_cols // (tm if mode == 'tn' else tk)
    if a.ndim == 2:
        a_block = lambda rows, cols, at: pl.BlockSpec((rows, cols), at)
    else:
        a_block = lambda rows, cols, at: pl.BlockSpec(
            (None, rows, cols), lambda i, j, kk: (at(i, j, kk)[1] // per_piece, at(i, j, kk)[0],
                                                  at(i, j, kk)[1] % per_piece))
    if mode == 'nn':
        a_spec = a_block(tm, tk, lambda i, j, kk: (i, kk))
        b_spec = pl.BlockSpec((tk, tn), lambda i, j, kk: (kk, j))
        dn = (((1,), (0,)), ((), ()))
    elif mode == 'nt':
        a_spec = a_block(tm, tk, lambda i, j, kk: (i, kk))
        b_spec = pl.BlockSpec((tn, tk), lambda i, j, kk: (j, kk))
        dn = (((1,), (1,)), ((), ()))
    else:
        a_spec = a_block(tk, tm, lambda i, j, kk: (kk, i))
        b_spec = pl.BlockSpec((tk, tn), lambda i, j, kk: (kk, j))
        dn = (((0,), (0,)), ((), ()))

    def product(a_ref, b_ref):
        return lax.dot_general(a_ref[...].astype(BF16), b_ref[...].astype(BF16), dn, preferred_element_type=F32)

    n_after = len(after)

    def body_one(a_ref, b_ref, *rest):
        o_ref = rest[n_after]
        o_ref[...] = product(a_ref, b_ref).astype(o_ref.dtype)

    def body(a_ref, b_ref, *rest):
        o_ref, acc_ref = rest[n_after:]
        kk = pl.program_id(2)

        @pl.when(kk == 0)
        def _():
            acc_ref[...] = jnp.zeros_like(acc_ref)

        acc_ref[...] += product(a_ref, b_ref)

        @pl.when(kk == nk - 1)
        def _():
            o_ref[...] = acc_ref[...].astype(o_ref.dtype)

    return pl.pallas_call(
        body_one if nk == 1 else body, name=name, grid=(m // tm, n // tn, nk),
        out_shape=jax.ShapeDtypeStruct((m, n), out_dtype),
        in_specs=[a_spec, b_spec] + [pl.BlockSpec(memory_space=pl.ANY)] * n_after,
        out_specs=pl.BlockSpec((tm, tn), lambda i, j, kk: (i, j)),
        scratch_shapes=[] if nk == 1 else [pltpu.VMEM((tm, tn), F32)],
        compiler_params=_cparams(("parallel", "parallel", "arbitrary")),
    )(a, b, *after)


def rowwise(name, fn, rows, params, out_rows, out_accs, n_rows, t_lat, tm):
    nb, nbl = n_rows // tm, t_lat // tm
    in_specs, piece_counts = [], []
    operands = []
    for arr, off, width, *kind in rows:
        g = math.gcd(off, width) if off else width
        assert g % 128 == 0 or (off == 0 and width == arr.shape[1]), (name, off, width)
        cnt = width // g
        last = arr.shape[0] // tm - 1
        clamp = arr.shape[0] < n_rows
        for p in range(cnt):
            cb = off // g + p
            if kind == ['ctx']:
                in_specs.append(pl.BlockSpec(
                    (tm, g), lambda i, cb=cb, last=last: (jnp.clip(i - nbl, 0, last), cb)))
            elif clamp:
                in_specs.append(pl.BlockSpec((tm, g), lambda i, cb=cb, last=last: (jnp.minimum(i, last), cb)))
            else:
                in_specs.append(pl.BlockSpec((tm, g), lambda i, cb=cb: (i, cb)))
            operands.append(arr)
        piece_counts.append(cnt)
    for p in params:
        in_specs.append(pl.BlockSpec(p.shape, lambda i, nd=p.ndim: (0,) * nd))
        operands.append(p)
    n_in = sum(piece_counts)
    n_par = len(params)
    n_or = len(out_rows)
    lat_only = [kind == ['lat'] for _, _, *kind in out_rows]
    out_shape = [jax.ShapeDtypeStruct((t_lat if lat else n_rows, w), dt)
                 for (w, dt, *_), lat in zip(out_rows, lat_only)]
    out_shape += [jax.ShapeDtypeStruct(s, F32) for s in out_accs]
    out_specs = [pl.BlockSpec((tm, w), (lambda i: (jnp.minimum(i, nbl - 1), 0)) if lat else (lambda i: (i, 0)))
                 for (w, *_), lat in zip(out_rows, lat_only)]
    out_specs += [pl.BlockSpec(s, lambda i, nd=len(s): (0,) * nd) for s in out_accs]

    def body(*refs):
        in_refs, par_refs = refs[:n_in], refs[n_in:n_in + n_par]
        orow_refs = refs[n_in + n_par:n_in + n_par + n_or]
        oacc_refs = refs[n_in + n_par + n_or:]
        i = pl.program_id(0)
        tiles, at = [], 0
        for cnt in piece_counts:
            parts = [in_refs[at + p][...].astype(F32) for p in range(cnt)]
            tiles.append(parts[0] if cnt == 1 else jnp.concatenate(parts, axis=1))
            at += cnt
        is_ctx = i * tm >= t_lat
        outs, accs = fn(is_ctx, tiles, [p[...] for p in par_refs])
        for o_ref, o, lat in zip(orow_refs, outs, lat_only):
            if lat:
                @pl.when(jnp.logical_not(is_ctx))
                def _(o_ref=o_ref, o=o):
                    o_ref[...] = o.astype(o_ref.dtype)
            else:
                o_ref[...] = o.astype(o_ref.dtype)
        if oacc_refs:
            @pl.when(i == 0)
            def _():
                for a_ref in oacc_refs:
                    a_ref[...] = jnp.zeros_like(a_ref)
            for a_ref, a in zip(oacc_refs, accs):
                a_ref[...] += a.astype(F32)

    res = pl.pallas_call(
        body, name=name, grid=(nb,),
        out_shape=out_shape, in_specs=in_specs, out_specs=out_specs,
        compiler_params=_cparams(("arbitrary",)),
    )(*operands)
    return res[:n_or], res[n_or:]


def _rms(x, g):
    return x * lax.rsqrt(jnp.mean(x * x, axis=-1, keepdims=True) + EPS) * g


def _norm_mod(x, g, sc, sh):
    return _rms(x, g) * (1.0 + sc) + sh


def _sigmoid(x):
    return 0.5 * jnp.tanh(0.5 * x) + 0.5


def _silu(x):
    return x * _sigmoid(x)


def _gelu(x):
    return 0.5 * x * (1.0 + jnp.tanh(math.sqrt(2.0 / math.pi) * (x + 0.044715 * (x * x * x))))


def _sel(is_ctx, p):
    return jnp.where(is_ctx, p[1:2], p[0:1])


def _seg_acc(is_ctx, v):
    rows = lax.broadcasted_iota(jnp.int32, (2, v.shape[1]), 0)
    return jnp.where(rows == is_ctx.astype(jnp.int32), jnp.broadcast_to(v, (2, v.shape[1])), 0.0)


def _rsum(v):
    return jnp.sum(v, axis=0, keepdims=True)


def _shift_rows(x, o, t_lat, n):
    if o == 0:
        return x
    y = pltpu.roll(x, (-o) % n, 0)
    t = lax.broadcasted_iota(jnp.int32, x.shape, 0)
    if o > 0:
        ok = t < n - o
        if t_lat < n:
            ok = ok & ((t < t_lat - o) | (t >= t_lat))
    else:
        ok = t >= -o
        if t_lat < n:
            ok = ok & ((t < t_lat) | (t >= t_lat - o))
    return jnp.where(ok, y, 0.0)


def conv_fwd(name, xarr, col_off, width, w, b, left, n_rows, t_lat, out_dtype, cb=128):
    taps = w.shape[0]
    assert col_off % cb == 0 and width % cb == 0

    def body(x_ref, w_ref, b_ref, o_ref):
        x = x_ref[...].astype(F32)
        acc = jnp.broadcast_to(b_ref[...], x.shape)
        for k in range(taps):
            acc = acc + _shift_rows(x, k - left, t_lat, n_rows) * w_ref[k:k + 1, :]
        o_ref[...] = acc.astype(o_ref.dtype)

    return pl.pallas_call(
        body, name=name, grid=(width // cb,),
        out_shape=jax.ShapeDtypeStruct((n_rows, width), out_dtype),
        in_specs=[pl.BlockSpec((n_rows, cb), lambda j: (0, col_off // cb + j)),
                  pl.BlockSpec((taps, cb), lambda j: (0, j)),
                  pl.BlockSpec((1, cb), lambda j: (0, j))],
        out_specs=pl.BlockSpec((n_rows, cb), lambda j: (0, j)),
        compiler_params=_cparams(("parallel",)),
    )(xarr, w, b)


def conv_bwd(name, dout, xarr, col_off, width, w, left, n_rows, t_lat, cb=128):
    taps = w.shape[0]

    def body(d_ref, x_ref, w_ref, dx_ref, dw_ref, db_ref):
        d = d_ref[...].astype(F32)
        x = x_ref[...].astype(F32)
        dx = jnp.zeros_like(d)
        dws = []
        for k in range(taps):
            dx = dx + _shift_rows(d, left - k, t_lat, n_rows) * w_ref[k:k + 1, :]
            dws.append(_rsum(d * _shift_rows(x, k - left, t_lat, n_rows)))
        dx_ref[...] = dx.astype(dx_ref.dtype)
        dw_ref[...] = jnp.concatenate(dws, axis=0)
        db_ref[...] = _rsum(d)

    return pl.pallas_call(
        body, name=name, grid=(width // cb,),
        out_shape=[jax.ShapeDtypeStruct((n_rows, width), BF16), jax.ShapeDtypeStruct((taps, width), F32),
                   jax.ShapeDtypeStruct((1, width), F32)],
        in_specs=[pl.BlockSpec((n_rows, cb), lambda j: (0, j)),
                  pl.BlockSpec((n_rows, cb), lambda j: (0, col_off // cb + j)),
                  pl.BlockSpec((taps, cb), lambda j: (0, j))],
        out_specs=[pl.BlockSpec((n_rows, cb), lambda j: (0, j)), pl.BlockSpec((taps, cb), lambda j: (0, j)),
                   pl.BlockSpec((1, cb), lambda j: (0, j))],
        compiler_params=_cparams(("parallel",)),
    )(dout, xarr, w)


def _ffn_conv(a, w_ref, b_ref, t_lat):
    shifted = [_shift_rows(a, k - 1, t_lat, t_lat) for k in range(3)]
    ac = jnp.broadcast_to(b_ref[...], a.shape)
    for k in range(3):
        ac = ac + shifted[k] * w_ref[k:k + 1, :]
    return ac, shifted


def ffn_mix_fwd(u, w, b, t_lat, cb=128):
    nblk = FFN // cb

    def body(a_ref, g_ref, w_ref, b_ref, f_ref):
        ac, _ = _ffn_conv(a_ref[...].astype(F32), w_ref, b_ref, t_lat)
        f_ref[...] = (_silu(ac) * g_ref[...].astype(F32)).astype(f_ref.dtype)

    col = lambda shape, off=0: pl.BlockSpec(shape, lambda j: (0, off + j))
    return pl.pallas_call(
        body, name="ffn_mix", grid=(nblk,),
        out_shape=jax.ShapeDtypeStruct((t_lat, FFN), BF16),
        in_specs=[col((t_lat, cb)), col((t_lat, cb), nblk), col((3, cb)), col((1, cb))],
        out_specs=col((t_lat, cb)),
        compiler_params=_cparams(("parallel",)),
    )(u, u, w, b)


def ffn_mix_bwd(u, df, w, b, t_lat, cb=128):
    nblk = FFN // cb

    def body(a_ref, g_ref, df_ref, w_ref, b_ref, du_ref, dw_ref, db_ref):
        ac, shifted = _ffn_conv(a_ref[...].astype(F32), w_ref, b_ref, t_lat)
        d = df_ref[...].astype(F32)
        s = _sigmoid(ac)
        du_ref[1] = (d * (ac * s)).astype(du_ref.dtype)
        dac = d * g_ref[...].astype(F32) * (s * (1.0 + ac * (1.0 - s)))
        da = jnp.zeros_like(dac)
        for k in range(3):
            da = da + _shift_rows(dac, 1 - k, t_lat, t_lat) * w_ref[k:k + 1, :]
        du_ref[0] = da.astype(du_ref.dtype)
        dw_ref[...] = jnp.concatenate([_rsum(dac * shifted[k]) for k in range(3)], axis=0)
        db_ref[...] = _rsum(dac)

    col = lambda shape, off=0: pl.BlockSpec(shape, lambda j: (0, off + j))
    return pl.pallas_call(
        body, name="ffn_mix_bwd", grid=(nblk,),
        out_shape=[jax.ShapeDtypeStruct((2, t_lat, FFN), BF16),
                   jax.ShapeDtypeStruct((3, FFN), F32), jax.ShapeDtypeStruct((1, FFN), F32)],
        in_specs=[col((t_lat, cb)), col((t_lat, cb), nblk), col((t_lat, cb)), col((3, cb)), col((1, cb))],
        out_specs=[pl.BlockSpec((2, t_lat, cb), lambda j: (0, 0, j)), col((3, cb)), col((1, cb))],
        compiler_params=_cparams(("parallel",)),
    )(u, u, df, w, b)


def _chunk_order(direction, nb, nbl):
    if direction == 'f':
        return lambda s: ((s + nbl) % nb, 0)
    return lambda s: (nb - 1 - s, 0)


def _adjoint_order(direction, nb, nbl):
    if direction == 'f':
        return lambda s: ((nb - 1 - s + nbl) % nb, 0)
    return lambda s: (s, 0)


SUBLANES = 8


def _chunk_scan(a, b, carry, rev):
    tc, width = a.shape
    nt = tc // SUBLANES
    row = lax.broadcasted_iota(jnp.int32, a.shape, 0)
    a, b = a.reshape(nt, SUBLANES, width), b.reshape(nt, SUBLANES, width)
    in_tile = lax.broadcasted_iota(jnp.int32, a.shape, 1)
    for k in (1, 2, 4):
        shift = SUBLANES - k if rev else k
        edge = in_tile >= SUBLANES - k if rev else in_tile < k
        b = jnp.where(edge, b, a * pltpu.roll(b, shift, 1) + b)
        a = jnp.where(edge, a, a * pltpu.roll(a, shift, 1))
    a, b = a.reshape(tc, width), b.reshape(tc, width)
    hs = [None] * nt
    c = carry
    for kt in range(nt):
        k = nt - 1 - kt if rev else kt
        h = b[k * SUBLANES:(k + 1) * SUBLANES] + a[k * SUBLANES:(k + 1) * SUBLANES] * c
        hs[k] = h
        c = h[0:1] if rev else h[SUBLANES - 1:SUBLANES]
    h = jnp.concatenate(hs, axis=0)
    if rev:
        return h, jnp.where(row == tc - 1, carry, pltpu.roll(h, tc - 1, 0)), c
    return h, jnp.where(row == 0, carry, pltpu.roll(h, 1, 0)), c


def scan_adj(name, a, dh, hprev, direction, n_rows, t_lat):
    w = a.shape[1]
    tc = _pick(math.gcd(t_lat, n_rows), (256, 128))
    nb, nbl = n_rows // tc, t_lat // tc
    order = _adjoint_order(direction, nb, nbl)
    rev = direction == 'f'

    def dh_order(s):
        c, _ = order(s)
        return (jnp.minimum(c, nbl - 1), 0)

    def body(a_ref, dh_ref, hp_ref, du_ref, da_ref, carry):
        s = pl.program_id(0)

        @pl.when(s == 0)
        def _():
            carry[...] = jnp.zeros_like(carry)

        chunk, _ = order(s)
        live = (chunk < nbl).astype(F32)

        av = a_ref[...]
        dv = dh_ref[...] * live
        _, c_next, carry[...] = _chunk_scan(av, av * dv, carry[...], rev)
        lam = dv + c_next
        du_ref[...] = lam
        da_ref[...] = lam * hp_ref[...]

    spec = pl.BlockSpec((tc, w), order)
    return pl.pallas_call(
        body, name=name, grid=(nb,),
        out_shape=[jax.ShapeDtypeStruct((n_rows, w), F32)] * 2,
        in_specs=[spec, pl.BlockSpec((tc, w), dh_order), spec], out_specs=[spec, spec],
        scratch_shapes=[pltpu.VMEM((1, w), F32)],
        compiler_params=_cparams(("arbitrary",)),
    )(a, dh, hprev)


def _one_minus_a_squared(log_a, a):
    return (1.0 + a * a) * jnp.tanh(-log_a)


def _gate_elem(pre_r, pre_i, xc, b_a, b_x, sp):
    r = _sigmoid(pre_r + b_a)
    i = _sigmoid(pre_i + b_x)
    log_a = (-LRU_C) * r * sp
    a = jnp.exp(log_a)
    m2 = _one_minus_a_squared(log_a, a)
    mult = jnp.where(m2 > 0.0, m2 * lax.rsqrt(m2), 0.0)
    return a, mult * (i * xc)


def _gate_elem_bwd(pre_r, pre_i, xc, b_a, b_x, sp, da, du):
    r = _sigmoid(pre_r + b_a)
    i = _sigmoid(pre_i + b_x)
    log_a = (-LRU_C) * r * sp
    a = jnp.exp(log_a)
    m2 = _one_minus_a_squared(log_a, a)
    inv_mult = lax.rsqrt(m2)
    g = du * (m2 * inv_mult)
    d_mult = du * (i * xc)
    d_log_a = (da - d_mult * a * inv_mult) * a
    d_pre_r = d_log_a * ((-LRU_C) * sp) * (r * (1.0 - r))
    d_pre_i = g * xc * (i * (1.0 - i))
    return d_pre_r, d_pre_i, g * i, _rsum(d_log_a * ((-LRU_C) * r))


def _blockdiag(xb16, w_ref_val, d):
    outs = []
    for n in range(LRU_BLOCKS):
        outs.append(jnp.dot(xb16[:, n * LRU_BW:(n + 1) * LRU_BW], w_ref_val[d * LRU_BLOCKS + n],
                            preferred_element_type=F32))
    return jnp.concatenate(outs, axis=1)


def lru_scan(name, xc, w_a, w_x, b_a, b_x, sp, direction, n_rows, t_lat):
    w = xc.shape[1]
    d = 0 if direction == 'f' else 1
    tc = _pick(math.gcd(t_lat, n_rows), (256, 128))
    nb, nbl = n_rows // tc, t_lat // tc
    order = _chunk_order(direction, nb, nbl)
    rev = direction == 'b'

    def body(x_ref, wa_ref, wx_ref, ba_ref, bx_ref, sp_ref, a_ref, h_ref, hp_ref, carry):
        @pl.when(pl.program_id(0) == 0)
        def _():
            carry[...] = jnp.zeros_like(carry)

        x = x_ref[...]
        xb16 = x.astype(BF16)
        a, u = _gate_elem(_blockdiag(xb16, wa_ref[...], d), _blockdiag(xb16, wx_ref[...], d), x,
                          ba_ref[d:d + 1, :], bx_ref[d:d + 1, :], sp_ref[d:d + 1, :])
        a_ref[...] = a
        h_ref[...], hp_ref[...], carry[...] = _chunk_scan(a, u, carry[...], rev)

    spec = pl.BlockSpec((tc, w), order)
    whole = lambda p: pl.BlockSpec(p.shape, lambda s, nd=p.ndim: (0,) * nd)
    return pl.pallas_call(
        body, name=name, grid=(nb,),
        out_shape=[jax.ShapeDtypeStruct((n_rows, w), F32)] * 3,
        in_specs=[spec] + [whole(p) for p in (w_a, w_x, b_a, b_x, sp)], out_specs=[spec] * 3,
        scratch_shapes=[pltpu.VMEM((1, w), F32)],
        compiler_params=_cparams(("arbitrary",)),
    )(xc, w_a, w_x, b_a, b_x, sp)


def gates_bwd(xc, da_f, du_f, da_b, du_b, w_a, w_x, b_a, b_x, sp, n_rows, t_lat, tm):
    def fn(is_ctx, rows, params):
        (x, daf, duf, dab, dub), (wa, wx, ba, bx, spv) = rows, params
        xb16 = x.astype(BF16)
        dxc = jnp.zeros_like(x)
        dwa, dwx, dba, dbx, dsp = [], [], [], [], []
        for d, (da, du) in enumerate(((daf, duf), (dab, dub))):
            dpr, dpi, dx_e, dsp_d = _gate_elem_bwd(_blockdiag(xb16, wa, d), _blockdiag(xb16, wx, d), x,
                                                   ba[d:d + 1], bx[d:d + 1], spv[d:d + 1], da, du)
            dba_d, dbx_d = _rsum(dpr), _rsum(dpi)
            dxc = dxc + dx_e
            dpr16, dpi16 = dpr.astype(BF16), dpi.astype(BF16)
            back = []
            for n in range(LRU_BLOCKS):
                sl = slice(n * LRU_BW, (n + 1) * LRU_BW)
                nt_dims = (((1,), (1,)), ((), ()))
                back.append(lax.dot_general(dpr16[:, sl], wa[d * LRU_BLOCKS + n], nt_dims, preferred_element_type=F32)
                            + lax.dot_general(dpi16[:, sl], wx[d * LRU_BLOCKS + n], nt_dims,
                                              preferred_element_type=F32))
                tn_dims = (((0,), (0,)), ((), ()))
                dwa.append(lax.dot_general(xb16[:, sl], dpr16[:, sl], tn_dims, preferred_element_type=F32)[None])
                dwx.append(lax.dot_general(xb16[:, sl], dpi16[:, sl], tn_dims, preferred_element_type=F32)[None])
            dxc = dxc + jnp.concatenate(back, axis=1)
            dba.append(dba_d)
            dbx.append(dbx_d)
            dsp.append(dsp_d)
        cat0 = lambda xs: jnp.concatenate(xs, axis=0)
        return [dxc], [cat0(dwa), cat0(dwx), cat0(dba), cat0(dbx), cat0(dsp)]

    (dxc,), accs = rowwise("gates_bwd", fn,
                           [(xc, 0, LRU_W), (da_f, 0, LRU_W), (du_f, 0, LRU_W), (da_b, 0, LRU_W), (du_b, 0, LRU_W)],
                           [w_a, w_x, b_a, b_x, sp], [(LRU_W, F32)],
                           [(2 * LRU_BLOCKS, LRU_BW, LRU_BW)] * 2 + [(2, LRU_W)] * 3, n_rows, t_lat, tm)
    return dxc, accs


def _rope_tables(t_lat, n_rows):
    rows = t_lat // GRID_W
    row_ids = jnp.repeat(jnp.arange(rows), GRID_W).astype(F32)
    col_ids = jnp.tile(jnp.arange(GRID_W), rows).astype(F32)
    axis_dim = QK_ROPE // 2
    inv = 1.0 / (ROPE_BASE ** (jnp.arange(0, axis_dim, 2, dtype=F32) / axis_dim))
    ang = jnp.concatenate([row_ids[:, None] * inv, col_ids[:, None] * inv], axis=-1)
    cos, sin = jnp.cos(ang), jnp.sin(ang)
    half = QK_ROPE // 2
    ones, zeros = jnp.ones((t_lat, QK_NOPE), F32), jnp.zeros((t_lat, QK_NOPE), F32)
    pad1, pad0 = jnp.ones((t_lat, HEAD_PAD - QK_DIM), F32), jnp.zeros((t_lat, HEAD_PAD - QK_DIM), F32)
    zh = jnp.zeros((t_lat, half), F32)
    c_tab = jnp.concatenate([ones, cos, cos, pad1], axis=1)
    s1 = jnp.concatenate([zeros, -sin, zh, pad0], axis=1)
    s2 = jnp.concatenate([zeros, zh, sin, pad0], axis=1)
    n_ctx = n_rows - t_lat
    c_tab = jnp.concatenate([c_tab, jnp.ones((n_ctx, HEAD_PAD), F32)], axis=0)
    s1 = jnp.concatenate([s1, jnp.zeros((n_ctx, HEAD_PAD), F32)], axis=0)
    s2 = jnp.concatenate([s2, jnp.zeros((n_ctx, HEAD_PAD), F32)], axis=0)
    return c_tab, s1, s2


def _rope(x, c, s1, s2):
    half = QK_ROPE // 2
    return x * c + pltpu.roll(x, HEAD_PAD - half, 1) * s1 + pltpu.roll(x, half, 1) * s2


def _rope_t(dy, c, s1, s2):
    half = QK_ROPE // 2
    return dy * c + pltpu.roll(dy * s1, half, 1) + pltpu.roll(dy * s2, HEAD_PAD - half, 1)


def _heads(x):
    return [x[:, h * HEAD_PAD:(h + 1) * HEAD_PAD] for h in range(N_HEADS)]


Q_SCALE = QK_DIM ** -0.5 * math.log2(math.e)

def attn_fwd(q, k, v, t_lat, n_rows, tq):
    def body(q_ref, k_ref, v_ref, o_ref, lse_ref):
        s = lax.dot_general(q_ref[...], k_ref[...], (((1,), (1,)), ((), ())), preferred_element_type=F32)
        m = jnp.max(s, axis=-1, keepdims=True)
        p = jnp.exp2(s - m)
        l = jnp.sum(p, axis=-1, keepdims=True)
        o = jnp.dot(p.astype(BF16), v_ref[...], preferred_element_type=F32) / l
        o_ref[...] = o.astype(o_ref.dtype)
        lse_ref[...] = jnp.broadcast_to(m + jnp.log2(l), lse_ref.shape)

    qspec = pl.BlockSpec((tq, HEAD_PAD), lambda h, i: (i, h))
    kspec = pl.BlockSpec((n_rows, HEAD_PAD), lambda h, i: (0, h))
    return pl.pallas_call(
        body, name="attn_fwd", grid=(N_HEADS, t_lat // tq),
        out_shape=[jax.ShapeDtypeStruct((t_lat, N_HEADS * HEAD_PAD), BF16),
                   jax.ShapeDtypeStruct((t_lat, N_HEADS * HEAD_PAD), F32)],
        in_specs=[qspec, kspec, kspec], out_specs=[qspec, qspec],
        compiler_params=_cparams(("parallel", "arbitrary")),
    )(q, k, v)


def attn_bwd(q, k, v, o, do, lse, t_lat, n_rows, tq):
    scale = QK_DIM ** -0.5
    nq = t_lat // tq
    nt = (((1,), (1,)), ((), ()))
    tn = (((0,), (0,)), ((), ()))

    def body(q_ref, k_ref, v_ref, o_ref, do_ref, lse_ref, dq_ref, dk_ref, dv_ref):
        @pl.when(pl.program_id(1) == 0)
        def _():
            dk_ref[...] = jnp.zeros_like(dk_ref)
            dv_ref[...] = jnp.zeros_like(dv_ref)

        qv, kv, vv, dov = q_ref[...], k_ref[...], v_ref[...], do_ref[...]
        s = lax.dot_general(qv, kv, nt, preferred_element_type=F32)
        p = jnp.exp2(s - lse_ref[:, 0:1])
        dv_ref[...] += lax.dot_general(p.astype(BF16), dov, tn, preferred_element_type=F32)
        dp = lax.dot_general(dov, vv, nt, preferred_element_type=F32)
        delta = jnp.sum(dov.astype(F32) * o_ref[...].astype(F32), axis=-1, keepdims=True)
        ds = (p * (dp - delta)).astype(BF16)
        dq_ref[...] = (jnp.dot(ds, kv, preferred_element_type=F32) * scale).astype(dq_ref.dtype)
        dk_ref[...] += lax.dot_general(ds, qv, tn, preferred_element_type=F32)

        @pl.when(pl.program_id(1) == nq - 1)
        def _():
            dk_ref[...] = dk_ref[...] * (scale / Q_SCALE)

    qspec = pl.BlockSpec((tq, HEAD_PAD), lambda h, i: (i, h))
    kspec = pl.BlockSpec((n_rows, HEAD_PAD), lambda h, i: (0, h))
    return pl.pallas_call(
        body, name="attn_bwd", grid=(N_HEADS, t_lat // tq),
        out_shape=[jax.ShapeDtypeStruct((t_lat, N_HEADS * HEAD_PAD), BF16),
                   jax.ShapeDtypeStruct((n_rows, N_HEADS * HEAD_PAD), F32),
                   jax.ShapeDtypeStruct((n_rows, N_HEADS * HEAD_PAD), F32)],
        in_specs=[qspec, kspec, kspec, qspec, qspec, qspec], out_specs=[qspec, kspec, kspec],
        compiler_params=_cparams(("parallel", "arbitrary")),
    )(q, k, v, o, do, lse)


def adamw(name, w, g, m, v):
    r, ccols = w.shape
    if r % 8 == 0:
        tr, tcol = _best_div(r, 8, max(8, 262144 // ccols)), ccols
    else:
        tr, tcol = r, _pick(ccols, (256, 128))
    c1 = 1.0 - ADAM_B1 ** ADAM_STEP
    c2 = 1.0 - ADAM_B2 ** ADAM_STEP

    def body(w_ref, g_ref, m_ref, v_ref, d_ref, nm_ref, nv_ref):
        gv = g_ref[...]
        nm = ADAM_B1 * m_ref[...] + (1.0 - ADAM_B1) * gv
        nv = ADAM_B2 * v_ref[...] + (1.0 - ADAM_B2) * (gv * gv)
        d_ref[...] = -ADAM_LR * ((nm / c1) / (jnp.sqrt(nv / c2) + ADAM_EPS) + ADAM_WD * w_ref[...])
        nm_ref[...] = nm
        nv_ref[...] = nv

    spec = pl.BlockSpec((tr, tcol), lambda i, j: (i, j))
    return pl.pallas_call(
        body, name=name, grid=(r // tr, ccols // tcol),
        out_shape=[jax.ShapeDtypeStruct((r, ccols), F32)] * 3,
        in_specs=[spec] * 4, out_specs=[spec] * 3,
        compiler_params=_cparams(("parallel", "parallel")),
    )(w, g, m, v)


def adamw_many(name, ws, gs, ms, vs):
    n = len(ws)
    c1 = 1.0 - ADAM_B1 ** ADAM_STEP
    c2 = 1.0 - ADAM_B2 ** ADAM_STEP

    def body(*refs):
        for i in range(n):
            w_ref, g_ref, m_ref, v_ref = (refs[k * n + i] for k in range(4))
            d_ref, nm_ref, nv_ref = (refs[(4 + k) * n + i] for k in range(3))
            gv = g_ref[...]
            nm = ADAM_B1 * m_ref[...] + (1.0 - ADAM_B1) * gv
            nv = ADAM_B2 * v_ref[...] + (1.0 - ADAM_B2) * (gv * gv)
            d_ref[...] = -ADAM_LR * ((nm / c1) / (jnp.sqrt(nv / c2) + ADAM_EPS) + ADAM_WD * w_ref[...])
            nm_ref[...] = nm
            nv_ref[...] = nv

    vmem = pl.BlockSpec(memory_space=pltpu.VMEM)
    res = pl.pallas_call(
        body, name=name,
        out_shape=[jax.ShapeDtypeStruct(w.shape, F32) for w in ws] * 3,
        in_specs=[vmem] * (4 * n), out_specs=[vmem] * (3 * n),
        compiler_params=_cparams(),
    )(*ws, *gs, *ms, *vs)
    return [tuple(res[k * n + i] for k in range(3)) for i in range(n)]


def _flat(parts, dtype, row_mult):
    v = jnp.concatenate([p.reshape(-1).astype(dtype) for p in parts])
    quantum = row_mult * FLAT_C
    total = -(-v.shape[0] // quantum) * quantum
    return jnp.pad(v, (0, total - v.shape[0])).reshape(total // FLAT_C, FLAT_C)


def _gathered_to_full(name, g):
    k = g.shape[1]
    return jnp.transpose(g, (1, 0, 2)).reshape(k, N_DEV * g.shape[2])


def _shard_to_rb(name, w):
    return w if name in ROW_SHARDED else w.T


def _rb_to_shard(name, g):
    return g if name in ROW_SHARDED else g.T


def _rb_from_gathered(name, g):
    cols = g.shape[2]
    if name == 'w_in':
        z = lambda k: jnp.zeros((k, cols), g.dtype)
        full = g.reshape(N_DEV * g.shape[1], cols)
        return jnp.concatenate([full[:Z_KR], z(QK_NOPE), full[Z_KR:Z_KR + QK_ROPE], z(HEAD_PAD - QK_DIM),
                                full[Z_KR + QK_ROPE:]], axis=0)
    if name == 'w_uq':
        return jnp.pad(g, ((0, 0), (0, HEAD_PAD - QK_DIM), (0, 0))).reshape(N_HEADS * HEAD_PAD, cols)
    if name == 'w_ukv':
        pad = lambda t: jnp.pad(t, ((0, 0), (0, HEAD_PAD - t.shape[1]), (0, 0))).reshape(N_HEADS * HEAD_PAD, cols)
        return jnp.concatenate([pad(g[:, :QK_NOPE]), pad(g[:, QK_NOPE:])], axis=0)
    if name == 'w_o_attn':
        full = g.reshape(D, N_HEADS, V_HEAD)
        return jnp.pad(full, ((0, 0), (0, 0), (0, HEAD_PAD - V_HEAD))).reshape(D, N_HEADS * HEAD_PAD)
    return g.reshape(N_DEV * g.shape[1], cols)


def _chunks_from_rb_grad(name, g):
    cols = g.shape[1]
    if name == 'w_in':
        full = jnp.concatenate([g[:Z_KR], g[Z_KR + QK_NOPE:Z_KR + QK_DIM], g[Z_XB:]], axis=0)
        return full.reshape(N_DEV, -1, cols)
    if name == 'w_uq':
        return g.reshape(N_HEADS, HEAD_PAD, cols)[:, :QK_DIM]
    if name == 'w_ukv':
        half = N_HEADS * HEAD_PAD
        gk = g[:half].reshape(N_HEADS, HEAD_PAD, cols)[:, :QK_NOPE]
        gv = g[half:].reshape(N_HEADS, HEAD_PAD, cols)[:, :V_HEAD]
        return jnp.concatenate([gk, gv], axis=1)
    if name == 'w_o_attn':
        full = g.reshape(D, N_HEADS, HEAD_PAD)[:, :, :V_HEAD].reshape(D, N_HEADS * V_HEAD)
        return full.reshape(N_DEV, D // N_DEV, N_HEADS * V_HEAD)
    return g.reshape(N_DEV, -1, cols)


def local_step(x, ctx, target, mod_l, mod_c, wt, on_grad=None, arrive=None):
    t_lat, n_ctx = x.shape[0], ctx.shape[0]
    n = t_lat + n_ctx
    tm = _pick(math.gcd(t_lat, n), (256, 128))
    tq_fwd = _pick(t_lat, (256, 128))
    tq_bwd = _pick(t_lat, (512, 256, 128))
    row = lambda v: v.reshape(1, -1).astype(F32)
    two = lambda a, b: jnp.stack([a, b]).astype(F32)
    sh1_l, sc1_l, g1_l, sh2_l, sc2_l, g2_l = jnp.split(mod_l, 6)
    sh1_c, sc1_c = jnp.split(mod_c, 6)[:2]
    sc1, sh1 = two(sc1_l, sc1_c), two(sh1_l, sh1_c)
    g1, g2, sc2, sh2 = row(g1_l), row(g2_l), row(sc2_l), row(sh2_l)
    norm1_g, norm2_g, final_g = row(wt['norm1_g']), row(wt['norm2_g']), row(wt['final_g'])
    q_g, kv_g, b_gate = row(wt['q_norm_g']), row(wt['kv_norm_g']), row(wt['b_gate'])
    wt = dict(wt)
    pending = []

    def sent():
        tokens = list(pending)
        pending.clear()
        return tokens

    def need(names, after):
        if arrive is not None:
            got = arrive(names, after)
            if '_token' in got:
                pending.append(got.pop('_token'))
            wt.update(got)
        return [wt[n] for n in names]
    lru_w_a = wt['lru_w_a'].reshape(2 * LRU_BLOCKS, LRU_BW, LRU_BW).astype(BF16)
    lru_w_x = wt['lru_w_x'].reshape(2 * LRU_BLOCKS, LRU_BW, LRU_BW).astype(BF16)
    b_a, b_x, lam = wt['lru_b_a'], wt['lru_b_x'], wt['lru_lambda']
    sp = jnp.logaddexp(-lam, 0.0)
    c_tab, s1_tab, s2_tab = _rope_tables(t_lat, n)
    rw = functools.partial(rowwise, n_rows=n, t_lat=t_lat, tm=tm)
    rw_lat = functools.partial(rowwise, n_rows=t_lat, t_lat=t_lat, tm=tm)

    stream = [(x, 0, D), (ctx, 0, D, 'ctx')]

    def f_norm1(is_ctx, rows, params):
        (xl, xc_), (g, sc, sh) = rows, params
        return [_norm_mod(jnp.where(is_ctx, xc_, xl), g, _sel(is_ctx, sc), _sel(is_ctx, sh))], []

    (h,), _ = rw("norm1", f_norm1, stream, [norm1_g, sc1, sh1], [(D, BF16)], [])
    (w_in_t,) = need(('w_in',), h)
    z = matmul("w_in", h, w_in_t, 'nt', BF16, after=sent())
    w_uq_t, w_ukv_t, w_o_lru = need(('w_uq', 'w_ukv', 'w_o_lru'), z)

    def f_qkv_norm(is_ctx, rows, params):
        (ql, kvl), (gq, gkv) = rows, params
        return [_rms(ql, gq), _rms(kvl, gkv)], []

    (qn, kvn), _ = rw("qkv_norm", f_qkv_norm, [(z, Z_Q, Q_RANK), (z, Z_KV, KV_RANK)], [q_g, kv_g],
                      [(Q_RANK, BF16), (KV_RANK, BF16)], [])
    qp = matmul("w_uq", qn, w_uq_t, 'nt', BF16)
    kvp = matmul("w_ukv", kvn, w_ukv_t, 'nt', BF16)

    def f_rope(is_ctx, rows, params):
        qv, kk, vv, kr, c, s1, s2 = rows
        krr = _rope(kr, c, s1, s2)
        qo = jnp.concatenate([_rope(qh, c, s1, s2) for qh in _heads(qv)], axis=1) * Q_SCALE
        ko = jnp.concatenate([kh + krr for kh in _heads(kk)], axis=1)
        return [qo, ko, vv], []

    hp = N_HEADS * HEAD_PAD
    (qr, kr_, vr), _ = rw("rope", f_rope,
                          [(qp, 0, hp), (kvp, 0, hp), (kvp, hp, hp), (z, Z_KR, HEAD_PAD), (c_tab, 0, HEAD_PAD),
                           (s1_tab, 0, HEAD_PAD), (s2_tab, 0, HEAD_PAD)], [], [(hp, BF16)] * 3, [])
    attn, lse = attn_fwd(qr, kr_, vr, t_lat, n, tq_fwd)

    xc = conv_fwd("lru_conv", z, Z_XB, LRU_W, wt['lru_conv_w'], row(wt['lru_conv_b']), 2, n, t_lat, F32)
    a_f, h_f, hp_f = lru_scan("lru_scan_f", xc, lru_w_a, lru_w_x, b_a, b_x, sp, 'f', n, t_lat)
    a_b, h_b, hp_b = lru_scan("lru_scan_b", xc, lru_w_a, lru_w_x, b_a, b_x, sp, 'b', n, t_lat)

    def f_lru_out(is_ctx, rows, params):
        hf, hb, yb = rows
        return [(hf + hb) * _gelu(yb)], []

    (ybin,), _ = rw_lat("lru_out", f_lru_out, [(h_f, 0, LRU_W), (h_b, 0, LRU_W), (z, Z_YB, LRU_W)], [],
                        [(LRU_W, BF16)], [])
    w_o_attn_t, w_out, w_up_t, w_down = need(('w_o_attn', 'w_out', 'w_up', 'w_down'), attn)
    y_a = matmul("w_o_attn", attn, w_o_attn_t, 'nt', BF16)
    y_b = matmul("w_o_lru", ybin, w_o_lru, 'nn', BF16)

    def _merge(ya, yb, gl, bg):
        gates = _sigmoid(gl + bg)
        return gates[:, :D] * ya + gates[:, D:] * yb

    def f_merge(is_ctx, rows, params):
        (ya, yb, gl), (bg,) = rows, params
        return [_merge(ya, yb, gl, bg)], []

    (mrg,), _ = rw_lat("merge", f_merge, [(y_a, 0, D), (y_b, 0, D), (z, Z_GL, 2 * D)], [b_gate], [(D, BF16)], [])
    o = matmul("w_out", mrg, w_out, 'nn', BF16)

    def _res_norm2(xv, ov, g1v, g, sc, sh):
        x1 = xv + g1v * ov
        return x1, _norm_mod(x1, g, sc, sh)

    def f_norm2(is_ctx, rows, params):
        (xv, ov), (g1v, g, sc, sh) = rows, params
        x1, h2v = _res_norm2(xv, ov, g1v, g, sc, sh)
        return [x1, h2v], []

    (x1, h2), _ = rw_lat("norm2", f_norm2, [(x, 0, D), (o, 0, D)], [g1, norm2_g, sc2, sh2], [(D, F32), (D, BF16)], [])
    u = matmul("w_up", h2, w_up_t, 'nt', BF16)
    f = ffn_mix_fwd(u, wt['ffn_conv_w'], row(wt['ffn_conv_b']), t_lat)
    dn = matmul("w_down", f, w_down, 'nn', BF16)

    def _tile_loss(x1v, dv, g2v, fg, tgt):
        y = _rms(x1v + g2v * dv, fg)
        e = y - tgt
        return 0.5 * jnp.sum(jnp.mean(e * e, axis=-1, keepdims=True), axis=0, keepdims=True)

    def f_final(is_ctx, rows, params):
        (x1v, dv, tgt), (g2v, fg) = rows, params
        lv, vjp = jax.vjp(lambda a, b, c, d: _tile_loss(a, b, c, d, tgt), x1v, dv, g2v, fg)
        dx2, dd, dg2, dfg = vjp(jnp.ones((1, 1), F32))
        return [dx2, dd], [dg2, dfg, jnp.broadcast_to(lv, (1, 128))]

    (dx2, dd), (dg2, dfinal_g, loss_v) = rw_lat("final", f_final, [(x1, 0, D), (dn, 0, D), (target, 0, D)],
                                                [g2, final_g], [(D, F32), (D, BF16)], [(1, D), (1, D), (1, 128)])
    loss = loss_v[0, 0]

    grads = {'final_g': dfinal_g}

    def put(name, g):
        grads[name] = g
        if on_grad is not None:
            pending.append(on_grad(name, g))
    df = matmul("d_f", dd, w_down, 'nt', BF16)
    put('w_down', matmul("g_w_down", f, dd, 'tn', BF16))

    du, grads['ffn_conv_w'], grads['ffn_conv_b'] = ffn_mix_bwd(u, df, wt['ffn_conv_w'], row(wt['ffn_conv_b']),
                                                               t_lat)
    dh2 = matmul("d_h2", du, w_up_t, 'nn', BF16, after=sent())
    put('w_up', matmul("g_w_up", du, h2, 'tn', BF16))

    def b_norm2(is_ctx, rows, params):
        (xv, ov, dh2v, dx2v), (g1v, g, sc, sh) = rows, params
        _, vjp = jax.vjp(_res_norm2, xv, ov, g1v, g, sc, sh)
        dx, do, dg1v, dg, dsc, dsh = vjp((dx2v, dh2v))
        return [dx, do], [dg1v, dg, dsc, dsh]

    (dx_res, do), (dg1, dnorm2_g, dsc2, dsh2) = rw_lat(
        "norm2_bwd", b_norm2, [(x, 0, D), (o, 0, D), (dh2, 0, D), (dx2, 0, D)], [g1, norm2_g, sc2, sh2],
        [(D, F32), (D, BF16)], [(1, D)] * 4)
    grads['norm2_g'] = dnorm2_g
    dmrg = matmul("d_merge", do, w_out, 'nt', BF16, after=sent())
    put('w_out', matmul("g_w_out", mrg, do, 'tn', BF16))

    def b_merge(is_ctx, rows, params):
        (ya, yb, gl, dm), (bg,) = rows, params
        _, vjp = jax.vjp(_merge, ya, yb, gl, bg)
        dya, dyb, dgl, dbg = vjp(dm)
        return [dya, dyb, dgl], [dbg]

    (dy_a, dy_b, dgl), (grads['b_gate'],) = rw_lat(
        "merge_bwd", b_merge, [(y_a, 0, D), (y_b, 0, D), (z, Z_GL, 2 * D), (dmrg, 0, D)], [b_gate],
        [(D, BF16), (D, BF16), (2 * D, BF16)], [(1, 2 * D)])
    dattn = matmul("d_attn", dy_a, w_o_attn_t, 'nn', BF16, after=sent())
    put('w_o_attn', matmul("g_w_o_attn", dy_a, attn, 'tn', BF16))
    dybin = matmul("d_lru_out", dy_b, w_o_lru, 'nt', BF16, after=sent())
    put('w_o_lru', matmul("g_w_o_lru", ybin, dy_b, 'tn', BF16))

    def b_lru_out(is_ctx, rows, params):
        hf, hb, yb, dyv = rows
        _, vjp = jax.vjp(lambda s, y: s * _gelu(y), hf + hb, yb)
        dh, dyb = vjp(dyv)
        return [dh, dyb], []

    (dh_lru, dyb), _ = rw_lat("lru_out_bwd", b_lru_out,
                              [(h_f, 0, LRU_W), (h_b, 0, LRU_W), (z, Z_YB, LRU_W), (dybin, 0, LRU_W)], [],
                              [(LRU_W, F32), (LRU_W, BF16)], [])
    du_f, da_f = scan_adj("scan_f_adj", a_f, dh_lru, hp_f, 'f', n, t_lat)
    du_b, da_b = scan_adj("scan_b_adj", a_b, dh_lru, hp_b, 'b', n, t_lat)
    dxc, (dw_a, dw_x, db_a, db_x, dsp) = gates_bwd(xc, da_f, du_f, da_b, du_b, lru_w_a, lru_w_x, b_a, b_x, sp,
                                                   n, t_lat, tm)
    put('lru_w_a', dw_a.reshape(2 * LRU_BLOCKS * LRU_BW, LRU_BW).astype(BF16))
    put('lru_w_x', dw_x.reshape(2 * LRU_BLOCKS * LRU_BW, LRU_BW).astype(BF16))
    grads['lru_b_a'], grads['lru_b_x'] = db_a, db_x
    grads['lru_lambda'] = -dsp * _sigmoid(-lam)
    dxb, grads['lru_conv_w'], grads['lru_conv_b'] = conv_bwd("lru_conv_bwd", dxc, z, Z_XB, LRU_W, wt['lru_conv_w'],
                                                             2, n, t_lat)

    dq, dk, dv = attn_bwd(qr, kr_, vr, attn, dattn, lse, t_lat, n, tq_bwd)

    def b_rope(is_ctx, rows, params):
        dqv, dkv, dvv, c, s1, s2 = rows
        live = jnp.where(is_ctx, 0.0, 1.0)
        dqo = jnp.concatenate([_rope_t(dqh, c, s1, s2) for dqh in _heads(dqv)], axis=1) * live
        dkh = _heads(dkv)
        dkr = dkh[0]
        for t in dkh[1:]:
            dkr = dkr + t
        lanes = lax.broadcasted_iota(jnp.int32, dkr.shape, 1)
        dkr = jnp.where((lanes >= QK_NOPE) & (lanes < QK_DIM), _rope_t(dkr, c, s1, s2), 0.0)
        return [dqo, jnp.concatenate([dkv, dvv], axis=1), dkr], []

    (dqp, dkvp, dkr), _ = rw("rope_bwd", b_rope,
                             [(dq, 0, hp), (dk, 0, hp), (dv, 0, hp), (c_tab, 0, HEAD_PAD), (s1_tab, 0, HEAD_PAD),
                              (s2_tab, 0, HEAD_PAD)], [], [(hp, BF16), (2 * hp, BF16), (HEAD_PAD, BF16)], [])
    dqn = matmul("d_qn", dqp, w_uq_t, 'nn', BF16, after=sent())
    put('w_uq', matmul("g_w_uq", dqp, qn, 'tn', BF16))
    dkvn = matmul("d_kvn", dkvp, w_ukv_t, 'nn', BF16, after=sent())
    put('w_ukv', matmul("g_w_ukv", dkvp, kvn, 'tn', BF16))

    def b_qkv_norm(is_ctx, rows, params):
        (ql, kvl, dqv, dkvv), (gq, gkv) = rows, params
        _, vjp_q = jax.vjp(_rms, ql, gq)
        _, vjp_kv = jax.vjp(_rms, kvl, gkv)
        dql, dgq = vjp_q(dqv)
        dkvl, dgkv = vjp_kv(dkvv)
        return [dql, dkvl], [dgq, dgkv]

    (dq_lat, dkv_lat), (grads['q_norm_g'], grads['kv_norm_g']) = rw(
        "qkv_norm_bwd", b_qkv_norm, [(z, Z_Q, Q_RANK), (z, Z_KV, KV_RANK), (dqn, 0, Q_RANK), (dkvn, 0, KV_RANK)],
        [q_g, kv_g], [(Q_RANK, BF16), (KV_RANK, BF16)], [(1, Q_RANK), (1, KV_RANK)])
    pad_ctx = lambda t: jnp.pad(t, ((0, n_ctx), (0, 0)))
    dz = jnp.concatenate([dq_lat, dkv_lat, dkr, dxb, pad_ctx(dyb), pad_ctx(dgl)], axis=1)
    put('w_in', matmul("g_w_in", dz, h, 'tn', BF16))
    dh = matmul("d_h", dz, w_in_t, 'nn', BF16, after=sent())

    def b_norm1(is_ctx, rows, params):
        (xl, xc_, dhv, dxr), (g, sc, sh) = rows, params
        scv, shv = _sel(is_ctx, sc), _sel(is_ctx, sh)
        _, vjp = jax.vjp(_norm_mod, jnp.where(is_ctx, xc_, xl), g, scv, shv)
        dx, dg, dsc, dsh = vjp(dhv)
        return [dx + dxr], [dg, _seg_acc(is_ctx, dsc), _seg_acc(is_ctx, dsh)]

    (grad_x,), (grads['norm1_g'], dsc1, dsh1) = rw("norm1_bwd", b_norm1, stream + [(dh, 0, D), (dx_res, 0, D)],
                                                   [norm1_g, sc1, sh1], [(D, F32, 'lat')],
                                                   [(1, D), (2, D), (2, D)])
    zero = jnp.zeros((D,), F32)
    dmod_l = jnp.concatenate([dsh1[0], dsc1[0], dg1[0], dsh2[0], dsc2[0], dg2[0]])
    dmod_c = jnp.concatenate([dsh1[1], dsc1[1], zero, zero, zero, zero])
    return loss, grad_x, grads, dmod_l, dmod_c


def kernel(x, c, ctx, c_ctx, w_mod, b_mod, norm1_g, w_in, b_gate, q_norm_g, kv_norm_g, w_uq, w_ukv, w_o_attn, lru_conv_w, lru_conv_b, lru_w_a, lru_b_a, lru_w_x, lru_b_x, lru_lambda, w_o_lru, w_out, norm2_g, w_up, ffn_conv_w, ffn_conv_b, w_down, final_g, loss_target, m_c_ctx, m_w_mod, m_b_mod, m_norm1_g, m_w_in, m_b_gate, m_q_norm_g, m_kv_norm_g, m_w_uq, m_w_ukv, m_w_o_attn, m_lru_conv_w, m_lru_conv_b, m_lru_w_a, m_lru_b_a, m_lru_w_x, m_lru_b_x, m_lru_lambda, m_w_o_lru, m_w_out, m_norm2_g, m_w_up, m_ffn_conv_w, m_ffn_conv_b, m_w_down, m_final_g, v_c_ctx, v_w_mod, v_b_mod, v_norm1_g, v_w_in, v_b_gate, v_q_norm_g, v_kv_norm_g, v_w_uq, v_w_ukv, v_w_o_attn, v_lru_conv_w, v_lru_conv_b, v_lru_w_a, v_lru_b_a, v_lru_w_x, v_lru_b_x, v_lru_lambda, v_w_o_lru, v_w_out, v_norm2_g, v_w_up, v_ffn_conv_w, v_ffn_conv_b, v_w_down, v_final_g):
    given = dict(locals())
    strip = lambda name, a: a if name in ('c_ctx', 'final_g') else a[0]
    wsh = {n: strip(n, given[n]) for n in WEIGHTS}
    msh = {n: strip(n, given['m_' + n]) for n in WEIGHTS}
    vsh = {n: strip(n, given['v_' + n]) for n in WEIGHTS}
    me = _my_index()

    small = _flat([c[0]] + [wsh[n] for n in SMALL_F32], F32, 8)
    small_all = all_gather("gather_small", small).reshape(N_DEV, -1)
    c_all = small_all[:, :D]
    full, at = {}, D
    for n in SMALL_F32:
        cnt = math.prod(wsh[n].shape)
        full[n] = _gathered_to_full(n, small_all[:, at:at + cnt].reshape((N_DEV,) + wsh[n].shape))
        at += cnt

    cond = jnp.concatenate([c_all, c_ctx[None], jnp.zeros((7, D), F32)], axis=0)
    sil = cond * jax.nn.sigmoid(cond)
    mod_cols = matmul("mod_proj", sil, wsh['w_mod'], 'nn', F32)
    mod_all = all_gather("gather_mod", mod_cols)
    mod_all = jnp.transpose(mod_all, (1, 0, 2)).reshape(16, 6 * D) + b_mod[0][None]
    mod_l = lax.dynamic_index_in_dim(mod_all, me, axis=0, keepdims=False)
    mod_c = mod_all[N_DEV]

    rb_shards = {n: _shard_to_rb(n, wsh[n]).astype(BF16) for n in BIG_BF16}
    (w_in_blocks,) = all_gather_multi("gather_w_in", [rb_shards['w_in']])
    later = [n for n in BIG_BF16 if n != 'w_in']
    weights_started, weights_sent = exchange_start("weights_send", 'gather', [rb_shards[n] for n in later],
                                                   after=[w_in_blocks, mod_all])
    for n in REPLICATED:
        if n not in ('c_ctx', 'b_mod'):
            full[n] = wsh[n]

    def arrive(names, after):
        if names == ('w_in',):
            return {'w_in': _rb_from_gathered('w_in', w_in_blocks), '_token': weights_sent}
        picked = [later.index(n) for n in names]
        lands = exchange_wait("weights_wait_" + names[0], 'gather',
                              tuple([part[i] for i in picked] for part in weights_started), after)
        return {n: _rb_from_gathered(n, lax.dynamic_update_slice_in_dim(land, rb_shards[n][None], me, axis=0))
                for n, land in zip(names, lands)}

    in_flight = {}

    def on_grad(n, g):
        chunks = _chunks_from_rb_grad(n, g)
        own = lax.dynamic_index_in_dim(chunks, me, axis=0, keepdims=True)
        started, token = exchange_start("grad_send_" + n, 'scatter', [chunks])
        in_flight[n] = (own, started)
        return token

    loss, grad_x, grads, dmod_l, dmod_c = local_step(x[0], ctx[0], loss_target[0], mod_l, mod_c, full, on_grad,
                                                     arrive)
    dmod = jnp.stack([dmod_l, dmod_c]).reshape(2 * 6 * D // FLAT_C, FLAT_C)
    dm = all_gather("gather_dmod", dmod).reshape(N_DEV, 2, 6 * D)
    dmod_c_tot = dm[0, 1]
    for p in range(1, N_DEV):
        dmod_c_tot = dmod_c_tot + dm[p, 1]
    dm16 = jnp.concatenate([dm[:, 0], dmod_c_tot[None], jnp.zeros((7, 6 * D), F32)], axis=0)
    ncol = 6 * D // N_DEV
    dm16_cols = lax.dynamic_slice_in_dim(dm16.reshape(16, N_DEV, ncol), me, 1, axis=1)[:, 0]
    grad_w_mod = matmul("g_w_mod", sil, dm16_cols, 'tn', F32)
    dsil = matmul("d_cond", dm16_cols, wsh['w_mod'], 'nt', F32)
    sg = jax.nn.sigmoid(c_ctx)
    grads['c_ctx'] = dsil[N_DEV] * (sg * (1.0 + c_ctx * (1.0 - sg)))
    grads['b_mod'] = dmod_l + dmod_c

    g_final = {'w_mod': grad_w_mod}
    reduced, stepped = {}, {}
    for n in BIG_BF16 + ['lru_w_a', 'lru_w_x']:
        own, started = in_flight[n]
        (land,) = exchange_wait("grad_wait_" + n, 'scatter', started, dm)
        if n in ROW_SHARDED:
            g_final[n], *stepped[n] = reduce_slots("step_" + n, land, own, (wsh[n], msh[n], vsh[n]))
        elif n in COL_SHARDED and wsh[n].shape[1] % 128:
            g_t, *outs = reduce_slots("step_" + n, land, own, (wsh[n].T, msh[n].T, vsh[n].T))
            g_final[n], stepped[n] = g_t.T, [o.T for o in outs]
        else:
            reduced[n] = reduce_slots("sum_" + n, land, own)
            if n in BIG_BF16:
                g_final[n] = _rb_to_shard(n, reduced[n])

    small_names = SMALL_F32 + [n for n in REPLICATED if n not in ('lru_w_a', 'lru_w_x')]
    partials = _flat([grads[n] for n in small_names] + [loss], F32, 8)
    parts_all, a_all, x_all = all_gather_multi("gather_small_grads", [partials, reduced['lru_w_a'], reduced['lru_w_x']])
    small_sum = sum_slots("sum_small", parts_all).reshape(-1)
    g_final['lru_w_a'], g_final['lru_w_x'] = a_all.reshape(wsh['lru_w_a'].shape), x_all.reshape(wsh['lru_w_x'].shape)
    at = 0
    for n in small_names:
        cnt = math.prod(full[n].shape) if n in SMALL_F32 else math.prod(wsh[n].shape)
        g = small_sum[at:at + cnt]
        if n in SMALL_F32:
            k = full[n].shape[0]
            g = lax.dynamic_index_in_dim(g.reshape(k, N_DEV, -1), me, axis=1, keepdims=False)
        g_final[n] = g.reshape(wsh[n].shape)
        at += cnt
    loss = small_sum[at]

    for n in ['w_mod'] + BIG_BF16:
        if n not in stepped:
            stepped[n] = adamw("adamw_" + n, wsh[n], g_final[n], msh[n], vsh[n])
    rest = [n for n in WEIGHTS if n not in stepped]
    as2d = lambda a: a.reshape(-1, a.shape[-1])
    rest_out = adamw_many("adamw_small", *[[as2d(d[n]) for n in rest] for d in (wsh, g_final, msh, vsh)])
    stepped.update(zip(rest, rest_out))
    shaped = lambda n, a: a.reshape(given[n].shape)
    return (loss, grad_x[None],
            *[shaped(n, g_final[n]) for n in WEIGHTS],
            *[shaped(n, stepped[n][k]) for k in range(3) for n in WEIGHTS])
```

```python
import functools
import math

import jax
import jax.numpy as jnp
from jax import lax
from jax.experimental import pallas as pl
from jax.experimental.pallas import tpu as pltpu

F32 = jnp.float32
BF16 = jnp.bfloat16
MESH = pl.DeviceIdType.MESH

N_DEV = 8
D = 1024
N_HEADS = 8
HEAD_PAD = 128
QK_NOPE, QK_ROPE, V_HEAD = 64, 32, 64
QK_DIM = QK_NOPE + QK_ROPE
Q_RANK, KV_RANK = 384, 256
LRU_W, LRU_BLOCKS, LRU_BW = 1280, 10, 128
FFN = 2816
GRID_W = 64
ROPE_BASE = 10000.0
LRU_C = 8.0
EPS = 1e-6
Z_Q, Z_KV, Z_KR, Z_XB, Z_YB, Z_GL, Z_END = 0, 384, 640, 768, 2048, 3328, 5376
ADAM_LR, ADAM_B1, ADAM_B2, ADAM_EPS, ADAM_WD, ADAM_STEP = 0.001, 0.9, 0.999, 1e-08, 0.01, 10

VMEM_LIMIT = 52 * 1024 * 1024
FLAT_C = 512

WEIGHTS = ['c_ctx', 'w_mod', 'b_mod', 'norm1_g', 'w_in', 'b_gate', 'q_norm_g', 'kv_norm_g', 'w_uq', 'w_ukv',
           'w_o_attn', 'lru_conv_w', 'lru_conv_b', 'lru_w_a', 'lru_b_a', 'lru_w_x', 'lru_b_x', 'lru_lambda',
           'w_o_lru', 'w_out', 'norm2_g', 'w_up', 'ffn_conv_w', 'ffn_conv_b', 'w_down', 'final_g']
COL_SHARDED = ['w_in', 'w_uq', 'w_ukv', 'w_o_attn', 'lru_conv_w', 'lru_b_a', 'lru_b_x', 'lru_lambda', 'w_up',
               'ffn_conv_w']
ROW_SHARDED = ['w_o_lru', 'w_out', 'w_down']
BIG_BF16 = ['w_in', 'w_uq', 'w_ukv', 'w_o_attn', 'w_o_lru', 'w_out', 'w_up', 'w_down']
SMALL_F32 = ['lru_conv_w', 'lru_b_a', 'lru_b_x', 'lru_lambda', 'ffn_conv_w']
REPLICATED = ['c_ctx', 'b_mod', 'norm1_g', 'b_gate', 'q_norm_g', 'kv_norm_g', 'lru_conv_b', 'lru_w_a', 'lru_w_x',
              'norm2_g', 'ffn_conv_b', 'final_g']


def _cparams(sem=None):
    return pltpu.CompilerParams(dimension_semantics=sem, vmem_limit_bytes=VMEM_LIMIT)


def _pick(n, cands):
    for c in cands:
        if c <= n and n % c == 0:
            return c
    return n


def _best_div(n, mult, cap):
    best = mult
    for d in range(mult, min(n, cap) + 1, mult):
        if n % d == 0:
            best = d
    return best


MXU_DIM = 256
ROW_TILES = (1088, 1024, 544, 512, 256, 128, 64, 32, 16, 8)
LANE_TILES = (2816, 1792, 1536, 1280, 1024, 768, 512, 256, 1408, 896, 640, 384, 128)
DEPTH_ROW_TILES = (2176, 2048, 1024, 512, 256, 1088, 128, 64, 32, 16, 8)
MATMUL_VMEM_BUDGET = 40 * 1024 * 1024
MXU_FILL_OK = 0.9


def _my_pos():
    return lax.axis_index("x"), lax.axis_index("y"), lax.axis_index("c")


def _my_index():
    x, y, c = _my_pos()
    return 4 * x + 2 * y + c


def all_gather_multi(name, shards):
    n_arr = len(shards)
    arrays = range(n_arr)

    def body(*refs):
        x_refs, out_refs = refs[:n_arr], refs[n_arr:2 * n_arr]
        send_sems, recv_sems, local_sems = refs[2 * n_arr:]
        x, y, c = _my_pos()
        me, sibling = (x, y, c), (x, y, 1 - c)
        chips = [(1 - x, y), (x, 1 - y), (1 - x, 1 - y)]

        def slot(a, px, py, pc):
            return out_refs[a].at[4 * px + 2 * py + pc]

        def copy(a, k, block, to, src=None):
            return pltpu.make_async_remote_copy(
                src_ref=slot(a, *block) if src is None else src, dst_ref=slot(a, *block),
                send_sem=send_sems.at[7 * a + k], recv_sem=recv_sems.at[7 * a + k], device_id=to,
                device_id_type=MESH)

        mine = [pltpu.make_async_copy(x_refs[a], slot(a, *me), local_sems.at[a]) for a in arrays]
        first = [copy(a, 1 + j, me, (*chip, c), src=x_refs[a]) for j, chip in enumerate(chips) for a in arrays]
        first += [copy(a, 0, me, sibling, src=x_refs[a]) for a in arrays]
        for cp in first + mine:
            cp.start()
        passed = []
        for j, chip in enumerate(chips):
            for a in arrays:
                copy(a, 1 + j, (*chip, c), me).wait_recv()
                passed.append(copy(a, 4 + j, (*chip, c), sibling))
                passed[-1].start()
        for a in arrays:
            copy(a, 0, sibling, me).wait_recv()
            for j, chip in enumerate(chips):
                copy(a, 4 + j, (*chip, 1 - c), me).wait_recv()
        for cp in first + passed:
            cp.wait_send()
        for cp in mine:
            cp.wait()

    hbm = pl.BlockSpec(memory_space=pl.ANY)
    return pl.pallas_call(
        body, name=name,
        out_shape=[jax.ShapeDtypeStruct((N_DEV,) + s.shape, s.dtype) for s in shards],
        in_specs=[hbm] * n_arr, out_specs=[hbm] * n_arr,
        scratch_shapes=[pltpu.SemaphoreType.DMA((7 * n_arr,)), pltpu.SemaphoreType.DMA((7 * n_arr,)),
                        pltpu.SemaphoreType.DMA((n_arr,))],
    )(*shards)


def all_gather(name, shard):
    return all_gather_multi(name, [shard])[0]


def _peers():
    x, y, c = _my_pos()
    out = []
    for rel in (6, 4, 2, 7, 5, 3, 1):
        px, py, pc = x ^ ((rel >> 2) & 1), y ^ ((rel >> 1) & 1), c ^ (rel & 1)
        out.append((rel - 1, (px, py, pc), 4 * px + 2 * py + pc))
    return out


def _exchange_copies(mode, src_refs, land_refs, send_sems, recv_sems):
    x, y, c = _my_pos()
    me = 4 * x + 2 * y + c
    sends, arrivals = [], []
    for k, peer_pos, peer in _peers():
        for a, (src, land) in enumerate(zip(src_refs, land_refs)):
            piece = src.at[peer] if mode == 'scatter' else src
            sems = dict(send_sem=send_sems[a].at[k], recv_sem=recv_sems[a].at[k], device_id_type=MESH)
            sends.append(pltpu.make_async_remote_copy(src_ref=piece, dst_ref=land.at[me], device_id=peer_pos, **sems))
            arrivals.append(pltpu.make_async_remote_copy(src_ref=piece, dst_ref=land.at[peer], device_id=(x, y, c), **sems))
    return sends, arrivals


_HBM = pl.BlockSpec(memory_space=pltpu.HBM)
_SEM = pl.BlockSpec(memory_space=pltpu.SEMAPHORE)


def exchange_start(name, mode, arrays, after=()):
    n_arr, n_after = len(arrays), len(after)
    land_shapes = [a.shape if mode == 'scatter' else (N_DEV,) + a.shape for a in arrays]

    def body(*refs):
        src_refs, land_refs = refs[:n_arr], refs[n_arr:2 * n_arr]
        refs = refs[n_after:]
        send_sems, recv_sems = refs[2 * n_arr:3 * n_arr], refs[3 * n_arr:4 * n_arr]
        sends, _ = _exchange_copies(mode, src_refs, land_refs, send_sems, recv_sems)
        for cp in sends:
            cp.start()
        token = refs[-1]
        token[...] = jnp.zeros_like(token)

    sem = pltpu.SemaphoreType.DMA((N_DEV - 1,))
    res = pl.pallas_call(
        body, name=name,
        out_shape=[sem] * (2 * n_arr) + [pltpu.HBM(a.shape, a.dtype) for a in arrays]
        + [pltpu.HBM(s, a.dtype) for s, a in zip(land_shapes, arrays)] + [jax.ShapeDtypeStruct((8, 128), F32)],
        in_specs=[_HBM] * (2 * n_arr) + [pl.BlockSpec(memory_space=pl.ANY)] * n_after,
        out_specs=[_SEM] * (2 * n_arr) + [_HBM] * (2 * n_arr) + [pl.BlockSpec(memory_space=pltpu.VMEM)],
        input_output_aliases={i: 2 * n_arr + i for i in range(2 * n_arr)},
        compiler_params=pltpu.CompilerParams(has_side_effects=pltpu.SideEffectType.DATAFLOW_SIDE_EFFECTING),
    )(*[pltpu.with_memory_space_constraint(a, pltpu.HBM) for a in arrays],
      *[pltpu.with_memory_space_constraint(lax.empty(s, a.dtype), pltpu.HBM) for s, a in zip(land_shapes, arrays)],
      *after)
    return (res[:n_arr], res[n_arr:2 * n_arr], res[2 * n_arr:3 * n_arr], res[3 * n_arr:4 * n_arr]), res[-1]


def exchange_wait(name, mode, started, after):
    send_sems, recv_sems, thru, land = started
    n_arr = len(thru)

    def body(*refs):
        src_refs, land_refs = refs[:n_arr], refs[n_arr:2 * n_arr]
        s_sems, r_sems = refs[2 * n_arr:3 * n_arr], refs[3 * n_arr:4 * n_arr]
        sends, arrivals = _exchange_copies(mode, src_refs, land_refs, s_sems, r_sems)
        for cp in sends:
            cp.wait_send()
        for cp in arrivals:
            cp.wait_recv()

    res = pl.pallas_call(
        body, name=name,
        out_shape=[pltpu.HBM(a.shape, a.dtype) for a in thru] + [pltpu.HBM(a.shape, a.dtype) for a in land],
        in_specs=[_HBM] * (2 * n_arr) + [_SEM] * (2 * n_arr) + [pl.BlockSpec(memory_space=pl.ANY)],
        out_specs=[_HBM] * (2 * n_arr),
        input_output_aliases={i: i for i in range(2 * n_arr)},
        compiler_params=pltpu.CompilerParams(has_side_effects=pltpu.SideEffectType.DATAFLOW_SIDE_EFFECTING),
    )(*thru, *land, *send_sems, *recv_sems, after)
    return res[n_arr:]


def _sum_with_own(slot_ref, own_ref):
    x, y, c = _my_pos()
    me = 4 * x + 2 * y + c
    acc = None
    for p in range(N_DEV):
        v = jnp.where(me == p, own_ref[0], slot_ref[p]).astype(F32)
        acc = v if acc is None else acc + v
    return acc


def reduce_slots(name, slots, own, step=None):
    _, r, ccols = slots.shape
    tc = _pick(ccols, (256, 128))
    c1 = 1.0 - ADAM_B1 ** ADAM_STEP
    c2 = 1.0 - ADAM_B2 ** ADAM_STEP

    def body(s_ref, own_ref, *refs):
        g = _sum_with_own(s_ref, own_ref)
        if step is None:
            refs[0][...] = g
            return
        w_ref, m_ref, v_ref, g_ref, d_ref, nm_ref, nv_ref = refs
        nm = ADAM_B1 * m_ref[...] + (1.0 - ADAM_B1) * g
        nv = ADAM_B2 * v_ref[...] + (1.0 - ADAM_B2) * (g * g)
        g_ref[...] = g
        d_ref[...] = -ADAM_LR * ((nm / c1) / (jnp.sqrt(nv / c2) + ADAM_EPS) + ADAM_WD * w_ref[...])
        nm_ref[...] = nm
        nv_ref[...] = nv

    col = pl.BlockSpec((r, tc), lambda j: (0, j))
    n_out = 1 if step is None else 4
    res = pl.pallas_call(
        body, name=name, grid=(ccols // tc,),
        out_shape=[jax.ShapeDtypeStruct((r, ccols), F32)] * n_out,
        in_specs=[pl.BlockSpec((N_DEV, r, tc), lambda j: (0, 0, j)), pl.BlockSpec((1, r, tc), lambda j: (0, 0, j))]
        + [col] * (0 if step is None else 3),
        out_specs=[col] * n_out,
        compiler_params=_cparams(("parallel",)),
    )(slots, own, *(step or ()))
    return res[0] if step is None else res


def sum_slots(name, slots):
    _, r, ccols = slots.shape
    tc = _pick(ccols, (256, 128))

    def body(s_ref, o_ref):
        acc = s_ref[0].astype(F32)
        for p in range(1, N_DEV):
            acc = acc + s_ref[p].astype(F32)
        o_ref[...] = acc

    return pl.pallas_call(
        body, name=name, grid=(ccols // tc,),
        out_shape=jax.ShapeDtypeStruct((r, ccols), F32),
        in_specs=[pl.BlockSpec((N_DEV, r, tc), lambda j: (0, 0, j))],
        out_specs=pl.BlockSpec((r, tc), lambda j: (0, j)),
        compiler_params=_cparams(("parallel",)),
    )(slots)


def _mxu_fill(t):
    return t / (-(-t // MXU_DIM) * MXU_DIM)


def _matmul_tiles(mode, m_extent, n, k_extent, k_total, itemsizes):
    a_bytes, b_bytes, o_bytes = itemsizes
    m_cands = [c for c in (LANE_TILES if mode == 'tn' else ROW_TILES) if m_extent % c == 0] or [m_extent]
    k_cands = [c for c in (DEPTH_ROW_TILES if mode == 'tn' else LANE_TILES) if k_extent % c == 0] or [k_extent]
    n_cands = [c for c in LANE_TILES if n % c == 0] or [n]
    best = None
    for tm in m_cands:
        for tk in k_cands:
            for tn in n_cands:
                f32_tiles = 2 if k_total // tk > 1 else 1
                vmem = 2 * (tm * tk * a_bytes + tk * tn * b_bytes + tm * tn * o_bytes) + tm * tn * 4 * f32_tiles
                if vmem > MATMUL_VMEM_BUDGET:
                    continue
                key = (_mxu_fill(tn) * _mxu_fill(tk) >= MXU_FILL_OK, tm * tn * tk)
                if best is None or key > best[0]:
                    best = (key, (tm, tn, tk))
    assert best is not None, (mode, m_extent, n, k_extent)
    return best[1]


def matmul(name, a, b, mode, out_dtype, after=()):
    after = [t for t in after if t is not None]
    pieces, a_rows, a_cols = (1,) + a.shape if a.ndim == 2 else a.shape
    if mode == 'nn':
        (m, k), (k2, n) = (a_rows, pieces * a_cols), b.shape
    elif mode == 'nt':
        (m, k), (n, k2) = (a_rows, pieces * a_cols), b.shape
    else:
        (k, m), (k2, n) = (a_rows, pieces * a_cols), b.shape
    assert k == k2, (name, a.shape, b.shape, mode)
    tm, tn, tk = _matmul_tiles(mode, a_cols if mode == 'tn' else m, n, k if mode == 'tn' else a_cols, k,
                               (a.dtype.itemsize, b.dtype.itemsize, jnp.dtype(out_dtype).itemsize))
    nk = k // tk
    per_piece = a_cols // (tm if mode == 'tn' else tk)
    if a.ndim == 2:
        a_block = lambda rows, cols, at: pl.BlockSpec((rows, cols), at)
    else:
        a_block = lambda rows, cols, at: pl.BlockSpec(
            (None, rows, cols), lambda i, j, kk: (at(i, j, kk)[1] // per_piece, at(i, j, kk)[0],
                                                  at(i, j, kk)[1] % per_piece))
    if mode == 'nn':
        a_spec = a_block(tm, tk, lambda i, j, kk: (i, kk))
        b_spec = pl.BlockSpec((tk, tn), lambda i, j, kk: (kk, j))
        dn = (((1,), (0,)), ((), ()))
    elif mode == 'nt':
        a_spec = a_block(tm, tk, lambda i, j, kk: (i, kk))
        b_spec = pl.BlockSpec((tn, tk), lambda i, j, kk: (j, kk))
        dn = (((1,), (1,)), ((), ()))
    else:
        a_spec = a_block(tk, tm, lambda i, j, kk: (kk, i))
        b_spec = pl.BlockSpec((tk, tn), lambda i, j, kk: (kk, j))
        dn = (((0,), (0,)), ((), ()))

    def product(a_ref, b_ref):
        return lax.dot_general(a_ref[...].astype(BF16), b_ref[...].astype(BF16), dn, preferred_element_type=F32)

    n_after = len(after)

    def body_one(a_ref, b_ref, *rest):
        o_ref = rest[n_after]
        o_ref[...] = product(a_ref, b_ref).astype(o_ref.dtype)

    def body(a_ref, b_ref, *rest):
        o_ref, acc_ref = rest[n_after:]
        kk = pl.program_id(2)

        @pl.when(kk == 0)
        def _():
            acc_ref[...] = jnp.zeros_like(acc_ref)

        acc_ref[...] += product(a_ref, b_ref)

        @pl.when(kk == nk - 1)
        def _():
            o_ref[...] = acc_ref[...].astype(o_ref.dtype)

    return pl.pallas_call(
        body_one if nk == 1 else body, name=name, grid=(m // tm, n // tn, nk),
        out_shape=jax.ShapeDtypeStruct((m, n), out_dtype),
        in_specs=[a_spec, b_spec] + [pl.BlockSpec(memory_space=pl.ANY)] * n_after,
        out_specs=pl.BlockSpec((tm, tn), lambda i, j, kk: (i, j)),
        scratch_shapes=[] if nk == 1 else [pltpu.VMEM((tm, tn), F32)],
        compiler_params=_cparams(("parallel", "parallel", "arbitrary")),
    )(a, b, *after)


def rowwise(name, fn, rows, params, out_rows, out_accs, n_rows, t_lat, tm):
    nb, nbl = n_rows // tm, t_lat // tm
    in_specs, piece_counts = [], []
    operands = []
    for arr, off, width, *kind in rows:
        g = math.gcd(off, width) if off else width
        assert g % 128 == 0 or (off == 0 and width == arr.shape[1]), (name, off, width)
        cnt = width // g
        last = arr.shape[0] // tm - 1
        clamp = arr.shape[0] < n_rows
        for p in range(cnt):
            cb = off // g + p
            if kind == ['ctx']:
                in_specs.append(pl.BlockSpec(
                    (tm, g), lambda i, cb=cb, last=last: (jnp.clip(i - nbl, 0, last), cb)))
            elif clamp:
                in_specs.append(pl.BlockSpec((tm, g), lambda i, cb=cb, last=last: (jnp.minimum(i, last), cb)))
            else:
                in_specs.append(pl.BlockSpec((tm, g), lambda i, cb=cb: (i, cb)))
            operands.append(arr)
        piece_counts.append(cnt)
    for p in params:
        in_specs.append(pl.BlockSpec(p.shape, lambda i, nd=p.ndim: (0,) * nd))
        operands.append(p)
    n_in = sum(piece_counts)
    n_par = len(params)
    n_or = len(out_rows)
    lat_only = [kind == ['lat'] for _, _, *kind in out_rows]
    out_shape = [jax.ShapeDtypeStruct((t_lat if lat else n_rows, w), dt)
                 for (w, dt, *_), lat in zip(out_rows, lat_only)]
    out_shape += [jax.ShapeDtypeStruct(s, F32) for s in out_accs]
    out_specs = [pl.BlockSpec((tm, w), (lambda i: (jnp.minimum(i, nbl - 1), 0)) if lat else (lambda i: (i, 0)))
                 for (w, *_), lat in zip(out_rows, lat_only)]
    out_specs += [pl.BlockSpec(s, lambda i, nd=len(s): (0,) * nd) for s in out_accs]

    def body(*refs):
        in_refs, par_refs = refs[:n_in], refs[n_in:n_in + n_par]
        orow_refs = refs[n_in + n_par:n_in + n_par + n_or]
        oacc_refs = refs[n_in + n_par + n_or:]
        i = pl.program_id(0)
        tiles, at = [], 0
        for cnt in piece_counts:
            parts = [in_refs[at + p][...].astype(F32) for p in range(cnt)]
            tiles.append(parts[0] if cnt == 1 else jnp.concatenate(parts, axis=1))
            at += cnt
        is_ctx = i * tm >= t_lat
        outs, accs = fn(is_ctx, tiles, [p[...] for p in par_refs])
        for o_ref, o, lat in zip(orow_refs, outs, lat_only):
            if lat:
                @pl.when(jnp.logical_not(is_ctx))
                def _(o_ref=o_ref, o=o):
                    o_ref[...] = o.astype(o_ref.dtype)
            else:
                o_ref[...] = o.astype(o_ref.dtype)
        if oacc_refs:
            @pl.when(i == 0)
            def _():
                for a_ref in oacc_refs:
                    a_ref[...] = jnp.zeros_like(a_ref)
            for a_ref, a in zip(oacc_refs, accs):
                a_ref[...] += a.astype(F32)

    res = pl.pallas_call(
        body, name=name, grid=(nb,),
        out_shape=out_shape, in_specs=in_specs, out_specs=out_specs,
        compiler_params=_cparams(("arbitrary",)),
    )(*operands)
    return res[:n_or], res[n_or:]


def _rms(x, g):
    return x * lax.rsqrt(jnp.mean(x * x, axis=-1, keepdims=True) + EPS) * g


def _norm_mod(x, g, sc, sh):
    return _rms(x, g) * (1.0 + sc) + sh


def _sigmoid(x):
    return 0.5 * jnp.tanh(0.5 * x) + 0.5


def _silu(x):
    return x * _sigmoid(x)


def _gelu(x):
    return 0.5 * x * (1.0 + jnp.tanh(math.sqrt(2.0 / math.pi) * (x + 0.044715 * (x * x * x))))


def _sel(is_ctx, p):
    return jnp.where(is_ctx, p[1:2], p[0:1])


def _seg_acc(is_ctx, v):
    rows = lax.broadcasted_iota(jnp.int32, (2, v.shape[1]), 0)
    return jnp.where(rows == is_ctx.astype(jnp.int32), jnp.broadcast_to(v, (2, v.shape[1])), 0.0)


def _rsum(v):
    return jnp.sum(v, axis=0, keepdims=True)


def _shift_rows(x, o, t_lat, n):
    if o == 0:
        return x
    y = pltpu.roll(x, (-o) % n, 0)
    t = lax.broadcasted_iota(jnp.int32, x.shape, 0)
    if o > 0:
        ok = t < n - o
        if t_lat < n:
            ok = ok & ((t < t_lat - o) | (t >= t_lat))
    else:
        ok = t >= -o
        if t_lat < n:
            ok = ok & ((t < t_lat) | (t >= t_lat - o))
    return jnp.where(ok, y, 0.0)


def conv_fwd(name, xarr, col_off, width, w, b, left, n_rows, t_lat, out_dtype, cb=128):
    taps = w.shape[0]
    assert col_off % cb == 0 and width % cb == 0

    def body(x_ref, w_ref, b_ref, o_ref):
        x = x_ref[...].astype(F32)
        acc = jnp.broadcast_to(b_ref[...], x.shape)
        for k in range(taps):
            acc = acc + _shift_rows(x, k - left, t_lat, n_rows) * w_ref[k:k + 1, :]
        o_ref[...] = acc.astype(o_ref.dtype)

    return pl.pallas_call(
        body, name=name, grid=(width // cb,),
        out_shape=jax.ShapeDtypeStruct((n_rows, width), out_dtype),
        in_specs=[pl.BlockSpec((n_rows, cb), lambda j: (0, col_off // cb + j)),
                  pl.BlockSpec((taps, cb), lambda j: (0, j)),
                  pl.BlockSpec((1, cb), lambda j: (0, j))],
        out_specs=pl.BlockSpec((n_rows, cb), lambda j: (0, j)),
        compiler_params=_cparams(("parallel",)),
    )(xarr, w, b)


def conv_bwd(name, dout, xarr, col_off, width, w, left, n_rows, t_lat, cb=128):
    taps = w.shape[0]

    def body(d_ref, x_ref, w_ref, dx_ref, dw_ref, db_ref):
        d = d_ref[...].astype(F32)
        x = x_ref[...].astype(F32)
        dx = jnp.zeros_like(d)
        dws = []
        for k in range(taps):
            dx = dx + _shift_rows(d, left - k, t_lat, n_rows) * w_ref[k:k + 1, :]
            dws.append(_rsum(d * _shift_rows(x, k - left, t_lat, n_rows)))
        dx_ref[...] = dx.astype(dx_ref.dtype)
        dw_ref[...] = jnp.concatenate(dws, axis=0)
        db_ref[...] = _rsum(d)

    return pl.pallas_call(
        body, name=name, grid=(width // cb,),
        out_shape=[jax.ShapeDtypeStruct((n_rows, width), BF16), jax.ShapeDtypeStruct((taps, width), F32),
                   jax.ShapeDtypeStruct((1, width), F32)],
        in_specs=[pl.BlockSpec((n_rows, cb), lambda j: (0, j)),
                  pl.BlockSpec((n_rows, cb), lambda j: (0, col_off // cb + j)),
                  pl.BlockSpec((taps, cb), lambda j: (0, j))],
        out_specs=[pl.BlockSpec((n_rows, cb), lambda j: (0, j)), pl.BlockSpec((taps, cb), lambda j: (0, j)),
                   pl.BlockSpec((1, cb), lambda j: (0, j))],
        compiler_params=_cparams(("parallel",)),
    )(dout, xarr, w)


def _ffn_conv(a, w_ref, b_ref, t_lat):
    shifted = [_shift_rows(a, k - 1, t_lat, t_lat) for k in range(3)]
    ac = jnp.broadcast_to(b_ref[...], a.shape)
    for k in range(3):
        ac = ac + shifted[k] * w_ref[k:k + 1, :]
    return ac, shifted


def ffn_mix_fwd(u, w, b, t_lat, cb=128):
    nblk = FFN // cb

    def body(a_ref, g_ref, w_ref, b_ref, f_ref):
        ac, _ = _ffn_conv(a_ref[...].astype(F32), w_ref, b_ref, t_lat)
        f_ref[...] = (_silu(ac) * g_ref[...].astype(F32)).astype(f_ref.dtype)

    col = lambda shape, off=0: pl.BlockSpec(shape, lambda j: (0, off + j))
    return pl.pallas_call(
        body, name="ffn_mix", grid=(nblk,),
        out_shape=jax.ShapeDtypeStruct((t_lat, FFN), BF16),
        in_specs=[col((t_lat, cb)), col((t_lat, cb), nblk), col((3, cb)), col((1, cb))],
        out_specs=col((t_lat, cb)),
        compiler_params=_cparams(("parallel",)),
    )(u, u, w, b)


def ffn_mix_bwd(u, df, w, b, t_lat, cb=128):
    nblk = FFN // cb

    def body(a_ref, g_ref, df_ref, w_ref, b_ref, du_ref, dw_ref, db_ref):
        ac, shifted = _ffn_conv(a_ref[...].astype(F32), w_ref, b_ref, t_lat)
        d = df_ref[...].astype(F32)
        s = _sigmoid(ac)
        du_ref[1] = (d * (ac * s)).astype(du_ref.dtype)
        dac = d * g_ref[...].astype(F32) * (s * (1.0 + ac * (1.0 - s)))
        da = jnp.zeros_like(dac)
        for k in range(3):
            da = da + _shift_rows(dac, 1 - k, t_lat, t_lat) * w_ref[k:k + 1, :]
        du_ref[0] = da.astype(du_ref.dtype)
        dw_ref[...] = jnp.concatenate([_rsum(dac * shifted[k]) for k in range(3)], axis=0)
        db_ref[...] = _rsum(dac)

    col = lambda shape, off=0: pl.BlockSpec(shape, lambda j: (0, off + j))
    return pl.pallas_call(
        body, name="ffn_mix_bwd", grid=(nblk,),
        out_shape=[jax.ShapeDtypeStruct((2, t_lat, FFN), BF16),
                   jax.ShapeDtypeStruct((3, FFN), F32), jax.ShapeDtypeStruct((1, FFN), F32)],
        in_specs=[col((t_lat, cb)), col((t_lat, cb), nblk), col((t_lat, cb)), col((3, cb)), col((1, cb))],
        out_specs=[pl.BlockSpec((2, t_lat, cb), lambda j: (0, 0, j)), col((3, cb)), col((1, cb))],
        compiler_params=_cparams(("parallel",)),
    )(u, u, df, w, b)


def _chunk_order(direction, nb, nbl):
    if direction == 'f':
        return lambda s: ((s + nbl) % nb, 0)
    return lambda s: (nb - 1 - s, 0)


def _adjoint_order(direction, nb, nbl):
    if direction == 'f':
        return lambda s: ((nb - 1 - s + nbl) % nb, 0)
    return lambda s: (s, 0)


SUBLANES = 8


def _chunk_scan(a, b, carry, rev):
    tc, width = a.shape
    nt = tc // SUBLANES
    row = lax.broadcasted_iota(jnp.int32, a.shape, 0)
    a, b = a.reshape(nt, SUBLANES, width), b.reshape(nt, SUBLANES, width)
    in_tile = lax.broadcasted_iota(jnp.int32, a.shape, 1)
    for k in (1, 2, 4):
        shift = SUBLANES - k if rev else k
        edge = in_tile >= SUBLANES - k if rev else in_tile < k
        b = jnp.where(edge, b, a * pltpu.roll(b, shift, 1) + b)
        a = jnp.where(edge, a, a * pltpu.roll(a, shift, 1))
    a, b = a.reshape(tc, width), b.reshape(tc, width)
    hs = [None] * nt
    c = carry
    for kt in range(nt):
        k = nt - 1 - kt if rev else kt
        h = b[k * SUBLANES:(k + 1) * SUBLANES] + a[k * SUBLANES:(k + 1) * SUBLANES] * c
        hs[k] = h
        c = h[0:1] if rev else h[SUBLANES - 1:SUBLANES]
    h = jnp.concatenate(hs, axis=0)
    if rev:
        return h, jnp.where(row == tc - 1, carry, pltpu.roll(h, tc - 1, 0)), c
    return h, jnp.where(row == 0, carry, pltpu.roll(h, 1, 0)), c


def _one_minus_a_squared(log_a, a):
    return (1.0 + a * a) * jnp.tanh(-log_a)


def _gate_elem(pre_r, pre_i, xc, b_a, b_x, sp):
    r = _sigmoid(pre_r + b_a)
    i = _sigmoid(pre_i + b_x)
    log_a = (-LRU_C) * r * sp
    a = jnp.exp(log_a)
    m2 = _one_minus_a_squared(log_a, a)
    mult = jnp.where(m2 > 0.0, m2 * lax.rsqrt(m2), 0.0)
    return a, mult * (i * xc)


def _gate_elem_bwd(pre_r, pre_i, xc, b_a, b_x, sp, da, du):
    r = _sigmoid(pre_r + b_a)
    i = _sigmoid(pre_i + b_x)
    log_a = (-LRU_C) * r * sp
    a = jnp.exp(log_a)
    m2 = _one_minus_a_squared(log_a, a)
    inv_mult = lax.rsqrt(m2)
    g = du * (m2 * inv_mult)
    d_mult = du * (i * xc)
    d_log_a = (da - d_mult * a * inv_mult) * a
    d_pre_r = d_log_a * ((-LRU_C) * sp) * (r * (1.0 - r))
    d_pre_i = g * xc * (i * (1.0 - i))
    return d_pre_r, d_pre_i, g * i, _rsum(d_log_a * ((-LRU_C) * r))


def _blockdiag(xb16, w_ref_val, d):
    outs = []
    for n in range(LRU_BLOCKS):
        outs.append(jnp.dot(xb16[:, n * LRU_BW:(n + 1) * LRU_BW], w_ref_val[d * LRU_BLOCKS + n],
                            preferred_element_type=F32))
    return jnp.concatenate(outs, axis=1)


def lru_scan(name, xc, w_a, w_x, b_a, b_x, sp, direction, n_rows, t_lat):
    w = xc.shape[1]
    d = 0 if direction == 'f' else 1
    tc = _pick(math.gcd(t_lat, n_rows), (256, 128))
    nb, nbl = n_rows // tc, t_lat // tc
    order = _chunk_order(direction, nb, nbl)
    rev = direction == 'b'

    def body(x_ref, wa_ref, wx_ref, ba_ref, bx_ref, sp_ref, a_ref, h_ref, hp_ref, carry):
        @pl.when(pl.program_id(0) == 0)
        def _():
            carry[...] = jnp.zeros_like(carry)

        x = x_ref[...]
        xb16 = x.astype(BF16)
        a, u = _gate_elem(_blockdiag(xb16, wa_ref[...], d), _blockdiag(xb16, wx_ref[...], d), x,
                          ba_ref[d:d + 1, :], bx_ref[d:d + 1, :], sp_ref[d:d + 1, :])
        a_ref[...] = a
        h_ref[...], hp_ref[...], carry[...] = _chunk_scan(a, u, carry[...], rev)

    spec = pl.BlockSpec((tc, w), order)
    whole = lambda p: pl.BlockSpec(p.shape, lambda s, nd=p.ndim: (0,) * nd)
    return pl.pallas_call(
        body, name=name, grid=(nb,),
        out_shape=[jax.ShapeDtypeStruct((n_rows, w), F32)] * 3,
        in_specs=[spec] + [whole(p) for p in (w_a, w_x, b_a, b_x, sp)], out_specs=[spec] * 3,
        scratch_shapes=[pltpu.VMEM((1, w), F32)],
        compiler_params=_cparams(("arbitrary",)),
    )(xc, w_a, w_x, b_a, b_x, sp)


def lru_scan_bwd(name, xc, a, dh, hprev, dxc_in, w_a, w_x, b_a, b_x, sp, direction, n_rows, t_lat):
    w = xc.shape[1]
    d = 0 if direction == 'f' else 1
    tc = _pick(math.gcd(t_lat, n_rows), (256, 128))
    nb, nbl = n_rows // tc, t_lat // tc
    order = _adjoint_order(direction, nb, nbl)
    rev = direction == 'f'
    has_in = dxc_in is not None
    nt_dims, tn_dims = (((1,), (1,)), ((), ())), (((0,), (0,)), ((), ()))

    def dh_order(s):
        c, _ = order(s)
        return (jnp.minimum(c, nbl - 1), 0)

    def body(*refs):
        x_ref, a_ref, dh_ref, hp_ref = refs[:4]
        in_ref = refs[4] if has_in else None
        wa_ref, wx_ref, ba_ref, bx_ref, sp_ref = refs[4 + has_in:9 + has_in]
        dx_ref, dwa_ref, dwx_ref, dba_ref, dbx_ref, dsp_ref, carry = refs[9 + has_in:]
        s = pl.program_id(0)

        @pl.when(s == 0)
        def _():
            carry[...] = jnp.zeros_like(carry)
            for acc in (dwa_ref, dwx_ref, dba_ref, dbx_ref, dsp_ref):
                acc[...] = jnp.zeros_like(acc)

        chunk, _ = order(s)
        live = (chunk < nbl).astype(F32)
        av = a_ref[...]
        dv = dh_ref[...].astype(F32) * live
        _, c_next, carry[...] = _chunk_scan(av, av * dv, carry[...], rev)
        lam = dv + c_next

        x = x_ref[...]
        xb16 = x.astype(BF16)
        wa, wx = wa_ref[...], wx_ref[...]
        dpr, dpi, dxc, dsp_d = _gate_elem_bwd(_blockdiag(xb16, wa, d), _blockdiag(xb16, wx, d), x,
                                              ba_ref[d:d + 1, :], bx_ref[d:d + 1, :], sp_ref[d:d + 1, :],
                                              lam * hp_ref[...], lam)
        dpr16, dpi16 = dpr.astype(BF16), dpi.astype(BF16)
        back = []
        for n in range(LRU_BLOCKS):
            sl = slice(n * LRU_BW, (n + 1) * LRU_BW)
            back.append(lax.dot_general(dpr16[:, sl], wa[d * LRU_BLOCKS + n], nt_dims, preferred_element_type=F32)
                        + lax.dot_general(dpi16[:, sl], wx[d * LRU_BLOCKS + n], nt_dims, preferred_element_type=F32))
            dwa_ref[n] += lax.dot_general(xb16[:, sl], dpr16[:, sl], tn_dims, preferred_element_type=F32)
            dwx_ref[n] += lax.dot_general(xb16[:, sl], dpi16[:, sl], tn_dims, preferred_element_type=F32)
        dxc = dxc + jnp.concatenate(back, axis=1)
        dx_ref[...] = dxc + in_ref[...] if has_in else dxc
        dba_ref[...] += _rsum(dpr)
        dbx_ref[...] += _rsum(dpi)
        dsp_ref[...] += dsp_d

    spec = pl.BlockSpec((tc, w), order)
    whole = lambda shape: pl.BlockSpec(shape, lambda s, nd=len(shape): (0,) * nd)
    params = (w_a, w_x, b_a, b_x, sp)
    acc_shapes = [(LRU_BLOCKS, LRU_BW, LRU_BW)] * 2 + [(1, w)] * 3
    return pl.pallas_call(
        body, name=name, grid=(nb,),
        out_shape=[jax.ShapeDtypeStruct((n_rows, w), F32)] + [jax.ShapeDtypeStruct(sh, F32) for sh in acc_shapes],
        in_specs=[spec, spec, pl.BlockSpec((tc, w), dh_order), spec] + [spec] * has_in
        + [whole(p.shape) for p in params],
        out_specs=[spec] + [whole(sh) for sh in acc_shapes],
        scratch_shapes=[pltpu.VMEM((1, w), F32)],
        compiler_params=_cparams(("arbitrary",)),
    )(xc, a, dh, hprev, *([dxc_in] if has_in else []), *params)


def _rope_tables(t_lat, n_rows):
    rows = t_lat // GRID_W
    row_ids = jnp.repeat(jnp.arange(rows), GRID_W).astype(F32)
    col_ids = jnp.tile(jnp.arange(GRID_W), rows).astype(F32)
    axis_dim = QK_ROPE // 2
    inv = 1.0 / (ROPE_BASE ** (jnp.arange(0, axis_dim, 2, dtype=F32) / axis_dim))
    ang = jnp.concatenate([row_ids[:, None] * inv, col_ids[:, None] * inv], axis=-1)
    cos, sin = jnp.cos(ang), jnp.sin(ang)
    half = QK_ROPE // 2
    ones, zeros = jnp.ones((t_lat, QK_NOPE), F32), jnp.zeros((t_lat, QK_NOPE), F32)
    pad1, pad0 = jnp.ones((t_lat, HEAD_PAD - QK_DIM), F32), jnp.zeros((t_lat, HEAD_PAD - QK_DIM), F32)
    zh = jnp.zeros((t_lat, half), F32)
    c_tab = jnp.concatenate([ones, cos, cos, pad1], axis=1)
    s1 = jnp.concatenate([zeros, -sin, zh, pad0], axis=1)
    s2 = jnp.concatenate([zeros, zh, sin, pad0], axis=1)
    n_ctx = n_rows - t_lat
    c_tab = jnp.concatenate([c_tab, jnp.ones((n_ctx, HEAD_PAD), F32)], axis=0)
    s1 = jnp.concatenate([s1, jnp.zeros((n_ctx, HEAD_PAD), F32)], axis=0)
    s2 = jnp.concatenate([s2, jnp.zeros((n_ctx, HEAD_PAD), F32)], axis=0)
    return c_tab, s1, s2


def _rope(x, c, s1, s2):
    half = QK_ROPE // 2
    return x * c + pltpu.roll(x, HEAD_PAD - half, 1) * s1 + pltpu.roll(x, half, 1) * s2


def _rope_t(dy, c, s1, s2):
    half = QK_ROPE // 2
    return dy * c + pltpu.roll(dy * s1, half, 1) + pltpu.roll(dy * s2, HEAD_PAD - half, 1)


def _heads(x):
    return [x[:, h * HEAD_PAD:(h + 1) * HEAD_PAD] for h in range(N_HEADS)]


Q_SCALE = QK_DIM ** -0.5 * math.log2(math.e)

def attn_fwd(q, k, v, t_lat, n_rows, tq):
    def body(q_ref, k_ref, v_ref, o_ref, lse_ref):
        s = lax.dot_general(q_ref[...], k_ref[...], (((1,), (1,)), ((), ())), preferred_element_type=F32)
        m = jnp.max(s, axis=-1, keepdims=True)
        p = jnp.exp2(s - m)
        l = jnp.sum(p, axis=-1, keepdims=True)
        o = jnp.dot(p.astype(BF16), v_ref[...], preferred_element_type=F32) / l
        o_ref[...] = o.astype(o_ref.dtype)
        lse_ref[...] = jnp.broadcast_to(m + jnp.log2(l), lse_ref.shape)

    qspec = pl.BlockSpec((tq, HEAD_PAD), lambda h, i: (i, h))
    kspec = pl.BlockSpec((n_rows, HEAD_PAD), lambda h, i: (0, h))
    return pl.pallas_call(
        body, name="attn_fwd", grid=(N_HEADS, t_lat // tq),
        out_shape=[jax.ShapeDtypeStruct((t_lat, N_HEADS * HEAD_PAD), BF16),
                   jax.ShapeDtypeStruct((t_lat, N_HEADS * HEAD_PAD), F32)],
        in_specs=[qspec, kspec, kspec], out_specs=[qspec, qspec],
        compiler_params=_cparams(("parallel", "arbitrary")),
    )(q, k, v)


def attn_bwd(q, k, v, o, do, lse, t_lat, n_rows, tq):
    scale = QK_DIM ** -0.5
    nq = t_lat // tq
    nt = (((1,), (1,)), ((), ()))
    tn = (((0,), (0,)), ((), ()))

    def body(q_ref, k_ref, v_ref, o_ref, do_ref, lse_ref, dq_ref, dk_ref, dv_ref):
        @pl.when(pl.program_id(1) == 0)
        def _():
            dk_ref[...] = jnp.zeros_like(dk_ref)
            dv_ref[...] = jnp.zeros_like(dv_ref)

        qv, kv, vv, dov = q_ref[...], k_ref[...], v_ref[...], do_ref[...]
        s = lax.dot_general(qv, kv, nt, preferred_element_type=F32)
        p = jnp.exp2(s - lse_ref[:, 0:1])
        dv_ref[...] += lax.dot_general(p.astype(BF16), dov, tn, preferred_element_type=F32)
        dp = lax.dot_general(dov, vv, nt, preferred_element_type=F32)
        delta = jnp.sum(dov.astype(F32) * o_ref[...].astype(F32), axis=-1, keepdims=True)
        ds = (p * (dp - delta)).astype(BF16)
        dq_ref[...] = (jnp.dot(ds, kv, preferred_element_type=F32) * scale).astype(dq_ref.dtype)
        dk_ref[...] += lax.dot_general(ds, qv, tn, preferred_element_type=F32)

        @pl.when(pl.program_id(1) == nq - 1)
        def _():
            dk_ref[...] = dk_ref[...] * (scale / Q_SCALE)

    qspec = pl.BlockSpec((tq, HEAD_PAD), lambda h, i: (i, h))
    kspec = pl.BlockSpec((n_rows, HEAD_PAD), lambda h, i: (0, h))
    return pl.pallas_call(
        body, name="attn_bwd", grid=(N_HEADS, t_lat // tq),
        out_shape=[jax.ShapeDtypeStruct((t_lat, N_HEADS * HEAD_PAD), BF16),
                   jax.ShapeDtypeStruct((n_rows, N_HEADS * HEAD_PAD), F32),
                   jax.ShapeDtypeStruct((n_rows, N_HEADS * HEAD_PAD), F32)],
        in_specs=[qspec, kspec, kspec, qspec, qspec, qspec], out_specs=[qspec, kspec, kspec],
        compiler_params=_cparams(("parallel", "arbitrary")),
    )(q, k, v, o, do, lse)


def adamw(name, w, g, m, v):
    r, ccols = w.shape
    if r % 8 == 0:
        tr, tcol = _best_div(r, 8, max(8, 262144 // ccols)), ccols
    else:
        tr, tcol = r, _pick(ccols, (256, 128))
    c1 = 1.0 - ADAM_B1 ** ADAM_STEP
    c2 = 1.0 - ADAM_B2 ** ADAM_STEP

    def body(w_ref, g_ref, m_ref, v_ref, d_ref, nm_ref, nv_ref):
        gv = g_ref[...]
        nm = ADAM_B1 * m_ref[...] + (1.0 - ADAM_B1) * gv
        nv = ADAM_B2 * v_ref[...] + (1.0 - ADAM_B2) * (gv * gv)
        d_ref[...] = -ADAM_LR * ((nm / c1) / (jnp.sqrt(nv / c2) + ADAM_EPS) + ADAM_WD * w_ref[...])
        nm_ref[...] = nm
        nv_ref[...] = nv

    spec = pl.BlockSpec((tr, tcol), lambda i, j: (i, j))
    return pl.pallas_call(
        body, name=name, grid=(r // tr, ccols // tcol),
        out_shape=[jax.ShapeDtypeStruct((r, ccols), F32)] * 3,
        in_specs=[spec] * 4, out_specs=[spec] * 3,
        compiler_params=_cparams(("parallel", "parallel")),
    )(w, g, m, v)


def adamw_many(name, ws, gs, ms, vs):
    n = len(ws)
    c1 = 1.0 - ADAM_B1 ** ADAM_STEP
    c2 = 1.0 - ADAM_B2 ** ADAM_STEP

    def body(*refs):
        for i in range(n):
            w_ref, g_ref, m_ref, v_ref = (refs[k * n + i] for k in range(4))
            d_ref, nm_ref, nv_ref = (refs[(4 + k) * n + i] for k in range(3))
            gv = g_ref[...]
            nm = ADAM_B1 * m_ref[...] + (1.0 - ADAM_B1) * gv
            nv = ADAM_B2 * v_ref[...] + (1.0 - ADAM_B2) * (gv * gv)
            d_ref[...] = -ADAM_LR * ((nm / c1) / (jnp.sqrt(nv / c2) + ADAM_EPS) + ADAM_WD * w_ref[...])
            nm_ref[...] = nm
            nv_ref[...] = nv

    vmem = pl.BlockSpec(memory_space=pltpu.VMEM)
    res = pl.pallas_call(
        body, name=name,
        out_shape=[jax.ShapeDtypeStruct(w.shape, F32) for w in ws] * 3,
        in_specs=[vmem] * (4 * n), out_specs=[vmem] * (3 * n),
        compiler_params=_cparams(),
    )(*ws, *gs, *ms, *vs)
    return [tuple(res[k * n + i] for k in range(3)) for i in range(n)]


def _flat(parts, dtype, row_mult):
    v = jnp.concatenate([p.reshape(-1).astype(dtype) for p in parts])
    quantum = row_mult * FLAT_C
    total = -(-v.shape[0] // quantum) * quantum
    return jnp.pad(v, (0, total - v.shape[0])).reshape(total // FLAT_C, FLAT_C)


def _gathered_to_full(name, g):
    k = g.shape[1]
    return jnp.transpose(g, (1, 0, 2)).reshape(k, N_DEV * g.shape[2])


def _shard_to_rb(name, w):
    return w if name in ROW_SHARDED else w.T


def _rb_to_shard(name, g):
    return g if name in ROW_SHARDED else g.T


def _rb_from_gathered(name, g):
    cols = g.shape[2]
    if name == 'w_in':
        z = lambda k: jnp.zeros((k, cols), g.dtype)
        full = g.reshape(N_DEV * g.shape[1], cols)
        return jnp.concatenate([full[:Z_KR], z(QK_NOPE), full[Z_KR:Z_KR + QK_ROPE], z(HEAD_PAD - QK_DIM),
                                full[Z_KR + QK_ROPE:]], axis=0)
    if name == 'w_uq':
        return jnp.pad(g, ((0, 0), (0, HEAD_PAD - QK_DIM), (0, 0))).reshape(N_HEADS * HEAD_PAD, cols)
    if name == 'w_ukv':
        pad = lambda t: jnp.pad(t, ((0, 0), (0, HEAD_PAD - t.shape[1]), (0, 0))).reshape(N_HEADS * HEAD_PAD, cols)
        return jnp.concatenate([pad(g[:, :QK_NOPE]), pad(g[:, QK_NOPE:])], axis=0)
    if name == 'w_o_attn':
        full = g.reshape(D, N_HEADS, V_HEAD)
        return jnp.pad(full, ((0, 0), (0, 0), (0, HEAD_PAD - V_HEAD))).reshape(D, N_HEADS * HEAD_PAD)
    return g.reshape(N_DEV * g.shape[1], cols)


def _chunks_from_rb_grad(name, g):
    cols = g.shape[1]
    if name == 'w_in':
        full = jnp.concatenate([g[:Z_KR], g[Z_KR + QK_NOPE:Z_KR + QK_DIM], g[Z_XB:]], axis=0)
        return full.reshape(N_DEV, -1, cols)
    if name == 'w_uq':
        return g.reshape(N_HEADS, HEAD_PAD, cols)[:, :QK_DIM]
    if name == 'w_ukv':
        half = N_HEADS * HEAD_PAD
        gk = g[:half].reshape(N_HEADS, HEAD_PAD, cols)[:, :QK_NOPE]
        gv = g[half:].reshape(N_HEADS, HEAD_PAD, cols)[:, :V_HEAD]
        return jnp.concatenate([gk, gv], axis=1)
    if name == 'w_o_attn':
        full = g.reshape(D, N_HEADS, HEAD_PAD)[:, :, :V_HEAD].reshape(D, N_HEADS * V_HEAD)
        return full.reshape(N_DEV, D // N_DEV, N_HEADS * V_HEAD)
    return g.reshape(N_DEV, -1, cols)


def local_step(x, ctx, target, mod_l, mod_c, wt, on_grad=None, arrive=None):
    t_lat, n_ctx = x.shape[0], ctx.shape[0]
    n = t_lat + n_ctx
    tm = _pick(math.gcd(t_lat, n), (256, 128))
    tq_fwd = _pick(t_lat, (256, 128))
    tq_bwd = _pick(t_lat, (512, 256, 128))
    row = lambda v: v.reshape(1, -1).astype(F32)
    two = lambda a, b: jnp.stack([a, b]).astype(F32)
    sh1_l, sc1_l, g1_l, sh2_l, sc2_l, g2_l = jnp.split(mod_l, 6)
    sh1_c, sc1_c = jnp.split(mod_c, 6)[:2]
    sc1, sh1 = two(sc1_l, sc1_c), two(sh1_l, sh1_c)
    g1, g2, sc2, sh2 = row(g1_l), row(g2_l), row(sc2_l), row(sh2_l)
    norm1_g, norm2_g, final_g = row(wt['norm1_g']), row(wt['norm2_g']), row(wt['final_g'])
    q_g, kv_g, b_gate = row(wt['q_norm_g']), row(wt['kv_norm_g']), row(wt['b_gate'])
    wt = dict(wt)
    pending = []

    def sent():
        tokens = list(pending)
        pending.clear()
        return tokens

    def need(names, after):
        if arrive is not None:
            got = arrive(names, after)
            if '_token' in got:
                pending.append(got.pop('_token'))
            wt.update(got)
        return [wt[n] for n in names]
    lru_w_a = wt['lru_w_a'].reshape(2 * LRU_BLOCKS, LRU_BW, LRU_BW).astype(BF16)
    lru_w_x = wt['lru_w_x'].reshape(2 * LRU_BLOCKS, LRU_BW, LRU_BW).astype(BF16)
    b_a, b_x, lam = wt['lru_b_a'], wt['lru_b_x'], wt['lru_lambda']
    sp = jnp.logaddexp(-lam, 0.0)
    c_tab, s1_tab, s2_tab = _rope_tables(t_lat, n)
    rw = functools.partial(rowwise, n_rows=n, t_lat=t_lat, tm=tm)
    rw_lat = functools.partial(rowwise, n_rows=t_lat, t_lat=t_lat, tm=tm)

    stream = [(x, 0, D), (ctx, 0, D, 'ctx')]

    def f_norm1(is_ctx, rows, params):
        (xl, xc_), (g, sc, sh) = rows, params
        return [_norm_mod(jnp.where(is_ctx, xc_, xl), g, _sel(is_ctx, sc), _sel(is_ctx, sh))], []

    (h,), _ = rw("norm1", f_norm1, stream, [norm1_g, sc1, sh1], [(D, BF16)], [])
    (w_in_t,) = need(('w_in',), h)
    z = matmul("w_in", h, w_in_t, 'nt', BF16, after=sent())
    w_uq_t, w_ukv_t, w_o_lru = need(('w_uq', 'w_ukv', 'w_o_lru'), z)

    def f_qkv_norm(is_ctx, rows, params):
        (ql, kvl), (gq, gkv) = rows, params
        return [_rms(ql, gq), _rms(kvl, gkv)], []

    (qn, kvn), _ = rw("qkv_norm", f_qkv_norm, [(z, Z_Q, Q_RANK), (z, Z_KV, KV_RANK)], [q_g, kv_g],
                      [(Q_RANK, BF16), (KV_RANK, BF16)], [])
    qp = matmul("w_uq", qn, w_uq_t, 'nt', BF16)
    kvp = matmul("w_ukv", kvn, w_ukv_t, 'nt', BF16)

    def f_rope(is_ctx, rows, params):
        qv, kk, vv, kr, c, s1, s2 = rows
        krr = _rope(kr, c, s1, s2)
        qo = jnp.concatenate([_rope(qh, c, s1, s2) for qh in _heads(qv)], axis=1) * Q_SCALE
        ko = jnp.concatenate([kh + krr for kh in _heads(kk)], axis=1)
        return [qo, ko, vv], []

    hp = N_HEADS * HEAD_PAD
    (qr, kr_, vr), _ = rw("rope", f_rope,
                          [(qp, 0, hp), (kvp, 0, hp), (kvp, hp, hp), (z, Z_KR, HEAD_PAD), (c_tab, 0, HEAD_PAD),
                           (s1_tab, 0, HEAD_PAD), (s2_tab, 0, HEAD_PAD)], [], [(hp, BF16)] * 3, [])
    attn, lse = attn_fwd(qr, kr_, vr, t_lat, n, tq_fwd)

    xc = conv_fwd("lru_conv", z, Z_XB, LRU_W, wt['lru_conv_w'], row(wt['lru_conv_b']), 2, n, t_lat, F32)
    a_f, h_f, hp_f = lru_scan("lru_scan_f", xc, lru_w_a, lru_w_x, b_a, b_x, sp, 'f', n, t_lat)
    a_b, h_b, hp_b = lru_scan("lru_scan_b", xc, lru_w_a, lru_w_x, b_a, b_x, sp, 'b', n, t_lat)

    def f_lru_out(is_ctx, rows, params):
        hf, hb, yb = rows
        return [(hf + hb) * _gelu(yb)], []

    (ybin,), _ = rw_lat("lru_out", f_lru_out, [(h_f, 0, LRU_W), (h_b, 0, LRU_W), (z, Z_YB, LRU_W)], [],
                        [(LRU_W, BF16)], [])
    w_o_attn_t, w_out, w_up_t, w_down = need(('w_o_attn', 'w_out', 'w_up', 'w_down'), attn)
    y_a = matmul("w_o_attn", attn, w_o_attn_t, 'nt', BF16)
    y_b = matmul("w_o_lru", ybin, w_o_lru, 'nn', BF16)

    def _merge(ya, yb, gl, bg):
        gates = _sigmoid(gl + bg)
        return gates[:, :D] * ya + gates[:, D:] * yb

    def f_merge(is_ctx, rows, params):
        (ya, yb, gl), (bg,) = rows, params
        return [_merge(ya, yb, gl, bg)], []

    (mrg,), _ = rw_lat("merge", f_merge, [(y_a, 0, D), (y_b, 0, D), (z, Z_GL, 2 * D)], [b_gate], [(D, BF16)], [])
    o = matmul("w_out", mrg, w_out, 'nn', BF16)

    def _res_norm2(xv, ov, g1v, g, sc, sh):
        x1 = xv + g1v * ov
        return x1, _norm_mod(x1, g, sc, sh)

    def f_norm2(is_ctx, rows, params):
        (xv, ov), (g1v, g, sc, sh) = rows, params
        x1, h2v = _res_norm2(xv, ov, g1v, g, sc, sh)
        return [x1, h2v], []

    (x1, h2), _ = rw_lat("norm2", f_norm2, [(x, 0, D), (o, 0, D)], [g1, norm2_g, sc2, sh2], [(D, F32), (D, BF16)], [])
    u = matmul("w_up", h2, w_up_t, 'nt', BF16)
    f = ffn_mix_fwd(u, wt['ffn_conv_w'], row(wt['ffn_conv_b']), t_lat)
    dn = matmul("w_down", f, w_down, 'nn', BF16)

    def _tile_loss(x1v, dv, g2v, fg, tgt):
        y = _rms(x1v + g2v * dv, fg)
        e = y - tgt
        return 0.5 * jnp.sum(jnp.mean(e * e, axis=-1, keepdims=True), axis=0, keepdims=True)

    def f_final(is_ctx, rows, params):
        (x1v, dv, tgt), (g2v, fg) = rows, params
        lv, vjp = jax.vjp(lambda a, b, c, d: _tile_loss(a, b, c, d, tgt), x1v, dv, g2v, fg)
        dx2, dd, dg2, dfg = vjp(jnp.ones((1, 1), F32))
        return [dx2, dd], [dg2, dfg, jnp.broadcast_to(lv, (1, 128))]

    (dx2, dd), (dg2, dfinal_g, loss_v) = rw_lat("final", f_final, [(x1, 0, D), (dn, 0, D), (target, 0, D)],
                                                [g2, final_g], [(D, F32), (D, BF16)], [(1, D), (1, D), (1, 128)])
    loss = loss_v[0, 0]

    grads = {'final_g': dfinal_g}

    def put(name, g):
        grads[name] = g
        if on_grad is not None:
            pending.append(on_grad(name, g))
    df = matmul("d_f", dd, w_down, 'nt', BF16)
    put('w_down', matmul("g_w_down", f, dd, 'tn', BF16))

    du, grads['ffn_conv_w'], grads['ffn_conv_b'] = ffn_mix_bwd(u, df, wt['ffn_conv_w'], row(wt['ffn_conv_b']),
                                                               t_lat)
    dh2 = matmul("d_h2", du, w_up_t, 'nn', BF16, after=sent())
    put('w_up', matmul("g_w_up", du, h2, 'tn', BF16))

    def b_norm2(is_ctx, rows, params):
        (xv, ov, dh2v, dx2v), (g1v, g, sc, sh) = rows, params
        _, vjp = jax.vjp(_res_norm2, xv, ov, g1v, g, sc, sh)
        dx, do, dg1v, dg, dsc, dsh = vjp((dx2v, dh2v))
        return [dx, do], [dg1v, dg, dsc, dsh]

    (dx_res, do), (dg1, dnorm2_g, dsc2, dsh2) = rw_lat(
        "norm2_bwd", b_norm2, [(x, 0, D), (o, 0, D), (dh2, 0, D), (dx2, 0, D)], [g1, norm2_g, sc2, sh2],
        [(D, F32), (D, BF16)], [(1, D)] * 4)
    grads['norm2_g'] = dnorm2_g
    dmrg = matmul("d_merge", do, w_out, 'nt', BF16, after=sent())
    put('w_out', matmul("g_w_out", mrg, do, 'tn', BF16))

    def b_merge(is_ctx, rows, params):
        (ya, yb, gl, dm), (bg,) = rows, params
        _, vjp = jax.vjp(_merge, ya, yb, gl, bg)
        dya, dyb, dgl, dbg = vjp(dm)
        return [dya, dyb, dgl], [dbg]

    (dy_a, dy_b, dgl), (grads['b_gate'],) = rw_lat(
        "merge_bwd", b_merge, [(y_a, 0, D), (y_b, 0, D), (z, Z_GL, 2 * D), (dmrg, 0, D)], [b_gate],
        [(D, BF16), (D, BF16), (2 * D, BF16)], [(1, 2 * D)])
    dattn = matmul("d_attn", dy_a, w_o_attn_t, 'nn', BF16, after=sent())
    put('w_o_attn', matmul("g_w_o_attn", dy_a, attn, 'tn', BF16))
    dybin = matmul("d_lru_out", dy_b, w_o_lru, 'nt', BF16, after=sent())
    put('w_o_lru', matmul("g_w_o_lru", ybin, dy_b, 'tn', BF16))

    def b_lru_out(is_ctx, rows, params):
        hf, hb, yb, dyv = rows
        _, vjp = jax.vjp(lambda s, y: s * _gelu(y), hf + hb, yb)
        dh, dyb = vjp(dyv)
        return [dh, dyb], []

    (dh_lru, dyb), _ = rw_lat("lru_out_bwd", b_lru_out,
                              [(h_f, 0, LRU_W), (h_b, 0, LRU_W), (z, Z_YB, LRU_W), (dybin, 0, LRU_W)], [],
                              [(LRU_W, F32), (LRU_W, BF16)], [])
    gate_params = (lru_w_a, lru_w_x, b_a, b_x, sp)
    dxc_f, *sums_f = lru_scan_bwd("lru_scan_f_bwd", xc, a_f, dh_lru, hp_f, None, *gate_params, 'f', n, t_lat)
    dxc, *sums_b = lru_scan_bwd("lru_scan_b_bwd", xc, a_b, dh_lru, hp_b, dxc_f, *gate_params, 'b', n, t_lat)
    dw_a, dw_x, db_a, db_x, dsp = (jnp.concatenate([f_, b_], axis=0) for f_, b_ in zip(sums_f, sums_b))
    put('lru_w_a', dw_a.reshape(2 * LRU_BLOCKS * LRU_BW, LRU_BW).astype(BF16))
    put('lru_w_x', dw_x.reshape(2 * LRU_BLOCKS * LRU_BW, LRU_BW).astype(BF16))
    grads['lru_b_a'], grads['lru_b_x'] = db_a, db_x
    grads['lru_lambda'] = -dsp * _sigmoid(-lam)
    dxb, grads['lru_conv_w'], grads['lru_conv_b'] = conv_bwd("lru_conv_bwd", dxc, z, Z_XB, LRU_W, wt['lru_conv_w'],
                                                             2, n, t_lat)

    dq, dk, dv = attn_bwd(qr, kr_, vr, attn, dattn, lse, t_lat, n, tq_bwd)

    def b_rope(is_ctx, rows, params):
        dqv, dkv, dvv, c, s1, s2 = rows
        live = jnp.where(is_ctx, 0.0, 1.0)
        dqo = jnp.concatenate([_rope_t(dqh, c, s1, s2) for dqh in _heads(dqv)], axis=1) * live
        dkh = _heads(dkv)
        dkr = dkh[0]
        for t in dkh[1:]:
            dkr = dkr + t
        lanes = lax.broadcasted_iota(jnp.int32, dkr.shape, 1)
        dkr = jnp.where((lanes >= QK_NOPE) & (lanes < QK_DIM), _rope_t(dkr, c, s1, s2), 0.0)
        return [dqo, jnp.concatenate([dkv, dvv], axis=1), dkr], []

    (dqp, dkvp, dkr), _ = rw("rope_bwd", b_rope,
                             [(dq, 0, hp), (dk, 0, hp), (dv, 0, hp), (c_tab, 0, HEAD_PAD), (s1_tab, 0, HEAD_PAD),
                              (s2_tab, 0, HEAD_PAD)], [], [(hp, BF16), (2 * hp, BF16), (HEAD_PAD, BF16)], [])
    dqn = matmul("d_qn", dqp, w_uq_t, 'nn', BF16, after=sent())
    put('w_uq', matmul("g_w_uq", dqp, qn, 'tn', BF16))
    dkvn = matmul("d_kvn", dkvp, w_ukv_t, 'nn', BF16, after=sent())
    put('w_ukv', matmul("g_w_ukv", dkvp, kvn, 'tn', BF16))

    def b_qkv_norm(is_ctx, rows, params):
        (ql, kvl, dqv, dkvv), (gq, gkv) = rows, params
        _, vjp_q = jax.vjp(_rms, ql, gq)
        _, vjp_kv = jax.vjp(_rms, kvl, gkv)
        dql, dgq = vjp_q(dqv)
        dkvl, dgkv = vjp_kv(dkvv)
        return [dql, dkvl], [dgq, dgkv]

    (dq_lat, dkv_lat), (grads['q_norm_g'], grads['kv_norm_g']) = rw(
        "qkv_norm_bwd", b_qkv_norm, [(z, Z_Q, Q_RANK), (z, Z_KV, KV_RANK), (dqn, 0, Q_RANK), (dkvn, 0, KV_RANK)],
        [q_g, kv_g], [(Q_RANK, BF16), (KV_RANK, BF16)], [(1, Q_RANK), (1, KV_RANK)])
    pad_ctx = lambda t: jnp.pad(t, ((0, n_ctx), (0, 0)))
    dz = jnp.concatenate([dq_lat, dkv_lat, dkr, dxb, pad_ctx(dyb), pad_ctx(dgl)], axis=1)
    put('w_in', matmul("g_w_in", dz, h, 'tn', BF16))
    dh = matmul("d_h", dz, w_in_t, 'nn', BF16, after=sent())

    def b_norm1(is_ctx, rows, params):
        (xl, xc_, dhv, dxr), (g, sc, sh) = rows, params
        scv, shv = _sel(is_ctx, sc), _sel(is_ctx, sh)
        _, vjp = jax.vjp(_norm_mod, jnp.where(is_ctx, xc_, xl), g, scv, shv)
        dx, dg, dsc, dsh = vjp(dhv)
        return [dx + dxr], [dg, _seg_acc(is_ctx, dsc), _seg_acc(is_ctx, dsh)]

    (grad_x,), (grads['norm1_g'], dsc1, dsh1) = rw("norm1_bwd", b_norm1, stream + [(dh, 0, D), (dx_res, 0, D)],
                                                   [norm1_g, sc1, sh1], [(D, F32, 'lat')],
                                                   [(1, D), (2, D), (2, D)])
    zero = jnp.zeros((D,), F32)
    dmod_l = jnp.concatenate([dsh1[0], dsc1[0], dg1[0], dsh2[0], dsc2[0], dg2[0]])
    dmod_c = jnp.concatenate([dsh1[1], dsc1[1], zero, zero, zero, zero])
    return loss, grad_x, grads, dmod_l, dmod_c


def kernel(x, c, ctx, c_ctx, w_mod, b_mod, norm1_g, w_in, b_gate, q_norm_g, kv_norm_g, w_uq, w_ukv, w_o_attn, lru_conv_w, lru_conv_b, lru_w_a, lru_b_a, lru_w_x, lru_b_x, lru_lambda, w_o_lru, w_out, norm2_g, w_up, ffn_conv_w, ffn_conv_b, w_down, final_g, loss_target, m_c_ctx, m_w_mod, m_b_mod, m_norm1_g, m_w_in, m_b_gate, m_q_norm_g, m_kv_norm_g, m_w_uq, m_w_ukv, m_w_o_attn, m_lru_conv_w, m_lru_conv_b, m_lru_w_a, m_lru_b_a, m_lru_w_x, m_lru_b_x, m_lru_lambda, m_w_o_lru, m_w_out, m_norm2_g, m_w_up, m_ffn_conv_w, m_ffn_conv_b, m_w_down, m_final_g, v_c_ctx, v_w_mod, v_b_mod, v_norm1_g, v_w_in, v_b_gate, v_q_norm_g, v_kv_norm_g, v_w_uq, v_w_ukv, v_w_o_attn, v_lru_conv_w, v_lru_conv_b, v_lru_w_a, v_lru_b_a, v_lru_w_x, v_lru_b_x, v_lru_lambda, v_w_o_lru, v_w_out, v_norm2_g, v_w_up, v_ffn_conv_w, v_ffn_conv_b, v_w_down, v_final_g):
    given = dict(locals())
    strip = lambda name, a: a if name in ('c_ctx', 'final_g') else a[0]
    wsh = {n: strip(n, given[n]) for n in WEIGHTS}
    msh = {n: strip(n, given['m_' + n]) for n in WEIGHTS}
    vsh = {n: strip(n, given['v_' + n]) for n in WEIGHTS}
    me = _my_index()

    small = _flat([c[0]] + [wsh[n] for n in SMALL_F32], F32, 8)
    small_all = all_gather("gather_small", small).reshape(N_DEV, -1)
    c_all = small_all[:, :D]
    full, at = {}, D
    for n in SMALL_F32:
        cnt = math.prod(wsh[n].shape)
        full[n] = _gathered_to_full(n, small_all[:, at:at + cnt].reshape((N_DEV,) + wsh[n].shape))
        at += cnt

    cond = jnp.concatenate([c_all, c_ctx[None], jnp.zeros((7, D), F32)], axis=0)
    sil = cond * jax.nn.sigmoid(cond)
    mod_cols = matmul("mod_proj", sil, wsh['w_mod'], 'nn', F32)
    mod_all = all_gather("gather_mod", mod_cols)
    mod_all = jnp.transpose(mod_all, (1, 0, 2)).reshape(16, 6 * D) + b_mod[0][None]
    mod_l = lax.dynamic_index_in_dim(mod_all, me, axis=0, keepdims=False)
    mod_c = mod_all[N_DEV]

    rb_shards = {n: _shard_to_rb(n, wsh[n]).astype(BF16) for n in BIG_BF16}
    (w_in_blocks,) = all_gather_multi("gather_w_in", [rb_shards['w_in']])
    later = [n for n in BIG_BF16 if n != 'w_in']
    weights_started, weights_sent = exchange_start("weights_send", 'gather', [rb_shards[n] for n in later],
                                                   after=[w_in_blocks, mod_all])
    for n in REPLICATED:
        if n not in ('c_ctx', 'b_mod'):
            full[n] = wsh[n]

    def arrive(names, after):
        if names == ('w_in',):
            return {'w_in': _rb_from_gathered('w_in', w_in_blocks), '_token': weights_sent}
        picked = [later.index(n) for n in names]
        lands = exchange_wait("weights_wait_" + names[0], 'gather',
                              tuple([part[i] for i in picked] for part in weights_started), after)
        return {n: _rb_from_gathered(n, lax.dynamic_update_slice_in_dim(land, rb_shards[n][None], me, axis=0))
                for n, land in zip(names, lands)}

    in_flight = {}

    def on_grad(n, g):
        chunks = _chunks_from_rb_grad(n, g)
        own = lax.dynamic_index_in_dim(chunks, me, axis=0, keepdims=True)
        started, token = exchange_start("grad_send_" + n, 'scatter', [chunks])
        in_flight[n] = (own, started)
        return token

    loss, grad_x, grads, dmod_l, dmod_c = local_step(x[0], ctx[0], loss_target[0], mod_l, mod_c, full, on_grad,
                                                     arrive)
    dmod = jnp.stack([dmod_l, dmod_c]).reshape(2 * 6 * D // FLAT_C, FLAT_C)
    dm = all_gather("gather_dmod", dmod).reshape(N_DEV, 2, 6 * D)
    dmod_c_tot = dm[0, 1]
    for p in range(1, N_DEV):
        dmod_c_tot = dmod_c_tot + dm[p, 1]
    dm16 = jnp.concatenate([dm[:, 0], dmod_c_tot[None], jnp.zeros((7, 6 * D), F32)], axis=0)
    ncol = 6 * D // N_DEV
    dm16_cols = lax.dynamic_slice_in_dim(dm16.reshape(16, N_DEV, ncol), me, 1, axis=1)[:, 0]
    grad_w_mod = matmul("g_w_mod", sil, dm16_cols, 'tn', F32)
    dsil = matmul("d_cond", dm16_cols, wsh['w_mod'], 'nt', F32)
    sg = jax.nn.sigmoid(c_ctx)
    grads['c_ctx'] = dsil[N_DEV] * (sg * (1.0 + c_ctx * (1.0 - sg)))
    grads['b_mod'] = dmod_l + dmod_c

    g_final = {'w_mod': grad_w_mod}
    reduced, stepped = {}, {}
    for n in BIG_BF16 + ['lru_w_a', 'lru_w_x']:
        own, started = in_flight[n]
        (land,) = exchange_wait("grad_wait_" + n, 'scatter', started, dm)
        if n in ROW_SHARDED:
            g_final[n], *stepped[n] = reduce_slots("step_" + n, land, own, (wsh[n], msh[n], vsh[n]))
        elif n in COL_SHARDED and wsh[n].shape[1] % 128:
            g_t, *outs = reduce_slots("step_" + n, land, own, (wsh[n].T, msh[n].T, vsh[n].T))
            g_final[n], stepped[n] = g_t.T, [o.T for o in outs]
        else:
            reduced[n] = reduce_slots("sum_" + n, land, own)
            if n in BIG_BF16:
                g_final[n] = _rb_to_shard(n, reduced[n])

    small_names = SMALL_F32 + [n for n in REPLICATED if n not in ('lru_w_a', 'lru_w_x')]
    partials = _flat([grads[n] for n in small_names] + [loss], F32, 8)
    parts_all, a_all, x_all = all_gather_multi("gather_small_grads", [partials, reduced['lru_w_a'], reduced['lru_w_x']])
    small_sum = sum_slots("sum_small", parts_all).reshape(-1)
    g_final['lru_w_a'], g_final['lru_w_x'] = a_all.reshape(wsh['lru_w_a'].shape), x_all.reshape(wsh['lru_w_x'].shape)
    at = 0
    for n in small_names:
        cnt = math.prod(full[n].shape) if n in SMALL_F32 else math.prod(wsh[n].shape)
        g = small_sum[at:at + cnt]
        if n in SMALL_F32:
            k = full[n].shape[0]
            g = lax.dynamic_index_in_dim(g.reshape(k, N_DEV, -1), me, axis=1, keepdims=False)
        g_final[n] = g.reshape(wsh[n].shape)
        at += cnt
    loss = small_sum[at]

    for n in ['w_mod'] + BIG_BF16:
        if n not in stepped:
            stepped[n] = adamw("adamw_" + n, wsh[n], g_final[n], msh[n], vsh[n])
    rest = [n for n in WEIGHTS if n not in stepped]
    as2d = lambda a: a.reshape(-1, a.shape[-1])
    rest_out = adamw_many("adamw_small", *[[as2d(d[n]) for n in rest] for d in (wsh, g_final, msh, vsh)])
    stepped.update(zip(rest, rest_out))
    shaped = lambda n, a: a.reshape(given[n].shape)
    return (loss, grad_x[None],
            *[shaped(n, g_final[n]) for n in WEIGHTS],
            *[shaped(n, stepped[n][k]) for k in range(3) for n in WEIGHTS])
```

```python
import functools
import math

import jax
import jax.numpy as jnp
from jax import lax
from jax.experimental import pallas as pl
from jax.experimental.pallas import tpu as pltpu

F32 = jnp.float32
BF16 = jnp.bfloat16
MESH = pl.DeviceIdType.MESH

N_DEV = 8
D = 1024
N_HEADS = 8
HEAD_PAD = 128
QK_NOPE, QK_ROPE, V_HEAD = 64, 32, 64
QK_DIM = QK_NOPE + QK_ROPE
Q_RANK, KV_RANK = 384, 256
LRU_W, LRU_BLOCKS, LRU_BW = 1280, 10, 128
FFN = 2816
GRID_W = 64
ROPE_BASE = 10000.0
LRU_C = 8.0
EPS = 1e-6
Z_Q, Z_KV, Z_KR, Z_XB, Z_YB, Z_GL, Z_END = 0, 384, 640, 768, 2048, 3328, 5376
ADAM_LR, ADAM_B1, ADAM_B2, ADAM_EPS, ADAM_WD, ADAM_STEP = 0.001, 0.9, 0.999, 1e-08, 0.01, 10

VMEM_LIMIT = 52 * 1024 * 1024
FLAT_C = 512

WEIGHTS = ['c_ctx', 'w_mod', 'b_mod', 'norm1_g', 'w_in', 'b_gate', 'q_norm_g', 'kv_norm_g', 'w_uq', 'w_ukv',
           'w_o_attn', 'lru_conv_w', 'lru_conv_b', 'lru_w_a', 'lru_b_a', 'lru_w_x', 'lru_b_x', 'lru_lambda',
           'w_o_lru', 'w_out', 'norm2_g', 'w_up', 'ffn_conv_w', 'ffn_conv_b', 'w_down', 'final_g']
COL_SHARDED = ['w_in', 'w_uq', 'w_ukv', 'w_o_attn', 'lru_conv_w', 'lru_b_a', 'lru_b_x', 'lru_lambda', 'w_up',
               'ffn_conv_w']
ROW_SHARDED = ['w_o_lru', 'w_out', 'w_down']
BIG_BF16 = ['w_in', 'w_uq', 'w_ukv', 'w_o_attn', 'w_o_lru', 'w_out', 'w_up', 'w_down']
SMALL_F32 = ['lru_conv_w', 'lru_b_a', 'lru_b_x', 'lru_lambda', 'ffn_conv_w']
REPLICATED = ['c_ctx', 'b_mod', 'norm1_g', 'b_gate', 'q_norm_g', 'kv_norm_g', 'lru_conv_b', 'lru_w_a', 'lru_w_x',
              'norm2_g', 'ffn_conv_b', 'final_g']


def _cparams(sem=None):
    return pltpu.CompilerParams(dimension_semantics=sem, vmem_limit_bytes=VMEM_LIMIT)


def _pick(n, cands):
    for c in cands:
        if c <= n and n % c == 0:
            return c
    return n


def _best_div(n, mult, cap):
    best = mult
    for d in range(mult, min(n, cap) + 1, mult):
        if n % d == 0:
            best = d
    return best


MXU_DIM = 256
ROW_TILES = (1088, 1024, 544, 512, 256, 128, 64, 32, 16, 8)
LANE_TILES = (2816, 1792, 1536, 1280, 1024, 768, 512, 256, 1408, 896, 640, 384, 128)
DEPTH_ROW_TILES = (2176, 2048, 1024, 512, 256, 1088, 128, 64, 32, 16, 8)
MATMUL_VMEM_BUDGET = 40 * 1024 * 1024
MXU_FILL_OK = 0.9


def _my_pos():
    return lax.axis_index("x"), lax.axis_index("y"), lax.axis_index("c")


def _my_index():
    x, y, c = _my_pos()
    return 4 * x + 2 * y + c


def all_gather_multi(name, shards):
    n_arr = len(shards)
    arrays = range(n_arr)

    def body(*refs):
        x_refs, out_refs = refs[:n_arr], refs[n_arr:2 * n_arr]
        send_sems, recv_sems, local_sems = refs[2 * n_arr:]
        x, y, c = _my_pos()
        me, sibling = (x, y, c), (x, y, 1 - c)
        chips = [(1 - x, y), (x, 1 - y), (1 - x, 1 - y)]

        def slot(a, px, py, pc):
            return out_refs[a].at[4 * px + 2 * py + pc]

        def copy(a, k, block, to, src=None):
            return pltpu.make_async_remote_copy(
                src_ref=slot(a, *block) if src is None else src, dst_ref=slot(a, *block),
                send_sem=send_sems.at[7 * a + k], recv_sem=recv_sems.at[7 * a + k], device_id=to,
                device_id_type=MESH)

        mine = [pltpu.make_async_copy(x_refs[a], slot(a, *me), local_sems.at[a]) for a in arrays]
        first = [copy(a, 1 + j, me, (*chip, c), src=x_refs[a]) for j, chip in enumerate(chips) for a in arrays]
        first += [copy(a, 0, me, sibling, src=x_refs[a]) for a in arrays]
        for cp in first + mine:
            cp.start()
        passed = []
        for j, chip in enumerate(chips):
            for a in arrays:
                copy(a, 1 + j, (*chip, c), me).wait_recv()
                passed.append(copy(a, 4 + j, (*chip, c), sibling))
                passed[-1].start()
        for a in arrays:
            copy(a, 0, sibling, me).wait_recv()
            for j, chip in enumerate(chips):
                copy(a, 4 + j, (*chip, 1 - c), me).wait_recv()
        for cp in first + passed:
            cp.wait_send()
        for cp in mine:
            cp.wait()

    hbm = pl.BlockSpec(memory_space=pl.ANY)
    return pl.pallas_call(
        body, name=name,
        out_shape=[jax.ShapeDtypeStruct((N_DEV,) + s.shape, s.dtype) for s in shards],
        in_specs=[hbm] * n_arr, out_specs=[hbm] * n_arr,
        scratch_shapes=[pltpu.SemaphoreType.DMA((7 * n_arr,)), pltpu.SemaphoreType.DMA((7 * n_arr,)),
                        pltpu.SemaphoreType.DMA((n_arr,))],
    )(*shards)


def all_gather(name, shard):
    return all_gather_multi(name, [shard])[0]


def _peers():
    x, y, c = _my_pos()
    out = []
    for rel in (6, 4, 2, 7, 5, 3, 1):
        px, py, pc = x ^ ((rel >> 2) & 1), y ^ ((rel >> 1) & 1), c ^ (rel & 1)
        out.append((rel - 1, (px, py, pc), 4 * px + 2 * py + pc))
    return out


def _exchange_copies(mode, src_refs, land_refs, send_sems, recv_sems, with_arrivals):
    x, y, c = _my_pos()
    me = 4 * x + 2 * y + c
    sends, arrivals = [], []
    for k, peer_pos, peer in _peers():
        for a, (src, land) in enumerate(zip(src_refs, land_refs)):
            piece = src.at[peer] if mode == 'scatter' else src
            sems = dict(send_sem=send_sems[a].at[k], recv_sem=recv_sems[a].at[k], device_id_type=MESH)
            sends.append(pltpu.make_async_remote_copy(src_ref=piece, dst_ref=land.at[me], device_id=peer_pos, **sems))
            if with_arrivals:
                arrivals.append(pltpu.make_async_remote_copy(src_ref=piece, dst_ref=land.at[peer],
                                                             device_id=(x, y, c), **sems))
    return sends, arrivals


_HBM = pl.BlockSpec(memory_space=pltpu.HBM)
_SEM = pl.BlockSpec(memory_space=pltpu.SEMAPHORE)


def exchange_start(name, mode, arrays, after=()):
    n_arr, n_after = len(arrays), len(after)
    land_shapes = [a.shape if mode == 'scatter' else (N_DEV,) + a.shape for a in arrays]

    def body(*refs):
        src_refs, land_refs = refs[:n_arr], refs[n_arr:2 * n_arr]
        refs = refs[n_after:]
        send_sems, recv_sems = refs[2 * n_arr:3 * n_arr], refs[3 * n_arr:4 * n_arr]
        sends, _ = _exchange_copies(mode, src_refs, land_refs, send_sems, recv_sems, with_arrivals=False)
        for cp in sends:
            cp.start()
        token = refs[-1]
        token[...] = jnp.zeros_like(token)

    sem = pltpu.SemaphoreType.DMA((N_DEV - 1,))
    res = pl.pallas_call(
        body, name=name,
        out_shape=[sem] * (2 * n_arr) + [pltpu.HBM(a.shape, a.dtype) for a in arrays]
        + [pltpu.HBM(s, a.dtype) for s, a in zip(land_shapes, arrays)] + [jax.ShapeDtypeStruct((8, 128), F32)],
        in_specs=[_HBM] * (2 * n_arr) + [pl.BlockSpec(memory_space=pl.ANY)] * n_after,
        out_specs=[_SEM] * (2 * n_arr) + [_HBM] * (2 * n_arr) + [pl.BlockSpec(memory_space=pltpu.VMEM)],
        input_output_aliases={i: 2 * n_arr + i for i in range(2 * n_arr)},
        compiler_params=pltpu.CompilerParams(has_side_effects=pltpu.SideEffectType.DATAFLOW_SIDE_EFFECTING),
    )(*[pltpu.with_memory_space_constraint(a, pltpu.HBM) for a in arrays],
      *[pltpu.with_memory_space_constraint(lax.empty(s, a.dtype), pltpu.HBM) for s, a in zip(land_shapes, arrays)],
      *after)
    return (res[:n_arr], res[n_arr:2 * n_arr], res[2 * n_arr:3 * n_arr], res[3 * n_arr:4 * n_arr]), res[-1]


def exchange_wait(name, mode, started, after):
    send_sems, recv_sems, thru, land = started
    n_arr = len(thru)

    def body(*refs):
        src_refs, land_refs = refs[:n_arr], refs[n_arr:2 * n_arr]
        s_sems, r_sems = refs[2 * n_arr:3 * n_arr], refs[3 * n_arr:4 * n_arr]
        sends, arrivals = _exchange_copies(mode, src_refs, land_refs, s_sems, r_sems, with_arrivals=True)
        for cp in sends:
            cp.wait_send()
        for cp in arrivals:
            cp.wait_recv()

    res = pl.pallas_call(
        body, name=name,
        out_shape=[pltpu.HBM(a.shape, a.dtype) for a in thru] + [pltpu.HBM(a.shape, a.dtype) for a in land],
        in_specs=[_HBM] * (2 * n_arr) + [_SEM] * (2 * n_arr) + [pl.BlockSpec(memory_space=pl.ANY)],
        out_specs=[_HBM] * (2 * n_arr),
        input_output_aliases={i: i for i in range(2 * n_arr)},
        compiler_params=pltpu.CompilerParams(has_side_effects=pltpu.SideEffectType.DATAFLOW_SIDE_EFFECTING),
    )(*thru, *land, *send_sems, *recv_sems, after)
    return res[n_arr:]


def _sum_with_own(slot_ref, own_ref):
    x, y, c = _my_pos()
    me = 4 * x + 2 * y + c
    acc = None
    for p in range(N_DEV):
        v = jnp.where(me == p, own_ref[0], slot_ref[p]).astype(F32)
        acc = v if acc is None else acc + v
    return acc


def reduce_slots(name, slots, own, step=None):
    _, r, ccols = slots.shape
    tc = _pick(ccols, (256, 128))
    c1 = 1.0 - ADAM_B1 ** ADAM_STEP
    c2 = 1.0 - ADAM_B2 ** ADAM_STEP

    def body(s_ref, own_ref, *refs):
        g = _sum_with_own(s_ref, own_ref)
        if step is None:
            refs[0][...] = g
            return
        w_ref, m_ref, v_ref, g_ref, d_ref, nm_ref, nv_ref = refs
        nm = ADAM_B1 * m_ref[...] + (1.0 - ADAM_B1) * g
        nv = ADAM_B2 * v_ref[...] + (1.0 - ADAM_B2) * (g * g)
        g_ref[...] = g
        d_ref[...] = -ADAM_LR * ((nm / c1) / (jnp.sqrt(nv / c2) + ADAM_EPS) + ADAM_WD * w_ref[...])
        nm_ref[...] = nm
        nv_ref[...] = nv

    col = pl.BlockSpec((r, tc), lambda j: (0, j))
    n_out = 1 if step is None else 4
    res = pl.pallas_call(
        body, name=name, grid=(ccols // tc,),
        out_shape=[jax.ShapeDtypeStruct((r, ccols), F32)] * n_out,
        in_specs=[pl.BlockSpec((N_DEV, r, tc), lambda j: (0, 0, j)), pl.BlockSpec((1, r, tc), lambda j: (0, 0, j))]
        + [col] * (0 if step is None else 3),
        out_specs=[col] * n_out,
        compiler_params=_cparams(("parallel",)),
    )(slots, own, *(step or ()))
    return res[0] if step is None else res


def sum_slots(name, slots):
    _, r, ccols = slots.shape
    tc = _pick(ccols, (256, 128))

    def body(s_ref, o_ref):
        acc = s_ref[0].astype(F32)
        for p in range(1, N_DEV):
            acc = acc + s_ref[p].astype(F32)
        o_ref[...] = acc

    return pl.pallas_call(
        body, name=name, grid=(ccols // tc,),
        out_shape=jax.ShapeDtypeStruct((r, ccols), F32),
        in_specs=[pl.BlockSpec((N_DEV, r, tc), lambda j: (0, 0, j))],
        out_specs=pl.BlockSpec((r, tc), lambda j: (0, j)),
        compiler_params=_cparams(("parallel",)),
    )(slots)


def _mxu_fill(t):
    return t / (-(-t // MXU_DIM) * MXU_DIM)


def _matmul_tiles(mode, m_extent, n, k_extent, k_total, itemsizes):
    a_bytes, b_bytes, o_bytes = itemsizes
    m_cands = [c for c in (LANE_TILES if mode == 'tn' else ROW_TILES) if m_extent % c == 0] or [m_extent]
    k_cands = [c for c in (DEPTH_ROW_TILES if mode == 'tn' else LANE_TILES) if k_extent % c == 0] or [k_extent]
    n_cands = [c for c in LANE_TILES if n % c == 0] or [n]
    best = None
    for tm in m_cands:
        for tk in k_cands:
            for tn in n_cands:
                f32_tiles = 2 if k_total // tk > 1 else 1
                vmem = 2 * (tm * tk * a_bytes + tk * tn * b_bytes + tm * tn * o_bytes) + tm * tn * 4 * f32_tiles
                if vmem > MATMUL_VMEM_BUDGET:
                    continue
                key = (_mxu_fill(tn) * _mxu_fill(tk) >= MXU_FILL_OK, tm * tn * tk)
                if best is None or key > best[0]:
                    best = (key, (tm, tn, tk))
    assert best is not None, (mode, m_extent, n, k_extent)
    return best[1]


def matmul(name, a, b, mode, out_dtype, after=()):
    after = [t for t in after if t is not None]
    pieces, a_rows, a_cols = (1,) + a.shape if a.ndim == 2 else a.shape
    if mode == 'nn':
        (m, k), (k2, n) = (a_rows, pieces * a_cols), b.shape
    elif mode == 'nt':
        (m, k), (n, k2) = (a_rows, pieces * a_cols), b.shape
    else:
        (k, m), (k2, n) = (a_rows, pieces * a_cols), b.shape
    assert k == k2, (name, a.shape, b.shape, mode)
    tm, tn, tk = _matmul_tiles(mode, a_cols if mode == 'tn' else m, n, k if mode == 'tn' else a_cols, k,
                               (a.dtype.itemsize, b.dtype.itemsize, jnp.dtype(out_dtype).itemsize))
    nk = k // tk
    per_piece = a_cols // (tm if mode == 'tn' else tk)
    if a.ndim == 2:
        a_block = lambda rows, cols, at: pl.BlockSpec((rows, cols), at)
    else:
        a_block = lambda rows, cols, at: pl.BlockSpec(
            (None, rows, cols), lambda i, j, kk: (at(i, j, kk)[1] // per_piece, at(i, j, kk)[0],
                                                  at(i, j, kk)[1] % per_piece))
    if mode == 'nn':
        a_spec = a_block(tm, tk, lambda i, j, kk: (i, kk))
        b_spec = pl.BlockSpec((tk, tn), lambda i, j, kk: (kk, j))
        dn = (((1,), (0,)), ((), ()))
    elif mode == 'nt':
        a_spec = a_block(tm, tk, lambda i, j, kk: (i, kk))
        b_spec = pl.BlockSpec((tn, tk), lambda i, j, kk: (j, kk))
        dn = (((1,), (1,)), ((), ()))
    else:
        a_spec = a_block(tk, tm, lambda i, j, kk: (kk, i))
        b_spec = pl.BlockSpec((tk, tn), lambda i, j, kk: (kk, j))
        dn = (((0,), (0,)), ((), ()))

    def product(a_ref, b_ref):
        return lax.dot_general(a_ref[...].astype(BF16), b_ref[...].astype(BF16), dn, preferred_element_type=F32)

    n_after = len(after)

    def body_one(a_ref, b_ref, *rest):
        o_ref = rest[n_after]
        o_ref[...] = product(a_ref, b_ref).astype(o_ref.dtype)

    def body(a_ref, b_ref, *rest):
        o_ref, acc_ref = rest[n_after:]
        kk = pl.program_id(2)

        @pl.when(kk == 0)
        def _():
            acc_ref[...] = jnp.zeros_like(acc_ref)

        acc_ref[...] += product(a_ref, b_ref)

        @pl.when(kk == nk - 1)
        def _():
            o_ref[...] = acc_ref[...].astype(o_ref.dtype)

    return pl.pallas_call(
        body_one if nk == 1 else body, name=name, grid=(m // tm, n // tn, nk),
        out_shape=jax.ShapeDtypeStruct((m, n), out_dtype),
        in_specs=[a_spec, b_spec] + [pl.BlockSpec(memory_space=pl.ANY)] * n_after,
        out_specs=pl.BlockSpec((tm, tn), lambda i, j, kk: (i, j)),
        scratch_shapes=[] if nk == 1 else [pltpu.VMEM((tm, tn), F32)],
        compiler_params=_cparams(("parallel", "parallel", "arbitrary")),
    )(a, b, *after)


def rowwise(name, fn, rows, params, out_rows, out_accs, n_rows, t_lat, tm):
    nb, nbl = n_rows // tm, t_lat // tm
    in_specs, piece_counts = [], []
    operands = []
    for arr, off, width, *kind in rows:
        g = math.gcd(off, width) if off else width
        assert g % 128 == 0 or (off == 0 and width == arr.shape[1]), (name, off, width)
        cnt = width // g
        last = arr.shape[0] // tm - 1
        clamp = arr.shape[0] < n_rows
        for p in range(cnt):
            cb = off // g + p
            if kind == ['ctx']:
                in_specs.append(pl.BlockSpec(
                    (tm, g), lambda i, cb=cb, last=last: (jnp.clip(i - nbl, 0, last), cb)))
            elif clamp:
                in_specs.append(pl.BlockSpec((tm, g), lambda i, cb=cb, last=last: (jnp.minimum(i, last), cb)))
            else:
                in_specs.append(pl.BlockSpec((tm, g), lambda i, cb=cb: (i, cb)))
            operands.append(arr)
        piece_counts.append(cnt)
    for p in params:
        in_specs.append(pl.BlockSpec(p.shape, lambda i, nd=p.ndim: (0,) * nd))
        operands.append(p)
    n_in = sum(piece_counts)
    n_par = len(params)
    n_or = len(out_rows)
    lat_only = [kind == ['lat'] for _, _, *kind in out_rows]
    out_shape = [jax.ShapeDtypeStruct((t_lat if lat else n_rows, w), dt)
                 for (w, dt, *_), lat in zip(out_rows, lat_only)]
    out_shape += [jax.ShapeDtypeStruct(s, F32) for s in out_accs]
    out_specs = [pl.BlockSpec((tm, w), (lambda i: (jnp.minimum(i, nbl - 1), 0)) if lat else (lambda i: (i, 0)))
                 for (w, *_), lat in zip(out_rows, lat_only)]
    out_specs += [pl.BlockSpec(s, lambda i, nd=len(s): (0,) * nd) for s in out_accs]

    def body(*refs):
        in_refs, par_refs = refs[:n_in], refs[n_in:n_in + n_par]
        orow_refs = refs[n_in + n_par:n_in + n_par + n_or]
        oacc_refs = refs[n_in + n_par + n_or:]
        i = pl.program_id(0)
        tiles, at = [], 0
        for cnt in piece_counts:
            parts = [in_refs[at + p][...].astype(F32) for p in range(cnt)]
            tiles.append(parts[0] if cnt == 1 else jnp.concatenate(parts, axis=1))
            at += cnt
        is_ctx = i * tm >= t_lat
        outs, accs = fn(is_ctx, tiles, [p[...] for p in par_refs])
        for o_ref, o, lat in zip(orow_refs, outs, lat_only):
            if lat:
                @pl.when(jnp.logical_not(is_ctx))
                def _(o_ref=o_ref, o=o):
                    o_ref[...] = o.astype(o_ref.dtype)
            else:
                o_ref[...] = o.astype(o_ref.dtype)
        if oacc_refs:
            @pl.when(i == 0)
            def _():
                for a_ref in oacc_refs:
                    a_ref[...] = jnp.zeros_like(a_ref)
            for a_ref, a in zip(oacc_refs, accs):
                a_ref[...] += a.astype(F32)

    res = pl.pallas_call(
        body, name=name, grid=(nb,),
        out_shape=out_shape, in_specs=in_specs, out_specs=out_specs,
        compiler_params=_cparams(("arbitrary",)),
    )(*operands)
    return res[:n_or], res[n_or:]


def _rms(x, g):
    return x * lax.rsqrt(jnp.mean(x * x, axis=-1, keepdims=True) + EPS) * g


def _norm_mod(x, g, sc, sh):
    return _rms(x, g) * (1.0 + sc) + sh


def _sigmoid(x):
    return 0.5 * jnp.tanh(0.5 * x) + 0.5


def _silu(x):
    return x * _sigmoid(x)


def _gelu(x):
    return 0.5 * x * (1.0 + jnp.tanh(math.sqrt(2.0 / math.pi) * (x + 0.044715 * (x * x * x))))


def _sel(is_ctx, p):
    return jnp.where(is_ctx, p[1:2], p[0:1])


def _seg_acc(is_ctx, v):
    rows = lax.broadcasted_iota(jnp.int32, (2, v.shape[1]), 0)
    return jnp.where(rows == is_ctx.astype(jnp.int32), jnp.broadcast_to(v, (2, v.shape[1])), 0.0)


def _rsum(v):
    return jnp.sum(v, axis=0, keepdims=True)


def _shift_rows(x, o, t_lat, n):
    if o == 0:
        return x
    y = pltpu.roll(x, (-o) % n, 0)
    t = lax.broadcasted_iota(jnp.int32, x.shape, 0)
    if o > 0:
        ok = t < n - o
        if t_lat < n:
            ok = ok & ((t < t_lat - o) | (t >= t_lat))
    else:
        ok = t >= -o
        if t_lat < n:
            ok = ok & ((t < t_lat) | (t >= t_lat - o))
    return jnp.where(ok, y, 0.0)


def conv_fwd(name, xarr, col_off, width, w, b, left, n_rows, t_lat, out_dtype, cb=128):
    taps = w.shape[0]
    assert col_off % cb == 0 and width % cb == 0

    def body(x_ref, w_ref, b_ref, o_ref):
        x = x_ref[...].astype(F32)
        acc = jnp.broadcast_to(b_ref[...], x.shape)
        for k in range(taps):
            acc = acc + _shift_rows(x, k - left, t_lat, n_rows) * w_ref[k:k + 1, :]
        o_ref[...] = acc.astype(o_ref.dtype)

    return pl.pallas_call(
        body, name=name, grid=(width // cb,),
        out_shape=jax.ShapeDtypeStruct((n_rows, width), out_dtype),
        in_specs=[pl.BlockSpec((n_rows, cb), lambda j: (0, col_off // cb + j)),
                  pl.BlockSpec((taps, cb), lambda j: (0, j)),
                  pl.BlockSpec((1, cb), lambda j: (0, j))],
        out_specs=pl.BlockSpec((n_rows, cb), lambda j: (0, j)),
        compiler_params=_cparams(("parallel",)),
    )(xarr, w, b)


def conv_bwd(name, dout, xarr, col_off, width, w, left, n_rows, t_lat, cb=128):
    taps = w.shape[0]

    def body(d_ref, x_ref, w_ref, dx_ref, dw_ref, db_ref):
        d = d_ref[...].astype(F32)
        x = x_ref[...].astype(F32)
        dx = jnp.zeros_like(d)
        dws = []
        for k in range(taps):
            dx = dx + _shift_rows(d, left - k, t_lat, n_rows) * w_ref[k:k + 1, :]
            dws.append(_rsum(d * _shift_rows(x, k - left, t_lat, n_rows)))
        dx_ref[...] = dx.astype(dx_ref.dtype)
        dw_ref[...] = jnp.concatenate(dws, axis=0)
        db_ref[...] = _rsum(d)

    return pl.pallas_call(
        body, name=name, grid=(width // cb,),
        out_shape=[jax.ShapeDtypeStruct((n_rows, width), BF16), jax.ShapeDtypeStruct((taps, width), F32),
                   jax.ShapeDtypeStruct((1, width), F32)],
        in_specs=[pl.BlockSpec((n_rows, cb), lambda j: (0, j)),
                  pl.BlockSpec((n_rows, cb), lambda j: (0, col_off // cb + j)),
                  pl.BlockSpec((taps, cb), lambda j: (0, j))],
        out_specs=[pl.BlockSpec((n_rows, cb), lambda j: (0, j)), pl.BlockSpec((taps, cb), lambda j: (0, j)),
                   pl.BlockSpec((1, cb), lambda j: (0, j))],
        compiler_params=_cparams(("parallel",)),
    )(dout, xarr, w)


def _ffn_conv(a, w_ref, b_ref, t_lat):
    shifted = [_shift_rows(a, k - 1, t_lat, t_lat) for k in range(3)]
    ac = jnp.broadcast_to(b_ref[...], a.shape)
    for k in range(3):
        ac = ac + shifted[k] * w_ref[k:k + 1, :]
    return ac, shifted


def ffn_mix_fwd(u, w, b, t_lat, cb=128):
    nblk = FFN // cb

    def body(a_ref, g_ref, w_ref, b_ref, f_ref):
        ac, _ = _ffn_conv(a_ref[...].astype(F32), w_ref, b_ref, t_lat)
        f_ref[...] = (_silu(ac) * g_ref[...].astype(F32)).astype(f_ref.dtype)

    col = lambda shape, off=0: pl.BlockSpec(shape, lambda j: (0, off + j))
    return pl.pallas_call(
        body, name="ffn_mix", grid=(nblk,),
        out_shape=jax.ShapeDtypeStruct((t_lat, FFN), BF16),
        in_specs=[col((t_lat, cb)), col((t_lat, cb), nblk), col((3, cb)), col((1, cb))],
        out_specs=col((t_lat, cb)),
        compiler_params=_cparams(("parallel",)),
    )(u, u, w, b)


def ffn_mix_bwd(u, df, w, b, t_lat, cb=128):
    nblk = FFN // cb

    def body(a_ref, g_ref, df_ref, w_ref, b_ref, du_ref, dw_ref, db_ref):
        ac, shifted = _ffn_conv(a_ref[...].astype(F32), w_ref, b_ref, t_lat)
        d = df_ref[...].astype(F32)
        s = _sigmoid(ac)
        du_ref[1] = (d * (ac * s)).astype(du_ref.dtype)
        dac = d * g_ref[...].astype(F32) * (s * (1.0 + ac * (1.0 - s)))
        da = jnp.zeros_like(dac)
        for k in range(3):
            da = da + _shift_rows(dac, 1 - k, t_lat, t_lat) * w_ref[k:k + 1, :]
        du_ref[0] = da.astype(du_ref.dtype)
        dw_ref[...] = jnp.concatenate([_rsum(dac * shifted[k]) for k in range(3)], axis=0)
        db_ref[...] = _rsum(dac)

    col = lambda shape, off=0: pl.BlockSpec(shape, lambda j: (0, off + j))
    return pl.pallas_call(
        body, name="ffn_mix_bwd", grid=(nblk,),
        out_shape=[jax.ShapeDtypeStruct((2, t_lat, FFN), BF16),
                   jax.ShapeDtypeStruct((3, FFN), F32), jax.ShapeDtypeStruct((1, FFN), F32)],
        in_specs=[col((t_lat, cb)), col((t_lat, cb), nblk), col((t_lat, cb)), col((3, cb)), col((1, cb))],
        out_specs=[pl.BlockSpec((2, t_lat, cb), lambda j: (0, 0, j)), col((3, cb)), col((1, cb))],
        compiler_params=_cparams(("parallel",)),
    )(u, u, df, w, b)


def _chunk_order(direction, nb, nbl):
    if direction == 'f':
        return lambda s: ((s + nbl) % nb, 0)
    return lambda s: (nb - 1 - s, 0)


def _adjoint_order(direction, nb, nbl):
    if direction == 'f':
        return lambda s: ((nb - 1 - s + nbl) % nb, 0)
    return lambda s: (s, 0)


SUBLANES = 8


def _chunk_scan(a, b, carry, rev):
    tc, width = a.shape
    nt = tc // SUBLANES
    row = lax.broadcasted_iota(jnp.int32, a.shape, 0)
    a, b = a.reshape(nt, SUBLANES, width), b.reshape(nt, SUBLANES, width)
    in_tile = lax.broadcasted_iota(jnp.int32, a.shape, 1)
    for k in (1, 2, 4):
        shift = SUBLANES - k if rev else k
        edge = in_tile >= SUBLANES - k if rev else in_tile < k
        b = jnp.where(edge, b, a * pltpu.roll(b, shift, 1) + b)
        a = jnp.where(edge, a, a * pltpu.roll(a, shift, 1))
    a, b = a.reshape(tc, width), b.reshape(tc, width)
    hs = [None] * nt
    c = carry
    for kt in range(nt):
        k = nt - 1 - kt if rev else kt
        h = b[k * SUBLANES:(k + 1) * SUBLANES] + a[k * SUBLANES:(k + 1) * SUBLANES] * c
        hs[k] = h
        c = h[0:1] if rev else h[SUBLANES - 1:SUBLANES]
    h = jnp.concatenate(hs, axis=0)
    if rev:
        return h, jnp.where(row == tc - 1, carry, pltpu.roll(h, tc - 1, 0)), c
    return h, jnp.where(row == 0, carry, pltpu.roll(h, 1, 0)), c


def _one_minus_a_squared(log_a, a):
    return (1.0 + a * a) * jnp.tanh(-log_a)


def _gate_elem(pre_r, pre_i, xc, b_a, b_x, sp):
    r = _sigmoid(pre_r + b_a)
    i = _sigmoid(pre_i + b_x)
    log_a = (-LRU_C) * r * sp
    a = jnp.exp(log_a)
    m2 = _one_minus_a_squared(log_a, a)
    mult = jnp.where(m2 > 0.0, m2 * lax.rsqrt(m2), 0.0)
    return a, mult * (i * xc)


def _gate_elem_bwd(pre_r, pre_i, xc, b_a, b_x, sp, da, du):
    r = _sigmoid(pre_r + b_a)
    i = _sigmoid(pre_i + b_x)
    log_a = (-LRU_C) * r * sp
    a = jnp.exp(log_a)
    m2 = _one_minus_a_squared(log_a, a)
    inv_mult = lax.rsqrt(m2)
    g = du * (m2 * inv_mult)
    d_mult = du * (i * xc)
    d_log_a = (da - d_mult * a * inv_mult) * a
    d_pre_r = d_log_a * ((-LRU_C) * sp) * (r * (1.0 - r))
    d_pre_i = g * xc * (i * (1.0 - i))
    return d_pre_r, d_pre_i, g * i, _rsum(d_log_a * ((-LRU_C) * r))


def _blockdiag(xb16, w_ref_val, d):
    outs = []
    for n in range(LRU_BLOCKS):
        outs.append(jnp.dot(xb16[:, n * LRU_BW:(n + 1) * LRU_BW], w_ref_val[d * LRU_BLOCKS + n],
                            preferred_element_type=F32))
    return jnp.concatenate(outs, axis=1)


def lru_scan(name, xc, w_a, w_x, b_a, b_x, sp, direction, n_rows, t_lat):
    w = xc.shape[1]
    d = 0 if direction == 'f' else 1
    tc = _pick(math.gcd(t_lat, n_rows), (256, 128))
    nb, nbl = n_rows // tc, t_lat // tc
    order = _chunk_order(direction, nb, nbl)
    rev = direction == 'b'

    def body(x_ref, wa_ref, wx_ref, ba_ref, bx_ref, sp_ref, a_ref, h_ref, hp_ref, carry):
        @pl.when(pl.program_id(0) == 0)
        def _():
            carry[...] = jnp.zeros_like(carry)

        x = x_ref[...]
        xb16 = x.astype(BF16)
        a, u = _gate_elem(_blockdiag(xb16, wa_ref[...], d), _blockdiag(xb16, wx_ref[...], d), x,
                          ba_ref[d:d + 1, :], bx_ref[d:d + 1, :], sp_ref[d:d + 1, :])
        a_ref[...] = a
        h_ref[...], hp_ref[...], carry[...] = _chunk_scan(a, u, carry[...], rev)

    spec = pl.BlockSpec((tc, w), order)
    whole = lambda p: pl.BlockSpec(p.shape, lambda s, nd=p.ndim: (0,) * nd)
    return pl.pallas_call(
        body, name=name, grid=(nb,),
        out_shape=[jax.ShapeDtypeStruct((n_rows, w), F32)] * 3,
        in_specs=[spec] + [whole(p) for p in (w_a, w_x, b_a, b_x, sp)], out_specs=[spec] * 3,
        scratch_shapes=[pltpu.VMEM((1, w), F32)],
        compiler_params=_cparams(("arbitrary",)),
    )(xc, w_a, w_x, b_a, b_x, sp)


def lru_scan_bwd(name, xc, a, dh, hprev, dxc_in, w_a, w_x, b_a, b_x, sp, direction, n_rows, t_lat):
    w = xc.shape[1]
    d = 0 if direction == 'f' else 1
    tc = _pick(math.gcd(t_lat, n_rows), (256, 128))
    nb, nbl = n_rows // tc, t_lat // tc
    order = _adjoint_order(direction, nb, nbl)
    rev = direction == 'f'
    has_in = dxc_in is not None
    nt_dims, tn_dims = (((1,), (1,)), ((), ())), (((0,), (0,)), ((), ()))

    def dh_order(s):
        c, _ = order(s)
        return (jnp.minimum(c, nbl - 1), 0)

    def body(*refs):
        x_ref, a_ref, dh_ref, hp_ref = refs[:4]
        in_ref = refs[4] if has_in else None
        wa_ref, wx_ref, ba_ref, bx_ref, sp_ref = refs[4 + has_in:9 + has_in]
        dx_ref, dwa_ref, dwx_ref, dba_ref, dbx_ref, dsp_ref, carry = refs[9 + has_in:]
        s = pl.program_id(0)

        @pl.when(s == 0)
        def _():
            carry[...] = jnp.zeros_like(carry)
            for acc in (dwa_ref, dwx_ref, dba_ref, dbx_ref, dsp_ref):
                acc[...] = jnp.zeros_like(acc)

        chunk, _ = order(s)
        live = (chunk < nbl).astype(F32)
        av = a_ref[...]
        dv = dh_ref[...].astype(F32) * live
        _, c_next, carry[...] = _chunk_scan(av, av * dv, carry[...], rev)
        lam = dv + c_next

        x = x_ref[...]
        xb16 = x.astype(BF16)
        wa, wx = wa_ref[...], wx_ref[...]
        dpr, dpi, dxc, dsp_d = _gate_elem_bwd(_blockdiag(xb16, wa, d), _blockdiag(xb16, wx, d), x,
                                              ba_ref[d:d + 1, :], bx_ref[d:d + 1, :], sp_ref[d:d + 1, :],
                                              lam * hp_ref[...], lam)
        dpr16, dpi16 = dpr.astype(BF16), dpi.astype(BF16)
        back = []
        for n in range(LRU_BLOCKS):
            sl = slice(n * LRU_BW, (n + 1) * LRU_BW)
            back.append(lax.dot_general(dpr16[:, sl], wa[d * LRU_BLOCKS + n], nt_dims, preferred_element_type=F32)
                        + lax.dot_general(dpi16[:, sl], wx[d * LRU_BLOCKS + n], nt_dims, preferred_element_type=F32))
            dwa_ref[n] += lax.dot_general(xb16[:, sl], dpr16[:, sl], tn_dims, preferred_element_type=F32)
            dwx_ref[n] += lax.dot_general(xb16[:, sl], dpi16[:, sl], tn_dims, preferred_element_type=F32)
        dxc = dxc + jnp.concatenate(back, axis=1)
        dx_ref[...] = dxc + in_ref[...] if has_in else dxc
        dba_ref[...] += _rsum(dpr)
        dbx_ref[...] += _rsum(dpi)
        dsp_ref[...] += dsp_d

    spec = pl.BlockSpec((tc, w), order)
    whole = lambda shape: pl.BlockSpec(shape, lambda s, nd=len(shape): (0,) * nd)
    params = (w_a, w_x, b_a, b_x, sp)
    acc_shapes = [(LRU_BLOCKS, LRU_BW, LRU_BW)] * 2 + [(1, w)] * 3
    return pl.pallas_call(
        body, name=name, grid=(nb,),
        out_shape=[jax.ShapeDtypeStruct((n_rows, w), F32)] + [jax.ShapeDtypeStruct(sh, F32) for sh in acc_shapes],
        in_specs=[spec, spec, pl.BlockSpec((tc, w), dh_order), spec] + [spec] * has_in
        + [whole(p.shape) for p in params],
        out_specs=[spec] + [whole(sh) for sh in acc_shapes],
        scratch_shapes=[pltpu.VMEM((1, w), F32)],
        compiler_params=_cparams(("arbitrary",)),
    )(xc, a, dh, hprev, *([dxc_in] if has_in else []), *params)


def _rope_tables(t_lat, n_rows):
    rows = t_lat // GRID_W
    row_ids = jnp.repeat(jnp.arange(rows), GRID_W).astype(F32)
    col_ids = jnp.tile(jnp.arange(GRID_W), rows).astype(F32)
    axis_dim = QK_ROPE // 2
    inv = 1.0 / (ROPE_BASE ** (jnp.arange(0, axis_dim, 2, dtype=F32) / axis_dim))
    ang = jnp.concatenate([row_ids[:, None] * inv, col_ids[:, None] * inv], axis=-1)
    cos, sin = jnp.cos(ang), jnp.sin(ang)
    half = QK_ROPE // 2
    ones, zeros = jnp.ones((t_lat, QK_NOPE), F32), jnp.zeros((t_lat, QK_NOPE), F32)
    pad1, pad0 = jnp.ones((t_lat, HEAD_PAD - QK_DIM), F32), jnp.zeros((t_lat, HEAD_PAD - QK_DIM), F32)
    zh = jnp.zeros((t_lat, half), F32)
    c_tab = jnp.concatenate([ones, cos, cos, pad1], axis=1)
    s1 = jnp.concatenate([zeros, -sin, zh, pad0], axis=1)
    s2 = jnp.concatenate([zeros, zh, sin, pad0], axis=1)
    n_ctx = n_rows - t_lat
    c_tab = jnp.concatenate([c_tab, jnp.ones((n_ctx, HEAD_PAD), F32)], axis=0)
    s1 = jnp.concatenate([s1, jnp.zeros((n_ctx, HEAD_PAD), F32)], axis=0)
    s2 = jnp.concatenate([s2, jnp.zeros((n_ctx, HEAD_PAD), F32)], axis=0)
    return c_tab, s1, s2


def _rope(x, c, s1, s2):
    half = QK_ROPE // 2
    return x * c + pltpu.roll(x, HEAD_PAD - half, 1) * s1 + pltpu.roll(x, half, 1) * s2


def _rope_t(dy, c, s1, s2):
    half = QK_ROPE // 2
    return dy * c + pltpu.roll(dy * s1, half, 1) + pltpu.roll(dy * s2, HEAD_PAD - half, 1)


def _heads(x):
    return [x[:, h * HEAD_PAD:(h + 1) * HEAD_PAD] for h in range(N_HEADS)]


Q_SCALE = QK_DIM ** -0.5 * math.log2(math.e)

def attn_fwd(q, k, v, t_lat, n_rows, tq):
    def body(q_ref, k_ref, v_ref, o_ref, lse_ref):
        s = lax.dot_general(q_ref[...], k_ref[...], (((1,), (1,)), ((), ())), preferred_element_type=F32)
        m = jnp.max(s, axis=-1, keepdims=True)
        p = jnp.exp2(s - m)
        l = jnp.sum(p, axis=-1, keepdims=True)
        o = jnp.dot(p.astype(BF16), v_ref[...], preferred_element_type=F32) / l
        o_ref[...] = o.astype(o_ref.dtype)
        lse_ref[...] = jnp.broadcast_to(m + jnp.log2(l), lse_ref.shape)

    qspec = pl.BlockSpec((tq, HEAD_PAD), lambda h, i: (i, h))
    kspec = pl.BlockSpec((n_rows, HEAD_PAD), lambda h, i: (0, h))
    return pl.pallas_call(
        body, name="attn_fwd", grid=(N_HEADS, t_lat // tq),
        out_shape=[jax.ShapeDtypeStruct((t_lat, N_HEADS * HEAD_PAD), BF16),
                   jax.ShapeDtypeStruct((t_lat, N_HEADS * HEAD_PAD), F32)],
        in_specs=[qspec, kspec, kspec], out_specs=[qspec, qspec],
        compiler_params=_cparams(("parallel", "arbitrary")),
    )(q, k, v)


def attn_bwd(q, k, v, o, do, lse, t_lat, n_rows, tq):
    scale = QK_DIM ** -0.5
    nq = t_lat // tq
    nt = (((1,), (1,)), ((), ()))
    tn = (((0,), (0,)), ((), ()))

    def body(q_ref, k_ref, v_ref, o_ref, do_ref, lse_ref, dq_ref, dk_ref, dv_ref):
        @pl.when(pl.program_id(1) == 0)
        def _():
            dk_ref[...] = jnp.zeros_like(dk_ref)
            dv_ref[...] = jnp.zeros_like(dv_ref)

        qv, kv, vv, dov = q_ref[...], k_ref[...], v_ref[...], do_ref[...]
        s = lax.dot_general(qv, kv, nt, preferred_element_type=F32)
        p = jnp.exp2(s - lse_ref[:, 0:1])
        dv_ref[...] += lax.dot_general(p.astype(BF16), dov, tn, preferred_element_type=F32)
        dp = lax.dot_general(dov, vv, nt, preferred_element_type=F32)
        delta = jnp.sum(dov.astype(F32) * o_ref[...].astype(F32), axis=-1, keepdims=True)
        ds = (p * (dp - delta)).astype(BF16)
        dq_ref[...] = (jnp.dot(ds, kv, preferred_element_type=F32) * scale).astype(dq_ref.dtype)
        dk_ref[...] += lax.dot_general(ds, qv, tn, preferred_element_type=F32)

        @pl.when(pl.program_id(1) == nq - 1)
        def _():
            dk_ref[...] = dk_ref[...] * (scale / Q_SCALE)

    qspec = pl.BlockSpec((tq, HEAD_PAD), lambda h, i: (i, h))
    kspec = pl.BlockSpec((n_rows, HEAD_PAD), lambda h, i: (0, h))
    return pl.pallas_call(
        body, name="attn_bwd", grid=(N_HEADS, t_lat // tq),
        out_shape=[jax.ShapeDtypeStruct((t_lat, N_HEADS * HEAD_PAD), BF16),
                   jax.ShapeDtypeStruct((n_rows, N_HEADS * HEAD_PAD), F32),
                   jax.ShapeDtypeStruct((n_rows, N_HEADS * HEAD_PAD), F32)],
        in_specs=[qspec, kspec, kspec, qspec, qspec, qspec], out_specs=[qspec, kspec, kspec],
        compiler_params=_cparams(("parallel", "arbitrary")),
    )(q, k, v, o, do, lse)


def adamw(name, w, g, m, v):
    r, ccols = w.shape
    if r % 8 == 0:
        tr, tcol = _best_div(r, 8, max(8, 262144 // ccols)), ccols
    else:
        tr, tcol = r, _pick(ccols, (256, 128))
    c1 = 1.0 - ADAM_B1 ** ADAM_STEP
    c2 = 1.0 - ADAM_B2 ** ADAM_STEP

    def body(w_ref, g_ref, m_ref, v_ref, d_ref, nm_ref, nv_ref):
        gv = g_ref[...]
        nm = ADAM_B1 * m_ref[...] + (1.0 - ADAM_B1) * gv
        nv = ADAM_B2 * v_ref[...] + (1.0 - ADAM_B2) * (gv * gv)
        d_ref[...] = -ADAM_LR * ((nm / c1) / (jnp.sqrt(nv / c2) + ADAM_EPS) + ADAM_WD * w_ref[...])
        nm_ref[...] = nm
        nv_ref[...] = nv

    spec = pl.BlockSpec((tr, tcol), lambda i, j: (i, j))
    return pl.pallas_call(
        body, name=name, grid=(r // tr, ccols // tcol),
        out_shape=[jax.ShapeDtypeStruct((r, ccols), F32)] * 3,
        in_specs=[spec] * 4, out_specs=[spec] * 3,
        compiler_params=_cparams(("parallel", "parallel")),
    )(w, g, m, v)


def adamw_many(name, ws, gs, ms, vs):
    n = len(ws)
    c1 = 1.0 - ADAM_B1 ** ADAM_STEP
    c2 = 1.0 - ADAM_B2 ** ADAM_STEP

    def body(*refs):
        for i in range(n):
            w_ref, g_ref, m_ref, v_ref = (refs[k * n + i] for k in range(4))
            d_ref, nm_ref, nv_ref = (refs[(4 + k) * n + i] for k in range(3))
            gv = g_ref[...]
            nm = ADAM_B1 * m_ref[...] + (1.0 - ADAM_B1) * gv
            nv = ADAM_B2 * v_ref[...] + (1.0 - ADAM_B2) * (gv * gv)
            d_ref[...] = -ADAM_LR * ((nm / c1) / (jnp.sqrt(nv / c2) + ADAM_EPS) + ADAM_WD * w_ref[...])
            nm_ref[...] = nm
            nv_ref[...] = nv

    vmem = pl.BlockSpec(memory_space=pltpu.VMEM)
    res = pl.pallas_call(
        body, name=name,
        out_shape=[jax.ShapeDtypeStruct(w.shape, F32) for w in ws] * 3,
        in_specs=[vmem] * (4 * n), out_specs=[vmem] * (3 * n),
        compiler_params=_cparams(),
    )(*ws, *gs, *ms, *vs)
    return [tuple(res[k * n + i] for k in range(3)) for i in range(n)]


def _flat(parts, dtype, row_mult):
    v = jnp.concatenate([p.reshape(-1).astype(dtype) for p in parts])
    quantum = row_mult * FLAT_C
    total = -(-v.shape[0] // quantum) * quantum
    return jnp.pad(v, (0, total - v.shape[0])).reshape(total // FLAT_C, FLAT_C)


def _gathered_to_full(name, g):
    k = g.shape[1]
    return jnp.transpose(g, (1, 0, 2)).reshape(k, N_DEV * g.shape[2])


def _shard_to_rb(name, w):
    return w if name in ROW_SHARDED else w.T


def _rb_to_shard(name, g):
    return g if name in ROW_SHARDED else g.T


def _rb_from_gathered(name, g):
    cols = g.shape[2]
    if name == 'w_in':
        z = lambda k: jnp.zeros((k, cols), g.dtype)
        full = g.reshape(N_DEV * g.shape[1], cols)
        return jnp.concatenate([full[:Z_KR], z(QK_NOPE), full[Z_KR:Z_KR + QK_ROPE], z(HEAD_PAD - QK_DIM),
                                full[Z_KR + QK_ROPE:]], axis=0)
    if name == 'w_uq':
        return jnp.pad(g, ((0, 0), (0, HEAD_PAD - QK_DIM), (0, 0))).reshape(N_HEADS * HEAD_PAD, cols)
    if name == 'w_ukv':
        pad = lambda t: jnp.pad(t, ((0, 0), (0, HEAD_PAD - t.shape[1]), (0, 0))).reshape(N_HEADS * HEAD_PAD, cols)
        return jnp.concatenate([pad(g[:, :QK_NOPE]), pad(g[:, QK_NOPE:])], axis=0)
    if name == 'w_o_attn':
        full = g.reshape(D, N_HEADS, V_HEAD)
        return jnp.pad(full, ((0, 0), (0, 0), (0, HEAD_PAD - V_HEAD))).reshape(D, N_HEADS * HEAD_PAD)
    return g.reshape(N_DEV * g.shape[1], cols)


def _chunks_from_rb_grad(name, g):
    cols = g.shape[1]
    if name == 'w_in':
        full = jnp.concatenate([g[:Z_KR], g[Z_KR + QK_NOPE:Z_KR + QK_DIM], g[Z_XB:]], axis=0)
        return full.reshape(N_DEV, -1, cols)
    if name == 'w_uq':
        return g.reshape(N_HEADS, HEAD_PAD, cols)[:, :QK_DIM]
    if name == 'w_ukv':
        half = N_HEADS * HEAD_PAD
        gk = g[:half].reshape(N_HEADS, HEAD_PAD, cols)[:, :QK_NOPE]
        gv = g[half:].reshape(N_HEADS, HEAD_PAD, cols)[:, :V_HEAD]
        return jnp.concatenate([gk, gv], axis=1)
    if name == 'w_o_attn':
        full = g.reshape(D, N_HEADS, HEAD_PAD)[:, :, :V_HEAD].reshape(D, N_HEADS * V_HEAD)
        return full.reshape(N_DEV, D // N_DEV, N_HEADS * V_HEAD)
    return g.reshape(N_DEV, -1, cols)


def local_step(x, ctx, target, mod_l, mod_c, wt, on_grad=None, arrive=None):
    t_lat, n_ctx = x.shape[0], ctx.shape[0]
    n = t_lat + n_ctx
    tm = _pick(math.gcd(t_lat, n), (256, 128))
    tq_fwd = _pick(t_lat, (256, 128))
    tq_bwd = _pick(t_lat, (512, 256, 128))
    row = lambda v: v.reshape(1, -1).astype(F32)
    two = lambda a, b: jnp.stack([a, b]).astype(F32)
    sh1_l, sc1_l, g1_l, sh2_l, sc2_l, g2_l = jnp.split(mod_l, 6)
    sh1_c, sc1_c = jnp.split(mod_c, 6)[:2]
    sc1, sh1 = two(sc1_l, sc1_c), two(sh1_l, sh1_c)
    g1, g2, sc2, sh2 = row(g1_l), row(g2_l), row(sc2_l), row(sh2_l)
    norm1_g, norm2_g, final_g = row(wt['norm1_g']), row(wt['norm2_g']), row(wt['final_g'])
    q_g, kv_g, b_gate = row(wt['q_norm_g']), row(wt['kv_norm_g']), row(wt['b_gate'])
    wt = dict(wt)
    pending = []

    def sent():
        tokens = list(pending)
        pending.clear()
        return tokens

    def need(names, after):
        if arrive is not None:
            got = arrive(names, after)
            if '_token' in got:
                pending.append(got.pop('_token'))
            wt.update(got)
        return [wt[n] for n in names]
    lru_w_a = wt['lru_w_a'].reshape(2 * LRU_BLOCKS, LRU_BW, LRU_BW).astype(BF16)
    lru_w_x = wt['lru_w_x'].reshape(2 * LRU_BLOCKS, LRU_BW, LRU_BW).astype(BF16)
    b_a, b_x, lam = wt['lru_b_a'], wt['lru_b_x'], wt['lru_lambda']
    sp = jnp.logaddexp(-lam, 0.0)
    c_tab, s1_tab, s2_tab = _rope_tables(t_lat, n)
    rw = functools.partial(rowwise, n_rows=n, t_lat=t_lat, tm=tm)
    rw_lat = functools.partial(rowwise, n_rows=t_lat, t_lat=t_lat, tm=_pick(t_lat, (512, 256, 128)))

    stream = [(x, 0, D), (ctx, 0, D, 'ctx')]

    def f_norm1(is_ctx, rows, params):
        (xl, xc_), (g, sc, sh) = rows, params
        return [_norm_mod(jnp.where(is_ctx, xc_, xl), g, _sel(is_ctx, sc), _sel(is_ctx, sh))], []

    (h,), _ = rw("norm1", f_norm1, stream, [norm1_g, sc1, sh1], [(D, BF16)], [])
    (w_in_t,) = need(('w_in',), h)
    z = matmul("w_in", h, w_in_t, 'nt', BF16, after=sent())
    w_uq_t, w_ukv_t, w_o_lru = need(('w_uq', 'w_ukv', 'w_o_lru'), z)

    def f_qkv_norm(is_ctx, rows, params):
        (ql, kvl), (gq, gkv) = rows, params
        return [_rms(ql, gq), _rms(kvl, gkv)], []

    (qn, kvn), _ = rw("qkv_norm", f_qkv_norm, [(z, Z_Q, Q_RANK), (z, Z_KV, KV_RANK)], [q_g, kv_g],
                      [(Q_RANK, BF16), (KV_RANK, BF16)], [])
    qp = matmul("w_uq", qn, w_uq_t, 'nt', BF16)
    kvp = matmul("w_ukv", kvn, w_ukv_t, 'nt', BF16)

    def f_rope(is_ctx, rows, params):
        qv, kk, vv, kr, c, s1, s2 = rows
        krr = _rope(kr, c, s1, s2)
        qo = jnp.concatenate([_rope(qh, c, s1, s2) for qh in _heads(qv)], axis=1) * Q_SCALE
        ko = jnp.concatenate([kh + krr for kh in _heads(kk)], axis=1)
        return [qo, ko, vv], []

    hp = N_HEADS * HEAD_PAD
    (qr, kr_, vr), _ = rw("rope", f_rope,
                          [(qp, 0, hp), (kvp, 0, hp), (kvp, hp, hp), (z, Z_KR, HEAD_PAD), (c_tab, 0, HEAD_PAD),
                           (s1_tab, 0, HEAD_PAD), (s2_tab, 0, HEAD_PAD)], [], [(hp, BF16)] * 3, [])
    attn, lse = attn_fwd(qr, kr_, vr, t_lat, n, tq_fwd)

    xc = conv_fwd("lru_conv", z, Z_XB, LRU_W, wt['lru_conv_w'], row(wt['lru_conv_b']), 2, n, t_lat, F32)
    a_f, h_f, hp_f = lru_scan("lru_scan_f", xc, lru_w_a, lru_w_x, b_a, b_x, sp, 'f', n, t_lat)
    a_b, h_b, hp_b = lru_scan("lru_scan_b", xc, lru_w_a, lru_w_x, b_a, b_x, sp, 'b', n, t_lat)

    def f_lru_out(is_ctx, rows, params):
        hf, hb, yb = rows
        return [(hf + hb) * _gelu(yb)], []

    (ybin,), _ = rw_lat("lru_out", f_lru_out, [(h_f, 0, LRU_W), (h_b, 0, LRU_W), (z, Z_YB, LRU_W)], [],
                        [(LRU_W, BF16)], [])
    w_o_attn_t, w_out, w_up_t, w_down = need(('w_o_attn', 'w_out', 'w_up', 'w_down'), attn)
    y_a = matmul("w_o_attn", attn, w_o_attn_t, 'nt', BF16)
    y_b = matmul("w_o_lru", ybin, w_o_lru, 'nn', BF16)

    def _merge(ya, yb, gl, bg):
        gates = _sigmoid(gl + bg)
        return gates[:, :D] * ya + gates[:, D:] * yb

    def f_merge(is_ctx, rows, params):
        (ya, yb, gl), (bg,) = rows, params
        return [_merge(ya, yb, gl, bg)], []

    (mrg,), _ = rw_lat("merge", f_merge, [(y_a, 0, D), (y_b, 0, D), (z, Z_GL, 2 * D)], [b_gate], [(D, BF16)], [])
    o = matmul("w_out", mrg, w_out, 'nn', BF16)

    def _res_norm2(xv, ov, g1v, g, sc, sh):
        x1 = xv + g1v * ov
        return x1, _norm_mod(x1, g, sc, sh)

    def f_norm2(is_ctx, rows, params):
        (xv, ov), (g1v, g, sc, sh) = rows, params
        x1, h2v = _res_norm2(xv, ov, g1v, g, sc, sh)
        return [x1, h2v], []

    (x1, h2), _ = rw_lat("norm2", f_norm2, [(x, 0, D), (o, 0, D)], [g1, norm2_g, sc2, sh2], [(D, F32), (D, BF16)], [])
    u = matmul("w_up", h2, w_up_t, 'nt', BF16)
    f = ffn_mix_fwd(u, wt['ffn_conv_w'], row(wt['ffn_conv_b']), t_lat)
    dn = matmul("w_down", f, w_down, 'nn', BF16)

    def _tile_loss(x1v, dv, g2v, fg, tgt):
        y = _rms(x1v + g2v * dv, fg)
        e = y - tgt
        return 0.5 * jnp.sum(jnp.mean(e * e, axis=-1, keepdims=True), axis=0, keepdims=True)

    def f_final(is_ctx, rows, params):
        (x1v, dv, tgt), (g2v, fg) = rows, params
        lv, vjp = jax.vjp(lambda a, b, c, d: _tile_loss(a, b, c, d, tgt), x1v, dv, g2v, fg)
        dx2, dd, dg2, dfg = vjp(jnp.ones((1, 1), F32))
        return [dx2, dd], [dg2, dfg, jnp.broadcast_to(lv, (1, 128))]

    (dx2, dd), (dg2, dfinal_g, loss_v) = rw_lat("final", f_final, [(x1, 0, D), (dn, 0, D), (target, 0, D)],
                                                [g2, final_g], [(D, F32), (D, BF16)], [(1, D), (1, D), (1, 128)])
    loss = loss_v[0, 0]

    grads = {'final_g': dfinal_g}

    def put(name, g):
        grads[name] = g
        if on_grad is not None:
            pending.append(on_grad(name, g))
    df = matmul("d_f", dd, w_down, 'nt', BF16)
    put('w_down', matmul("g_w_down", f, dd, 'tn', BF16))

    du, grads['ffn_conv_w'], grads['ffn_conv_b'] = ffn_mix_bwd(u, df, wt['ffn_conv_w'], row(wt['ffn_conv_b']),
                                                               t_lat)
    dh2 = matmul("d_h2", du, w_up_t, 'nn', BF16, after=sent())
    put('w_up', matmul("g_w_up", du, h2, 'tn', BF16))

    def b_norm2(is_ctx, rows, params):
        (xv, ov, dh2v, dx2v), (g1v, g, sc, sh) = rows, params
        _, vjp = jax.vjp(_res_norm2, xv, ov, g1v, g, sc, sh)
        dx, do, dg1v, dg, dsc, dsh = vjp((dx2v, dh2v))
        return [dx, do], [dg1v, dg, dsc, dsh]

    (dx_res, do), (dg1, dnorm2_g, dsc2, dsh2) = rw_lat(
        "norm2_bwd", b_norm2, [(x, 0, D), (o, 0, D), (dh2, 0, D), (dx2, 0, D)], [g1, norm2_g, sc2, sh2],
        [(D, F32), (D, BF16)], [(1, D)] * 4)
    grads['norm2_g'] = dnorm2_g
    dmrg = matmul("d_merge", do, w_out, 'nt', BF16, after=sent())
    put('w_out', matmul("g_w_out", mrg, do, 'tn', BF16))

    def b_merge(is_ctx, rows, params):
        (ya, yb, gl, dm), (bg,) = rows, params
        _, vjp = jax.vjp(_merge, ya, yb, gl, bg)
        dya, dyb, dgl, dbg = vjp(dm)
        return [dya, dyb, dgl], [dbg]

    (dy_a, dy_b, dgl), (grads['b_gate'],) = rw_lat(
        "merge_bwd", b_merge, [(y_a, 0, D), (y_b, 0, D), (z, Z_GL, 2 * D), (dmrg, 0, D)], [b_gate],
        [(D, BF16), (D, BF16), (2 * D, BF16)], [(1, 2 * D)])
    dattn = matmul("d_attn", dy_a, w_o_attn_t, 'nn', BF16, after=sent())
    put('w_o_attn', matmul("g_w_o_attn", dy_a, attn, 'tn', BF16))
    dybin = matmul("d_lru_out", dy_b, w_o_lru, 'nt', BF16, after=sent())
    put('w_o_lru', matmul("g_w_o_lru", ybin, dy_b, 'tn', BF16))

    def b_lru_out(is_ctx, rows, params):
        hf, hb, yb, dyv = rows
        _, vjp = jax.vjp(lambda s, y: s * _gelu(y), hf + hb, yb)
        dh, dyb = vjp(dyv)
        return [dh, dyb], []

    (dh_lru, dyb), _ = rw_lat("lru_out_bwd", b_lru_out,
                              [(h_f, 0, LRU_W), (h_b, 0, LRU_W), (z, Z_YB, LRU_W), (dybin, 0, LRU_W)], [],
                              [(LRU_W, F32), (LRU_W, BF16)], [])
    gate_params = (lru_w_a, lru_w_x, b_a, b_x, sp)
    dxc_f, *sums_f = lru_scan_bwd("lru_scan_f_bwd", xc, a_f, dh_lru, hp_f, None, *gate_params, 'f', n, t_lat)
    dxc, *sums_b = lru_scan_bwd("lru_scan_b_bwd", xc, a_b, dh_lru, hp_b, dxc_f, *gate_params, 'b', n, t_lat)
    dw_a, dw_x, db_a, db_x, dsp = (jnp.concatenate([f_, b_], axis=0) for f_, b_ in zip(sums_f, sums_b))
    put('lru_w_a', dw_a.reshape(2 * LRU_BLOCKS * LRU_BW, LRU_BW).astype(BF16))
    put('lru_w_x', dw_x.reshape(2 * LRU_BLOCKS * LRU_BW, LRU_BW).astype(BF16))
    grads['lru_b_a'], grads['lru_b_x'] = db_a, db_x
    grads['lru_lambda'] = -dsp * _sigmoid(-lam)
    dxb, grads['lru_conv_w'], grads['lru_conv_b'] = conv_bwd("lru_conv_bwd", dxc, z, Z_XB, LRU_W, wt['lru_conv_w'],
                                                             2, n, t_lat)

    dq, dk, dv = attn_bwd(qr, kr_, vr, attn, dattn, lse, t_lat, n, tq_bwd)

    def b_rope(is_ctx, rows, params):
        dqv, dkv, dvv, c, s1, s2 = rows
        live = jnp.where(is_ctx, 0.0, 1.0)
        dqo = jnp.concatenate([_rope_t(dqh, c, s1, s2) for dqh in _heads(dqv)], axis=1) * live
        dkh = _heads(dkv)
        dkr = dkh[0]
        for t in dkh[1:]:
            dkr = dkr + t
        lanes = lax.broadcasted_iota(jnp.int32, dkr.shape, 1)
        dkr = jnp.where((lanes >= QK_NOPE) & (lanes < QK_DIM), _rope_t(dkr, c, s1, s2), 0.0)
        return [dqo, jnp.concatenate([dkv, dvv], axis=1), dkr], []

    (dqp, dkvp, dkr), _ = rw("rope_bwd", b_rope,
                             [(dq, 0, hp), (dk, 0, hp), (dv, 0, hp), (c_tab, 0, HEAD_PAD), (s1_tab, 0, HEAD_PAD),
                              (s2_tab, 0, HEAD_PAD)], [], [(hp, BF16), (2 * hp, BF16), (HEAD_PAD, BF16)], [])
    dqn = matmul("d_qn", dqp, w_uq_t, 'nn', BF16, after=sent())
    put('w_uq', matmul("g_w_uq", dqp, qn, 'tn', BF16))
    dkvn = matmul("d_kvn", dkvp, w_ukv_t, 'nn', BF16, after=sent())
    put('w_ukv', matmul("g_w_ukv", dkvp, kvn, 'tn', BF16))

    def b_qkv_norm(is_ctx, rows, params):
        (ql, kvl, dqv, dkvv), (gq, gkv) = rows, params
        _, vjp_q = jax.vjp(_rms, ql, gq)
        _, vjp_kv = jax.vjp(_rms, kvl, gkv)
        dql, dgq = vjp_q(dqv)
        dkvl, dgkv = vjp_kv(dkvv)
        return [dql, dkvl], [dgq, dgkv]

    (dq_lat, dkv_lat), (grads['q_norm_g'], grads['kv_norm_g']) = rw(
        "qkv_norm_bwd", b_qkv_norm, [(z, Z_Q, Q_RANK), (z, Z_KV, KV_RANK), (dqn, 0, Q_RANK), (dkvn, 0, KV_RANK)],
        [q_g, kv_g], [(Q_RANK, BF16), (KV_RANK, BF16)], [(1, Q_RANK), (1, KV_RANK)])
    pad_ctx = lambda t: jnp.pad(t, ((0, n_ctx), (0, 0)))
    dz = jnp.concatenate([dq_lat, dkv_lat, dkr, dxb, pad_ctx(dyb), pad_ctx(dgl)], axis=1)
    put('w_in', matmul("g_w_in", dz, h, 'tn', BF16))
    dh = matmul("d_h", dz, w_in_t, 'nn', BF16, after=sent())

    def b_norm1(is_ctx, rows, params):
        (xl, xc_, dhv, dxr), (g, sc, sh) = rows, params
        scv, shv = _sel(is_ctx, sc), _sel(is_ctx, sh)
        _, vjp = jax.vjp(_norm_mod, jnp.where(is_ctx, xc_, xl), g, scv, shv)
        dx, dg, dsc, dsh = vjp(dhv)
        return [dx + dxr], [dg, _seg_acc(is_ctx, dsc), _seg_acc(is_ctx, dsh)]

    (grad_x,), (grads['norm1_g'], dsc1, dsh1) = rw("norm1_bwd", b_norm1, stream + [(dh, 0, D), (dx_res, 0, D)],
                                                   [norm1_g, sc1, sh1], [(D, F32, 'lat')],
                                                   [(1, D), (2, D), (2, D)])
    zero = jnp.zeros((D,), F32)
    dmod_l = jnp.concatenate([dsh1[0], dsc1[0], dg1[0], dsh2[0], dsc2[0], dg2[0]])
    dmod_c = jnp.concatenate([dsh1[1], dsc1[1], zero, zero, zero, zero])
    return loss, grad_x, grads, dmod_l, dmod_c


def kernel(x, c, ctx, c_ctx, w_mod, b_mod, norm1_g, w_in, b_gate, q_norm_g, kv_norm_g, w_uq, w_ukv, w_o_attn, lru_conv_w, lru_conv_b, lru_w_a, lru_b_a, lru_w_x, lru_b_x, lru_lambda, w_o_lru, w_out, norm2_g, w_up, ffn_conv_w, ffn_conv_b, w_down, final_g, loss_target, m_c_ctx, m_w_mod, m_b_mod, m_norm1_g, m_w_in, m_b_gate, m_q_norm_g, m_kv_norm_g, m_w_uq, m_w_ukv, m_w_o_attn, m_lru_conv_w, m_lru_conv_b, m_lru_w_a, m_lru_b_a, m_lru_w_x, m_lru_b_x, m_lru_lambda, m_w_o_lru, m_w_out, m_norm2_g, m_w_up, m_ffn_conv_w, m_ffn_conv_b, m_w_down, m_final_g, v_c_ctx, v_w_mod, v_b_mod, v_norm1_g, v_w_in, v_b_gate, v_q_norm_g, v_kv_norm_g, v_w_uq, v_w_ukv, v_w_o_attn, v_lru_conv_w, v_lru_conv_b, v_lru_w_a, v_lru_b_a, v_lru_w_x, v_lru_b_x, v_lru_lambda, v_w_o_lru, v_w_out, v_norm2_g, v_w_up, v_ffn_conv_w, v_ffn_conv_b, v_w_down, v_final_g):
    given = dict(locals())
    strip = lambda name, a: a if name in ('c_ctx', 'final_g') else a[0]
    wsh = {n: strip(n, given[n]) for n in WEIGHTS}
    msh = {n: strip(n, given['m_' + n]) for n in WEIGHTS}
    vsh = {n: strip(n, given['v_' + n]) for n in WEIGHTS}
    me = _my_index()

    small = _flat([c[0]] + [wsh[n] for n in SMALL_F32], F32, 8)
    small_all = all_gather("gather_small", small).reshape(N_DEV, -1)
    c_all = small_all[:, :D]
    full, at = {}, D
    for n in SMALL_F32:
        cnt = math.prod(wsh[n].shape)
        full[n] = _gathered_to_full(n, small_all[:, at:at + cnt].reshape((N_DEV,) + wsh[n].shape))
        at += cnt

    cond = jnp.concatenate([c_all, c_ctx[None], jnp.zeros((7, D), F32)], axis=0)
    sil = cond * jax.nn.sigmoid(cond)
    mod_cols = matmul("mod_proj", sil, wsh['w_mod'], 'nn', F32)
    mod_all = all_gather("gather_mod", mod_cols)
    mod_all = jnp.transpose(mod_all, (1, 0, 2)).reshape(16, 6 * D) + b_mod[0][None]
    mod_l = lax.dynamic_index_in_dim(mod_all, me, axis=0, keepdims=False)
    mod_c = mod_all[N_DEV]

    rb_shards = {n: _shard_to_rb(n, wsh[n]).astype(BF16) for n in BIG_BF16}
    (w_in_blocks,) = all_gather_multi("gather_w_in", [rb_shards['w_in']])
    later = [n for n in BIG_BF16 if n != 'w_in']
    weights_started, weights_sent = exchange_start("weights_send", 'gather', [rb_shards[n] for n in later],
                                                   after=[w_in_blocks, mod_all])
    for n in REPLICATED:
        if n not in ('c_ctx', 'b_mod'):
            full[n] = wsh[n]

    def arrive(names, after):
        if names == ('w_in',):
            return {'w_in': _rb_from_gathered('w_in', w_in_blocks), '_token': weights_sent}
        picked = [later.index(n) for n in names]
        lands = exchange_wait("weights_wait_" + names[0], 'gather',
                              tuple([part[i] for i in picked] for part in weights_started), after)
        return {n: _rb_from_gathered(n, lax.dynamic_update_slice_in_dim(land, rb_shards[n][None], me, axis=0))
                for n, land in zip(names, lands)}

    in_flight = {}

    def on_grad(n, g):
        chunks = _chunks_from_rb_grad(n, g)
        own = lax.dynamic_index_in_dim(chunks, me, axis=0, keepdims=True)
        started, token = exchange_start("grad_send_" + n, 'scatter', [chunks])
        in_flight[n] = (own, started)
        return token

    loss, grad_x, grads, dmod_l, dmod_c = local_step(x[0], ctx[0], loss_target[0], mod_l, mod_c, full, on_grad,
                                                     arrive)
    dmod = jnp.stack([dmod_l, dmod_c]).reshape(2 * 6 * D // FLAT_C, FLAT_C)
    dm = all_gather("gather_dmod", dmod).reshape(N_DEV, 2, 6 * D)
    dmod_c_tot = dm[0, 1]
    for p in range(1, N_DEV):
        dmod_c_tot = dmod_c_tot + dm[p, 1]
    dm16 = jnp.concatenate([dm[:, 0], dmod_c_tot[None], jnp.zeros((7, 6 * D), F32)], axis=0)
    ncol = 6 * D // N_DEV
    dm16_cols = lax.dynamic_slice_in_dim(dm16.reshape(16, N_DEV, ncol), me, 1, axis=1)[:, 0]
    grad_w_mod = matmul("g_w_mod", sil, dm16_cols, 'tn', F32)
    dsil = matmul("d_cond", dm16_cols, wsh['w_mod'], 'nt', F32)
    sg = jax.nn.sigmoid(c_ctx)
    grads['c_ctx'] = dsil[N_DEV] * (sg * (1.0 + c_ctx * (1.0 - sg)))
    grads['b_mod'] = dmod_l + dmod_c

    g_final = {'w_mod': grad_w_mod}
    reduced, stepped = {}, {}
    for n in BIG_BF16 + ['lru_w_a', 'lru_w_x']:
        own, started = in_flight[n]
        (land,) = exchange_wait("grad_wait_" + n, 'scatter', started, dm)
        if n in ROW_SHARDED:
            g_final[n], *stepped[n] = reduce_slots("step_" + n, land, own, (wsh[n], msh[n], vsh[n]))
        elif n in COL_SHARDED and wsh[n].shape[1] % 128:
            g_t, *outs = reduce_slots("step_" + n, land, own, (wsh[n].T, msh[n].T, vsh[n].T))
            g_final[n], stepped[n] = g_t.T, [o.T for o in outs]
        else:
            reduced[n] = reduce_slots("sum_" + n, land, own)
            if n in BIG_BF16:
                g_final[n] = _rb_to_shard(n, reduced[n])

    small_names = SMALL_F32 + [n for n in REPLICATED if n not in ('lru_w_a', 'lru_w_x')]
    partials = _flat([grads[n] for n in small_names] + [loss], F32, 8)
    parts_all, a_all, x_all = all_gather_multi("gather_small_grads", [partials, reduced['lru_w_a'], reduced['lru_w_x']])
    small_sum = sum_slots("sum_small", parts_all).reshape(-1)
    g_final['lru_w_a'], g_final['lru_w_x'] = a_all.reshape(wsh['lru_w_a'].shape), x_all.reshape(wsh['lru_w_x'].shape)
    at = 0
    for n in small_names:
        cnt = math.prod(full[n].shape) if n in SMALL_F32 else math.prod(wsh[n].shape)
        g = small_sum[at:at + cnt]
        if n in SMALL_F32:
            k = full[n].shape[0]
            g = lax.dynamic_index_in_dim(g.reshape(k, N_DEV, -1), me, axis=1, keepdims=False)
        g_final[n] = g.reshape(wsh[n].shape)
        at += cnt
    loss = small_sum[at]

    for n in ['w_mod'] + BIG_BF16:
        if n not in stepped:
            stepped[n] = adamw("adamw_" + n, wsh[n], g_final[n], msh[n], vsh[n])
    rest = [n for n in WEIGHTS if n not in stepped]
    as2d = lambda a: a.reshape(-1, a.shape[-1])
    rest_out = adamw_many("adamw_small", *[[as2d(d[n]) for n in rest] for d in (wsh, g_final, msh, vsh)])
    stepped.update(zip(rest, rest_out))
    shaped = lambda n, a: a.reshape(given[n].shape)
    return (loss, grad_x[None],
            *[shaped(n, g_final[n]) for n in WEIGHTS],
            *[shaped(n, stepped[n][k]) for k in range(3) for n in WEIGHTS])
```

```python
import functools
import math

import jax
import jax.numpy as jnp
from jax import lax
from jax.experimental import pallas as pl
from jax.experimental.pallas import tpu as pltpu

F32 = jnp.float32
BF16 = jnp.bfloat16
MESH = pl.DeviceIdType.MESH

N_DEV = 8
D = 1024
N_HEADS = 8
HEAD_PAD = 128
QK_NOPE, QK_ROPE, V_HEAD = 64, 32, 64
QK_DIM = QK_NOPE + QK_ROPE
Q_RANK, KV_RANK = 384, 256
LRU_W, LRU_BLOCKS, LRU_BW = 1280, 10, 128
FFN = 2816
GRID_W = 64
ROPE_BASE = 10000.0
LRU_C = 8.0
EPS = 1e-6
Z_Q, Z_KV, Z_KR, Z_XB, Z_YB, Z_GL, Z_END = 0, 384, 640, 768, 2048, 3328, 5376
ADAM_LR, ADAM_B1, ADAM_B2, ADAM_EPS, ADAM_WD, ADAM_STEP = 0.001, 0.9, 0.999, 1e-08, 0.01, 10

VMEM_LIMIT = 52 * 1024 * 1024
FLAT_C = 512

WEIGHTS = ['c_ctx', 'w_mod', 'b_mod', 'norm1_g', 'w_in', 'b_gate', 'q_norm_g', 'kv_norm_g', 'w_uq', 'w_ukv',
           'w_o_attn', 'lru_conv_w', 'lru_conv_b', 'lru_w_a', 'lru_b_a', 'lru_w_x', 'lru_b_x', 'lru_lambda',
           'w_o_lru', 'w_out', 'norm2_g', 'w_up', 'ffn_conv_w', 'ffn_conv_b', 'w_down', 'final_g']
COL_SHARDED = ['w_in', 'w_uq', 'w_ukv', 'w_o_attn', 'lru_conv_w', 'lru_b_a', 'lru_b_x', 'lru_lambda', 'w_up',
               'ffn_conv_w']
ROW_SHARDED = ['w_o_lru', 'w_out', 'w_down']
BIG_BF16 = ['w_in', 'w_uq', 'w_ukv', 'w_o_attn', 'w_o_lru', 'w_out', 'w_up', 'w_down']
SMALL_F32 = ['lru_conv_w', 'lru_b_a', 'lru_b_x', 'lru_lambda', 'ffn_conv_w']
REPLICATED = ['c_ctx', 'b_mod', 'norm1_g', 'b_gate', 'q_norm_g', 'kv_norm_g', 'lru_conv_b', 'lru_w_a', 'lru_w_x',
              'norm2_g', 'ffn_conv_b', 'final_g']


def _cparams(sem=None):
    return pltpu.CompilerParams(dimension_semantics=sem, vmem_limit_bytes=VMEM_LIMIT)


def _pick(n, cands):
    for c in cands:
        if c <= n and n % c == 0:
            return c
    return n


def _best_div(n, mult, cap):
    best = mult
    for d in range(mult, min(n, cap) + 1, mult):
        if n % d == 0:
            best = d
    return best


MXU_DIM = 256
ROW_TILES = (1088, 1024, 544, 512, 256, 128, 64, 32, 16, 8)
LANE_TILES = (2816, 1792, 1536, 1280, 1024, 768, 512, 256, 1408, 896, 640, 384, 128)
DEPTH_ROW_TILES = (2176, 2048, 1024, 512, 256, 1088, 128, 64, 32, 16, 8)
MATMUL_VMEM_BUDGET = 40 * 1024 * 1024
MXU_FILL_OK = 0.9


def _my_pos():
    return lax.axis_index("x"), lax.axis_index("y"), lax.axis_index("c")


def _my_index():
    x, y, c = _my_pos()
    return 4 * x + 2 * y + c


def all_gather_multi(name, shards):
    n_arr = len(shards)
    arrays = range(n_arr)

    def body(*refs):
        x_refs, out_refs = refs[:n_arr], refs[n_arr:2 * n_arr]
        send_sems, recv_sems, local_sems = refs[2 * n_arr:]
        x, y, c = _my_pos()
        me, sibling = (x, y, c), (x, y, 1 - c)
        chips = [(1 - x, y), (x, 1 - y), (1 - x, 1 - y)]

        def slot(a, px, py, pc):
            return out_refs[a].at[4 * px + 2 * py + pc]

        def copy(a, k, block, to, src=None):
            return pltpu.make_async_remote_copy(
                src_ref=slot(a, *block) if src is None else src, dst_ref=slot(a, *block),
                send_sem=send_sems.at[7 * a + k], recv_sem=recv_sems.at[7 * a + k], device_id=to,
                device_id_type=MESH)

        mine = [pltpu.make_async_copy(x_refs[a], slot(a, *me), local_sems.at[a]) for a in arrays]
        first = [copy(a, 1 + j, me, (*chip, c), src=x_refs[a]) for j, chip in enumerate(chips) for a in arrays]
        first += [copy(a, 0, me, sibling, src=x_refs[a]) for a in arrays]
        for cp in first + mine:
            cp.start()
        passed = []
        for j, chip in enumerate(chips):
            for a in arrays:
                copy(a, 1 + j, (*chip, c), me).wait_recv()
                passed.append(copy(a, 4 + j, (*chip, c), sibling))
                passed[-1].start()
        for a in arrays:
            copy(a, 0, sibling, me).wait_recv()
            for j, chip in enumerate(chips):
                copy(a, 4 + j, (*chip, 1 - c), me).wait_recv()
        for cp in first + passed:
            cp.wait_send()
        for cp in mine:
            cp.wait()

    hbm = pl.BlockSpec(memory_space=pl.ANY)
    return pl.pallas_call(
        body, name=name,
        out_shape=[jax.ShapeDtypeStruct((N_DEV,) + s.shape, s.dtype) for s in shards],
        in_specs=[hbm] * n_arr, out_specs=[hbm] * n_arr,
        scratch_shapes=[pltpu.SemaphoreType.DMA((7 * n_arr,)), pltpu.SemaphoreType.DMA((7 * n_arr,)),
                        pltpu.SemaphoreType.DMA((n_arr,))],
    )(*shards)


def all_gather(name, shard):
    return all_gather_multi(name, [shard])[0]


def _peers():
    x, y, c = _my_pos()
    out = []
    for rel in (6, 4, 2, 7, 5, 3, 1):
        px, py, pc = x ^ ((rel >> 2) & 1), y ^ ((rel >> 1) & 1), c ^ (rel & 1)
        out.append((rel - 1, (px, py, pc), 4 * px + 2 * py + pc))
    return out


def _exchange_copies(mode, src_refs, land_refs, send_sems, recv_sems, with_arrivals):
    x, y, c = _my_pos()
    me = 4 * x + 2 * y + c
    sends, arrivals = [], []
    for k, peer_pos, peer in _peers():
        for a, (src, land) in enumerate(zip(src_refs, land_refs)):
            piece = src.at[peer] if mode == 'scatter' else src
            sems = dict(send_sem=send_sems[a].at[k], recv_sem=recv_sems[a].at[k], device_id_type=MESH)
            sends.append(pltpu.make_async_remote_copy(src_ref=piece, dst_ref=land.at[me], device_id=peer_pos, **sems))
            if with_arrivals:
                arrivals.append(pltpu.make_async_remote_copy(src_ref=piece, dst_ref=land.at[peer],
                                                             device_id=(x, y, c), **sems))
    return sends, arrivals


_HBM = pl.BlockSpec(memory_space=pltpu.HBM)
_SEM = pl.BlockSpec(memory_space=pltpu.SEMAPHORE)


def exchange_start(name, mode, arrays, after=()):
    n_arr, n_after = len(arrays), len(after)
    land_shapes = [a.shape if mode == 'scatter' else (N_DEV,) + a.shape for a in arrays]

    def body(*refs):
        src_refs, land_refs = refs[:n_arr], refs[n_arr:2 * n_arr]
        refs = refs[n_after:]
        send_sems, recv_sems = refs[2 * n_arr:3 * n_arr], refs[3 * n_arr:4 * n_arr]
        sends, _ = _exchange_copies(mode, src_refs, land_refs, send_sems, recv_sems, with_arrivals=False)
        for cp in sends:
            cp.start()
        token = refs[-1]
        token[...] = jnp.zeros_like(token)

    sem = pltpu.SemaphoreType.DMA((N_DEV - 1,))
    res = pl.pallas_call(
        body, name=name,
        out_shape=[sem] * (2 * n_arr) + [pltpu.HBM(a.shape, a.dtype) for a in arrays]
        + [pltpu.HBM(s, a.dtype) for s, a in zip(land_shapes, arrays)] + [jax.ShapeDtypeStruct((8, 128), F32)],
        in_specs=[_HBM] * (2 * n_arr) + [pl.BlockSpec(memory_space=pl.ANY)] * n_after,
        out_specs=[_SEM] * (2 * n_arr) + [_HBM] * (2 * n_arr) + [pl.BlockSpec(memory_space=pltpu.VMEM)],
        input_output_aliases={i: 2 * n_arr + i for i in range(2 * n_arr)},
        compiler_params=pltpu.CompilerParams(has_side_effects=pltpu.SideEffectType.DATAFLOW_SIDE_EFFECTING),
    )(*[pltpu.with_memory_space_constraint(a, pltpu.HBM) for a in arrays],
      *[pltpu.with_memory_space_constraint(lax.empty(s, a.dtype), pltpu.HBM) for s, a in zip(land_shapes, arrays)],
      *after)
    return (res[:n_arr], res[n_arr:2 * n_arr], res[2 * n_arr:3 * n_arr], res[3 * n_arr:4 * n_arr]), res[-1]


def exchange_wait(name, mode, started, after):
    send_sems, recv_sems, thru, land = started
    n_arr = len(thru)

    def body(*refs):
        src_refs, land_refs = refs[:n_arr], refs[n_arr:2 * n_arr]
        s_sems, r_sems = refs[2 * n_arr:3 * n_arr], refs[3 * n_arr:4 * n_arr]
        sends, arrivals = _exchange_copies(mode, src_refs, land_refs, s_sems, r_sems, with_arrivals=True)
        for cp in sends:
            cp.wait_send()
        for cp in arrivals:
            cp.wait_recv()

    res = pl.pallas_call(
        body, name=name,
        out_shape=[pltpu.HBM(a.shape, a.dtype) for a in thru] + [pltpu.HBM(a.shape, a.dtype) for a in land],
        in_specs=[_HBM] * (2 * n_arr) + [_SEM] * (2 * n_arr) + [pl.BlockSpec(memory_space=pl.ANY)],
        out_specs=[_HBM] * (2 * n_arr),
        input_output_aliases={i: i for i in range(2 * n_arr)},
        compiler_params=pltpu.CompilerParams(has_side_effects=pltpu.SideEffectType.DATAFLOW_SIDE_EFFECTING),
    )(*thru, *land, *send_sems, *recv_sems, after)
    return res[n_arr:]


def _sum_with_own(slot_ref, own_ref):
    x, y, c = _my_pos()
    me = 4 * x + 2 * y + c
    acc = None
    for p in range(N_DEV):
        v = jnp.where(me == p, own_ref[0], slot_ref[p]).astype(F32)
        acc = v if acc is None else acc + v
    return acc


def reduce_slots(name, slots, own, step=None):
    _, r, ccols = slots.shape
    tc = _pick(ccols, (256, 128))
    c1 = 1.0 - ADAM_B1 ** ADAM_STEP
    c2 = 1.0 - ADAM_B2 ** ADAM_STEP

    def body(s_ref, own_ref, *refs):
        g = _sum_with_own(s_ref, own_ref)
        if step is None:
            refs[0][...] = g
            return
        w_ref, m_ref, v_ref, g_ref, d_ref, nm_ref, nv_ref = refs
        nm = ADAM_B1 * m_ref[...] + (1.0 - ADAM_B1) * g
        nv = ADAM_B2 * v_ref[...] + (1.0 - ADAM_B2) * (g * g)
        g_ref[...] = g
        d_ref[...] = -ADAM_LR * ((nm / c1) / (jnp.sqrt(nv / c2) + ADAM_EPS) + ADAM_WD * w_ref[...])
        nm_ref[...] = nm
        nv_ref[...] = nv

    col = pl.BlockSpec((r, tc), lambda j: (0, j))
    n_out = 1 if step is None else 4
    res = pl.pallas_call(
        body, name=name, grid=(ccols // tc,),
        out_shape=[jax.ShapeDtypeStruct((r, ccols), F32)] * n_out,
        in_specs=[pl.BlockSpec((N_DEV, r, tc), lambda j: (0, 0, j)), pl.BlockSpec((1, r, tc), lambda j: (0, 0, j))]
        + [col] * (0 if step is None else 3),
        out_specs=[col] * n_out,
        compiler_params=_cparams(("parallel",)),
    )(slots, own, *(step or ()))
    return res[0] if step is None else res


def sum_slots(name, slots):
    _, r, ccols = slots.shape
    tc = _pick(ccols, (256, 128))

    def body(s_ref, o_ref):
        acc = s_ref[0].astype(F32)
        for p in range(1, N_DEV):
            acc = acc + s_ref[p].astype(F32)
        o_ref[...] = acc

    return pl.pallas_call(
        body, name=name, grid=(ccols // tc,),
        out_shape=jax.ShapeDtypeStruct((r, ccols), F32),
        in_specs=[pl.BlockSpec((N_DEV, r, tc), lambda j: (0, 0, j))],
        out_specs=pl.BlockSpec((r, tc), lambda j: (0, j)),
        compiler_params=_cparams(("parallel",)),
    )(slots)


def _mxu_fill(t):
    return t / (-(-t // MXU_DIM) * MXU_DIM)


def _matmul_tiles(mode, m_extent, n, k_extent, k_total, itemsizes):
    a_bytes, b_bytes, o_bytes = itemsizes
    m_cands = [c for c in (LANE_TILES if mode == 'tn' else ROW_TILES) if m_extent % c == 0] or [m_extent]
    k_cands = [c for c in (DEPTH_ROW_TILES if mode == 'tn' else LANE_TILES) if k_extent % c == 0] or [k_extent]
    n_cands = [c for c in LANE_TILES if n % c == 0] or [n]
    best = None
    for tm in m_cands:
        for tk in k_cands:
            for tn in n_cands:
                f32_tiles = 2 if k_total // tk > 1 else 1
                vmem = 2 * (tm * tk * a_bytes + tk * tn * b_bytes + tm * tn * o_bytes) + tm * tn * 4 * f32_tiles
                if vmem > MATMUL_VMEM_BUDGET:
                    continue
                key = (_mxu_fill(tn) * _mxu_fill(tk) >= MXU_FILL_OK, tm * tn * tk)
                if best is None or key > best[0]:
                    best = (key, (tm, tn, tk))
    assert best is not None, (mode, m_extent, n, k_extent)
    return best[1]


def matmul(name, a, b, mode, out_dtype, after=()):
    after = [t for t in after if t is not None]
    pieces, a_rows, a_cols = (1,) + a.shape if a.ndim == 2 else a.shape
    if mode == 'nn':
        (m, k), (k2, n) = (a_rows, pieces * a_cols), b.shape
    elif mode == 'nt':
        (m, k), (n, k2) = (a_rows, pieces * a_cols), b.shape
    else:
        (k, m), (k2, n) = (a_rows, pieces * a_cols), b.shape
    assert k == k2, (name, a.shape, b.shape, mode)
    tm, tn, tk = _matmul_tiles(mode, a_cols if mode == 'tn' else m, n, k if mode == 'tn' else a_cols, k,
                               (a.dtype.itemsize, b.dtype.itemsize, jnp.dtype(out_dtype).itemsize))
    nk = k // tk
    per_piece = a_cols // (tm if mode == 'tn' else tk)
    if a.ndim == 2:
        a_block = lambda rows, cols, at: pl.BlockSpec((rows, cols), at)
    else:
        a_block = lambda rows, cols, at: pl.BlockSpec(
            (None, rows, cols), lambda i, j, kk: (at(i, j, kk)[1] // per_piece, at(i, j, kk)[0],
                                                  at(i, j, kk)[1] % per_piece))
    if mode == 'nn':
        a_spec = a_block(tm, tk, lambda i, j, kk: (i, kk))
        b_spec = pl.BlockSpec((tk, tn), lambda i, j, kk: (kk, j))
        dn = (((1,), (0,)), ((), ()))
    elif mode == 'nt':
        a_spec = a_block(tm, tk, lambda i, j, kk: (i, kk))
        b_spec = pl.BlockSpec((tn, tk), lambda i, j, kk: (j, kk))
        dn = (((1,), (1,)), ((), ()))
    else:
        a_spec = a_block(tk, tm, lambda i, j, kk: (kk, i))
        b_spec = pl.BlockSpec((tk, tn), lambda i, j, kk: (kk, j))
        dn = (((0,), (0,)), ((), ()))

    def product(a_ref, b_ref):
        return lax.dot_general(a_ref[...].astype(BF16), b_ref[...].astype(BF16), dn, preferred_element_type=F32)

    n_after = len(after)

    def body_one(a_ref, b_ref, *rest):
        o_ref = rest[n_after]
        o_ref[...] = product(a_ref, b_ref).astype(o_ref.dtype)

    def body(a_ref, b_ref, *rest):
        o_ref, acc_ref = rest[n_after:]
        kk = pl.program_id(2)

        @pl.when(kk == 0)
        def _():
            acc_ref[...] = jnp.zeros_like(acc_ref)

        acc_ref[...] += product(a_ref, b_ref)

        @pl.when(kk == nk - 1)
        def _():
            o_ref[...] = acc_ref[...].astype(o_ref.dtype)

    return pl.pallas_call(
        body_one if nk == 1 else body, name=name, grid=(m // tm, n // tn, nk),
        out_shape=jax.ShapeDtypeStruct((m, n), out_dtype),
        in_specs=[a_spec, b_spec] + [pl.BlockSpec(memory_space=pl.ANY)] * n_after,
        out_specs=pl.BlockSpec((tm, tn), lambda i, j, kk: (i, j)),
        scratch_shapes=[] if nk == 1 else [pltpu.VMEM((tm, tn), F32)],
        compiler_params=_cparams(("parallel", "parallel", "arbitrary")),
    )(a, b, *after)


def rowwise(name, fn, rows, params, out_rows, out_accs, n_rows, t_lat, tm):
    nb, nbl = n_rows // tm, t_lat // tm
    in_specs, piece_counts = [], []
    operands = []
    for arr, off, width, *kind in rows:
        g = math.gcd(off, width) if off else width
        assert g % 128 == 0 or (off == 0 and width == arr.shape[1]), (name, off, width)
        cnt = width // g
        last = arr.shape[0] // tm - 1
        clamp = arr.shape[0] < n_rows
        for p in range(cnt):
            cb = off // g + p
            if kind == ['ctx']:
                in_specs.append(pl.BlockSpec(
                    (tm, g), lambda i, cb=cb, last=last: (jnp.clip(i - nbl, 0, last), cb)))
            elif clamp:
                in_specs.append(pl.BlockSpec((tm, g), lambda i, cb=cb, last=last: (jnp.minimum(i, last), cb)))
            else:
                in_specs.append(pl.BlockSpec((tm, g), lambda i, cb=cb: (i, cb)))
            operands.append(arr)
        piece_counts.append(cnt)
    for p in params:
        in_specs.append(pl.BlockSpec(p.shape, lambda i, nd=p.ndim: (0,) * nd))
        operands.append(p)
    n_in = sum(piece_counts)
    n_par = len(params)
    n_or = len(out_rows)
    lat_only = [kind == ['lat'] for _, _, *kind in out_rows]
    out_shape = [jax.ShapeDtypeStruct((t_lat if lat else n_rows, w), dt)
                 for (w, dt, *_), lat in zip(out_rows, lat_only)]
    out_shape += [jax.ShapeDtypeStruct(s, F32) for s in out_accs]
    out_specs = [pl.BlockSpec((tm, w), (lambda i: (jnp.minimum(i, nbl - 1), 0)) if lat else (lambda i: (i, 0)))
                 for (w, *_), lat in zip(out_rows, lat_only)]
    out_specs += [pl.BlockSpec(s, lambda i, nd=len(s): (0,) * nd) for s in out_accs]

    def body(*refs):
        in_refs, par_refs = refs[:n_in], refs[n_in:n_in + n_par]
        orow_refs = refs[n_in + n_par:n_in + n_par + n_or]
        oacc_refs = refs[n_in + n_par + n_or:]
        i = pl.program_id(0)
        tiles, at = [], 0
        for cnt in piece_counts:
            parts = [in_refs[at + p][...].astype(F32) for p in range(cnt)]
            tiles.append(parts[0] if cnt == 1 else jnp.concatenate(parts, axis=1))
            at += cnt
        is_ctx = i * tm >= t_lat
        outs, accs = fn(is_ctx, tiles, [p[...] for p in par_refs])
        for o_ref, o, lat in zip(orow_refs, outs, lat_only):
            if lat:
                @pl.when(jnp.logical_not(is_ctx))
                def _(o_ref=o_ref, o=o):
                    o_ref[...] = o.astype(o_ref.dtype)
            else:
                o_ref[...] = o.astype(o_ref.dtype)
        if oacc_refs:
            @pl.when(i == 0)
            def _():
                for a_ref in oacc_refs:
                    a_ref[...] = jnp.zeros_like(a_ref)
            for a_ref, a in zip(oacc_refs, accs):
                a_ref[...] += a.astype(F32)

    res = pl.pallas_call(
        body, name=name, grid=(nb,),
        out_shape=out_shape, in_specs=in_specs, out_specs=out_specs,
        compiler_params=_cparams(("arbitrary",)),
    )(*operands)
    return res[:n_or], res[n_or:]


def _rms(x, g):
    return x * lax.rsqrt(jnp.mean(x * x, axis=-1, keepdims=True) + EPS) * g


def _norm_mod(x, g, sc, sh):
    return _rms(x, g) * (1.0 + sc) + sh


def _sigmoid(x):
    return 0.5 * jnp.tanh(0.5 * x) + 0.5


def _silu(x):
    return x * _sigmoid(x)


def _gelu(x):
    return 0.5 * x * (1.0 + jnp.tanh(math.sqrt(2.0 / math.pi) * (x + 0.044715 * (x * x * x))))


def _sel(is_ctx, p):
    return jnp.where(is_ctx, p[1:2], p[0:1])


def _seg_acc(is_ctx, v):
    rows = lax.broadcasted_iota(jnp.int32, (2, v.shape[1]), 0)
    return jnp.where(rows == is_ctx.astype(jnp.int32), jnp.broadcast_to(v, (2, v.shape[1])), 0.0)


def _rsum(v):
    return jnp.sum(v, axis=0, keepdims=True)


def _shift_rows(x, o, t_lat, n):
    if o == 0:
        return x
    y = pltpu.roll(x, (-o) % n, 0)
    t = lax.broadcasted_iota(jnp.int32, x.shape, 0)
    if o > 0:
        ok = t < n - o
        if t_lat < n:
            ok = ok & ((t < t_lat - o) | (t >= t_lat))
    else:
        ok = t >= -o
        if t_lat < n:
            ok = ok & ((t < t_lat) | (t >= t_lat - o))
    return jnp.where(ok, y, 0.0)


def conv_fwd(name, xarr, col_off, width, w, b, left, n_rows, t_lat, out_dtype, cb=128):
    taps = w.shape[0]
    assert col_off % cb == 0 and width % cb == 0

    def body(x_ref, w_ref, b_ref, o_ref):
        x = x_ref[...].astype(F32)
        acc = jnp.broadcast_to(b_ref[...], x.shape)
        for k in range(taps):
            acc = acc + _shift_rows(x, k - left, t_lat, n_rows) * w_ref[k:k + 1, :]
        o_ref[...] = acc.astype(o_ref.dtype)

    return pl.pallas_call(
        body, name=name, grid=(width // cb,),
        out_shape=jax.ShapeDtypeStruct((n_rows, width), out_dtype),
        in_specs=[pl.BlockSpec((n_rows, cb), lambda j: (0, col_off // cb + j)),
                  pl.BlockSpec((taps, cb), lambda j: (0, j)),
                  pl.BlockSpec((1, cb), lambda j: (0, j))],
        out_specs=pl.BlockSpec((n_rows, cb), lambda j: (0, j)),
        compiler_params=_cparams(("parallel",)),
    )(xarr, w, b)


def conv_bwd(name, dout, xarr, col_off, width, w, left, n_rows, t_lat, cb=128):
    taps = w.shape[0]

    def body(d_ref, x_ref, w_ref, dx_ref, dw_ref, db_ref):
        d = d_ref[...].astype(F32)
        x = x_ref[...].astype(F32)
        dx = jnp.zeros_like(d)
        dws = []
        for k in range(taps):
            dx = dx + _shift_rows(d, left - k, t_lat, n_rows) * w_ref[k:k + 1, :]
            dws.append(_rsum(d * _shift_rows(x, k - left, t_lat, n_rows)))
        dx_ref[...] = dx.astype(dx_ref.dtype)
        dw_ref[...] = jnp.concatenate(dws, axis=0)
        db_ref[...] = _rsum(d)

    return pl.pallas_call(
        body, name=name, grid=(width // cb,),
        out_shape=[jax.ShapeDtypeStruct((n_rows, width), BF16), jax.ShapeDtypeStruct((taps, width), F32),
                   jax.ShapeDtypeStruct((1, width), F32)],
        in_specs=[pl.BlockSpec((n_rows, cb), lambda j: (0, j)),
                  pl.BlockSpec((n_rows, cb), lambda j: (0, col_off // cb + j)),
                  pl.BlockSpec((taps, cb), lambda j: (0, j))],
        out_specs=[pl.BlockSpec((n_rows, cb), lambda j: (0, j)), pl.BlockSpec((taps, cb), lambda j: (0, j)),
                   pl.BlockSpec((1, cb), lambda j: (0, j))],
        compiler_params=_cparams(("parallel",)),
    )(dout, xarr, w)


def _ffn_conv(a, w_ref, b_ref, t_lat):
    shifted = [_shift_rows(a, k - 1, t_lat, t_lat) for k in range(3)]
    ac = jnp.broadcast_to(b_ref[...], a.shape)
    for k in range(3):
        ac = ac + shifted[k] * w_ref[k:k + 1, :]
    return ac, shifted


def ffn_mix_fwd(u, w, b, t_lat, cb=128):
    nblk = FFN // cb

    def body(a_ref, g_ref, w_ref, b_ref, f_ref):
        ac, _ = _ffn_conv(a_ref[...].astype(F32), w_ref, b_ref, t_lat)
        f_ref[...] = (_silu(ac) * g_ref[...].astype(F32)).astype(f_ref.dtype)

    col = lambda shape, off=0: pl.BlockSpec(shape, lambda j: (0, off + j))
    return pl.pallas_call(
        body, name="ffn_mix", grid=(nblk,),
        out_shape=jax.ShapeDtypeStruct((t_lat, FFN), BF16),
        in_specs=[col((t_lat, cb)), col((t_lat, cb), nblk), col((3, cb)), col((1, cb))],
        out_specs=col((t_lat, cb)),
        compiler_params=_cparams(("parallel",)),
    )(u, u, w, b)


def ffn_mix_bwd(u, df, w, b, t_lat, cb=128):
    nblk = FFN // cb

    def body(a_ref, g_ref, df_ref, w_ref, b_ref, du_ref, dw_ref, db_ref):
        ac, shifted = _ffn_conv(a_ref[...].astype(F32), w_ref, b_ref, t_lat)
        d = df_ref[...].astype(F32)
        s = _sigmoid(ac)
        du_ref[1] = (d * (ac * s)).astype(du_ref.dtype)
        dac = d * g_ref[...].astype(F32) * (s * (1.0 + ac * (1.0 - s)))
        da = jnp.zeros_like(dac)
        for k in range(3):
            da = da + _shift_rows(dac, 1 - k, t_lat, t_lat) * w_ref[k:k + 1, :]
        du_ref[0] = da.astype(du_ref.dtype)
        dw_ref[...] = jnp.concatenate([_rsum(dac * shifted[k]) for k in range(3)], axis=0)
        db_ref[...] = _rsum(dac)

    col = lambda shape, off=0: pl.BlockSpec(shape, lambda j: (0, off + j))
    return pl.pallas_call(
        body, name="ffn_mix_bwd", grid=(nblk,),
        out_shape=[jax.ShapeDtypeStruct((2, t_lat, FFN), BF16),
                   jax.ShapeDtypeStruct((3, FFN), F32), jax.ShapeDtypeStruct((1, FFN), F32)],
        in_specs=[col((t_lat, cb)), col((t_lat, cb), nblk), col((t_lat, cb)), col((3, cb)), col((1, cb))],
        out_specs=[pl.BlockSpec((2, t_lat, cb), lambda j: (0, 0, j)), col((3, cb)), col((1, cb))],
        compiler_params=_cparams(("parallel",)),
    )(u, u, df, w, b)


def _chunk_order(direction, nb, nbl):
    if direction == 'f':
        return lambda s: ((s + nbl) % nb, 0)
    return lambda s: (nb - 1 - s, 0)


def _adjoint_order(direction, nb, nbl):
    if direction == 'f':
        return lambda s: ((nb - 1 - s + nbl) % nb, 0)
    return lambda s: (s, 0)


SUBLANES = 8


def _chunk_scan(a, b, carry, rev):
    tc, width = a.shape
    nt = tc // SUBLANES
    row = lax.broadcasted_iota(jnp.int32, a.shape, 0)
    a, b = a.reshape(nt, SUBLANES, width), b.reshape(nt, SUBLANES, width)
    in_tile = lax.broadcasted_iota(jnp.int32, a.shape, 1)
    for k in (1, 2, 4):
        shift = SUBLANES - k if rev else k
        edge = in_tile >= SUBLANES - k if rev else in_tile < k
        b = jnp.where(edge, b, a * pltpu.roll(b, shift, 1) + b)
        a = jnp.where(edge, a, a * pltpu.roll(a, shift, 1))
    a, b = a.reshape(tc, width), b.reshape(tc, width)
    hs = [None] * nt
    c = carry
    for kt in range(nt):
        k = nt - 1 - kt if rev else kt
        h = b[k * SUBLANES:(k + 1) * SUBLANES] + a[k * SUBLANES:(k + 1) * SUBLANES] * c
        hs[k] = h
        c = h[0:1] if rev else h[SUBLANES - 1:SUBLANES]
    h = jnp.concatenate(hs, axis=0)
    if rev:
        return h, jnp.where(row == tc - 1, carry, pltpu.roll(h, tc - 1, 0)), c
    return h, jnp.where(row == 0, carry, pltpu.roll(h, 1, 0)), c


def _one_minus_a_squared(log_a, a):
    return (1.0 + a * a) * jnp.tanh(-log_a)


def _gate_elem(pre_r, pre_i, xc, b_a, b_x, sp):
    r = _sigmoid(pre_r + b_a)
    i = _sigmoid(pre_i + b_x)
    log_a = (-LRU_C) * r * sp
    a = jnp.exp(log_a)
    m2 = _one_minus_a_squared(log_a, a)
    mult = jnp.where(m2 > 0.0, m2 * lax.rsqrt(m2), 0.0)
    return a, mult * (i * xc)


def _gate_elem_bwd(pre_r, pre_i, xc, b_a, b_x, sp, da, du):
    r = _sigmoid(pre_r + b_a)
    i = _sigmoid(pre_i + b_x)
    log_a = (-LRU_C) * r * sp
    a = jnp.exp(log_a)
    m2 = _one_minus_a_squared(log_a, a)
    inv_mult = lax.rsqrt(m2)
    g = du * (m2 * inv_mult)
    d_mult = du * (i * xc)
    d_log_a = (da - d_mult * a * inv_mult) * a
    d_pre_r = d_log_a * ((-LRU_C) * sp) * (r * (1.0 - r))
    d_pre_i = g * xc * (i * (1.0 - i))
    return d_pre_r, d_pre_i, g * i, _rsum(d_log_a * ((-LRU_C) * r))


def _blockdiag(xb16, w_ref_val, d):
    outs = []
    for n in range(LRU_BLOCKS):
        outs.append(jnp.dot(xb16[:, n * LRU_BW:(n + 1) * LRU_BW], w_ref_val[d * LRU_BLOCKS + n],
                            preferred_element_type=F32))
    return jnp.concatenate(outs, axis=1)


def lru_scan(name, xc, w_a, w_x, b_a, b_x, sp, direction, n_rows, t_lat):
    w = xc.shape[1]
    d = 0 if direction == 'f' else 1
    tc = _pick(math.gcd(t_lat, n_rows), (256, 128))
    nb, nbl = n_rows // tc, t_lat // tc
    order = _chunk_order(direction, nb, nbl)
    rev = direction == 'b'

    def body(x_ref, wa_ref, wx_ref, ba_ref, bx_ref, sp_ref, a_ref, h_ref, hp_ref, carry):
        @pl.when(pl.program_id(0) == 0)
        def _():
            carry[...] = jnp.zeros_like(carry)

        x = x_ref[...]
        xb16 = x.astype(BF16)
        a, u = _gate_elem(_blockdiag(xb16, wa_ref[...], d), _blockdiag(xb16, wx_ref[...], d), x,
                          ba_ref[d:d + 1, :], bx_ref[d:d + 1, :], sp_ref[d:d + 1, :])
        a_ref[...] = a
        h_ref[...], hp_ref[...], carry[...] = _chunk_scan(a, u, carry[...], rev)

    spec = pl.BlockSpec((tc, w), order)
    whole = lambda p: pl.BlockSpec(p.shape, lambda s, nd=p.ndim: (0,) * nd)
    return pl.pallas_call(
        body, name=name, grid=(nb,),
        out_shape=[jax.ShapeDtypeStruct((n_rows, w), F32)] * 3,
        in_specs=[spec] + [whole(p) for p in (w_a, w_x, b_a, b_x, sp)], out_specs=[spec] * 3,
        scratch_shapes=[pltpu.VMEM((1, w), F32)],
        compiler_params=_cparams(("arbitrary",)),
    )(xc, w_a, w_x, b_a, b_x, sp)


def lru_scan_bwd(name, xc, a, dh, hprev, dxc_in, w_a, w_x, b_a, b_x, sp, direction, n_rows, t_lat):
    w = xc.shape[1]
    d = 0 if direction == 'f' else 1
    tc = _pick(math.gcd(t_lat, n_rows), (256, 128))
    nb, nbl = n_rows // tc, t_lat // tc
    order = _adjoint_order(direction, nb, nbl)
    rev = direction == 'f'
    has_in = dxc_in is not None
    nt_dims, tn_dims = (((1,), (1,)), ((), ())), (((0,), (0,)), ((), ()))

    def dh_order(s):
        c, _ = order(s)
        return (jnp.minimum(c, nbl - 1), 0)

    def body(*refs):
        x_ref, a_ref, dh_ref, hp_ref = refs[:4]
        in_ref = refs[4] if has_in else None
        wa_ref, wx_ref, ba_ref, bx_ref, sp_ref = refs[4 + has_in:9 + has_in]
        dx_ref, dwa_ref, dwx_ref, dba_ref, dbx_ref, dsp_ref, carry = refs[9 + has_in:]
        s = pl.program_id(0)

        @pl.when(s == 0)
        def _():
            carry[...] = jnp.zeros_like(carry)
            for acc in (dwa_ref, dwx_ref, dba_ref, dbx_ref, dsp_ref):
                acc[...] = jnp.zeros_like(acc)

        chunk, _ = order(s)
        live = (chunk < nbl).astype(F32)
        av = a_ref[...]
        dv = dh_ref[...].astype(F32) * live
        _, c_next, carry[...] = _chunk_scan(av, av * dv, carry[...], rev)
        lam = dv + c_next

        x = x_ref[...]
        xb16 = x.astype(BF16)
        wa, wx = wa_ref[...], wx_ref[...]
        dpr, dpi, dxc, dsp_d = _gate_elem_bwd(_blockdiag(xb16, wa, d), _blockdiag(xb16, wx, d), x,
                                              ba_ref[d:d + 1, :], bx_ref[d:d + 1, :], sp_ref[d:d + 1, :],
                                              lam * hp_ref[...], lam)
        dpr16, dpi16 = dpr.astype(BF16), dpi.astype(BF16)
        back = []
        for n in range(LRU_BLOCKS):
            sl = slice(n * LRU_BW, (n + 1) * LRU_BW)
            back.append(lax.dot_general(dpr16[:, sl], wa[d * LRU_BLOCKS + n], nt_dims, preferred_element_type=F32)
                        + lax.dot_general(dpi16[:, sl], wx[d * LRU_BLOCKS + n], nt_dims, preferred_element_type=F32))
            dwa_ref[n] += lax.dot_general(xb16[:, sl], dpr16[:, sl], tn_dims, preferred_element_type=F32)
            dwx_ref[n] += lax.dot_general(xb16[:, sl], dpi16[:, sl], tn_dims, preferred_element_type=F32)
        dxc = dxc + jnp.concatenate(back, axis=1)
        dx_ref[...] = dxc + in_ref[...] if has_in else dxc
        dba_ref[...] += _rsum(dpr)
        dbx_ref[...] += _rsum(dpi)
        dsp_ref[...] += dsp_d

    spec = pl.BlockSpec((tc, w), order)
    whole = lambda shape: pl.BlockSpec(shape, lambda s, nd=len(shape): (0,) * nd)
    params = (w_a, w_x, b_a, b_x, sp)
    acc_shapes = [(LRU_BLOCKS, LRU_BW, LRU_BW)] * 2 + [(1, w)] * 3
    return pl.pallas_call(
        body, name=name, grid=(nb,),
        out_shape=[jax.ShapeDtypeStruct((n_rows, w), F32)] + [jax.ShapeDtypeStruct(sh, F32) for sh in acc_shapes],
        in_specs=[spec, spec, pl.BlockSpec((tc, w), dh_order), spec] + [spec] * has_in
        + [whole(p.shape) for p in params],
        out_specs=[spec] + [whole(sh) for sh in acc_shapes],
        scratch_shapes=[pltpu.VMEM((1, w), F32)],
        compiler_params=_cparams(("arbitrary",)),
    )(xc, a, dh, hprev, *([dxc_in] if has_in else []), *params)


def _rope_tables(t_lat, n_rows):
    rows = t_lat // GRID_W
    row_ids = jnp.repeat(jnp.arange(rows), GRID_W).astype(F32)
    col_ids = jnp.tile(jnp.arange(GRID_W), rows).astype(F32)
    axis_dim = QK_ROPE // 2
    inv = 1.0 / (ROPE_BASE ** (jnp.arange(0, axis_dim, 2, dtype=F32) / axis_dim))
    ang = jnp.concatenate([row_ids[:, None] * inv, col_ids[:, None] * inv], axis=-1)
    cos, sin = jnp.cos(ang), jnp.sin(ang)
    half = QK_ROPE // 2
    ones, zeros = jnp.ones((t_lat, QK_NOPE), F32), jnp.zeros((t_lat, QK_NOPE), F32)
    pad1, pad0 = jnp.ones((t_lat, HEAD_PAD - QK_DIM), F32), jnp.zeros((t_lat, HEAD_PAD - QK_DIM), F32)
    zh = jnp.zeros((t_lat, half), F32)
    c_tab = jnp.concatenate([ones, cos, cos, pad1], axis=1)
    s1 = jnp.concatenate([zeros, -sin, zh, pad0], axis=1)
    s2 = jnp.concatenate([zeros, zh, sin, pad0], axis=1)
    n_ctx = n_rows - t_lat
    c_tab = jnp.concatenate([c_tab, jnp.ones((n_ctx, HEAD_PAD), F32)], axis=0)
    s1 = jnp.concatenate([s1, jnp.zeros((n_ctx, HEAD_PAD), F32)], axis=0)
    s2 = jnp.concatenate([s2, jnp.zeros((n_ctx, HEAD_PAD), F32)], axis=0)
    return c_tab, s1, s2


def _rope(x, c, s1, s2):
    half = QK_ROPE // 2
    return x * c + pltpu.roll(x, HEAD_PAD - half, 1) * s1 + pltpu.roll(x, half, 1) * s2


def _rope_t(dy, c, s1, s2):
    half = QK_ROPE // 2
    return dy * c + pltpu.roll(dy * s1, half, 1) + pltpu.roll(dy * s2, HEAD_PAD - half, 1)


def _heads(x):
    return [x[:, h * HEAD_PAD:(h + 1) * HEAD_PAD] for h in range(N_HEADS)]


Q_SCALE = QK_DIM ** -0.5 * math.log2(math.e)
ATTN_FWD_HEADS = 2

def attn_fwd(q, k, v, t_lat, n_rows, tq):
    def body(q_ref, k_ref, v_ref, o_ref, lse_ref):
        for hh in range(ATTN_FWD_HEADS):
            lanes = slice(hh * HEAD_PAD, (hh + 1) * HEAD_PAD)
            s = lax.dot_general(q_ref[:, lanes], k_ref[:, lanes], (((1,), (1,)), ((), ())),
                                preferred_element_type=F32)
            m = jnp.max(s, axis=-1, keepdims=True)
            p = jnp.exp2(s - m)
            l = jnp.sum(p, axis=-1, keepdims=True)
            o = jnp.dot(p.astype(BF16), v_ref[:, lanes], preferred_element_type=F32) / l
            o_ref[:, lanes] = o.astype(o_ref.dtype)
            lse_ref[:, lanes] = jnp.broadcast_to(m + jnp.log2(l), (tq, HEAD_PAD))

    width = ATTN_FWD_HEADS * HEAD_PAD
    qspec = pl.BlockSpec((tq, width), lambda h, i: (i, h))
    kspec = pl.BlockSpec((n_rows, width), lambda h, i: (0, h))
    return pl.pallas_call(
        body, name="attn_fwd", grid=(N_HEADS // ATTN_FWD_HEADS, t_lat // tq),
        out_shape=[jax.ShapeDtypeStruct((t_lat, N_HEADS * HEAD_PAD), BF16),
                   jax.ShapeDtypeStruct((t_lat, N_HEADS * HEAD_PAD), F32)],
        in_specs=[qspec, kspec, kspec], out_specs=[qspec, qspec],
        compiler_params=_cparams(("parallel", "arbitrary")),
    )(q, k, v)


def attn_bwd(q, k, v, o, do, lse, t_lat, n_rows, tq):
    scale = QK_DIM ** -0.5
    nq = t_lat // tq
    nt = (((1,), (1,)), ((), ()))
    tn = (((0,), (0,)), ((), ()))

    def body(q_ref, k_ref, v_ref, o_ref, do_ref, lse_ref, dq_ref, dk_ref, dv_ref):
        @pl.when(pl.program_id(1) == 0)
        def _():
            dk_ref[...] = jnp.zeros_like(dk_ref)
            dv_ref[...] = jnp.zeros_like(dv_ref)

        qv, kv, vv, dov = q_ref[...], k_ref[...], v_ref[...], do_ref[...]
        s = lax.dot_general(qv, kv, nt, preferred_element_type=F32)
        p = jnp.exp2(s - lse_ref[:, 0:1])
        dv_ref[...] += lax.dot_general(p.astype(BF16), dov, tn, preferred_element_type=F32)
        dp = lax.dot_general(dov, vv, nt, preferred_element_type=F32)
        delta = jnp.sum(dov.astype(F32) * o_ref[...].astype(F32), axis=-1, keepdims=True)
        ds = (p * (dp - delta)).astype(BF16)
        dq_ref[...] = (jnp.dot(ds, kv, preferred_element_type=F32) * scale).astype(dq_ref.dtype)
        dk_ref[...] += lax.dot_general(ds, qv, tn, preferred_element_type=F32)

        @pl.when(pl.program_id(1) == nq - 1)
        def _():
            dk_ref[...] = dk_ref[...] * (scale / Q_SCALE)

    qspec = pl.BlockSpec((tq, HEAD_PAD), lambda h, i: (i, h))
    kspec = pl.BlockSpec((n_rows, HEAD_PAD), lambda h, i: (0, h))
    return pl.pallas_call(
        body, name="attn_bwd", grid=(N_HEADS, t_lat // tq),
        out_shape=[jax.ShapeDtypeStruct((t_lat, N_HEADS * HEAD_PAD), BF16),
                   jax.ShapeDtypeStruct((n_rows, N_HEADS * HEAD_PAD), F32),
                   jax.ShapeDtypeStruct((n_rows, N_HEADS * HEAD_PAD), F32)],
        in_specs=[qspec, kspec, kspec, qspec, qspec, qspec], out_specs=[qspec, kspec, kspec],
        compiler_params=_cparams(("parallel", "arbitrary")),
    )(q, k, v, o, do, lse)


def adamw(name, w, g, m, v):
    r, ccols = w.shape
    if r % 8 == 0:
        tr, tcol = _best_div(r, 8, max(8, 262144 // ccols)), ccols
    else:
        tr, tcol = r, _pick(ccols, (256, 128))
    c1 = 1.0 - ADAM_B1 ** ADAM_STEP
    c2 = 1.0 - ADAM_B2 ** ADAM_STEP

    def body(w_ref, g_ref, m_ref, v_ref, d_ref, nm_ref, nv_ref):
        gv = g_ref[...]
        nm = ADAM_B1 * m_ref[...] + (1.0 - ADAM_B1) * gv
        nv = ADAM_B2 * v_ref[...] + (1.0 - ADAM_B2) * (gv * gv)
        d_ref[...] = -ADAM_LR * ((nm / c1) / (jnp.sqrt(nv / c2) + ADAM_EPS) + ADAM_WD * w_ref[...])
        nm_ref[...] = nm
        nv_ref[...] = nv

    spec = pl.BlockSpec((tr, tcol), lambda i, j: (i, j))
    return pl.pallas_call(
        body, name=name, grid=(r // tr, ccols // tcol),
        out_shape=[jax.ShapeDtypeStruct((r, ccols), F32)] * 3,
        in_specs=[spec] * 4, out_specs=[spec] * 3,
        compiler_params=_cparams(("parallel", "parallel")),
    )(w, g, m, v)


def adamw_many(name, ws, gs, ms, vs):
    n = len(ws)
    c1 = 1.0 - ADAM_B1 ** ADAM_STEP
    c2 = 1.0 - ADAM_B2 ** ADAM_STEP

    def body(*refs):
        for i in range(n):
            w_ref, g_ref, m_ref, v_ref = (refs[k * n + i] for k in range(4))
            d_ref, nm_ref, nv_ref = (refs[(4 + k) * n + i] for k in range(3))
            gv = g_ref[...]
            nm = ADAM_B1 * m_ref[...] + (1.0 - ADAM_B1) * gv
            nv = ADAM_B2 * v_ref[...] + (1.0 - ADAM_B2) * (gv * gv)
            d_ref[...] = -ADAM_LR * ((nm / c1) / (jnp.sqrt(nv / c2) + ADAM_EPS) + ADAM_WD * w_ref[...])
            nm_ref[...] = nm
            nv_ref[...] = nv

    vmem = pl.BlockSpec(memory_space=pltpu.VMEM)
    res = pl.pallas_call(
        body, name=name,
        out_shape=[jax.ShapeDtypeStruct(w.shape, F32) for w in ws] * 3,
        in_specs=[vmem] * (4 * n), out_specs=[vmem] * (3 * n),
        compiler_params=_cparams(),
    )(*ws, *gs, *ms, *vs)
    return [tuple(res[k * n + i] for k in range(3)) for i in range(n)]


def _flat(parts, dtype, row_mult):
    v = jnp.concatenate([p.reshape(-1).astype(dtype) for p in parts])
    quantum = row_mult * FLAT_C
    total = -(-v.shape[0] // quantum) * quantum
    return jnp.pad(v, (0, total - v.shape[0])).reshape(total // FLAT_C, FLAT_C)


def _gathered_to_full(name, g):
    k = g.shape[1]
    return jnp.transpose(g, (1, 0, 2)).reshape(k, N_DEV * g.shape[2])


def _shard_to_rb(name, w):
    return w if name in ROW_SHARDED else w.T


def _rb_to_shard(name, g):
    return g if name in ROW_SHARDED else g.T


def _rb_from_gathered(name, g):
    cols = g.shape[2]
    if name == 'w_in':
        z = lambda k: jnp.zeros((k, cols), g.dtype)
        full = g.reshape(N_DEV * g.shape[1], cols)
        return jnp.concatenate([full[:Z_KR], z(QK_NOPE), full[Z_KR:Z_KR + QK_ROPE], z(HEAD_PAD - QK_DIM),
                                full[Z_KR + QK_ROPE:]], axis=0)
    if name == 'w_uq':
        return jnp.pad(g, ((0, 0), (0, HEAD_PAD - QK_DIM), (0, 0))).reshape(N_HEADS * HEAD_PAD, cols)
    if name == 'w_ukv':
        pad = lambda t: jnp.pad(t, ((0, 0), (0, HEAD_PAD - t.shape[1]), (0, 0))).reshape(N_HEADS * HEAD_PAD, cols)
        return jnp.concatenate([pad(g[:, :QK_NOPE]), pad(g[:, QK_NOPE:])], axis=0)
    if name == 'w_o_attn':
        full = g.reshape(D, N_HEADS, V_HEAD)
        return jnp.pad(full, ((0, 0), (0, 0), (0, HEAD_PAD - V_HEAD))).reshape(D, N_HEADS * HEAD_PAD)
    return g.reshape(N_DEV * g.shape[1], cols)


def _chunks_from_rb_grad(name, g):
    cols = g.shape[1]
    if name == 'w_in':
        full = jnp.concatenate([g[:Z_KR], g[Z_KR + QK_NOPE:Z_KR + QK_DIM], g[Z_XB:]], axis=0)
        return full.reshape(N_DEV, -1, cols)
    if name == 'w_uq':
        return g.reshape(N_HEADS, HEAD_PAD, cols)[:, :QK_DIM]
    if name == 'w_ukv':
        half = N_HEADS * HEAD_PAD
        gk = g[:half].reshape(N_HEADS, HEAD_PAD, cols)[:, :QK_NOPE]
        gv = g[half:].reshape(N_HEADS, HEAD_PAD, cols)[:, :V_HEAD]
        return jnp.concatenate([gk, gv], axis=1)
    if name == 'w_o_attn':
        full = g.reshape(D, N_HEADS, HEAD_PAD)[:, :, :V_HEAD].reshape(D, N_HEADS * V_HEAD)
        return full.reshape(N_DEV, D // N_DEV, N_HEADS * V_HEAD)
    return g.reshape(N_DEV, -1, cols)


def local_step(x, ctx, target, mod_l, mod_c, wt, on_grad=None, arrive=None):
    t_lat, n_ctx = x.shape[0], ctx.shape[0]
    n = t_lat + n_ctx
    tm = _pick(math.gcd(t_lat, n), (256, 128))
    tq_fwd = _pick(t_lat, (256, 128))
    tq_bwd = _pick(t_lat, (512, 256, 128))
    row = lambda v: v.reshape(1, -1).astype(F32)
    two = lambda a, b: jnp.stack([a, b]).astype(F32)
    sh1_l, sc1_l, g1_l, sh2_l, sc2_l, g2_l = jnp.split(mod_l, 6)
    sh1_c, sc1_c = jnp.split(mod_c, 6)[:2]
    sc1, sh1 = two(sc1_l, sc1_c), two(sh1_l, sh1_c)
    g1, g2, sc2, sh2 = row(g1_l), row(g2_l), row(sc2_l), row(sh2_l)
    norm1_g, norm2_g, final_g = row(wt['norm1_g']), row(wt['norm2_g']), row(wt['final_g'])
    q_g, kv_g, b_gate = row(wt['q_norm_g']), row(wt['kv_norm_g']), row(wt['b_gate'])
    wt = dict(wt)
    pending = []

    def sent():
        tokens = list(pending)
        pending.clear()
        return tokens

    def need(names, after):
        if arrive is not None:
            got = arrive(names, after)
            if '_token' in got:
                pending.append(got.pop('_token'))
            wt.update(got)
        return [wt[n] for n in names]
    lru_w_a = wt['lru_w_a'].reshape(2 * LRU_BLOCKS, LRU_BW, LRU_BW).astype(BF16)
    lru_w_x = wt['lru_w_x'].reshape(2 * LRU_BLOCKS, LRU_BW, LRU_BW).astype(BF16)
    b_a, b_x, lam = wt['lru_b_a'], wt['lru_b_x'], wt['lru_lambda']
    sp = jnp.logaddexp(-lam, 0.0)
    c_tab, s1_tab, s2_tab = _rope_tables(t_lat, n)
    rw = functools.partial(rowwise, n_rows=n, t_lat=t_lat, tm=tm)
    rw_lat = functools.partial(rowwise, n_rows=t_lat, t_lat=t_lat, tm=_pick(t_lat, (512, 256, 128)))

    stream = [(x, 0, D), (ctx, 0, D, 'ctx')]

    def f_norm1(is_ctx, rows, params):
        (xl, xc_), (g, sc, sh) = rows, params
        return [_norm_mod(jnp.where(is_ctx, xc_, xl), g, _sel(is_ctx, sc), _sel(is_ctx, sh))], []

    (h,), _ = rw("norm1", f_norm1, stream, [norm1_g, sc1, sh1], [(D, BF16)], [])
    (w_in_t,) = need(('w_in',), h)
    z = matmul("w_in", h, w_in_t, 'nt', BF16, after=sent())
    w_uq_t, w_ukv_t, w_o_lru = need(('w_uq', 'w_ukv', 'w_o_lru'), z)

    def f_qkv_norm(is_ctx, rows, params):
        (ql, kvl), (gq, gkv) = rows, params
        return [_rms(ql, gq), _rms(kvl, gkv)], []

    (qn, kvn), _ = rw("qkv_norm", f_qkv_norm, [(z, Z_Q, Q_RANK), (z, Z_KV, KV_RANK)], [q_g, kv_g],
                      [(Q_RANK, BF16), (KV_RANK, BF16)], [])
    qp = matmul("w_uq", qn, w_uq_t, 'nt', BF16)
    kvp = matmul("w_ukv", kvn, w_ukv_t, 'nt', BF16)

    def f_rope(is_ctx, rows, params):
        qv, kk, vv, kr, c, s1, s2 = rows
        krr = _rope(kr, c, s1, s2)
        qo = jnp.concatenate([_rope(qh, c, s1, s2) for qh in _heads(qv)], axis=1) * Q_SCALE
        ko = jnp.concatenate([kh + krr for kh in _heads(kk)], axis=1)
        return [qo, ko, vv], []

    hp = N_HEADS * HEAD_PAD
    (qr, kr_, vr), _ = rw("rope", f_rope,
                          [(qp, 0, hp), (kvp, 0, hp), (kvp, hp, hp), (z, Z_KR, HEAD_PAD), (c_tab, 0, HEAD_PAD),
                           (s1_tab, 0, HEAD_PAD), (s2_tab, 0, HEAD_PAD)], [], [(hp, BF16)] * 3, [])
    attn, lse = attn_fwd(qr, kr_, vr, t_lat, n, tq_fwd)

    xc = conv_fwd("lru_conv", z, Z_XB, LRU_W, wt['lru_conv_w'], row(wt['lru_conv_b']), 2, n, t_lat, F32)
    a_f, h_f, hp_f = lru_scan("lru_scan_f", xc, lru_w_a, lru_w_x, b_a, b_x, sp, 'f', n, t_lat)
    a_b, h_b, hp_b = lru_scan("lru_scan_b", xc, lru_w_a, lru_w_x, b_a, b_x, sp, 'b', n, t_lat)

    def f_lru_out(is_ctx, rows, params):
        hf, hb, yb = rows
        return [(hf + hb) * _gelu(yb)], []

    (ybin,), _ = rw_lat("lru_out", f_lru_out, [(h_f, 0, LRU_W), (h_b, 0, LRU_W), (z, Z_YB, LRU_W)], [],
                        [(LRU_W, BF16)], [])
    w_o_attn_t, w_out, w_up_t, w_down = need(('w_o_attn', 'w_out', 'w_up', 'w_down'), attn)
    y_a = matmul("w_o_attn", attn, w_o_attn_t, 'nt', BF16)
    y_b = matmul("w_o_lru", ybin, w_o_lru, 'nn', BF16)

    def _merge(ya, yb, gl, bg):
        gates = _sigmoid(gl + bg)
        return gates[:, :D] * ya + gates[:, D:] * yb

    def f_merge(is_ctx, rows, params):
        (ya, yb, gl), (bg,) = rows, params
        return [_merge(ya, yb, gl, bg)], []

    (mrg,), _ = rw_lat("merge", f_merge, [(y_a, 0, D), (y_b, 0, D), (z, Z_GL, 2 * D)], [b_gate], [(D, BF16)], [])
    o = matmul("w_out", mrg, w_out, 'nn', BF16)

    def _res_norm2(xv, ov, g1v, g, sc, sh):
        x1 = xv + g1v * ov
        return x1, _norm_mod(x1, g, sc, sh)

    def f_norm2(is_ctx, rows, params):
        (xv, ov), (g1v, g, sc, sh) = rows, params
        x1, h2v = _res_norm2(xv, ov, g1v, g, sc, sh)
        return [x1, h2v], []

    (x1, h2), _ = rw_lat("norm2", f_norm2, [(x, 0, D), (o, 0, D)], [g1, norm2_g, sc2, sh2], [(D, F32), (D, BF16)], [])
    u = matmul("w_up", h2, w_up_t, 'nt', BF16)
    f = ffn_mix_fwd(u, wt['ffn_conv_w'], row(wt['ffn_conv_b']), t_lat)
    dn = matmul("w_down", f, w_down, 'nn', BF16)

    def _tile_loss(x1v, dv, g2v, fg, tgt):
        y = _rms(x1v + g2v * dv, fg)
        e = y - tgt
        return 0.5 * jnp.sum(jnp.mean(e * e, axis=-1, keepdims=True), axis=0, keepdims=True)

    def f_final(is_ctx, rows, params):
        (x1v, dv, tgt), (g2v, fg) = rows, params
        lv, vjp = jax.vjp(lambda a, b, c, d: _tile_loss(a, b, c, d, tgt), x1v, dv, g2v, fg)
        dx2, dd, dg2, dfg = vjp(jnp.ones((1, 1), F32))
        return [dx2, dd], [dg2, dfg, jnp.broadcast_to(lv, (1, 128))]

    (dx2, dd), (dg2, dfinal_g, loss_v) = rw_lat("final", f_final, [(x1, 0, D), (dn, 0, D), (target, 0, D)],
                                                [g2, final_g], [(D, F32), (D, BF16)], [(1, D), (1, D), (1, 128)])
    loss = loss_v[0, 0]

    grads = {'final_g': dfinal_g}

    def put(name, g):
        grads[name] = g
        if on_grad is not None:
            pending.append(on_grad(name, g))
    df = matmul("d_f", dd, w_down, 'nt', BF16)
    put('w_down', matmul("g_w_down", f, dd, 'tn', BF16))

    du, grads['ffn_conv_w'], grads['ffn_conv_b'] = ffn_mix_bwd(u, df, wt['ffn_conv_w'], row(wt['ffn_conv_b']),
                                                               t_lat)
    dh2 = matmul("d_h2", du, w_up_t, 'nn', BF16, after=sent())
    put('w_up', matmul("g_w_up", du, h2, 'tn', BF16))

    def b_norm2(is_ctx, rows, params):
        (xv, ov, dh2v, dx2v), (g1v, g, sc, sh) = rows, params
        _, vjp = jax.vjp(_res_norm2, xv, ov, g1v, g, sc, sh)
        dx, do, dg1v, dg, dsc, dsh = vjp((dx2v, dh2v))
        return [dx, do], [dg1v, dg, dsc, dsh]

    (dx_res, do), (dg1, dnorm2_g, dsc2, dsh2) = rw_lat(
        "norm2_bwd", b_norm2, [(x, 0, D), (o, 0, D), (dh2, 0, D), (dx2, 0, D)], [g1, norm2_g, sc2, sh2],
        [(D, F32), (D, BF16)], [(1, D)] * 4)
    grads['norm2_g'] = dnorm2_g
    dmrg = matmul("d_merge", do, w_out, 'nt', BF16, after=sent())
    put('w_out', matmul("g_w_out", mrg, do, 'tn', BF16))

    def b_merge(is_ctx, rows, params):
        (ya, yb, gl, dm), (bg,) = rows, params
        _, vjp = jax.vjp(_merge, ya, yb, gl, bg)
        dya, dyb, dgl, dbg = vjp(dm)
        return [dya, dyb, dgl], [dbg]

    (dy_a, dy_b, dgl), (grads['b_gate'],) = rw_lat(
        "merge_bwd", b_merge, [(y_a, 0, D), (y_b, 0, D), (z, Z_GL, 2 * D), (dmrg, 0, D)], [b_gate],
        [(D, BF16), (D, BF16), (2 * D, BF16)], [(1, 2 * D)])
    dattn = matmul("d_attn", dy_a, w_o_attn_t, 'nn', BF16, after=sent())
    put('w_o_attn', matmul("g_w_o_attn", dy_a, attn, 'tn', BF16))
    dybin = matmul("d_lru_out", dy_b, w_o_lru, 'nt', BF16, after=sent())
    put('w_o_lru', matmul("g_w_o_lru", ybin, dy_b, 'tn', BF16))

    def b_lru_out(is_ctx, rows, params):
        hf, hb, yb, dyv = rows
        _, vjp = jax.vjp(lambda s, y: s * _gelu(y), hf + hb, yb)
        dh, dyb = vjp(dyv)
        return [dh, dyb], []

    (dh_lru, dyb), _ = rw_lat("lru_out_bwd", b_lru_out,
                              [(h_f, 0, LRU_W), (h_b, 0, LRU_W), (z, Z_YB, LRU_W), (dybin, 0, LRU_W)], [],
                              [(LRU_W, F32), (LRU_W, BF16)], [])
    gate_params = (lru_w_a, lru_w_x, b_a, b_x, sp)
    dxc_f, *sums_f = lru_scan_bwd("lru_scan_f_bwd", xc, a_f, dh_lru, hp_f, None, *gate_params, 'f', n, t_lat)
    dxc, *sums_b = lru_scan_bwd("lru_scan_b_bwd", xc, a_b, dh_lru, hp_b, dxc_f, *gate_params, 'b', n, t_lat)
    dw_a, dw_x, db_a, db_x, dsp = (jnp.concatenate([f_, b_], axis=0) for f_, b_ in zip(sums_f, sums_b))
    put('lru_w_a', dw_a.reshape(2 * LRU_BLOCKS * LRU_BW, LRU_BW).astype(BF16))
    put('lru_w_x', dw_x.reshape(2 * LRU_BLOCKS * LRU_BW, LRU_BW).astype(BF16))
    grads['lru_b_a'], grads['lru_b_x'] = db_a, db_x
    grads['lru_lambda'] = -dsp * _sigmoid(-lam)
    dxb, grads['lru_conv_w'], grads['lru_conv_b'] = conv_bwd("lru_conv_bwd", dxc, z, Z_XB, LRU_W, wt['lru_conv_w'],
                                                             2, n, t_lat)

    dq, dk, dv = attn_bwd(qr, kr_, vr, attn, dattn, lse, t_lat, n, tq_bwd)

    def b_rope(is_ctx, rows, params):
        dqv, dkv, dvv, c, s1, s2 = rows
        live = jnp.where(is_ctx, 0.0, 1.0)
        dqo = jnp.concatenate([_rope_t(dqh, c, s1, s2) for dqh in _heads(dqv)], axis=1) * live
        dkh = _heads(dkv)
        dkr = dkh[0]
        for t in dkh[1:]:
            dkr = dkr + t
        lanes = lax.broadcasted_iota(jnp.int32, dkr.shape, 1)
        dkr = jnp.where((lanes >= QK_NOPE) & (lanes < QK_DIM), _rope_t(dkr, c, s1, s2), 0.0)
        return [dqo, jnp.concatenate([dkv, dvv], axis=1), dkr], []

    (dqp, dkvp, dkr), _ = rw("rope_bwd", b_rope,
                             [(dq, 0, hp), (dk, 0, hp), (dv, 0, hp), (c_tab, 0, HEAD_PAD), (s1_tab, 0, HEAD_PAD),
                              (s2_tab, 0, HEAD_PAD)], [], [(hp, BF16), (2 * hp, BF16), (HEAD_PAD, BF16)], [])
    dqn = matmul("d_qn", dqp, w_uq_t, 'nn', BF16, after=sent())
    put('w_uq', matmul("g_w_uq", dqp, qn, 'tn', BF16))
    dkvn = matmul("d_kvn", dkvp, w_ukv_t, 'nn', BF16, after=sent())
    put('w_ukv', matmul("g_w_ukv", dkvp, kvn, 'tn', BF16))

    def b_qkv_norm(is_ctx, rows, params):
        (ql, kvl, dqv, dkvv), (gq, gkv) = rows, params
        _, vjp_q = jax.vjp(_rms, ql, gq)
        _, vjp_kv = jax.vjp(_rms, kvl, gkv)
        dql, dgq = vjp_q(dqv)
        dkvl, dgkv = vjp_kv(dkvv)
        return [dql, dkvl], [dgq, dgkv]

    (dq_lat, dkv_lat), (grads['q_norm_g'], grads['kv_norm_g']) = rw(
        "qkv_norm_bwd", b_qkv_norm, [(z, Z_Q, Q_RANK), (z, Z_KV, KV_RANK), (dqn, 0, Q_RANK), (dkvn, 0, KV_RANK)],
        [q_g, kv_g], [(Q_RANK, BF16), (KV_RANK, BF16)], [(1, Q_RANK), (1, KV_RANK)])
    pad_ctx = lambda t: jnp.pad(t, ((0, n_ctx), (0, 0)))
    dz = jnp.concatenate([dq_lat, dkv_lat, dkr, dxb, pad_ctx(dyb), pad_ctx(dgl)], axis=1)
    put('w_in', matmul("g_w_in", dz, h, 'tn', BF16))
    dh = matmul("d_h", dz, w_in_t, 'nn', BF16, after=sent())

    def b_norm1(is_ctx, rows, params):
        (xl, xc_, dhv, dxr), (g, sc, sh) = rows, params
        scv, shv = _sel(is_ctx, sc), _sel(is_ctx, sh)
        _, vjp = jax.vjp(_norm_mod, jnp.where(is_ctx, xc_, xl), g, scv, shv)
        dx, dg, dsc, dsh = vjp(dhv)
        return [dx + dxr], [dg, _seg_acc(is_ctx, dsc), _seg_acc(is_ctx, dsh)]

    (grad_x,), (grads['norm1_g'], dsc1, dsh1) = rw("norm1_bwd", b_norm1, stream + [(dh, 0, D), (dx_res, 0, D)],
                                                   [norm1_g, sc1, sh1], [(D, F32, 'lat')],
                                                   [(1, D), (2, D), (2, D)])
    zero = jnp.zeros((D,), F32)
    dmod_l = jnp.concatenate([dsh1[0], dsc1[0], dg1[0], dsh2[0], dsc2[0], dg2[0]])
    dmod_c = jnp.concatenate([dsh1[1], dsc1[1], zero, zero, zero, zero])
    return loss, grad_x, grads, dmod_l, dmod_c


def kernel(x, c, ctx, c_ctx, w_mod, b_mod, norm1_g, w_in, b_gate, q_norm_g, kv_norm_g, w_uq, w_ukv, w_o_attn, lru_conv_w, lru_conv_b, lru_w_a, lru_b_a, lru_w_x, lru_b_x, lru_lambda, w_o_lru, w_out, norm2_g, w_up, ffn_conv_w, ffn_conv_b, w_down, final_g, loss_target, m_c_ctx, m_w_mod, m_b_mod, m_norm1_g, m_w_in, m_b_gate, m_q_norm_g, m_kv_norm_g, m_w_uq, m_w_ukv, m_w_o_attn, m_lru_conv_w, m_lru_conv_b, m_lru_w_a, m_lru_b_a, m_lru_w_x, m_lru_b_x, m_lru_lambda, m_w_o_lru, m_w_out, m_norm2_g, m_w_up, m_ffn_conv_w, m_ffn_conv_b, m_w_down, m_final_g, v_c_ctx, v_w_mod, v_b_mod, v_norm1_g, v_w_in, v_b_gate, v_q_norm_g, v_kv_norm_g, v_w_uq, v_w_ukv, v_w_o_attn, v_lru_conv_w, v_lru_conv_b, v_lru_w_a, v_lru_b_a, v_lru_w_x, v_lru_b_x, v_lru_lambda, v_w_o_lru, v_w_out, v_norm2_g, v_w_up, v_ffn_conv_w, v_ffn_conv_b, v_w_down, v_final_g):
    given = dict(locals())
    strip = lambda name, a: a if name in ('c_ctx', 'final_g') else a[0]
    wsh = {n: strip(n, given[n]) for n in WEIGHTS}
    msh = {n: strip(n, given['m_' + n]) for n in WEIGHTS}
    vsh = {n: strip(n, given['v_' + n]) for n in WEIGHTS}
    me = _my_index()

    small = _flat([c[0]] + [wsh[n] for n in SMALL_F32], F32, 8)
    small_all = all_gather("gather_small", small).reshape(N_DEV, -1)
    c_all = small_all[:, :D]
    full, at = {}, D
    for n in SMALL_F32:
        cnt = math.prod(wsh[n].shape)
        full[n] = _gathered_to_full(n, small_all[:, at:at + cnt].reshape((N_DEV,) + wsh[n].shape))
        at += cnt

    cond = jnp.concatenate([c_all, c_ctx[None], jnp.zeros((7, D), F32)], axis=0)
    sil = cond * jax.nn.sigmoid(cond)
    mod_cols = matmul("mod_proj", sil, wsh['w_mod'], 'nn', F32)
    mod_all = all_gather("gather_mod", mod_cols)
    mod_all = jnp.transpose(mod_all, (1, 0, 2)).reshape(16, 6 * D) + b_mod[0][None]
    mod_l = lax.dynamic_index_in_dim(mod_all, me, axis=0, keepdims=False)
    mod_c = mod_all[N_DEV]

    rb_shards = {n: _shard_to_rb(n, wsh[n]).astype(BF16) for n in BIG_BF16}
    (w_in_blocks,) = all_gather_multi("gather_w_in", [rb_shards['w_in']])
    later = [n for n in BIG_BF16 if n != 'w_in']
    weights_started, weights_sent = exchange_start("weights_send", 'gather', [rb_shards[n] for n in later],
                                                   after=[w_in_blocks, mod_all])
    for n in REPLICATED:
        if n not in ('c_ctx', 'b_mod'):
            full[n] = wsh[n]

    def arrive(names, after):
        if names == ('w_in',):
            return {'w_in': _rb_from_gathered('w_in', w_in_blocks), '_token': weights_sent}
        picked = [later.index(n) for n in names]
        lands = exchange_wait("weights_wait_" + names[0], 'gather',
                              tuple([part[i] for i in picked] for part in weights_started), after)
        return {n: _rb_from_gathered(n, lax.dynamic_update_slice_in_dim(land, rb_shards[n][None], me, axis=0))
                for n, land in zip(names, lands)}

    in_flight = {}

    def on_grad(n, g):
        chunks = _chunks_from_rb_grad(n, g)
        own = lax.dynamic_index_in_dim(chunks, me, axis=0, keepdims=True)
        started, token = exchange_start("grad_send_" + n, 'scatter', [chunks])
        in_flight[n] = (own, started)
        return token

    loss, grad_x, grads, dmod_l, dmod_c = local_step(x[0], ctx[0], loss_target[0], mod_l, mod_c, full, on_grad,
                                                     arrive)
    dmod = jnp.stack([dmod_l, dmod_c]).reshape(2 * 6 * D // FLAT_C, FLAT_C)
    dm = all_gather("gather_dmod", dmod).reshape(N_DEV, 2, 6 * D)
    dmod_c_tot = dm[0, 1]
    for p in range(1, N_DEV):
        dmod_c_tot = dmod_c_tot + dm[p, 1]
    dm16 = jnp.concatenate([dm[:, 0], dmod_c_tot[None], jnp.zeros((7, 6 * D), F32)], axis=0)
    ncol = 6 * D // N_DEV
    dm16_cols = lax.dynamic_slice_in_dim(dm16.reshape(16, N_DEV, ncol), me, 1, axis=1)[:, 0]
    grad_w_mod = matmul("g_w_mod", sil, dm16_cols, 'tn', F32)
    dsil = matmul("d_cond", dm16_cols, wsh['w_mod'], 'nt', F32)
    sg = jax.nn.sigmoid(c_ctx)
    grads['c_ctx'] = dsil[N_DEV] * (sg * (1.0 + c_ctx * (1.0 - sg)))
    grads['b_mod'] = dmod_l + dmod_c

    g_final = {'w_mod': grad_w_mod}
    reduced, stepped = {}, {}
    for n in BIG_BF16 + ['lru_w_a', 'lru_w_x']:
        own, started = in_flight[n]
        (land,) = exchange_wait("grad_wait_" + n, 'scatter', started, dm)
        if n in ROW_SHARDED:
            g_final[n], *stepped[n] = reduce_slots("step_" + n, land, own, (wsh[n], msh[n], vsh[n]))
        elif n in COL_SHARDED and wsh[n].shape[1] % 128:
            g_t, *outs = reduce_slots("step_" + n, land, own, (wsh[n].T, msh[n].T, vsh[n].T))
            g_final[n], stepped[n] = g_t.T, [o.T for o in outs]
        else:
            reduced[n] = reduce_slots("sum_" + n, land, own)
            if n in BIG_BF16:
                g_final[n] = _rb_to_shard(n, reduced[n])

    small_names = SMALL_F32 + [n for n in REPLICATED if n not in ('lru_w_a', 'lru_w_x')]
    partials = _flat([grads[n] for n in small_names] + [loss], F32, 8)
    parts_all, a_all, x_all = all_gather_multi("gather_small_grads", [partials, reduced['lru_w_a'], reduced['lru_w_x']])
    small_sum = sum_slots("sum_small", parts_all).reshape(-1)
    g_final['lru_w_a'], g_final['lru_w_x'] = a_all.reshape(wsh['lru_w_a'].shape), x_all.reshape(wsh['lru_w_x'].shape)
    at = 0
    for n in small_names:
        cnt = math.prod(full[n].shape) if n in SMALL_F32 else math.prod(wsh[n].shape)
        g = small_sum[at:at + cnt]
        if n in SMALL_F32:
            k = full[n].shape[0]
            g = lax.dynamic_index_in_dim(g.reshape(k, N_DEV, -1), me, axis=1, keepdims=False)
        g_final[n] = g.reshape(wsh[n].shape)
        at += cnt
    loss = small_sum[at]

    for n in ['w_mod'] + BIG_BF16:
        if n not in stepped:
            stepped[n] = adamw("adamw_" + n, wsh[n], g_final[n], msh[n], vsh[n])
    rest = [n for n in WEIGHTS if n not in stepped]
    as2d = lambda a: a.reshape(-1, a.shape[-1])
    rest_out = adamw_many("adamw_small", *[[as2d(d[n]) for n in rest] for d in (wsh, g_final, msh, vsh)])
    stepped.update(zip(rest, rest_out))
    shaped = lambda n, a: a.reshape(given[n].shape)
    return (loss, grad_x[None],
            *[shaped(n, g_final[n]) for n in WEIGHTS],
            *[shaped(n, stepped[n][k]) for k in range(3) for n in WEIGHTS])
```

```python
import functools
import math

import jax
import jax.numpy as jnp
from jax import lax
from jax.experimental import pallas as pl
from jax.experimental.pallas import tpu as pltpu

F32 = jnp.float32
BF16 = jnp.bfloat16
MESH = pl.DeviceIdType.MESH

N_DEV = 8
D = 1024
N_HEADS = 8
HEAD_PAD = 128
QK_NOPE, QK_ROPE, V_HEAD = 64, 32, 64
QK_DIM = QK_NOPE + QK_ROPE
Q_RANK, KV_RANK = 384, 256
LRU_W, LRU_BLOCKS, LRU_BW = 1280, 10, 128
FFN = 2816
GRID_W = 64
ROPE_BASE = 10000.0
LRU_C = 8.0
EPS = 1e-6
Z_Q, Z_KV, Z_KR, Z_XB, Z_YB, Z_GL, Z_END = 0, 384, 640, 768, 2048, 3328, 5376
ADAM_LR, ADAM_B1, ADAM_B2, ADAM_EPS, ADAM_WD, ADAM_STEP = 0.001, 0.9, 0.999, 1e-08, 0.01, 10

VMEM_LIMIT = 52 * 1024 * 1024
FLAT_C = 512

WEIGHTS = ['c_ctx', 'w_mod', 'b_mod', 'norm1_g', 'w_in', 'b_gate', 'q_norm_g', 'kv_norm_g', 'w_uq', 'w_ukv',
           'w_o_attn', 'lru_conv_w', 'lru_conv_b', 'lru_w_a', 'lru_b_a', 'lru_w_x', 'lru_b_x', 'lru_lambda',
           'w_o_lru', 'w_out', 'norm2_g', 'w_up', 'ffn_conv_w', 'ffn_conv_b', 'w_down', 'final_g']
COL_SHARDED = ['w_in', 'w_uq', 'w_ukv', 'w_o_attn', 'lru_conv_w', 'lru_b_a', 'lru_b_x', 'lru_lambda', 'w_up',
               'ffn_conv_w']
ROW_SHARDED = ['w_o_lru', 'w_out', 'w_down']
BIG_BF16 = ['w_in', 'w_uq', 'w_ukv', 'w_o_attn', 'w_o_lru', 'w_out', 'w_up', 'w_down']
SMALL_F32 = ['lru_conv_w', 'lru_b_a', 'lru_b_x', 'lru_lambda', 'ffn_conv_w']
REPLICATED = ['c_ctx', 'b_mod', 'norm1_g', 'b_gate', 'q_norm_g', 'kv_norm_g', 'lru_conv_b', 'lru_w_a', 'lru_w_x',
              'norm2_g', 'ffn_conv_b', 'final_g']


def _cparams(sem=None):
    return pltpu.CompilerParams(dimension_semantics=sem, vmem_limit_bytes=VMEM_LIMIT)


def _pick(n, cands):
    for c in cands:
        if c <= n and n % c == 0:
            return c
    return n


def _best_div(n, mult, cap):
    best = mult
    for d in range(mult, min(n, cap) + 1, mult):
        if n % d == 0:
            best = d
    return best


MXU_DIM = 256
ROW_TILES = (1088, 1024, 544, 512, 256, 128, 64, 32, 16, 8)
LANE_TILES = (2816, 1792, 1536, 1280, 1024, 768, 512, 256, 1408, 896, 640, 384, 128)
DEPTH_ROW_TILES = (2176, 2048, 1024, 512, 256, 1088, 128, 64, 32, 16, 8)
MATMUL_VMEM_BUDGET = 40 * 1024 * 1024
MXU_FILL_OK = 0.9


def _my_pos():
    return lax.axis_index("x"), lax.axis_index("y"), lax.axis_index("c")


def _my_index():
    x, y, c = _my_pos()
    return 4 * x + 2 * y + c


def all_gather_multi(name, shards):
    n_arr = len(shards)
    arrays = range(n_arr)

    def body(*refs):
        x_refs, out_refs = refs[:n_arr], refs[n_arr:2 * n_arr]
        send_sems, recv_sems, local_sems = refs[2 * n_arr:]
        x, y, c = _my_pos()
        me, sibling = (x, y, c), (x, y, 1 - c)
        chips = [(1 - x, y), (x, 1 - y), (1 - x, 1 - y)]

        def slot(a, px, py, pc):
            return out_refs[a].at[4 * px + 2 * py + pc]

        def copy(a, k, block, to, src=None):
            return pltpu.make_async_remote_copy(
                src_ref=slot(a, *block) if src is None else src, dst_ref=slot(a, *block),
                send_sem=send_sems.at[7 * a + k], recv_sem=recv_sems.at[7 * a + k], device_id=to,
                device_id_type=MESH)

        mine = [pltpu.make_async_copy(x_refs[a], slot(a, *me), local_sems.at[a]) for a in arrays]
        first = [copy(a, 1 + j, me, (*chip, c), src=x_refs[a]) for j, chip in enumerate(chips) for a in arrays]
        first += [copy(a, 0, me, sibling, src=x_refs[a]) for a in arrays]
        for cp in first + mine:
            cp.start()
        passed = []
        for j, chip in enumerate(chips):
            for a in arrays:
                copy(a, 1 + j, (*chip, c), me).wait_recv()
                passed.append(copy(a, 4 + j, (*chip, c), sibling))
                passed[-1].start()
        for a in arrays:
            copy(a, 0, sibling, me).wait_recv()
            for j, chip in enumerate(chips):
                copy(a, 4 + j, (*chip, 1 - c), me).wait_recv()
        for cp in first + passed:
            cp.wait_send()
        for cp in mine:
            cp.wait()

    hbm = pl.BlockSpec(memory_space=pl.ANY)
    return pl.pallas_call(
        body, name=name,
        out_shape=[jax.ShapeDtypeStruct((N_DEV,) + s.shape, s.dtype) for s in shards],
        in_specs=[hbm] * n_arr, out_specs=[hbm] * n_arr,
        scratch_shapes=[pltpu.SemaphoreType.DMA((7 * n_arr,)), pltpu.SemaphoreType.DMA((7 * n_arr,)),
                        pltpu.SemaphoreType.DMA((n_arr,))],
    )(*shards)


def all_gather(name, shard):
    return all_gather_multi(name, [shard])[0]


def _peers():
    x, y, c = _my_pos()
    out = []
    for rel in (6, 4, 2, 7, 5, 3, 1):
        px, py, pc = x ^ ((rel >> 2) & 1), y ^ ((rel >> 1) & 1), c ^ (rel & 1)
        out.append((rel - 1, (px, py, pc), 4 * px + 2 * py + pc))
    return out


def _exchange_copies(mode, src_refs, land_refs, send_sems, recv_sems, with_arrivals):
    x, y, c = _my_pos()
    me = 4 * x + 2 * y + c
    sends, arrivals = [], []
    for k, peer_pos, peer in _peers():
        for a, (src, land) in enumerate(zip(src_refs, land_refs)):
            piece = src.at[peer] if mode == 'scatter' else src
            sems = dict(send_sem=send_sems[a].at[k], recv_sem=recv_sems[a].at[k], device_id_type=MESH)
            sends.append(pltpu.make_async_remote_copy(src_ref=piece, dst_ref=land.at[me], device_id=peer_pos, **sems))
            if with_arrivals:
                arrivals.append(pltpu.make_async_remote_copy(src_ref=piece, dst_ref=land.at[peer],
                                                             device_id=(x, y, c), **sems))
    return sends, arrivals


_HBM = pl.BlockSpec(memory_space=pltpu.HBM)
_SEM = pl.BlockSpec(memory_space=pltpu.SEMAPHORE)


def exchange_start(name, mode, arrays, after=()):
    n_arr, n_after = len(arrays), len(after)
    land_shapes = [a.shape if mode == 'scatter' else (N_DEV,) + a.shape for a in arrays]

    def body(*refs):
        src_refs, land_refs = refs[:n_arr], refs[n_arr:2 * n_arr]
        refs = refs[n_after:]
        send_sems, recv_sems = refs[2 * n_arr:3 * n_arr], refs[3 * n_arr:4 * n_arr]
        sends, _ = _exchange_copies(mode, src_refs, land_refs, send_sems, recv_sems, with_arrivals=False)
        for cp in sends:
            cp.start()
        token = refs[-1]
        token[...] = jnp.zeros_like(token)

    sem = pltpu.SemaphoreType.DMA((N_DEV - 1,))
    res = pl.pallas_call(
        body, name=name,
        out_shape=[sem] * (2 * n_arr) + [pltpu.HBM(a.shape, a.dtype) for a in arrays]
        + [pltpu.HBM(s, a.dtype) for s, a in zip(land_shapes, arrays)] + [jax.ShapeDtypeStruct((8, 128), F32)],
        in_specs=[_HBM] * (2 * n_arr) + [pl.BlockSpec(memory_space=pl.ANY)] * n_after,
        out_specs=[_SEM] * (2 * n_arr) + [_HBM] * (2 * n_arr) + [pl.BlockSpec(memory_space=pltpu.VMEM)],
        input_output_aliases={i: 2 * n_arr + i for i in range(2 * n_arr)},
        compiler_params=pltpu.CompilerParams(has_side_effects=pltpu.SideEffectType.DATAFLOW_SIDE_EFFECTING),
    )(*[pltpu.with_memory_space_constraint(a, pltpu.HBM) for a in arrays],
      *[pltpu.with_memory_space_constraint(lax.empty(s, a.dtype), pltpu.HBM) for s, a in zip(land_shapes, arrays)],
      *after)
    return (res[:n_arr], res[n_arr:2 * n_arr], res[2 * n_arr:3 * n_arr], res[3 * n_arr:4 * n_arr]), res[-1]


def exchange_wait(name, mode, started, after):
    send_sems, recv_sems, thru, land = started
    n_arr = len(thru)

    def body(*refs):
        src_refs, land_refs = refs[:n_arr], refs[n_arr:2 * n_arr]
        s_sems, r_sems = refs[2 * n_arr:3 * n_arr], refs[3 * n_arr:4 * n_arr]
        sends, arrivals = _exchange_copies(mode, src_refs, land_refs, s_sems, r_sems, with_arrivals=True)
        for cp in sends:
            cp.wait_send()
        for cp in arrivals:
            cp.wait_recv()

    res = pl.pallas_call(
        body, name=name,
        out_shape=[pltpu.HBM(a.shape, a.dtype) for a in thru] + [pltpu.HBM(a.shape, a.dtype) for a in land],
        in_specs=[_HBM] * (2 * n_arr) + [_SEM] * (2 * n_arr) + [pl.BlockSpec(memory_space=pl.ANY)],
        out_specs=[_HBM] * (2 * n_arr),
        input_output_aliases={i: i for i in range(2 * n_arr)},
        compiler_params=pltpu.CompilerParams(has_side_effects=pltpu.SideEffectType.DATAFLOW_SIDE_EFFECTING),
    )(*thru, *land, *send_sems, *recv_sems, after)
    return res[n_arr:]


def _sum_with_own(slot_ref, own_ref):
    x, y, c = _my_pos()
    me = 4 * x + 2 * y + c
    acc = None
    for p in range(N_DEV):
        v = jnp.where(me == p, own_ref[0], slot_ref[p]).astype(F32)
        acc = v if acc is None else acc + v
    return acc


def reduce_slots(name, slots, own, step=None):
    _, r, ccols = slots.shape
    tc = _pick(ccols, (256, 128))
    c1 = 1.0 - ADAM_B1 ** ADAM_STEP
    c2 = 1.0 - ADAM_B2 ** ADAM_STEP

    def body(s_ref, own_ref, *refs):
        g = _sum_with_own(s_ref, own_ref)
        if step is None:
            refs[0][...] = g
            return
        w_ref, m_ref, v_ref, g_ref, d_ref, nm_ref, nv_ref = refs
        nm = ADAM_B1 * m_ref[...] + (1.0 - ADAM_B1) * g
        nv = ADAM_B2 * v_ref[...] + (1.0 - ADAM_B2) * (g * g)
        g_ref[...] = g
        d_ref[...] = -ADAM_LR * ((nm / c1) / (jnp.sqrt(nv / c2) + ADAM_EPS) + ADAM_WD * w_ref[...])
        nm_ref[...] = nm
        nv_ref[...] = nv

    col = pl.BlockSpec((r, tc), lambda j: (0, j))
    n_out = 1 if step is None else 4
    res = pl.pallas_call(
        body, name=name, grid=(ccols // tc,),
        out_shape=[jax.ShapeDtypeStruct((r, ccols), F32)] * n_out,
        in_specs=[pl.BlockSpec((N_DEV, r, tc), lambda j: (0, 0, j)), pl.BlockSpec((1, r, tc), lambda j: (0, 0, j))]
        + [col] * (0 if step is None else 3),
        out_specs=[col] * n_out,
        compiler_params=_cparams(("parallel",)),
    )(slots, own, *(step or ()))
    return res[0] if step is None else res


def sum_slots(name, slots):
    _, r, ccols = slots.shape
    tc = _pick(ccols, (256, 128))

    def body(s_ref, o_ref):
        acc = s_ref[0].astype(F32)
        for p in range(1, N_DEV):
            acc = acc + s_ref[p].astype(F32)
        o_ref[...] = acc

    return pl.pallas_call(
        body, name=name, grid=(ccols // tc,),
        out_shape=jax.ShapeDtypeStruct((r, ccols), F32),
        in_specs=[pl.BlockSpec((N_DEV, r, tc), lambda j: (0, 0, j))],
        out_specs=pl.BlockSpec((r, tc), lambda j: (0, j)),
        compiler_params=_cparams(("parallel",)),
    )(slots)


def _mxu_fill(t):
    return t / (-(-t // MXU_DIM) * MXU_DIM)


def _matmul_tiles(mode, m_extent, n, k_extent, k_total, itemsizes):
    a_bytes, b_bytes, o_bytes = itemsizes
    m_cands = [c for c in (LANE_TILES if mode == 'tn' else ROW_TILES) if m_extent % c == 0] or [m_extent]
    k_cands = [c for c in (DEPTH_ROW_TILES if mode == 'tn' else LANE_TILES) if k_extent % c == 0] or [k_extent]
    n_cands = [c for c in LANE_TILES if n % c == 0] or [n]
    best = None
    for tm in m_cands:
        for tk in k_cands:
            for tn in n_cands:
                f32_tiles = 2 if k_total // tk > 1 else 1
                vmem = 2 * (tm * tk * a_bytes + tk * tn * b_bytes + tm * tn * o_bytes) + tm * tn * 4 * f32_tiles
                if vmem > MATMUL_VMEM_BUDGET:
                    continue
                key = (_mxu_fill(tn) * _mxu_fill(tk) >= MXU_FILL_OK, tm * tn * tk)
                if best is None or key > best[0]:
                    best = (key, (tm, tn, tk))
    assert best is not None, (mode, m_extent, n, k_extent)
    return best[1]


def matmul(name, a, b, mode, out_dtype, after=()):
    after = [t for t in after if t is not None]
    pieces, a_rows, a_cols = (1,) + a.shape if a.ndim == 2 else a.shape
    if mode == 'nn':
        (m, k), (k2, n) = (a_rows, pieces * a_cols), b.shape
    elif mode == 'nt':
        (m, k), (n, k2) = (a_rows, pieces * a_cols), b.shape
    else:
        (k, m), (k2, n) = (a_rows, pieces * a_cols), b.shape
    assert k == k2, (name, a.shape, b.shape, mode)
    tm, tn, tk = _matmul_tiles(mode, a_cols if mode == 'tn' else m, n, k if mode == 'tn' else a_cols, k,
                               (a.dtype.itemsize, b.dtype.itemsize, jnp.dtype(out_dtype).itemsize))
    nk = k // tk
    per_piece = a_cols // (tm if mode == 'tn' else tk)
    if a.ndim == 2:
        a_block = lambda rows, cols, at: pl.BlockSpec((rows, cols), at)
    else:
        a_block = lambda rows, cols, at: pl.BlockSpec(
            (None, rows, cols), lambda i, j, kk: (at(i, j, kk)[1] // per_piece, at(i, j, kk)[0],
                                                  at(i, j, kk)[1] % per_piece))
    if mode == 'nn':
        a_spec = a_block(tm, tk, lambda i, j, kk: (i, kk))
        b_spec = pl.BlockSpec((tk, tn), lambda i, j, kk: (kk, j))
        dn = (((1,), (0,)), ((), ()))
    elif mode == 'nt':
        a_spec = a_block(tm, tk, lambda i, j, kk: (i, kk))
        b_spec = pl.BlockSpec((tn, tk), lambda i, j, kk: (j, kk))
        dn = (((1,), (1,)), ((), ()))
    else:
        a_spec = a_block(tk, tm, lambda i, j, kk: (kk, i))
        b_spec = pl.BlockSpec((tk, tn), lambda i, j, kk: (kk, j))
        dn = (((0,), (0,)), ((), ()))

    def product(a_ref, b_ref):
        return lax.dot_general(a_ref[...].astype(BF16), b_ref[...].astype(BF16), dn, preferred_element_type=F32)

    n_after = len(after)

    def body_one(a_ref, b_ref, *rest):
        o_ref = rest[n_after]
        o_ref[...] = product(a_ref, b_ref).astype(o_ref.dtype)

    def body(a_ref, b_ref, *rest):
        o_ref, acc_ref = rest[n_after:]
        kk = pl.program_id(2)

        @pl.when(kk == 0)
        def _():
            acc_ref[...] = jnp.zeros_like(acc_ref)

        acc_ref[...] += product(a_ref, b_ref)

        @pl.when(kk == nk - 1)
        def _():
            o_ref[...] = acc_ref[...].astype(o_ref.dtype)

    return pl.pallas_call(
        body_one if nk == 1 else body, name=name, grid=(m // tm, n // tn, nk),
        out_shape=jax.ShapeDtypeStruct((m, n), out_dtype),
        in_specs=[a_spec, b_spec] + [pl.BlockSpec(memory_space=pl.ANY)] * n_after,
        out_specs=pl.BlockSpec((tm, tn), lambda i, j, kk: (i, j)),
        scratch_shapes=[] if nk == 1 else [pltpu.VMEM((tm, tn), F32)],
        compiler_params=_cparams(("parallel", "parallel", "arbitrary")),
    )(a, b, *after)


def rowwise(name, fn, rows, params, out_rows, out_accs, n_rows, t_lat, tm):
    nb, nbl = n_rows // tm, t_lat // tm
    in_specs, piece_counts = [], []
    operands = []
    for arr, off, width, *kind in rows:
        g = math.gcd(off, width) if off else width
        assert g % 128 == 0 or (off == 0 and width == arr.shape[1]), (name, off, width)
        cnt = width // g
        last = arr.shape[0] // tm - 1
        clamp = arr.shape[0] < n_rows
        for p in range(cnt):
            cb = off // g + p
            if kind == ['ctx']:
                in_specs.append(pl.BlockSpec(
                    (tm, g), lambda i, cb=cb, last=last: (jnp.clip(i - nbl, 0, last), cb)))
            elif clamp:
                in_specs.append(pl.BlockSpec((tm, g), lambda i, cb=cb, last=last: (jnp.minimum(i, last), cb)))
            else:
                in_specs.append(pl.BlockSpec((tm, g), lambda i, cb=cb: (i, cb)))
            operands.append(arr)
        piece_counts.append(cnt)
    for p in params:
        in_specs.append(pl.BlockSpec(p.shape, lambda i, nd=p.ndim: (0,) * nd))
        operands.append(p)
    n_in = sum(piece_counts)
    n_par = len(params)
    n_or = len(out_rows)
    lat_only = [kind == ['lat'] for _, _, *kind in out_rows]
    out_shape = [jax.ShapeDtypeStruct((t_lat if lat else n_rows, w), dt)
                 for (w, dt, *_), lat in zip(out_rows, lat_only)]
    out_shape += [jax.ShapeDtypeStruct(s, F32) for s in out_accs]
    out_specs = [pl.BlockSpec((tm, w), (lambda i: (jnp.minimum(i, nbl - 1), 0)) if lat else (lambda i: (i, 0)))
                 for (w, *_), lat in zip(out_rows, lat_only)]
    out_specs += [pl.BlockSpec(s, lambda i, nd=len(s): (0,) * nd) for s in out_accs]

    def body(*refs):
        in_refs, par_refs = refs[:n_in], refs[n_in:n_in + n_par]
        orow_refs = refs[n_in + n_par:n_in + n_par + n_or]
        oacc_refs = refs[n_in + n_par + n_or:]
        i = pl.program_id(0)
        tiles, at = [], 0
        for cnt in piece_counts:
            parts = [in_refs[at + p][...].astype(F32) for p in range(cnt)]
            tiles.append(parts[0] if cnt == 1 else jnp.concatenate(parts, axis=1))
            at += cnt
        is_ctx = i * tm >= t_lat
        outs, accs = fn(is_ctx, tiles, [p[...] for p in par_refs])
        for o_ref, o, lat in zip(orow_refs, outs, lat_only):
            if lat:
                @pl.when(jnp.logical_not(is_ctx))
                def _(o_ref=o_ref, o=o):
                    o_ref[...] = o.astype(o_ref.dtype)
            else:
                o_ref[...] = o.astype(o_ref.dtype)
        if oacc_refs:
            @pl.when(i == 0)
            def _():
                for a_ref in oacc_refs:
                    a_ref[...] = jnp.zeros_like(a_ref)
            for a_ref, a in zip(oacc_refs, accs):
                a_ref[...] += a.astype(F32)

    res = pl.pallas_call(
        body, name=name, grid=(nb,),
        out_shape=out_shape, in_specs=in_specs, out_specs=out_specs,
        compiler_params=_cparams(("arbitrary",)),
    )(*operands)
    return res[:n_or], res[n_or:]


def _rms(x, g):
    return x * lax.rsqrt(jnp.mean(x * x, axis=-1, keepdims=True) + EPS) * g


def _norm_mod(x, g, sc, sh):
    return _rms(x, g) * (1.0 + sc) + sh


def _sigmoid(x):
    return 0.5 * jnp.tanh(0.5 * x) + 0.5


def _silu(x):
    return x * _sigmoid(x)


def _gelu(x):
    return 0.5 * x * (1.0 + jnp.tanh(math.sqrt(2.0 / math.pi) * (x + 0.044715 * (x * x * x))))


def _sel(is_ctx, p):
    return jnp.where(is_ctx, p[1:2], p[0:1])


def _seg_acc(is_ctx, v):
    rows = lax.broadcasted_iota(jnp.int32, (2, v.shape[1]), 0)
    return jnp.where(rows == is_ctx.astype(jnp.int32), jnp.broadcast_to(v, (2, v.shape[1])), 0.0)


def _rsum(v):
    return jnp.sum(v, axis=0, keepdims=True)


def _shift_rows(x, o, t_lat, n):
    if o == 0:
        return x
    y = pltpu.roll(x, (-o) % n, 0)
    t = lax.broadcasted_iota(jnp.int32, x.shape, 0)
    if o > 0:
        ok = t < n - o
        if t_lat < n:
            ok = ok & ((t < t_lat - o) | (t >= t_lat))
    else:
        ok = t >= -o
        if t_lat < n:
            ok = ok & ((t < t_lat) | (t >= t_lat - o))
    return jnp.where(ok, y, 0.0)


def conv_fwd(name, xarr, col_off, width, w, b, left, n_rows, t_lat, out_dtype, cb=128):
    taps = w.shape[0]
    assert col_off % cb == 0 and width % cb == 0

    def body(x_ref, w_ref, b_ref, o_ref):
        x = x_ref[...].astype(F32)
        acc = jnp.broadcast_to(b_ref[...], x.shape)
        for k in range(taps):
            acc = acc + _shift_rows(x, k - left, t_lat, n_rows) * w_ref[k:k + 1, :]
        o_ref[...] = acc.astype(o_ref.dtype)

    return pl.pallas_call(
        body, name=name, grid=(width // cb,),
        out_shape=jax.ShapeDtypeStruct((n_rows, width), out_dtype),
        in_specs=[pl.BlockSpec((n_rows, cb), lambda j: (0, col_off // cb + j)),
                  pl.BlockSpec((taps, cb), lambda j: (0, j)),
                  pl.BlockSpec((1, cb), lambda j: (0, j))],
        out_specs=pl.BlockSpec((n_rows, cb), lambda j: (0, j)),
        compiler_params=_cparams(("parallel",)),
    )(xarr, w, b)


def conv_bwd(name, dout, xarr, col_off, width, w, left, n_rows, t_lat, cb=128):
    taps = w.shape[0]

    def body(d_ref, x_ref, w_ref, dx_ref, dw_ref, db_ref):
        d = d_ref[...].astype(F32)
        x = x_ref[...].astype(F32)
        dx = jnp.zeros_like(d)
        dws = []
        for k in range(taps):
            dx = dx + _shift_rows(d, left - k, t_lat, n_rows) * w_ref[k:k + 1, :]
            dws.append(_rsum(d * _shift_rows(x, k - left, t_lat, n_rows)))
        dx_ref[...] = dx.astype(dx_ref.dtype)
        dw_ref[...] = jnp.concatenate(dws, axis=0)
        db_ref[...] = _rsum(d)

    return pl.pallas_call(
        body, name=name, grid=(width // cb,),
        out_shape=[jax.ShapeDtypeStruct((n_rows, width), BF16), jax.ShapeDtypeStruct((taps, width), F32),
                   jax.ShapeDtypeStruct((1, width), F32)],
        in_specs=[pl.BlockSpec((n_rows, cb), lambda j: (0, j)),
                  pl.BlockSpec((n_rows, cb), lambda j: (0, col_off // cb + j)),
                  pl.BlockSpec((taps, cb), lambda j: (0, j))],
        out_specs=[pl.BlockSpec((n_rows, cb), lambda j: (0, j)), pl.BlockSpec((taps, cb), lambda j: (0, j)),
                   pl.BlockSpec((1, cb), lambda j: (0, j))],
        compiler_params=_cparams(("parallel",)),
    )(dout, xarr, w)


def _ffn_conv(a, w_ref, b_ref, t_lat):
    shifted = [_shift_rows(a, k - 1, t_lat, t_lat) for k in range(3)]
    ac = jnp.broadcast_to(b_ref[...], a.shape)
    for k in range(3):
        ac = ac + shifted[k] * w_ref[k:k + 1, :]
    return ac, shifted


def ffn_mix_fwd(u, w, b, t_lat, cb=128):
    nblk = FFN // cb

    def body(a_ref, g_ref, w_ref, b_ref, f_ref):
        ac, _ = _ffn_conv(a_ref[...].astype(F32), w_ref, b_ref, t_lat)
        f_ref[...] = (_silu(ac) * g_ref[...].astype(F32)).astype(f_ref.dtype)

    col = lambda shape, off=0: pl.BlockSpec(shape, lambda j: (0, off + j))
    return pl.pallas_call(
        body, name="ffn_mix", grid=(nblk,),
        out_shape=jax.ShapeDtypeStruct((t_lat, FFN), BF16),
        in_specs=[col((t_lat, cb)), col((t_lat, cb), nblk), col((3, cb)), col((1, cb))],
        out_specs=col((t_lat, cb)),
        compiler_params=_cparams(("parallel",)),
    )(u, u, w, b)


def ffn_mix_bwd(u, df, w, b, t_lat, cb=128):
    nblk = FFN // cb

    def body(a_ref, g_ref, df_ref, w_ref, b_ref, du_ref, dw_ref, db_ref):
        ac, shifted = _ffn_conv(a_ref[...].astype(F32), w_ref, b_ref, t_lat)
        d = df_ref[...].astype(F32)
        s = _sigmoid(ac)
        du_ref[1] = (d * (ac * s)).astype(du_ref.dtype)
        dac = d * g_ref[...].astype(F32) * (s * (1.0 + ac * (1.0 - s)))
        da = jnp.zeros_like(dac)
        for k in range(3):
            da = da + _shift_rows(dac, 1 - k, t_lat, t_lat) * w_ref[k:k + 1, :]
        du_ref[0] = da.astype(du_ref.dtype)
        dw_ref[...] = jnp.concatenate([_rsum(dac * shifted[k]) for k in range(3)], axis=0)
        db_ref[...] = _rsum(dac)

    col = lambda shape, off=0: pl.BlockSpec(shape, lambda j: (0, off + j))
    return pl.pallas_call(
        body, name="ffn_mix_bwd", grid=(nblk,),
        out_shape=[jax.ShapeDtypeStruct((2, t_lat, FFN), BF16),
                   jax.ShapeDtypeStruct((3, FFN), F32), jax.ShapeDtypeStruct((1, FFN), F32)],
        in_specs=[col((t_lat, cb)), col((t_lat, cb), nblk), col((t_lat, cb)), col((3, cb)), col((1, cb))],
        out_specs=[pl.BlockSpec((2, t_lat, cb), lambda j: (0, 0, j)), col((3, cb)), col((1, cb))],
        compiler_params=_cparams(("parallel",)),
    )(u, u, df, w, b)


def _chunk_order(direction, nb, nbl):
    if direction == 'f':
        return lambda s: ((s + nbl) % nb, 0)
    return lambda s: (nb - 1 - s, 0)


def _adjoint_order(direction, nb, nbl):
    if direction == 'f':
        return lambda s: ((nb - 1 - s + nbl) % nb, 0)
    return lambda s: (s, 0)


SUBLANES = 8


def _chunk_scan(a, b, carry, rev):
    tc, width = a.shape
    nt = tc // SUBLANES
    row = lax.broadcasted_iota(jnp.int32, a.shape, 0)
    a, b = a.reshape(nt, SUBLANES, width), b.reshape(nt, SUBLANES, width)
    in_tile = lax.broadcasted_iota(jnp.int32, a.shape, 1)
    for k in (1, 2, 4):
        shift = SUBLANES - k if rev else k
        edge = in_tile >= SUBLANES - k if rev else in_tile < k
        b = jnp.where(edge, b, a * pltpu.roll(b, shift, 1) + b)
        a = jnp.where(edge, a, a * pltpu.roll(a, shift, 1))
    a, b = a.reshape(tc, width), b.reshape(tc, width)
    hs = [None] * nt
    c = carry
    for kt in range(nt):
        k = nt - 1 - kt if rev else kt
        h = b[k * SUBLANES:(k + 1) * SUBLANES] + a[k * SUBLANES:(k + 1) * SUBLANES] * c
        hs[k] = h
        c = h[0:1] if rev else h[SUBLANES - 1:SUBLANES]
    h = jnp.concatenate(hs, axis=0)
    if rev:
        return h, jnp.where(row == tc - 1, carry, pltpu.roll(h, tc - 1, 0)), c
    return h, jnp.where(row == 0, carry, pltpu.roll(h, 1, 0)), c


def _one_minus_a_squared(log_a, a):
    return (1.0 + a * a) * jnp.tanh(-log_a)


def _gate_elem(pre_r, pre_i, xc, b_a, b_x, sp):
    r = _sigmoid(pre_r + b_a)
    i = _sigmoid(pre_i + b_x)
    log_a = (-LRU_C) * r * sp
    a = jnp.exp(log_a)
    m2 = _one_minus_a_squared(log_a, a)
    mult = jnp.where(m2 > 0.0, m2 * lax.rsqrt(m2), 0.0)
    return a, mult * (i * xc)


def _gate_elem_bwd(pre_r, pre_i, xc, b_a, b_x, sp, da, du):
    r = _sigmoid(pre_r + b_a)
    i = _sigmoid(pre_i + b_x)
    log_a = (-LRU_C) * r * sp
    a = jnp.exp(log_a)
    m2 = _one_minus_a_squared(log_a, a)
    inv_mult = lax.rsqrt(m2)
    g = du * (m2 * inv_mult)
    d_mult = du * (i * xc)
    d_log_a = (da - d_mult * a * inv_mult) * a
    d_pre_r = d_log_a * ((-LRU_C) * sp) * (r * (1.0 - r))
    d_pre_i = g * xc * (i * (1.0 - i))
    return d_pre_r, d_pre_i, g * i, _rsum(d_log_a * ((-LRU_C) * r))


def _blockdiag(xb16, w_ref_val, d):
    outs = []
    for n in range(LRU_BLOCKS):
        outs.append(jnp.dot(xb16[:, n * LRU_BW:(n + 1) * LRU_BW], w_ref_val[d * LRU_BLOCKS + n],
                            preferred_element_type=F32))
    return jnp.concatenate(outs, axis=1)


def lru_scan(name, xc, w_a, w_x, b_a, b_x, sp, direction, n_rows, t_lat):
    w = xc.shape[1]
    d = 0 if direction == 'f' else 1
    tc = _pick(math.gcd(t_lat, n_rows), (256, 128))
    nb, nbl = n_rows // tc, t_lat // tc
    order = _chunk_order(direction, nb, nbl)
    rev = direction == 'b'

    def body(x_ref, wa_ref, wx_ref, ba_ref, bx_ref, sp_ref, a_ref, h_ref, hp_ref, carry):
        @pl.when(pl.program_id(0) == 0)
        def _():
            carry[...] = jnp.zeros_like(carry)

        x = x_ref[...]
        xb16 = x.astype(BF16)
        a, u = _gate_elem(_blockdiag(xb16, wa_ref[...], d), _blockdiag(xb16, wx_ref[...], d), x,
                          ba_ref[d:d + 1, :], bx_ref[d:d + 1, :], sp_ref[d:d + 1, :])
        a_ref[...] = a
        h_ref[...], hp_ref[...], carry[...] = _chunk_scan(a, u, carry[...], rev)

    spec = pl.BlockSpec((tc, w), order)
    whole = lambda p: pl.BlockSpec(p.shape, lambda s, nd=p.ndim: (0,) * nd)
    return pl.pallas_call(
        body, name=name, grid=(nb,),
        out_shape=[jax.ShapeDtypeStruct((n_rows, w), F32)] * 3,
        in_specs=[spec] + [whole(p) for p in (w_a, w_x, b_a, b_x, sp)], out_specs=[spec] * 3,
        scratch_shapes=[pltpu.VMEM((1, w), F32)],
        compiler_params=_cparams(("arbitrary",)),
    )(xc, w_a, w_x, b_a, b_x, sp)


def lru_scan_bwd(name, xc, a, dh, hprev, dxc_in, w_a, w_x, b_a, b_x, sp, direction, n_rows, t_lat):
    w = xc.shape[1]
    d = 0 if direction == 'f' else 1
    tc = _pick(math.gcd(t_lat, n_rows), (256, 128))
    nb, nbl = n_rows // tc, t_lat // tc
    order = _adjoint_order(direction, nb, nbl)
    rev = direction == 'f'
    has_in = dxc_in is not None
    nt_dims, tn_dims = (((1,), (1,)), ((), ())), (((0,), (0,)), ((), ()))

    def dh_order(s):
        c, _ = order(s)
        return (jnp.minimum(c, nbl - 1), 0)

    def body(*refs):
        x_ref, a_ref, dh_ref, hp_ref = refs[:4]
        in_ref = refs[4] if has_in else None
        wa_ref, wx_ref, ba_ref, bx_ref, sp_ref = refs[4 + has_in:9 + has_in]
        dx_ref, dwa_ref, dwx_ref, dba_ref, dbx_ref, dsp_ref, carry = refs[9 + has_in:]
        s = pl.program_id(0)

        @pl.when(s == 0)
        def _():
            carry[...] = jnp.zeros_like(carry)
            for acc in (dwa_ref, dwx_ref, dba_ref, dbx_ref, dsp_ref):
                acc[...] = jnp.zeros_like(acc)

        chunk, _ = order(s)
        live = (chunk < nbl).astype(F32)
        av = a_ref[...]
        dv = dh_ref[...].astype(F32) * live
        _, c_next, carry[...] = _chunk_scan(av, av * dv, carry[...], rev)
        lam = dv + c_next

        x = x_ref[...]
        xb16 = x.astype(BF16)
        wa, wx = wa_ref[...], wx_ref[...]
        dpr, dpi, dxc, dsp_d = _gate_elem_bwd(_blockdiag(xb16, wa, d), _blockdiag(xb16, wx, d), x,
                                              ba_ref[d:d + 1, :], bx_ref[d:d + 1, :], sp_ref[d:d + 1, :],
                                              lam * hp_ref[...], lam)
        dpr16, dpi16 = dpr.astype(BF16), dpi.astype(BF16)
        back = []
        for n in range(LRU_BLOCKS):
            sl = slice(n * LRU_BW, (n + 1) * LRU_BW)
            back.append(lax.dot_general(dpr16[:, sl], wa[d * LRU_BLOCKS + n], nt_dims, preferred_element_type=F32)
                        + lax.dot_general(dpi16[:, sl], wx[d * LRU_BLOCKS + n], nt_dims, preferred_element_type=F32))
            dwa_ref[n] += lax.dot_general(xb16[:, sl], dpr16[:, sl], tn_dims, preferred_element_type=F32)
            dwx_ref[n] += lax.dot_general(xb16[:, sl], dpi16[:, sl], tn_dims, preferred_element_type=F32)
        dxc = dxc + jnp.concatenate(back, axis=1)
        dx_ref[...] = dxc + in_ref[...] if has_in else dxc
        dba_ref[...] += _rsum(dpr)
        dbx_ref[...] += _rsum(dpi)
        dsp_ref[...] += dsp_d

    spec = pl.BlockSpec((tc, w), order)
    whole = lambda shape: pl.BlockSpec(shape, lambda s, nd=len(shape): (0,) * nd)
    params = (w_a, w_x, b_a, b_x, sp)
    acc_shapes = [(LRU_BLOCKS, LRU_BW, LRU_BW)] * 2 + [(1, w)] * 3
    return pl.pallas_call(
        body, name=name, grid=(nb,),
        out_shape=[jax.ShapeDtypeStruct((n_rows, w), F32)] + [jax.ShapeDtypeStruct(sh, F32) for sh in acc_shapes],
        in_specs=[spec, spec, pl.BlockSpec((tc, w), dh_order), spec] + [spec] * has_in
        + [whole(p.shape) for p in params],
        out_specs=[spec] + [whole(sh) for sh in acc_shapes],
        scratch_shapes=[pltpu.VMEM((1, w), F32)],
        compiler_params=_cparams(("arbitrary",)),
    )(xc, a, dh, hprev, *([dxc_in] if has_in else []), *params)


def _rope_tables(t_lat, n_rows):
    rows = t_lat // GRID_W
    row_ids = jnp.repeat(jnp.arange(rows), GRID_W).astype(F32)
    col_ids = jnp.tile(jnp.arange(GRID_W), rows).astype(F32)
    axis_dim = QK_ROPE // 2
    inv = 1.0 / (ROPE_BASE ** (jnp.arange(0, axis_dim, 2, dtype=F32) / axis_dim))
    ang = jnp.concatenate([row_ids[:, None] * inv, col_ids[:, None] * inv], axis=-1)
    cos, sin = jnp.cos(ang), jnp.sin(ang)
    half = QK_ROPE // 2
    ones, zeros = jnp.ones((t_lat, QK_NOPE), F32), jnp.zeros((t_lat, QK_NOPE), F32)
    pad1, pad0 = jnp.ones((t_lat, HEAD_PAD - QK_DIM), F32), jnp.zeros((t_lat, HEAD_PAD - QK_DIM), F32)
    zh = jnp.zeros((t_lat, half), F32)
    c_tab = jnp.concatenate([ones, cos, cos, pad1], axis=1)
    s1 = jnp.concatenate([zeros, -sin, zh, pad0], axis=1)
    s2 = jnp.concatenate([zeros, zh, sin, pad0], axis=1)
    n_ctx = n_rows - t_lat
    c_tab = jnp.concatenate([c_tab, jnp.ones((n_ctx, HEAD_PAD), F32)], axis=0)
    s1 = jnp.concatenate([s1, jnp.zeros((n_ctx, HEAD_PAD), F32)], axis=0)
    s2 = jnp.concatenate([s2, jnp.zeros((n_ctx, HEAD_PAD), F32)], axis=0)
    return c_tab, s1, s2


def _rope(x, c, s1, s2):
    half = QK_ROPE // 2
    return x * c + pltpu.roll(x, HEAD_PAD - half, 1) * s1 + pltpu.roll(x, half, 1) * s2


def _rope_t(dy, c, s1, s2):
    half = QK_ROPE // 2
    return dy * c + pltpu.roll(dy * s1, half, 1) + pltpu.roll(dy * s2, HEAD_PAD - half, 1)


def _heads(x):
    return [x[:, h * HEAD_PAD:(h + 1) * HEAD_PAD] for h in range(N_HEADS)]


Q_SCALE = QK_DIM ** -0.5 * math.log2(math.e)
ATTN_FWD_HEADS = 4

def attn_fwd(q, k, v, t_lat, n_rows, tq):
    def body(q_ref, k_ref, v_ref, o_ref, lse_ref):
        for hh in range(ATTN_FWD_HEADS):
            lanes = slice(hh * HEAD_PAD, (hh + 1) * HEAD_PAD)
            s = lax.dot_general(q_ref[:, lanes], k_ref[:, lanes], (((1,), (1,)), ((), ())),
                                preferred_element_type=F32)
            m = jnp.max(s, axis=-1, keepdims=True)
            p = jnp.exp2(s - m)
            l = jnp.sum(p, axis=-1, keepdims=True)
            o = jnp.dot(p.astype(BF16), v_ref[:, lanes], preferred_element_type=F32) / l
            o_ref[:, lanes] = o.astype(o_ref.dtype)
            lse_ref[:, lanes] = jnp.broadcast_to(m + jnp.log2(l), (tq, HEAD_PAD))

    width = ATTN_FWD_HEADS * HEAD_PAD
    qspec = pl.BlockSpec((tq, width), lambda h, i: (i, h))
    kspec = pl.BlockSpec((n_rows, width), lambda h, i: (0, h))
    return pl.pallas_call(
        body, name="attn_fwd", grid=(N_HEADS // ATTN_FWD_HEADS, t_lat // tq),
        out_shape=[jax.ShapeDtypeStruct((t_lat, N_HEADS * HEAD_PAD), BF16),
                   jax.ShapeDtypeStruct((t_lat, N_HEADS * HEAD_PAD), F32)],
        in_specs=[qspec, kspec, kspec], out_specs=[qspec, qspec],
        compiler_params=_cparams(("parallel", "arbitrary")),
    )(q, k, v)


def attn_bwd(q, k, v, o, do, lse, t_lat, n_rows, tq):
    scale = QK_DIM ** -0.5
    nq = t_lat // tq
    nt = (((1,), (1,)), ((), ()))
    tn = (((0,), (0,)), ((), ()))

    def body(q_ref, k_ref, v_ref, o_ref, do_ref, lse_ref, dq_ref, dk_ref, dv_ref):
        @pl.when(pl.program_id(1) == 0)
        def _():
            dk_ref[...] = jnp.zeros_like(dk_ref)
            dv_ref[...] = jnp.zeros_like(dv_ref)

        qv, kv, vv, dov = q_ref[...], k_ref[...], v_ref[...], do_ref[...]
        s = lax.dot_general(qv, kv, nt, preferred_element_type=F32)
        p = jnp.exp2(s - lse_ref[:, 0:1])
        dv_ref[...] += lax.dot_general(p.astype(BF16), dov, tn, preferred_element_type=F32)
        dp = lax.dot_general(dov, vv, nt, preferred_element_type=F32)
        delta = jnp.sum(dov.astype(F32) * o_ref[...].astype(F32), axis=-1, keepdims=True)
        ds = (p * (dp - delta)).astype(BF16)
        dq_ref[...] = (jnp.dot(ds, kv, preferred_element_type=F32) * scale).astype(dq_ref.dtype)
        dk_ref[...] += lax.dot_general(ds, qv, tn, preferred_element_type=F32)

        @pl.when(pl.program_id(1) == nq - 1)
        def _():
            dk_ref[...] = dk_ref[...] * (scale / Q_SCALE)

    qspec = pl.BlockSpec((tq, HEAD_PAD), lambda h, i: (i, h))
    kspec = pl.BlockSpec((n_rows, HEAD_PAD), lambda h, i: (0, h))
    return pl.pallas_call(
        body, name="attn_bwd", grid=(N_HEADS, t_lat // tq),
        out_shape=[jax.ShapeDtypeStruct((t_lat, N_HEADS * HEAD_PAD), BF16),
                   jax.ShapeDtypeStruct((n_rows, N_HEADS * HEAD_PAD), F32),
                   jax.ShapeDtypeStruct((n_rows, N_HEADS * HEAD_PAD), F32)],
        in_specs=[qspec, kspec, kspec, qspec, qspec, qspec], out_specs=[qspec, kspec, kspec],
        compiler_params=_cparams(("parallel", "arbitrary")),
    )(q, k, v, o, do, lse)


def adamw(name, w, g, m, v):
    r, ccols = w.shape
    if r % 8 == 0:
        tr, tcol = _best_div(r, 8, max(8, 262144 // ccols)), ccols
    else:
        tr, tcol = r, _pick(ccols, (256, 128))
    c1 = 1.0 - ADAM_B1 ** ADAM_STEP
    c2 = 1.0 - ADAM_B2 ** ADAM_STEP

    def body(w_ref, g_ref, m_ref, v_ref, d_ref, nm_ref, nv_ref):
        gv = g_ref[...]
        nm = ADAM_B1 * m_ref[...] + (1.0 - ADAM_B1) * gv
        nv = ADAM_B2 * v_ref[...] + (1.0 - ADAM_B2) * (gv * gv)
        d_ref[...] = -ADAM_LR * ((nm / c1) / (jnp.sqrt(nv / c2) + ADAM_EPS) + ADAM_WD * w_ref[...])
        nm_ref[...] = nm
        nv_ref[...] = nv

    spec = pl.BlockSpec((tr, tcol), lambda i, j: (i, j))
    return pl.pallas_call(
        body, name=name, grid=(r // tr, ccols // tcol),
        out_shape=[jax.ShapeDtypeStruct((r, ccols), F32)] * 3,
        in_specs=[spec] * 4, out_specs=[spec] * 3,
        compiler_params=_cparams(("parallel", "parallel")),
    )(w, g, m, v)


def adamw_many(name, ws, gs, ms, vs):
    n = len(ws)
    c1 = 1.0 - ADAM_B1 ** ADAM_STEP
    c2 = 1.0 - ADAM_B2 ** ADAM_STEP

    def body(*refs):
        for i in range(n):
            w_ref, g_ref, m_ref, v_ref = (refs[k * n + i] for k in range(4))
            d_ref, nm_ref, nv_ref = (refs[(4 + k) * n + i] for k in range(3))
            gv = g_ref[...]
            nm = ADAM_B1 * m_ref[...] + (1.0 - ADAM_B1) * gv
            nv = ADAM_B2 * v_ref[...] + (1.0 - ADAM_B2) * (gv * gv)
            d_ref[...] = -ADAM_LR * ((nm / c1) / (jnp.sqrt(nv / c2) + ADAM_EPS) + ADAM_WD * w_ref[...])
            nm_ref[...] = nm
            nv_ref[...] = nv

    vmem = pl.BlockSpec(memory_space=pltpu.VMEM)
    res = pl.pallas_call(
        body, name=name,
        out_shape=[jax.ShapeDtypeStruct(w.shape, F32) for w in ws] * 3,
        in_specs=[vmem] * (4 * n), out_specs=[vmem] * (3 * n),
        compiler_params=_cparams(),
    )(*ws, *gs, *ms, *vs)
    return [tuple(res[k * n + i] for k in range(3)) for i in range(n)]


def _flat(parts, dtype, row_mult):
    v = jnp.concatenate([p.reshape(-1).astype(dtype) for p in parts])
    quantum = row_mult * FLAT_C
    total = -(-v.shape[0] // quantum) * quantum
    return jnp.pad(v, (0, total - v.shape[0])).reshape(total // FLAT_C, FLAT_C)


def _gathered_to_full(name, g):
    k = g.shape[1]
    return jnp.transpose(g, (1, 0, 2)).reshape(k, N_DEV * g.shape[2])


def _shard_to_rb(name, w):
    return w if name in ROW_SHARDED else w.T


def _rb_to_shard(name, g):
    return g if name in ROW_SHARDED else g.T


def _rb_from_gathered(name, g):
    cols = g.shape[2]
    if name == 'w_in':
        z = lambda k: jnp.zeros((k, cols), g.dtype)
        full = g.reshape(N_DEV * g.shape[1], cols)
        return jnp.concatenate([full[:Z_KR], z(QK_NOPE), full[Z_KR:Z_KR + QK_ROPE], z(HEAD_PAD - QK_DIM),
                                full[Z_KR + QK_ROPE:]], axis=0)
    if name == 'w_uq':
        return jnp.pad(g, ((0, 0), (0, HEAD_PAD - QK_DIM), (0, 0))).reshape(N_HEADS * HEAD_PAD, cols)
    if name == 'w_ukv':
        pad = lambda t: jnp.pad(t, ((0, 0), (0, HEAD_PAD - t.shape[1]), (0, 0))).reshape(N_HEADS * HEAD_PAD, cols)
        return jnp.concatenate([pad(g[:, :QK_NOPE]), pad(g[:, QK_NOPE:])], axis=0)
    if name == 'w_o_attn':
        full = g.reshape(D, N_HEADS, V_HEAD)
        return jnp.pad(full, ((0, 0), (0, 0), (0, HEAD_PAD - V_HEAD))).reshape(D, N_HEADS * HEAD_PAD)
    return g.reshape(N_DEV * g.shape[1], cols)


def _chunks_from_rb_grad(name, g):
    cols = g.shape[1]
    if name == 'w_in':
        full = jnp.concatenate([g[:Z_KR], g[Z_KR + QK_NOPE:Z_KR + QK_DIM], g[Z_XB:]], axis=0)
        return full.reshape(N_DEV, -1, cols)
    if name == 'w_uq':
        return g.reshape(N_HEADS, HEAD_PAD, cols)[:, :QK_DIM]
    if name == 'w_ukv':
        half = N_HEADS * HEAD_PAD
        gk = g[:half].reshape(N_HEADS, HEAD_PAD, cols)[:, :QK_NOPE]
        gv = g[half:].reshape(N_HEADS, HEAD_PAD, cols)[:, :V_HEAD]
        return jnp.concatenate([gk, gv], axis=1)
    if name == 'w_o_attn':
        full = g.reshape(D, N_HEADS, HEAD_PAD)[:, :, :V_HEAD].reshape(D, N_HEADS * V_HEAD)
        return full.reshape(N_DEV, D // N_DEV, N_HEADS * V_HEAD)
    return g.reshape(N_DEV, -1, cols)


def local_step(x, ctx, target, mod_l, mod_c, wt, on_grad=None, arrive=None):
    t_lat, n_ctx = x.shape[0], ctx.shape[0]
    n = t_lat + n_ctx
    tm = _pick(math.gcd(t_lat, n), (256, 128))
    tq_fwd = _pick(t_lat, (256, 128))
    tq_bwd = _pick(t_lat, (512, 256, 128))
    row = lambda v: v.reshape(1, -1).astype(F32)
    two = lambda a, b: jnp.stack([a, b]).astype(F32)
    sh1_l, sc1_l, g1_l, sh2_l, sc2_l, g2_l = jnp.split(mod_l, 6)
    sh1_c, sc1_c = jnp.split(mod_c, 6)[:2]
    sc1, sh1 = two(sc1_l, sc1_c), two(sh1_l, sh1_c)
    g1, g2, sc2, sh2 = row(g1_l), row(g2_l), row(sc2_l), row(sh2_l)
    norm1_g, norm2_g, final_g = row(wt['norm1_g']), row(wt['norm2_g']), row(wt['final_g'])
    q_g, kv_g, b_gate = row(wt['q_norm_g']), row(wt['kv_norm_g']), row(wt['b_gate'])
    wt = dict(wt)
    pending = []

    def sent():
        tokens = list(pending)
        pending.clear()
        return tokens

    def need(names, after):
        if arrive is not None:
            got = arrive(names, after)
            if '_token' in got:
                pending.append(got.pop('_token'))
            wt.update(got)
        return [wt[n] for n in names]
    lru_w_a = wt['lru_w_a'].reshape(2 * LRU_BLOCKS, LRU_BW, LRU_BW).astype(BF16)
    lru_w_x = wt['lru_w_x'].reshape(2 * LRU_BLOCKS, LRU_BW, LRU_BW).astype(BF16)
    b_a, b_x, lam = wt['lru_b_a'], wt['lru_b_x'], wt['lru_lambda']
    sp = jnp.logaddexp(-lam, 0.0)
    c_tab, s1_tab, s2_tab = _rope_tables(t_lat, n)
    rw = functools.partial(rowwise, n_rows=n, t_lat=t_lat, tm=tm)
    rw_lat = functools.partial(rowwise, n_rows=t_lat, t_lat=t_lat, tm=_pick(t_lat, (512, 256, 128)))

    stream = [(x, 0, D), (ctx, 0, D, 'ctx')]

    def f_norm1(is_ctx, rows, params):
        (xl, xc_), (g, sc, sh) = rows, params
        return [_norm_mod(jnp.where(is_ctx, xc_, xl), g, _sel(is_ctx, sc), _sel(is_ctx, sh))], []

    (h,), _ = rw("norm1", f_norm1, stream, [norm1_g, sc1, sh1], [(D, BF16)], [])
    (w_in_t,) = need(('w_in',), h)
    z = matmul("w_in", h, w_in_t, 'nt', BF16, after=sent())
    w_uq_t, w_ukv_t, w_o_lru = need(('w_uq', 'w_ukv', 'w_o_lru'), z)

    def f_qkv_norm(is_ctx, rows, params):
        (ql, kvl), (gq, gkv) = rows, params
        return [_rms(ql, gq), _rms(kvl, gkv)], []

    (qn, kvn), _ = rw("qkv_norm", f_qkv_norm, [(z, Z_Q, Q_RANK), (z, Z_KV, KV_RANK)], [q_g, kv_g],
                      [(Q_RANK, BF16), (KV_RANK, BF16)], [])
    qp = matmul("w_uq", qn, w_uq_t, 'nt', BF16)
    kvp = matmul("w_ukv", kvn, w_ukv_t, 'nt', BF16)

    def f_rope(is_ctx, rows, params):
        qv, kk, vv, kr, c, s1, s2 = rows
        krr = _rope(kr, c, s1, s2)
        qo = jnp.concatenate([_rope(qh, c, s1, s2) for qh in _heads(qv)], axis=1) * Q_SCALE
        ko = jnp.concatenate([kh + krr for kh in _heads(kk)], axis=1)
        return [qo, ko, vv], []

    hp = N_HEADS * HEAD_PAD
    (qr, kr_, vr), _ = rw("rope", f_rope,
                          [(qp, 0, hp), (kvp, 0, hp), (kvp, hp, hp), (z, Z_KR, HEAD_PAD), (c_tab, 0, HEAD_PAD),
                           (s1_tab, 0, HEAD_PAD), (s2_tab, 0, HEAD_PAD)], [], [(hp, BF16)] * 3, [])
    attn, lse = attn_fwd(qr, kr_, vr, t_lat, n, tq_fwd)

    xc = conv_fwd("lru_conv", z, Z_XB, LRU_W, wt['lru_conv_w'], row(wt['lru_conv_b']), 2, n, t_lat, F32)
    a_f, h_f, hp_f = lru_scan("lru_scan_f", xc, lru_w_a, lru_w_x, b_a, b_x, sp, 'f', n, t_lat)
    a_b, h_b, hp_b = lru_scan("lru_scan_b", xc, lru_w_a, lru_w_x, b_a, b_x, sp, 'b', n, t_lat)

    def f_lru_out(is_ctx, rows, params):
        hf, hb, yb = rows
        return [(hf + hb) * _gelu(yb)], []

    (ybin,), _ = rw_lat("lru_out", f_lru_out, [(h_f, 0, LRU_W), (h_b, 0, LRU_W), (z, Z_YB, LRU_W)], [],
                        [(LRU_W, BF16)], [])
    w_o_attn_t, w_out, w_up_t, w_down = need(('w_o_attn', 'w_out', 'w_up', 'w_down'), attn)
    y_a = matmul("w_o_attn", attn, w_o_attn_t, 'nt', BF16)
    y_b = matmul("w_o_lru", ybin, w_o_lru, 'nn', BF16)

    def _merge(ya, yb, gl, bg):
        gates = _sigmoid(gl + bg)
        return gates[:, :D] * ya + gates[:, D:] * yb

    def f_merge(is_ctx, rows, params):
        (ya, yb, gl), (bg,) = rows, params
        return [_merge(ya, yb, gl, bg)], []

    (mrg,), _ = rw_lat("merge", f_merge, [(y_a, 0, D), (y_b, 0, D), (z, Z_GL, 2 * D)], [b_gate], [(D, BF16)], [])
    o = matmul("w_out", mrg, w_out, 'nn', BF16)

    def _res_norm2(xv, ov, g1v, g, sc, sh):
        x1 = xv + g1v * ov
        return x1, _norm_mod(x1, g, sc, sh)

    def f_norm2(is_ctx, rows, params):
        (xv, ov), (g1v, g, sc, sh) = rows, params
        x1, h2v = _res_norm2(xv, ov, g1v, g, sc, sh)
        return [x1, h2v], []

    (x1, h2), _ = rw_lat("norm2", f_norm2, [(x, 0, D), (o, 0, D)], [g1, norm2_g, sc2, sh2], [(D, F32), (D, BF16)], [])
    u = matmul("w_up", h2, w_up_t, 'nt', BF16)
    f = ffn_mix_fwd(u, wt['ffn_conv_w'], row(wt['ffn_conv_b']), t_lat)
    dn = matmul("w_down", f, w_down, 'nn', BF16)

    def _tile_loss(x1v, dv, g2v, fg, tgt):
        y = _rms(x1v + g2v * dv, fg)
        e = y - tgt
        return 0.5 * jnp.sum(jnp.mean(e * e, axis=-1, keepdims=True), axis=0, keepdims=True)

    def f_final(is_ctx, rows, params):
        (x1v, dv, tgt), (g2v, fg) = rows, params
        lv, vjp = jax.vjp(lambda a, b, c, d: _tile_loss(a, b, c, d, tgt), x1v, dv, g2v, fg)
        dx2, dd, dg2, dfg = vjp(jnp.ones((1, 1), F32))
        return [dx2, dd], [dg2, dfg, jnp.broadcast_to(lv, (1, 128))]

    (dx2, dd), (dg2, dfinal_g, loss_v) = rw_lat("final", f_final, [(x1, 0, D), (dn, 0, D), (target, 0, D)],
                                                [g2, final_g], [(D, F32), (D, BF16)], [(1, D), (1, D), (1, 128)])
    loss = loss_v[0, 0]

    grads = {'final_g': dfinal_g}

    def put(name, g):
        grads[name] = g
        if on_grad is not None:
            pending.append(on_grad(name, g))
    df = matmul("d_f", dd, w_down, 'nt', BF16)
    put('w_down', matmul("g_w_down", f, dd, 'tn', BF16))

    du, grads['ffn_conv_w'], grads['ffn_conv_b'] = ffn_mix_bwd(u, df, wt['ffn_conv_w'], row(wt['ffn_conv_b']),
                                                               t_lat)
    dh2 = matmul("d_h2", du, w_up_t, 'nn', BF16, after=sent())
    put('w_up', matmul("g_w_up", du, h2, 'tn', BF16))

    def b_norm2(is_ctx, rows, params):
        (xv, ov, dh2v, dx2v), (g1v, g, sc, sh) = rows, params
        _, vjp = jax.vjp(_res_norm2, xv, ov, g1v, g, sc, sh)
        dx, do, dg1v, dg, dsc, dsh = vjp((dx2v, dh2v))
        return [dx, do], [dg1v, dg, dsc, dsh]

    (dx_res, do), (dg1, dnorm2_g, dsc2, dsh2) = rw_lat(
        "norm2_bwd", b_norm2, [(x, 0, D), (o, 0, D), (dh2, 0, D), (dx2, 0, D)], [g1, norm2_g, sc2, sh2],
        [(D, F32), (D, BF16)], [(1, D)] * 4)
    grads['norm2_g'] = dnorm2_g
    dmrg = matmul("d_merge", do, w_out, 'nt', BF16, after=sent())
    put('w_out', matmul("g_w_out", mrg, do, 'tn', BF16))

    def b_merge(is_ctx, rows, params):
        (ya, yb, gl, dm), (bg,) = rows, params
        _, vjp = jax.vjp(_merge, ya, yb, gl, bg)
        dya, dyb, dgl, dbg = vjp(dm)
        return [dya, dyb, dgl], [dbg]

    (dy_a, dy_b, dgl), (grads['b_gate'],) = rw_lat(
        "merge_bwd", b_merge, [(y_a, 0, D), (y_b, 0, D), (z, Z_GL, 2 * D), (dmrg, 0, D)], [b_gate],
        [(D, BF16), (D, BF16), (2 * D, BF16)], [(1, 2 * D)])
    dattn = matmul("d_attn", dy_a, w_o_attn_t, 'nn', BF16, after=sent())
    put('w_o_attn', matmul("g_w_o_attn", dy_a, attn, 'tn', BF16))
    dybin = matmul("d_lru_out", dy_b, w_o_lru, 'nt', BF16, after=sent())
    put('w_o_lru', matmul("g_w_o_lru", ybin, dy_b, 'tn', BF16))

    def b_lru_out(is_ctx, rows, params):
        hf, hb, yb, dyv = rows
        _, vjp = jax.vjp(lambda s, y: s * _gelu(y), hf + hb, yb)
        dh, dyb = vjp(dyv)
        return [dh, dyb], []

    (dh_lru, dyb), _ = rw_lat("lru_out_bwd", b_lru_out,
                              [(h_f, 0, LRU_W), (h_b, 0, LRU_W), (z, Z_YB, LRU_W), (dybin, 0, LRU_W)], [],
                              [(LRU_W, F32), (LRU_W, BF16)], [])
    gate_params = (lru_w_a, lru_w_x, b_a, b_x, sp)
    dxc_f, *sums_f = lru_scan_bwd("lru_scan_f_bwd", xc, a_f, dh_lru, hp_f, None, *gate_params, 'f', n, t_lat)
    dxc, *sums_b = lru_scan_bwd("lru_scan_b_bwd", xc, a_b, dh_lru, hp_b, dxc_f, *gate_params, 'b', n, t_lat)
    dw_a, dw_x, db_a, db_x, dsp = (jnp.concatenate([f_, b_], axis=0) for f_, b_ in zip(sums_f, sums_b))
    put('lru_w_a', dw_a.reshape(2 * LRU_BLOCKS * LRU_BW, LRU_BW).astype(BF16))
    put('lru_w_x', dw_x.reshape(2 * LRU_BLOCKS * LRU_BW, LRU_BW).astype(BF16))
    grads['lru_b_a'], grads['lru_b_x'] = db_a, db_x
    grads['lru_lambda'] = -dsp * _sigmoid(-lam)
    dxb, grads['lru_conv_w'], grads['lru_conv_b'] = conv_bwd("lru_conv_bwd", dxc, z, Z_XB, LRU_W, wt['lru_conv_w'],
                                                             2, n, t_lat)

    dq, dk, dv = attn_bwd(qr, kr_, vr, attn, dattn, lse, t_lat, n, tq_bwd)

    def b_rope(is_ctx, rows, params):
        dqv, dkv, dvv, c, s1, s2 = rows
        live = jnp.where(is_ctx, 0.0, 1.0)
        dqo = jnp.concatenate([_rope_t(dqh, c, s1, s2) for dqh in _heads(dqv)], axis=1) * live
        dkh = _heads(dkv)
        dkr = dkh[0]
        for t in dkh[1:]:
            dkr = dkr + t
        lanes = lax.broadcasted_iota(jnp.int32, dkr.shape, 1)
        dkr = jnp.where((lanes >= QK_NOPE) & (lanes < QK_DIM), _rope_t(dkr, c, s1, s2), 0.0)
        return [dqo, jnp.concatenate([dkv, dvv], axis=1), dkr], []

    (dqp, dkvp, dkr), _ = rw("rope_bwd", b_rope,
                             [(dq, 0, hp), (dk, 0, hp), (dv, 0, hp), (c_tab, 0, HEAD_PAD), (s1_tab, 0, HEAD_PAD),
                              (s2_tab, 0, HEAD_PAD)], [], [(hp, BF16), (2 * hp, BF16), (HEAD_PAD, BF16)], [])
    dqn = matmul("d_qn", dqp, w_uq_t, 'nn', BF16, after=sent())
    put('w_uq', matmul("g_w_uq", dqp, qn, 'tn', BF16))
    dkvn = matmul("d_kvn", dkvp, w_ukv_t, 'nn', BF16, after=sent())
    put('w_ukv', matmul("g_w_ukv", dkvp, kvn, 'tn', BF16))

    def b_qkv_norm(is_ctx, rows, params):
        (ql, kvl, dqv, dkvv), (gq, gkv) = rows, params
        _, vjp_q = jax.vjp(_rms, ql, gq)
        _, vjp_kv = jax.vjp(_rms, kvl, gkv)
        dql, dgq = vjp_q(dqv)
        dkvl, dgkv = vjp_kv(dkvv)
        return [dql, dkvl], [dgq, dgkv]

    (dq_lat, dkv_lat), (grads['q_norm_g'], grads['kv_norm_g']) = rw(
        "qkv_norm_bwd", b_qkv_norm, [(z, Z_Q, Q_RANK), (z, Z_KV, KV_RANK), (dqn, 0, Q_RANK), (dkvn, 0, KV_RANK)],
        [q_g, kv_g], [(Q_RANK, BF16), (KV_RANK, BF16)], [(1, Q_RANK), (1, KV_RANK)])
    pad_ctx = lambda t: jnp.pad(t, ((0, n_ctx), (0, 0)))
    dz = jnp.concatenate([dq_lat, dkv_lat, dkr, dxb, pad_ctx(dyb), pad_ctx(dgl)], axis=1)
    put('w_in', matmul("g_w_in", dz, h, 'tn', BF16))
    dh = matmul("d_h", dz, w_in_t, 'nn', BF16, after=sent())

    def b_norm1(is_ctx, rows, params):
        (xl, xc_, dhv, dxr), (g, sc, sh) = rows, params
        scv, shv = _sel(is_ctx, sc), _sel(is_ctx, sh)
        _, vjp = jax.vjp(_norm_mod, jnp.where(is_ctx, xc_, xl), g, scv, shv)
        dx, dg, dsc, dsh = vjp(dhv)
        return [dx + dxr], [dg, _seg_acc(is_ctx, dsc), _seg_acc(is_ctx, dsh)]

    (grad_x,), (grads['norm1_g'], dsc1, dsh1) = rw("norm1_bwd", b_norm1, stream + [(dh, 0, D), (dx_res, 0, D)],
                                                   [norm1_g, sc1, sh1], [(D, F32, 'lat')],
                                                   [(1, D), (2, D), (2, D)])
    zero = jnp.zeros((D,), F32)
    dmod_l = jnp.concatenate([dsh1[0], dsc1[0], dg1[0], dsh2[0], dsc2[0], dg2[0]])
    dmod_c = jnp.concatenate([dsh1[1], dsc1[1], zero, zero, zero, zero])
    return loss, grad_x, grads, dmod_l, dmod_c


def kernel(x, c, ctx, c_ctx, w_mod, b_mod, norm1_g, w_in, b_gate, q_norm_g, kv_norm_g, w_uq, w_ukv, w_o_attn, lru_conv_w, lru_conv_b, lru_w_a, lru_b_a, lru_w_x, lru_b_x, lru_lambda, w_o_lru, w_out, norm2_g, w_up, ffn_conv_w, ffn_conv_b, w_down, final_g, loss_target, m_c_ctx, m_w_mod, m_b_mod, m_norm1_g, m_w_in, m_b_gate, m_q_norm_g, m_kv_norm_g, m_w_uq, m_w_ukv, m_w_o_attn, m_lru_conv_w, m_lru_conv_b, m_lru_w_a, m_lru_b_a, m_lru_w_x, m_lru_b_x, m_lru_lambda, m_w_o_lru, m_w_out, m_norm2_g, m_w_up, m_ffn_conv_w, m_ffn_conv_b, m_w_down, m_final_g, v_c_ctx, v_w_mod, v_b_mod, v_norm1_g, v_w_in, v_b_gate, v_q_norm_g, v_kv_norm_g, v_w_uq, v_w_ukv, v_w_o_attn, v_lru_conv_w, v_lru_conv_b, v_lru_w_a, v_lru_b_a, v_lru_w_x, v_lru_b_x, v_lru_lambda, v_w_o_lru, v_w_out, v_norm2_g, v_w_up, v_ffn_conv_w, v_ffn_conv_b, v_w_down, v_final_g):
    given = dict(locals())
    strip = lambda name, a: a if name in ('c_ctx', 'final_g') else a[0]
    wsh = {n: strip(n, given[n]) for n in WEIGHTS}
    msh = {n: strip(n, given['m_' + n]) for n in WEIGHTS}
    vsh = {n: strip(n, given['v_' + n]) for n in WEIGHTS}
    me = _my_index()

    small = _flat([c[0]] + [wsh[n] for n in SMALL_F32], F32, 8)
    small_all = all_gather("gather_small", small).reshape(N_DEV, -1)
    c_all = small_all[:, :D]
    full, at = {}, D
    for n in SMALL_F32:
        cnt = math.prod(wsh[n].shape)
        full[n] = _gathered_to_full(n, small_all[:, at:at + cnt].reshape((N_DEV,) + wsh[n].shape))
        at += cnt

    cond = jnp.concatenate([c_all, c_ctx[None], jnp.zeros((7, D), F32)], axis=0)
    sil = cond * jax.nn.sigmoid(cond)
    mod_cols = matmul("mod_proj", sil, wsh['w_mod'], 'nn', F32)
    mod_all = all_gather("gather_mod", mod_cols)
    mod_all = jnp.transpose(mod_all, (1, 0, 2)).reshape(16, 6 * D) + b_mod[0][None]
    mod_l = lax.dynamic_index_in_dim(mod_all, me, axis=0, keepdims=False)
    mod_c = mod_all[N_DEV]

    rb_shards = {n: _shard_to_rb(n, wsh[n]).astype(BF16) for n in BIG_BF16}
    (w_in_blocks,) = all_gather_multi("gather_w_in", [rb_shards['w_in']])
    later = [n for n in BIG_BF16 if n != 'w_in']
    weights_started, weights_sent = exchange_start("weights_send", 'gather', [rb_shards[n] for n in later],
                                                   after=[w_in_blocks, mod_all])
    for n in REPLICATED:
        if n not in ('c_ctx', 'b_mod'):
            full[n] = wsh[n]

    def arrive(names, after):
        if names == ('w_in',):
            return {'w_in': _rb_from_gathered('w_in', w_in_blocks), '_token': weights_sent}
        picked = [later.index(n) for n in names]
        lands = exchange_wait("weights_wait_" + names[0], 'gather',
                              tuple([part[i] for i in picked] for part in weights_started), after)
        return {n: _rb_from_gathered(n, lax.dynamic_update_slice_in_dim(land, rb_shards[n][None], me, axis=0))
                for n, land in zip(names, lands)}

    in_flight = {}

    def on_grad(n, g):
        chunks = _chunks_from_rb_grad(n, g)
        own = lax.dynamic_index_in_dim(chunks, me, axis=0, keepdims=True)
        started, token = exchange_start("grad_send_" + n, 'scatter', [chunks])
        in_flight[n] = (own, started)
        return token

    loss, grad_x, grads, dmod_l, dmod_c = local_step(x[0], ctx[0], loss_target[0], mod_l, mod_c, full, on_grad,
                                                     arrive)
    dmod = jnp.stack([dmod_l, dmod_c]).reshape(2 * 6 * D // FLAT_C, FLAT_C)
    dm = all_gather("gather_dmod", dmod).reshape(N_DEV, 2, 6 * D)
    dmod_c_tot = dm[0, 1]
    for p in range(1, N_DEV):
        dmod_c_tot = dmod_c_tot + dm[p, 1]
    dm16 = jnp.concatenate([dm[:, 0], dmod_c_tot[None], jnp.zeros((7, 6 * D), F32)], axis=0)
    ncol = 6 * D // N_DEV
    dm16_cols = lax.dynamic_slice_in_dim(dm16.reshape(16, N_DEV, ncol), me, 1, axis=1)[:, 0]
    grad_w_mod = matmul("g_w_mod", sil, dm16_cols, 'tn', F32)
    dsil = matmul("d_cond", dm16_cols, wsh['w_mod'], 'nt', F32)
    sg = jax.nn.sigmoid(c_ctx)
    grads['c_ctx'] = dsil[N_DEV] * (sg * (1.0 + c_ctx * (1.0 - sg)))
    grads['b_mod'] = dmod_l + dmod_c

    g_final = {'w_mod': grad_w_mod}
    reduced, stepped = {}, {}
    for n in BIG_BF16 + ['lru_w_a', 'lru_w_x']:
        own, started = in_flight[n]
        (land,) = exchange_wait("grad_wait_" + n, 'scatter', started, dm)
        if n in ROW_SHARDED:
            g_final[n], *stepped[n] = reduce_slots("step_" + n, land, own, (wsh[n], msh[n], vsh[n]))
        elif n in COL_SHARDED and wsh[n].shape[1] % 128:
            g_t, *outs = reduce_slots("step_" + n, land, own, (wsh[n].T, msh[n].T, vsh[n].T))
            g_final[n], stepped[n] = g_t.T, [o.T for o in outs]
        else:
            reduced[n] = reduce_slots("sum_" + n, land, own)
            if n in BIG_BF16:
                g_final[n] = _rb_to_shard(n, reduced[n])

    small_names = SMALL_F32 + [n for n in REPLICATED if n not in ('lru_w_a', 'lru_w_x')]
    partials = _flat([grads[n] for n in small_names] + [loss], F32, 8)
    parts_all, a_all, x_all = all_gather_multi("gather_small_grads", [partials, reduced['lru_w_a'], reduced['lru_w_x']])
    small_sum = sum_slots("sum_small", parts_all).reshape(-1)
    g_final['lru_w_a'], g_final['lru_w_x'] = a_all.reshape(wsh['lru_w_a'].shape), x_all.reshape(wsh['lru_w_x'].shape)
    at = 0
    for n in small_names:
        cnt = math.prod(full[n].shape) if n in SMALL_F32 else math.prod(wsh[n].shape)
        g = small_sum[at:at + cnt]
        if n in SMALL_F32:
            k = full[n].shape[0]
            g = lax.dynamic_index_in_dim(g.reshape(k, N_DEV, -1), me, axis=1, keepdims=False)
        g_final[n] = g.reshape(wsh[n].shape)
        at += cnt
    loss = small_sum[at]

    for n in ['w_mod'] + BIG_BF16:
        if n not in stepped:
            stepped[n] = adamw("adamw_" + n, wsh[n], g_final[n], msh[n], vsh[n])
    rest = [n for n in WEIGHTS if n not in stepped]
    as2d = lambda a: a.reshape(-1, a.shape[-1])
    rest_out = adamw_many("adamw_small", *[[as2d(d[n]) for n in rest] for d in (wsh, g_final, msh, vsh)])
    stepped.update(zip(rest, rest_out))
    shaped = lambda n, a: a.reshape(given[n].shape)
    return (loss, grad_x[None],
            *[shaped(n, g_final[n]) for n in WEIGHTS],
            *[shaped(n, stepped[n][k]) for k in range(3) for n in WEIGHTS])
```

```python
import functools
import math

import jax
import jax.numpy as jnp
from jax import lax
from jax.experimental import pallas as pl
from jax.experimental.pallas import tpu as pltpu

F32 = jnp.float32
BF16 = jnp.bfloat16
MESH = pl.DeviceIdType.MESH

N_DEV = 8
D = 1024
N_HEADS = 8
HEAD_PAD = 128
QK_NOPE, QK_ROPE, V_HEAD = 64, 32, 64
QK_DIM = QK_NOPE + QK_ROPE
Q_RANK, KV_RANK = 384, 256
LRU_W, LRU_BLOCKS, LRU_BW = 1280, 10, 128
FFN = 2816
GRID_W = 64
ROPE_BASE = 10000.0
LRU_C = 8.0
EPS = 1e-6
Z_Q, Z_KV, Z_KR, Z_XB, Z_YB, Z_GL, Z_END = 0, 384, 640, 768, 2048, 3328, 5376
ADAM_LR, ADAM_B1, ADAM_B2, ADAM_EPS, ADAM_WD, ADAM_STEP = 0.001, 0.9, 0.999, 1e-08, 0.01, 10

VMEM_LIMIT = 52 * 1024 * 1024
FLAT_C = 512

WEIGHTS = ['c_ctx', 'w_mod', 'b_mod', 'norm1_g', 'w_in', 'b_gate', 'q_norm_g', 'kv_norm_g', 'w_uq', 'w_ukv',
           'w_o_attn', 'lru_conv_w', 'lru_conv_b', 'lru_w_a', 'lru_b_a', 'lru_w_x', 'lru_b_x', 'lru_lambda',
           'w_o_lru', 'w_out', 'norm2_g', 'w_up', 'ffn_conv_w', 'ffn_conv_b', 'w_down', 'final_g']
COL_SHARDED = ['w_in', 'w_uq', 'w_ukv', 'w_o_attn', 'lru_conv_w', 'lru_b_a', 'lru_b_x', 'lru_lambda', 'w_up',
               'ffn_conv_w']
ROW_SHARDED = ['w_o_lru', 'w_out', 'w_down']
BIG_BF16 = ['w_in', 'w_uq', 'w_ukv', 'w_o_attn', 'w_o_lru', 'w_out', 'w_up', 'w_down']
SMALL_F32 = ['lru_conv_w', 'lru_b_a', 'lru_b_x', 'lru_lambda', 'ffn_conv_w']
REPLICATED = ['c_ctx', 'b_mod', 'norm1_g', 'b_gate', 'q_norm_g', 'kv_norm_g', 'lru_conv_b', 'lru_w_a', 'lru_w_x',
              'norm2_g', 'ffn_conv_b', 'final_g']


def _cparams(sem=None):
    return pltpu.CompilerParams(dimension_semantics=sem, vmem_limit_bytes=VMEM_LIMIT)


def _pick(n, cands):
    for c in cands:
        if c <= n and n % c == 0:
            return c
    return n


def _best_div(n, mult, cap):
    best = mult
    for d in range(mult, min(n, cap) + 1, mult):
        if n % d == 0:
            best = d
    return best


MXU_DIM = 256
ROW_TILES = (1088, 1024, 544, 512, 256, 128, 64, 32, 16, 8)
LANE_TILES = (2816, 1792, 1536, 1280, 1024, 768, 512, 256, 1408, 896, 640, 384, 128)
DEPTH_ROW_TILES = (2176, 2048, 1024, 512, 256, 1088, 128, 64, 32, 16, 8)
MATMUL_VMEM_BUDGET = 40 * 1024 * 1024
MXU_FILL_OK = 0.9


def _my_pos():
    return lax.axis_index("x"), lax.axis_index("y"), lax.axis_index("c")


def _my_index():
    x, y, c = _my_pos()
    return 4 * x + 2 * y + c


def all_gather_multi(name, shards):
    n_arr = len(shards)
    arrays = range(n_arr)

    def body(*refs):
        x_refs, out_refs = refs[:n_arr], refs[n_arr:2 * n_arr]
        send_sems, recv_sems, local_sems = refs[2 * n_arr:]
        x, y, c = _my_pos()
        me, sibling = (x, y, c), (x, y, 1 - c)
        chips = [(1 - x, y), (x, 1 - y), (1 - x, 1 - y)]

        def slot(a, px, py, pc):
            return out_refs[a].at[4 * px + 2 * py + pc]

        def copy(a, k, block, to, src=None):
            return pltpu.make_async_remote_copy(
                src_ref=slot(a, *block) if src is None else src, dst_ref=slot(a, *block),
                send_sem=send_sems.at[7 * a + k], recv_sem=recv_sems.at[7 * a + k], device_id=to,
                device_id_type=MESH)

        mine = [pltpu.make_async_copy(x_refs[a], slot(a, *me), local_sems.at[a]) for a in arrays]
        first = [copy(a, 1 + j, me, (*chip, c), src=x_refs[a]) for j, chip in enumerate(chips) for a in arrays]
        first += [copy(a, 0, me, sibling, src=x_refs[a]) for a in arrays]
        for cp in first + mine:
            cp.start()
        passed = []
        for j, chip in enumerate(chips):
            for a in arrays:
                copy(a, 1 + j, (*chip, c), me).wait_recv()
                passed.append(copy(a, 4 + j, (*chip, c), sibling))
                passed[-1].start()
        for a in arrays:
            copy(a, 0, sibling, me).wait_recv()
            for j, chip in enumerate(chips):
                copy(a, 4 + j, (*chip, 1 - c), me).wait_recv()
        for cp in first + passed:
            cp.wait_send()
        for cp in mine:
            cp.wait()

    hbm = pl.BlockSpec(memory_space=pl.ANY)
    return pl.pallas_call(
        body, name=name,
        out_shape=[jax.ShapeDtypeStruct((N_DEV,) + s.shape, s.dtype) for s in shards],
        in_specs=[hbm] * n_arr, out_specs=[hbm] * n_arr,
        scratch_shapes=[pltpu.SemaphoreType.DMA((7 * n_arr,)), pltpu.SemaphoreType.DMA((7 * n_arr,)),
                        pltpu.SemaphoreType.DMA((n_arr,))],
    )(*shards)


def all_gather(name, shard):
    return all_gather_multi(name, [shard])[0]


def _peers():
    x, y, c = _my_pos()
    out = []
    for rel in (6, 4, 2, 7, 5, 3, 1):
        px, py, pc = x ^ ((rel >> 2) & 1), y ^ ((rel >> 1) & 1), c ^ (rel & 1)
        out.append((rel - 1, (px, py, pc), 4 * px + 2 * py + pc))
    return out


def _exchange_copies(mode, src_refs, land_refs, send_sems, recv_sems, with_arrivals):
    x, y, c = _my_pos()
    me = 4 * x + 2 * y + c
    sends, arrivals = [], []
    for k, peer_pos, peer in _peers():
        for a, (src, land) in enumerate(zip(src_refs, land_refs)):
            piece = src.at[peer] if mode == 'scatter' else src
            sems = dict(send_sem=send_sems[a].at[k], recv_sem=recv_sems[a].at[k], device_id_type=MESH)
            sends.append(pltpu.make_async_remote_copy(src_ref=piece, dst_ref=land.at[me], device_id=peer_pos, **sems))
            if with_arrivals:
                arrivals.append(pltpu.make_async_remote_copy(src_ref=piece, dst_ref=land.at[peer],
                                                             device_id=(x, y, c), **sems))
    return sends, arrivals


_HBM = pl.BlockSpec(memory_space=pltpu.HBM)
_SEM = pl.BlockSpec(memory_space=pltpu.SEMAPHORE)


def exchange_start(name, mode, arrays, after=()):
    n_arr, n_after = len(arrays), len(after)
    land_shapes = [a.shape if mode == 'scatter' else (N_DEV,) + a.shape for a in arrays]

    def body(*refs):
        src_refs, land_refs = refs[:n_arr], refs[n_arr:2 * n_arr]
        refs = refs[n_after:]
        send_sems, recv_sems = refs[2 * n_arr:3 * n_arr], refs[3 * n_arr:4 * n_arr]
        sends, _ = _exchange_copies(mode, src_refs, land_refs, send_sems, recv_sems, with_arrivals=False)
        for cp in sends:
            cp.start()
        token = refs[-1]
        token[...] = jnp.zeros_like(token)

    sem = pltpu.SemaphoreType.DMA((N_DEV - 1,))
    res = pl.pallas_call(
        body, name=name,
        out_shape=[sem] * (2 * n_arr) + [pltpu.HBM(a.shape, a.dtype) for a in arrays]
        + [pltpu.HBM(s, a.dtype) for s, a in zip(land_shapes, arrays)] + [jax.ShapeDtypeStruct((8, 128), F32)],
        in_specs=[_HBM] * (2 * n_arr) + [pl.BlockSpec(memory_space=pl.ANY)] * n_after,
        out_specs=[_SEM] * (2 * n_arr) + [_HBM] * (2 * n_arr) + [pl.BlockSpec(memory_space=pltpu.VMEM)],
        input_output_aliases={i: 2 * n_arr + i for i in range(2 * n_arr)},
        compiler_params=pltpu.CompilerParams(has_side_effects=pltpu.SideEffectType.DATAFLOW_SIDE_EFFECTING),
    )(*[pltpu.with_memory_space_constraint(a, pltpu.HBM) for a in arrays],
      *[pltpu.with_memory_space_constraint(lax.empty(s, a.dtype), pltpu.HBM) for s, a in zip(land_shapes, arrays)],
      *after)
    return (res[:n_arr], res[n_arr:2 * n_arr], res[2 * n_arr:3 * n_arr], res[3 * n_arr:4 * n_arr]), res[-1]


def exchange_wait(name, mode, started, after):
    send_sems, recv_sems, thru, land = started
    n_arr = len(thru)

    def body(*refs):
        src_refs, land_refs = refs[:n_arr], refs[n_arr:2 * n_arr]
        s_sems, r_sems = refs[2 * n_arr:3 * n_arr], refs[3 * n_arr:4 * n_arr]
        sends, arrivals = _exchange_copies(mode, src_refs, land_refs, s_sems, r_sems, with_arrivals=True)
        for cp in sends:
            cp.wait_send()
        for cp in arrivals:
            cp.wait_recv()

    res = pl.pallas_call(
        body, name=name,
        out_shape=[pltpu.HBM(a.shape, a.dtype) for a in thru] + [pltpu.HBM(a.shape, a.dtype) for a in land],
        in_specs=[_HBM] * (2 * n_arr) + [_SEM] * (2 * n_arr) + [pl.BlockSpec(memory_space=pl.ANY)],
        out_specs=[_HBM] * (2 * n_arr),
        input_output_aliases={i: i for i in range(2 * n_arr)},
        compiler_params=pltpu.CompilerParams(has_side_effects=pltpu.SideEffectType.DATAFLOW_SIDE_EFFECTING),
    )(*thru, *land, *send_sems, *recv_sems, after)
    return res[n_arr:]


def _sum_with_own(slot_ref, own_ref):
    x, y, c = _my_pos()
    me = 4 * x + 2 * y + c
    acc = None
    for p in range(N_DEV):
        v = jnp.where(me == p, own_ref[0], slot_ref[p]).astype(F32)
        acc = v if acc is None else acc + v
    return acc


def reduce_slots(name, slots, own, step=None):
    _, r, ccols = slots.shape
    tc = _pick(ccols, (256, 128))
    c1 = 1.0 - ADAM_B1 ** ADAM_STEP
    c2 = 1.0 - ADAM_B2 ** ADAM_STEP

    def body(s_ref, own_ref, *refs):
        g = _sum_with_own(s_ref, own_ref)
        if step is None:
            refs[0][...] = g
            return
        w_ref, m_ref, v_ref, g_ref, d_ref, nm_ref, nv_ref = refs
        nm = ADAM_B1 * m_ref[...] + (1.0 - ADAM_B1) * g
        nv = ADAM_B2 * v_ref[...] + (1.0 - ADAM_B2) * (g * g)
        g_ref[...] = g
        d_ref[...] = -ADAM_LR * ((nm / c1) / (jnp.sqrt(nv / c2) + ADAM_EPS) + ADAM_WD * w_ref[...])
        nm_ref[...] = nm
        nv_ref[...] = nv

    col = pl.BlockSpec((r, tc), lambda j: (0, j))
    n_out = 1 if step is None else 4
    res = pl.pallas_call(
        body, name=name, grid=(ccols // tc,),
        out_shape=[jax.ShapeDtypeStruct((r, ccols), F32)] * n_out,
        in_specs=[pl.BlockSpec((N_DEV, r, tc), lambda j: (0, 0, j)), pl.BlockSpec((1, r, tc), lambda j: (0, 0, j))]
        + [col] * (0 if step is None else 3),
        out_specs=[col] * n_out,
        compiler_params=_cparams(("parallel",)),
    )(slots, own, *(step or ()))
    return res[0] if step is None else res


def sum_slots(name, slots):
    _, r, ccols = slots.shape
    tc = _pick(ccols, (256, 128))

    def body(s_ref, o_ref):
        acc = s_ref[0].astype(F32)
        for p in range(1, N_DEV):
            acc = acc + s_ref[p].astype(F32)
        o_ref[...] = acc

    return pl.pallas_call(
        body, name=name, grid=(ccols // tc,),
        out_shape=jax.ShapeDtypeStruct((r, ccols), F32),
        in_specs=[pl.BlockSpec((N_DEV, r, tc), lambda j: (0, 0, j))],
        out_specs=pl.BlockSpec((r, tc), lambda j: (0, j)),
        compiler_params=_cparams(("parallel",)),
    )(slots)


def _mxu_fill(t):
    return t / (-(-t // MXU_DIM) * MXU_DIM)


def _matmul_tiles(mode, m_extent, n, k_extent, k_total, itemsizes):
    a_bytes, b_bytes, o_bytes = itemsizes
    m_cands = [c for c in (LANE_TILES if mode == 'tn' else ROW_TILES) if m_extent % c == 0] or [m_extent]
    k_cands = [c for c in (DEPTH_ROW_TILES if mode == 'tn' else LANE_TILES) if k_extent % c == 0] or [k_extent]
    n_cands = [c for c in LANE_TILES if n % c == 0] or [n]
    best = None
    for tm in m_cands:
        for tk in k_cands:
            for tn in n_cands:
                f32_tiles = 2 if k_total // tk > 1 else 1
                vmem = 2 * (tm * tk * a_bytes + tk * tn * b_bytes + tm * tn * o_bytes) + tm * tn * 4 * f32_tiles
                if vmem > MATMUL_VMEM_BUDGET:
                    continue
                key = (_mxu_fill(tn) * _mxu_fill(tk) >= MXU_FILL_OK, tm * tn * tk)
                if best is None or key > best[0]:
                    best = (key, (tm, tn, tk))
    assert best is not None, (mode, m_extent, n, k_extent)
    return best[1]


def matmul(name, a, b, mode, out_dtype, after=()):
    after = [t for t in after if t is not None]
    pieces, a_rows, a_cols = (1,) + a.shape if a.ndim == 2 else a.shape
    if mode == 'nn':
        (m, k), (k2, n) = (a_rows, pieces * a_cols), b.shape
    elif mode == 'nt':
        (m, k), (n, k2) = (a_rows, pieces * a_cols), b.shape
    else:
        (k, m), (k2, n) = (a_rows, pieces * a_cols), b.shape
    assert k == k2, (name, a.shape, b.shape, mode)
    tm, tn, tk = _matmul_tiles(mode, a_cols if mode == 'tn' else m, n, k if mode == 'tn' else a_cols, k,
                               (a.dtype.itemsize, b.dtype.itemsize, jnp.dtype(out_dtype).itemsize))
    nk = k // tk
    per_piece = a_cols // (tm if mode == 'tn' else tk)
    if a.ndim == 2:
        a_block = lambda rows, cols, at: pl.BlockSpec((rows, cols), at)
    else:
        a_block = lambda rows, cols, at: pl.BlockSpec(
            (None, rows, cols), lambda i, j, kk: (at(i, j, kk)[1] // per_piece, at(i, j, kk)[0],
                                                  at(i, j, kk)[1] % per_piece))
    if mode == 'nn':
        a_spec = a_block(tm, tk, lambda i, j, kk: (i, kk))
        b_spec = pl.BlockSpec((tk, tn), lambda i, j, kk: (kk, j))
        dn = (((1,), (0,)), ((), ()))
    elif mode == 'nt':
        a_spec = a_block(tm, tk, lambda i, j, kk: (i, kk))
        b_spec = pl.BlockSpec((tn, tk), lambda i, j, kk: (j, kk))
        dn = (((1,), (1,)), ((), ()))
    else:
        a_spec = a_block(tk, tm, lambda i, j, kk: (kk, i))
        b_spec = pl.BlockSpec((tk, tn), lambda i, j, kk: (kk, j))
        dn = (((0,), (0,)), ((), ()))

    def product(a_ref, b_ref):
        return lax.dot_general(a_ref[...].astype(BF16), b_ref[...].astype(BF16), dn, preferred_element_type=F32)

    n_after = len(after)

    def body_one(a_ref, b_ref, *rest):
        o_ref = rest[n_after]
        o_ref[...] = product(a_ref, b_ref).astype(o_ref.dtype)

    def body(a_ref, b_ref, *rest):
        o_ref, acc_ref = rest[n_after:]
        kk = pl.program_id(2)

        @pl.when(kk == 0)
        def _():
            acc_ref[...] = jnp.zeros_like(acc_ref)

        acc_ref[...] += product(a_ref, b_ref)

        @pl.when(kk == nk - 1)
        def _():
            o_ref[...] = acc_ref[...].astype(o_ref.dtype)

    return pl.pallas_call(
        body_one if nk == 1 else body, name=name, grid=(m // tm, n // tn, nk),
        out_shape=jax.ShapeDtypeStruct((m, n), out_dtype),
        in_specs=[a_spec, b_spec] + [pl.BlockSpec(memory_space=pl.ANY)] * n_after,
        out_specs=pl.BlockSpec((tm, tn), lambda i, j, kk: (i, j)),
        scratch_shapes=[] if nk == 1 else [pltpu.VMEM((tm, tn), F32)],
        compiler_params=_cparams(("parallel", "parallel", "arbitrary")),
    )(a, b, *after)


def rowwise(name, fn, rows, params, out_rows, out_accs, n_rows, t_lat, tm):
    nb, nbl = n_rows // tm, t_lat // tm
    in_specs, piece_counts = [], []
    operands = []
    for arr, off, width, *kind in rows:
        g = math.gcd(off, width) if off else width
        assert g % 128 == 0 or (off == 0 and width == arr.shape[1]), (name, off, width)
        cnt = width // g
        last = arr.shape[0] // tm - 1
        clamp = arr.shape[0] < n_rows
        for p in range(cnt):
            cb = off // g + p
            if kind == ['ctx']:
                in_specs.append(pl.BlockSpec(
                    (tm, g), lambda i, cb=cb, last=last: (jnp.clip(i - nbl, 0, last), cb)))
            elif clamp:
                in_specs.append(pl.BlockSpec((tm, g), lambda i, cb=cb, last=last: (jnp.minimum(i, last), cb)))
            else:
                in_specs.append(pl.BlockSpec((tm, g), lambda i, cb=cb: (i, cb)))
            operands.append(arr)
        piece_counts.append(cnt)
    for p in params:
        in_specs.append(pl.BlockSpec(p.shape, lambda i, nd=p.ndim: (0,) * nd))
        operands.append(p)
    n_in = sum(piece_counts)
    n_par = len(params)
    n_or = len(out_rows)
    lat_only = [kind == ['lat'] for _, _, *kind in out_rows]
    out_shape = [jax.ShapeDtypeStruct((t_lat if lat else n_rows, w), dt)
                 for (w, dt, *_), lat in zip(out_rows, lat_only)]
    out_shape += [jax.ShapeDtypeStruct(s, F32) for s in out_accs]
    out_specs = [pl.BlockSpec((tm, w), (lambda i: (jnp.minimum(i, nbl - 1), 0)) if lat else (lambda i: (i, 0)))
                 for (w, *_), lat in zip(out_rows, lat_only)]
    out_specs += [pl.BlockSpec(s, lambda i, nd=len(s): (0,) * nd) for s in out_accs]

    def body(*refs):
        in_refs, par_refs = refs[:n_in], refs[n_in:n_in + n_par]
        orow_refs = refs[n_in + n_par:n_in + n_par + n_or]
        oacc_refs = refs[n_in + n_par + n_or:]
        i = pl.program_id(0)
        tiles, at = [], 0
        for cnt in piece_counts:
            parts = [in_refs[at + p][...].astype(F32) for p in range(cnt)]
            tiles.append(parts[0] if cnt == 1 else jnp.concatenate(parts, axis=1))
            at += cnt
        is_ctx = i * tm >= t_lat
        outs, accs = fn(is_ctx, tiles, [p[...] for p in par_refs])
        for o_ref, o, lat in zip(orow_refs, outs, lat_only):
            if lat:
                @pl.when(jnp.logical_not(is_ctx))
                def _(o_ref=o_ref, o=o):
                    o_ref[...] = o.astype(o_ref.dtype)
            else:
                o_ref[...] = o.astype(o_ref.dtype)
        if oacc_refs:
            @pl.when(i == 0)
            def _():
                for a_ref in oacc_refs:
                    a_ref[...] = jnp.zeros_like(a_ref)
            for a_ref, a in zip(oacc_refs, accs):
                a_ref[...] += a.astype(F32)

    res = pl.pallas_call(
        body, name=name, grid=(nb,),
        out_shape=out_shape, in_specs=in_specs, out_specs=out_specs,
        compiler_params=_cparams(("arbitrary",)),
    )(*operands)
    return res[:n_or], res[n_or:]


def _rms(x, g):
    return x * lax.rsqrt(jnp.mean(x * x, axis=-1, keepdims=True) + EPS) * g


def _norm_mod(x, g, sc, sh):
    return _rms(x, g) * (1.0 + sc) + sh


def _sigmoid(x):
    return 0.5 * jnp.tanh(0.5 * x) + 0.5


def _silu(x):
    return x * _sigmoid(x)


def _gelu(x):
    return 0.5 * x * (1.0 + jnp.tanh(math.sqrt(2.0 / math.pi) * (x + 0.044715 * (x * x * x))))


def _sel(is_ctx, p):
    return jnp.where(is_ctx, p[1:2], p[0:1])


def _seg_acc(is_ctx, v):
    rows = lax.broadcasted_iota(jnp.int32, (2, v.shape[1]), 0)
    return jnp.where(rows == is_ctx.astype(jnp.int32), jnp.broadcast_to(v, (2, v.shape[1])), 0.0)


def _rsum(v):
    return jnp.sum(v, axis=0, keepdims=True)


def _shift_rows(x, o, t_lat, n):
    if o == 0:
        return x
    y = pltpu.roll(x, (-o) % n, 0)
    t = lax.broadcasted_iota(jnp.int32, x.shape, 0)
    if o > 0:
        ok = t < n - o
        if t_lat < n:
            ok = ok & ((t < t_lat - o) | (t >= t_lat))
    else:
        ok = t >= -o
        if t_lat < n:
            ok = ok & ((t < t_lat) | (t >= t_lat - o))
    return jnp.where(ok, y, 0.0)


def conv_fwd(name, xarr, col_off, width, w, b, left, n_rows, t_lat, out_dtype, cb=128):
    taps = w.shape[0]
    assert col_off % cb == 0 and width % cb == 0

    def body(x_ref, w_ref, b_ref, o_ref):
        x = x_ref[...].astype(F32)
        acc = jnp.broadcast_to(b_ref[...], x.shape)
        for k in range(taps):
            acc = acc + _shift_rows(x, k - left, t_lat, n_rows) * w_ref[k:k + 1, :]
        o_ref[...] = acc.astype(o_ref.dtype)

    return pl.pallas_call(
        body, name=name, grid=(width // cb,),
        out_shape=jax.ShapeDtypeStruct((n_rows, width), out_dtype),
        in_specs=[pl.BlockSpec((n_rows, cb), lambda j: (0, col_off // cb + j)),
                  pl.BlockSpec((taps, cb), lambda j: (0, j)),
                  pl.BlockSpec((1, cb), lambda j: (0, j))],
        out_specs=pl.BlockSpec((n_rows, cb), lambda j: (0, j)),
        compiler_params=_cparams(("parallel",)),
    )(xarr, w, b)


def conv_bwd(name, dout, xarr, col_off, width, w, left, n_rows, t_lat, cb=128):
    taps = w.shape[0]

    def body(d_ref, x_ref, w_ref, dx_ref, dw_ref, db_ref):
        d = d_ref[...].astype(F32)
        x = x_ref[...].astype(F32)
        dx = jnp.zeros_like(d)
        dws = []
        for k in range(taps):
            dx = dx + _shift_rows(d, left - k, t_lat, n_rows) * w_ref[k:k + 1, :]
            dws.append(_rsum(d * _shift_rows(x, k - left, t_lat, n_rows)))
        dx_ref[...] = dx.astype(dx_ref.dtype)
        dw_ref[...] = jnp.concatenate(dws, axis=0)
        db_ref[...] = _rsum(d)

    return pl.pallas_call(
        body, name=name, grid=(width // cb,),
        out_shape=[jax.ShapeDtypeStruct((n_rows, width), BF16), jax.ShapeDtypeStruct((taps, width), F32),
                   jax.ShapeDtypeStruct((1, width), F32)],
        in_specs=[pl.BlockSpec((n_rows, cb), lambda j: (0, j)),
                  pl.BlockSpec((n_rows, cb), lambda j: (0, col_off // cb + j)),
                  pl.BlockSpec((taps, cb), lambda j: (0, j))],
        out_specs=[pl.BlockSpec((n_rows, cb), lambda j: (0, j)), pl.BlockSpec((taps, cb), lambda j: (0, j)),
                   pl.BlockSpec((1, cb), lambda j: (0, j))],
        compiler_params=_cparams(("parallel",)),
    )(dout, xarr, w)


def _ffn_conv(a, w_ref, b_ref, t_lat):
    shifted = [_shift_rows(a, k - 1, t_lat, t_lat) for k in range(3)]
    ac = jnp.broadcast_to(b_ref[...], a.shape)
    for k in range(3):
        ac = ac + shifted[k] * w_ref[k:k + 1, :]
    return ac, shifted


def ffn_mix_fwd(u, w, b, t_lat, cb=128):
    nblk = FFN // cb

    def body(a_ref, g_ref, w_ref, b_ref, f_ref):
        ac, _ = _ffn_conv(a_ref[...].astype(F32), w_ref, b_ref, t_lat)
        f_ref[...] = (_silu(ac) * g_ref[...].astype(F32)).astype(f_ref.dtype)

    col = lambda shape, off=0: pl.BlockSpec(shape, lambda j: (0, off + j))
    return pl.pallas_call(
        body, name="ffn_mix", grid=(nblk,),
        out_shape=jax.ShapeDtypeStruct((t_lat, FFN), BF16),
        in_specs=[col((t_lat, cb)), col((t_lat, cb), nblk), col((3, cb)), col((1, cb))],
        out_specs=col((t_lat, cb)),
        compiler_params=_cparams(("parallel",)),
    )(u, u, w, b)


def ffn_mix_bwd(u, df, w, b, t_lat, cb=128):
    nblk = FFN // cb

    def body(a_ref, g_ref, df_ref, w_ref, b_ref, du_ref, dw_ref, db_ref):
        ac, shifted = _ffn_conv(a_ref[...].astype(F32), w_ref, b_ref, t_lat)
        d = df_ref[...].astype(F32)
        s = _sigmoid(ac)
        du_ref[1] = (d * (ac * s)).astype(du_ref.dtype)
        dac = d * g_ref[...].astype(F32) * (s * (1.0 + ac * (1.0 - s)))
        da = jnp.zeros_like(dac)
        for k in range(3):
            da = da + _shift_rows(dac, 1 - k, t_lat, t_lat) * w_ref[k:k + 1, :]
        du_ref[0] = da.astype(du_ref.dtype)
        dw_ref[...] = jnp.concatenate([_rsum(dac * shifted[k]) for k in range(3)], axis=0)
        db_ref[...] = _rsum(dac)

    col = lambda shape, off=0: pl.BlockSpec(shape, lambda j: (0, off + j))
    return pl.pallas_call(
        body, name="ffn_mix_bwd", grid=(nblk,),
        out_shape=[jax.ShapeDtypeStruct((2, t_lat, FFN), BF16),
                   jax.ShapeDtypeStruct((3, FFN), F32), jax.ShapeDtypeStruct((1, FFN), F32)],
        in_specs=[col((t_lat, cb)), col((t_lat, cb), nblk), col((t_lat, cb)), col((3, cb)), col((1, cb))],
        out_specs=[pl.BlockSpec((2, t_lat, cb), lambda j: (0, 0, j)), col((3, cb)), col((1, cb))],
        compiler_params=_cparams(("parallel",)),
    )(u, u, df, w, b)


def _chunk_order(direction, nb, nbl):
    if direction == 'f':
        return lambda s: ((s + nbl) % nb, 0)
    return lambda s: (nb - 1 - s, 0)


def _adjoint_order(direction, nb, nbl):
    if direction == 'f':
        return lambda s: ((nb - 1 - s + nbl) % nb, 0)
    return lambda s: (s, 0)


SUBLANES = 8


def _chunk_scan(a, b, carry, rev):
    tc, width = a.shape
    nt = tc // SUBLANES
    row = lax.broadcasted_iota(jnp.int32, a.shape, 0)
    a, b = a.reshape(nt, SUBLANES, width), b.reshape(nt, SUBLANES, width)
    in_tile = lax.broadcasted_iota(jnp.int32, a.shape, 1)
    for k in (1, 2, 4):
        shift = SUBLANES - k if rev else k
        edge = in_tile >= SUBLANES - k if rev else in_tile < k
        b = jnp.where(edge, b, a * pltpu.roll(b, shift, 1) + b)
        a = jnp.where(edge, a, a * pltpu.roll(a, shift, 1))
    a, b = a.reshape(tc, width), b.reshape(tc, width)
    hs = [None] * nt
    c = carry
    for kt in range(nt):
        k = nt - 1 - kt if rev else kt
        h = b[k * SUBLANES:(k + 1) * SUBLANES] + a[k * SUBLANES:(k + 1) * SUBLANES] * c
        hs[k] = h
        c = h[0:1] if rev else h[SUBLANES - 1:SUBLANES]
    h = jnp.concatenate(hs, axis=0)
    if rev:
        return h, jnp.where(row == tc - 1, carry, pltpu.roll(h, tc - 1, 0)), c
    return h, jnp.where(row == 0, carry, pltpu.roll(h, 1, 0)), c


def _one_minus_a_squared(log_a, a):
    return (1.0 + a * a) * jnp.tanh(-log_a)


def _gate_elem(pre_r, pre_i, xc, b_a, b_x, sp):
    r = _sigmoid(pre_r + b_a)
    i = _sigmoid(pre_i + b_x)
    log_a = (-LRU_C) * r * sp
    a = jnp.exp(log_a)
    m2 = _one_minus_a_squared(log_a, a)
    mult = jnp.where(m2 > 0.0, m2 * lax.rsqrt(m2), 0.0)
    return a, mult * (i * xc)


def _gate_elem_bwd(pre_r, pre_i, xc, b_a, b_x, sp, da, du):
    r = _sigmoid(pre_r + b_a)
    i = _sigmoid(pre_i + b_x)
    log_a = (-LRU_C) * r * sp
    a = jnp.exp(log_a)
    m2 = _one_minus_a_squared(log_a, a)
    inv_mult = lax.rsqrt(m2)
    g = du * (m2 * inv_mult)
    d_mult = du * (i * xc)
    d_log_a = (da - d_mult * a * inv_mult) * a
    d_pre_r = d_log_a * ((-LRU_C) * sp) * (r * (1.0 - r))
    d_pre_i = g * xc * (i * (1.0 - i))
    return d_pre_r, d_pre_i, g * i, _rsum(d_log_a * ((-LRU_C) * r))


def _blockdiag(xb16, w_ref_val, d):
    outs = []
    for n in range(LRU_BLOCKS):
        outs.append(jnp.dot(xb16[:, n * LRU_BW:(n + 1) * LRU_BW], w_ref_val[d * LRU_BLOCKS + n],
                            preferred_element_type=F32))
    return jnp.concatenate(outs, axis=1)


def lru_scan(name, xc, w_a, w_x, b_a, b_x, sp, direction, n_rows, t_lat):
    w = xc.shape[1]
    d = 0 if direction == 'f' else 1
    tc = _pick(math.gcd(t_lat, n_rows), (256, 128))
    nb, nbl = n_rows // tc, t_lat // tc
    order = _chunk_order(direction, nb, nbl)
    rev = direction == 'b'

    def body(x_ref, wa_ref, wx_ref, ba_ref, bx_ref, sp_ref, a_ref, h_ref, hp_ref, carry):
        @pl.when(pl.program_id(0) == 0)
        def _():
            carry[...] = jnp.zeros_like(carry)

        x = x_ref[...]
        xb16 = x.astype(BF16)
        a, u = _gate_elem(_blockdiag(xb16, wa_ref[...], d), _blockdiag(xb16, wx_ref[...], d), x,
                          ba_ref[d:d + 1, :], bx_ref[d:d + 1, :], sp_ref[d:d + 1, :])
        a_ref[...] = a
        h_ref[...], hp_ref[...], carry[...] = _chunk_scan(a, u, carry[...], rev)

    spec = pl.BlockSpec((tc, w), order)
    whole = lambda p: pl.BlockSpec(p.shape, lambda s, nd=p.ndim: (0,) * nd)
    return pl.pallas_call(
        body, name=name, grid=(nb,),
        out_shape=[jax.ShapeDtypeStruct((n_rows, w), F32)] * 3,
        in_specs=[spec] + [whole(p) for p in (w_a, w_x, b_a, b_x, sp)], out_specs=[spec] * 3,
        scratch_shapes=[pltpu.VMEM((1, w), F32)],
        compiler_params=_cparams(("arbitrary",)),
    )(xc, w_a, w_x, b_a, b_x, sp)


def lru_scan_bwd(name, xc, a, dh, hprev, dxc_in, w_a, w_x, b_a, b_x, sp, direction, n_rows, t_lat):
    w = xc.shape[1]
    d = 0 if direction == 'f' else 1
    tc = _pick(math.gcd(t_lat, n_rows), (256, 128))
    nb, nbl = n_rows // tc, t_lat // tc
    order = _adjoint_order(direction, nb, nbl)
    rev = direction == 'f'
    has_in = dxc_in is not None
    nt_dims, tn_dims = (((1,), (1,)), ((), ())), (((0,), (0,)), ((), ()))

    def dh_order(s):
        c, _ = order(s)
        return (jnp.minimum(c, nbl - 1), 0)

    def body(*refs):
        x_ref, a_ref, dh_ref, hp_ref = refs[:4]
        in_ref = refs[4] if has_in else None
        wa_ref, wx_ref, ba_ref, bx_ref, sp_ref = refs[4 + has_in:9 + has_in]
        dx_ref, dwa_ref, dwx_ref, dba_ref, dbx_ref, dsp_ref, carry = refs[9 + has_in:]
        s = pl.program_id(0)

        @pl.when(s == 0)
        def _():
            carry[...] = jnp.zeros_like(carry)
            for acc in (dwa_ref, dwx_ref, dba_ref, dbx_ref, dsp_ref):
                acc[...] = jnp.zeros_like(acc)

        chunk, _ = order(s)
        live = (chunk < nbl).astype(F32)
        av = a_ref[...]
        dv = dh_ref[...].astype(F32) * live
        _, c_next, carry[...] = _chunk_scan(av, av * dv, carry[...], rev)
        lam = dv + c_next

        x = x_ref[...]
        xb16 = x.astype(BF16)
        wa, wx = wa_ref[...], wx_ref[...]
        dpr, dpi, dxc, dsp_d = _gate_elem_bwd(_blockdiag(xb16, wa, d), _blockdiag(xb16, wx, d), x,
                                              ba_ref[d:d + 1, :], bx_ref[d:d + 1, :], sp_ref[d:d + 1, :],
                                              lam * hp_ref[...], lam)
        dpr16, dpi16 = dpr.astype(BF16), dpi.astype(BF16)
        back = []
        for n in range(LRU_BLOCKS):
            sl = slice(n * LRU_BW, (n + 1) * LRU_BW)
            back.append(lax.dot_general(dpr16[:, sl], wa[d * LRU_BLOCKS + n], nt_dims, preferred_element_type=F32)
                        + lax.dot_general(dpi16[:, sl], wx[d * LRU_BLOCKS + n], nt_dims, preferred_element_type=F32))
            dwa_ref[n] += lax.dot_general(xb16[:, sl], dpr16[:, sl], tn_dims, preferred_element_type=F32)
            dwx_ref[n] += lax.dot_general(xb16[:, sl], dpi16[:, sl], tn_dims, preferred_element_type=F32)
        dxc = dxc + jnp.concatenate(back, axis=1)
        dx_ref[...] = dxc + in_ref[...] if has_in else dxc
        dba_ref[...] += _rsum(dpr)
        dbx_ref[...] += _rsum(dpi)
        dsp_ref[...] += dsp_d

    spec = pl.BlockSpec((tc, w), order)
    whole = lambda shape: pl.BlockSpec(shape, lambda s, nd=len(shape): (0,) * nd)
    params = (w_a, w_x, b_a, b_x, sp)
    acc_shapes = [(LRU_BLOCKS, LRU_BW, LRU_BW)] * 2 + [(1, w)] * 3
    return pl.pallas_call(
        body, name=name, grid=(nb,),
        out_shape=[jax.ShapeDtypeStruct((n_rows, w), F32)] + [jax.ShapeDtypeStruct(sh, F32) for sh in acc_shapes],
        in_specs=[spec, spec, pl.BlockSpec((tc, w), dh_order), spec] + [spec] * has_in
        + [whole(p.shape) for p in params],
        out_specs=[spec] + [whole(sh) for sh in acc_shapes],
        scratch_shapes=[pltpu.VMEM((1, w), F32)],
        compiler_params=_cparams(("arbitrary",)),
    )(xc, a, dh, hprev, *([dxc_in] if has_in else []), *params)


def _rope_tables(t_lat, n_rows):
    rows = t_lat // GRID_W
    row_ids = jnp.repeat(jnp.arange(rows), GRID_W).astype(F32)
    col_ids = jnp.tile(jnp.arange(GRID_W), rows).astype(F32)
    axis_dim = QK_ROPE // 2
    inv = 1.0 / (ROPE_BASE ** (jnp.arange(0, axis_dim, 2, dtype=F32) / axis_dim))
    ang = jnp.concatenate([row_ids[:, None] * inv, col_ids[:, None] * inv], axis=-1)
    cos, sin = jnp.cos(ang), jnp.sin(ang)
    half = QK_ROPE // 2
    ones, zeros = jnp.ones((t_lat, QK_NOPE), F32), jnp.zeros((t_lat, QK_NOPE), F32)
    pad1, pad0 = jnp.ones((t_lat, HEAD_PAD - QK_DIM), F32), jnp.zeros((t_lat, HEAD_PAD - QK_DIM), F32)
    zh = jnp.zeros((t_lat, half), F32)
    c_tab = jnp.concatenate([ones, cos, cos, pad1], axis=1)
    s1 = jnp.concatenate([zeros, -sin, zh, pad0], axis=1)
    s2 = jnp.concatenate([zeros, zh, sin, pad0], axis=1)
    n_ctx = n_rows - t_lat
    c_tab = jnp.concatenate([c_tab, jnp.ones((n_ctx, HEAD_PAD), F32)], axis=0)
    s1 = jnp.concatenate([s1, jnp.zeros((n_ctx, HEAD_PAD), F32)], axis=0)
    s2 = jnp.concatenate([s2, jnp.zeros((n_ctx, HEAD_PAD), F32)], axis=0)
    return c_tab, s1, s2


def _rope(x, c, s1, s2):
    half = QK_ROPE // 2
    return x * c + pltpu.roll(x, HEAD_PAD - half, 1) * s1 + pltpu.roll(x, half, 1) * s2


def _rope_t(dy, c, s1, s2):
    half = QK_ROPE // 2
    return dy * c + pltpu.roll(dy * s1, half, 1) + pltpu.roll(dy * s2, HEAD_PAD - half, 1)


def _heads(x):
    return [x[:, h * HEAD_PAD:(h + 1) * HEAD_PAD] for h in range(N_HEADS)]


Q_SCALE = QK_DIM ** -0.5 * math.log2(math.e)
ATTN_BWD_HEADS = 2
ATTN_BWD_KEY_CHUNKS = 3
ATTN_FWD_HEADS = 4

def attn_fwd(q, k, v, t_lat, n_rows, tq):
    def body(q_ref, k_ref, v_ref, o_ref, lse_ref):
        for hh in range(ATTN_FWD_HEADS):
            lanes = slice(hh * HEAD_PAD, (hh + 1) * HEAD_PAD)
            s = lax.dot_general(q_ref[:, lanes], k_ref[:, lanes], (((1,), (1,)), ((), ())),
                                preferred_element_type=F32)
            m = jnp.max(s, axis=-1, keepdims=True)
            p = jnp.exp2(s - m)
            l = jnp.sum(p, axis=-1, keepdims=True)
            o = jnp.dot(p.astype(BF16), v_ref[:, lanes], preferred_element_type=F32) / l
            o_ref[:, lanes] = o.astype(o_ref.dtype)
            lse_ref[:, lanes] = jnp.broadcast_to(m + jnp.log2(l), (tq, HEAD_PAD))

    width = ATTN_FWD_HEADS * HEAD_PAD
    qspec = pl.BlockSpec((tq, width), lambda h, i: (i, h))
    kspec = pl.BlockSpec((n_rows, width), lambda h, i: (0, h))
    return pl.pallas_call(
        body, name="attn_fwd", grid=(N_HEADS // ATTN_FWD_HEADS, t_lat // tq),
        out_shape=[jax.ShapeDtypeStruct((t_lat, N_HEADS * HEAD_PAD), BF16),
                   jax.ShapeDtypeStruct((t_lat, N_HEADS * HEAD_PAD), F32)],
        in_specs=[qspec, kspec, kspec], out_specs=[qspec, qspec],
        compiler_params=_cparams(("parallel", "arbitrary")),
    )(q, k, v)


def attn_bwd(q, k, v, o, do, lse, t_lat, n_rows, tq):
    scale = QK_DIM ** -0.5
    nq = t_lat // tq
    nt = (((1,), (1,)), ((), ()))
    tn = (((0,), (0,)), ((), ()))

    def body(q_ref, k_ref, v_ref, o_ref, do_ref, lse_ref, dq_ref, dk_ref, dv_ref):
        @pl.when(pl.program_id(1) == 0)
        def _():
            dk_ref[...] = jnp.zeros_like(dk_ref)
            dv_ref[...] = jnp.zeros_like(dv_ref)

        for hh in range(ATTN_BWD_HEADS):
            lanes = slice(hh * HEAD_PAD, (hh + 1) * HEAD_PAD)
            qv, dov = q_ref[:, lanes], do_ref[:, lanes]
            lse_col = lse_ref[:, hh * HEAD_PAD:hh * HEAD_PAD + 1]
            delta = jnp.sum(dov.astype(F32) * o_ref[:, lanes].astype(F32), axis=-1, keepdims=True)
            dq = jnp.zeros((tq, HEAD_PAD), F32)
            for lo, hi in key_chunks:
                kv, vv = k_ref[lo:hi, lanes], v_ref[lo:hi, lanes]
                s = lax.dot_general(qv, kv, nt, preferred_element_type=F32)
                p = jnp.exp2(s - lse_col)
                dv_ref[lo:hi, lanes] += lax.dot_general(p.astype(BF16), dov, tn, preferred_element_type=F32)
                dp = lax.dot_general(dov, vv, nt, preferred_element_type=F32)
                ds = (p * (dp - delta)).astype(BF16)
                dq = dq + jnp.dot(ds, kv, preferred_element_type=F32)
                dk_ref[lo:hi, lanes] += lax.dot_general(ds, qv, tn, preferred_element_type=F32)
            dq_ref[:, lanes] = (dq * scale).astype(dq_ref.dtype)

        @pl.when(pl.program_id(1) == nq - 1)
        def _():
            dk_ref[...] = dk_ref[...] * (scale / Q_SCALE)

    n_key_tiles = n_rows // MXU_DIM
    bounds = [round(c * n_key_tiles / ATTN_BWD_KEY_CHUNKS) * MXU_DIM for c in range(ATTN_BWD_KEY_CHUNKS)] + [n_rows]
    key_chunks = [(lo, hi) for lo, hi in zip(bounds[:-1], bounds[1:]) if hi > lo]
    width = ATTN_BWD_HEADS * HEAD_PAD
    qspec = pl.BlockSpec((tq, width), lambda h, i: (i, h))
    kspec = pl.BlockSpec((n_rows, width), lambda h, i: (0, h))
    return pl.pallas_call(
        body, name="attn_bwd", grid=(N_HEADS // ATTN_BWD_HEADS, t_lat // tq),
        out_shape=[jax.ShapeDtypeStruct((t_lat, N_HEADS * HEAD_PAD), BF16),
                   jax.ShapeDtypeStruct((n_rows, N_HEADS * HEAD_PAD), F32),
                   jax.ShapeDtypeStruct((n_rows, N_HEADS * HEAD_PAD), F32)],
        in_specs=[qspec, kspec, kspec, qspec, qspec, qspec], out_specs=[qspec, kspec, kspec],
        compiler_params=_cparams(("parallel", "arbitrary")),
    )(q, k, v, o, do, lse)


def adamw(name, w, g, m, v):
    r, ccols = w.shape
    if r % 8 == 0:
        tr, tcol = _best_div(r, 8, max(8, 262144 // ccols)), ccols
    else:
        tr, tcol = r, _pick(ccols, (256, 128))
    c1 = 1.0 - ADAM_B1 ** ADAM_STEP
    c2 = 1.0 - ADAM_B2 ** ADAM_STEP

    def body(w_ref, g_ref, m_ref, v_ref, d_ref, nm_ref, nv_ref):
        gv = g_ref[...]
        nm = ADAM_B1 * m_ref[...] + (1.0 - ADAM_B1) * gv
        nv = ADAM_B2 * v_ref[...] + (1.0 - ADAM_B2) * (gv * gv)
        d_ref[...] = -ADAM_LR * ((nm / c1) / (jnp.sqrt(nv / c2) + ADAM_EPS) + ADAM_WD * w_ref[...])
        nm_ref[...] = nm
        nv_ref[...] = nv

    spec = pl.BlockSpec((tr, tcol), lambda i, j: (i, j))
    return pl.pallas_call(
        body, name=name, grid=(r // tr, ccols // tcol),
        out_shape=[jax.ShapeDtypeStruct((r, ccols), F32)] * 3,
        in_specs=[spec] * 4, out_specs=[spec] * 3,
        compiler_params=_cparams(("parallel", "parallel")),
    )(w, g, m, v)


def adamw_many(name, ws, gs, ms, vs):
    n = len(ws)
    c1 = 1.0 - ADAM_B1 ** ADAM_STEP
    c2 = 1.0 - ADAM_B2 ** ADAM_STEP

    def body(*refs):
        for i in range(n):
            w_ref, g_ref, m_ref, v_ref = (refs[k * n + i] for k in range(4))
            d_ref, nm_ref, nv_ref = (refs[(4 + k) * n + i] for k in range(3))
            gv = g_ref[...]
            nm = ADAM_B1 * m_ref[...] + (1.0 - ADAM_B1) * gv
            nv = ADAM_B2 * v_ref[...] + (1.0 - ADAM_B2) * (gv * gv)
            d_ref[...] = -ADAM_LR * ((nm / c1) / (jnp.sqrt(nv / c2) + ADAM_EPS) + ADAM_WD * w_ref[...])
            nm_ref[...] = nm
            nv_ref[...] = nv

    vmem = pl.BlockSpec(memory_space=pltpu.VMEM)
    res = pl.pallas_call(
        body, name=name,
        out_shape=[jax.ShapeDtypeStruct(w.shape, F32) for w in ws] * 3,
        in_specs=[vmem] * (4 * n), out_specs=[vmem] * (3 * n),
        compiler_params=_cparams(),
    )(*ws, *gs, *ms, *vs)
    return [tuple(res[k * n + i] for k in range(3)) for i in range(n)]


def _flat(parts, dtype, row_mult):
    v = jnp.concatenate([p.reshape(-1).astype(dtype) for p in parts])
    quantum = row_mult * FLAT_C
    total = -(-v.shape[0] // quantum) * quantum
    return jnp.pad(v, (0, total - v.shape[0])).reshape(total // FLAT_C, FLAT_C)


def _gathered_to_full(name, g):
    k = g.shape[1]
    return jnp.transpose(g, (1, 0, 2)).reshape(k, N_DEV * g.shape[2])


def _shard_to_rb(name, w):
    return w if name in ROW_SHARDED else w.T


def _rb_to_shard(name, g):
    return g if name in ROW_SHARDED else g.T


def _rb_from_gathered(name, g):
    cols = g.shape[2]
    if name == 'w_in':
        z = lambda k: jnp.zeros((k, cols), g.dtype)
        full = g.reshape(N_DEV * g.shape[1], cols)
        return jnp.concatenate([full[:Z_KR], z(QK_NOPE), full[Z_KR:Z_KR + QK_ROPE], z(HEAD_PAD - QK_DIM),
                                full[Z_KR + QK_ROPE:]], axis=0)
    if name == 'w_uq':
        return jnp.pad(g, ((0, 0), (0, HEAD_PAD - QK_DIM), (0, 0))).reshape(N_HEADS * HEAD_PAD, cols)
    if name == 'w_ukv':
        pad = lambda t: jnp.pad(t, ((0, 0), (0, HEAD_PAD - t.shape[1]), (0, 0))).reshape(N_HEADS * HEAD_PAD, cols)
        return jnp.concatenate([pad(g[:, :QK_NOPE]), pad(g[:, QK_NOPE:])], axis=0)
    if name == 'w_o_attn':
        full = g.reshape(D, N_HEADS, V_HEAD)
        return jnp.pad(full, ((0, 0), (0, 0), (0, HEAD_PAD - V_HEAD))).reshape(D, N_HEADS * HEAD_PAD)
    return g.reshape(N_DEV * g.shape[1], cols)


def _chunks_from_rb_grad(name, g):
    cols = g.shape[1]
    if name == 'w_in':
        full = jnp.concatenate([g[:Z_KR], g[Z_KR + QK_NOPE:Z_KR + QK_DIM], g[Z_XB:]], axis=0)
        return full.reshape(N_DEV, -1, cols)
    if name == 'w_uq':
        return g.reshape(N_HEADS, HEAD_PAD, cols)[:, :QK_DIM]
    if name == 'w_ukv':
        half = N_HEADS * HEAD_PAD
        gk = g[:half].reshape(N_HEADS, HEAD_PAD, cols)[:, :QK_NOPE]
        gv = g[half:].reshape(N_HEADS, HEAD_PAD, cols)[:, :V_HEAD]
        return jnp.concatenate([gk, gv], axis=1)
    if name == 'w_o_attn':
        full = g.reshape(D, N_HEADS, HEAD_PAD)[:, :, :V_HEAD].reshape(D, N_HEADS * V_HEAD)
        return full.reshape(N_DEV, D // N_DEV, N_HEADS * V_HEAD)
    return g.reshape(N_DEV, -1, cols)


def local_step(x, ctx, target, mod_l, mod_c, wt, on_grad=None, arrive=None):
    t_lat, n_ctx = x.shape[0], ctx.shape[0]
    n = t_lat + n_ctx
    tm = _pick(math.gcd(t_lat, n), (256, 128))
    tq_fwd = _pick(t_lat, (256, 128))
    tq_bwd = _pick(t_lat, (512, 256, 128))
    row = lambda v: v.reshape(1, -1).astype(F32)
    two = lambda a, b: jnp.stack([a, b]).astype(F32)
    sh1_l, sc1_l, g1_l, sh2_l, sc2_l, g2_l = jnp.split(mod_l, 6)
    sh1_c, sc1_c = jnp.split(mod_c, 6)[:2]
    sc1, sh1 = two(sc1_l, sc1_c), two(sh1_l, sh1_c)
    g1, g2, sc2, sh2 = row(g1_l), row(g2_l), row(sc2_l), row(sh2_l)
    norm1_g, norm2_g, final_g = row(wt['norm1_g']), row(wt['norm2_g']), row(wt['final_g'])
    q_g, kv_g, b_gate = row(wt['q_norm_g']), row(wt['kv_norm_g']), row(wt['b_gate'])
    wt = dict(wt)
    pending = []

    def sent():
        tokens = list(pending)
        pending.clear()
        return tokens

    def need(names, after):
        if arrive is not None:
            got = arrive(names, after)
            if '_token' in got:
                pending.append(got.pop('_token'))
            wt.update(got)
        return [wt[n] for n in names]
    lru_w_a = wt['lru_w_a'].reshape(2 * LRU_BLOCKS, LRU_BW, LRU_BW).astype(BF16)
    lru_w_x = wt['lru_w_x'].reshape(2 * LRU_BLOCKS, LRU_BW, LRU_BW).astype(BF16)
    b_a, b_x, lam = wt['lru_b_a'], wt['lru_b_x'], wt['lru_lambda']
    sp = jnp.logaddexp(-lam, 0.0)
    c_tab, s1_tab, s2_tab = _rope_tables(t_lat, n)
    rw = functools.partial(rowwise, n_rows=n, t_lat=t_lat, tm=tm)
    rw_lat = functools.partial(rowwise, n_rows=t_lat, t_lat=t_lat, tm=_pick(t_lat, (512, 256, 128)))

    stream = [(x, 0, D), (ctx, 0, D, 'ctx')]

    def f_norm1(is_ctx, rows, params):
        (xl, xc_), (g, sc, sh) = rows, params
        return [_norm_mod(jnp.where(is_ctx, xc_, xl), g, _sel(is_ctx, sc), _sel(is_ctx, sh))], []

    (h,), _ = rw("norm1", f_norm1, stream, [norm1_g, sc1, sh1], [(D, BF16)], [])
    (w_in_t,) = need(('w_in',), h)
    z = matmul("w_in", h, w_in_t, 'nt', BF16, after=sent())
    w_uq_t, w_ukv_t, w_o_lru = need(('w_uq', 'w_ukv', 'w_o_lru'), z)

    def f_qkv_norm(is_ctx, rows, params):
        (ql, kvl), (gq, gkv) = rows, params
        return [_rms(ql, gq), _rms(kvl, gkv)], []

    (qn, kvn), _ = rw("qkv_norm", f_qkv_norm, [(z, Z_Q, Q_RANK), (z, Z_KV, KV_RANK)], [q_g, kv_g],
                      [(Q_RANK, BF16), (KV_RANK, BF16)], [])
    qp = matmul("w_uq", qn, w_uq_t, 'nt', BF16)
    kvp = matmul("w_ukv", kvn, w_ukv_t, 'nt', BF16)

    def f_rope(is_ctx, rows, params):
        qv, kk, vv, kr, c, s1, s2 = rows
        krr = _rope(kr, c, s1, s2)
        qo = jnp.concatenate([_rope(qh, c, s1, s2) for qh in _heads(qv)], axis=1) * Q_SCALE
        ko = jnp.concatenate([kh + krr for kh in _heads(kk)], axis=1)
        return [qo, ko, vv], []

    hp = N_HEADS * HEAD_PAD
    (qr, kr_, vr), _ = rw("rope", f_rope,
                          [(qp, 0, hp), (kvp, 0, hp), (kvp, hp, hp), (z, Z_KR, HEAD_PAD), (c_tab, 0, HEAD_PAD),
                           (s1_tab, 0, HEAD_PAD), (s2_tab, 0, HEAD_PAD)], [], [(hp, BF16)] * 3, [])
    attn, lse = attn_fwd(qr, kr_, vr, t_lat, n, tq_fwd)

    xc = conv_fwd("lru_conv", z, Z_XB, LRU_W, wt['lru_conv_w'], row(wt['lru_conv_b']), 2, n, t_lat, F32)
    a_f, h_f, hp_f = lru_scan("lru_scan_f", xc, lru_w_a, lru_w_x, b_a, b_x, sp, 'f', n, t_lat)
    a_b, h_b, hp_b = lru_scan("lru_scan_b", xc, lru_w_a, lru_w_x, b_a, b_x, sp, 'b', n, t_lat)

    def f_lru_out(is_ctx, rows, params):
        hf, hb, yb = rows
        return [(hf + hb) * _gelu(yb)], []

    (ybin,), _ = rw_lat("lru_out", f_lru_out, [(h_f, 0, LRU_W), (h_b, 0, LRU_W), (z, Z_YB, LRU_W)], [],
                        [(LRU_W, BF16)], [])
    w_o_attn_t, w_out, w_up_t, w_down = need(('w_o_attn', 'w_out', 'w_up', 'w_down'), attn)
    y_a = matmul("w_o_attn", attn, w_o_attn_t, 'nt', BF16)
    y_b = matmul("w_o_lru", ybin, w_o_lru, 'nn', BF16)

    def _merge(ya, yb, gl, bg):
        gates = _sigmoid(gl + bg)
        return gates[:, :D] * ya + gates[:, D:] * yb

    def f_merge(is_ctx, rows, params):
        (ya, yb, gl), (bg,) = rows, params
        return [_merge(ya, yb, gl, bg)], []

    (mrg,), _ = rw_lat("merge", f_merge, [(y_a, 0, D), (y_b, 0, D), (z, Z_GL, 2 * D)], [b_gate], [(D, BF16)], [])
    o = matmul("w_out", mrg, w_out, 'nn', BF16)

    def _res_norm2(xv, ov, g1v, g, sc, sh):
        x1 = xv + g1v * ov
        return x1, _norm_mod(x1, g, sc, sh)

    def f_norm2(is_ctx, rows, params):
        (xv, ov), (g1v, g, sc, sh) = rows, params
        x1, h2v = _res_norm2(xv, ov, g1v, g, sc, sh)
        return [x1, h2v], []

    (x1, h2), _ = rw_lat("norm2", f_norm2, [(x, 0, D), (o, 0, D)], [g1, norm2_g, sc2, sh2], [(D, F32), (D, BF16)], [])
    u = matmul("w_up", h2, w_up_t, 'nt', BF16)
    f = ffn_mix_fwd(u, wt['ffn_conv_w'], row(wt['ffn_conv_b']), t_lat)
    dn = matmul("w_down", f, w_down, 'nn', BF16)

    def _tile_loss(x1v, dv, g2v, fg, tgt):
        y = _rms(x1v + g2v * dv, fg)
        e = y - tgt
        return 0.5 * jnp.sum(jnp.mean(e * e, axis=-1, keepdims=True), axis=0, keepdims=True)

    def f_final(is_ctx, rows, params):
        (x1v, dv, tgt), (g2v, fg) = rows, params
        lv, vjp = jax.vjp(lambda a, b, c, d: _tile_loss(a, b, c, d, tgt), x1v, dv, g2v, fg)
        dx2, dd, dg2, dfg = vjp(jnp.ones((1, 1), F32))
        return [dx2, dd], [dg2, dfg, jnp.broadcast_to(lv, (1, 128))]

    (dx2, dd), (dg2, dfinal_g, loss_v) = rw_lat("final", f_final, [(x1, 0, D), (dn, 0, D), (target, 0, D)],
                                                [g2, final_g], [(D, F32), (D, BF16)], [(1, D), (1, D), (1, 128)])
    loss = loss_v[0, 0]

    grads = {'final_g': dfinal_g}

    def put(name, g):
        grads[name] = g
        if on_grad is not None:
            pending.append(on_grad(name, g))
    df = matmul("d_f", dd, w_down, 'nt', BF16)
    put('w_down', matmul("g_w_down", f, dd, 'tn', BF16))

    du, grads['ffn_conv_w'], grads['ffn_conv_b'] = ffn_mix_bwd(u, df, wt['ffn_conv_w'], row(wt['ffn_conv_b']),
                                                               t_lat)
    dh2 = matmul("d_h2", du, w_up_t, 'nn', BF16, after=sent())
    put('w_up', matmul("g_w_up", du, h2, 'tn', BF16))

    def b_norm2(is_ctx, rows, params):
        (xv, ov, dh2v, dx2v), (g1v, g, sc, sh) = rows, params
        _, vjp = jax.vjp(_res_norm2, xv, ov, g1v, g, sc, sh)
        dx, do, dg1v, dg, dsc, dsh = vjp((dx2v, dh2v))
        return [dx, do], [dg1v, dg, dsc, dsh]

    (dx_res, do), (dg1, dnorm2_g, dsc2, dsh2) = rw_lat(
        "norm2_bwd", b_norm2, [(x, 0, D), (o, 0, D), (dh2, 0, D), (dx2, 0, D)], [g1, norm2_g, sc2, sh2],
        [(D, F32), (D, BF16)], [(1, D)] * 4)
    grads['norm2_g'] = dnorm2_g
    dmrg = matmul("d_merge", do, w_out, 'nt', BF16, after=sent())
    put('w_out', matmul("g_w_out", mrg, do, 'tn', BF16))

    def b_merge(is_ctx, rows, params):
        (ya, yb, gl, dm), (bg,) = rows, params
        _, vjp = jax.vjp(_merge, ya, yb, gl, bg)
        dya, dyb, dgl, dbg = vjp(dm)
        return [dya, dyb, dgl], [dbg]

    (dy_a, dy_b, dgl), (grads['b_gate'],) = rw_lat(
        "merge_bwd", b_merge, [(y_a, 0, D), (y_b, 0, D), (z, Z_GL, 2 * D), (dmrg, 0, D)], [b_gate],
        [(D, BF16), (D, BF16), (2 * D, BF16)], [(1, 2 * D)])
    dattn = matmul("d_attn", dy_a, w_o_attn_t, 'nn', BF16, after=sent())
    put('w_o_attn', matmul("g_w_o_attn", dy_a, attn, 'tn', BF16))
    dybin = matmul("d_lru_out", dy_b, w_o_lru, 'nt', BF16, after=sent())
    put('w_o_lru', matmul("g_w_o_lru", ybin, dy_b, 'tn', BF16))

    def b_lru_out(is_ctx, rows, params):
        hf, hb, yb, dyv = rows
        _, vjp = jax.vjp(lambda s, y: s * _gelu(y), hf + hb, yb)
        dh, dyb = vjp(dyv)
        return [dh, dyb], []

    (dh_lru, dyb), _ = rw_lat("lru_out_bwd", b_lru_out,
                              [(h_f, 0, LRU_W), (h_b, 0, LRU_W), (z, Z_YB, LRU_W), (dybin, 0, LRU_W)], [],
                              [(LRU_W, F32), (LRU_W, BF16)], [])
    gate_params = (lru_w_a, lru_w_x, b_a, b_x, sp)
    dxc_f, *sums_f = lru_scan_bwd("lru_scan_f_bwd", xc, a_f, dh_lru, hp_f, None, *gate_params, 'f', n, t_lat)
    dxc, *sums_b = lru_scan_bwd("lru_scan_b_bwd", xc, a_b, dh_lru, hp_b, dxc_f, *gate_params, 'b', n, t_lat)
    dw_a, dw_x, db_a, db_x, dsp = (jnp.concatenate([f_, b_], axis=0) for f_, b_ in zip(sums_f, sums_b))
    put('lru_w_a', dw_a.reshape(2 * LRU_BLOCKS * LRU_BW, LRU_BW).astype(BF16))
    put('lru_w_x', dw_x.reshape(2 * LRU_BLOCKS * LRU_BW, LRU_BW).astype(BF16))
    grads['lru_b_a'], grads['lru_b_x'] = db_a, db_x
    grads['lru_lambda'] = -dsp * _sigmoid(-lam)
    dxb, grads['lru_conv_w'], grads['lru_conv_b'] = conv_bwd("lru_conv_bwd", dxc, z, Z_XB, LRU_W, wt['lru_conv_w'],
                                                             2, n, t_lat)

    dq, dk, dv = attn_bwd(qr, kr_, vr, attn, dattn, lse, t_lat, n, tq_bwd)

    def b_rope(is_ctx, rows, params):
        dqv, dkv, dvv, c, s1, s2 = rows
        live = jnp.where(is_ctx, 0.0, 1.0)
        dqo = jnp.concatenate([_rope_t(dqh, c, s1, s2) for dqh in _heads(dqv)], axis=1) * live
        dkh = _heads(dkv)
        dkr = dkh[0]
        for t in dkh[1:]:
            dkr = dkr + t
        lanes = lax.broadcasted_iota(jnp.int32, dkr.shape, 1)
        dkr = jnp.where((lanes >= QK_NOPE) & (lanes < QK_DIM), _rope_t(dkr, c, s1, s2), 0.0)
        return [dqo, jnp.concatenate([dkv, dvv], axis=1), dkr], []

    (dqp, dkvp, dkr), _ = rw("rope_bwd", b_rope,
                             [(dq, 0, hp), (dk, 0, hp), (dv, 0, hp), (c_tab, 0, HEAD_PAD), (s1_tab, 0, HEAD_PAD),
                              (s2_tab, 0, HEAD_PAD)], [], [(hp, BF16), (2 * hp, BF16), (HEAD_PAD, BF16)], [])
    dqn = matmul("d_qn", dqp, w_uq_t, 'nn', BF16, after=sent())
    put('w_uq', matmul("g_w_uq", dqp, qn, 'tn', BF16))
    dkvn = matmul("d_kvn", dkvp, w_ukv_t, 'nn', BF16, after=sent())
    put('w_ukv', matmul("g_w_ukv", dkvp, kvn, 'tn', BF16))

    def b_qkv_norm(is_ctx, rows, params):
        (ql, kvl, dqv, dkvv), (gq, gkv) = rows, params
        _, vjp_q = jax.vjp(_rms, ql, gq)
        _, vjp_kv = jax.vjp(_rms, kvl, gkv)
        dql, dgq = vjp_q(dqv)
        dkvl, dgkv = vjp_kv(dkvv)
        return [dql, dkvl], [dgq, dgkv]

    (dq_lat, dkv_lat), (grads['q_norm_g'], grads['kv_norm_g']) = rw(
        "qkv_norm_bwd", b_qkv_norm, [(z, Z_Q, Q_RANK), (z, Z_KV, KV_RANK), (dqn, 0, Q_RANK), (dkvn, 0, KV_RANK)],
        [q_g, kv_g], [(Q_RANK, BF16), (KV_RANK, BF16)], [(1, Q_RANK), (1, KV_RANK)])
    pad_ctx = lambda t: jnp.pad(t, ((0, n_ctx), (0, 0)))
    dz = jnp.concatenate([dq_lat, dkv_lat, dkr, dxb, pad_ctx(dyb), pad_ctx(dgl)], axis=1)
    put('w_in', matmul("g_w_in", dz, h, 'tn', BF16))
    dh = matmul("d_h", dz, w_in_t, 'nn', BF16, after=sent())

    def b_norm1(is_ctx, rows, params):
        (xl, xc_, dhv, dxr), (g, sc, sh) = rows, params
        scv, shv = _sel(is_ctx, sc), _sel(is_ctx, sh)
        _, vjp = jax.vjp(_norm_mod, jnp.where(is_ctx, xc_, xl), g, scv, shv)
        dx, dg, dsc, dsh = vjp(dhv)
        return [dx + dxr], [dg, _seg_acc(is_ctx, dsc), _seg_acc(is_ctx, dsh)]

    (grad_x,), (grads['norm1_g'], dsc1, dsh1) = rw("norm1_bwd", b_norm1, stream + [(dh, 0, D), (dx_res, 0, D)],
                                                   [norm1_g, sc1, sh1], [(D, F32, 'lat')],
                                                   [(1, D), (2, D), (2, D)])
    zero = jnp.zeros((D,), F32)
    dmod_l = jnp.concatenate([dsh1[0], dsc1[0], dg1[0], dsh2[0], dsc2[0], dg2[0]])
    dmod_c = jnp.concatenate([dsh1[1], dsc1[1], zero, zero, zero, zero])
    return loss, grad_x, grads, dmod_l, dmod_c


def kernel(x, c, ctx, c_ctx, w_mod, b_mod, norm1_g, w_in, b_gate, q_norm_g, kv_norm_g, w_uq, w_ukv, w_o_attn, lru_conv_w, lru_conv_b, lru_w_a, lru_b_a, lru_w_x, lru_b_x, lru_lambda, w_o_lru, w_out, norm2_g, w_up, ffn_conv_w, ffn_conv_b, w_down, final_g, loss_target, m_c_ctx, m_w_mod, m_b_mod, m_norm1_g, m_w_in, m_b_gate, m_q_norm_g, m_kv_norm_g, m_w_uq, m_w_ukv, m_w_o_attn, m_lru_conv_w, m_lru_conv_b, m_lru_w_a, m_lru_b_a, m_lru_w_x, m_lru_b_x, m_lru_lambda, m_w_o_lru, m_w_out, m_norm2_g, m_w_up, m_ffn_conv_w, m_ffn_conv_b, m_w_down, m_final_g, v_c_ctx, v_w_mod, v_b_mod, v_norm1_g, v_w_in, v_b_gate, v_q_norm_g, v_kv_norm_g, v_w_uq, v_w_ukv, v_w_o_attn, v_lru_conv_w, v_lru_conv_b, v_lru_w_a, v_lru_b_a, v_lru_w_x, v_lru_b_x, v_lru_lambda, v_w_o_lru, v_w_out, v_norm2_g, v_w_up, v_ffn_conv_w, v_ffn_conv_b, v_w_down, v_final_g):
    given = dict(locals())
    strip = lambda name, a: a if name in ('c_ctx', 'final_g') else a[0]
    wsh = {n: strip(n, given[n]) for n in WEIGHTS}
    msh = {n: strip(n, given['m_' + n]) for n in WEIGHTS}
    vsh = {n: strip(n, given['v_' + n]) for n in WEIGHTS}
    me = _my_index()

    small = _flat([c[0]] + [wsh[n] for n in SMALL_F32], F32, 8)
    small_all = all_gather("gather_small", small).reshape(N_DEV, -1)
    c_all = small_all[:, :D]
    full, at = {}, D
    for n in SMALL_F32:
        cnt = math.prod(wsh[n].shape)
        full[n] = _gathered_to_full(n, small_all[:, at:at + cnt].reshape((N_DEV,) + wsh[n].shape))
        at += cnt

    cond = jnp.concatenate([c_all, c_ctx[None], jnp.zeros((7, D), F32)], axis=0)
    sil = cond * jax.nn.sigmoid(cond)
    mod_cols = matmul("mod_proj", sil, wsh['w_mod'], 'nn', F32)
    mod_all = all_gather("gather_mod", mod_cols)
    mod_all = jnp.transpose(mod_all, (1, 0, 2)).reshape(16, 6 * D) + b_mod[0][None]
    mod_l = lax.dynamic_index_in_dim(mod_all, me, axis=0, keepdims=False)
    mod_c = mod_all[N_DEV]

    rb_shards = {n: _shard_to_rb(n, wsh[n]).astype(BF16) for n in BIG_BF16}
    (w_in_blocks,) = all_gather_multi("gather_w_in", [rb_shards['w_in']])
    later = [n for n in BIG_BF16 if n != 'w_in']
    weights_started, weights_sent = exchange_start("weights_send", 'gather', [rb_shards[n] for n in later],
                                                   after=[w_in_blocks, mod_all])
    for n in REPLICATED:
        if n not in ('c_ctx', 'b_mod'):
            full[n] = wsh[n]

    def arrive(names, after):
        if names == ('w_in',):
            return {'w_in': _rb_from_gathered('w_in', w_in_blocks), '_token': weights_sent}
        picked = [later.index(n) for n in names]
        lands = exchange_wait("weights_wait_" + names[0], 'gather',
                              tuple([part[i] for i in picked] for part in weights_started), after)
        return {n: _rb_from_gathered(n, lax.dynamic_update_slice_in_dim(land, rb_shards[n][None], me, axis=0))
                for n, land in zip(names, lands)}

    in_flight = {}

    def on_grad(n, g):
        chunks = _chunks_from_rb_grad(n, g)
        own = lax.dynamic_index_in_dim(chunks, me, axis=0, keepdims=True)
        started, token = exchange_start("grad_send_" + n, 'scatter', [chunks])
        in_flight[n] = (own, started)
        return token

    loss, grad_x, grads, dmod_l, dmod_c = local_step(x[0], ctx[0], loss_target[0], mod_l, mod_c, full, on_grad,
                                                     arrive)
    dmod = jnp.stack([dmod_l, dmod_c]).reshape(2 * 6 * D // FLAT_C, FLAT_C)
    dm = all_gather("gather_dmod", dmod).reshape(N_DEV, 2, 6 * D)
    dmod_c_tot = dm[0, 1]
    for p in range(1, N_DEV):
        dmod_c_tot = dmod_c_tot + dm[p, 1]
    dm16 = jnp.concatenate([dm[:, 0], dmod_c_tot[None], jnp.zeros((7, 6 * D), F32)], axis=0)
    ncol = 6 * D // N_DEV
    dm16_cols = lax.dynamic_slice_in_dim(dm16.reshape(16, N_DEV, ncol), me, 1, axis=1)[:, 0]
    grad_w_mod = matmul("g_w_mod", sil, dm16_cols, 'tn', F32)
    dsil = matmul("d_cond", dm16_cols, wsh['w_mod'], 'nt', F32)
    sg = jax.nn.sigmoid(c_ctx)
    grads['c_ctx'] = dsil[N_DEV] * (sg * (1.0 + c_ctx * (1.0 - sg)))
    grads['b_mod'] = dmod_l + dmod_c

    g_final = {'w_mod': grad_w_mod}
    reduced, stepped = {}, {}
    for n in BIG_BF16 + ['lru_w_a', 'lru_w_x']:
        own, started = in_flight[n]
        (land,) = exchange_wait("grad_wait_" + n, 'scatter', started, dm)
        if n in ROW_SHARDED:
            g_final[n], *stepped[n] = reduce_slots("step_" + n, land, own, (wsh[n], msh[n], vsh[n]))
        elif n in COL_SHARDED and wsh[n].shape[1] % 128:
            g_t, *outs = reduce_slots("step_" + n, land, own, (wsh[n].T, msh[n].T, vsh[n].T))
            g_final[n], stepped[n] = g_t.T, [o.T for o in outs]
        else:
            reduced[n] = reduce_slots("sum_" + n, land, own)
            if n in BIG_BF16:
                g_final[n] = _rb_to_shard(n, reduced[n])

    small_names = SMALL_F32 + [n for n in REPLICATED if n not in ('lru_w_a', 'lru_w_x')]
    partials = _flat([grads[n] for n in small_names] + [loss], F32, 8)
    parts_all, a_all, x_all = all_gather_multi("gather_small_grads", [partials, reduced['lru_w_a'], reduced['lru_w_x']])
    small_sum = sum_slots("sum_small", parts_all).reshape(-1)
    g_final['lru_w_a'], g_final['lru_w_x'] = a_all.reshape(wsh['lru_w_a'].shape), x_all.reshape(wsh['lru_w_x'].shape)
    at = 0
    for n in small_names:
        cnt = math.prod(full[n].shape) if n in SMALL_F32 else math.prod(wsh[n].shape)
        g = small_sum[at:at + cnt]
        if n in SMALL_F32:
            k = full[n].shape[0]
            g = lax.dynamic_index_in_dim(g.reshape(k, N_DEV, -1), me, axis=1, keepdims=False)
        g_final[n] = g.reshape(wsh[n].shape)
        at += cnt
    loss = small_sum[at]

    for n in ['w_mod'] + BIG_BF16:
        if n not in stepped:
            stepped[n] = adamw("adamw_" + n, wsh[n], g_final[n], msh[n], vsh[n])
    rest = [n for n in WEIGHTS if n not in stepped]
    as2d = lambda a: a.reshape(-1, a.shape[-1])
    rest_out = adamw_many("adamw_small", *[[as2d(d[n]) for n in rest] for d in (wsh, g_final, msh, vsh)])
    stepped.update(zip(rest, rest_out))
    shaped = lambda n, a: a.reshape(given[n].shape)
    return (loss, grad_x[None],
            *[shaped(n, g_final[n]) for n in WEIGHTS],
            *[shaped(n, stepped[n][k]) for k in range(3) for n in WEIGHTS])
```

```python
import functools
import math

import jax
import jax.numpy as jnp
from jax import lax
from jax.experimental import pallas as pl
from jax.experimental.pallas import tpu as pltpu

F32 = jnp.float32
BF16 = jnp.bfloat16
MESH = pl.DeviceIdType.MESH

N_DEV = 8
D = 1024
N_HEADS = 8
HEAD_PAD = 128
QK_NOPE, QK_ROPE, V_HEAD = 64, 32, 64
QK_DIM = QK_NOPE + QK_ROPE
Q_RANK, KV_RANK = 384, 256
LRU_W, LRU_BLOCKS, LRU_BW = 1280, 10, 128
FFN = 2816
GRID_W = 64
ROPE_BASE = 10000.0
LRU_C = 8.0
EPS = 1e-6
Z_Q, Z_KV, Z_KR, Z_XB, Z_YB, Z_GL, Z_END = 0, 384, 640, 768, 2048, 3328, 5376
ADAM_LR, ADAM_B1, ADAM_B2, ADAM_EPS, ADAM_WD, ADAM_STEP = 0.001, 0.9, 0.999, 1e-08, 0.01, 10

VMEM_LIMIT = 52 * 1024 * 1024
FLAT_C = 512

WEIGHTS = ['c_ctx', 'w_mod', 'b_mod', 'norm1_g', 'w_in', 'b_gate', 'q_norm_g', 'kv_norm_g', 'w_uq', 'w_ukv',
           'w_o_attn', 'lru_conv_w', 'lru_conv_b', 'lru_w_a', 'lru_b_a', 'lru_w_x', 'lru_b_x', 'lru_lambda',
           'w_o_lru', 'w_out', 'norm2_g', 'w_up', 'ffn_conv_w', 'ffn_conv_b', 'w_down', 'final_g']
COL_SHARDED = ['w_in', 'w_uq', 'w_ukv', 'w_o_attn', 'lru_conv_w', 'lru_b_a', 'lru_b_x', 'lru_lambda', 'w_up',
               'ffn_conv_w']
ROW_SHARDED = ['w_o_lru', 'w_out', 'w_down']
BIG_BF16 = ['w_in', 'w_uq', 'w_ukv', 'w_o_attn', 'w_o_lru', 'w_out', 'w_up', 'w_down']
SMALL_F32 = ['lru_conv_w', 'lru_b_a', 'lru_b_x', 'lru_lambda', 'ffn_conv_w']
REPLICATED = ['c_ctx', 'b_mod', 'norm1_g', 'b_gate', 'q_norm_g', 'kv_norm_g', 'lru_conv_b', 'lru_w_a', 'lru_w_x',
              'norm2_g', 'ffn_conv_b', 'final_g']


def _cparams(sem=None):
    return pltpu.CompilerParams(dimension_semantics=sem, vmem_limit_bytes=VMEM_LIMIT)


def _pick(n, cands):
    for c in cands:
        if c <= n and n % c == 0:
            return c
    return n


def _best_div(n, mult, cap):
    best = mult
    for d in range(mult, min(n, cap) + 1, mult):
        if n % d == 0:
            best = d
    return best


MXU_DIM = 256
ROW_TILES = (1088, 1024, 544, 512, 256, 128, 64, 32, 16, 8)
LANE_TILES = (2816, 1792, 1536, 1280, 1024, 768, 512, 256, 1408, 896, 640, 384, 128)
DEPTH_ROW_TILES = (2176, 2048, 1024, 512, 256, 1088, 128, 64, 32, 16, 8)
MATMUL_VMEM_BUDGET = 40 * 1024 * 1024
MXU_FILL_OK = 0.9


def _my_pos():
    return lax.axis_index("x"), lax.axis_index("y"), lax.axis_index("c")


def _my_index():
    x, y, c = _my_pos()
    return 4 * x + 2 * y + c


def all_gather_multi(name, shards):
    n_arr = len(shards)
    arrays = range(n_arr)

    def body(*refs):
        x_refs, out_refs = refs[:n_arr], refs[n_arr:2 * n_arr]
        send_sems, recv_sems, local_sems = refs[2 * n_arr:]
        x, y, c = _my_pos()
        me, sibling = (x, y, c), (x, y, 1 - c)
        chips = [(1 - x, y), (x, 1 - y), (1 - x, 1 - y)]

        def slot(a, px, py, pc):
            return out_refs[a].at[4 * px + 2 * py + pc]

        def copy(a, k, block, to, src=None):
            return pltpu.make_async_remote_copy(
                src_ref=slot(a, *block) if src is None else src, dst_ref=slot(a, *block),
                send_sem=send_sems.at[7 * a + k], recv_sem=recv_sems.at[7 * a + k], device_id=to,
                device_id_type=MESH)

        mine = [pltpu.make_async_copy(x_refs[a], slot(a, *me), local_sems.at[a]) for a in arrays]
        first = [copy(a, 1 + j, me, (*chip, c), src=x_refs[a]) for j, chip in enumerate(chips) for a in arrays]
        first += [copy(a, 0, me, sibling, src=x_refs[a]) for a in arrays]
        for cp in first + mine:
            cp.start()
        passed = []
        for j, chip in enumerate(chips):
            for a in arrays:
                copy(a, 1 + j, (*chip, c), me).wait_recv()
                passed.append(copy(a, 4 + j, (*chip, c), sibling))
                passed[-1].start()
        for a in arrays:
            copy(a, 0, sibling, me).wait_recv()
            for j, chip in enumerate(chips):
                copy(a, 4 + j, (*chip, 1 - c), me).wait_recv()
        for cp in first + passed:
            cp.wait_send()
        for cp in mine:
            cp.wait()

    hbm = pl.BlockSpec(memory_space=pl.ANY)
    return pl.pallas_call(
        body, name=name,
        out_shape=[jax.ShapeDtypeStruct((N_DEV,) + s.shape, s.dtype) for s in shards],
        in_specs=[hbm] * n_arr, out_specs=[hbm] * n_arr,
        scratch_shapes=[pltpu.SemaphoreType.DMA((7 * n_arr,)), pltpu.SemaphoreType.DMA((7 * n_arr,)),
                        pltpu.SemaphoreType.DMA((n_arr,))],
    )(*shards)


def all_gather(name, shard):
    return all_gather_multi(name, [shard])[0]


def _peers():
    x, y, c = _my_pos()
    out = []
    for rel in (6, 4, 2, 7, 5, 3, 1):
        px, py, pc = x ^ ((rel >> 2) & 1), y ^ ((rel >> 1) & 1), c ^ (rel & 1)
        out.append((rel - 1, (px, py, pc), 4 * px + 2 * py + pc))
    return out


def _exchange_copies(mode, src_refs, land_refs, send_sems, recv_sems, with_arrivals):
    x, y, c = _my_pos()
    me = 4 * x + 2 * y + c
    sends, arrivals = [], []
    for k, peer_pos, peer in _peers():
        for a, (src, land) in enumerate(zip(src_refs, land_refs)):
            piece = src.at[peer] if mode == 'scatter' else src
            sems = dict(send_sem=send_sems[a].at[k], recv_sem=recv_sems[a].at[k], device_id_type=MESH)
            sends.append(pltpu.make_async_remote_copy(src_ref=piece, dst_ref=land.at[me], device_id=peer_pos, **sems))
            if with_arrivals:
                arrivals.append(pltpu.make_async_remote_copy(src_ref=piece, dst_ref=land.at[peer],
                                                             device_id=(x, y, c), **sems))
    return sends, arrivals


_HBM = pl.BlockSpec(memory_space=pltpu.HBM)
_SEM = pl.BlockSpec(memory_space=pltpu.SEMAPHORE)


def exchange_start(name, mode, arrays, after=()):
    n_arr, n_after = len(arrays), len(after)
    land_shapes = [a.shape if mode == 'scatter' else (N_DEV,) + a.shape for a in arrays]

    def body(*refs):
        src_refs, land_refs = refs[:n_arr], refs[n_arr:2 * n_arr]
        refs = refs[n_after:]
        send_sems, recv_sems = refs[2 * n_arr:3 * n_arr], refs[3 * n_arr:4 * n_arr]
        sends, _ = _exchange_copies(mode, src_refs, land_refs, send_sems, recv_sems, with_arrivals=False)
        for cp in sends:
            cp.start()
        token = refs[-1]
        token[...] = jnp.zeros_like(token)

    sem = pltpu.SemaphoreType.DMA((N_DEV - 1,))
    res = pl.pallas_call(
        body, name=name,
        out_shape=[sem] * (2 * n_arr) + [pltpu.HBM(a.shape, a.dtype) for a in arrays]
        + [pltpu.HBM(s, a.dtype) for s, a in zip(land_shapes, arrays)] + [jax.ShapeDtypeStruct((8, 128), F32)],
        in_specs=[_HBM] * (2 * n_arr) + [pl.BlockSpec(memory_space=pl.ANY)] * n_after,
        out_specs=[_SEM] * (2 * n_arr) + [_HBM] * (2 * n_arr) + [pl.BlockSpec(memory_space=pltpu.VMEM)],
        input_output_aliases={i: 2 * n_arr + i for i in range(2 * n_arr)},
        compiler_params=pltpu.CompilerParams(has_side_effects=pltpu.SideEffectType.DATAFLOW_SIDE_EFFECTING),
    )(*[pltpu.with_memory_space_constraint(a, pltpu.HBM) for a in arrays],
      *[pltpu.with_memory_space_constraint(lax.empty(s, a.dtype), pltpu.HBM) for s, a in zip(land_shapes, arrays)],
      *after)
    return (res[:n_arr], res[n_arr:2 * n_arr], res[2 * n_arr:3 * n_arr], res[3 * n_arr:4 * n_arr]), res[-1]


def exchange_wait(name, mode, started, after):
    send_sems, recv_sems, thru, land = started
    n_arr = len(thru)

    def body(*refs):
        src_refs, land_refs = refs[:n_arr], refs[n_arr:2 * n_arr]
        s_sems, r_sems = refs[2 * n_arr:3 * n_arr], refs[3 * n_arr:4 * n_arr]
        sends, arrivals = _exchange_copies(mode, src_refs, land_refs, s_sems, r_sems, with_arrivals=True)
        for cp in sends:
            cp.wait_send()
        for cp in arrivals:
            cp.wait_recv()

    res = pl.pallas_call(
        body, name=name,
        out_shape=[pltpu.HBM(a.shape, a.dtype) for a in thru] + [pltpu.HBM(a.shape, a.dtype) for a in land],
        in_specs=[_HBM] * (2 * n_arr) + [_SEM] * (2 * n_arr) + [pl.BlockSpec(memory_space=pl.ANY)],
        out_specs=[_HBM] * (2 * n_arr),
        input_output_aliases={i: i for i in range(2 * n_arr)},
        compiler_params=pltpu.CompilerParams(has_side_effects=pltpu.SideEffectType.DATAFLOW_SIDE_EFFECTING),
    )(*thru, *land, *send_sems, *recv_sems, after)
    return res[n_arr:]


def _sum_with_own(slot_ref, own_ref):
    x, y, c = _my_pos()
    me = 4 * x + 2 * y + c
    acc = None
    for p in range(N_DEV):
        v = jnp.where(me == p, own_ref[0], slot_ref[p]).astype(F32)
        acc = v if acc is None else acc + v
    return acc


def reduce_slots(name, slots, own, step=None):
    _, r, ccols = slots.shape
    tc = _pick(ccols, (256, 128))
    c1 = 1.0 - ADAM_B1 ** ADAM_STEP
    c2 = 1.0 - ADAM_B2 ** ADAM_STEP

    def body(s_ref, own_ref, *refs):
        g = _sum_with_own(s_ref, own_ref)
        if step is None:
            refs[0][...] = g
            return
        w_ref, m_ref, v_ref, g_ref, d_ref, nm_ref, nv_ref = refs
        nm = ADAM_B1 * m_ref[...] + (1.0 - ADAM_B1) * g
        nv = ADAM_B2 * v_ref[...] + (1.0 - ADAM_B2) * (g * g)
        g_ref[...] = g
        d_ref[...] = -ADAM_LR * ((nm / c1) / (jnp.sqrt(nv / c2) + ADAM_EPS) + ADAM_WD * w_ref[...])
        nm_ref[...] = nm
        nv_ref[...] = nv

    col = pl.BlockSpec((r, tc), lambda j: (0, j))
    n_out = 1 if step is None else 4
    res = pl.pallas_call(
        body, name=name, grid=(ccols // tc,),
        out_shape=[jax.ShapeDtypeStruct((r, ccols), F32)] * n_out,
        in_specs=[pl.BlockSpec((N_DEV, r, tc), lambda j: (0, 0, j)), pl.BlockSpec((1, r, tc), lambda j: (0, 0, j))]
        + [col] * (0 if step is None else 3),
        out_specs=[col] * n_out,
        compiler_params=_cparams(("parallel",)),
    )(slots, own, *(step or ()))
    return res[0] if step is None else res


def sum_slots(name, slots):
    _, r, ccols = slots.shape
    tc = _pick(ccols, (256, 128))

    def body(s_ref, o_ref):
        acc = s_ref[0].astype(F32)
        for p in range(1, N_DEV):
            acc = acc + s_ref[p].astype(F32)
        o_ref[...] = acc

    return pl.pallas_call(
        body, name=name, grid=(ccols // tc,),
        out_shape=jax.ShapeDtypeStruct((r, ccols), F32),
        in_specs=[pl.BlockSpec((N_DEV, r, tc), lambda j: (0, 0, j))],
        out_specs=pl.BlockSpec((r, tc), lambda j: (0, j)),
        compiler_params=_cparams(("parallel",)),
    )(slots)


def _mxu_fill(t):
    return t / (-(-t // MXU_DIM) * MXU_DIM)


def _matmul_tiles(mode, m_extent, n, k_extent, k_total, itemsizes):
    a_bytes, b_bytes, o_bytes = itemsizes
    m_cands = [c for c in (LANE_TILES if mode == 'tn' else ROW_TILES) if m_extent % c == 0] or [m_extent]
    k_cands = [c for c in (DEPTH_ROW_TILES if mode == 'tn' else LANE_TILES) if k_extent % c == 0] or [k_extent]
    n_cands = [c for c in LANE_TILES if n % c == 0] or [n]
    best = None
    for tm in m_cands:
        for tk in k_cands:
            for tn in n_cands:
                f32_tiles = 2 if k_total // tk > 1 else 1
                vmem = 2 * (tm * tk * a_bytes + tk * tn * b_bytes + tm * tn * o_bytes) + tm * tn * 4 * f32_tiles
                if vmem > MATMUL_VMEM_BUDGET:
                    continue
                key = (_mxu_fill(tn) * _mxu_fill(tk) >= MXU_FILL_OK, tm * tn * tk)
                if best is None or key > best[0]:
                    best = (key, (tm, tn, tk))
    assert best is not None, (mode, m_extent, n, k_extent)
    return best[1]


def matmul(name, a, b, mode, out_dtype, after=()):
    after = [t for t in after if t is not None]
    pieces, a_rows, a_cols = (1,) + a.shape if a.ndim == 2 else a.shape
    if mode == 'nn':
        (m, k), (k2, n) = (a_rows, pieces * a_cols), b.shape
    elif mode == 'nt':
        (m, k), (n, k2) = (a_rows, pieces * a_cols), b.shape
    else:
        (k, m), (k2, n) = (a_rows, pieces * a_cols), b.shape
    assert k == k2, (name, a.shape, b.shape, mode)
    tm, tn, tk = _matmul_tiles(mode, a_cols if mode == 'tn' else m, n, k if mode == 'tn' else a_cols, k,
                               (a.dtype.itemsize, b.dtype.itemsize, jnp.dtype(out_dtype).itemsize))
    nk = k // tk
    per_piece = a_cols // (tm if mode == 'tn' else tk)
    if a.ndim == 2:
        a_block = lambda rows, cols, at: pl.BlockSpec((rows, cols), at)
    else:
        a_block = lambda rows, cols, at: pl.BlockSpec(
            (None, rows, cols), lambda i, j, kk: (at(i, j, kk)[1] // per_piece, at(i, j, kk)[0],
                                                  at(i, j, kk)[1] % per_piece))
    if mode == 'nn':
        a_spec = a_block(tm, tk, lambda i, j, kk: (i, kk))
        b_spec = pl.BlockSpec((tk, tn), lambda i, j, kk: (kk, j))
        dn = (((1,), (0,)), ((), ()))
    elif mode == 'nt':
        a_spec = a_block(tm, tk, lambda i, j, kk: (i, kk))
        b_spec = pl.BlockSpec((tn, tk), lambda i, j, kk: (j, kk))
        dn = (((1,), (1,)), ((), ()))
    else:
        a_spec = a_block(tk, tm, lambda i, j, kk: (kk, i))
        b_spec = pl.BlockSpec((tk, tn), lambda i, j, kk: (kk, j))
        dn = (((0,), (0,)), ((), ()))

    def product(a_ref, b_ref):
        return lax.dot_general(a_ref[...].astype(BF16), b_ref[...].astype(BF16), dn, preferred_element_type=F32)

    n_after = len(after)

    def body_one(a_ref, b_ref, *rest):
        o_ref = rest[n_after]
        o_ref[...] = product(a_ref, b_ref).astype(o_ref.dtype)

    def body(a_ref, b_ref, *rest):
        o_ref, acc_ref = rest[n_after:]
        kk = pl.program_id(2)

        @pl.when(kk == 0)
        def _():
            acc_ref[...] = jnp.zeros_like(acc_ref)

        acc_ref[...] += product(a_ref, b_ref)

        @pl.when(kk == nk - 1)
        def _():
            o_ref[...] = acc_ref[...].astype(o_ref.dtype)

    return pl.pallas_call(
        body_one if nk == 1 else body, name=name, grid=(m // tm, n // tn, nk),
        out_shape=jax.ShapeDtypeStruct((m, n), out_dtype),
        in_specs=[a_spec, b_spec] + [pl.BlockSpec(memory_space=pl.ANY)] * n_after,
        out_specs=pl.BlockSpec((tm, tn), lambda i, j, kk: (i, j)),
        scratch_shapes=[] if nk == 1 else [pltpu.VMEM((tm, tn), F32)],
        compiler_params=_cparams(("parallel", "parallel", "arbitrary")),
    )(a, b, *after)


def rowwise(name, fn, rows, params, out_rows, out_accs, n_rows, t_lat, tm):
    nb, nbl = n_rows // tm, t_lat // tm
    in_specs, piece_counts = [], []
    operands = []
    for arr, off, width, *kind in rows:
        g = math.gcd(off, width) if off else width
        assert g % 128 == 0 or (off == 0 and width == arr.shape[1]), (name, off, width)
        cnt = width // g
        last = arr.shape[0] // tm - 1
        clamp = arr.shape[0] < n_rows
        for p in range(cnt):
            cb = off // g + p
            if kind == ['ctx']:
                in_specs.append(pl.BlockSpec(
                    (tm, g), lambda i, cb=cb, last=last: (jnp.clip(i - nbl, 0, last), cb)))
            elif clamp:
                in_specs.append(pl.BlockSpec((tm, g), lambda i, cb=cb, last=last: (jnp.minimum(i, last), cb)))
            else:
                in_specs.append(pl.BlockSpec((tm, g), lambda i, cb=cb: (i, cb)))
            operands.append(arr)
        piece_counts.append(cnt)
    for p in params:
        in_specs.append(pl.BlockSpec(p.shape, lambda i, nd=p.ndim: (0,) * nd))
        operands.append(p)
    n_in = sum(piece_counts)
    n_par = len(params)
    n_or = len(out_rows)
    lat_only = [kind == ['lat'] for _, _, *kind in out_rows]
    out_shape = [jax.ShapeDtypeStruct((t_lat if lat else n_rows, w), dt)
                 for (w, dt, *_), lat in zip(out_rows, lat_only)]
    out_shape += [jax.ShapeDtypeStruct(s, F32) for s in out_accs]
    out_specs = [pl.BlockSpec((tm, w), (lambda i: (jnp.minimum(i, nbl - 1), 0)) if lat else (lambda i: (i, 0)))
                 for (w, *_), lat in zip(out_rows, lat_only)]
    out_specs += [pl.BlockSpec(s, lambda i, nd=len(s): (0,) * nd) for s in out_accs]

    def body(*refs):
        in_refs, par_refs = refs[:n_in], refs[n_in:n_in + n_par]
        orow_refs = refs[n_in + n_par:n_in + n_par + n_or]
        oacc_refs = refs[n_in + n_par + n_or:]
        i = pl.program_id(0)
        tiles, at = [], 0
        for cnt in piece_counts:
            parts = [in_refs[at + p][...].astype(F32) for p in range(cnt)]
            tiles.append(parts[0] if cnt == 1 else jnp.concatenate(parts, axis=1))
            at += cnt
        is_ctx = i * tm >= t_lat
        outs, accs = fn(is_ctx, tiles, [p[...] for p in par_refs])
        for o_ref, o, lat in zip(orow_refs, outs, lat_only):
            if lat:
                @pl.when(jnp.logical_not(is_ctx))
                def _(o_ref=o_ref, o=o):
                    o_ref[...] = o.astype(o_ref.dtype)
            else:
                o_ref[...] = o.astype(o_ref.dtype)
        if oacc_refs:
            @pl.when(i == 0)
            def _():
                for a_ref in oacc_refs:
                    a_ref[...] = jnp.zeros_like(a_ref)
            for a_ref, a in zip(oacc_refs, accs):
                a_ref[...] += a.astype(F32)

    res = pl.pallas_call(
        body, name=name, grid=(nb,),
        out_shape=out_shape, in_specs=in_specs, out_specs=out_specs,
        compiler_params=_cparams(("arbitrary",)),
    )(*operands)
    return res[:n_or], res[n_or:]


def _rms(x, g):
    return x * lax.rsqrt(jnp.mean(x * x, axis=-1, keepdims=True) + EPS) * g


def _norm_mod(x, g, sc, sh):
    return _rms(x, g) * (1.0 + sc) + sh


def _sigmoid(x):
    return 0.5 * jnp.tanh(0.5 * x) + 0.5


def _silu(x):
    return x * _sigmoid(x)


def _gelu(x):
    return 0.5 * x * (1.0 + jnp.tanh(math.sqrt(2.0 / math.pi) * (x + 0.044715 * (x * x * x))))


def _sel(is_ctx, p):
    return jnp.where(is_ctx, p[1:2], p[0:1])


def _seg_acc(is_ctx, v):
    rows = lax.broadcasted_iota(jnp.int32, (2, v.shape[1]), 0)
    return jnp.where(rows == is_ctx.astype(jnp.int32), jnp.broadcast_to(v, (2, v.shape[1])), 0.0)


def _rsum(v):
    return jnp.sum(v, axis=0, keepdims=True)


def _shift_rows(x, o, t_lat, n):
    if o == 0:
        return x
    y = pltpu.roll(x, (-o) % n, 0)
    t = lax.broadcasted_iota(jnp.int32, x.shape, 0)
    if o > 0:
        ok = t < n - o
        if t_lat < n:
            ok = ok & ((t < t_lat - o) | (t >= t_lat))
    else:
        ok = t >= -o
        if t_lat < n:
            ok = ok & ((t < t_lat) | (t >= t_lat - o))
    return jnp.where(ok, y, 0.0)


def conv_fwd(name, xarr, col_off, width, w, b, left, n_rows, t_lat, out_dtype, cb=128):
    taps = w.shape[0]
    assert col_off % cb == 0 and width % cb == 0

    def body(x_ref, w_ref, b_ref, o_ref):
        x = x_ref[...].astype(F32)
        acc = jnp.broadcast_to(b_ref[...], x.shape)
        for k in range(taps):
            acc = acc + _shift_rows(x, k - left, t_lat, n_rows) * w_ref[k:k + 1, :]
        o_ref[...] = acc.astype(o_ref.dtype)

    return pl.pallas_call(
        body, name=name, grid=(width // cb,),
        out_shape=jax.ShapeDtypeStruct((n_rows, width), out_dtype),
        in_specs=[pl.BlockSpec((n_rows, cb), lambda j: (0, col_off // cb + j)),
                  pl.BlockSpec((taps, cb), lambda j: (0, j)),
                  pl.BlockSpec((1, cb), lambda j: (0, j))],
        out_specs=pl.BlockSpec((n_rows, cb), lambda j: (0, j)),
        compiler_params=_cparams(("parallel",)),
    )(xarr, w, b)


def conv_bwd(name, dout, xarr, col_off, width, w, left, n_rows, t_lat, cb=128):
    taps = w.shape[0]

    def body(d_ref, x_ref, w_ref, dx_ref, dw_ref, db_ref):
        d = d_ref[...].astype(F32)
        x = x_ref[...].astype(F32)
        dx = jnp.zeros_like(d)
        dws = []
        for k in range(taps):
            dx = dx + _shift_rows(d, left - k, t_lat, n_rows) * w_ref[k:k + 1, :]
            dws.append(_rsum(d * _shift_rows(x, k - left, t_lat, n_rows)))
        dx_ref[...] = dx.astype(dx_ref.dtype)
        dw_ref[...] = jnp.concatenate(dws, axis=0)
        db_ref[...] = _rsum(d)

    return pl.pallas_call(
        body, name=name, grid=(width // cb,),
        out_shape=[jax.ShapeDtypeStruct((n_rows, width), BF16), jax.ShapeDtypeStruct((taps, width), F32),
                   jax.ShapeDtypeStruct((1, width), F32)],
        in_specs=[pl.BlockSpec((n_rows, cb), lambda j: (0, j)),
                  pl.BlockSpec((n_rows, cb), lambda j: (0, col_off // cb + j)),
                  pl.BlockSpec((taps, cb), lambda j: (0, j))],
        out_specs=[pl.BlockSpec((n_rows, cb), lambda j: (0, j)), pl.BlockSpec((taps, cb), lambda j: (0, j)),
                   pl.BlockSpec((1, cb), lambda j: (0, j))],
        compiler_params=_cparams(("parallel",)),
    )(dout, xarr, w)


def _ffn_conv(a, w_ref, b_ref, t_lat):
    shifted = [_shift_rows(a, k - 1, t_lat, t_lat) for k in range(3)]
    ac = jnp.broadcast_to(b_ref[...], a.shape)
    for k in range(3):
        ac = ac + shifted[k] * w_ref[k:k + 1, :]
    return ac, shifted


def ffn_mix_fwd(u, w, b, t_lat, cb=128):
    nblk = FFN // cb

    def body(a_ref, g_ref, w_ref, b_ref, f_ref):
        ac, _ = _ffn_conv(a_ref[...].astype(F32), w_ref, b_ref, t_lat)
        f_ref[...] = (_silu(ac) * g_ref[...].astype(F32)).astype(f_ref.dtype)

    col = lambda shape, off=0: pl.BlockSpec(shape, lambda j: (0, off + j))
    return pl.pallas_call(
        body, name="ffn_mix", grid=(nblk,),
        out_shape=jax.ShapeDtypeStruct((t_lat, FFN), BF16),
        in_specs=[col((t_lat, cb)), col((t_lat, cb), nblk), col((3, cb)), col((1, cb))],
        out_specs=col((t_lat, cb)),
        compiler_params=_cparams(("parallel",)),
    )(u, u, w, b)


def ffn_mix_bwd(u, df, w, b, t_lat, cb=128):
    nblk = FFN // cb

    def body(a_ref, g_ref, df_ref, w_ref, b_ref, du_ref, dw_ref, db_ref):
        ac, shifted = _ffn_conv(a_ref[...].astype(F32), w_ref, b_ref, t_lat)
        d = df_ref[...].astype(F32)
        s = _sigmoid(ac)
        du_ref[1] = (d * (ac * s)).astype(du_ref.dtype)
        dac = d * g_ref[...].astype(F32) * (s * (1.0 + ac * (1.0 - s)))
        da = jnp.zeros_like(dac)
        for k in range(3):
            da = da + _shift_rows(dac, 1 - k, t_lat, t_lat) * w_ref[k:k + 1, :]
        du_ref[0] = da.astype(du_ref.dtype)
        dw_ref[...] = jnp.concatenate([_rsum(dac * shifted[k]) for k in range(3)], axis=0)
        db_ref[...] = _rsum(dac)

    col = lambda shape, off=0: pl.BlockSpec(shape, lambda j: (0, off + j))
    return pl.pallas_call(
        body, name="ffn_mix_bwd", grid=(nblk,),
        out_shape=[jax.ShapeDtypeStruct((2, t_lat, FFN), BF16),
                   jax.ShapeDtypeStruct((3, FFN), F32), jax.ShapeDtypeStruct((1, FFN), F32)],
        in_specs=[col((t_lat, cb)), col((t_lat, cb), nblk), col((t_lat, cb)), col((3, cb)), col((1, cb))],
        out_specs=[pl.BlockSpec((2, t_lat, cb), lambda j: (0, 0, j)), col((3, cb)), col((1, cb))],
        compiler_params=_cparams(("parallel",)),
    )(u, u, df, w, b)


def _chunk_order(direction, nb, nbl):
    if direction == 'f':
        return lambda s: ((s + nbl) % nb, 0)
    return lambda s: (nb - 1 - s, 0)


def _adjoint_order(direction, nb, nbl):
    if direction == 'f':
        return lambda s: ((nb - 1 - s + nbl) % nb, 0)
    return lambda s: (s, 0)


SUBLANES = 8


def _chunk_scan(a, b, carry, rev):
    tc, width = a.shape
    nt = tc // SUBLANES
    row = lax.broadcasted_iota(jnp.int32, a.shape, 0)
    a, b = a.reshape(nt, SUBLANES, width), b.reshape(nt, SUBLANES, width)
    in_tile = lax.broadcasted_iota(jnp.int32, a.shape, 1)
    for k in (1, 2, 4):
        shift = SUBLANES - k if rev else k
        edge = in_tile >= SUBLANES - k if rev else in_tile < k
        b = jnp.where(edge, b, a * pltpu.roll(b, shift, 1) + b)
        a = jnp.where(edge, a, a * pltpu.roll(a, shift, 1))
    a, b = a.reshape(tc, width), b.reshape(tc, width)
    hs = [None] * nt
    c = carry
    for kt in range(nt):
        k = nt - 1 - kt if rev else kt
        h = b[k * SUBLANES:(k + 1) * SUBLANES] + a[k * SUBLANES:(k + 1) * SUBLANES] * c
        hs[k] = h
        c = h[0:1] if rev else h[SUBLANES - 1:SUBLANES]
    h = jnp.concatenate(hs, axis=0)
    if rev:
        return h, jnp.where(row == tc - 1, carry, pltpu.roll(h, tc - 1, 0)), c
    return h, jnp.where(row == 0, carry, pltpu.roll(h, 1, 0)), c


def _one_minus_a_squared(log_a, a):
    return (1.0 + a * a) * jnp.tanh(-log_a)


def _gate_elem(pre_r, pre_i, xc, b_a, b_x, sp):
    r = _sigmoid(pre_r + b_a)
    i = _sigmoid(pre_i + b_x)
    log_a = (-LRU_C) * r * sp
    a = jnp.exp(log_a)
    m2 = _one_minus_a_squared(log_a, a)
    mult = jnp.where(m2 > 0.0, m2 * lax.rsqrt(m2), 0.0)
    return a, mult * (i * xc)


def _gate_elem_bwd(pre_r, pre_i, xc, b_a, b_x, sp, da, du):
    r = _sigmoid(pre_r + b_a)
    i = _sigmoid(pre_i + b_x)
    log_a = (-LRU_C) * r * sp
    a = jnp.exp(log_a)
    m2 = _one_minus_a_squared(log_a, a)
    inv_mult = lax.rsqrt(m2)
    g = du * (m2 * inv_mult)
    d_mult = du * (i * xc)
    d_log_a = (da - d_mult * a * inv_mult) * a
    d_pre_r = d_log_a * ((-LRU_C) * sp) * (r * (1.0 - r))
    d_pre_i = g * xc * (i * (1.0 - i))
    return d_pre_r, d_pre_i, g * i, _rsum(d_log_a * ((-LRU_C) * r))


def _blockdiag(xb16, w_ref_val, d):
    outs = []
    for n in range(LRU_BLOCKS):
        outs.append(jnp.dot(xb16[:, n * LRU_BW:(n + 1) * LRU_BW], w_ref_val[d * LRU_BLOCKS + n],
                            preferred_element_type=F32))
    return jnp.concatenate(outs, axis=1)


def lru_scan(name, xc, w_a, w_x, b_a, b_x, sp, direction, n_rows, t_lat):
    w = xc.shape[1]
    d = 0 if direction == 'f' else 1
    tc = _pick(math.gcd(t_lat, n_rows), (256, 128))
    nb, nbl = n_rows // tc, t_lat // tc
    order = _chunk_order(direction, nb, nbl)
    rev = direction == 'b'

    def body(x_ref, wa_ref, wx_ref, ba_ref, bx_ref, sp_ref, a_ref, h_ref, hp_ref, carry):
        @pl.when(pl.program_id(0) == 0)
        def _():
            carry[...] = jnp.zeros_like(carry)

        x = x_ref[...]
        xb16 = x.astype(BF16)
        a, u = _gate_elem(_blockdiag(xb16, wa_ref[...], d), _blockdiag(xb16, wx_ref[...], d), x,
                          ba_ref[d:d + 1, :], bx_ref[d:d + 1, :], sp_ref[d:d + 1, :])
        a_ref[...] = a
        h_ref[...], hp_ref[...], carry[...] = _chunk_scan(a, u, carry[...], rev)

    spec = pl.BlockSpec((tc, w), order)
    whole = lambda p: pl.BlockSpec(p.shape, lambda s, nd=p.ndim: (0,) * nd)
    return pl.pallas_call(
        body, name=name, grid=(nb,),
        out_shape=[jax.ShapeDtypeStruct((n_rows, w), F32)] * 3,
        in_specs=[spec] + [whole(p) for p in (w_a, w_x, b_a, b_x, sp)], out_specs=[spec] * 3,
        scratch_shapes=[pltpu.VMEM((1, w), F32)],
        compiler_params=_cparams(("arbitrary",)),
    )(xc, w_a, w_x, b_a, b_x, sp)


def lru_scan_bwd(name, xc, a, dh, hprev, dxc_in, w_a, w_x, b_a, b_x, sp, direction, n_rows, t_lat):
    w = xc.shape[1]
    d = 0 if direction == 'f' else 1
    tc = _pick(math.gcd(t_lat, n_rows), (256, 128))
    nb, nbl = n_rows // tc, t_lat // tc
    order = _adjoint_order(direction, nb, nbl)
    rev = direction == 'f'
    has_in = dxc_in is not None
    nt_dims, tn_dims = (((1,), (1,)), ((), ())), (((0,), (0,)), ((), ()))

    def dh_order(s):
        c, _ = order(s)
        return (jnp.minimum(c, nbl - 1), 0)

    def body(*refs):
        x_ref, a_ref, dh_ref, hp_ref = refs[:4]
        in_ref = refs[4] if has_in else None
        wa_ref, wx_ref, ba_ref, bx_ref, sp_ref = refs[4 + has_in:9 + has_in]
        dx_ref, dwa_ref, dwx_ref, dba_ref, dbx_ref, dsp_ref, carry = refs[9 + has_in:]
        s = pl.program_id(0)

        @pl.when(s == 0)
        def _():
            carry[...] = jnp.zeros_like(carry)
            for acc in (dwa_ref, dwx_ref, dba_ref, dbx_ref, dsp_ref):
                acc[...] = jnp.zeros_like(acc)

        chunk, _ = order(s)
        live = (chunk < nbl).astype(F32)
        av = a_ref[...]
        dv = dh_ref[...].astype(F32) * live
        _, c_next, carry[...] = _chunk_scan(av, av * dv, carry[...], rev)
        lam = dv + c_next

        x = x_ref[...]
        xb16 = x.astype(BF16)
        wa, wx = wa_ref[...], wx_ref[...]
        dpr, dpi, dxc, dsp_d = _gate_elem_bwd(_blockdiag(xb16, wa, d), _blockdiag(xb16, wx, d), x,
                                              ba_ref[d:d + 1, :], bx_ref[d:d + 1, :], sp_ref[d:d + 1, :],
                                              lam * hp_ref[...], lam)
        dpr16, dpi16 = dpr.astype(BF16), dpi.astype(BF16)
        back = []
        for n in range(LRU_BLOCKS):
            sl = slice(n * LRU_BW, (n + 1) * LRU_BW)
            back.append(lax.dot_general(dpr16[:, sl], wa[d * LRU_BLOCKS + n], nt_dims, preferred_element_type=F32)
                        + lax.dot_general(dpi16[:, sl], wx[d * LRU_BLOCKS + n], nt_dims, preferred_element_type=F32))
            dwa_ref[n] += lax.dot_general(xb16[:, sl], dpr16[:, sl], tn_dims, preferred_element_type=F32)
            dwx_ref[n] += lax.dot_general(xb16[:, sl], dpi16[:, sl], tn_dims, preferred_element_type=F32)
        dxc = dxc + jnp.concatenate(back, axis=1)
        dx_ref[...] = dxc + in_ref[...] if has_in else dxc
        dba_ref[...] += _rsum(dpr)
        dbx_ref[...] += _rsum(dpi)
        dsp_ref[...] += dsp_d

    spec = pl.BlockSpec((tc, w), order)
    whole = lambda shape: pl.BlockSpec(shape, lambda s, nd=len(shape): (0,) * nd)
    params = (w_a, w_x, b_a, b_x, sp)
    acc_shapes = [(LRU_BLOCKS, LRU_BW, LRU_BW)] * 2 + [(1, w)] * 3
    return pl.pallas_call(
        body, name=name, grid=(nb,),
        out_shape=[jax.ShapeDtypeStruct((n_rows, w), F32)] + [jax.ShapeDtypeStruct(sh, F32) for sh in acc_shapes],
        in_specs=[spec, spec, pl.BlockSpec((tc, w), dh_order), spec] + [spec] * has_in
        + [whole(p.shape) for p in params],
        out_specs=[spec] + [whole(sh) for sh in acc_shapes],
        scratch_shapes=[pltpu.VMEM((1, w), F32)],
        compiler_params=_cparams(("arbitrary",)),
    )(xc, a, dh, hprev, *([dxc_in] if has_in else []), *params)


def _rope_tables(t_lat, n_rows):
    rows = t_lat // GRID_W
    row_ids = jnp.repeat(jnp.arange(rows), GRID_W).astype(F32)
    col_ids = jnp.tile(jnp.arange(GRID_W), rows).astype(F32)
    axis_dim = QK_ROPE // 2
    inv = 1.0 / (ROPE_BASE ** (jnp.arange(0, axis_dim, 2, dtype=F32) / axis_dim))
    ang = jnp.concatenate([row_ids[:, None] * inv, col_ids[:, None] * inv], axis=-1)
    cos, sin = jnp.cos(ang), jnp.sin(ang)
    half = QK_ROPE // 2
    ones, zeros = jnp.ones((t_lat, QK_NOPE), F32), jnp.zeros((t_lat, QK_NOPE), F32)
    pad1, pad0 = jnp.ones((t_lat, HEAD_PAD - QK_DIM), F32), jnp.zeros((t_lat, HEAD_PAD - QK_DIM), F32)
    zh = jnp.zeros((t_lat, half), F32)
    c_tab = jnp.concatenate([ones, cos, cos, pad1], axis=1)
    s1 = jnp.concatenate([zeros, -sin, zh, pad0], axis=1)
    s2 = jnp.concatenate([zeros, zh, sin, pad0], axis=1)
    n_ctx = n_rows - t_lat
    c_tab = jnp.concatenate([c_tab, jnp.ones((n_ctx, HEAD_PAD), F32)], axis=0)
    s1 = jnp.concatenate([s1, jnp.zeros((n_ctx, HEAD_PAD), F32)], axis=0)
    s2 = jnp.concatenate([s2, jnp.zeros((n_ctx, HEAD_PAD), F32)], axis=0)
    return c_tab, s1, s2


def _rope(x, c, s1, s2):
    half = QK_ROPE // 2
    return x * c + pltpu.roll(x, HEAD_PAD - half, 1) * s1 + pltpu.roll(x, half, 1) * s2


def _rope_t(dy, c, s1, s2):
    half = QK_ROPE // 2
    return dy * c + pltpu.roll(dy * s1, half, 1) + pltpu.roll(dy * s2, HEAD_PAD - half, 1)


def _heads(x):
    return [x[:, h * HEAD_PAD:(h + 1) * HEAD_PAD] for h in range(N_HEADS)]


Q_SCALE = QK_DIM ** -0.5 * math.log2(math.e)
ATTN_BWD_HEADS = 1
ATTN_BWD_KEY_CHUNKS = 3
ATTN_FWD_HEADS = 4

def attn_fwd(q, k, v, t_lat, n_rows, tq):
    def body(q_ref, k_ref, v_ref, o_ref, lse_ref):
        for hh in range(ATTN_FWD_HEADS):
            lanes = slice(hh * HEAD_PAD, (hh + 1) * HEAD_PAD)
            s = lax.dot_general(q_ref[:, lanes], k_ref[:, lanes], (((1,), (1,)), ((), ())),
                                preferred_element_type=F32)
            m = jnp.max(s, axis=-1, keepdims=True)
            p = jnp.exp2(s - m)
            l = jnp.sum(p, axis=-1, keepdims=True)
            o = jnp.dot(p.astype(BF16), v_ref[:, lanes], preferred_element_type=F32) / l
            o_ref[:, lanes] = o.astype(o_ref.dtype)
            lse_ref[:, lanes] = jnp.broadcast_to(m + jnp.log2(l), (tq, HEAD_PAD))

    width = ATTN_FWD_HEADS * HEAD_PAD
    qspec = pl.BlockSpec((tq, width), lambda h, i: (i, h))
    kspec = pl.BlockSpec((n_rows, width), lambda h, i: (0, h))
    return pl.pallas_call(
        body, name="attn_fwd", grid=(N_HEADS // ATTN_FWD_HEADS, t_lat // tq),
        out_shape=[jax.ShapeDtypeStruct((t_lat, N_HEADS * HEAD_PAD), BF16),
                   jax.ShapeDtypeStruct((t_lat, N_HEADS * HEAD_PAD), F32)],
        in_specs=[qspec, kspec, kspec], out_specs=[qspec, qspec],
        compiler_params=_cparams(("parallel", "arbitrary")),
    )(q, k, v)


def attn_bwd(q, k, v, o, do, lse, t_lat, n_rows, tq):
    scale = QK_DIM ** -0.5
    nq = t_lat // tq
    nt = (((1,), (1,)), ((), ()))
    tn = (((0,), (0,)), ((), ()))

    def body(q_ref, k_ref, v_ref, o_ref, do_ref, lse_ref, dq_ref, dk_ref, dv_ref):
        @pl.when(pl.program_id(1) == 0)
        def _():
            dk_ref[...] = jnp.zeros_like(dk_ref)
            dv_ref[...] = jnp.zeros_like(dv_ref)

        for hh in range(ATTN_BWD_HEADS):
            lanes = slice(hh * HEAD_PAD, (hh + 1) * HEAD_PAD)
            qv, dov = q_ref[:, lanes], do_ref[:, lanes]
            lse_col = lse_ref[:, hh * HEAD_PAD:hh * HEAD_PAD + 1]
            delta = jnp.sum(dov.astype(F32) * o_ref[:, lanes].astype(F32), axis=-1, keepdims=True)
            dq = jnp.zeros((tq, HEAD_PAD), F32)
            for lo, hi in key_chunks:
                kv, vv = k_ref[lo:hi, lanes], v_ref[lo:hi, lanes]
                s = lax.dot_general(qv, kv, nt, preferred_element_type=F32)
                p = jnp.exp2(s - lse_col)
                dv_ref[lo:hi, lanes] += lax.dot_general(p.astype(BF16), dov, tn, preferred_element_type=F32)
                dp = lax.dot_general(dov, vv, nt, preferred_element_type=F32)
                ds = (p * (dp - delta)).astype(BF16)
                dq = dq + jnp.dot(ds, kv, preferred_element_type=F32)
                dk_ref[lo:hi, lanes] += lax.dot_general(ds, qv, tn, preferred_element_type=F32)
            dq_ref[:, lanes] = (dq * scale).astype(dq_ref.dtype)

        @pl.when(pl.program_id(1) == nq - 1)
        def _():
            dk_ref[...] = dk_ref[...] * (scale / Q_SCALE)

    n_key_tiles = n_rows // MXU_DIM
    bounds = [round(c * n_key_tiles / ATTN_BWD_KEY_CHUNKS) * MXU_DIM for c in range(ATTN_BWD_KEY_CHUNKS)] + [n_rows]
    key_chunks = [(lo, hi) for lo, hi in zip(bounds[:-1], bounds[1:]) if hi > lo]
    width = ATTN_BWD_HEADS * HEAD_PAD
    qspec = pl.BlockSpec((tq, width), lambda h, i: (i, h))
    kspec = pl.BlockSpec((n_rows, width), lambda h, i: (0, h))
    return pl.pallas_call(
        body, name="attn_bwd", grid=(N_HEADS // ATTN_BWD_HEADS, t_lat // tq),
        out_shape=[jax.ShapeDtypeStruct((t_lat, N_HEADS * HEAD_PAD), BF16),
                   jax.ShapeDtypeStruct((n_rows, N_HEADS * HEAD_PAD), F32),
                   jax.ShapeDtypeStruct((n_rows, N_HEADS * HEAD_PAD), F32)],
        in_specs=[qspec, kspec, kspec, qspec, qspec, qspec], out_specs=[qspec, kspec, kspec],
        compiler_params=_cparams(("parallel", "arbitrary")),
    )(q, k, v, o, do, lse)


def adamw(name, w, g, m, v):
    r, ccols = w.shape
    if r % 8 == 0:
        tr, tcol = _best_div(r, 8, max(8, 262144 // ccols)), ccols
    else:
        tr, tcol = r, _pick(ccols, (256, 128))
    c1 = 1.0 - ADAM_B1 ** ADAM_STEP
    c2 = 1.0 - ADAM_B2 ** ADAM_STEP

    def body(w_ref, g_ref, m_ref, v_ref, d_ref, nm_ref, nv_ref):
        gv = g_ref[...]
        nm = ADAM_B1 * m_ref[...] + (1.0 - ADAM_B1) * gv
        nv = ADAM_B2 * v_ref[...] + (1.0 - ADAM_B2) * (gv * gv)
        d_ref[...] = -ADAM_LR * ((nm / c1) / (jnp.sqrt(nv / c2) + ADAM_EPS) + ADAM_WD * w_ref[...])
        nm_ref[...] = nm
        nv_ref[...] = nv

    spec = pl.BlockSpec((tr, tcol), lambda i, j: (i, j))
    return pl.pallas_call(
        body, name=name, grid=(r // tr, ccols // tcol),
        out_shape=[jax.ShapeDtypeStruct((r, ccols), F32)] * 3,
        in_specs=[spec] * 4, out_specs=[spec] * 3,
        compiler_params=_cparams(("parallel", "parallel")),
    )(w, g, m, v)


def adamw_many(name, ws, gs, ms, vs):
    n = len(ws)
    c1 = 1.0 - ADAM_B1 ** ADAM_STEP
    c2 = 1.0 - ADAM_B2 ** ADAM_STEP

    def body(*refs):
        for i in range(n):
            w_ref, g_ref, m_ref, v_ref = (refs[k * n + i] for k in range(4))
            d_ref, nm_ref, nv_ref = (refs[(4 + k) * n + i] for k in range(3))
            gv = g_ref[...]
            nm = ADAM_B1 * m_ref[...] + (1.0 - ADAM_B1) * gv
            nv = ADAM_B2 * v_ref[...] + (1.0 - ADAM_B2) * (gv * gv)
            d_ref[...] = -ADAM_LR * ((nm / c1) / (jnp.sqrt(nv / c2) + ADAM_EPS) + ADAM_WD * w_ref[...])
            nm_ref[...] = nm
            nv_ref[...] = nv

    vmem = pl.BlockSpec(memory_space=pltpu.VMEM)
    res = pl.pallas_call(
        body, name=name,
        out_shape=[jax.ShapeDtypeStruct(w.shape, F32) for w in ws] * 3,
        in_specs=[vmem] * (4 * n), out_specs=[vmem] * (3 * n),
        compiler_params=_cparams(),
    )(*ws, *gs, *ms, *vs)
    return [tuple(res[k * n + i] for k in range(3)) for i in range(n)]


def _flat(parts, dtype, row_mult):
    v = jnp.concatenate([p.reshape(-1).astype(dtype) for p in parts])
    quantum = row_mult * FLAT_C
    total = -(-v.shape[0] // quantum) * quantum
    return jnp.pad(v, (0, total - v.shape[0])).reshape(total // FLAT_C, FLAT_C)


def _gathered_to_full(name, g):
    k = g.shape[1]
    return jnp.transpose(g, (1, 0, 2)).reshape(k, N_DEV * g.shape[2])


def _shard_to_rb(name, w):
    return w if name in ROW_SHARDED else w.T


def _rb_to_shard(name, g):
    return g if name in ROW_SHARDED else g.T


def _rb_from_gathered(name, g):
    cols = g.shape[2]
    if name == 'w_in':
        z = lambda k: jnp.zeros((k, cols), g.dtype)
        full = g.reshape(N_DEV * g.shape[1], cols)
        return jnp.concatenate([full[:Z_KR], z(QK_NOPE), full[Z_KR:Z_KR + QK_ROPE], z(HEAD_PAD - QK_DIM),
                                full[Z_KR + QK_ROPE:]], axis=0)
    if name == 'w_uq':
        return jnp.pad(g, ((0, 0), (0, HEAD_PAD - QK_DIM), (0, 0))).reshape(N_HEADS * HEAD_PAD, cols)
    if name == 'w_ukv':
        pad = lambda t: jnp.pad(t, ((0, 0), (0, HEAD_PAD - t.shape[1]), (0, 0))).reshape(N_HEADS * HEAD_PAD, cols)
        return jnp.concatenate([pad(g[:, :QK_NOPE]), pad(g[:, QK_NOPE:])], axis=0)
    if name == 'w_o_attn':
        full = g.reshape(D, N_HEADS, V_HEAD)
        return jnp.pad(full, ((0, 0), (0, 0), (0, HEAD_PAD - V_HEAD))).reshape(D, N_HEADS * HEAD_PAD)
    return g.reshape(N_DEV * g.shape[1], cols)


def _chunks_from_rb_grad(name, g):
    cols = g.shape[1]
    if name == 'w_in':
        full = jnp.concatenate([g[:Z_KR], g[Z_KR + QK_NOPE:Z_KR + QK_DIM], g[Z_XB:]], axis=0)
        return full.reshape(N_DEV, -1, cols)
    if name == 'w_uq':
        return g.reshape(N_HEADS, HEAD_PAD, cols)[:, :QK_DIM]
    if name == 'w_ukv':
        half = N_HEADS * HEAD_PAD
        gk = g[:half].reshape(N_HEADS, HEAD_PAD, cols)[:, :QK_NOPE]
        gv = g[half:].reshape(N_HEADS, HEAD_PAD, cols)[:, :V_HEAD]
        return jnp.concatenate([gk, gv], axis=1)
    if name == 'w_o_attn':
        full = g.reshape(D, N_HEADS, HEAD_PAD)[:, :, :V_HEAD].reshape(D, N_HEADS * V_HEAD)
        return full.reshape(N_DEV, D // N_DEV, N_HEADS * V_HEAD)
    return g.reshape(N_DEV, -1, cols)


def local_step(x, ctx, target, mod_l, mod_c, wt, on_grad=None, arrive=None):
    t_lat, n_ctx = x.shape[0], ctx.shape[0]
    n = t_lat + n_ctx
    tm = _pick(math.gcd(t_lat, n), (256, 128))
    tq_fwd = _pick(t_lat, (256, 128))
    tq_bwd = _pick(t_lat, (1024, 512, 256, 128))
    row = lambda v: v.reshape(1, -1).astype(F32)
    two = lambda a, b: jnp.stack([a, b]).astype(F32)
    sh1_l, sc1_l, g1_l, sh2_l, sc2_l, g2_l = jnp.split(mod_l, 6)
    sh1_c, sc1_c = jnp.split(mod_c, 6)[:2]
    sc1, sh1 = two(sc1_l, sc1_c), two(sh1_l, sh1_c)
    g1, g2, sc2, sh2 = row(g1_l), row(g2_l), row(sc2_l), row(sh2_l)
    norm1_g, norm2_g, final_g = row(wt['norm1_g']), row(wt['norm2_g']), row(wt['final_g'])
    q_g, kv_g, b_gate = row(wt['q_norm_g']), row(wt['kv_norm_g']), row(wt['b_gate'])
    wt = dict(wt)
    pending = []

    def sent():
        tokens = list(pending)
        pending.clear()
        return tokens

    def need(names, after):
        if arrive is not None:
            got = arrive(names, after)
            if '_token' in got:
                pending.append(got.pop('_token'))
            wt.update(got)
        return [wt[n] for n in names]
    lru_w_a = wt['lru_w_a'].reshape(2 * LRU_BLOCKS, LRU_BW, LRU_BW).astype(BF16)
    lru_w_x = wt['lru_w_x'].reshape(2 * LRU_BLOCKS, LRU_BW, LRU_BW).astype(BF16)
    b_a, b_x, lam = wt['lru_b_a'], wt['lru_b_x'], wt['lru_lambda']
    sp = jnp.logaddexp(-lam, 0.0)
    c_tab, s1_tab, s2_tab = _rope_tables(t_lat, n)
    rw = functools.partial(rowwise, n_rows=n, t_lat=t_lat, tm=tm)
    rw_lat = functools.partial(rowwise, n_rows=t_lat, t_lat=t_lat, tm=_pick(t_lat, (512, 256, 128)))

    stream = [(x, 0, D), (ctx, 0, D, 'ctx')]

    def f_norm1(is_ctx, rows, params):
        (xl, xc_), (g, sc, sh) = rows, params
        return [_norm_mod(jnp.where(is_ctx, xc_, xl), g, _sel(is_ctx, sc), _sel(is_ctx, sh))], []

    (h,), _ = rw("norm1", f_norm1, stream, [norm1_g, sc1, sh1], [(D, BF16)], [])
    (w_in_t,) = need(('w_in',), h)
    z = matmul("w_in", h, w_in_t, 'nt', BF16, after=sent())
    w_uq_t, w_ukv_t, w_o_lru = need(('w_uq', 'w_ukv', 'w_o_lru'), z)

    def f_qkv_norm(is_ctx, rows, params):
        (ql, kvl), (gq, gkv) = rows, params
        return [_rms(ql, gq), _rms(kvl, gkv)], []

    (qn, kvn), _ = rw("qkv_norm", f_qkv_norm, [(z, Z_Q, Q_RANK), (z, Z_KV, KV_RANK)], [q_g, kv_g],
                      [(Q_RANK, BF16), (KV_RANK, BF16)], [])
    qp = matmul("w_uq", qn, w_uq_t, 'nt', BF16)
    kvp = matmul("w_ukv", kvn, w_ukv_t, 'nt', BF16)

    def f_rope(is_ctx, rows, params):
        qv, kk, vv, kr, c, s1, s2 = rows
        krr = _rope(kr, c, s1, s2)
        qo = jnp.concatenate([_rope(qh, c, s1, s2) for qh in _heads(qv)], axis=1) * Q_SCALE
        ko = jnp.concatenate([kh + krr for kh in _heads(kk)], axis=1)
        return [qo, ko, vv], []

    hp = N_HEADS * HEAD_PAD
    (qr, kr_, vr), _ = rw("rope", f_rope,
                          [(qp, 0, hp), (kvp, 0, hp), (kvp, hp, hp), (z, Z_KR, HEAD_PAD), (c_tab, 0, HEAD_PAD),
                           (s1_tab, 0, HEAD_PAD), (s2_tab, 0, HEAD_PAD)], [], [(hp, BF16)] * 3, [])
    attn, lse = attn_fwd(qr, kr_, vr, t_lat, n, tq_fwd)

    xc = conv_fwd("lru_conv", z, Z_XB, LRU_W, wt['lru_conv_w'], row(wt['lru_conv_b']), 2, n, t_lat, F32)
    a_f, h_f, hp_f = lru_scan("lru_scan_f", xc, lru_w_a, lru_w_x, b_a, b_x, sp, 'f', n, t_lat)
    a_b, h_b, hp_b = lru_scan("lru_scan_b", xc, lru_w_a, lru_w_x, b_a, b_x, sp, 'b', n, t_lat)

    def f_lru_out(is_ctx, rows, params):
        hf, hb, yb = rows
        return [(hf + hb) * _gelu(yb)], []

    (ybin,), _ = rw_lat("lru_out", f_lru_out, [(h_f, 0, LRU_W), (h_b, 0, LRU_W), (z, Z_YB, LRU_W)], [],
                        [(LRU_W, BF16)], [])
    w_o_attn_t, w_out, w_up_t, w_down = need(('w_o_attn', 'w_out', 'w_up', 'w_down'), attn)
    y_a = matmul("w_o_attn", attn, w_o_attn_t, 'nt', BF16)
    y_b = matmul("w_o_lru", ybin, w_o_lru, 'nn', BF16)

    def _merge(ya, yb, gl, bg):
        gates = _sigmoid(gl + bg)
        return gates[:, :D] * ya + gates[:, D:] * yb

    def f_merge(is_ctx, rows, params):
        (ya, yb, gl), (bg,) = rows, params
        return [_merge(ya, yb, gl, bg)], []

    (mrg,), _ = rw_lat("merge", f_merge, [(y_a, 0, D), (y_b, 0, D), (z, Z_GL, 2 * D)], [b_gate], [(D, BF16)], [])
    o = matmul("w_out", mrg, w_out, 'nn', BF16)

    def _res_norm2(xv, ov, g1v, g, sc, sh):
        x1 = xv + g1v * ov
        return x1, _norm_mod(x1, g, sc, sh)

    def f_norm2(is_ctx, rows, params):
        (xv, ov), (g1v, g, sc, sh) = rows, params
        x1, h2v = _res_norm2(xv, ov, g1v, g, sc, sh)
        return [x1, h2v], []

    (x1, h2), _ = rw_lat("norm2", f_norm2, [(x, 0, D), (o, 0, D)], [g1, norm2_g, sc2, sh2], [(D, F32), (D, BF16)], [])
    u = matmul("w_up", h2, w_up_t, 'nt', BF16)
    f = ffn_mix_fwd(u, wt['ffn_conv_w'], row(wt['ffn_conv_b']), t_lat)
    dn = matmul("w_down", f, w_down, 'nn', BF16)

    def _tile_loss(x1v, dv, g2v, fg, tgt):
        y = _rms(x1v + g2v * dv, fg)
        e = y - tgt
        return 0.5 * jnp.sum(jnp.mean(e * e, axis=-1, keepdims=True), axis=0, keepdims=True)

    def f_final(is_ctx, rows, params):
        (x1v, dv, tgt), (g2v, fg) = rows, params
        lv, vjp = jax.vjp(lambda a, b, c, d: _tile_loss(a, b, c, d, tgt), x1v, dv, g2v, fg)
        dx2, dd, dg2, dfg = vjp(jnp.ones((1, 1), F32))
        return [dx2, dd], [dg2, dfg, jnp.broadcast_to(lv, (1, 128))]

    (dx2, dd), (dg2, dfinal_g, loss_v) = rw_lat("final", f_final, [(x1, 0, D), (dn, 0, D), (target, 0, D)],
                                                [g2, final_g], [(D, F32), (D, BF16)], [(1, D), (1, D), (1, 128)])
    loss = loss_v[0, 0]

    grads = {'final_g': dfinal_g}

    def put(name, g):
        grads[name] = g
        if on_grad is not None:
            pending.append(on_grad(name, g))
    df = matmul("d_f", dd, w_down, 'nt', BF16)
    put('w_down', matmul("g_w_down", f, dd, 'tn', BF16))

    du, grads['ffn_conv_w'], grads['ffn_conv_b'] = ffn_mix_bwd(u, df, wt['ffn_conv_w'], row(wt['ffn_conv_b']),
                                                               t_lat)
    dh2 = matmul("d_h2", du, w_up_t, 'nn', BF16, after=sent())
    put('w_up', matmul("g_w_up", du, h2, 'tn', BF16))

    def b_norm2(is_ctx, rows, params):
        (xv, ov, dh2v, dx2v), (g1v, g, sc, sh) = rows, params
        _, vjp = jax.vjp(_res_norm2, xv, ov, g1v, g, sc, sh)
        dx, do, dg1v, dg, dsc, dsh = vjp((dx2v, dh2v))
        return [dx, do], [dg1v, dg, dsc, dsh]

    (dx_res, do), (dg1, dnorm2_g, dsc2, dsh2) = rw_lat(
        "norm2_bwd", b_norm2, [(x, 0, D), (o, 0, D), (dh2, 0, D), (dx2, 0, D)], [g1, norm2_g, sc2, sh2],
        [(D, F32), (D, BF16)], [(1, D)] * 4)
    grads['norm2_g'] = dnorm2_g
    dmrg = matmul("d_merge", do, w_out, 'nt', BF16, after=sent())
    put('w_out', matmul("g_w_out", mrg, do, 'tn', BF16))

    def b_merge(is_ctx, rows, params):
        (ya, yb, gl, dm), (bg,) = rows, params
        _, vjp = jax.vjp(_merge, ya, yb, gl, bg)
        dya, dyb, dgl, dbg = vjp(dm)
        return [dya, dyb, dgl], [dbg]

    (dy_a, dy_b, dgl), (grads['b_gate'],) = rw_lat(
        "merge_bwd", b_merge, [(y_a, 0, D), (y_b, 0, D), (z, Z_GL, 2 * D), (dmrg, 0, D)], [b_gate],
        [(D, BF16), (D, BF16), (2 * D, BF16)], [(1, 2 * D)])
    dattn = matmul("d_attn", dy_a, w_o_attn_t, 'nn', BF16, after=sent())
    put('w_o_attn', matmul("g_w_o_attn", dy_a, attn, 'tn', BF16))
    dybin = matmul("d_lru_out", dy_b, w_o_lru, 'nt', BF16, after=sent())
    put('w_o_lru', matmul("g_w_o_lru", ybin, dy_b, 'tn', BF16))

    def b_lru_out(is_ctx, rows, params):
        hf, hb, yb, dyv = rows
        _, vjp = jax.vjp(lambda s, y: s * _gelu(y), hf + hb, yb)
        dh, dyb = vjp(dyv)
        return [dh, dyb], []

    (dh_lru, dyb), _ = rw_lat("lru_out_bwd", b_lru_out,
                              [(h_f, 0, LRU_W), (h_b, 0, LRU_W), (z, Z_YB, LRU_W), (dybin, 0, LRU_W)], [],
                              [(LRU_W, F32), (LRU_W, BF16)], [])
    gate_params = (lru_w_a, lru_w_x, b_a, b_x, sp)
    dxc_f, *sums_f = lru_scan_bwd("lru_scan_f_bwd", xc, a_f, dh_lru, hp_f, None, *gate_params, 'f', n, t_lat)
    dxc, *sums_b = lru_scan_bwd("lru_scan_b_bwd", xc, a_b, dh_lru, hp_b, dxc_f, *gate_params, 'b', n, t_lat)
    dw_a, dw_x, db_a, db_x, dsp = (jnp.concatenate([f_, b_], axis=0) for f_, b_ in zip(sums_f, sums_b))
    put('lru_w_a', dw_a.reshape(2 * LRU_BLOCKS * LRU_BW, LRU_BW).astype(BF16))
    put('lru_w_x', dw_x.reshape(2 * LRU_BLOCKS * LRU_BW, LRU_BW).astype(BF16))
    grads['lru_b_a'], grads['lru_b_x'] = db_a, db_x
    grads['lru_lambda'] = -dsp * _sigmoid(-lam)
    dxb, grads['lru_conv_w'], grads['lru_conv_b'] = conv_bwd("lru_conv_bwd", dxc, z, Z_XB, LRU_W, wt['lru_conv_w'],
                                                             2, n, t_lat)

    dq, dk, dv = attn_bwd(qr, kr_, vr, attn, dattn, lse, t_lat, n, tq_bwd)

    def b_rope(is_ctx, rows, params):
        dqv, dkv, dvv, c, s1, s2 = rows
        live = jnp.where(is_ctx, 0.0, 1.0)
        dqo = jnp.concatenate([_rope_t(dqh, c, s1, s2) for dqh in _heads(dqv)], axis=1) * live
        dkh = _heads(dkv)
        dkr = dkh[0]
        for t in dkh[1:]:
            dkr = dkr + t
        lanes = lax.broadcasted_iota(jnp.int32, dkr.shape, 1)
        dkr = jnp.where((lanes >= QK_NOPE) & (lanes < QK_DIM), _rope_t(dkr, c, s1, s2), 0.0)
        return [dqo, jnp.concatenate([dkv, dvv], axis=1), dkr], []

    (dqp, dkvp, dkr), _ = rw("rope_bwd", b_rope,
                             [(dq, 0, hp), (dk, 0, hp), (dv, 0, hp), (c_tab, 0, HEAD_PAD), (s1_tab, 0, HEAD_PAD),
                              (s2_tab, 0, HEAD_PAD)], [], [(hp, BF16), (2 * hp, BF16), (HEAD_PAD, BF16)], [])
    dqn = matmul("d_qn", dqp, w_uq_t, 'nn', BF16, after=sent())
    put('w_uq', matmul("g_w_uq", dqp, qn, 'tn', BF16))
    dkvn = matmul("d_kvn", dkvp, w_ukv_t, 'nn', BF16, after=sent())
    put('w_ukv', matmul("g_w_ukv", dkvp, kvn, 'tn', BF16))

    def b_qkv_norm(is_ctx, rows, params):
        (ql, kvl, dqv, dkvv), (gq, gkv) = rows, params
        _, vjp_q = jax.vjp(_rms, ql, gq)
        _, vjp_kv = jax.vjp(_rms, kvl, gkv)
        dql, dgq = vjp_q(dqv)
        dkvl, dgkv = vjp_kv(dkvv)
        return [dql, dkvl], [dgq, dgkv]

    (dq_lat, dkv_lat), (grads['q_norm_g'], grads['kv_norm_g']) = rw(
        "qkv_norm_bwd", b_qkv_norm, [(z, Z_Q, Q_RANK), (z, Z_KV, KV_RANK), (dqn, 0, Q_RANK), (dkvn, 0, KV_RANK)],
        [q_g, kv_g], [(Q_RANK, BF16), (KV_RANK, BF16)], [(1, Q_RANK), (1, KV_RANK)])
    pad_ctx = lambda t: jnp.pad(t, ((0, n_ctx), (0, 0)))
    dz = jnp.concatenate([dq_lat, dkv_lat, dkr, dxb, pad_ctx(dyb), pad_ctx(dgl)], axis=1)
    put('w_in', matmul("g_w_in", dz, h, 'tn', BF16))
    dh = matmul("d_h", dz, w_in_t, 'nn', BF16, after=sent())

    def b_norm1(is_ctx, rows, params):
        (xl, xc_, dhv, dxr), (g, sc, sh) = rows, params
        scv, shv = _sel(is_ctx, sc), _sel(is_ctx, sh)
        _, vjp = jax.vjp(_norm_mod, jnp.where(is_ctx, xc_, xl), g, scv, shv)
        dx, dg, dsc, dsh = vjp(dhv)
        return [dx + dxr], [dg, _seg_acc(is_ctx, dsc), _seg_acc(is_ctx, dsh)]

    (grad_x,), (grads['norm1_g'], dsc1, dsh1) = rw("norm1_bwd", b_norm1, stream + [(dh, 0, D), (dx_res, 0, D)],
                                                   [norm1_g, sc1, sh1], [(D, F32, 'lat')],
                                                   [(1, D), (2, D), (2, D)])
    zero = jnp.zeros((D,), F32)
    dmod_l = jnp.concatenate([dsh1[0], dsc1[0], dg1[0], dsh2[0], dsc2[0], dg2[0]])
    dmod_c = jnp.concatenate([dsh1[1], dsc1[1], zero, zero, zero, zero])
    return loss, grad_x, grads, dmod_l, dmod_c


def kernel(x, c, ctx, c_ctx, w_mod, b_mod, norm1_g, w_in, b_gate, q_norm_g, kv_norm_g, w_uq, w_ukv, w_o_attn, lru_conv_w, lru_conv_b, lru_w_a, lru_b_a, lru_w_x, lru_b_x, lru_lambda, w_o_lru, w_out, norm2_g, w_up, ffn_conv_w, ffn_conv_b, w_down, final_g, loss_target, m_c_ctx, m_w_mod, m_b_mod, m_norm1_g, m_w_in, m_b_gate, m_q_norm_g, m_kv_norm_g, m_w_uq, m_w_ukv, m_w_o_attn, m_lru_conv_w, m_lru_conv_b, m_lru_w_a, m_lru_b_a, m_lru_w_x, m_lru_b_x, m_lru_lambda, m_w_o_lru, m_w_out, m_norm2_g, m_w_up, m_ffn_conv_w, m_ffn_conv_b, m_w_down, m_final_g, v_c_ctx, v_w_mod, v_b_mod, v_norm1_g, v_w_in, v_b_gate, v_q_norm_g, v_kv_norm_g, v_w_uq, v_w_ukv, v_w_o_attn, v_lru_conv_w, v_lru_conv_b, v_lru_w_a, v_lru_b_a, v_lru_w_x, v_lru_b_x, v_lru_lambda, v_w_o_lru, v_w_out, v_norm2_g, v_w_up, v_ffn_conv_w, v_ffn_conv_b, v_w_down, v_final_g):
    given = dict(locals())
    strip = lambda name, a: a if name in ('c_ctx', 'final_g') else a[0]
    wsh = {n: strip(n, given[n]) for n in WEIGHTS}
    msh = {n: strip(n, given['m_' + n]) for n in WEIGHTS}
    vsh = {n: strip(n, given['v_' + n]) for n in WEIGHTS}
    me = _my_index()

    small = _flat([c[0]] + [wsh[n] for n in SMALL_F32], F32, 8)
    small_all = all_gather("gather_small", small).reshape(N_DEV, -1)
    c_all = small_all[:, :D]
    full, at = {}, D
    for n in SMALL_F32:
        cnt = math.prod(wsh[n].shape)
        full[n] = _gathered_to_full(n, small_all[:, at:at + cnt].reshape((N_DEV,) + wsh[n].shape))
        at += cnt

    cond = jnp.concatenate([c_all, c_ctx[None], jnp.zeros((7, D), F32)], axis=0)
    sil = cond * jax.nn.sigmoid(cond)
    mod_cols = matmul("mod_proj", sil, wsh['w_mod'], 'nn', F32)
    mod_all = all_gather("gather_mod", mod_cols)
    mod_all = jnp.transpose(mod_all, (1, 0, 2)).reshape(16, 6 * D) + b_mod[0][None]
    mod_l = lax.dynamic_index_in_dim(mod_all, me, axis=0, keepdims=False)
    mod_c = mod_all[N_DEV]

    rb_shards = {n: _shard_to_rb(n, wsh[n]).astype(BF16) for n in BIG_BF16}
    (w_in_blocks,) = all_gather_multi("gather_w_in", [rb_shards['w_in']])
    later = [n for n in BIG_BF16 if n != 'w_in']
    weights_started, weights_sent = exchange_start("weights_send", 'gather', [rb_shards[n] for n in later],
                                                   after=[w_in_blocks, mod_all])
    for n in REPLICATED:
        if n not in ('c_ctx', 'b_mod'):
            full[n] = wsh[n]

    def arrive(names, after):
        if names == ('w_in',):
            return {'w_in': _rb_from_gathered('w_in', w_in_blocks), '_token': weights_sent}
        picked = [later.index(n) for n in names]
        lands = exchange_wait("weights_wait_" + names[0], 'gather',
                              tuple([part[i] for i in picked] for part in weights_started), after)
        return {n: _rb_from_gathered(n, lax.dynamic_update_slice_in_dim(land, rb_shards[n][None], me, axis=0))
                for n, land in zip(names, lands)}

    in_flight = {}

    def on_grad(n, g):
        chunks = _chunks_from_rb_grad(n, g)
        own = lax.dynamic_index_in_dim(chunks, me, axis=0, keepdims=True)
        started, token = exchange_start("grad_send_" + n, 'scatter', [chunks])
        in_flight[n] = (own, started)
        return token

    loss, grad_x, grads, dmod_l, dmod_c = local_step(x[0], ctx[0], loss_target[0], mod_l, mod_c, full, on_grad,
                                                     arrive)
    dmod = jnp.stack([dmod_l, dmod_c]).reshape(2 * 6 * D // FLAT_C, FLAT_C)
    dm = all_gather("gather_dmod", dmod).reshape(N_DEV, 2, 6 * D)
    dmod_c_tot = dm[0, 1]
    for p in range(1, N_DEV):
        dmod_c_tot = dmod_c_tot + dm[p, 1]
    dm16 = jnp.concatenate([dm[:, 0], dmod_c_tot[None], jnp.zeros((7, 6 * D), F32)], axis=0)
    ncol = 6 * D // N_DEV
    dm16_cols = lax.dynamic_slice_in_dim(dm16.reshape(16, N_DEV, ncol), me, 1, axis=1)[:, 0]
    grad_w_mod = matmul("g_w_mod", sil, dm16_cols, 'tn', F32)
    dsil = matmul("d_cond", dm16_cols, wsh['w_mod'], 'nt', F32)
    sg = jax.nn.sigmoid(c_ctx)
    grads['c_ctx'] = dsil[N_DEV] * (sg * (1.0 + c_ctx * (1.0 - sg)))
    grads['b_mod'] = dmod_l + dmod_c

    g_final = {'w_mod': grad_w_mod}
    reduced, stepped = {}, {}
    for n in BIG_BF16 + ['lru_w_a', 'lru_w_x']:
        own, started = in_flight[n]
        (land,) = exchange_wait("grad_wait_" + n, 'scatter', started, dm)
        if n in ROW_SHARDED:
            g_final[n], *stepped[n] = reduce_slots("step_" + n, land, own, (wsh[n], msh[n], vsh[n]))
        elif n in COL_SHARDED and wsh[n].shape[1] % 128:
            g_t, *outs = reduce_slots("step_" + n, land, own, (wsh[n].T, msh[n].T, vsh[n].T))
            g_final[n], stepped[n] = g_t.T, [o.T for o in outs]
        else:
            reduced[n] = reduce_slots("sum_" + n, land, own)
            if n in BIG_BF16:
                g_final[n] = _rb_to_shard(n, reduced[n])

    small_names = SMALL_F32 + [n for n in REPLICATED if n not in ('lru_w_a', 'lru_w_x')]
    partials = _flat([grads[n] for n in small_names] + [loss], F32, 8)
    parts_all, a_all, x_all = all_gather_multi("gather_small_grads", [partials, reduced['lru_w_a'], reduced['lru_w_x']])
    small_sum = sum_slots("sum_small", parts_all).reshape(-1)
    g_final['lru_w_a'], g_final['lru_w_x'] = a_all.reshape(wsh['lru_w_a'].shape), x_all.reshape(wsh['lru_w_x'].shape)
    at = 0
    for n in small_names:
        cnt = math.prod(full[n].shape) if n in SMALL_F32 else math.prod(wsh[n].shape)
        g = small_sum[at:at + cnt]
        if n in SMALL_F32:
            k = full[n].shape[0]
            g = lax.dynamic_index_in_dim(g.reshape(k, N_DEV, -1), me, axis=1, keepdims=False)
        g_final[n] = g.reshape(wsh[n].shape)
        at += cnt
    loss = small_sum[at]

    for n in ['w_mod'] + BIG_BF16:
        if n not in stepped:
            stepped[n] = adamw("adamw_" + n, wsh[n], g_final[n], msh[n], vsh[n])
    rest = [n for n in WEIGHTS if n not in stepped]
    as2d = lambda a: a.reshape(-1, a.shape[-1])
    rest_out = adamw_many("adamw_small", *[[as2d(d[n]) for n in rest] for d in (wsh, g_final, msh, vsh)])
    stepped.update(zip(rest, rest_out))
    shaped = lambda n, a: a.reshape(given[n].shape)
    return (loss, grad_x[None],
            *[shaped(n, g_final[n]) for n in WEIGHTS],
            *[shaped(n, stepped[n][k]) for k in range(3) for n in WEIGHTS])
```
